```python
import math
import jax, jax.numpy as jnp
from jax import lax
import numpy as np

D_MODEL = 1024
BATCH = 8
SEQ = 2048
DEPTH = 1
DEC_BATCH = 128
DEC_SEQ = 8
PAST_LEN = 16384
PAGE_SIZE = 128

H_K = 8
D_K = 128
H_V = 16
D_V = 128
Q_W = H_K * D_K
V_W = H_V * D_V
QKV_W = 2 * Q_W + V_W
CONV_W = 4
CHUNK = 64
POOL_WINDOWS = (2, 4, 8, 16)
N_POOL_GROUPS = 4
D_POOL = D_MODEL
POOL_GROUP_W = D_POOL // N_POOL_GROUPS
POOL_BUF = max(POOL_WINDOWS) - 1
N_BRANCH = 2
IN_W = QKV_W + V_W + 2 * H_V + D_POOL + N_BRANCH * D_MODEL
N_EXPERTS = 256
TOP_K = 8
N_GROUP = 8
TOPK_GROUP = 4
D_EXPERT = D_MODEL // 4
D_SHARED = D_EXPERT
ROUTED_SCALE = 2.5
MOE_BLOCK = 64
ALPHA = (2 * DEPTH) ** 0.25
DN_BETA = (8 * DEPTH) ** -0.25
LN_EPS = 1e-5
NORM_EPS = 1e-6

kernel_name = "hybrid_gdn_pool_moe_deepnorm_step"


def layer_norm(x, g, b):
    xf = x.astype(jnp.float32)
    mu = jnp.mean(xf, axis=-1, keepdims=True)
    var = jnp.mean(jnp.square(xf - mu), axis=-1, keepdims=True)
    return ((xf - mu) * lax.rsqrt(var + LN_EPS) * g.astype(jnp.float32) + b.astype(jnp.float32)).astype(x.dtype)


def l2_normalize(x):
    xf = x.astype(jnp.float32)
    return xf * lax.rsqrt(jnp.sum(xf * xf, axis=-1, keepdims=True) + NORM_EPS)


def short_conv(x_raw, buf, conv_w):
    L = x_raw.shape[1]
    ext = jnp.concatenate([buf.astype(x_raw.dtype), x_raw], axis=1)
    y = sum(ext[:, j:j + L] * conv_w[j] for j in range(CONV_W))
    return jax.nn.silu(y), ext[:, -(CONV_W - 1):]


def gated_delta_chunked(q, k, v, beta, g, S0):
    B, L, H, _ = q.shape
    C = min(CHUNK, L)
    n = -(-L // C)
    pad = n * C - L

    def to_chunks(t):
        t = jnp.pad(t.astype(jnp.float32), [(0, 0), (0, pad)] + [(0, 0)] * (t.ndim - 2))
        t = jnp.moveaxis(t, 1, 2)
        return t.reshape((B, H, n, C) + t.shape[3:])

    qc, kc, vc, bc, gc = (to_chunks(t) for t in (q, k, v, beta, g))
    G = jnp.cumsum(gc, axis=-1)
    idx = jnp.arange(C)
    incl = idx[:, None] >= idx[None, :]
    strict = idx[:, None] > idx[None, :]
    decay = jnp.exp(jnp.where(incl, G[..., :, None] - G[..., None, :], -jnp.inf))
    kb = kc * bc[..., None]
    vb = vc * bc[..., None]
    lmat = jnp.where(strict, jnp.einsum('bhncd,bhnsd->bhncs', kb, kc) * decay, 0.0)
    a_mat = lmat + jnp.eye(C, dtype=jnp.float32)
    rhs = jnp.concatenate([vb, kb * jnp.exp(G)[..., None]], axis=-1)
    sol = lax.linalg.triangular_solve(a_mat, rhs, left_side=True, lower=True, unit_diagonal=True)
    u, w = sol[..., :D_V], sol[..., D_V:]
    attn = jnp.einsum('bhncd,bhnsd->bhncs', qc, kc) * decay

    def step(S, inp):
        q_c, k_c, u_c, w_c, G_c, a_c = inp
        v_new = u_c - jnp.einsum('bhcd,bhde->bhce', w_c, S)
        o = jnp.einsum('bhcd,bhde->bhce', q_c * jnp.exp(G_c)[..., None], S) + jnp.einsum('bhcs,bhse->bhce', a_c, v_new)
        G_last = G_c[..., -1]
        S = S * jnp.exp(G_last)[..., None, None] + jnp.einsum(
            'bhcd,bhce->bhde', k_c * jnp.exp(G_last[..., None] - G_c)[..., None], v_new)
        return S, o

    xs = tuple(jnp.moveaxis(t, 2, 0) for t in (qc, kc, u, w, G, attn))
    S_fin, o = lax.scan(step, S0.astype(jnp.float32), xs)
    o = jnp.moveaxis(o, 0, 2).reshape(B, H, n * C, D_V)[:, :, :L]
    return jnp.transpose(o, (0, 2, 1, 3)), S_fin


def pool_mix(u, buf, start_pos, w_pool, pool_scale):
    B, L, _ = u.shape
    ext = jnp.concatenate([buf.astype(u.dtype), u], axis=1)
    cs = jnp.cumsum(ext.astype(jnp.float32), axis=1)
    cs = jnp.concatenate([jnp.zeros((B, 1, D_POOL), jnp.float32), cs], axis=1)
    upper = cs[:, POOL_BUF + 1:]
    pos = start_pos + jnp.arange(L)
    uf = u.astype(jnp.float32)
    outs = []
    for gi, win in enumerate(POOL_WINDOWS):
        sl = slice(gi * POOL_GROUP_W, (gi + 1) * POOL_GROUP_W)
        lower = cs[:, POOL_BUF + 1 - win:POOL_BUF + 1 - win + L, sl]
        cnt = jnp.minimum(win, pos + 1).astype(jnp.float32)[None, :, None]
        outs.append((upper[..., sl] - lower) / cnt - uf[..., sl])
    pooled = jnp.stack(outs, axis=2)
    mixed = jnp.einsum('blgc,gcd->blgd', pooled, w_pool.astype(jnp.float32)).reshape(B, L, D_POOL)
    mixed = mixed * pool_scale.astype(jnp.float32)
    return mixed.astype(u.dtype), ext[:, -POOL_BUF:]


def swiglu(x, w_gate, w_up, w_down):
    return (jax.nn.silu(x @ w_gate) * (x @ w_up)) @ w_down


def moe_ffn(h, w_router, router_bias, w_exp_gate, w_exp_up, w_exp_down, w_sh_gate, w_sh_up, w_sh_down):
    T = h.shape[0]
    scores = jax.nn.sigmoid((h @ w_router).astype(jnp.float32))
    choice = scores + router_bias.astype(jnp.float32)
    grp = choice.reshape(T, N_GROUP, N_EXPERTS // N_GROUP)
    gscore = jnp.sum(lax.top_k(grp, 2)[0], axis=-1)
    gidx = lax.top_k(gscore, TOPK_GROUP)[1]
    gmask = jnp.sum(jax.nn.one_hot(gidx, N_GROUP, dtype=jnp.float32), axis=-2) > 0
    emask = jnp.repeat(gmask, N_EXPERTS // N_GROUP, axis=-1)
    idx = lax.top_k(jnp.where(emask, choice, -jnp.inf), TOP_K)[1]
    wts = jnp.take_along_axis(scores, idx, axis=-1)
    wts = wts / jnp.sum(wts, axis=-1, keepdims=True) * ROUTED_SCALE

    TK = T * TOP_K
    e_flat = idx.reshape(-1)
    tok_flat = jnp.arange(TK, dtype=jnp.int32) // TOP_K
    g_flat = wts.reshape(-1)
    order = jnp.argsort(e_flat)
    e_s, tok_s, g_s = e_flat[order], tok_flat[order], g_flat[order]
    counts = jax.ops.segment_sum(jnp.ones((TK,), jnp.int32), e_flat, num_segments=N_EXPERTS)
    starts = jnp.cumsum(counts) - counts
    padded = ((counts + MOE_BLOCK - 1) // MOE_BLOCK) * MOE_BLOCK
    pends = jnp.cumsum(padded)
    pstarts = pends - padded
    dest = pstarts[e_s] + (jnp.arange(TK, dtype=jnp.int32) - starts[e_s])
    n_blocks = (TK + N_EXPERTS * (MOE_BLOCK - 1) + MOE_BLOCK - 1) // MOE_BLOCK
    n_rows = n_blocks * MOE_BLOCK
    row_tok = jnp.full((n_rows,), T, jnp.int32).at[dest].set(tok_s)
    row_gate = jnp.zeros((n_rows,), jnp.float32).at[dest].set(g_s)
    block_start = jnp.arange(n_blocks, dtype=jnp.int32) * MOE_BLOCK
    block_exp = jnp.minimum(jnp.searchsorted(pends, block_start, side='right'), N_EXPERTS - 1)
    h_pad = jnp.concatenate([h, jnp.zeros((1, h.shape[1]), h.dtype)], axis=0)
    xb = h_pad[row_tok].reshape(n_blocks, MOE_BLOCK, h.shape[1])

    def expert_block(args):
        xblk, e = args
        return swiglu(xblk, w_exp_gate[e], w_exp_up[e], w_exp_down[e])

    yb = lax.map(expert_block, (xb, block_exp)).reshape(n_rows, h.shape[1])
    routed = jax.ops.segment_sum(yb.astype(jnp.float32) * row_gate[:, None], row_tok, num_segments=T + 1)[:T]
    shared = swiglu(h, w_sh_gate, w_sh_up, w_sh_down)
    return (routed + shared.astype(jnp.float32)).astype(h.dtype)


def trunk_layer(x, S0, conv_buf, pool_buf, start_pos, w_in, conv_w, a_log, dt_bias, gamma_a, w_br_a,
                w_pool, pool_scale, w_br_b, w_out, ln1_g, ln1_b, w_router, router_bias,
                w_exp_gate, w_exp_up, w_exp_down, w_sh_gate, w_sh_up, w_sh_down, ln2_g, ln2_b):
    B, L, _ = x.shape
    proj = x @ w_in
    sizes = [QKV_W, V_W, H_V, H_V, D_POOL, D_MODEL, D_MODEL]
    splits = np.cumsum(sizes)[:-1].tolist()
    qkv_raw, z, b_raw, a_raw, u, gate_a, gate_b = jnp.split(proj, splits, axis=-1)

    qkv, new_conv = short_conv(qkv_raw, conv_buf, conv_w)
    q, k, v = jnp.split(qkv, [Q_W, 2 * Q_W], axis=-1)
    rep = H_V // H_K
    q = jnp.repeat(l2_normalize(q.reshape(B, L, H_K, D_K)) * (D_K ** -0.5), rep, axis=2)
    k = jnp.repeat(l2_normalize(k.reshape(B, L, H_K, D_K)), rep, axis=2)
    v = v.reshape(B, L, H_V, D_V)
    beta = jax.nn.sigmoid(b_raw.astype(jnp.float32))
    g = -jnp.exp(a_log.astype(jnp.float32)) * jax.nn.softplus(a_raw.astype(jnp.float32) + dt_bias.astype(jnp.float32))
    o, S_new = gated_delta_chunked(q, k, v, beta, g, S0)
    o = o * lax.rsqrt(jnp.mean(o * o, axis=-1, keepdims=True) + NORM_EPS) * gamma_a.astype(jnp.float32)
    o = o * jax.nn.silu(z.reshape(B, L, H_V, D_V).astype(jnp.float32))
    branch_a = o.reshape(B, L, V_W).astype(x.dtype) @ w_br_a

    pooled, new_pool = pool_mix(u, pool_buf, start_pos, w_pool, pool_scale)
    branch_b = pooled @ w_br_b

    merged = jax.nn.sigmoid(gate_a) * branch_a + jax.nn.sigmoid(gate_b) * branch_b
    h = layer_norm(ALPHA * x + merged @ w_out, ln1_g, ln1_b)
    f = moe_ffn(h.reshape(B * L, D_MODEL), w_router, router_bias, w_exp_gate, w_exp_up, w_exp_down,
                w_sh_gate, w_sh_up, w_sh_down).reshape(B, L, D_MODEL)
    y = layer_norm(ALPHA * h + f, ln2_g, ln2_b)
    return y, S_new.astype(x.dtype), new_conv, new_pool


def setup_inputs(seed: int = 0) -> dict:
    key = jax.random.key(seed)
    ks = jax.random.split(key, 32)
    f32 = jnp.float32
    nrm = lambda k, shape, s: jax.random.normal(k, shape, f32) * s
    dt = jnp.exp(jax.random.uniform(ks[7], (DEPTH, H_V), f32, math.log(1e-3), math.log(1e-1)))
    return {
        "x_prompt": nrm(ks[0], (BATCH, SEQ, D_MODEL), 1.0),
        "x_sample": nrm(ks[1], (DEC_BATCH, DEC_SEQ, D_MODEL), 1.0),
        "state_delta": nrm(ks[2], (DEPTH, DEC_BATCH, H_V, D_K, D_V), D_K ** -0.5),
        "state_conv": nrm(ks[3], (DEPTH, DEC_BATCH, CONV_W - 1, QKV_W), 1.0),
        "state_pool": nrm(ks[4], (DEPTH, DEC_BATCH, POOL_BUF, D_POOL), 1.0),
        "w_in": nrm(ks[5], (DEPTH, D_MODEL, IN_W), D_MODEL ** -0.5),
        "conv_w": nrm(ks[6], (DEPTH, CONV_W, QKV_W), CONV_W ** -0.5),
        "a_log": jnp.log(jax.random.uniform(ks[8], (DEPTH, H_V), f32, 1.0, 16.0)),
        "dt_bias": dt + jnp.log(-jnp.expm1(-dt)),
        "gamma_a": 1.0 + nrm(ks[9], (DEPTH, D_V), 0.02),
        "w_br_a": nrm(ks[10], (DEPTH, V_W, D_MODEL), V_W ** -0.5 * DN_BETA),
        "w_pool": nrm(ks[11], (DEPTH, N_POOL_GROUPS, POOL_GROUP_W, POOL_GROUP_W), POOL_GROUP_W ** -0.5),
        "pool_scale": 1.0 + nrm(ks[12], (DEPTH, D_POOL), 0.02),
        "w_br_b": nrm(ks[13], (DEPTH, D_POOL, D_MODEL), D_POOL ** -0.5 * DN_BETA),
        "w_out": nrm(ks[14], (DEPTH, D_MODEL, D_MODEL), D_MODEL ** -0.5 * DN_BETA),
        "ln1_g": 1.0 + nrm(ks[15], (DEPTH, D_MODEL), 0.02),
        "ln1_b": nrm(ks[16], (DEPTH, D_MODEL), 0.02),
        "w_router": nrm(ks[17], (DEPTH, D_MODEL, N_EXPERTS), D_MODEL ** -0.5),
        "router_bias": nrm(ks[18], (DEPTH, N_EXPERTS), 0.01),
        "w_exp_gate": nrm(ks[19], (DEPTH, N_EXPERTS, D_MODEL, D_EXPERT), D_MODEL ** -0.5),
        "w_exp_up": nrm(ks[20], (DEPTH, N_EXPERTS, D_MODEL, D_EXPERT), D_MODEL ** -0.5),
        "w_exp_down": nrm(ks[21], (DEPTH, N_EXPERTS, D_EXPERT, D_MODEL), D_EXPERT ** -0.5 * DN_BETA),
        "w_sh_gate": nrm(ks[22], (DEPTH, D_MODEL, D_SHARED), D_MODEL ** -0.5),
        "w_sh_up": nrm(ks[23], (DEPTH, D_MODEL, D_SHARED), D_MODEL ** -0.5),
        "w_sh_down": nrm(ks[24], (DEPTH, D_SHARED, D_MODEL), D_SHARED ** -0.5 * DN_BETA),
        "ln2_g": 1.0 + nrm(ks[25], (DEPTH, D_MODEL), 0.02),
        "ln2_b": nrm(ks[26], (DEPTH, D_MODEL), 0.02),
    }


def reference(x_prompt, x_sample, state_delta, state_conv, state_pool, w_in, conv_w, a_log, dt_bias, gamma_a,
              w_br_a, w_pool, pool_scale, w_br_b, w_out, ln1_g, ln1_b, w_router, router_bias,
              w_exp_gate, w_exp_up, w_exp_down, w_sh_gate, w_sh_up, w_sh_down, ln2_g, ln2_b):
    yp, ys = x_prompt, x_sample
    Bp = x_prompt.shape[0]
    dt = x_prompt.dtype
    pd, pc, pp, sd, sc, sp = [], [], [], [], [], []
    for l in range(DEPTH):
        p = dict(w_in=w_in[l], conv_w=conv_w[l], a_log=a_log[l], dt_bias=dt_bias[l], gamma_a=gamma_a[l],
                 w_br_a=w_br_a[l], w_pool=w_pool[l], pool_scale=pool_scale[l], w_br_b=w_br_b[l], w_out=w_out[l],
                 ln1_g=ln1_g[l], ln1_b=ln1_b[l], w_router=w_router[l], router_bias=router_bias[l],
                 w_exp_gate=w_exp_gate[l], w_exp_up=w_exp_up[l], w_exp_down=w_exp_down[l],
                 w_sh_gate=w_sh_gate[l], w_sh_up=w_sh_up[l], w_sh_down=w_sh_down[l],
                 ln2_g=ln2_g[l], ln2_b=ln2_b[l])
        yp, d_p, c_p, q_p = trunk_layer(
            yp, jnp.zeros((Bp, H_V, D_K, D_V), dt), jnp.zeros((Bp, CONV_W - 1, QKV_W), dt),
            jnp.zeros((Bp, POOL_BUF, D_POOL), dt), 0, **p)
        ys, d_s, c_s, q_s = trunk_layer(ys, state_delta[l], state_conv[l], state_pool[l], PAST_LEN, **p)
        pd.append(d_p); pc.append(c_p); pp.append(q_p)
        sd.append(d_s); sc.append(c_s); sp.append(q_s)
    return (yp, ys, jnp.stack(pd), jnp.stack(pc), jnp.stack(pp), jnp.stack(sd), jnp.stack(sc), jnp.stack(sp))
```

```python
import functools
import math

import jax
import jax.numpy as jnp
from jax import lax
from jax.experimental import pallas as pl
from jax.experimental.pallas import tpu as pltpu

F32 = jnp.float32
BF16 = jnp.bfloat16

D_MODEL = 1024
H_K = 8
D_K = 128
H_V = 16
D_V = 128
Q_W = H_K * D_K
V_W = H_V * D_V
QKV_W = 2 * Q_W + V_W
CONV_W = 4
POOL_WINDOWS = (2, 4, 8, 16)
POOL_GROUP_W = D_MODEL // len(POOL_WINDOWS)
POOL_BUF = max(POOL_WINDOWS) - 1
N_EXPERTS = 256
TOP_K = 8
N_GROUP = 8
TOPK_GROUP = 4
GROUP_SZ = N_EXPERTS // N_GROUP
D_EXPERT = D_MODEL // 4
ROUTED_SCALE = 2.5
LN_EPS = 1e-5
NORM_EPS = 1e-6
PAST_LEN = 16384

MAIN_W = QKV_W + V_W + 3 * D_MODEL
Z_BLK = QKV_W // V_W
U_BLK = (QKV_W + V_W) // D_MODEL

STACK_ROWS = 128
EXPERT_BM = 256
VMEM_LIMIT = 56 * 1024 * 1024


def _cparams(sem):
    return pltpu.CompilerParams(dimension_semantics=sem, vmem_limit_bytes=VMEM_LIMIT)


def _sigmoid(x):
    return 1.0 / (1.0 + jnp.exp(-x))


def _silu(x):
    return x * _sigmoid(x)


def _softplus(x):
    return jnp.maximum(x, 0.0) + jnp.log(1.0 + jnp.exp(-jnp.abs(x)))


def _dot(a, b):
    return jnp.dot(a.astype(BF16), b.astype(BF16), preferred_element_type=F32)


def _dot_nt(a, b):
    return lax.dot_general(a.astype(BF16), b.astype(BF16), (((1,), (1,)), ((), ())),
                           preferred_element_type=F32)


def _dot_tn(a, b):
    return lax.dot_general(a.astype(BF16), b.astype(BF16), (((0,), (0,)), ((), ())),
                           preferred_element_type=F32)


def _split(a):
    hi = a.astype(BF16)
    lo = (a - hi.astype(F32)).astype(BF16)
    return hi, lo


def _dot3(a, b):
    ah, al = _split(a)
    bh, bl = _split(b)
    d = functools.partial(jnp.dot, preferred_element_type=F32)
    return d(ah, bh) + (d(ah, bl) + d(al, bh))


def _dot3_nt(a, b):
    ah, al = _split(a)
    bh, bl = _split(b)
    d = functools.partial(lax.dot_general, dimension_numbers=(((1,), (1,)), ((), ())),
                          preferred_element_type=F32)
    return d(ah, bh) + (d(ah, bl) + d(al, bh))


def _layer_norm(x, g, b):
    mu = jnp.mean(x, axis=-1, keepdims=True)
    xc = x - mu
    var = jnp.mean(xc * xc, axis=-1, keepdims=True)
    return xc * lax.rsqrt(var + LN_EPS) * g + b


def _mm_kernel(x_ref, w_ref, o_ref):
    o_ref[...] = jnp.dot(x_ref[...], w_ref[...], preferred_element_type=F32).astype(o_ref.dtype)


def _matmul(x, w, tm, tn, out_dtype):
    t, k = x.shape
    n = w.shape[1]
    return pl.pallas_call(
        _mm_kernel,
        grid=(t // tm, n // tn),
        in_specs=[pl.BlockSpec((tm, k), lambda i, j: (i, 0)),
                  pl.BlockSpec((k, tn), lambda i, j: (0, j))],
        out_specs=pl.BlockSpec((tm, tn), lambda i, j: (i, j)),
        out_shape=jax.ShapeDtypeStruct((t, n), out_dtype),
        compiler_params=_cparams(("parallel", "parallel")),
    )(x, w)


def _gdn_kernel(*refs, C, hg, has_state):
    if has_state:
        (qkv_ref, z_ref, ba_ref, cw_ref, ad_ref, gam_ref, buf_ref, s0_ref,
         o_ref, sout_ref, s_scr, tail_scr, act_scr) = refs
    else:
        (qkv_ref, z_ref, ba_ref, cw_ref, ad_ref, gam_ref,
         o_ref, sout_ref, s_scr, tail_scr, act_scr) = refs
    R = hg * C
    ng = H_V // hg
    log2c = int(math.log2(C))
    c = pl.program_id(1)

    @pl.when(c == 0)
    def _init():
        if has_state:
            s_scr[...] = s0_ref[0]
            tail_scr[...] = buf_ref[0]
        else:
            s_scr[...] = jnp.zeros(s_scr.shape, F32)
            tail_scr[...] = jnp.zeros(tail_scr.shape, F32)

    x = qkv_ref[...]
    xe = jnp.concatenate([tail_scr[...], x], axis=0)
    cw = cw_ref[...]
    y = xe[8:8 + C] * cw[3:4]
    for j in range(CONV_W - 1):
        y = y + xe[5 + j:5 + j + C] * cw[j:j + 1]
    tail_scr[...] = x[C - 8:C]
    act_scr[...] = _silu(y)

    ri = lax.broadcasted_iota(jnp.int32, (R, R), 0)
    ci = lax.broadcasted_iota(jnp.int32, (R, R), 1)
    same = (ri >> log2c) == (ci >> log2c)
    incl = same & (ri >= ci)
    strict = same & (ri > ci)
    eye = ri == ci
    is_last = ci == (((ri >> log2c) << log2c) + (C - 1))
    eye_f = jnp.where(eye, 1.0, 0.0).astype(F32)
    cum_u = jnp.where(same & (ri <= ci), 1.0, 0.0).astype(BF16)

    nrow = ba_ref.shape[2] // 2
    bav = ba_ref[0, 0]
    adv = ad_ref[...]
    beta_rows = _sigmoid(bav[0:nrow])
    g_rows = -jnp.exp(adv[0:nrow]) * _softplus(bav[nrow:2 * nrow] + adv[nrow:2 * nrow])
    g1 = g_rows.astype(BF16)
    r1 = g_rows - g1.astype(F32)
    g2 = r1.astype(BF16)
    g3 = (r1 - g2.astype(F32)).astype(BF16)
    dd = functools.partial(jnp.dot, preferred_element_type=F32)
    G_rows = dd(g1, cum_u) + (dd(g2, cum_u) + dd(g3, cum_u))

    gam = gam_ref[...]
    normed = {}

    def qk_head(kind, kh):
        key = (kind, kh)
        if key not in normed:
            off = (0 if kind == "q" else Q_W) + kh * D_K
            v = act_scr[:, off:off + D_K]
            v = v * lax.rsqrt(jnp.sum(v * v, axis=-1, keepdims=True) + NORM_EPS)
            if kind == "q":
                v = v * (D_K ** -0.5)
            normed[key] = v
        return normed[key]

    for g in range(ng):
        heads = [g * hg + hh for hh in range(hg)]
        rep = H_V // H_K
        qst = jnp.concatenate([qk_head("q", h // rep) for h in heads], axis=0)
        kst = jnp.concatenate([qk_head("k", h // rep) for h in heads], axis=0)
        vst = jnp.concatenate([act_scr[:, 2 * Q_W + h * D_V:2 * Q_W + (h + 1) * D_V] for h in heads], axis=0)
        zst = jnp.concatenate([z_ref[:, h * D_V:(h + 1) * D_V] for h in heads], axis=0)

        grow = G_rows[g:g + 1]
        brow = beta_rows[g:g + 1]
        gcol = jnp.sum(eye_f * grow, axis=1, keepdims=True)
        bcol = jnp.sum(eye_f * brow, axis=1, keepdims=True)
        glast = jnp.sum(jnp.where(is_last, grow, 0.0), axis=1, keepdims=True)
        dm = jnp.where(incl, jnp.exp(jnp.minimum(gcol - grow, 0.0)), 0.0)

        kk = _dot_nt(kst, kst)
        lm = jnp.where(strict, kk * bcol * dm, 0.0)
        tinv = eye_f - lm
        pw = lm
        for _ in range(log2c - 1):
            pw = _dot3(pw, pw)
            tinv = tinv + _dot3(tinv, pw)
        eg = jnp.exp(gcol)
        rhs = jnp.concatenate([vst * bcol, kst * (bcol * eg)], axis=1)
        sol = _dot3(tinv, rhs)
        u = sol[:, :D_V]
        w = sol[:, D_V:]
        attn = jnp.where(incl, _dot_nt(qst, kst) * dm, 0.0)
        qe = qst * eg
        kt = kst * jnp.exp(glast - gcol)
        egl = jnp.exp(glast)

        vnew_parts, qs_parts = [], []
        for hh, h in enumerate(heads):
            sl = slice(hh * C, (hh + 1) * C)
            xs = _dot(jnp.concatenate([w[sl], qe[sl]], axis=0), s_scr[h])
            vnew_parts.append(u[sl] - xs[:C])
            qs_parts.append(xs[C:])
        vnew = jnp.concatenate(vnew_parts, axis=0)
        o = jnp.concatenate(qs_parts, axis=0) + _dot(attn, vnew)

        row_head = lax.broadcasted_iota(jnp.int32, (R, D_V), 0) >> log2c
        for hh, h in enumerate(heads):
            sl = slice(hh * C, (hh + 1) * C)
            if C >= 16:
                upd = _dot_tn(kt[sl], vnew[sl])
            else:
                upd = _dot_tn(kt, jnp.where(row_head == hh, vnew, 0.0))
            s_scr[h] = s_scr[h] * egl[hh * C:hh * C + 1] + upd

        on = o * lax.rsqrt(jnp.mean(o * o, axis=-1, keepdims=True) + NORM_EPS) * gam * _silu(zst)
        for hh, h in enumerate(heads):
            o_ref[:, h * D_V:(h + 1) * D_V] = on[hh * C:(hh + 1) * C].astype(o_ref.dtype)

    @pl.when(c == pl.num_programs(1) - 1)
    def _fin():
        sout_ref[0] = s_scr[...]


def _gdn(proj, ba, conv_w, a_log, dt_bias, gamma_a, row0, B, L, C, buf8=None, s0=None):
    has_state = s0 is not None
    hg = STACK_ROWS // C
    ng = H_V // hg
    nrow = 16
    nc = L // C
    R = STACK_ROWS

    def arrange(v):
        v = v.reshape(B, nc, C, ng, hg).transpose(0, 1, 3, 4, 2).reshape(B, nc, ng, R)
        return jnp.pad(v, ((0, 0), (0, 0), (0, nrow - ng), (0, 0)))

    rows = ba[row0:row0 + B * L]
    ba_arr = jnp.concatenate([arrange(rows[:, :H_V]), arrange(rows[:, H_V:2 * H_V])], axis=2)

    def arrange_param(p):
        v = jnp.broadcast_to(p.reshape(ng, hg, 1), (ng, hg, C)).reshape(ng, R)
        return jnp.pad(v, ((0, nrow - ng), (0, 0)))

    ad = jnp.concatenate([arrange_param(a_log), arrange_param(dt_bias)], axis=0)
    rb0 = row0 // C
    in_specs = [
        pl.BlockSpec((C, QKV_W), lambda b, c: (rb0 + b * nc + c, 0)),
        pl.BlockSpec((C, V_W), lambda b, c: (rb0 + b * nc + c, Z_BLK)),
        pl.BlockSpec((1, 1, 2 * nrow, R), lambda b, c: (b, c, 0, 0)),
        pl.BlockSpec((CONV_W, QKV_W), lambda b, c: (0, 0)),
        pl.BlockSpec((2 * nrow, R), lambda b, c: (0, 0)),
        pl.BlockSpec((1, D_V), lambda b, c: (0, 0)),
    ]
    args = [proj, proj, ba_arr, conv_w, ad, gamma_a.reshape(1, D_V)]
    if has_state:
        in_specs += [pl.BlockSpec((1, 8, QKV_W), lambda b, c: (b, 0, 0)),
                     pl.BlockSpec((1, H_V, D_K, D_V), lambda b, c: (b, 0, 0, 0))]
        args += [buf8, s0]
    o_dtype = BF16 if C % 16 == 0 else F32
    return pl.pallas_call(
        functools.partial(_gdn_kernel, C=C, hg=hg, has_state=has_state),
        grid=(B, nc),
        in_specs=in_specs,
        out_specs=[pl.BlockSpec((C, V_W), lambda b, c: (b * nc + c, 0)),
                   pl.BlockSpec((1, H_V, D_K, D_V), lambda b, c: (b, 0, 0, 0))],
        out_shape=[jax.ShapeDtypeStruct((B * L, V_W), o_dtype),
                   jax.ShapeDtypeStruct((B, H_V, D_K, D_V), F32)],
        scratch_shapes=[pltpu.VMEM((H_V, D_K, D_V), F32),
                        pltpu.VMEM((8, QKV_W), F32),
                        pltpu.VMEM((C, QKV_W), F32)],
        compiler_params=_cparams(("parallel", "arbitrary")),
    )(*args)


def _pool_kernel(*refs, tc, start_pos, has_state):
    if has_state:
        u_ref, buf_ref, o_ref, tail_scr = refs
    else:
        u_ref, o_ref, tail_scr = refs
    hist = POOL_BUF + 1
    c = pl.program_id(1)

    @pl.when(c == 0)
    def _init():
        if has_state:
            tail_scr[...] = buf_ref[0]
        else:
            tail_scr[...] = jnp.zeros(tail_scr.shape, F32)

    u = u_ref[...]
    xe = jnp.concatenate([tail_scr[...], u], axis=0)
    tail_scr[...] = xe[tc:tc + hist]
    pos1 = (start_pos + 1 + c * tc + lax.broadcasted_iota(jnp.int32, (tc, 1), 0)).astype(F32)
    for gi, win in enumerate(POOL_WINDOWS):
        sl = slice(gi * POOL_GROUP_W, (gi + 1) * POOL_GROUP_W)
        s = xe[:, sl]
        shift = 1
        while shift < win:
            s = s + pltpu.roll(s, shift, 0)
            shift *= 2
        cnt = jnp.minimum(float(win), pos1)
        o_ref[:, sl] = s[hist:hist + tc] / cnt - u[:, sl]


def _pool(proj, row0, B, L, tc, start_pos, buf16=None):
    has_state = buf16 is not None
    nc = L // tc
    rb0 = row0 // tc
    in_specs = [pl.BlockSpec((tc, D_MODEL), lambda b, c: (rb0 + b * nc + c, U_BLK))]
    args = [proj]
    if has_state:
        in_specs.append(pl.BlockSpec((1, POOL_BUF + 1, D_MODEL), lambda b, c: (b, 0, 0)))
        args.append(buf16)
    return pl.pallas_call(
        functools.partial(_pool_kernel, tc=tc, start_pos=start_pos, has_state=has_state),
        grid=(B, nc),
        in_specs=in_specs,
        out_specs=pl.BlockSpec((tc, D_MODEL), lambda b, c: (b * nc + c, 0)),
        out_shape=jax.ShapeDtypeStruct((B * L, D_MODEL), F32),
        scratch_shapes=[pltpu.VMEM((POOL_BUF + 1, D_MODEL), F32)],
        compiler_params=_cparams(("parallel", "arbitrary")),
    )(*args)


def _post_kernel(pooled_ref, o_ref, ga_ref, gb_ref, x_ref, wp_ref, ps_ref, wbb_ref, wba_ref, wo_ref,
                 g_ref, b_ref, h_ref, hb_ref, *, alpha):
    pooled = pooled_ref[...]
    mixed = jnp.concatenate(
        [_dot(pooled[:, gi * POOL_GROUP_W:(gi + 1) * POOL_GROUP_W], wp_ref[gi]) for gi in range(len(POOL_WINDOWS))],
        axis=1) * ps_ref[...]
    branch_b = _dot(mixed, wbb_ref[...])
    branch_a = _dot(o_ref[...], wba_ref[...])
    merged = _sigmoid(ga_ref[...]) * branch_a + _sigmoid(gb_ref[...]) * branch_b
    h = _layer_norm(alpha * x_ref[...] + _dot(merged, wo_ref[...]), g_ref[...], b_ref[...])
    h_ref[...] = h
    hb_ref[...] = h.astype(BF16)


def _post(pooled, o_gated, proj, x, w_pool, pool_scale, w_br_b, w_br_a, w_out, ln_g, ln_b, alpha, tm):
    t = x.shape[0]
    row = lambda i: (i, 0)
    const2 = lambda i: (0, 0)
    return pl.pallas_call(
        functools.partial(_post_kernel, alpha=alpha),
        grid=(t // tm,),
        in_specs=[pl.BlockSpec((tm, D_MODEL), row),
                  pl.BlockSpec((tm, V_W), row),
                  pl.BlockSpec((tm, D_MODEL), lambda i: (i, U_BLK + 1)),
                  pl.BlockSpec((tm, D_MODEL), lambda i: (i, U_BLK + 2)),
                  pl.BlockSpec((tm, D_MODEL), row),
                  pl.BlockSpec(w_pool.shape, lambda i: (0, 0, 0)),
                  pl.BlockSpec((1, D_MODEL), const2),
                  pl.BlockSpec(w_br_b.shape, const2),
                  pl.BlockSpec(w_br_a.shape, const2),
                  pl.BlockSpec(w_out.shape, const2),
                  pl.BlockSpec((1, D_MODEL), const2),
                  pl.BlockSpec((1, D_MODEL), const2)],
        out_specs=[pl.BlockSpec((tm, D_MODEL), row), pl.BlockSpec((tm, D_MODEL), row)],
        out_shape=[jax.ShapeDtypeStruct((t, D_MODEL), F32), jax.ShapeDtypeStruct((t, D_MODEL), BF16)],
        compiler_params=_cparams(("parallel",)),
    )(pooled, o_gated, proj, proj, x, w_pool, pool_scale, w_br_b, w_br_a, w_out, ln_g, ln_b)


def _router_kernel(h_ref, wt_ref, bias_ref, idx_ref, wts_ref, rank_ref, cnt_ref, carry_scr, *, tm):
    i = pl.program_id(0)

    @pl.when(i == 0)
    def _init():
        carry_scr[...] = jnp.zeros(carry_scr.shape, F32)

    logits = _dot3_nt(wt_ref[...], h_ref[...])
    sc = _sigmoid(logits)
    ch = sc + bias_ref[...]
    neg = -jnp.inf
    e_in = lax.broadcasted_iota(jnp.int32, (GROUP_SZ, tm), 0)
    gs_rows = []
    for g in range(N_GROUP):
        blk = ch[g * GROUP_SZ:(g + 1) * GROUP_SZ]
        m1 = jnp.max(blk, axis=0, keepdims=True)
        i1 = jnp.min(jnp.where(blk == m1, e_in, GROUP_SZ), axis=0, keepdims=True)
        m2 = jnp.max(jnp.where(e_in == i1, neg, blk), axis=0, keepdims=True)
        gs_rows.append(m1 + m2)
    gs = jnp.concatenate(gs_rows, axis=0)
    g_io = lax.broadcasted_iota(jnp.int32, (N_GROUP, tm), 0)
    e_io = lax.broadcasted_iota(jnp.int32, (N_EXPERTS, tm), 0)
    e_grp = e_io >> int(math.log2(GROUP_SZ))
    masked = jnp.full((N_EXPERTS, tm), neg, F32)
    for _ in range(TOPK_GROUP):
        mx = jnp.max(gs, axis=0, keepdims=True)
        gi = jnp.min(jnp.where(gs == mx, g_io, N_GROUP), axis=0, keepdims=True)
        gs = jnp.where(g_io == gi, neg, gs)
        masked = jnp.where(e_grp == gi, ch, masked)
    idx_rows, w_rows = [], []
    onehot = jnp.zeros((N_EXPERTS, tm), F32)
    for _ in range(TOP_K):
        mx = jnp.max(masked, axis=0, keepdims=True)
        ei = jnp.min(jnp.where(masked == mx, e_io, N_EXPERTS), axis=0, keepdims=True)
        hit = e_io == ei
        idx_rows.append(ei)
        w_rows.append(jnp.sum(jnp.where(hit, sc, 0.0), axis=0, keepdims=True))
        onehot = jnp.where(hit, 1.0, onehot)
        masked = jnp.where(hit, neg, masked)
    wsel = jnp.concatenate(w_rows, axis=0)
    wts_ref[...] = wsel / jnp.sum(wsel, axis=0, keepdims=True) * ROUTED_SCALE
    idx_ref[...] = jnp.concatenate(idx_rows, axis=0)

    tr = lax.broadcasted_iota(jnp.int32, (tm, tm), 0)
    tc_ = lax.broadcasted_iota(jnp.int32, (tm, tm), 1)
    before = jnp.where(tr < tc_, 1.0, 0.0).astype(BF16)
    cum = jnp.dot(onehot.astype(BF16), before, preferred_element_type=F32) + carry_scr[...]
    rank_ref[...] = jnp.concatenate(
        [jnp.sum(jnp.where(e_io == ei, cum, 0.0), axis=0, keepdims=True) for ei in idx_rows],
        axis=0).astype(jnp.int32)
    carry_scr[...] = carry_scr[...] + jnp.sum(onehot, axis=1, keepdims=True)
    cnt_ref[...] = carry_scr[...]


def _router(h, w_router_t, bias_col, tm):
    t = h.shape[0]
    return pl.pallas_call(
        functools.partial(_router_kernel, tm=tm),
        grid=(t // tm,),
        in_specs=[pl.BlockSpec((tm, D_MODEL), lambda i: (i, 0)),
                  pl.BlockSpec((N_EXPERTS, D_MODEL), lambda i: (0, 0)),
                  pl.BlockSpec((N_EXPERTS, 1), lambda i: (0, 0))],
        out_specs=[pl.BlockSpec((TOP_K, tm), lambda i: (0, i)),
                   pl.BlockSpec((TOP_K, tm), lambda i: (0, i)),
                   pl.BlockSpec((TOP_K, tm), lambda i: (0, i)),
                   pl.BlockSpec((N_EXPERTS, 1), lambda i: (0, 0))],
        out_shape=[jax.ShapeDtypeStruct((TOP_K, t), jnp.int32),
                   jax.ShapeDtypeStruct((TOP_K, t), F32),
                   jax.ShapeDtypeStruct((TOP_K, t), jnp.int32),
                   jax.ShapeDtypeStruct((N_EXPERTS, 1), F32)],
        scratch_shapes=[pltpu.VMEM((N_EXPERTS, 1), F32)],
        compiler_params=_cparams(("arbitrary",)),
    )(h, w_router_t, bias_col)


def _expert_kernel(be_ref, nu_ref, x_ref, wg_ref, wu_ref, wd_ref, y_ref):
    @pl.when(pl.program_id(0) < nu_ref[0])
    def _():
        x = x_ref[...]
        gate = _dot(x, wg_ref[0])
        up = _dot(x, wu_ref[0])
        y_ref[...] = _dot(_silu(gate) * up, wd_ref[0]).astype(y_ref.dtype)


def _experts(xs, blk_exp, n_used, w_gate, w_up, w_down):
    n_rows = xs.shape[0]
    n_blocks = n_rows // EXPERT_BM

    def live(i, nu):
        return jnp.minimum(i, nu[0] - 1)

    grid_spec = pltpu.PrefetchScalarGridSpec(
        num_scalar_prefetch=2,
        grid=(n_blocks,),
        in_specs=[pl.BlockSpec((EXPERT_BM, D_MODEL), lambda i, be, nu: (live(i, nu), 0)),
                  pl.BlockSpec((1, D_MODEL, D_EXPERT), lambda i, be, nu: (be[live(i, nu)], 0, 0)),
                  pl.BlockSpec((1, D_MODEL, D_EXPERT), lambda i, be, nu: (be[live(i, nu)], 0, 0)),
                  pl.BlockSpec((1, D_EXPERT, D_MODEL), lambda i, be, nu: (be[live(i, nu)], 0, 0))],
        out_specs=pl.BlockSpec((EXPERT_BM, D_MODEL), lambda i, be, nu: (live(i, nu), 0)),
    )
    return pl.pallas_call(
        _expert_kernel,
        grid_spec=grid_spec,
        out_shape=jax.ShapeDtypeStruct((n_rows, D_MODEL), BF16),
        compiler_params=_cparams(("arbitrary",)),
    )(blk_exp, n_used, xs, w_gate, w_up, w_down)


def _combine_kernel(yg_ref, wts_ref, h_ref, hb_ref, wgu_ref, wd_ref, g_ref, b_ref, out_ref, *, alpha):
    wts = wts_ref[...]
    routed = yg_ref[0].astype(F32) * wts[:, 0:1]
    for k in range(1, TOP_K):
        routed = routed + yg_ref[k].astype(F32) * wts[:, k:k + 1]
    gu = _dot(hb_ref[...], wgu_ref[...])
    shared = _dot(_silu(gu[:, :D_EXPERT]) * gu[:, D_EXPERT:], wd_ref[...])
    out_ref[...] = _layer_norm(alpha * h_ref[...] + (routed + shared), g_ref[...], b_ref[...])


def _combine(yg, wts, h, hb, w_sh_gu, w_sh_down, ln_g, ln_b, alpha, tm):
    t = h.shape[0]
    row = lambda i: (i, 0)
    const2 = lambda i: (0, 0)
    return pl.pallas_call(
        functools.partial(_combine_kernel, alpha=alpha),
        grid=(t // tm,),
        in_specs=[pl.BlockSpec((TOP_K, tm, D_MODEL), lambda i: (0, i, 0)),
                  pl.BlockSpec((tm, TOP_K), row),
                  pl.BlockSpec((tm, D_MODEL), row),
                  pl.BlockSpec((tm, D_MODEL), row),
                  pl.BlockSpec(w_sh_gu.shape, const2),
                  pl.BlockSpec(w_sh_down.shape, const2),
                  pl.BlockSpec((1, D_MODEL), const2),
                  pl.BlockSpec((1, D_MODEL), const2)],
        out_specs=pl.BlockSpec((tm, D_MODEL), row),
        out_shape=jax.ShapeDtypeStruct((t, D_MODEL), F32),
        compiler_params=_cparams(("parallel",)),
    )(yg, wts, h, hb, w_sh_gu, w_sh_down, ln_g, ln_b)


def _layer(xp, xs, s_delta, s_conv, s_pool, w_in, conv_w, a_log, dt_bias, gamma_a, w_br_a, w_pool,
           pool_scale, w_br_b, w_out, ln1_g, ln1_b, w_router, router_bias, w_exp_gate, w_exp_up,
           w_exp_down, w_sh_gate, w_sh_up, w_sh_down, ln2_g, ln2_b, alpha):
    Bp, Lp, _ = xp.shape
    Bs, Ls, _ = xs.shape
    Tp, Ts = Bp * Lp, Bs * Ls
    T = Tp + Ts
    x = jnp.concatenate([xp.reshape(Tp, D_MODEL), xs.reshape(Ts, D_MODEL)], axis=0)
    xb = x.astype(BF16)

    o_z, o_b, o_a, o_u = QKV_W, QKV_W + V_W, QKV_W + V_W + H_V, QKV_W + V_W + 2 * H_V
    w_main = jnp.concatenate([w_in[:, :o_b], w_in[:, o_u:]], axis=1).astype(BF16)
    w_ba = jnp.pad(w_in[:, o_b:o_u], ((0, 0), (0, 128 - 2 * H_V))).astype(BF16)
    proj = _matmul(xb, w_main, 1024, 512, F32)
    ba = _matmul(xb, w_ba, 1024, 128, F32)

    o_p, sd_p = _gdn(proj, ba, conv_w, a_log, dt_bias, gamma_a, 0, Bp, Lp, 64)
    buf8 = jnp.pad(s_conv, ((0, 0), (8 - (CONV_W - 1), 0), (0, 0)))
    o_s, sd_s = _gdn(proj, ba, conv_w, a_log, dt_bias, gamma_a, Tp, Bs, Ls, Ls, buf8=buf8, s0=s_delta)
    o_gated = jnp.concatenate([o_p, o_s.astype(BF16)], axis=0)

    pooled_p = _pool(proj, 0, Bp, Lp, 256, 0)
    buf16 = jnp.pad(s_pool, ((0, 0), (1, 0), (0, 0)))
    pooled_s = _pool(proj, Tp, Bs, Ls, Ls, PAST_LEN, buf16=buf16)
    pooled = jnp.concatenate([pooled_p, pooled_s], axis=0)

    h, hb = _post(pooled, o_gated, proj, x, w_pool.astype(BF16), pool_scale.reshape(1, D_MODEL),
                  w_br_b.astype(BF16), w_br_a.astype(BF16), w_out.astype(BF16),
                  ln1_g.reshape(1, D_MODEL), ln1_b.reshape(1, D_MODEL), alpha, 256)

    idx_t, wts_t, rank_t, cnt = _router(h, w_router.T, router_bias.reshape(N_EXPERTS, 1), 512)
    counts = cnt[:, 0].astype(jnp.int32)
    padded = ((counts + EXPERT_BM - 1) // EXPERT_BM) * EXPERT_BM
    pends = jnp.cumsum(padded)
    pstarts = pends - padded
    dest = pstarts[idx_t] + rank_t
    n_blocks = (T * TOP_K + N_EXPERTS * (EXPERT_BM - 1) + EXPERT_BM - 1) // EXPERT_BM
    n_rows = n_blocks * EXPERT_BM
    tok = jnp.broadcast_to(jnp.arange(T, dtype=jnp.int32)[None, :], (TOP_K, T))
    row_tok = jnp.zeros((n_rows,), jnp.int32).at[dest.reshape(-1)].set(tok.reshape(-1))
    blk_exp = jnp.minimum(
        jnp.searchsorted(pends, jnp.arange(n_blocks, dtype=jnp.int32) * EXPERT_BM, side="right"),
        N_EXPERTS - 1).astype(jnp.int32)
    n_used = (pends[-1:] // EXPERT_BM).astype(jnp.int32)

    x_sorted = jnp.take(hb, row_tok, axis=0)
    y_sorted = _experts(x_sorted, blk_exp, n_used, w_exp_gate, w_exp_up, w_exp_down)
    yg = jnp.take(y_sorted, dest, axis=0)

    w_sh_gu = jnp.concatenate([w_sh_gate, w_sh_up], axis=1).astype(BF16)
    y = _combine(yg, wts_t.T, h, hb, w_sh_gu, w_sh_down.astype(BF16),
                 ln2_g.reshape(1, D_MODEL), ln2_b.reshape(1, D_MODEL), alpha, 256)

    qkv_p = proj[:Tp, :QKV_W].reshape(Bp, Lp, QKV_W)
    qkv_s = proj[Tp:, :QKV_W].reshape(Bs, Ls, QKV_W)
    u_p = proj[:Tp, o_b:o_b + D_MODEL].reshape(Bp, Lp, D_MODEL)
    u_s = proj[Tp:, o_b:o_b + D_MODEL].reshape(Bs, Ls, D_MODEL)
    new_conv_p = qkv_p[:, Lp - (CONV_W - 1):]
    new_conv_s = jnp.concatenate([s_conv, qkv_s], axis=1)[:, -(CONV_W - 1):]
    new_pool_p = u_p[:, Lp - POOL_BUF:]
    new_pool_s = jnp.concatenate([s_pool, u_s], axis=1)[:, -POOL_BUF:]
    return (y[:Tp].reshape(Bp, Lp, D_MODEL), y[Tp:].reshape(Bs, Ls, D_MODEL),
            sd_p, new_conv_p, new_pool_p, sd_s, new_conv_s, new_pool_s)


def kernel(x_prompt, x_sample, state_delta, state_conv, state_pool, w_in, conv_w, a_log, dt_bias, gamma_a,
           w_br_a, w_pool, pool_scale, w_br_b, w_out, ln1_g, ln1_b, w_router, router_bias,
           w_exp_gate, w_exp_up, w_exp_down, w_sh_gate, w_sh_up, w_sh_down, ln2_g, ln2_b):
    depth = w_in.shape[0]
    alpha = (2 * depth) ** 0.25
    yp, ys = x_prompt, x_sample
    outs = [[] for _ in range(6)]
    for l in range(depth):
        res = _layer(yp, ys, state_delta[l], state_conv[l], state_pool[l], w_in[l], conv_w[l], a_log[l],
                     dt_bias[l], gamma_a[l], w_br_a[l], w_pool[l], pool_scale[l], w_br_b[l], w_out[l],
                     ln1_g[l], ln1_b[l], w_router[l], router_bias[l], w_exp_gate[l], w_exp_up[l],
                     w_exp_down[l], w_sh_gate[l], w_sh_up[l], w_sh_down[l], ln2_g[l], ln2_b[l], alpha)
        yp, ys = res[0], res[1]
        for lst, v in zip(outs, res[2:]):
            lst.append(v)
    return (yp, ys) + tuple(jnp.stack(v) for v in outs)
```

```python
import functools
import math

import jax
import jax.numpy as jnp
from jax import lax
from jax.experimental import pallas as pl
from jax.experimental.pallas import tpu as pltpu
from jax.experimental.pallas import tpu_sc as plsc

F32 = jnp.float32
BF16 = jnp.bfloat16

D_MODEL = 1024
H_K = 8
D_K = 128
H_V = 16
D_V = 128
Q_W = H_K * D_K
V_W = H_V * D_V
QKV_W = 2 * Q_W + V_W
CONV_W = 4
POOL_WINDOWS = (2, 4, 8, 16)
POOL_GROUP_W = D_MODEL // len(POOL_WINDOWS)
POOL_BUF = max(POOL_WINDOWS) - 1
N_EXPERTS = 256
TOP_K = 8
N_GROUP = 8
TOPK_GROUP = 4
GROUP_SZ = N_EXPERTS // N_GROUP
D_EXPERT = D_MODEL // 4
ROUTED_SCALE = 2.5
LN_EPS = 1e-5
NORM_EPS = 1e-6
PAST_LEN = 16384

MAIN_W = QKV_W + V_W + 3 * D_MODEL
Z_BLK = QKV_W // V_W
U_BLK = (QKV_W + V_W) // D_MODEL

STACK_ROWS = 128
EXPERT_BM = 256
VMEM_LIMIT = 56 * 1024 * 1024


def _cparams(sem):
    return pltpu.CompilerParams(dimension_semantics=sem, vmem_limit_bytes=VMEM_LIMIT)


def _sigmoid(x):
    return 1.0 / (1.0 + jnp.exp(-x))


def _silu(x):
    return x * _sigmoid(x)


def _softplus(x):
    return jnp.maximum(x, 0.0) + jnp.log(1.0 + jnp.exp(-jnp.abs(x)))


def _dot(a, b):
    return jnp.dot(a.astype(BF16), b.astype(BF16), preferred_element_type=F32)


def _dot_nt(a, b):
    return lax.dot_general(a.astype(BF16), b.astype(BF16), (((1,), (1,)), ((), ())),
                           preferred_element_type=F32)


def _dot_tn(a, b):
    return lax.dot_general(a.astype(BF16), b.astype(BF16), (((0,), (0,)), ((), ())),
                           preferred_element_type=F32)


def _split(a):
    hi = a.astype(BF16)
    lo = (a - hi.astype(F32)).astype(BF16)
    return hi, lo


def _dot3(a, b):
    ah, al = _split(a)
    bh, bl = _split(b)
    d = functools.partial(jnp.dot, preferred_element_type=F32)
    return d(ah, bh) + (d(ah, bl) + d(al, bh))


def _dot3_nt(a, b):
    ah, al = _split(a)
    bh, bl = _split(b)
    d = functools.partial(lax.dot_general, dimension_numbers=(((1,), (1,)), ((), ())),
                          preferred_element_type=F32)
    return d(ah, bh) + (d(ah, bl) + d(al, bh))


def _pack_halves(x):
    n = x.shape[1] // 2
    hi = lax.bitcast_convert_type(x[:, :n].astype(BF16).astype(F32), jnp.uint32)
    lo = lax.bitcast_convert_type(x[:, n:].astype(BF16).astype(F32), jnp.uint32)
    return (hi & jnp.uint32(0xFFFF0000)) | (lo >> 16)


def _unpack_halves(w):
    hi = lax.bitcast_convert_type(w & jnp.uint32(0xFFFF0000), F32)
    lo = lax.bitcast_convert_type(w << 16, F32)
    return hi, lo


def _layer_norm(x, g, b):
    mu = jnp.mean(x, axis=-1, keepdims=True)
    xc = x - mu
    var = jnp.mean(xc * xc, axis=-1, keepdims=True)
    return xc * lax.rsqrt(var + LN_EPS) * g + b


def _mm_kernel(x_ref, w_ref, o_ref):
    o_ref[...] = jnp.dot(x_ref[...], w_ref[...], preferred_element_type=F32).astype(o_ref.dtype)


def _matmul(x, w, tm, tn, out_dtype, name):
    t, k = x.shape
    n = w.shape[1]
    return pl.pallas_call(
        _mm_kernel,
        grid=(t // tm, n // tn),
        in_specs=[pl.BlockSpec((tm, k), lambda i, j: (i, 0)),
                  pl.BlockSpec((k, tn), lambda i, j: (0, j))],
        out_specs=pl.BlockSpec((tm, tn), lambda i, j: (i, j)),
        out_shape=jax.ShapeDtypeStruct((t, n), out_dtype),
        compiler_params=_cparams(("parallel", "parallel")),
        name=name,
    )(x, w)


def _gdn_kernel(*refs, C, hg, has_state):
    if has_state:
        (qkv_ref, z_ref, ba_ref, cw_ref, ad_ref, gam_ref, buf_ref, s0_ref,
         o_ref, sout_ref, s_scr, tail_scr, act_scr) = refs
    else:
        (qkv_ref, z_ref, ba_ref, cw_ref, ad_ref, gam_ref,
         o_ref, sout_ref, s_scr, tail_scr, act_scr) = refs
    R = hg * C
    ng = H_V // hg
    log2c = int(math.log2(C))
    c = pl.program_id(1)

    @pl.when(c == 0)
    def _init():
        if has_state:
            s_scr[...] = s0_ref[0]
            tail_scr[...] = buf_ref[0]
        else:
            s_scr[...] = jnp.zeros(s_scr.shape, F32)
            tail_scr[...] = jnp.zeros(tail_scr.shape, F32)

    x = qkv_ref[...]
    xe = jnp.concatenate([tail_scr[...], x], axis=0)
    cw = cw_ref[...]
    y = xe[8:8 + C] * cw[3:4]
    for j in range(CONV_W - 1):
        y = y + xe[5 + j:5 + j + C] * cw[j:j + 1]
    tail_scr[...] = x[C - 8:C]
    act_scr[...] = _silu(y)

    ri = lax.broadcasted_iota(jnp.int32, (R, R), 0)
    ci = lax.broadcasted_iota(jnp.int32, (R, R), 1)
    same = (ri >> log2c) == (ci >> log2c)
    incl = same & (ri >= ci)
    strict = same & (ri > ci)
    eye = ri == ci
    is_last = ci == (((ri >> log2c) << log2c) + (C - 1))
    eye_f = jnp.where(eye, 1.0, 0.0).astype(F32)
    cum_u = jnp.where(same & (ri <= ci), 1.0, 0.0).astype(BF16)

    nrow = ba_ref.shape[2] // 2
    bav = ba_ref[0, 0]
    adv = ad_ref[...]
    beta_rows = _sigmoid(bav[0:nrow])
    g_rows = -jnp.exp(adv[0:nrow]) * _softplus(bav[nrow:2 * nrow] + adv[nrow:2 * nrow])
    g1 = g_rows.astype(BF16)
    r1 = g_rows - g1.astype(F32)
    g2 = r1.astype(BF16)
    g3 = (r1 - g2.astype(F32)).astype(BF16)
    dd = functools.partial(jnp.dot, preferred_element_type=F32)
    G_rows = dd(g1, cum_u) + (dd(g2, cum_u) + dd(g3, cum_u))

    gam = gam_ref[...]
    normed = {}

    def qk_head(kind, kh):
        key = (kind, kh)
        if key not in normed:
            off = (0 if kind == "q" else Q_W) + kh * D_K
            v = act_scr[:, off:off + D_K]
            v = v * lax.rsqrt(jnp.sum(v * v, axis=-1, keepdims=True) + NORM_EPS)
            if kind == "q":
                v = v * (D_K ** -0.5)
            normed[key] = v
        return normed[key]

    for g in range(ng):
        heads = [g * hg + hh for hh in range(hg)]
        rep = H_V // H_K
        qst = jnp.concatenate([qk_head("q", h // rep) for h in heads], axis=0)
        kst = jnp.concatenate([qk_head("k", h // rep) for h in heads], axis=0)
        vst = jnp.concatenate([act_scr[:, 2 * Q_W + h * D_V:2 * Q_W + (h + 1) * D_V] for h in heads], axis=0)
        zst = jnp.concatenate([z_ref[:, h * D_V:(h + 1) * D_V] for h in heads], axis=0)

        grow = G_rows[g:g + 1]
        brow = beta_rows[g:g + 1]
        gcol = jnp.sum(eye_f * grow, axis=1, keepdims=True)
        bcol = jnp.sum(eye_f * brow, axis=1, keepdims=True)
        glast = jnp.sum(jnp.where(is_last, grow, 0.0), axis=1, keepdims=True)
        dm = jnp.where(incl, jnp.exp(jnp.minimum(gcol - grow, 0.0)), 0.0)

        kk = _dot_nt(kst, kst)
        lm = jnp.where(strict, kk * bcol * dm, 0.0)
        tinv = eye_f - lm
        pw = lm
        for _ in range(log2c - 1):
            pw = _dot3(pw, pw)
            tinv = tinv + _dot3(tinv, pw)
        eg = jnp.exp(gcol)
        rhs = jnp.concatenate([vst * bcol, kst * (bcol * eg)], axis=1)
        sol = _dot3(tinv, rhs)
        u = sol[:, :D_V]
        w = sol[:, D_V:]
        attn = jnp.where(incl, _dot_nt(qst, kst) * dm, 0.0)
        qe = qst * eg
        kt = kst * jnp.exp(glast - gcol)
        egl = jnp.exp(glast)

        vnew_parts, qs_parts = [], []
        for hh, h in enumerate(heads):
            sl = slice(hh * C, (hh + 1) * C)
            xs = _dot(jnp.concatenate([w[sl], qe[sl]], axis=0), s_scr[h])
            vnew_parts.append(u[sl] - xs[:C])
            qs_parts.append(xs[C:])
        vnew = jnp.concatenate(vnew_parts, axis=0)
        o = jnp.concatenate(qs_parts, axis=0) + _dot(attn, vnew)

        row_head = lax.broadcasted_iota(jnp.int32, (R, D_V), 0) >> log2c
        for hh, h in enumerate(heads):
            sl = slice(hh * C, (hh + 1) * C)
            if C >= 16:
                upd = _dot_tn(kt[sl], vnew[sl])
            else:
                upd = _dot_tn(kt, jnp.where(row_head == hh, vnew, 0.0))
            s_scr[h] = s_scr[h] * egl[hh * C:hh * C + 1] + upd

        on = o * lax.rsqrt(jnp.mean(o * o, axis=-1, keepdims=True) + NORM_EPS) * gam * _silu(zst)
        for hh, h in enumerate(heads):
            o_ref[:, h * D_V:(h + 1) * D_V] = on[hh * C:(hh + 1) * C].astype(o_ref.dtype)

    @pl.when(c == pl.num_programs(1) - 1)
    def _fin():
        sout_ref[0] = s_scr[...]


def _gdn(proj, ba, conv_w, a_log, dt_bias, gamma_a, row0, B, L, C, buf8=None, s0=None):
    has_state = s0 is not None
    hg = STACK_ROWS // C
    ng = H_V // hg
    nrow = 16
    nc = L // C
    R = STACK_ROWS

    def arrange(v):
        v = v.reshape(B, nc, C, ng, hg).transpose(0, 1, 3, 4, 2).reshape(B, nc, ng, R)
        return jnp.pad(v, ((0, 0), (0, 0), (0, nrow - ng), (0, 0)))

    rows = ba[row0:row0 + B * L]
    ba_arr = jnp.concatenate([arrange(rows[:, :H_V]), arrange(rows[:, H_V:2 * H_V])], axis=2)

    def arrange_param(p):
        v = jnp.broadcast_to(p.reshape(ng, hg, 1), (ng, hg, C)).reshape(ng, R)
        return jnp.pad(v, ((0, nrow - ng), (0, 0)))

    ad = jnp.concatenate([arrange_param(a_log), arrange_param(dt_bias)], axis=0)
    rb0 = row0 // C
    in_specs = [
        pl.BlockSpec((C, QKV_W), lambda b, c: (rb0 + b * nc + c, 0)),
        pl.BlockSpec((C, V_W), lambda b, c: (rb0 + b * nc + c, Z_BLK)),
        pl.BlockSpec((1, 1, 2 * nrow, R), lambda b, c: (b, c, 0, 0)),
        pl.BlockSpec((CONV_W, QKV_W), lambda b, c: (0, 0)),
        pl.BlockSpec((2 * nrow, R), lambda b, c: (0, 0)),
        pl.BlockSpec((1, D_V), lambda b, c: (0, 0)),
    ]
    args = [proj, proj, ba_arr, conv_w, ad, gamma_a.reshape(1, D_V)]
    if has_state:
        in_specs += [pl.BlockSpec((1, 8, QKV_W), lambda b, c: (b, 0, 0)),
                     pl.BlockSpec((1, H_V, D_K, D_V), lambda b, c: (b, 0, 0, 0))]
        args += [buf8, s0]
    o_dtype = BF16 if C % 16 == 0 else F32
    return pl.pallas_call(
        functools.partial(_gdn_kernel, C=C, hg=hg, has_state=has_state),
        grid=(B, nc),
        in_specs=in_specs,
        out_specs=[pl.BlockSpec((C, V_W), lambda b, c: (b * nc + c, 0)),
                   pl.BlockSpec((1, H_V, D_K, D_V), lambda b, c: (b, 0, 0, 0))],
        out_shape=[jax.ShapeDtypeStruct((B * L, V_W), o_dtype),
                   jax.ShapeDtypeStruct((B, H_V, D_K, D_V), F32)],
        scratch_shapes=[pltpu.VMEM((H_V, D_K, D_V), F32),
                        pltpu.VMEM((8, QKV_W), F32),
                        pltpu.VMEM((C, QKV_W), F32)],
        compiler_params=_cparams(("parallel", "arbitrary")),
        name="gdn_state" if has_state else "gdn_fresh",
    )(*args)


def _pool_kernel(*refs, tc, start_pos, has_state):
    if has_state:
        u_ref, buf_ref, o_ref, tail_scr = refs
    else:
        u_ref, o_ref, tail_scr = refs
    hist = POOL_BUF + 1
    c = pl.program_id(1)

    @pl.when(c == 0)
    def _init():
        if has_state:
            tail_scr[...] = buf_ref[0]
        else:
            tail_scr[...] = jnp.zeros(tail_scr.shape, F32)

    u = u_ref[...]
    xe = jnp.concatenate([tail_scr[...], u], axis=0)
    tail_scr[...] = xe[tc:tc + hist]
    pos1 = (start_pos + 1 + c * tc + lax.broadcasted_iota(jnp.int32, (tc, 1), 0)).astype(F32)
    for gi, win in enumerate(POOL_WINDOWS):
        sl = slice(gi * POOL_GROUP_W, (gi + 1) * POOL_GROUP_W)
        s = xe[:, sl]
        shift = 1
        while shift < win:
            s = s + pltpu.roll(s, shift, 0)
            shift *= 2
        cnt = jnp.minimum(float(win), pos1)
        o_ref[:, sl] = s[hist:hist + tc] / cnt - u[:, sl]


def _pool(proj, row0, B, L, tc, start_pos, buf16=None):
    has_state = buf16 is not None
    nc = L // tc
    rb0 = row0 // tc
    in_specs = [pl.BlockSpec((tc, D_MODEL), lambda b, c: (rb0 + b * nc + c, U_BLK))]
    args = [proj]
    if has_state:
        in_specs.append(pl.BlockSpec((1, POOL_BUF + 1, D_MODEL), lambda b, c: (b, 0, 0)))
        args.append(buf16)
    return pl.pallas_call(
        functools.partial(_pool_kernel, tc=tc, start_pos=start_pos, has_state=has_state),
        grid=(B, nc),
        in_specs=in_specs,
        out_specs=pl.BlockSpec((tc, D_MODEL), lambda b, c: (b * nc + c, 0)),
        out_shape=jax.ShapeDtypeStruct((B * L, D_MODEL), F32),
        scratch_shapes=[pltpu.VMEM((POOL_BUF + 1, D_MODEL), F32)],
        compiler_params=_cparams(("parallel", "arbitrary")),
        name="pool_state" if has_state else "pool_fresh",
    )(*args)


def _post_kernel(pooled_ref, o_ref, ga_ref, gb_ref, x_ref, wp_ref, ps_ref, wbb_ref, wba_ref, wo_ref,
                 g_ref, b_ref, h_ref, hp_ref, *, alpha):
    pooled = pooled_ref[...]
    mixed = jnp.concatenate(
        [_dot(pooled[:, gi * POOL_GROUP_W:(gi + 1) * POOL_GROUP_W], wp_ref[gi]) for gi in range(len(POOL_WINDOWS))],
        axis=1) * ps_ref[...]
    branch_b = _dot(mixed, wbb_ref[...])
    branch_a = _dot(o_ref[...], wba_ref[...])
    merged = _sigmoid(ga_ref[...]) * branch_a + _sigmoid(gb_ref[...]) * branch_b
    h = _layer_norm(alpha * x_ref[...] + _dot(merged, wo_ref[...]), g_ref[...], b_ref[...])
    h_ref[...] = h
    hp_ref[...] = _pack_halves(h)


def _post(pooled, o_gated, proj, x, w_pool, pool_scale, w_br_b, w_br_a, w_out, ln_g, ln_b, alpha, tm):
    t = x.shape[0]
    row = lambda i: (i, 0)
    const2 = lambda i: (0, 0)
    return pl.pallas_call(
        functools.partial(_post_kernel, alpha=alpha),
        grid=(t // tm,),
        in_specs=[pl.BlockSpec((tm, D_MODEL), row),
                  pl.BlockSpec((tm, V_W), row),
                  pl.BlockSpec((tm, D_MODEL), lambda i: (i, U_BLK + 1)),
                  pl.BlockSpec((tm, D_MODEL), lambda i: (i, U_BLK + 2)),
                  pl.BlockSpec((tm, D_MODEL), row),
                  pl.BlockSpec(w_pool.shape, lambda i: (0, 0, 0)),
                  pl.BlockSpec((1, D_MODEL), const2),
                  pl.BlockSpec(w_br_b.shape, const2),
                  pl.BlockSpec(w_br_a.shape, const2),
                  pl.BlockSpec(w_out.shape, const2),
                  pl.BlockSpec((1, D_MODEL), const2),
                  pl.BlockSpec((1, D_MODEL), const2)],
        out_specs=[pl.BlockSpec((tm, D_MODEL), row), pl.BlockSpec((tm, D_MODEL // 2), row)],
        out_shape=[jax.ShapeDtypeStruct((t, D_MODEL), F32), jax.ShapeDtypeStruct((t, D_MODEL // 2), jnp.uint32)],
        compiler_params=_cparams(("parallel",)),
        name="post_mixers",
    )(pooled, o_gated, proj, proj, x, w_pool, pool_scale, w_br_b, w_br_a, w_out, ln_g, ln_b)


def _router_kernel(h_ref, wt_ref, bias_ref, idx_ref, wts_ref, rank_ref, cnt_ref, carry_scr, *, tm):
    i = pl.program_id(0)

    @pl.when(i == 0)
    def _init():
        carry_scr[...] = jnp.zeros(carry_scr.shape, F32)

    logits = _dot3_nt(wt_ref[...], h_ref[...])
    sc = _sigmoid(logits)
    ch = sc + bias_ref[...]
    neg = -jnp.inf
    e_in = lax.broadcasted_iota(jnp.int32, (GROUP_SZ, tm), 0)
    gs_rows = []
    for g in range(N_GROUP):
        blk = ch[g * GROUP_SZ:(g + 1) * GROUP_SZ]
        m1 = jnp.max(blk, axis=0, keepdims=True)
        i1 = jnp.min(jnp.where(blk == m1, e_in, GROUP_SZ), axis=0, keepdims=True)
        m2 = jnp.max(jnp.where(e_in == i1, neg, blk), axis=0, keepdims=True)
        gs_rows.append(m1 + m2)
    gs = jnp.concatenate(gs_rows, axis=0)
    g_io = lax.broadcasted_iota(jnp.int32, (N_GROUP, tm), 0)
    e_io = lax.broadcasted_iota(jnp.int32, (N_EXPERTS, tm), 0)
    e_grp = e_io >> int(math.log2(GROUP_SZ))
    masked = jnp.full((N_EXPERTS, tm), neg, F32)
    for _ in range(TOPK_GROUP):
        mx = jnp.max(gs, axis=0, keepdims=True)
        gi = jnp.min(jnp.where(gs == mx, g_io, N_GROUP), axis=0, keepdims=True)
        gs = jnp.where(g_io == gi, neg, gs)
        masked = jnp.where(e_grp == gi, ch, masked)
    idx_rows, w_rows = [], []
    onehot = jnp.zeros((N_EXPERTS, tm), F32)
    for _ in range(TOP_K):
        mx = jnp.max(masked, axis=0, keepdims=True)
        ei = jnp.min(jnp.where(masked == mx, e_io, N_EXPERTS), axis=0, keepdims=True)
        hit = e_io == ei
        idx_rows.append(ei)
        w_rows.append(jnp.sum(jnp.where(hit, sc, 0.0), axis=0, keepdims=True))
        onehot = jnp.where(hit, 1.0, onehot)
        masked = jnp.where(hit, neg, masked)
    wsel = jnp.concatenate(w_rows, axis=0)
    wts_ref[...] = wsel / jnp.sum(wsel, axis=0, keepdims=True) * ROUTED_SCALE
    idx_ref[...] = jnp.concatenate(idx_rows, axis=0)

    tr = lax.broadcasted_iota(jnp.int32, (tm, tm), 0)
    tc_ = lax.broadcasted_iota(jnp.int32, (tm, tm), 1)
    before = jnp.where(tr < tc_, 1.0, 0.0).astype(BF16)
    cum = jnp.dot(onehot.astype(BF16), before, preferred_element_type=F32) + carry_scr[...]
    rank_ref[...] = jnp.concatenate(
        [jnp.sum(jnp.where(e_io == ei, cum, 0.0), axis=0, keepdims=True) for ei in idx_rows],
        axis=0).astype(jnp.int32)
    carry_scr[...] = carry_scr[...] + jnp.sum(onehot, axis=1, keepdims=True)
    cnt_ref[...] = carry_scr[...]


def _router(h, w_router_t, bias_col, tm):
    t = h.shape[0]
    return pl.pallas_call(
        functools.partial(_router_kernel, tm=tm),
        grid=(t // tm,),
        in_specs=[pl.BlockSpec((tm, D_MODEL), lambda i: (i, 0)),
                  pl.BlockSpec((N_EXPERTS, D_MODEL), lambda i: (0, 0)),
                  pl.BlockSpec((N_EXPERTS, 1), lambda i: (0, 0))],
        out_specs=[pl.BlockSpec((TOP_K, tm), lambda i: (0, i)),
                   pl.BlockSpec((TOP_K, tm), lambda i: (0, i)),
                   pl.BlockSpec((TOP_K, tm), lambda i: (0, i)),
                   pl.BlockSpec((N_EXPERTS, 1), lambda i: (0, 0))],
        out_shape=[jax.ShapeDtypeStruct((TOP_K, t), jnp.int32),
                   jax.ShapeDtypeStruct((TOP_K, t), F32),
                   jax.ShapeDtypeStruct((TOP_K, t), jnp.int32),
                   jax.ShapeDtypeStruct((N_EXPERTS, 1), F32)],
        scratch_shapes=[pltpu.VMEM((N_EXPERTS, 1), F32)],
        compiler_params=_cparams(("arbitrary",)),
        name="router",
    )(h, w_router_t, bias_col)


def _expert_kernel(be_ref, nu_ref, x_ref, wg_ref, wu_ref, wd_ref, y_ref):
    @pl.when(pl.program_id(0) < nu_ref[0])
    def _():
        half = D_MODEL // 2
        xa, xb = _unpack_halves(x_ref[...])
        wg = wg_ref[0]
        wu = wu_ref[0]
        gate = _dot(xa, wg[:half]) + _dot(xb, wg[half:])
        up = _dot(xa, wu[:half]) + _dot(xb, wu[half:])
        y_ref[...] = _pack_halves(_dot(_silu(gate) * up, wd_ref[0]))


def _experts(xs, blk_exp, n_used, w_gate, w_up, w_down):
    n_rows, half = xs.shape
    n_blocks = n_rows // EXPERT_BM

    def live(i, nu):
        return jnp.minimum(i, nu[0] - 1)

    grid_spec = pltpu.PrefetchScalarGridSpec(
        num_scalar_prefetch=2,
        grid=(n_blocks,),
        in_specs=[pl.BlockSpec((EXPERT_BM, half), lambda i, be, nu: (live(i, nu), 0)),
                  pl.BlockSpec((1, D_MODEL, D_EXPERT), lambda i, be, nu: (be[live(i, nu)], 0, 0)),
                  pl.BlockSpec((1, D_MODEL, D_EXPERT), lambda i, be, nu: (be[live(i, nu)], 0, 0)),
                  pl.BlockSpec((1, D_EXPERT, D_MODEL), lambda i, be, nu: (be[live(i, nu)], 0, 0))],
        out_specs=pl.BlockSpec((EXPERT_BM, half), lambda i, be, nu: (live(i, nu), 0)),
    )
    return pl.pallas_call(
        _expert_kernel,
        grid_spec=grid_spec,
        out_shape=jax.ShapeDtypeStruct((n_rows, half), jnp.uint32),
        compiler_params=_cparams(("arbitrary",)),
        name="experts",
    )(blk_exp, n_used, xs, w_gate, w_up, w_down)


SC_WINDOW = 128
SC_SPLIT = 2


def _split_rows(a):
    return a.reshape(a.shape[0] * SC_SPLIT, a.shape[1] // SC_SPLIT)


def _split_index(idx):
    return (idx[:, None] * SC_SPLIT + jnp.arange(SC_SPLIT, dtype=idx.dtype)[None, :]).reshape(-1)


def _sc_mesh():
    return plsc.VectorSubcoreMesh(core_axis_name="core", subcore_axis_name="subcore")


def _sc_scatter_rows(src, dest, n_rows):
    n_src, d = src.shape
    n_idx = dest.shape[0]
    src_blocks = n_src // SC_WINDOW
    dest2 = dest.reshape(1, n_idx)

    @functools.partial(pl.kernel, out_type=jax.ShapeDtypeStruct((n_rows, d), src.dtype), mesh=_sc_mesh(),
                       scratch_types=[], name="sc_dispatch")
    def run(src_hbm, idx_hbm, out_hbm):
        def body(rows_vmem, idx_vmem):
            pltpu.sync_copy(rows_vmem, out_hbm.at[idx_vmem.at[0]])

        pltpu.emit_pipeline(
            body,
            grid=(n_idx // SC_WINDOW,),
            in_specs=[pl.BlockSpec((SC_WINDOW, d), lambda i: (i % src_blocks, 0)),
                      pl.BlockSpec((1, SC_WINDOW), lambda i: (0, i))],
            out_specs=[],
            core_axis_name=("core", "subcore"),
            dimension_semantics=(pltpu.PARALLEL,),
        )(src_hbm, idx_hbm)

    return run(src, dest2)


def _sc_gather_rows(table, idx):
    d = table.shape[1]
    n_idx = idx.shape[0]
    idx2 = idx.reshape(1, n_idx)

    @functools.partial(pl.kernel, out_type=jax.ShapeDtypeStruct((n_idx, d), table.dtype), mesh=_sc_mesh(),
                       scratch_types=[], name="sc_combine_gather")
    def run(table_hbm, idx_hbm, out_hbm):
        def body(idx_vmem, rows_vmem):
            pltpu.sync_copy(table_hbm.at[idx_vmem.at[0]], rows_vmem)

        pltpu.emit_pipeline(
            body,
            grid=(n_idx // SC_WINDOW,),
            in_specs=[pl.BlockSpec((1, SC_WINDOW), lambda i: (0, i))],
            out_specs=[pl.BlockSpec((SC_WINDOW, d), lambda i: (i, 0))],
            core_axis_name=("core", "subcore"),
            dimension_semantics=(pltpu.PARALLEL,),
        )(idx_hbm, out_hbm)

    return run(table, idx2)


def _combine_kernel(yg_ref, wts_ref, h_ref, wgu_ref, wd_ref, g_ref, b_ref, out_ref, *, alpha):
    wts = wts_ref[...]
    r_hi, r_lo = None, None
    for k in range(TOP_K):
        y_hi, y_lo = _unpack_halves(yg_ref[k])
        wk = wts[:, k:k + 1]
        r_hi = y_hi * wk if r_hi is None else r_hi + y_hi * wk
        r_lo = y_lo * wk if r_lo is None else r_lo + y_lo * wk
    routed = jnp.concatenate([r_hi, r_lo], axis=1)
    h = h_ref[...]
    gu = _dot(h, wgu_ref[...])
    shared = _dot(_silu(gu[:, :D_EXPERT]) * gu[:, D_EXPERT:], wd_ref[...])
    out_ref[...] = _layer_norm(alpha * h + (routed + shared), g_ref[...], b_ref[...])


def _combine(yg, wts, h, w_sh_gu, w_sh_down, ln_g, ln_b, alpha, tm):
    t = h.shape[0]
    row = lambda i: (i, 0)
    const2 = lambda i: (0, 0)
    return pl.pallas_call(
        functools.partial(_combine_kernel, alpha=alpha),
        grid=(t // tm,),
        in_specs=[pl.BlockSpec((TOP_K, tm, D_MODEL // 2), lambda i: (0, i, 0)),
                  pl.BlockSpec((tm, TOP_K), row),
                  pl.BlockSpec((tm, D_MODEL), row),
                  pl.BlockSpec(w_sh_gu.shape, const2),
                  pl.BlockSpec(w_sh_down.shape, const2),
                  pl.BlockSpec((1, D_MODEL), const2),
                  pl.BlockSpec((1, D_MODEL), const2)],
        out_specs=pl.BlockSpec((tm, D_MODEL), row),
        out_shape=jax.ShapeDtypeStruct((t, D_MODEL), F32),
        compiler_params=_cparams(("parallel",)),
        name="combine_ln2",
    )(yg, wts, h, w_sh_gu, w_sh_down, ln_g, ln_b)


def _layer(xp, xs, s_delta, s_conv, s_pool, w_in, conv_w, a_log, dt_bias, gamma_a, w_br_a, w_pool,
           pool_scale, w_br_b, w_out, ln1_g, ln1_b, w_router, router_bias, w_exp_gate, w_exp_up,
           w_exp_down, w_sh_gate, w_sh_up, w_sh_down, ln2_g, ln2_b, alpha):
    Bp, Lp, _ = xp.shape
    Bs, Ls, _ = xs.shape
    Tp, Ts = Bp * Lp, Bs * Ls
    T = Tp + Ts
    x = jnp.concatenate([xp.reshape(Tp, D_MODEL), xs.reshape(Ts, D_MODEL)], axis=0)
    xb = x.astype(BF16)

    o_z, o_b, o_a, o_u = QKV_W, QKV_W + V_W, QKV_W + V_W + H_V, QKV_W + V_W + 2 * H_V
    w_main = jnp.concatenate([w_in[:, :o_b], w_in[:, o_u:]], axis=1).astype(BF16)
    w_ba = jnp.pad(w_in[:, o_b:o_u], ((0, 0), (0, 128 - 2 * H_V))).astype(BF16)
    proj = _matmul(xb, w_main, 1024, 512, F32, "in_proj")
    ba = _matmul(xb, w_ba, 1024, 128, F32, "in_proj_beta_a")

    o_p, sd_p = _gdn(proj, ba, conv_w, a_log, dt_bias, gamma_a, 0, Bp, Lp, 64)
    buf8 = jnp.pad(s_conv, ((0, 0), (8 - (CONV_W - 1), 0), (0, 0)))
    o_s, sd_s = _gdn(proj, ba, conv_w, a_log, dt_bias, gamma_a, Tp, Bs, Ls, Ls, buf8=buf8, s0=s_delta)
    o_gated = jnp.concatenate([o_p, o_s.astype(BF16)], axis=0)

    pooled_p = _pool(proj, 0, Bp, Lp, 256, 0)
    buf16 = jnp.pad(s_pool, ((0, 0), (1, 0), (0, 0)))
    pooled_s = _pool(proj, Tp, Bs, Ls, Ls, PAST_LEN, buf16=buf16)
    pooled = jnp.concatenate([pooled_p, pooled_s], axis=0)

    h, hp = _post(pooled, o_gated, proj, x, w_pool.astype(BF16), pool_scale.reshape(1, D_MODEL),
                  w_br_b.astype(BF16), w_br_a.astype(BF16), w_out.astype(BF16),
                  ln1_g.reshape(1, D_MODEL), ln1_b.reshape(1, D_MODEL), alpha, 256)

    idx_t, wts_t, rank_t, cnt = _router(h, w_router.T, router_bias.reshape(N_EXPERTS, 1), 512)
    counts = cnt[:, 0].astype(jnp.int32)
    padded = ((counts + EXPERT_BM - 1) // EXPERT_BM) * EXPERT_BM
    pends = jnp.cumsum(padded)
    pstarts = pends - padded
    dest = pstarts[idx_t] + rank_t
    n_blocks = (T * TOP_K + N_EXPERTS * (EXPERT_BM - 1) + EXPERT_BM - 1) // EXPERT_BM
    n_rows = n_blocks * EXPERT_BM
    blk_start = jnp.arange(n_blocks, dtype=jnp.int32) * EXPERT_BM
    blk_exp = jnp.minimum(jnp.sum((pends[None, :] <= blk_start[:, None]).astype(jnp.int32), axis=1),
                          N_EXPERTS - 1)
    n_used = (pends[-1:] // EXPERT_BM).astype(jnp.int32)

    dest_flat = dest.reshape(TOP_K * T)
    piece_idx = _split_index(dest_flat)
    x_sorted = _sc_scatter_rows(_split_rows(hp), piece_idx, n_rows * SC_SPLIT).reshape(n_rows, D_MODEL // 2)
    y_sorted = _experts(x_sorted, blk_exp, n_used, w_exp_gate, w_exp_up, w_exp_down)
    yg = _sc_gather_rows(_split_rows(y_sorted), piece_idx).reshape(TOP_K, T, D_MODEL // 2)

    w_sh_gu = jnp.concatenate([w_sh_gate, w_sh_up], axis=1).astype(BF16)
    y = _combine(yg, wts_t.T, h, w_sh_gu, w_sh_down.astype(BF16),
                 ln2_g.reshape(1, D_MODEL), ln2_b.reshape(1, D_MODEL), alpha, 256)

    def last_rows(group_row0, B, L, n, col0, width):
        rows = group_row0 + jnp.arange(B, dtype=jnp.int32)[:, None] * L + (L - n) + jnp.arange(n, dtype=jnp.int32)
        return jnp.take(proj, rows.reshape(-1), axis=0)[:, col0:col0 + width].reshape(B, n, width)

    nc_ = CONV_W - 1
    new_conv_p = last_rows(0, Bp, Lp, nc_, 0, QKV_W)
    new_pool_p = last_rows(0, Bp, Lp, POOL_BUF, o_b, D_MODEL)
    qkv_s = proj[Tp:, :QKV_W].reshape(Bs, Ls, QKV_W)
    u_s = proj[Tp:, o_b:o_b + D_MODEL].reshape(Bs, Ls, D_MODEL)
    new_conv_s = jnp.concatenate([s_conv, qkv_s], axis=1)[:, -nc_:]
    new_pool_s = jnp.concatenate([s_pool, u_s], axis=1)[:, -POOL_BUF:]
    return (y[:Tp].reshape(Bp, Lp, D_MODEL), y[Tp:].reshape(Bs, Ls, D_MODEL),
            sd_p, new_conv_p, new_pool_p, sd_s, new_conv_s, new_pool_s)


def kernel(x_prompt, x_sample, state_delta, state_conv, state_pool, w_in, conv_w, a_log, dt_bias, gamma_a,
           w_br_a, w_pool, pool_scale, w_br_b, w_out, ln1_g, ln1_b, w_router, router_bias,
           w_exp_gate, w_exp_up, w_exp_down, w_sh_gate, w_sh_up, w_sh_down, ln2_g, ln2_b):
    depth = w_in.shape[0]
    alpha = (2 * depth) ** 0.25
    yp, ys = x_prompt, x_sample
    outs = [[] for _ in range(6)]
    for l in range(depth):
        res = _layer(yp, ys, state_delta[l], state_conv[l], state_pool[l], w_in[l], conv_w[l], a_log[l],
                     dt_bias[l], gamma_a[l], w_br_a[l], w_pool[l], pool_scale[l], w_br_b[l], w_out[l],
                     ln1_g[l], ln1_b[l], w_router[l], router_bias[l], w_exp_gate[l], w_exp_up[l],
                     w_exp_down[l], w_sh_gate[l], w_sh_up[l], w_sh_down[l], ln2_g[l], ln2_b[l], alpha)
        yp, ys = res[0], res[1]
        for lst, v in zip(outs, res[2:]):
            lst.append(v)
    return (yp, ys) + tuple(jnp.stack(v) for v in outs)
```

```python
import functools
import math

import jax
import jax.numpy as jnp
from jax import lax
from jax.experimental import pallas as pl
from jax.experimental.pallas import tpu as pltpu
from jax.experimental.pallas import tpu_sc as plsc

F32 = jnp.float32
BF16 = jnp.bfloat16

D_MODEL = 1024
H_K = 8
D_K = 128
H_V = 16
D_V = 128
Q_W = H_K * D_K
V_W = H_V * D_V
QKV_W = 2 * Q_W + V_W
CONV_W = 4
POOL_WINDOWS = (2, 4, 8, 16)
POOL_GROUP_W = D_MODEL // len(POOL_WINDOWS)
POOL_BUF = max(POOL_WINDOWS) - 1
N_EXPERTS = 256
TOP_K = 8
N_GROUP = 8
TOPK_GROUP = 4
GROUP_SZ = N_EXPERTS // N_GROUP
D_EXPERT = D_MODEL // 4
ROUTED_SCALE = 2.5
LN_EPS = 1e-5
NORM_EPS = 1e-6
PAST_LEN = 16384

MAIN_W = QKV_W + V_W + 3 * D_MODEL
Z_BLK = QKV_W // V_W
U_BLK = (QKV_W + V_W) // D_MODEL

STACK_ROWS = 128
EXPERT_BM = 256
VMEM_LIMIT = 56 * 1024 * 1024


def _cparams(sem):
    return pltpu.CompilerParams(dimension_semantics=sem, vmem_limit_bytes=VMEM_LIMIT)


def _sigmoid(x):
    return 1.0 / (1.0 + jnp.exp(-x))


def _silu(x):
    return x * _sigmoid(x)


def _softplus(x):
    return jnp.maximum(x, 0.0) + jnp.log(1.0 + jnp.exp(-jnp.abs(x)))


def _dot(a, b):
    return jnp.dot(a.astype(BF16), b.astype(BF16), preferred_element_type=F32)


def _dot_nt(a, b):
    return lax.dot_general(a.astype(BF16), b.astype(BF16), (((1,), (1,)), ((), ())),
                           preferred_element_type=F32)


def _dot_tn(a, b):
    return lax.dot_general(a.astype(BF16), b.astype(BF16), (((0,), (0,)), ((), ())),
                           preferred_element_type=F32)


def _split(a):
    hi = a.astype(BF16)
    lo = (a - hi.astype(F32)).astype(BF16)
    return hi, lo


def _dot3(a, b):
    ah, al = _split(a)
    bh, bl = _split(b)
    d = functools.partial(jnp.dot, preferred_element_type=F32)
    return d(ah, bh) + (d(ah, bl) + d(al, bh))


def _dot3_nt(a, b):
    ah, al = _split(a)
    bh, bl = _split(b)
    d = functools.partial(lax.dot_general, dimension_numbers=(((1,), (1,)), ((), ())),
                          preferred_element_type=F32)
    return d(ah, bh) + (d(ah, bl) + d(al, bh))


def _pack_halves(x):
    n = x.shape[1] // 2
    hi = lax.bitcast_convert_type(x[:, :n].astype(BF16).astype(F32), jnp.uint32)
    lo = lax.bitcast_convert_type(x[:, n:].astype(BF16).astype(F32), jnp.uint32)
    return (hi & jnp.uint32(0xFFFF0000)) | (lo >> 16)


def _unpack_halves(w):
    hi = lax.bitcast_convert_type(w & jnp.uint32(0xFFFF0000), F32)
    lo = lax.bitcast_convert_type(w << 16, F32)
    return hi, lo


N_PIECE = 2
PIECE_W = D_MODEL // (2 * N_PIECE)


def _store_pieces(ref, x):
    for p in range(N_PIECE):
        ref[p] = _pack_halves(x[:, 2 * p * PIECE_W:2 * (p + 1) * PIECE_W])


def _load_pieces(pieces):
    cols = []
    for w in pieces:
        cols.extend(_unpack_halves(w))
    return cols


def _layer_norm(x, g, b):
    mu = jnp.mean(x, axis=-1, keepdims=True)
    xc = x - mu
    var = jnp.mean(xc * xc, axis=-1, keepdims=True)
    return xc * lax.rsqrt(var + LN_EPS) * g + b


def _mm_kernel(x_ref, w_ref, o_ref):
    o_ref[...] = jnp.dot(x_ref[...], w_ref[...], preferred_element_type=F32).astype(o_ref.dtype)


def _matmul(x, w, tm, tn, out_dtype, name):
    t, k = x.shape
    n = w.shape[1]
    return pl.pallas_call(
        _mm_kernel,
        grid=(t // tm, n // tn),
        in_specs=[pl.BlockSpec((tm, k), lambda i, j: (i, 0)),
                  pl.BlockSpec((k, tn), lambda i, j: (0, j))],
        out_specs=pl.BlockSpec((tm, tn), lambda i, j: (i, j)),
        out_shape=jax.ShapeDtypeStruct((t, n), out_dtype),
        compiler_params=_cparams(("parallel", "parallel")),
        name=name,
    )(x, w)


def _gdn_kernel(*refs, C, hg, has_state):
    if has_state:
        (qkv_ref, z_ref, ba_ref, cw_ref, ad_ref, gam_ref, buf_ref, s0_ref,
         o_ref, sout_ref, s_scr, tail_scr, act_scr) = refs
    else:
        (qkv_ref, z_ref, ba_ref, cw_ref, ad_ref, gam_ref,
         o_ref, sout_ref, s_scr, tail_scr, act_scr) = refs
    R = hg * C
    ng = H_V // hg
    log2c = int(math.log2(C))
    c = pl.program_id(1)

    @pl.when(c == 0)
    def _init():
        if has_state:
            s_scr[...] = s0_ref[0]
            tail_scr[...] = buf_ref[0]
        else:
            s_scr[...] = jnp.zeros(s_scr.shape, F32)
            tail_scr[...] = jnp.zeros(tail_scr.shape, F32)

    x = qkv_ref[...]
    xe = jnp.concatenate([tail_scr[...], x], axis=0)
    cw = cw_ref[...]
    y = xe[8:8 + C] * cw[3:4]
    for j in range(CONV_W - 1):
        y = y + xe[5 + j:5 + j + C] * cw[j:j + 1]
    tail_scr[...] = x[C - 8:C]
    act_scr[...] = _silu(y)

    ri = lax.broadcasted_iota(jnp.int32, (R, R), 0)
    ci = lax.broadcasted_iota(jnp.int32, (R, R), 1)
    same = (ri >> log2c) == (ci >> log2c)
    incl = same & (ri >= ci)
    strict = same & (ri > ci)
    eye = ri == ci
    is_last = ci == (((ri >> log2c) << log2c) + (C - 1))
    eye_f = jnp.where(eye, 1.0, 0.0).astype(F32)
    cum_u = jnp.where(same & (ri <= ci), 1.0, 0.0).astype(BF16)

    nrow = ba_ref.shape[2] // 2
    bav = ba_ref[0, 0]
    adv = ad_ref[...]
    beta_rows = _sigmoid(bav[0:nrow])
    g_rows = -jnp.exp(adv[0:nrow]) * _softplus(bav[nrow:2 * nrow] + adv[nrow:2 * nrow])
    g1 = g_rows.astype(BF16)
    r1 = g_rows - g1.astype(F32)
    g2 = r1.astype(BF16)
    g3 = (r1 - g2.astype(F32)).astype(BF16)
    dd = functools.partial(jnp.dot, preferred_element_type=F32)
    G_rows = dd(g1, cum_u) + (dd(g2, cum_u) + dd(g3, cum_u))

    gam = gam_ref[...]
    normed = {}

    def qk_head(kind, kh):
        key = (kind, kh)
        if key not in normed:
            off = (0 if kind == "q" else Q_W) + kh * D_K
            v = act_scr[:, off:off + D_K]
            v = v * lax.rsqrt(jnp.sum(v * v, axis=-1, keepdims=True) + NORM_EPS)
            if kind == "q":
                v = v * (D_K ** -0.5)
            normed[key] = v
        return normed[key]

    for g in range(ng):
        heads = [g * hg + hh for hh in range(hg)]
        rep = H_V // H_K
        qst = jnp.concatenate([qk_head("q", h // rep) for h in heads], axis=0)
        kst = jnp.concatenate([qk_head("k", h // rep) for h in heads], axis=0)
        vst = jnp.concatenate([act_scr[:, 2 * Q_W + h * D_V:2 * Q_W + (h + 1) * D_V] for h in heads], axis=0)
        zst = jnp.concatenate([z_ref[:, h * D_V:(h + 1) * D_V] for h in heads], axis=0)

        grow = G_rows[g:g + 1]
        brow = beta_rows[g:g + 1]
        gcol = jnp.sum(eye_f * grow, axis=1, keepdims=True)
        bcol = jnp.sum(eye_f * brow, axis=1, keepdims=True)
        glast = jnp.sum(jnp.where(is_last, grow, 0.0), axis=1, keepdims=True)
        dm = jnp.where(incl, jnp.exp(jnp.minimum(gcol - grow, 0.0)), 0.0)

        kk = _dot_nt(kst, kst)
        lm = jnp.where(strict, kk * bcol * dm, 0.0)
        tinv = eye_f - lm
        pw = lm
        for _ in range(log2c - 1):
            pw = _dot3(pw, pw)
            tinv = tinv + _dot3(tinv, pw)
        eg = jnp.exp(gcol)
        rhs = jnp.concatenate([vst * bcol, kst * (bcol * eg)], axis=1)
        sol = _dot3(tinv, rhs)
        u = sol[:, :D_V]
        w = sol[:, D_V:]
        attn = jnp.where(incl, _dot_nt(qst, kst) * dm, 0.0)
        qe = qst * eg
        kt = kst * jnp.exp(glast - gcol)
        egl = jnp.exp(glast)

        vnew_parts, qs_parts = [], []
        for hh, h in enumerate(heads):
            sl = slice(hh * C, (hh + 1) * C)
            xs = _dot(jnp.concatenate([w[sl], qe[sl]], axis=0), s_scr[h])
            vnew_parts.append(u[sl] - xs[:C])
            qs_parts.append(xs[C:])
        vnew = jnp.concatenate(vnew_parts, axis=0)
        o = jnp.concatenate(qs_parts, axis=0) + _dot(attn, vnew)

        row_head = lax.broadcasted_iota(jnp.int32, (R, D_V), 0) >> log2c
        for hh, h in enumerate(heads):
            sl = slice(hh * C, (hh + 1) * C)
            if C >= 16:
                upd = _dot_tn(kt[sl], vnew[sl])
            else:
                upd = _dot_tn(kt, jnp.where(row_head == hh, vnew, 0.0))
            s_scr[h] = s_scr[h] * egl[hh * C:hh * C + 1] + upd

        on = o * lax.rsqrt(jnp.mean(o * o, axis=-1, keepdims=True) + NORM_EPS) * gam * _silu(zst)
        for hh, h in enumerate(heads):
            o_ref[:, h * D_V:(h + 1) * D_V] = on[hh * C:(hh + 1) * C].astype(o_ref.dtype)

    @pl.when(c == pl.num_programs(1) - 1)
    def _fin():
        sout_ref[0] = s_scr[...]


def _gdn(proj, ba, conv_w, a_log, dt_bias, gamma_a, row0, B, L, C, buf8=None, s0=None):
    has_state = s0 is not None
    hg = STACK_ROWS // C
    ng = H_V // hg
    nrow = 16
    nc = L // C
    R = STACK_ROWS

    def arrange(v):
        v = v.reshape(B, nc, C, ng, hg).transpose(0, 1, 3, 4, 2).reshape(B, nc, ng, R)
        return jnp.pad(v, ((0, 0), (0, 0), (0, nrow - ng), (0, 0)))

    rows = ba[row0:row0 + B * L]
    ba_arr = jnp.concatenate([arrange(rows[:, :H_V]), arrange(rows[:, H_V:2 * H_V])], axis=2)

    def arrange_param(p):
        v = jnp.broadcast_to(p.reshape(ng, hg, 1), (ng, hg, C)).reshape(ng, R)
        return jnp.pad(v, ((0, nrow - ng), (0, 0)))

    ad = jnp.concatenate([arrange_param(a_log), arrange_param(dt_bias)], axis=0)
    rb0 = row0 // C
    in_specs = [
        pl.BlockSpec((C, QKV_W), lambda b, c: (rb0 + b * nc + c, 0)),
        pl.BlockSpec((C, V_W), lambda b, c: (rb0 + b * nc + c, Z_BLK)),
        pl.BlockSpec((1, 1, 2 * nrow, R), lambda b, c: (b, c, 0, 0)),
        pl.BlockSpec((CONV_W, QKV_W), lambda b, c: (0, 0)),
        pl.BlockSpec((2 * nrow, R), lambda b, c: (0, 0)),
        pl.BlockSpec((1, D_V), lambda b, c: (0, 0)),
    ]
    args = [proj, proj, ba_arr, conv_w, ad, gamma_a.reshape(1, D_V)]
    if has_state:
        in_specs += [pl.BlockSpec((1, 8, QKV_W), lambda b, c: (b, 0, 0)),
                     pl.BlockSpec((1, H_V, D_K, D_V), lambda b, c: (b, 0, 0, 0))]
        args += [buf8, s0]
    o_dtype = BF16 if C % 16 == 0 else F32
    return pl.pallas_call(
        functools.partial(_gdn_kernel, C=C, hg=hg, has_state=has_state),
        grid=(B, nc),
        in_specs=in_specs,
        out_specs=[pl.BlockSpec((C, V_W), lambda b, c: (b * nc + c, 0)),
                   pl.BlockSpec((1, H_V, D_K, D_V), lambda b, c: (b, 0, 0, 0))],
        out_shape=[jax.ShapeDtypeStruct((B * L, V_W), o_dtype),
                   jax.ShapeDtypeStruct((B, H_V, D_K, D_V), F32)],
        scratch_shapes=[pltpu.VMEM((H_V, D_K, D_V), F32),
                        pltpu.VMEM((8, QKV_W), F32),
                        pltpu.VMEM((C, QKV_W), F32)],
        compiler_params=_cparams(("parallel", "arbitrary")),
        name="gdn_state" if has_state else "gdn_fresh",
    )(*args)


def _pool_kernel(*refs, tc, start_pos, has_state):
    if has_state:
        u_ref, buf_ref, o_ref, tail_scr = refs
    else:
        u_ref, o_ref, tail_scr = refs
    hist = POOL_BUF + 1
    c = pl.program_id(1)

    @pl.when(c == 0)
    def _init():
        if has_state:
            tail_scr[...] = buf_ref[0]
        else:
            tail_scr[...] = jnp.zeros(tail_scr.shape, F32)

    u = u_ref[...]
    xe = jnp.concatenate([tail_scr[...], u], axis=0)
    tail_scr[...] = xe[tc:tc + hist]
    pos1 = (start_pos + 1 + c * tc + lax.broadcasted_iota(jnp.int32, (tc, 1), 0)).astype(F32)
    for gi, win in enumerate(POOL_WINDOWS):
        sl = slice(gi * POOL_GROUP_W, (gi + 1) * POOL_GROUP_W)
        s = xe[:, sl]
        shift = 1
        while shift < win:
            s = s + pltpu.roll(s, shift, 0)
            shift *= 2
        cnt = jnp.minimum(float(win), pos1)
        o_ref[:, sl] = s[hist:hist + tc] / cnt - u[:, sl]


def _pool(proj, row0, B, L, tc, start_pos, buf16=None):
    has_state = buf16 is not None
    nc = L // tc
    rb0 = row0 // tc
    in_specs = [pl.BlockSpec((tc, D_MODEL), lambda b, c: (rb0 + b * nc + c, U_BLK))]
    args = [proj]
    if has_state:
        in_specs.append(pl.BlockSpec((1, POOL_BUF + 1, D_MODEL), lambda b, c: (b, 0, 0)))
        args.append(buf16)
    return pl.pallas_call(
        functools.partial(_pool_kernel, tc=tc, start_pos=start_pos, has_state=has_state),
        grid=(B, nc),
        in_specs=in_specs,
        out_specs=pl.BlockSpec((tc, D_MODEL), lambda b, c: (b * nc + c, 0)),
        out_shape=jax.ShapeDtypeStruct((B * L, D_MODEL), F32),
        scratch_shapes=[pltpu.VMEM((POOL_BUF + 1, D_MODEL), F32)],
        compiler_params=_cparams(("parallel", "arbitrary")),
        name="pool_state" if has_state else "pool_fresh",
    )(*args)


def _post_kernel(pooled_ref, o_ref, ga_ref, gb_ref, x_ref, wp_ref, ps_ref, wbb_ref, wba_ref, wo_ref,
                 g_ref, b_ref, h_ref, hp_ref, *, alpha):
    pooled = pooled_ref[...]
    mixed = jnp.concatenate(
        [_dot(pooled[:, gi * POOL_GROUP_W:(gi + 1) * POOL_GROUP_W], wp_ref[gi]) for gi in range(len(POOL_WINDOWS))],
        axis=1) * ps_ref[...]
    branch_b = _dot(mixed, wbb_ref[...])
    branch_a = _dot(o_ref[...], wba_ref[...])
    merged = _sigmoid(ga_ref[...]) * branch_a + _sigmoid(gb_ref[...]) * branch_b
    h = _layer_norm(alpha * x_ref[...] + _dot(merged, wo_ref[...]), g_ref[...], b_ref[...])
    h_ref[...] = h
    _store_pieces(hp_ref, h)


def _post(pooled, o_gated, proj, x, w_pool, pool_scale, w_br_b, w_br_a, w_out, ln_g, ln_b, alpha, tm):
    t = x.shape[0]
    row = lambda i: (i, 0)
    const2 = lambda i: (0, 0)
    return pl.pallas_call(
        functools.partial(_post_kernel, alpha=alpha),
        grid=(t // tm,),
        in_specs=[pl.BlockSpec((tm, D_MODEL), row),
                  pl.BlockSpec((tm, V_W), row),
                  pl.BlockSpec((tm, D_MODEL), lambda i: (i, U_BLK + 1)),
                  pl.BlockSpec((tm, D_MODEL), lambda i: (i, U_BLK + 2)),
                  pl.BlockSpec((tm, D_MODEL), row),
                  pl.BlockSpec(w_pool.shape, lambda i: (0, 0, 0)),
                  pl.BlockSpec((1, D_MODEL), const2),
                  pl.BlockSpec(w_br_b.shape, const2),
                  pl.BlockSpec(w_br_a.shape, const2),
                  pl.BlockSpec(w_out.shape, const2),
                  pl.BlockSpec((1, D_MODEL), const2),
                  pl.BlockSpec((1, D_MODEL), const2)],
        out_specs=[pl.BlockSpec((tm, D_MODEL), row), pl.BlockSpec((N_PIECE, tm, PIECE_W), lambda i: (0, i, 0))],
        out_shape=[jax.ShapeDtypeStruct((t, D_MODEL), F32),
                   jax.ShapeDtypeStruct((N_PIECE, t, PIECE_W), jnp.uint32)],
        compiler_params=_cparams(("parallel",)),
        name="post_mixers",
    )(pooled, o_gated, proj, proj, x, w_pool, pool_scale, w_br_b, w_br_a, w_out, ln_g, ln_b)


def _router_kernel(h_ref, wt_ref, bias_ref, idx_ref, wts_ref, rank_ref, cnt_ref, carry_scr, *, tm):
    i = pl.program_id(0)

    @pl.when(i == 0)
    def _init():
        carry_scr[...] = jnp.zeros(carry_scr.shape, F32)

    logits = _dot3_nt(wt_ref[...], h_ref[...])
    sc = _sigmoid(logits)
    ch = sc + bias_ref[...]
    neg = -jnp.inf
    e_in = lax.broadcasted_iota(jnp.int32, (GROUP_SZ, tm), 0)
    gs_rows = []
    for g in range(N_GROUP):
        blk = ch[g * GROUP_SZ:(g + 1) * GROUP_SZ]
        m1 = jnp.max(blk, axis=0, keepdims=True)
        i1 = jnp.min(jnp.where(blk == m1, e_in, GROUP_SZ), axis=0, keepdims=True)
        m2 = jnp.max(jnp.where(e_in == i1, neg, blk), axis=0, keepdims=True)
        gs_rows.append(m1 + m2)
    gs = jnp.concatenate(gs_rows, axis=0)
    g_io = lax.broadcasted_iota(jnp.int32, (N_GROUP, tm), 0)
    e_io = lax.broadcasted_iota(jnp.int32, (N_EXPERTS, tm), 0)
    e_grp = e_io >> int(math.log2(GROUP_SZ))
    masked = jnp.full((N_EXPERTS, tm), neg, F32)
    for _ in range(TOPK_GROUP):
        mx = jnp.max(gs, axis=0, keepdims=True)
        gi = jnp.min(jnp.where(gs == mx, g_io, N_GROUP), axis=0, keepdims=True)
        gs = jnp.where(g_io == gi, neg, gs)
        masked = jnp.where(e_grp == gi, ch, masked)
    idx_rows, w_rows = [], []
    onehot = jnp.zeros((N_EXPERTS, tm), F32)
    for _ in range(TOP_K):
        mx = jnp.max(masked, axis=0, keepdims=True)
        ei = jnp.min(jnp.where(masked == mx, e_io, N_EXPERTS), axis=0, keepdims=True)
        hit = e_io == ei
        idx_rows.append(ei)
        w_rows.append(jnp.sum(jnp.where(hit, sc, 0.0), axis=0, keepdims=True))
        onehot = jnp.where(hit, 1.0, onehot)
        masked = jnp.where(hit, neg, masked)
    wsel = jnp.concatenate(w_rows, axis=0)
    wts_ref[...] = wsel / jnp.sum(wsel, axis=0, keepdims=True) * ROUTED_SCALE
    idx_ref[...] = jnp.concatenate(idx_rows, axis=0)

    tr = lax.broadcasted_iota(jnp.int32, (tm, tm), 0)
    tc_ = lax.broadcasted_iota(jnp.int32, (tm, tm), 1)
    before = jnp.where(tr < tc_, 1.0, 0.0).astype(BF16)
    cum = jnp.dot(onehot.astype(BF16), before, preferred_element_type=F32) + carry_scr[...]
    rank_ref[...] = jnp.concatenate(
        [jnp.sum(jnp.where(e_io == ei, cum, 0.0), axis=0, keepdims=True) for ei in idx_rows],
        axis=0).astype(jnp.int32)
    carry_scr[...] = carry_scr[...] + jnp.sum(onehot, axis=1, keepdims=True)
    cnt_ref[...] = carry_scr[...]


def _router(h, w_router_t, bias_col, tm):
    t = h.shape[0]
    return pl.pallas_call(
        functools.partial(_router_kernel, tm=tm),
        grid=(t // tm,),
        in_specs=[pl.BlockSpec((tm, D_MODEL), lambda i: (i, 0)),
                  pl.BlockSpec((N_EXPERTS, D_MODEL), lambda i: (0, 0)),
                  pl.BlockSpec((N_EXPERTS, 1), lambda i: (0, 0))],
        out_specs=[pl.BlockSpec((TOP_K, tm), lambda i: (0, i)),
                   pl.BlockSpec((TOP_K, tm), lambda i: (0, i)),
                   pl.BlockSpec((TOP_K, tm), lambda i: (0, i)),
                   pl.BlockSpec((N_EXPERTS, 1), lambda i: (0, 0))],
        out_shape=[jax.ShapeDtypeStruct((TOP_K, t), jnp.int32),
                   jax.ShapeDtypeStruct((TOP_K, t), F32),
                   jax.ShapeDtypeStruct((TOP_K, t), jnp.int32),
                   jax.ShapeDtypeStruct((N_EXPERTS, 1), F32)],
        scratch_shapes=[pltpu.VMEM((N_EXPERTS, 1), F32)],
        compiler_params=_cparams(("arbitrary",)),
        name="router",
    )(h, w_router_t, bias_col)


def _dest_kernel(idx_ref, rank_ref, start_ref, dest_ref, *, tm, n_rows):
    e_io = lax.broadcasted_iota(jnp.int32, (N_EXPERTS, tm), 0)
    starts = start_ref[...]
    rows = []
    for k in range(TOP_K):
        seg = jnp.sum(jnp.where(e_io == idx_ref[k:k + 1, :], starts, 0.0), axis=0, keepdims=True)
        rows.append(seg.astype(jnp.int32) + rank_ref[k:k + 1, :])
    base = jnp.concatenate(rows, axis=0)
    for p in range(N_PIECE):
        dest_ref[p] = base + p * n_rows


def _dest(idx_t, rank_t, seg_start_col, n_rows, tm):
    t = idx_t.shape[1]
    blk = pl.BlockSpec((TOP_K, tm), lambda i: (0, i))
    return pl.pallas_call(
        functools.partial(_dest_kernel, tm=tm, n_rows=n_rows),
        grid=(t // tm,),
        in_specs=[blk, blk, pl.BlockSpec((N_EXPERTS, 1), lambda i: (0, 0))],
        out_specs=pl.BlockSpec((N_PIECE, TOP_K, tm), lambda i: (0, 0, i)),
        out_shape=jax.ShapeDtypeStruct((N_PIECE, TOP_K, t), jnp.int32),
        compiler_params=_cparams(("parallel",)),
        name="dispatch_rows",
    )(idx_t, rank_t, seg_start_col)


def _expert_kernel(be_ref, nu_ref, x_ref, wg_ref, wu_ref, wd_ref, y_ref):
    @pl.when(pl.program_id(0) < nu_ref[0])
    def _():
        x = jnp.concatenate([c.astype(BF16) for c in _load_pieces([x_ref[p] for p in range(N_PIECE)])], axis=1)
        gate = _dot(x, wg_ref[0])
        up = _dot(x, wu_ref[0])
        _store_pieces(y_ref, _dot(_silu(gate) * up, wd_ref[0]))


def _experts(xs, blk_exp, n_used, w_gate, w_up, w_down):
    n_rows = xs.shape[1]
    n_blocks = n_rows // EXPERT_BM

    def live(i, nu):
        return jnp.minimum(i, nu[0] - 1)

    grid_spec = pltpu.PrefetchScalarGridSpec(
        num_scalar_prefetch=2,
        grid=(n_blocks,),
        in_specs=[pl.BlockSpec((N_PIECE, EXPERT_BM, PIECE_W), lambda i, be, nu: (0, live(i, nu), 0)),
                  pl.BlockSpec((1, D_MODEL, D_EXPERT), lambda i, be, nu: (be[live(i, nu)], 0, 0)),
                  pl.BlockSpec((1, D_MODEL, D_EXPERT), lambda i, be, nu: (be[live(i, nu)], 0, 0)),
                  pl.BlockSpec((1, D_EXPERT, D_MODEL), lambda i, be, nu: (be[live(i, nu)], 0, 0))],
        out_specs=pl.BlockSpec((N_PIECE, EXPERT_BM, PIECE_W), lambda i, be, nu: (0, live(i, nu), 0)),
    )
    return pl.pallas_call(
        _expert_kernel,
        grid_spec=grid_spec,
        out_shape=jax.ShapeDtypeStruct((N_PIECE, n_rows, PIECE_W), jnp.uint32),
        compiler_params=_cparams(("arbitrary",)),
        name="experts",
    )(blk_exp, n_used, xs, w_gate, w_up, w_down)


SC_WINDOW = 128


def _sc_mesh():
    return plsc.VectorSubcoreMesh(core_axis_name="core", subcore_axis_name="subcore")


def _sc_scatter_rows(src, dest, n_rows, seg, repeat):
    d = src.shape[1]
    n_idx = dest.shape[0]
    seg_blocks = seg // SC_WINDOW
    dest2 = dest.reshape(1, n_idx)

    def src_block(i):
        return ((i // (repeat * seg_blocks)) * seg_blocks + i % seg_blocks, 0)

    @functools.partial(pl.kernel, out_type=jax.ShapeDtypeStruct((n_rows, d), src.dtype), mesh=_sc_mesh(),
                       scratch_types=[], name="sc_dispatch")
    def run(src_hbm, idx_hbm, out_hbm):
        def body(rows_vmem, idx_vmem):
            pltpu.sync_copy(rows_vmem, out_hbm.at[idx_vmem.at[0]])

        pltpu.emit_pipeline(
            body,
            grid=(n_idx // SC_WINDOW,),
            in_specs=[pl.BlockSpec((SC_WINDOW, d), src_block),
                      pl.BlockSpec((1, SC_WINDOW), lambda i: (0, i))],
            out_specs=[],
            core_axis_name=("core", "subcore"),
            dimension_semantics=(pltpu.PARALLEL,),
        )(src_hbm, idx_hbm)

    return run(src, dest2)


def _sc_gather_rows(table, idx):
    d = table.shape[1]
    n_idx = idx.shape[0]
    idx2 = idx.reshape(1, n_idx)

    @functools.partial(pl.kernel, out_type=jax.ShapeDtypeStruct((n_idx, d), table.dtype), mesh=_sc_mesh(),
                       scratch_types=[], name="sc_combine_gather")
    def run(table_hbm, idx_hbm, out_hbm):
        def body(idx_vmem, rows_vmem):
            pltpu.sync_copy(table_hbm.at[idx_vmem.at[0]], rows_vmem)

        pltpu.emit_pipeline(
            body,
            grid=(n_idx // SC_WINDOW,),
            in_specs=[pl.BlockSpec((1, SC_WINDOW), lambda i: (0, i))],
            out_specs=[pl.BlockSpec((SC_WINDOW, d), lambda i: (i, 0))],
            core_axis_name=("core", "subcore"),
            dimension_semantics=(pltpu.PARALLEL,),
        )(idx_hbm, out_hbm)

    return run(table, idx2)


def _combine_kernel(yg_ref, wts_ref, h_ref, wgu_ref, wd_ref, g_ref, b_ref, out_ref, *, alpha):
    wts = wts_ref[...]
    acc = None
    for k in range(TOP_K):
        cols = _load_pieces([yg_ref[p, k] for p in range(N_PIECE)])
        wk = wts[:, k:k + 1]
        acc = [c * wk for c in cols] if acc is None else [a + c * wk for a, c in zip(acc, cols)]
    routed = jnp.concatenate(acc, axis=1)
    h = h_ref[...]
    gu = _dot(h, wgu_ref[...])
    shared = _dot(_silu(gu[:, :D_EXPERT]) * gu[:, D_EXPERT:], wd_ref[...])
    out_ref[...] = _layer_norm(alpha * h + (routed + shared), g_ref[...], b_ref[...])


def _combine(yg, wts, h, w_sh_gu, w_sh_down, ln_g, ln_b, alpha, tm):
    t = h.shape[0]
    row = lambda i: (i, 0)
    const2 = lambda i: (0, 0)
    return pl.pallas_call(
        functools.partial(_combine_kernel, alpha=alpha),
        grid=(t // tm,),
        in_specs=[pl.BlockSpec((N_PIECE, TOP_K, tm, PIECE_W), lambda i: (0, 0, i, 0)),
                  pl.BlockSpec((tm, TOP_K), row),
                  pl.BlockSpec((tm, D_MODEL), row),
                  pl.BlockSpec(w_sh_gu.shape, const2),
                  pl.BlockSpec(w_sh_down.shape, const2),
                  pl.BlockSpec((1, D_MODEL), const2),
                  pl.BlockSpec((1, D_MODEL), const2)],
        out_specs=pl.BlockSpec((tm, D_MODEL), row),
        out_shape=jax.ShapeDtypeStruct((t, D_MODEL), F32),
        compiler_params=_cparams(("parallel",)),
        name="combine_ln2",
    )(yg, wts, h, w_sh_gu, w_sh_down, ln_g, ln_b)


def _layer(xp, xs, s_delta, s_conv, s_pool, w_in, conv_w, a_log, dt_bias, gamma_a, w_br_a, w_pool,
           pool_scale, w_br_b, w_out, ln1_g, ln1_b, w_router, router_bias, w_exp_gate, w_exp_up,
           w_exp_down, w_sh_gate, w_sh_up, w_sh_down, ln2_g, ln2_b, alpha):
    Bp, Lp, _ = xp.shape
    Bs, Ls, _ = xs.shape
    Tp, Ts = Bp * Lp, Bs * Ls
    T = Tp + Ts
    x = jnp.concatenate([xp.reshape(Tp, D_MODEL), xs.reshape(Ts, D_MODEL)], axis=0)
    xb = x.astype(BF16)

    o_z, o_b, o_a, o_u = QKV_W, QKV_W + V_W, QKV_W + V_W + H_V, QKV_W + V_W + 2 * H_V
    w_main = jnp.concatenate([w_in[:, :o_b], w_in[:, o_u:]], axis=1).astype(BF16)
    w_ba = jnp.pad(w_in[:, o_b:o_u], ((0, 0), (0, 128 - 2 * H_V))).astype(BF16)
    proj = _matmul(xb, w_main, 1024, 512, F32, "in_proj")
    ba = _matmul(xb, w_ba, 1024, 128, F32, "in_proj_beta_a")

    o_p, sd_p = _gdn(proj, ba, conv_w, a_log, dt_bias, gamma_a, 0, Bp, Lp, 64)
    buf8 = jnp.pad(s_conv, ((0, 0), (8 - (CONV_W - 1), 0), (0, 0)))
    o_s, sd_s = _gdn(proj, ba, conv_w, a_log, dt_bias, gamma_a, Tp, Bs, Ls, Ls, buf8=buf8, s0=s_delta)
    o_gated = jnp.concatenate([o_p, o_s.astype(BF16)], axis=0)

    pooled_p = _pool(proj, 0, Bp, Lp, 256, 0)
    buf16 = jnp.pad(s_pool, ((0, 0), (1, 0), (0, 0)))
    pooled_s = _pool(proj, Tp, Bs, Ls, Ls, PAST_LEN, buf16=buf16)
    pooled = jnp.concatenate([pooled_p, pooled_s], axis=0)

    h, hp = _post(pooled, o_gated, proj, x, w_pool.astype(BF16), pool_scale.reshape(1, D_MODEL),
                  w_br_b.astype(BF16), w_br_a.astype(BF16), w_out.astype(BF16),
                  ln1_g.reshape(1, D_MODEL), ln1_b.reshape(1, D_MODEL), alpha, 256)

    idx_t, wts_t, rank_t, cnt = _router(h, w_router.T, router_bias.reshape(N_EXPERTS, 1), 512)
    counts = cnt[:, 0].astype(jnp.int32)
    padded = ((counts + EXPERT_BM - 1) // EXPERT_BM) * EXPERT_BM
    pends = jnp.cumsum(padded)
    pstarts = pends - padded
    n_blocks = (T * TOP_K + N_EXPERTS * (EXPERT_BM - 1) + EXPERT_BM - 1) // EXPERT_BM
    n_rows = n_blocks * EXPERT_BM
    piece_rows = _dest(idx_t, rank_t, pstarts.astype(F32).reshape(N_EXPERTS, 1), n_rows, 512)
    blk_start = jnp.arange(n_blocks, dtype=jnp.int32) * EXPERT_BM
    blk_exp = jnp.minimum(jnp.sum((pends[None, :] <= blk_start[:, None]).astype(jnp.int32), axis=1),
                          N_EXPERTS - 1)
    n_used = (pends[-1:] // EXPERT_BM).astype(jnp.int32)

    piece_idx = piece_rows.reshape(N_PIECE * TOP_K * T)
    x_sorted = _sc_scatter_rows(hp.reshape(N_PIECE * T, PIECE_W), piece_idx, N_PIECE * n_rows, T, TOP_K)
    y_sorted = _experts(x_sorted.reshape(N_PIECE, n_rows, PIECE_W), blk_exp, n_used,
                        w_exp_gate, w_exp_up, w_exp_down)
    yg = _sc_gather_rows(y_sorted.reshape(N_PIECE * n_rows, PIECE_W), piece_idx)
    yg = yg.reshape(N_PIECE, TOP_K, T, PIECE_W)

    w_sh_gu = jnp.concatenate([w_sh_gate, w_sh_up], axis=1).astype(BF16)
    y = _combine(yg, wts_t.T, h, w_sh_gu, w_sh_down.astype(BF16),
                 ln2_g.reshape(1, D_MODEL), ln2_b.reshape(1, D_MODEL), alpha, 256)

    def last_rows(group_row0, B, L, n, col0, width):
        rows = group_row0 + jnp.arange(B, dtype=jnp.int32)[:, None] * L + (L - n) + jnp.arange(n, dtype=jnp.int32)
        return jnp.take(proj, rows.reshape(-1), axis=0)[:, col0:col0 + width].reshape(B, n, width)

    nc_ = CONV_W - 1
    new_conv_p = last_rows(0, Bp, Lp, nc_, 0, QKV_W)
    new_pool_p = last_rows(0, Bp, Lp, POOL_BUF, o_b, D_MODEL)
    qkv_s = proj[Tp:, :QKV_W].reshape(Bs, Ls, QKV_W)
    u_s = proj[Tp:, o_b:o_b + D_MODEL].reshape(Bs, Ls, D_MODEL)
    new_conv_s = jnp.concatenate([s_conv, qkv_s], axis=1)[:, -nc_:]
    new_pool_s = jnp.concatenate([s_pool, u_s], axis=1)[:, -POOL_BUF:]
    return (y[:Tp].reshape(Bp, Lp, D_MODEL), y[Tp:].reshape(Bs, Ls, D_MODEL),
            sd_p, new_conv_p, new_pool_p, sd_s, new_conv_s, new_pool_s)


def kernel(x_prompt, x_sample, state_delta, state_conv, state_pool, w_in, conv_w, a_log, dt_bias, gamma_a,
           w_br_a, w_pool, pool_scale, w_br_b, w_out, ln1_g, ln1_b, w_router, router_bias,
           w_exp_gate, w_exp_up, w_exp_down, w_sh_gate, w_sh_up, w_sh_down, ln2_g, ln2_b):
    depth = w_in.shape[0]
    alpha = (2 * depth) ** 0.25
    yp, ys = x_prompt, x_sample
    outs = [[] for _ in range(6)]
    for l in range(depth):
        res = _layer(yp, ys, state_delta[l], state_conv[l], state_pool[l], w_in[l], conv_w[l], a_log[l],
                     dt_bias[l], gamma_a[l], w_br_a[l], w_pool[l], pool_scale[l], w_br_b[l], w_out[l],
                     ln1_g[l], ln1_b[l], w_router[l], router_bias[l], w_exp_gate[l], w_exp_up[l],
                     w_exp_down[l], w_sh_gate[l], w_sh_up[l], w_sh_down[l], ln2_g[l], ln2_b[l], alpha)
        yp, ys = res[0], res[1]
        for lst, v in zip(outs, res[2:]):
            lst.append(v)
    return (yp, ys) + tuple(jnp.stack(v) for v in outs)
```

```python
import functools
import math

import jax
import jax.numpy as jnp
from jax import lax
from jax.experimental import pallas as pl
from jax.experimental.pallas import tpu as pltpu
from jax.experimental.pallas import tpu_sc as plsc

F32 = jnp.float32
BF16 = jnp.bfloat16

D_MODEL = 1024
H_K = 8
D_K = 128
H_V = 16
D_V = 128
Q_W = H_K * D_K
V_W = H_V * D_V
QKV_W = 2 * Q_W + V_W
CONV_W = 4
POOL_WINDOWS = (2, 4, 8, 16)
POOL_GROUP_W = D_MODEL // len(POOL_WINDOWS)
POOL_BUF = max(POOL_WINDOWS) - 1
N_EXPERTS = 256
TOP_K = 8
N_GROUP = 8
TOPK_GROUP = 4
GROUP_SZ = N_EXPERTS // N_GROUP
D_EXPERT = D_MODEL // 4
ROUTED_SCALE = 2.5
LN_EPS = 1e-5
NORM_EPS = 1e-6
PAST_LEN = 16384

MAIN_W = QKV_W + V_W + 3 * D_MODEL
Z_BLK = QKV_W // V_W
U_BLK = (QKV_W + V_W) // D_MODEL

STACK_ROWS = 128
EXPERT_BM = 256
VMEM_LIMIT = 56 * 1024 * 1024


def _cparams(sem):
    return pltpu.CompilerParams(dimension_semantics=sem, vmem_limit_bytes=VMEM_LIMIT)


def _sigmoid(x):
    return 1.0 / (1.0 + jnp.exp(-x))


def _silu(x):
    return x * _sigmoid(x)


def _softplus(x):
    return jnp.maximum(x, 0.0) + jnp.log(1.0 + jnp.exp(-jnp.abs(x)))


def _dot(a, b):
    return jnp.dot(a.astype(BF16), b.astype(BF16), preferred_element_type=F32)


def _dot_nt(a, b):
    return lax.dot_general(a.astype(BF16), b.astype(BF16), (((1,), (1,)), ((), ())),
                           preferred_element_type=F32)


def _dot_tn(a, b):
    return lax.dot_general(a.astype(BF16), b.astype(BF16), (((0,), (0,)), ((), ())),
                           preferred_element_type=F32)


def _split(a):
    hi = a.astype(BF16)
    lo = (a - hi.astype(F32)).astype(BF16)
    return hi, lo


def _dot3(a, b):
    ah, al = _split(a)
    bh, bl = _split(b)
    d = functools.partial(jnp.dot, preferred_element_type=F32)
    return d(ah, bh) + (d(ah, bl) + d(al, bh))


def _dot3_nt(a, b):
    ah, al = _split(a)
    bh, bl = _split(b)
    d = functools.partial(lax.dot_general, dimension_numbers=(((1,), (1,)), ((), ())),
                          preferred_element_type=F32)
    return d(ah, bh) + (d(ah, bl) + d(al, bh))


def _pack_halves(x):
    n = x.shape[1] // 2
    hi = lax.bitcast_convert_type(x[:, :n].astype(BF16).astype(F32), jnp.uint32)
    lo = lax.bitcast_convert_type(x[:, n:].astype(BF16).astype(F32), jnp.uint32)
    return (hi & jnp.uint32(0xFFFF0000)) | (lo >> 16)


def _unpack_halves(w):
    hi = lax.bitcast_convert_type(w & jnp.uint32(0xFFFF0000), F32)
    lo = lax.bitcast_convert_type(w << 16, F32)
    return hi, lo


N_PIECE = 2
PIECE_W = D_MODEL // (2 * N_PIECE)


def _store_pieces(ref, x):
    for p in range(N_PIECE):
        ref[p] = _pack_halves(x[:, 2 * p * PIECE_W:2 * (p + 1) * PIECE_W])


def _load_pieces(pieces):
    cols = []
    for w in pieces:
        cols.extend(_unpack_halves(w))
    return cols


def _layer_norm(x, g, b):
    mu = jnp.mean(x, axis=-1, keepdims=True)
    xc = x - mu
    var = jnp.mean(xc * xc, axis=-1, keepdims=True)
    return xc * lax.rsqrt(var + LN_EPS) * g + b


def _mm_kernel(x_ref, w_ref, o_ref):
    o_ref[...] = jnp.dot(x_ref[...], w_ref[...], preferred_element_type=F32).astype(o_ref.dtype)


def _matmul(x, w, tm, tn, out_dtype, name):
    t, k = x.shape
    n = w.shape[1]
    return pl.pallas_call(
        _mm_kernel,
        grid=(t // tm, n // tn),
        in_specs=[pl.BlockSpec((tm, k), lambda i, j: (i, 0)),
                  pl.BlockSpec((k, tn), lambda i, j: (0, j))],
        out_specs=pl.BlockSpec((tm, tn), lambda i, j: (i, j)),
        out_shape=jax.ShapeDtypeStruct((t, n), out_dtype),
        compiler_params=_cparams(("parallel", "parallel")),
        name=name,
    )(x, w)


def _gdn_kernel(*refs, C, hg, has_state):
    if has_state:
        (qkv_ref, z_ref, ba_ref, cw_ref, ad_ref, gam_ref, buf_ref, s0_ref,
         o_ref, sout_ref, s_scr, tail_scr, act_scr) = refs
    else:
        (qkv_ref, z_ref, ba_ref, cw_ref, ad_ref, gam_ref,
         o_ref, sout_ref, s_scr, tail_scr, act_scr) = refs
    R = hg * C
    ng = H_V // hg
    log2c = int(math.log2(C))
    c = pl.program_id(1)

    @pl.when(c == 0)
    def _init():
        if has_state:
            s_scr[...] = s0_ref[0]
            tail_scr[...] = buf_ref[0]
        else:
            s_scr[...] = jnp.zeros(s_scr.shape, F32)
            tail_scr[...] = jnp.zeros(tail_scr.shape, F32)

    x = qkv_ref[...]
    xe = jnp.concatenate([tail_scr[...], x], axis=0)
    cw = cw_ref[...]
    y = xe[8:8 + C] * cw[3:4]
    for j in range(CONV_W - 1):
        y = y + xe[5 + j:5 + j + C] * cw[j:j + 1]
    tail_scr[...] = x[C - 8:C]
    act_scr[...] = _silu(y)

    ri = lax.broadcasted_iota(jnp.int32, (R, R), 0)
    ci = lax.broadcasted_iota(jnp.int32, (R, R), 1)
    same = (ri >> log2c) == (ci >> log2c)
    incl = same & (ri >= ci)
    strict = same & (ri > ci)
    eye = ri == ci
    is_last = ci == (((ri >> log2c) << log2c) + (C - 1))
    eye_f = jnp.where(eye, 1.0, 0.0).astype(F32)
    cum_u = jnp.where(same & (ri <= ci), 1.0, 0.0).astype(BF16)

    nrow = ba_ref.shape[2] // 2
    bav = ba_ref[0, 0]
    adv = ad_ref[...]
    beta_rows = _sigmoid(bav[0:nrow])
    g_rows = -jnp.exp(adv[0:nrow]) * _softplus(bav[nrow:2 * nrow] + adv[nrow:2 * nrow])
    g1 = g_rows.astype(BF16)
    r1 = g_rows - g1.astype(F32)
    g2 = r1.astype(BF16)
    g3 = (r1 - g2.astype(F32)).astype(BF16)
    dd = functools.partial(jnp.dot, preferred_element_type=F32)
    G_rows = dd(g1, cum_u) + (dd(g2, cum_u) + dd(g3, cum_u))

    gam = gam_ref[...]
    normed = {}

    def qk_head(kind, kh):
        key = (kind, kh)
        if key not in normed:
            off = (0 if kind == "q" else Q_W) + kh * D_K
            v = act_scr[:, off:off + D_K]
            v = v * lax.rsqrt(jnp.sum(v * v, axis=-1, keepdims=True) + NORM_EPS)
            if kind == "q":
                v = v * (D_K ** -0.5)
            normed[key] = v
        return normed[key]

    rep = H_V // H_K
    grp = []
    for g in range(ng):
        heads = [g * hg + hh for hh in range(hg)]
        qst = jnp.concatenate([qk_head("q", h // rep) for h in heads], axis=0)
        kst = jnp.concatenate([qk_head("k", h // rep) for h in heads], axis=0)
        vst = jnp.concatenate([act_scr[:, 2 * Q_W + h * D_V:2 * Q_W + (h + 1) * D_V] for h in heads], axis=0)
        grow = G_rows[g:g + 1]
        brow = beta_rows[g:g + 1]
        gcol = jnp.sum(eye_f * grow, axis=1, keepdims=True)
        bcol = jnp.sum(eye_f * brow, axis=1, keepdims=True)
        glast = jnp.sum(jnp.where(is_last, grow, 0.0), axis=1, keepdims=True)
        dm = jnp.where(incl, jnp.exp(jnp.minimum(gcol - grow, 0.0)), 0.0)
        eg = jnp.exp(gcol)
        grp.append(dict(heads=heads, qst=qst, kst=kst, gcol=gcol, bcol=bcol, glast=glast, dm=dm, eg=eg,
                        x=jnp.concatenate([vst * bcol, kst * (bcol * eg)], axis=1)))

    for d in grp:
        d["kk"] = _dot_nt(d["kst"], d["kst"])
        d["qk"] = _dot_nt(d["qst"], d["kst"])
    for d in grp:
        d["pw"] = jnp.where(strict, -(d["kk"] * d["bcol"] * d["dm"]), 0.0)
        d["attn"] = jnp.where(incl, d["qk"] * d["dm"], 0.0)

    for r in range(log2c):
        last = r == log2c - 1
        for d in grp:
            if last:
                d["x"] = d["x"] + _dot(d["pw"], d["x"])
            else:
                out = _dot(d["pw"], jnp.concatenate([d["pw"], d["x"]], axis=1))
                d["pw"] = out[:, :R]
                d["x"] = d["x"] + out[:, R:]

    xs = {}
    for d in grp:
        qe = d["qst"] * d["eg"]
        for hh, h in enumerate(d["heads"]):
            sl = slice(hh * C, (hh + 1) * C)
            xs[h] = _dot(jnp.concatenate([d["x"][sl, D_V:], qe[sl]], axis=0), s_scr[h])
    for d in grp:
        d["vnew"] = jnp.concatenate(
            [d["x"][hh * C:(hh + 1) * C, :D_V] - xs[h][:C] for hh, h in enumerate(d["heads"])], axis=0)
        qs = jnp.concatenate([xs[h][C:] for h in d["heads"]], axis=0)
        d["o"] = qs + _dot(d["attn"], d["vnew"])

    row_head = lax.broadcasted_iota(jnp.int32, (R, D_V), 0) >> log2c
    for d in grp:
        kt = d["kst"] * jnp.exp(d["glast"] - d["gcol"])
        egl = jnp.exp(d["glast"])
        for hh, h in enumerate(d["heads"]):
            sl = slice(hh * C, (hh + 1) * C)
            if C >= 16:
                upd = _dot_tn(kt[sl], d["vnew"][sl])
            else:
                upd = _dot_tn(kt, jnp.where(row_head == hh, d["vnew"], 0.0))
            s_scr[h] = s_scr[h] * egl[hh * C:hh * C + 1] + upd

    for d in grp:
        o = d["o"]
        zst = jnp.concatenate([z_ref[:, h * D_V:(h + 1) * D_V] for h in d["heads"]], axis=0)
        on = o * lax.rsqrt(jnp.mean(o * o, axis=-1, keepdims=True) + NORM_EPS) * gam * _silu(zst)
        for hh, h in enumerate(d["heads"]):
            o_ref[:, h * D_V:(h + 1) * D_V] = on[hh * C:(hh + 1) * C].astype(o_ref.dtype)

    @pl.when(c == pl.num_programs(1) - 1)
    def _fin():
        sout_ref[0] = s_scr[...]


def _gdn(proj, ba, conv_w, a_log, dt_bias, gamma_a, row0, B, L, C, buf8=None, s0=None):
    has_state = s0 is not None
    hg = STACK_ROWS // C
    ng = H_V // hg
    nrow = 16
    nc = L // C
    R = STACK_ROWS

    def arrange(v):
        v = v.reshape(B, nc, C, ng, hg).transpose(0, 1, 3, 4, 2).reshape(B, nc, ng, R)
        return jnp.pad(v, ((0, 0), (0, 0), (0, nrow - ng), (0, 0)))

    rows = ba[row0:row0 + B * L]
    ba_arr = jnp.concatenate([arrange(rows[:, :H_V]), arrange(rows[:, H_V:2 * H_V])], axis=2)

    def arrange_param(p):
        v = jnp.broadcast_to(p.reshape(ng, hg, 1), (ng, hg, C)).reshape(ng, R)
        return jnp.pad(v, ((0, nrow - ng), (0, 0)))

    ad = jnp.concatenate([arrange_param(a_log), arrange_param(dt_bias)], axis=0)
    rb0 = row0 // C
    in_specs = [
        pl.BlockSpec((C, QKV_W), lambda b, c: (rb0 + b * nc + c, 0)),
        pl.BlockSpec((C, V_W), lambda b, c: (rb0 + b * nc + c, Z_BLK)),
        pl.BlockSpec((1, 1, 2 * nrow, R), lambda b, c: (b, c, 0, 0)),
        pl.BlockSpec((CONV_W, QKV_W), lambda b, c: (0, 0)),
        pl.BlockSpec((2 * nrow, R), lambda b, c: (0, 0)),
        pl.BlockSpec((1, D_V), lambda b, c: (0, 0)),
    ]
    args = [proj, proj, ba_arr, conv_w, ad, gamma_a.reshape(1, D_V)]
    if has_state:
        in_specs += [pl.BlockSpec((1, 8, QKV_W), lambda b, c: (b, 0, 0)),
                     pl.BlockSpec((1, H_V, D_K, D_V), lambda b, c: (b, 0, 0, 0))]
        args += [buf8, s0]
    o_dtype = BF16 if C % 16 == 0 else F32
    return pl.pallas_call(
        functools.partial(_gdn_kernel, C=C, hg=hg, has_state=has_state),
        grid=(B, nc),
        in_specs=in_specs,
        out_specs=[pl.BlockSpec((C, V_W), lambda b, c: (b * nc + c, 0)),
                   pl.BlockSpec((1, H_V, D_K, D_V), lambda b, c: (b, 0, 0, 0))],
        out_shape=[jax.ShapeDtypeStruct((B * L, V_W), o_dtype),
                   jax.ShapeDtypeStruct((B, H_V, D_K, D_V), F32)],
        scratch_shapes=[pltpu.VMEM((H_V, D_K, D_V), F32),
                        pltpu.VMEM((8, QKV_W), F32),
                        pltpu.VMEM((C, QKV_W), F32)],
        compiler_params=_cparams(("parallel", "arbitrary")),
        name="gdn_state" if has_state else "gdn_fresh",
    )(*args)


def _pool_kernel(*refs, tc, start_pos, has_state):
    if has_state:
        u_ref, buf_ref, o_ref, tail_scr = refs
    else:
        u_ref, o_ref, tail_scr = refs
    hist = POOL_BUF + 1
    c = pl.program_id(1)

    @pl.when(c == 0)
    def _init():
        if has_state:
            tail_scr[...] = buf_ref[0]
        else:
            tail_scr[...] = jnp.zeros(tail_scr.shape, F32)

    u = u_ref[...]
    xe = jnp.concatenate([tail_scr[...], u], axis=0)
    tail_scr[...] = xe[tc:tc + hist]
    pos1 = (start_pos + 1 + c * tc + lax.broadcasted_iota(jnp.int32, (tc, 1), 0)).astype(F32)
    for gi, win in enumerate(POOL_WINDOWS):
        sl = slice(gi * POOL_GROUP_W, (gi + 1) * POOL_GROUP_W)
        s = xe[:, sl]
        shift = 1
        while shift < win:
            s = s + pltpu.roll(s, shift, 0)
            shift *= 2
        cnt = jnp.minimum(float(win), pos1)
        o_ref[:, sl] = s[hist:hist + tc] / cnt - u[:, sl]


def _pool(proj, row0, B, L, tc, start_pos, buf16=None):
    has_state = buf16 is not None
    nc = L // tc
    rb0 = row0 // tc
    in_specs = [pl.BlockSpec((tc, D_MODEL), lambda b, c: (rb0 + b * nc + c, U_BLK))]
    args = [proj]
    if has_state:
        in_specs.append(pl.BlockSpec((1, POOL_BUF + 1, D_MODEL), lambda b, c: (b, 0, 0)))
        args.append(buf16)
    return pl.pallas_call(
        functools.partial(_pool_kernel, tc=tc, start_pos=start_pos, has_state=has_state),
        grid=(B, nc),
        in_specs=in_specs,
        out_specs=pl.BlockSpec((tc, D_MODEL), lambda b, c: (b * nc + c, 0)),
        out_shape=jax.ShapeDtypeStruct((B * L, D_MODEL), F32),
        scratch_shapes=[pltpu.VMEM((POOL_BUF + 1, D_MODEL), F32)],
        compiler_params=_cparams(("parallel", "arbitrary")),
        name="pool_state" if has_state else "pool_fresh",
    )(*args)


def _post_kernel(pooled_ref, o_ref, ga_ref, gb_ref, x_ref, wp_ref, ps_ref, wbb_ref, wba_ref, wo_ref,
                 g_ref, b_ref, h_ref, hp_ref, *, alpha):
    pooled = pooled_ref[...]
    mixed = jnp.concatenate(
        [_dot(pooled[:, gi * POOL_GROUP_W:(gi + 1) * POOL_GROUP_W], wp_ref[gi]) for gi in range(len(POOL_WINDOWS))],
        axis=1) * ps_ref[...]
    branch_b = _dot(mixed, wbb_ref[...])
    branch_a = _dot(o_ref[...], wba_ref[...])
    merged = _sigmoid(ga_ref[...]) * branch_a + _sigmoid(gb_ref[...]) * branch_b
    h = _layer_norm(alpha * x_ref[...] + _dot(merged, wo_ref[...]), g_ref[...], b_ref[...])
    h_ref[...] = h
    _store_pieces(hp_ref, h)


def _post(pooled, o_gated, proj, x, w_pool, pool_scale, w_br_b, w_br_a, w_out, ln_g, ln_b, alpha, tm):
    t = x.shape[0]
    row = lambda i: (i, 0)
    const2 = lambda i: (0, 0)
    return pl.pallas_call(
        functools.partial(_post_kernel, alpha=alpha),
        grid=(t // tm,),
        in_specs=[pl.BlockSpec((tm, D_MODEL), row),
                  pl.BlockSpec((tm, V_W), row),
                  pl.BlockSpec((tm, D_MODEL), lambda i: (i, U_BLK + 1)),
                  pl.BlockSpec((tm, D_MODEL), lambda i: (i, U_BLK + 2)),
                  pl.BlockSpec((tm, D_MODEL), row),
                  pl.BlockSpec(w_pool.shape, lambda i: (0, 0, 0)),
                  pl.BlockSpec((1, D_MODEL), const2),
                  pl.BlockSpec(w_br_b.shape, const2),
                  pl.BlockSpec(w_br_a.shape, const2),
                  pl.BlockSpec(w_out.shape, const2),
                  pl.BlockSpec((1, D_MODEL), const2),
                  pl.BlockSpec((1, D_MODEL), const2)],
        out_specs=[pl.BlockSpec((tm, D_MODEL), row), pl.BlockSpec((N_PIECE, tm, PIECE_W), lambda i: (0, i, 0))],
        out_shape=[jax.ShapeDtypeStruct((t, D_MODEL), F32),
                   jax.ShapeDtypeStruct((N_PIECE, t, PIECE_W), jnp.uint32)],
        compiler_params=_cparams(("parallel",)),
        name="post_mixers",
    )(pooled, o_gated, proj, proj, x, w_pool, pool_scale, w_br_b, w_br_a, w_out, ln_g, ln_b)


def _router_kernel(h_ref, wt_ref, bias_ref, idx_ref, wts_ref, rank_ref, cnt_ref, carry_scr, *, tm):
    i = pl.program_id(0)

    @pl.when(i == 0)
    def _init():
        carry_scr[...] = jnp.zeros(carry_scr.shape, F32)

    logits = _dot3_nt(wt_ref[...], h_ref[...])
    sc = _sigmoid(logits)
    ch = sc + bias_ref[...]
    neg = -jnp.inf
    e_in = lax.broadcasted_iota(jnp.int32, (GROUP_SZ, tm), 0)
    gs_rows = []
    for g in range(N_GROUP):
        blk = ch[g * GROUP_SZ:(g + 1) * GROUP_SZ]
        m1 = jnp.max(blk, axis=0, keepdims=True)
        i1 = jnp.min(jnp.where(blk == m1, e_in, GROUP_SZ), axis=0, keepdims=True)
        m2 = jnp.max(jnp.where(e_in == i1, neg, blk), axis=0, keepdims=True)
        gs_rows.append(m1 + m2)
    gs = jnp.concatenate(gs_rows, axis=0)
    g_io = lax.broadcasted_iota(jnp.int32, (N_GROUP, tm), 0)
    e_io = lax.broadcasted_iota(jnp.int32, (N_EXPERTS, tm), 0)
    e_grp = e_io >> int(math.log2(GROUP_SZ))
    masked = jnp.full((N_EXPERTS, tm), neg, F32)
    for _ in range(TOPK_GROUP):
        mx = jnp.max(gs, axis=0, keepdims=True)
        gi = jnp.min(jnp.where(gs == mx, g_io, N_GROUP), axis=0, keepdims=True)
        gs = jnp.where(g_io == gi, neg, gs)
        masked = jnp.where(e_grp == gi, ch, masked)
    idx_rows, w_rows = [], []
    onehot = jnp.zeros((N_EXPERTS, tm), F32)
    for _ in range(TOP_K):
        mx = jnp.max(masked, axis=0, keepdims=True)
        ei = jnp.min(jnp.where(masked == mx, e_io, N_EXPERTS), axis=0, keepdims=True)
        hit = e_io == ei
        idx_rows.append(ei)
        w_rows.append(jnp.sum(jnp.where(hit, sc, 0.0), axis=0, keepdims=True))
        onehot = jnp.where(hit, 1.0, onehot)
        masked = jnp.where(hit, neg, masked)
    wsel = jnp.concatenate(w_rows, axis=0)
    wts_ref[...] = wsel / jnp.sum(wsel, axis=0, keepdims=True) * ROUTED_SCALE
    idx_ref[...] = jnp.concatenate(idx_rows, axis=0)

    tr = lax.broadcasted_iota(jnp.int32, (tm, tm), 0)
    tc_ = lax.broadcasted_iota(jnp.int32, (tm, tm), 1)
    before = jnp.where(tr < tc_, 1.0, 0.0).astype(BF16)
    cum = jnp.dot(onehot.astype(BF16), before, preferred_element_type=F32) + carry_scr[...]
    rank_ref[...] = jnp.concatenate(
        [jnp.sum(jnp.where(e_io == ei, cum, 0.0), axis=0, keepdims=True) for ei in idx_rows],
        axis=0).astype(jnp.int32)
    carry_scr[...] = carry_scr[...] + jnp.sum(onehot, axis=1, keepdims=True)
    cnt_ref[...] = carry_scr[...]


def _router(h, w_router_t, bias_col, tm):
    t = h.shape[0]
    return pl.pallas_call(
        functools.partial(_router_kernel, tm=tm),
        grid=(t // tm,),
        in_specs=[pl.BlockSpec((tm, D_MODEL), lambda i: (i, 0)),
                  pl.BlockSpec((N_EXPERTS, D_MODEL), lambda i: (0, 0)),
                  pl.BlockSpec((N_EXPERTS, 1), lambda i: (0, 0))],
        out_specs=[pl.BlockSpec((TOP_K, tm), lambda i: (0, i)),
                   pl.BlockSpec((TOP_K, tm), lambda i: (0, i)),
                   pl.BlockSpec((TOP_K, tm), lambda i: (0, i)),
                   pl.BlockSpec((N_EXPERTS, 1), lambda i: (0, 0))],
        out_shape=[jax.ShapeDtypeStruct((TOP_K, t), jnp.int32),
                   jax.ShapeDtypeStruct((TOP_K, t), F32),
                   jax.ShapeDtypeStruct((TOP_K, t), jnp.int32),
                   jax.ShapeDtypeStruct((N_EXPERTS, 1), F32)],
        scratch_shapes=[pltpu.VMEM((N_EXPERTS, 1), F32)],
        compiler_params=_cparams(("arbitrary",)),
        name="router",
    )(h, w_router_t, bias_col)


def _dest_kernel(idx_ref, rank_ref, start_ref, dest_ref, *, tm, n_rows):
    e_io = lax.broadcasted_iota(jnp.int32, (N_EXPERTS, tm), 0)
    starts = start_ref[...]
    rows = []
    for k in range(TOP_K):
        seg = jnp.sum(jnp.where(e_io == idx_ref[k:k + 1, :], starts, 0.0), axis=0, keepdims=True)
        rows.append(seg.astype(jnp.int32) + rank_ref[k:k + 1, :])
    base = jnp.concatenate(rows, axis=0)
    for p in range(N_PIECE):
        dest_ref[p] = base + p * n_rows


def _dest(idx_t, rank_t, seg_start_col, n_rows, tm):
    t = idx_t.shape[1]
    blk = pl.BlockSpec((TOP_K, tm), lambda i: (0, i))
    return pl.pallas_call(
        functools.partial(_dest_kernel, tm=tm, n_rows=n_rows),
        grid=(t // tm,),
        in_specs=[blk, blk, pl.BlockSpec((N_EXPERTS, 1), lambda i: (0, 0))],
        out_specs=pl.BlockSpec((N_PIECE, TOP_K, tm), lambda i: (0, 0, i)),
        out_shape=jax.ShapeDtypeStruct((N_PIECE, TOP_K, t), jnp.int32),
        compiler_params=_cparams(("parallel",)),
        name="dispatch_rows",
    )(idx_t, rank_t, seg_start_col)


def _expert_kernel(be_ref, nu_ref, x_ref, wg_ref, wu_ref, wd_ref, y_ref):
    @pl.when(pl.program_id(0) < nu_ref[0])
    def _():
        x = jnp.concatenate([c.astype(BF16) for c in _load_pieces([x_ref[p] for p in range(N_PIECE)])], axis=1)
        gate = _dot(x, wg_ref[0])
        up = _dot(x, wu_ref[0])
        _store_pieces(y_ref, _dot(_silu(gate) * up, wd_ref[0]))


def _experts(xs, blk_exp, n_used, w_gate, w_up, w_down):
    n_rows = xs.shape[1]
    n_blocks = n_rows // EXPERT_BM

    def live(i, nu):
        return jnp.minimum(i, nu[0] - 1)

    grid_spec = pltpu.PrefetchScalarGridSpec(
        num_scalar_prefetch=2,
        grid=(n_blocks,),
        in_specs=[pl.BlockSpec((N_PIECE, EXPERT_BM, PIECE_W), lambda i, be, nu: (0, live(i, nu), 0)),
                  pl.BlockSpec((1, D_MODEL, D_EXPERT), lambda i, be, nu: (be[live(i, nu)], 0, 0)),
                  pl.BlockSpec((1, D_MODEL, D_EXPERT), lambda i, be, nu: (be[live(i, nu)], 0, 0)),
                  pl.BlockSpec((1, D_EXPERT, D_MODEL), lambda i, be, nu: (be[live(i, nu)], 0, 0))],
        out_specs=pl.BlockSpec((N_PIECE, EXPERT_BM, PIECE_W), lambda i, be, nu: (0, live(i, nu), 0)),
    )
    return pl.pallas_call(
        _expert_kernel,
        grid_spec=grid_spec,
        out_shape=jax.ShapeDtypeStruct((N_PIECE, n_rows, PIECE_W), jnp.uint32),
        compiler_params=_cparams(("arbitrary",)),
        name="experts",
    )(blk_exp, n_used, xs, w_gate, w_up, w_down)


SC_WINDOW = 128


def _sc_mesh():
    return plsc.VectorSubcoreMesh(core_axis_name="core", subcore_axis_name="subcore")


def _sc_scatter_rows(src, dest, n_rows, seg, repeat):
    d = src.shape[1]
    n_idx = dest.shape[0]
    seg_blocks = seg // SC_WINDOW
    dest2 = dest.reshape(1, n_idx)

    def src_block(i):
        return ((i // (repeat * seg_blocks)) * seg_blocks + i % seg_blocks, 0)

    @functools.partial(pl.kernel, out_type=jax.ShapeDtypeStruct((n_rows, d), src.dtype), mesh=_sc_mesh(),
                       scratch_types=[], name="sc_dispatch")
    def run(src_hbm, idx_hbm, out_hbm):
        def body(rows_vmem, idx_vmem):
            pltpu.sync_copy(rows_vmem, out_hbm.at[idx_vmem.at[0]])

        pltpu.emit_pipeline(
            body,
            grid=(n_idx // SC_WINDOW,),
            in_specs=[pl.BlockSpec((SC_WINDOW, d), src_block),
                      pl.BlockSpec((1, SC_WINDOW), lambda i: (0, i))],
            out_specs=[],
            core_axis_name=("core", "subcore"),
            dimension_semantics=(pltpu.PARALLEL,),
        )(src_hbm, idx_hbm)

    return run(src, dest2)


def _sc_gather_rows(table, idx):
    d = table.shape[1]
    n_idx = idx.shape[0]
    idx2 = idx.reshape(1, n_idx)

    @functools.partial(pl.kernel, out_type=jax.ShapeDtypeStruct((n_idx, d), table.dtype), mesh=_sc_mesh(),
                       scratch_types=[], name="sc_combine_gather")
    def run(table_hbm, idx_hbm, out_hbm):
        def body(idx_vmem, rows_vmem):
            pltpu.sync_copy(table_hbm.at[idx_vmem.at[0]], rows_vmem)

        pltpu.emit_pipeline(
            body,
            grid=(n_idx // SC_WINDOW,),
            in_specs=[pl.BlockSpec((1, SC_WINDOW), lambda i: (0, i))],
            out_specs=[pl.BlockSpec((SC_WINDOW, d), lambda i: (i, 0))],
            core_axis_name=("core", "subcore"),
            dimension_semantics=(pltpu.PARALLEL,),
        )(idx_hbm, out_hbm)

    return run(table, idx2)


def _combine_kernel(yg_ref, wts_ref, h_ref, wgu_ref, wd_ref, g_ref, b_ref, out_ref, *, alpha):
    wts = wts_ref[...]
    acc = None
    for k in range(TOP_K):
        cols = _load_pieces([yg_ref[p, k] for p in range(N_PIECE)])
        wk = wts[:, k:k + 1]
        acc = [c * wk for c in cols] if acc is None else [a + c * wk for a, c in zip(acc, cols)]
    routed = jnp.concatenate(acc, axis=1)
    h = h_ref[...]
    gu = _dot(h, wgu_ref[...])
    shared = _dot(_silu(gu[:, :D_EXPERT]) * gu[:, D_EXPERT:], wd_ref[...])
    out_ref[...] = _layer_norm(alpha * h + (routed + shared), g_ref[...], b_ref[...])


def _combine(yg, wts, h, w_sh_gu, w_sh_down, ln_g, ln_b, alpha, tm):
    t = h.shape[0]
    row = lambda i: (i, 0)
    const2 = lambda i: (0, 0)
    return pl.pallas_call(
        functools.partial(_combine_kernel, alpha=alpha),
        grid=(t // tm,),
        in_specs=[pl.BlockSpec((N_PIECE, TOP_K, tm, PIECE_W), lambda i: (0, 0, i, 0)),
                  pl.BlockSpec((tm, TOP_K), row),
                  pl.BlockSpec((tm, D_MODEL), row),
                  pl.BlockSpec(w_sh_gu.shape, const2),
                  pl.BlockSpec(w_sh_down.shape, const2),
                  pl.BlockSpec((1, D_MODEL), const2),
                  pl.BlockSpec((1, D_MODEL), const2)],
        out_specs=pl.BlockSpec((tm, D_MODEL), row),
        out_shape=jax.ShapeDtypeStruct((t, D_MODEL), F32),
        compiler_params=_cparams(("parallel",)),
        name="combine_ln2",
    )(yg, wts, h, w_sh_gu, w_sh_down, ln_g, ln_b)


def _layer(xp, xs, s_delta, s_conv, s_pool, w_in, conv_w, a_log, dt_bias, gamma_a, w_br_a, w_pool,
           pool_scale, w_br_b, w_out, ln1_g, ln1_b, w_router, router_bias, w_exp_gate, w_exp_up,
           w_exp_down, w_sh_gate, w_sh_up, w_sh_down, ln2_g, ln2_b, alpha):
    Bp, Lp, _ = xp.shape
    Bs, Ls, _ = xs.shape
    Tp, Ts = Bp * Lp, Bs * Ls
    T = Tp + Ts
    x = jnp.concatenate([xp.reshape(Tp, D_MODEL), xs.reshape(Ts, D_MODEL)], axis=0)
    xb = x.astype(BF16)

    o_z, o_b, o_a, o_u = QKV_W, QKV_W + V_W, QKV_W + V_W + H_V, QKV_W + V_W + 2 * H_V
    w_main = jnp.concatenate([w_in[:, :o_b], w_in[:, o_u:]], axis=1).astype(BF16)
    w_ba = jnp.pad(w_in[:, o_b:o_u], ((0, 0), (0, 128 - 2 * H_V))).astype(BF16)
    proj = _matmul(xb, w_main, 1024, 512, F32, "in_proj")
    ba = _matmul(xb, w_ba, 1024, 128, F32, "in_proj_beta_a")

    o_p, sd_p = _gdn(proj, ba, conv_w, a_log, dt_bias, gamma_a, 0, Bp, Lp, 64)
    buf8 = jnp.pad(s_conv, ((0, 0), (8 - (CONV_W - 1), 0), (0, 0)))
    o_s, sd_s = _gdn(proj, ba, conv_w, a_log, dt_bias, gamma_a, Tp, Bs, Ls, Ls, buf8=buf8, s0=s_delta)
    o_gated = jnp.concatenate([o_p, o_s.astype(BF16)], axis=0)

    pooled_p = _pool(proj, 0, Bp, Lp, 256, 0)
    buf16 = jnp.pad(s_pool, ((0, 0), (1, 0), (0, 0)))
    pooled_s = _pool(proj, Tp, Bs, Ls, Ls, PAST_LEN, buf16=buf16)
    pooled = jnp.concatenate([pooled_p, pooled_s], axis=0)

    h, hp = _post(pooled, o_gated, proj, x, w_pool.astype(BF16), pool_scale.reshape(1, D_MODEL),
                  w_br_b.astype(BF16), w_br_a.astype(BF16), w_out.astype(BF16),
                  ln1_g.reshape(1, D_MODEL), ln1_b.reshape(1, D_MODEL), alpha, 256)

    idx_t, wts_t, rank_t, cnt = _router(h, w_router.T, router_bias.reshape(N_EXPERTS, 1), 512)
    counts = cnt[:, 0].astype(jnp.int32)
    padded = ((counts + EXPERT_BM - 1) // EXPERT_BM) * EXPERT_BM
    pends = jnp.cumsum(padded)
    pstarts = pends - padded
    n_blocks = (T * TOP_K + N_EXPERTS * (EXPERT_BM - 1) + EXPERT_BM - 1) // EXPERT_BM
    n_rows = n_blocks * EXPERT_BM
    piece_rows = _dest(idx_t, rank_t, pstarts.astype(F32).reshape(N_EXPERTS, 1), n_rows, 512)
    blk_start = jnp.arange(n_blocks, dtype=jnp.int32) * EXPERT_BM
    blk_exp = jnp.minimum(jnp.sum((pends[None, :] <= blk_start[:, None]).astype(jnp.int32), axis=1),
                          N_EXPERTS - 1)
    n_used = (pends[-1:] // EXPERT_BM).astype(jnp.int32)

    piece_idx = piece_rows.reshape(N_PIECE * TOP_K * T)
    x_sorted = _sc_scatter_rows(hp.reshape(N_PIECE * T, PIECE_W), piece_idx, N_PIECE * n_rows, T, TOP_K)
    y_sorted = _experts(x_sorted.reshape(N_PIECE, n_rows, PIECE_W), blk_exp, n_used,
                        w_exp_gate, w_exp_up, w_exp_down)
    yg = _sc_gather_rows(y_sorted.reshape(N_PIECE * n_rows, PIECE_W), piece_idx)
    yg = yg.reshape(N_PIECE, TOP_K, T, PIECE_W)

    w_sh_gu = jnp.concatenate([w_sh_gate, w_sh_up], axis=1).astype(BF16)
    y = _combine(yg, wts_t.T, h, w_sh_gu, w_sh_down.astype(BF16),
                 ln2_g.reshape(1, D_MODEL), ln2_b.reshape(1, D_MODEL), alpha, 256)

    def last_rows(group_row0, B, L, n, col0, width):
        rows = group_row0 + jnp.arange(B, dtype=jnp.int32)[:, None] * L + (L - n) + jnp.arange(n, dtype=jnp.int32)
        return jnp.take(proj, rows.reshape(-1), axis=0)[:, col0:col0 + width].reshape(B, n, width)

    nc_ = CONV_W - 1
    new_conv_p = last_rows(0, Bp, Lp, nc_, 0, QKV_W)
    new_pool_p = last_rows(0, Bp, Lp, POOL_BUF, o_b, D_MODEL)
    qkv_s = proj[Tp:, :QKV_W].reshape(Bs, Ls, QKV_W)
    u_s = proj[Tp:, o_b:o_b + D_MODEL].reshape(Bs, Ls, D_MODEL)
    new_conv_s = jnp.concatenate([s_conv, qkv_s], axis=1)[:, -nc_:]
    new_pool_s = jnp.concatenate([s_pool, u_s], axis=1)[:, -POOL_BUF:]
    return (y[:Tp].reshape(Bp, Lp, D_MODEL), y[Tp:].reshape(Bs, Ls, D_MODEL),
            sd_p, new_conv_p, new_pool_p, sd_s, new_conv_s, new_pool_s)


def kernel(x_prompt, x_sample, state_delta, state_conv, state_pool, w_in, conv_w, a_log, dt_bias, gamma_a,
           w_br_a, w_pool, pool_scale, w_br_b, w_out, ln1_g, ln1_b, w_router, router_bias,
           w_exp_gate, w_exp_up, w_exp_down, w_sh_gate, w_sh_up, w_sh_down, ln2_g, ln2_b):
    depth = w_in.shape[0]
    alpha = (2 * depth) ** 0.25
    yp, ys = x_prompt, x_sample
    outs = [[] for _ in range(6)]
    for l in range(depth):
        res = _layer(yp, ys, state_delta[l], state_conv[l], state_pool[l], w_in[l], conv_w[l], a_log[l],
                     dt_bias[l], gamma_a[l], w_br_a[l], w_pool[l], pool_scale[l], w_br_b[l], w_out[l],
                     ln1_g[l], ln1_b[l], w_router[l], router_bias[l], w_exp_gate[l], w_exp_up[l],
                     w_exp_down[l], w_sh_gate[l], w_sh_up[l], w_sh_down[l], ln2_g[l], ln2_b[l], alpha)
        yp, ys = res[0], res[1]
        for lst, v in zip(outs, res[2:]):
            lst.append(v)
    return (yp, ys) + tuple(jnp.stack(v) for v in outs)
```

```python
import functools
import math

import jax
import jax.numpy as jnp
from jax import lax
from jax.experimental import pallas as pl
from jax.experimental.pallas import tpu as pltpu
from jax.experimental.pallas import tpu_sc as plsc

F32 = jnp.float32
BF16 = jnp.bfloat16

D_MODEL = 1024
H_K = 8
D_K = 128
H_V = 16
D_V = 128
Q_W = H_K * D_K
V_W = H_V * D_V
QKV_W = 2 * Q_W + V_W
CONV_W = 4
POOL_WINDOWS = (2, 4, 8, 16)
POOL_GROUP_W = D_MODEL // len(POOL_WINDOWS)
POOL_BUF = max(POOL_WINDOWS) - 1
N_EXPERTS = 256
TOP_K = 8
N_GROUP = 8
TOPK_GROUP = 4
GROUP_SZ = N_EXPERTS // N_GROUP
D_EXPERT = D_MODEL // 4
ROUTED_SCALE = 2.5
LN_EPS = 1e-5
NORM_EPS = 1e-6
PAST_LEN = 16384

MAIN_W = QKV_W + V_W + 3 * D_MODEL
Z_BLK = QKV_W // V_W
U_BLK = (QKV_W + V_W) // D_MODEL

STACK_ROWS = 128
SOLVE_BASE = 16
EXPERT_BM = 256
VMEM_LIMIT = 56 * 1024 * 1024


def _cparams(sem):
    return pltpu.CompilerParams(dimension_semantics=sem, vmem_limit_bytes=VMEM_LIMIT)


def _sigmoid(x):
    return 1.0 / (1.0 + jnp.exp(-x))


def _silu(x):
    return x * _sigmoid(x)


def _softplus(x):
    return jnp.maximum(x, 0.0) + jnp.log(1.0 + jnp.exp(-jnp.abs(x)))


def _dot(a, b):
    return jnp.dot(a.astype(BF16), b.astype(BF16), preferred_element_type=F32)


def _dot_nt(a, b):
    return lax.dot_general(a.astype(BF16), b.astype(BF16), (((1,), (1,)), ((), ())),
                           preferred_element_type=F32)


def _dot_tn(a, b):
    return lax.dot_general(a.astype(BF16), b.astype(BF16), (((0,), (0,)), ((), ())),
                           preferred_element_type=F32)


def _split(a):
    hi = a.astype(BF16)
    lo = (a - hi.astype(F32)).astype(BF16)
    return hi, lo


def _dot3_nt(a, b):
    ah, al = _split(a)
    bh, bl = _split(b)
    d = functools.partial(lax.dot_general, dimension_numbers=(((1,), (1,)), ((), ())),
                          preferred_element_type=F32)
    return d(ah, bh) + (d(ah, bl) + d(al, bh))


def _pack_halves(x):
    n = x.shape[1] // 2
    hi = lax.bitcast_convert_type(x[:, :n].astype(BF16).astype(F32), jnp.uint32)
    lo = lax.bitcast_convert_type(x[:, n:].astype(BF16).astype(F32), jnp.uint32)
    return (hi & jnp.uint32(0xFFFF0000)) | (lo >> 16)


def _unpack_halves(w):
    hi = lax.bitcast_convert_type(w & jnp.uint32(0xFFFF0000), F32)
    lo = lax.bitcast_convert_type(w << 16, F32)
    return hi, lo


N_PIECE = 2
PIECE_W = D_MODEL // (2 * N_PIECE)


def _store_pieces(ref, x):
    for p in range(N_PIECE):
        ref[p] = _pack_halves(x[:, 2 * p * PIECE_W:2 * (p + 1) * PIECE_W])


def _load_pieces(pieces):
    cols = []
    for w in pieces:
        cols.extend(_unpack_halves(w))
    return cols


def _layer_norm(x, g, b):
    mu = jnp.mean(x, axis=-1, keepdims=True)
    xc = x - mu
    var = jnp.mean(xc * xc, axis=-1, keepdims=True)
    return xc * lax.rsqrt(var + LN_EPS) * g + b


def _mm_kernel(x_ref, w_ref, o_ref):
    o_ref[...] = jnp.dot(x_ref[...], w_ref[...], preferred_element_type=F32).astype(o_ref.dtype)


def _matmul(x, w, tm, tn, out_dtype, name):
    t, k = x.shape
    n = w.shape[1]
    return pl.pallas_call(
        _mm_kernel,
        grid=(t // tm, n // tn),
        in_specs=[pl.BlockSpec((tm, k), lambda i, j: (i, 0)),
                  pl.BlockSpec((k, tn), lambda i, j: (0, j))],
        out_specs=pl.BlockSpec((tm, tn), lambda i, j: (i, j)),
        out_shape=jax.ShapeDtypeStruct((t, n), out_dtype),
        compiler_params=_cparams(("parallel", "parallel")),
        name=name,
    )(x, w)


def _gdn_kernel(*refs, C, hg, has_state):
    if has_state:
        (qkv_ref, z_ref, ba_ref, cw_ref, ad_ref, gam_ref, buf_ref, s0_ref,
         o_ref, sout_ref, s_scr, tail_scr, act_scr) = refs
    else:
        (qkv_ref, z_ref, ba_ref, cw_ref, ad_ref, gam_ref,
         o_ref, sout_ref, s_scr, tail_scr, act_scr) = refs
    R = hg * C
    ng = H_V // hg
    log2c = int(math.log2(C))
    c = pl.program_id(1)

    @pl.when(c == 0)
    def _init():
        if has_state:
            s_scr[...] = s0_ref[0]
            tail_scr[...] = buf_ref[0]
        else:
            s_scr[...] = jnp.zeros(s_scr.shape, F32)
            tail_scr[...] = jnp.zeros(tail_scr.shape, F32)

    x = qkv_ref[...]
    xe = jnp.concatenate([tail_scr[...], x], axis=0)
    cw = cw_ref[...]
    y = xe[8:8 + C] * cw[3:4]
    for j in range(CONV_W - 1):
        y = y + xe[5 + j:5 + j + C] * cw[j:j + 1]
    tail_scr[...] = x[C - 8:C]
    act_scr[...] = _silu(y)

    ri = lax.broadcasted_iota(jnp.int32, (R, R), 0)
    ci = lax.broadcasted_iota(jnp.int32, (R, R), 1)
    same = (ri >> log2c) == (ci >> log2c)
    incl = same & (ri >= ci)
    strict = same & (ri > ci)
    eye = ri == ci
    is_last = ci == (((ri >> log2c) << log2c) + (C - 1))
    eye_f = jnp.where(eye, 1.0, 0.0).astype(F32)
    cum_u = jnp.where(same & (ri <= ci), 1.0, 0.0).astype(BF16)

    nrow = ba_ref.shape[2] // 2
    bav = ba_ref[0, 0]
    adv = ad_ref[...]
    beta_rows = _sigmoid(bav[0:nrow])
    g_rows = -jnp.exp(adv[0:nrow]) * _softplus(bav[nrow:2 * nrow] + adv[nrow:2 * nrow])
    g1 = g_rows.astype(BF16)
    r1 = g_rows - g1.astype(F32)
    g2 = r1.astype(BF16)
    g3 = (r1 - g2.astype(F32)).astype(BF16)
    dd = functools.partial(jnp.dot, preferred_element_type=F32)
    G_rows = dd(g1, cum_u) + (dd(g2, cum_u) + dd(g3, cum_u))

    gam = gam_ref[...]
    normed = {}

    def qk_head(kind, kh):
        key = (kind, kh)
        if key not in normed:
            off = (0 if kind == "q" else Q_W) + kh * D_K
            v = act_scr[:, off:off + D_K]
            v = v * lax.rsqrt(jnp.sum(v * v, axis=-1, keepdims=True) + NORM_EPS)
            if kind == "q":
                v = v * (D_K ** -0.5)
            normed[key] = v
        return normed[key]

    rep = H_V // H_K
    grp = []
    for g in range(ng):
        heads = [g * hg + hh for hh in range(hg)]
        qst = jnp.concatenate([qk_head("q", h // rep) for h in heads], axis=0)
        kst = jnp.concatenate([qk_head("k", h // rep) for h in heads], axis=0)
        vst = jnp.concatenate([act_scr[:, 2 * Q_W + h * D_V:2 * Q_W + (h + 1) * D_V] for h in heads], axis=0)
        grow = G_rows[g:g + 1]
        brow = beta_rows[g:g + 1]
        gcol = jnp.sum(eye_f * grow, axis=1, keepdims=True)
        bcol = jnp.sum(eye_f * brow, axis=1, keepdims=True)
        glast = jnp.sum(jnp.where(is_last, grow, 0.0), axis=1, keepdims=True)
        dm = jnp.where(incl, jnp.exp(jnp.minimum(gcol - grow, 0.0)), 0.0)
        eg = jnp.exp(gcol)
        grp.append(dict(heads=heads, qst=qst, kst=kst, gcol=gcol, bcol=bcol, glast=glast, dm=dm, eg=eg,
                        x=jnp.concatenate([vst * bcol, kst * (bcol * eg)], axis=1)))

    for d in grp:
        d["kk"] = _dot_nt(d["kst"], d["kst"])
        d["qk"] = _dot_nt(d["qst"], d["kst"])
    log2b = min(log2c, int(math.log2(SOLVE_BASE)))
    same_base = ((ri ^ ci) >> log2b) == 0
    for d in grp:
        d["m"] = jnp.where(strict, -(d["kk"] * d["bcol"] * d["dm"]), 0.0)
        d["attn"] = jnp.where(incl, d["qk"] * d["dm"], 0.0)
        d["pw"] = jnp.where(same_base, d["m"], 0.0)
        d["t"] = eye_f + d["pw"]

    for r in range(log2b):
        last = r == log2b - 1
        for d in grp:
            if r == 0:
                if not last:
                    d["pw"] = _dot(d["pw"], d["pw"])
            elif last:
                d["t"] = d["t"] + _dot(d["pw"], d["t"])
            else:
                out = _dot(d["pw"], jnp.concatenate([d["pw"], d["t"]], axis=1))
                d["pw"] = out[:, :R]
                d["t"] = d["t"] + out[:, R:]
    for lev in range(log2b + 1, log2c + 1):
        coupling = ((ri ^ ci) >> (lev - 1)) == 1
        for d in grp:
            d["y"] = _dot(d["t"], jnp.where(coupling, d["m"], 0.0))
        for d in grp:
            d["t"] = d["t"] + _dot(d["y"], d["t"])
    for d in grp:
        d["x"] = _dot(d["t"], d["x"])

    xs = {}
    for d in grp:
        qe = d["qst"] * d["eg"]
        for hh, h in enumerate(d["heads"]):
            sl = slice(hh * C, (hh + 1) * C)
            xs[h] = _dot(jnp.concatenate([d["x"][sl, D_V:], qe[sl]], axis=0), s_scr[h])
    for d in grp:
        d["vnew"] = jnp.concatenate(
            [d["x"][hh * C:(hh + 1) * C, :D_V] - xs[h][:C] for hh, h in enumerate(d["heads"])], axis=0)
        qs = jnp.concatenate([xs[h][C:] for h in d["heads"]], axis=0)
        d["o"] = qs + _dot(d["attn"], d["vnew"])

    row_head = lax.broadcasted_iota(jnp.int32, (R, D_V), 0) >> log2c
    for d in grp:
        kt = d["kst"] * jnp.exp(d["glast"] - d["gcol"])
        egl = jnp.exp(d["glast"])
        for hh, h in enumerate(d["heads"]):
            sl = slice(hh * C, (hh + 1) * C)
            if C >= 16:
                upd = _dot_tn(kt[sl], d["vnew"][sl])
            else:
                upd = _dot_tn(kt, jnp.where(row_head == hh, d["vnew"], 0.0))
            s_scr[h] = s_scr[h] * egl[hh * C:hh * C + 1] + upd

    for d in grp:
        o = d["o"]
        zst = jnp.concatenate([z_ref[:, h * D_V:(h + 1) * D_V] for h in d["heads"]], axis=0)
        on = o * lax.rsqrt(jnp.mean(o * o, axis=-1, keepdims=True) + NORM_EPS) * gam * _silu(zst)
        for hh, h in enumerate(d["heads"]):
            o_ref[:, h * D_V:(h + 1) * D_V] = on[hh * C:(hh + 1) * C].astype(o_ref.dtype)

    @pl.when(c == pl.num_programs(1) - 1)
    def _fin():
        sout_ref[0] = s_scr[...]


def _gdn(proj, ba, conv_w, a_log, dt_bias, gamma_a, row0, B, L, C, buf8=None, s0=None):
    has_state = s0 is not None
    hg = STACK_ROWS // C
    ng = H_V // hg
    nrow = 16
    nc = L // C
    R = STACK_ROWS

    def arrange(v):
        v = v.reshape(B, nc, C, ng, hg).transpose(0, 1, 3, 4, 2).reshape(B, nc, ng, R)
        return jnp.pad(v, ((0, 0), (0, 0), (0, nrow - ng), (0, 0)))

    rows = ba[row0:row0 + B * L]
    ba_arr = jnp.concatenate([arrange(rows[:, :H_V]), arrange(rows[:, H_V:2 * H_V])], axis=2)

    def arrange_param(p):
        v = jnp.broadcast_to(p.reshape(ng, hg, 1), (ng, hg, C)).reshape(ng, R)
        return jnp.pad(v, ((0, nrow - ng), (0, 0)))

    ad = jnp.concatenate([arrange_param(a_log), arrange_param(dt_bias)], axis=0)
    rb0 = row0 // C
    in_specs = [
        pl.BlockSpec((C, QKV_W), lambda b, c: (rb0 + b * nc + c, 0)),
        pl.BlockSpec((C, V_W), lambda b, c: (rb0 + b * nc + c, Z_BLK)),
        pl.BlockSpec((1, 1, 2 * nrow, R), lambda b, c: (b, c, 0, 0)),
        pl.BlockSpec((CONV_W, QKV_W), lambda b, c: (0, 0)),
        pl.BlockSpec((2 * nrow, R), lambda b, c: (0, 0)),
        pl.BlockSpec((1, D_V), lambda b, c: (0, 0)),
    ]
    args = [proj, proj, ba_arr, conv_w, ad, gamma_a.reshape(1, D_V)]
    if has_state:
        in_specs += [pl.BlockSpec((1, 8, QKV_W), lambda b, c: (b, 0, 0)),
                     pl.BlockSpec((1, H_V, D_K, D_V), lambda b, c: (b, 0, 0, 0))]
        args += [buf8, s0]
    o_dtype = BF16 if C % 16 == 0 else F32
    return pl.pallas_call(
        functools.partial(_gdn_kernel, C=C, hg=hg, has_state=has_state),
        grid=(B, nc),
        in_specs=in_specs,
        out_specs=[pl.BlockSpec((C, V_W), lambda b, c: (b * nc + c, 0)),
                   pl.BlockSpec((1, H_V, D_K, D_V), lambda b, c: (b, 0, 0, 0))],
        out_shape=[jax.ShapeDtypeStruct((B * L, V_W), o_dtype),
                   jax.ShapeDtypeStruct((B, H_V, D_K, D_V), F32)],
        scratch_shapes=[pltpu.VMEM((H_V, D_K, D_V), F32),
                        pltpu.VMEM((8, QKV_W), F32),
                        pltpu.VMEM((C, QKV_W), F32)],
        compiler_params=_cparams(("parallel", "arbitrary")),
        name="gdn_state" if has_state else "gdn_fresh",
    )(*args)


def _pool_kernel(*refs, tc, start_pos, has_state):
    if has_state:
        u_ref, buf_ref, o_ref, tail_scr = refs
    else:
        u_ref, o_ref, tail_scr = refs
    hist = POOL_BUF + 1
    c = pl.program_id(1)

    @pl.when(c == 0)
    def _init():
        if has_state:
            tail_scr[...] = buf_ref[0]
        else:
            tail_scr[...] = jnp.zeros(tail_scr.shape, F32)

    u = u_ref[...]
    xe = jnp.concatenate([tail_scr[...], u], axis=0)
    tail_scr[...] = xe[tc:tc + hist]
    pos1 = (start_pos + 1 + c * tc + lax.broadcasted_iota(jnp.int32, (tc, 1), 0)).astype(F32)
    for gi, win in enumerate(POOL_WINDOWS):
        sl = slice(gi * POOL_GROUP_W, (gi + 1) * POOL_GROUP_W)
        s = xe[:, sl]
        shift = 1
        while shift < win:
            s = s + pltpu.roll(s, shift, 0)
            shift *= 2
        cnt = jnp.minimum(float(win), pos1)
        o_ref[:, sl] = s[hist:hist + tc] / cnt - u[:, sl]


def _pool(proj, row0, B, L, tc, start_pos, buf16=None):
    has_state = buf16 is not None
    nc = L // tc
    rb0 = row0 // tc
    in_specs = [pl.BlockSpec((tc, D_MODEL), lambda b, c: (rb0 + b * nc + c, U_BLK))]
    args = [proj]
    if has_state:
        in_specs.append(pl.BlockSpec((1, POOL_BUF + 1, D_MODEL), lambda b, c: (b, 0, 0)))
        args.append(buf16)
    return pl.pallas_call(
        functools.partial(_pool_kernel, tc=tc, start_pos=start_pos, has_state=has_state),
        grid=(B, nc),
        in_specs=in_specs,
        out_specs=pl.BlockSpec((tc, D_MODEL), lambda b, c: (b * nc + c, 0)),
        out_shape=jax.ShapeDtypeStruct((B * L, D_MODEL), F32),
        scratch_shapes=[pltpu.VMEM((POOL_BUF + 1, D_MODEL), F32)],
        compiler_params=_cparams(("parallel", "arbitrary")),
        name="pool_state" if has_state else "pool_fresh",
    )(*args)


def _post_kernel(pooled_ref, o_ref, ga_ref, gb_ref, x_ref, wp_ref, ps_ref, wbb_ref, wba_ref, wo_ref,
                 g_ref, b_ref, h_ref, hp_ref, *, alpha):
    pooled = pooled_ref[...]
    mixed = jnp.concatenate(
        [_dot(pooled[:, gi * POOL_GROUP_W:(gi + 1) * POOL_GROUP_W], wp_ref[gi]) for gi in range(len(POOL_WINDOWS))],
        axis=1) * ps_ref[...]
    branch_b = _dot(mixed, wbb_ref[...])
    branch_a = _dot(o_ref[...], wba_ref[...])
    merged = _sigmoid(ga_ref[...]) * branch_a + _sigmoid(gb_ref[...]) * branch_b
    h = _layer_norm(alpha * x_ref[...] + _dot(merged, wo_ref[...]), g_ref[...], b_ref[...])
    h_ref[...] = h
    _store_pieces(hp_ref, h)


def _post(pooled, o_gated, proj, x, w_pool, pool_scale, w_br_b, w_br_a, w_out, ln_g, ln_b, alpha, tm):
    t = x.shape[0]
    row = lambda i: (i, 0)
    const2 = lambda i: (0, 0)
    return pl.pallas_call(
        functools.partial(_post_kernel, alpha=alpha),
        grid=(t // tm,),
        in_specs=[pl.BlockSpec((tm, D_MODEL), row),
                  pl.BlockSpec((tm, V_W), row),
                  pl.BlockSpec((tm, D_MODEL), lambda i: (i, U_BLK + 1)),
                  pl.BlockSpec((tm, D_MODEL), lambda i: (i, U_BLK + 2)),
                  pl.BlockSpec((tm, D_MODEL), row),
                  pl.BlockSpec(w_pool.shape, lambda i: (0, 0, 0)),
                  pl.BlockSpec((1, D_MODEL), const2),
                  pl.BlockSpec(w_br_b.shape, const2),
                  pl.BlockSpec(w_br_a.shape, const2),
                  pl.BlockSpec(w_out.shape, const2),
                  pl.BlockSpec((1, D_MODEL), const2),
                  pl.BlockSpec((1, D_MODEL), const2)],
        out_specs=[pl.BlockSpec((tm, D_MODEL), row), pl.BlockSpec((N_PIECE, tm, PIECE_W), lambda i: (0, i, 0))],
        out_shape=[jax.ShapeDtypeStruct((t, D_MODEL), F32),
                   jax.ShapeDtypeStruct((N_PIECE, t, PIECE_W), jnp.uint32)],
        compiler_params=_cparams(("parallel",)),
        name="post_mixers",
    )(pooled, o_gated, proj, proj, x, w_pool, pool_scale, w_br_b, w_br_a, w_out, ln_g, ln_b)


def _router_kernel(h_ref, wt_ref, bias_ref, idx_ref, wts_ref, rank_ref, cnt_ref, carry_scr, *, tm):
    i = pl.program_id(0)

    @pl.when(i == 0)
    def _init():
        carry_scr[...] = jnp.zeros(carry_scr.shape, F32)

    logits = _dot3_nt(wt_ref[...], h_ref[...])
    sc = _sigmoid(logits)
    ch = sc + bias_ref[...]
    neg = -jnp.inf
    e_in = lax.broadcasted_iota(jnp.int32, (GROUP_SZ, tm), 0)
    gs_rows = []
    for g in range(N_GROUP):
        blk = ch[g * GROUP_SZ:(g + 1) * GROUP_SZ]
        m1 = jnp.max(blk, axis=0, keepdims=True)
        i1 = jnp.min(jnp.where(blk == m1, e_in, GROUP_SZ), axis=0, keepdims=True)
        m2 = jnp.max(jnp.where(e_in == i1, neg, blk), axis=0, keepdims=True)
        gs_rows.append(m1 + m2)
    gs = jnp.concatenate(gs_rows, axis=0)
    g_io = lax.broadcasted_iota(jnp.int32, (N_GROUP, tm), 0)
    e_io = lax.broadcasted_iota(jnp.int32, (N_EXPERTS, tm), 0)
    e_grp = e_io >> int(math.log2(GROUP_SZ))
    masked = jnp.full((N_EXPERTS, tm), neg, F32)
    for _ in range(TOPK_GROUP):
        mx = jnp.max(gs, axis=0, keepdims=True)
        gi = jnp.min(jnp.where(gs == mx, g_io, N_GROUP), axis=0, keepdims=True)
        gs = jnp.where(g_io == gi, neg, gs)
        masked = jnp.where(e_grp == gi, ch, masked)
    idx_rows, w_rows = [], []
    onehot = jnp.zeros((N_EXPERTS, tm), F32)
    for _ in range(TOP_K):
        mx = jnp.max(masked, axis=0, keepdims=True)
        ei = jnp.min(jnp.where(masked == mx, e_io, N_EXPERTS), axis=0, keepdims=True)
        hit = e_io == ei
        idx_rows.append(ei)
        w_rows.append(jnp.sum(jnp.where(hit, sc, 0.0), axis=0, keepdims=True))
        onehot = jnp.where(hit, 1.0, onehot)
        masked = jnp.where(hit, neg, masked)
    wsel = jnp.concatenate(w_rows, axis=0)
    wts_ref[...] = wsel / jnp.sum(wsel, axis=0, keepdims=True) * ROUTED_SCALE
    idx_ref[...] = jnp.concatenate(idx_rows, axis=0)

    tr = lax.broadcasted_iota(jnp.int32, (tm, tm), 0)
    tc_ = lax.broadcasted_iota(jnp.int32, (tm, tm), 1)
    before = jnp.where(tr < tc_, 1.0, 0.0).astype(BF16)
    cum = jnp.dot(onehot.astype(BF16), before, preferred_element_type=F32) + carry_scr[...]
    rank_ref[...] = jnp.concatenate(
        [jnp.sum(jnp.where(e_io == ei, cum, 0.0), axis=0, keepdims=True) for ei in idx_rows],
        axis=0).astype(jnp.int32)
    carry_scr[...] = carry_scr[...] + jnp.sum(onehot, axis=1, keepdims=True)
    cnt_ref[...] = carry_scr[...]


def _router(h, w_router_t, bias_col, tm):
    t = h.shape[0]
    return pl.pallas_call(
        functools.partial(_router_kernel, tm=tm),
        grid=(t // tm,),
        in_specs=[pl.BlockSpec((tm, D_MODEL), lambda i: (i, 0)),
                  pl.BlockSpec((N_EXPERTS, D_MODEL), lambda i: (0, 0)),
                  pl.BlockSpec((N_EXPERTS, 1), lambda i: (0, 0))],
        out_specs=[pl.BlockSpec((TOP_K, tm), lambda i: (0, i)),
                   pl.BlockSpec((TOP_K, tm), lambda i: (0, i)),
                   pl.BlockSpec((TOP_K, tm), lambda i: (0, i)),
                   pl.BlockSpec((N_EXPERTS, 1), lambda i: (0, 0))],
        out_shape=[jax.ShapeDtypeStruct((TOP_K, t), jnp.int32),
                   jax.ShapeDtypeStruct((TOP_K, t), F32),
                   jax.ShapeDtypeStruct((TOP_K, t), jnp.int32),
                   jax.ShapeDtypeStruct((N_EXPERTS, 1), F32)],
        scratch_shapes=[pltpu.VMEM((N_EXPERTS, 1), F32)],
        compiler_params=_cparams(("arbitrary",)),
        name="router",
    )(h, w_router_t, bias_col)


def _dest_kernel(idx_ref, rank_ref, start_ref, dest_ref, *, tm, n_rows):
    e_io = lax.broadcasted_iota(jnp.int32, (N_EXPERTS, tm), 0)
    starts = start_ref[...]
    rows = []
    for k in range(TOP_K):
        seg = jnp.sum(jnp.where(e_io == idx_ref[k:k + 1, :], starts, 0.0), axis=0, keepdims=True)
        rows.append(seg.astype(jnp.int32) + rank_ref[k:k + 1, :])
    base = jnp.concatenate(rows, axis=0)
    for p in range(N_PIECE):
        dest_ref[p] = base + p * n_rows


def _dest(idx_t, rank_t, seg_start_col, n_rows, tm):
    t = idx_t.shape[1]
    blk = pl.BlockSpec((TOP_K, tm), lambda i: (0, i))
    return pl.pallas_call(
        functools.partial(_dest_kernel, tm=tm, n_rows=n_rows),
        grid=(t // tm,),
        in_specs=[blk, blk, pl.BlockSpec((N_EXPERTS, 1), lambda i: (0, 0))],
        out_specs=pl.BlockSpec((N_PIECE, TOP_K, tm), lambda i: (0, 0, i)),
        out_shape=jax.ShapeDtypeStruct((N_PIECE, TOP_K, t), jnp.int32),
        compiler_params=_cparams(("parallel",)),
        name="dispatch_rows",
    )(idx_t, rank_t, seg_start_col)


def _expert_kernel(be_ref, nu_ref, x_ref, wg_ref, wu_ref, wd_ref, y_ref, wgu_scr, wd_scr):
    i = pl.program_id(0)
    live = i < nu_ref[0]
    new_expert = (i == 0) | (be_ref[i] != be_ref[jnp.maximum(i - 1, 0)])

    @pl.when(live & new_expert)
    def _cache_weights():
        wgu_scr[:, :D_EXPERT] = wg_ref[0].astype(BF16)
        wgu_scr[:, D_EXPERT:] = wu_ref[0].astype(BF16)
        wd_scr[...] = wd_ref[0].astype(BF16)

    @pl.when(live)
    def _():
        n_sub = 2
        sub = EXPERT_BM // n_sub
        xs = [jnp.concatenate([c.astype(BF16) for c in
                               _load_pieces([x_ref[p, s * sub:(s + 1) * sub] for p in range(N_PIECE)])], axis=1)
              for s in range(n_sub)]
        gus = [jnp.dot(x, wgu_scr[...], preferred_element_type=F32) for x in xs]
        acts = [(_silu(gu[:, :D_EXPERT]) * gu[:, D_EXPERT:]).astype(BF16) for gu in gus]
        ys = [jnp.dot(a, wd_scr[...], preferred_element_type=F32) for a in acts]
        for s, y in enumerate(ys):
            for p in range(N_PIECE):
                y_ref[p, s * sub:(s + 1) * sub] = _pack_halves(y[:, 2 * p * PIECE_W:2 * (p + 1) * PIECE_W])


def _experts(xs, blk_exp, n_used, w_gate, w_up, w_down):
    n_rows = xs.shape[1]
    n_blocks = n_rows // EXPERT_BM

    def live(i, nu):
        return jnp.minimum(i, nu[0] - 1)

    grid_spec = pltpu.PrefetchScalarGridSpec(
        num_scalar_prefetch=2,
        grid=(n_blocks,),
        in_specs=[pl.BlockSpec((N_PIECE, EXPERT_BM, PIECE_W), lambda i, be, nu: (0, live(i, nu), 0)),
                  pl.BlockSpec((1, D_MODEL, D_EXPERT), lambda i, be, nu: (be[live(i, nu)], 0, 0)),
                  pl.BlockSpec((1, D_MODEL, D_EXPERT), lambda i, be, nu: (be[live(i, nu)], 0, 0)),
                  pl.BlockSpec((1, D_EXPERT, D_MODEL), lambda i, be, nu: (be[live(i, nu)], 0, 0))],
        out_specs=pl.BlockSpec((N_PIECE, EXPERT_BM, PIECE_W), lambda i, be, nu: (0, live(i, nu), 0)),
        scratch_shapes=[pltpu.VMEM((D_MODEL, 2 * D_EXPERT), BF16), pltpu.VMEM((D_EXPERT, D_MODEL), BF16)],
    )
    return pl.pallas_call(
        _expert_kernel,
        grid_spec=grid_spec,
        out_shape=jax.ShapeDtypeStruct((N_PIECE, n_rows, PIECE_W), jnp.uint32),
        compiler_params=_cparams(("arbitrary",)),
        name="experts",
    )(blk_exp, n_used, xs, w_gate, w_up, w_down)


SC_WINDOW = 128


def _sc_mesh():
    return plsc.VectorSubcoreMesh(core_axis_name="core", subcore_axis_name="subcore")


def _sc_scatter_rows(src, dest, n_rows, seg, repeat):
    d = src.shape[1]
    n_idx = dest.shape[0]
    seg_blocks = seg // SC_WINDOW
    dest2 = dest.reshape(1, n_idx)

    def src_block(i):
        return ((i // (repeat * seg_blocks)) * seg_blocks + i % seg_blocks, 0)

    @functools.partial(pl.kernel, out_type=jax.ShapeDtypeStruct((n_rows, d), src.dtype), mesh=_sc_mesh(),
                       scratch_types=[], name="sc_dispatch")
    def run(src_hbm, idx_hbm, out_hbm):
        def body(rows_vmem, idx_vmem):
            pltpu.sync_copy(rows_vmem, out_hbm.at[idx_vmem.at[0]])

        pltpu.emit_pipeline(
            body,
            grid=(n_idx // SC_WINDOW,),
            in_specs=[pl.BlockSpec((SC_WINDOW, d), src_block),
                      pl.BlockSpec((1, SC_WINDOW), lambda i: (0, i))],
            out_specs=[],
            core_axis_name=("core", "subcore"),
            dimension_semantics=(pltpu.PARALLEL,),
        )(src_hbm, idx_hbm)

    return run(src, dest2)


def _sc_gather_rows(table, idx):
    d = table.shape[1]
    n_idx = idx.shape[0]
    idx2 = idx.reshape(1, n_idx)

    @functools.partial(pl.kernel, out_type=jax.ShapeDtypeStruct((n_idx, d), table.dtype), mesh=_sc_mesh(),
                       scratch_types=[], name="sc_combine_gather")
    def run(table_hbm, idx_hbm, out_hbm):
        def body(idx_vmem, rows_vmem):
            pltpu.sync_copy(table_hbm.at[idx_vmem.at[0]], rows_vmem)

        pltpu.emit_pipeline(
            body,
            grid=(n_idx // SC_WINDOW,),
            in_specs=[pl.BlockSpec((1, SC_WINDOW), lambda i: (0, i))],
            out_specs=[pl.BlockSpec((SC_WINDOW, d), lambda i: (i, 0))],
            core_axis_name=("core", "subcore"),
            dimension_semantics=(pltpu.PARALLEL,),
        )(idx_hbm, out_hbm)

    return run(table, idx2)


def _combine_kernel(yg_ref, wts_ref, h_ref, wgu_ref, wd_ref, g_ref, b_ref, out_ref, *, alpha):
    wts = wts_ref[...]
    acc = None
    for k in range(TOP_K):
        cols = _load_pieces([yg_ref[p, k] for p in range(N_PIECE)])
        wk = wts[:, k:k + 1]
        acc = [c * wk for c in cols] if acc is None else [a + c * wk for a, c in zip(acc, cols)]
    routed = jnp.concatenate(acc, axis=1)
    h = h_ref[...]
    gu = _dot(h, wgu_ref[...])
    shared = _dot(_silu(gu[:, :D_EXPERT]) * gu[:, D_EXPERT:], wd_ref[...])
    out_ref[...] = _layer_norm(alpha * h + (routed + shared), g_ref[...], b_ref[...])


def _combine(yg, wts, h, w_sh_gu, w_sh_down, ln_g, ln_b, alpha, tm):
    t = h.shape[0]
    row = lambda i: (i, 0)
    const2 = lambda i: (0, 0)
    return pl.pallas_call(
        functools.partial(_combine_kernel, alpha=alpha),
        grid=(t // tm,),
        in_specs=[pl.BlockSpec((N_PIECE, TOP_K, tm, PIECE_W), lambda i: (0, 0, i, 0)),
                  pl.BlockSpec((tm, TOP_K), row),
                  pl.BlockSpec((tm, D_MODEL), row),
                  pl.BlockSpec(w_sh_gu.shape, const2),
                  pl.BlockSpec(w_sh_down.shape, const2),
                  pl.BlockSpec((1, D_MODEL), const2),
                  pl.BlockSpec((1, D_MODEL), const2)],
        out_specs=pl.BlockSpec((tm, D_MODEL), row),
        out_shape=jax.ShapeDtypeStruct((t, D_MODEL), F32),
        compiler_params=_cparams(("parallel",)),
        name="combine_ln2",
    )(yg, wts, h, w_sh_gu, w_sh_down, ln_g, ln_b)


def _layer(xp, xs, s_delta, s_conv, s_pool, w_in, conv_w, a_log, dt_bias, gamma_a, w_br_a, w_pool,
           pool_scale, w_br_b, w_out, ln1_g, ln1_b, w_router, router_bias, w_exp_gate, w_exp_up,
           w_exp_down, w_sh_gate, w_sh_up, w_sh_down, ln2_g, ln2_b, alpha):
    Bp, Lp, _ = xp.shape
    Bs, Ls, _ = xs.shape
    Tp, Ts = Bp * Lp, Bs * Ls
    T = Tp + Ts
    x = jnp.concatenate([xp.reshape(Tp, D_MODEL), xs.reshape(Ts, D_MODEL)], axis=0)
    xb = x.astype(BF16)

    o_z, o_b, o_a, o_u = QKV_W, QKV_W + V_W, QKV_W + V_W + H_V, QKV_W + V_W + 2 * H_V
    w_main = jnp.concatenate([w_in[:, :o_b], w_in[:, o_u:]], axis=1).astype(BF16)
    w_ba = jnp.pad(w_in[:, o_b:o_u], ((0, 0), (0, 128 - 2 * H_V))).astype(BF16)
    proj = _matmul(xb, w_main, 1024, 512, F32, "in_proj")
    ba = _matmul(xb, w_ba, 1024, 128, F32, "in_proj_beta_a")

    o_p, sd_p = _gdn(proj, ba, conv_w, a_log, dt_bias, gamma_a, 0, Bp, Lp, 64)
    buf8 = jnp.pad(s_conv, ((0, 0), (8 - (CONV_W - 1), 0), (0, 0)))
    o_s, sd_s = _gdn(proj, ba, conv_w, a_log, dt_bias, gamma_a, Tp, Bs, Ls, Ls, buf8=buf8, s0=s_delta)
    o_gated = jnp.concatenate([o_p, o_s.astype(BF16)], axis=0)

    pooled_p = _pool(proj, 0, Bp, Lp, 256, 0)
    buf16 = jnp.pad(s_pool, ((0, 0), (1, 0), (0, 0)))
    pooled_s = _pool(proj, Tp, Bs, Ls, Ls, PAST_LEN, buf16=buf16)
    pooled = jnp.concatenate([pooled_p, pooled_s], axis=0)

    h, hp = _post(pooled, o_gated, proj, x, w_pool.astype(BF16), pool_scale.reshape(1, D_MODEL),
                  w_br_b.astype(BF16), w_br_a.astype(BF16), w_out.astype(BF16),
                  ln1_g.reshape(1, D_MODEL), ln1_b.reshape(1, D_MODEL), alpha, 256)

    idx_t, wts_t, rank_t, cnt = _router(h, w_router.T, router_bias.reshape(N_EXPERTS, 1), 512)
    counts = cnt[:, 0].astype(jnp.int32)
    padded = ((counts + EXPERT_BM - 1) // EXPERT_BM) * EXPERT_BM
    pends = jnp.cumsum(padded)
    pstarts = pends - padded
    n_blocks = (T * TOP_K + N_EXPERTS * (EXPERT_BM - 1) + EXPERT_BM - 1) // EXPERT_BM
    n_rows = n_blocks * EXPERT_BM
    piece_rows = _dest(idx_t, rank_t, pstarts.astype(F32).reshape(N_EXPERTS, 1), n_rows, 512)
    blk_start = jnp.arange(n_blocks, dtype=jnp.int32) * EXPERT_BM
    blk_exp = jnp.minimum(jnp.sum((pends[None, :] <= blk_start[:, None]).astype(jnp.int32), axis=1),
                          N_EXPERTS - 1)
    n_used = (pends[-1:] // EXPERT_BM).astype(jnp.int32)

    piece_idx = piece_rows.reshape(N_PIECE * TOP_K * T)
    x_sorted = _sc_scatter_rows(hp.reshape(N_PIECE * T, PIECE_W), piece_idx, N_PIECE * n_rows, T, TOP_K)
    y_sorted = _experts(x_sorted.reshape(N_PIECE, n_rows, PIECE_W), blk_exp, n_used,
                        w_exp_gate, w_exp_up, w_exp_down)
    yg = _sc_gather_rows(y_sorted.reshape(N_PIECE * n_rows, PIECE_W), piece_idx)
    yg = yg.reshape(N_PIECE, TOP_K, T, PIECE_W)

    w_sh_gu = jnp.concatenate([w_sh_gate, w_sh_up], axis=1).astype(BF16)
    y = _combine(yg, wts_t.T, h, w_sh_gu, w_sh_down.astype(BF16),
                 ln2_g.reshape(1, D_MODEL), ln2_b.reshape(1, D_MODEL), alpha, 256)

    def last_rows(group_row0, B, L, n, col0, width):
        rows = group_row0 + jnp.arange(B, dtype=jnp.int32)[:, None] * L + (L - n) + jnp.arange(n, dtype=jnp.int32)
        return jnp.take(proj, rows.reshape(-1), axis=0)[:, col0:col0 + width].reshape(B, n, width)

    nc_ = CONV_W - 1
    new_conv_p = last_rows(0, Bp, Lp, nc_, 0, QKV_W)
    new_pool_p = last_rows(0, Bp, Lp, POOL_BUF, o_b, D_MODEL)
    qkv_s = proj[Tp:, :QKV_W].reshape(Bs, Ls, QKV_W)
    u_s = proj[Tp:, o_b:o_b + D_MODEL].reshape(Bs, Ls, D_MODEL)
    new_conv_s = jnp.concatenate([s_conv, qkv_s], axis=1)[:, -nc_:]
    new_pool_s = jnp.concatenate([s_pool, u_s], axis=1)[:, -POOL_BUF:]
    return (y[:Tp].reshape(Bp, Lp, D_MODEL), y[Tp:].reshape(Bs, Ls, D_MODEL),
            sd_p, new_conv_p, new_pool_p, sd_s, new_conv_s, new_pool_s)


def kernel(x_prompt, x_sample, state_delta, state_conv, state_pool, w_in, conv_w, a_log, dt_bias, gamma_a,
           w_br_a, w_pool, pool_scale, w_br_b, w_out, ln1_g, ln1_b, w_router, router_bias,
           w_exp_gate, w_exp_up, w_exp_down, w_sh_gate, w_sh_up, w_sh_down, ln2_g, ln2_b):
    depth = w_in.shape[0]
    alpha = (2 * depth) ** 0.25
    yp, ys = x_prompt, x_sample
    outs = [[] for _ in range(6)]
    for l in range(depth):
        res = _layer(yp, ys, state_delta[l], state_conv[l], state_pool[l], w_in[l], conv_w[l], a_log[l],
                     dt_bias[l], gamma_a[l], w_br_a[l], w_pool[l], pool_scale[l], w_br_b[l], w_out[l],
                     ln1_g[l], ln1_b[l], w_router[l], router_bias[l], w_exp_gate[l], w_exp_up[l],
                     w_exp_down[l], w_sh_gate[l], w_sh_up[l], w_sh_down[l], ln2_g[l], ln2_b[l], alpha)
        yp, ys = res[0], res[1]
        for lst, v in zip(outs, res[2:]):
            lst.append(v)
    return (yp, ys) + tuple(jnp.stack(v) for v in outs)
```

```python
import functools
import math

import jax
import jax.numpy as jnp
from jax import lax
from jax.experimental import pallas as pl
from jax.experimental.pallas import tpu as pltpu
from jax.experimental.pallas import tpu_sc as plsc

F32 = jnp.float32
BF16 = jnp.bfloat16

D_MODEL = 1024
H_K = 8
D_K = 128
H_V = 16
D_V = 128
Q_W = H_K * D_K
V_W = H_V * D_V
QKV_W = 2 * Q_W + V_W
CONV_W = 4
POOL_WINDOWS = (2, 4, 8, 16)
POOL_GROUP_W = D_MODEL // len(POOL_WINDOWS)
POOL_BUF = max(POOL_WINDOWS) - 1
N_EXPERTS = 256
TOP_K = 8
N_GROUP = 8
TOPK_GROUP = 4
GROUP_SZ = N_EXPERTS // N_GROUP
D_EXPERT = D_MODEL // 4
ROUTED_SCALE = 2.5
LN_EPS = 1e-5
NORM_EPS = 1e-6
PAST_LEN = 16384

MAIN_W = QKV_W + V_W + 3 * D_MODEL
Z_BLK = QKV_W // V_W
U_BLK = (QKV_W + V_W) // D_MODEL

STACK_ROWS = 128
SOLVE_BASE = 16
EXPERT_BM = 512
GDN_STATE_SEQS = 4
VMEM_LIMIT = 56 * 1024 * 1024


def _cparams(sem):
    return pltpu.CompilerParams(dimension_semantics=sem, vmem_limit_bytes=VMEM_LIMIT)


def _sigmoid(x):
    return 1.0 / (1.0 + jnp.exp(-x))


def _silu(x):
    return x * _sigmoid(x)


def _softplus(x):
    return jnp.maximum(x, 0.0) + jnp.log(1.0 + jnp.exp(-jnp.abs(x)))


def _dot(a, b):
    return jnp.dot(a.astype(BF16), b.astype(BF16), preferred_element_type=F32)


def _dot_nt(a, b):
    return lax.dot_general(a.astype(BF16), b.astype(BF16), (((1,), (1,)), ((), ())),
                           preferred_element_type=F32)


def _dot_tn(a, b):
    return lax.dot_general(a.astype(BF16), b.astype(BF16), (((0,), (0,)), ((), ())),
                           preferred_element_type=F32)


def _split(a):
    hi = a.astype(BF16)
    lo = (a - hi.astype(F32)).astype(BF16)
    return hi, lo


def _dot3_nt(a, b):
    ah, al = _split(a)
    bh, bl = _split(b)
    d = functools.partial(lax.dot_general, dimension_numbers=(((1,), (1,)), ((), ())),
                          preferred_element_type=F32)
    return d(ah, bh) + (d(ah, bl) + d(al, bh))


def _pack_halves(x):
    n = x.shape[1] // 2
    hi = lax.bitcast_convert_type(x[:, :n].astype(BF16).astype(F32), jnp.uint32)
    lo = lax.bitcast_convert_type(x[:, n:].astype(BF16).astype(F32), jnp.uint32)
    return (hi & jnp.uint32(0xFFFF0000)) | (lo >> 16)


def _unpack_halves(w):
    hi = lax.bitcast_convert_type(w & jnp.uint32(0xFFFF0000), F32)
    lo = lax.bitcast_convert_type(w << 16, F32)
    return hi, lo


N_PIECE = 2
PIECE_W = D_MODEL // (2 * N_PIECE)


def _store_pieces(ref, x):
    for p in range(N_PIECE):
        ref[p] = _pack_halves(x[:, 2 * p * PIECE_W:2 * (p + 1) * PIECE_W])


def _load_pieces(pieces):
    cols = []
    for w in pieces:
        cols.extend(_unpack_halves(w))
    return cols


def _layer_norm(x, g, b):
    mu = jnp.mean(x, axis=-1, keepdims=True)
    xc = x - mu
    var = jnp.mean(xc * xc, axis=-1, keepdims=True)
    return xc * lax.rsqrt(var + LN_EPS) * g + b


def _mm_kernel(x_ref, w_ref, o_ref):
    o_ref[...] = jnp.dot(x_ref[...], w_ref[...], preferred_element_type=F32).astype(o_ref.dtype)


def _matmul(x, w, tm, tn, out_dtype, name):
    t, k = x.shape
    n = w.shape[1]
    return pl.pallas_call(
        _mm_kernel,
        grid=(t // tm, n // tn),
        in_specs=[pl.BlockSpec((tm, k), lambda i, j: (i, 0)),
                  pl.BlockSpec((k, tn), lambda i, j: (0, j))],
        out_specs=pl.BlockSpec((tm, tn), lambda i, j: (i, j)),
        out_shape=jax.ShapeDtypeStruct((t, n), out_dtype),
        compiler_params=_cparams(("parallel", "parallel")),
        name=name,
    )(x, w)


def _gdn_kernel(*refs, C, hg, nseq, has_state):
    if has_state:
        (qkv_ref, z_ref, ba_ref, cw_ref, ad_ref, gam_ref, buf_ref, s0_ref, _,
         o_ref, sout_ref, s_scr, tail_scr, act_scr) = refs
    else:
        (qkv_ref, z_ref, ba_ref, cw_ref, ad_ref, gam_ref,
         o_ref, sout_ref, s_scr, tail_scr, act_scr) = refs
    R = hg * C
    ng = H_V // hg
    log2c = int(math.log2(C))
    c = pl.program_id(1)

    @pl.when(c == 0)
    def _init():
        if has_state:
            for s in range(nseq):
                s_scr[s * H_V:(s + 1) * H_V] = s0_ref[s]
                tail_scr[s * 8:(s + 1) * 8] = buf_ref[s]
        else:
            s_scr[...] = jnp.zeros(s_scr.shape, F32)
            tail_scr[...] = jnp.zeros(tail_scr.shape, F32)

    cw = cw_ref[...]
    for s in range(nseq):
        x = qkv_ref[s * C:(s + 1) * C]
        xe = jnp.concatenate([tail_scr[s * 8:(s + 1) * 8], x], axis=0)
        y = xe[8:8 + C] * cw[3:4]
        for j in range(CONV_W - 1):
            y = y + xe[5 + j:5 + j + C] * cw[j:j + 1]
        tail_scr[s * 8:(s + 1) * 8] = x[C - 8:C]
        act_scr[s * C:(s + 1) * C] = _silu(y)

    ri = lax.broadcasted_iota(jnp.int32, (R, R), 0)
    ci = lax.broadcasted_iota(jnp.int32, (R, R), 1)
    same = (ri >> log2c) == (ci >> log2c)
    incl = same & (ri >= ci)
    strict = same & (ri > ci)
    eye = ri == ci
    is_last = ci == (((ri >> log2c) << log2c) + (C - 1))
    eye_f = jnp.where(eye, 1.0, 0.0).astype(F32)
    cum_u = jnp.where(same & (ri <= ci), 1.0, 0.0).astype(BF16)

    nrow = ba_ref.shape[2] // 2
    adv = ad_ref[...]
    dd = functools.partial(jnp.dot, preferred_element_type=F32)
    beta_seq, G_seq = [], []
    for s in range(nseq):
        bav = ba_ref[s, 0]
        beta_seq.append(_sigmoid(bav[0:nrow]))
        g_rows = -jnp.exp(adv[0:nrow]) * _softplus(bav[nrow:2 * nrow] + adv[nrow:2 * nrow])
        g1 = g_rows.astype(BF16)
        r1 = g_rows - g1.astype(F32)
        g2 = r1.astype(BF16)
        g3 = (r1 - g2.astype(F32)).astype(BF16)
        G_seq.append(dd(g1, cum_u) + (dd(g2, cum_u) + dd(g3, cum_u)))

    gam = gam_ref[...]
    normed = {}

    def qk_head(s, kind, kh):
        key = (s, kind, kh)
        if key not in normed:
            off = (0 if kind == "q" else Q_W) + kh * D_K
            v = act_scr[s * C:(s + 1) * C, off:off + D_K]
            v = v * lax.rsqrt(jnp.sum(v * v, axis=-1, keepdims=True) + NORM_EPS)
            if kind == "q":
                v = v * (D_K ** -0.5)
            normed[key] = v
        return normed[key]

    rep = H_V // H_K
    grp = []
    for s in range(nseq):
        rows = slice(s * C, (s + 1) * C)
        for g in range(ng):
            heads = [g * hg + hh for hh in range(hg)]
            qst = jnp.concatenate([qk_head(s, "q", h // rep) for h in heads], axis=0)
            kst = jnp.concatenate([qk_head(s, "k", h // rep) for h in heads], axis=0)
            vst = jnp.concatenate([act_scr[rows, 2 * Q_W + h * D_V:2 * Q_W + (h + 1) * D_V] for h in heads], axis=0)
            grow = G_seq[s][g:g + 1]
            brow = beta_seq[s][g:g + 1]
            gcol = jnp.sum(eye_f * grow, axis=1, keepdims=True)
            bcol = jnp.sum(eye_f * brow, axis=1, keepdims=True)
            glast = jnp.sum(jnp.where(is_last, grow, 0.0), axis=1, keepdims=True)
            dm = jnp.where(incl, jnp.exp(jnp.minimum(gcol - grow, 0.0)), 0.0)
            eg = jnp.exp(gcol)
            grp.append(dict(rows=rows, s0=s * H_V, heads=heads, qst=qst, kst=kst, gcol=gcol, bcol=bcol,
                            glast=glast, dm=dm, eg=eg,
                            x=jnp.concatenate([vst * bcol, kst * (bcol * eg)], axis=1)))

    for d in grp:
        d["kk"] = _dot_nt(d["kst"], d["kst"])
        d["qk"] = _dot_nt(d["qst"], d["kst"])
    log2b = min(log2c, int(math.log2(SOLVE_BASE)))
    same_base = ((ri ^ ci) >> log2b) == 0
    for d in grp:
        d["m"] = jnp.where(strict, -(d["kk"] * d["bcol"] * d["dm"]), 0.0)
        d["attn"] = jnp.where(incl, d["qk"] * d["dm"], 0.0)
        d["pw"] = jnp.where(same_base, d["m"], 0.0)
        d["t"] = eye_f + d["pw"]

    for r in range(log2b):
        last = r == log2b - 1
        for d in grp:
            if r == 0:
                if not last:
                    d["pw"] = _dot(d["pw"], d["pw"])
            elif last:
                d["t"] = d["t"] + _dot(d["pw"], d["t"])
            else:
                out = _dot(d["pw"], jnp.concatenate([d["pw"], d["t"]], axis=1))
                d["pw"] = out[:, :R]
                d["t"] = d["t"] + out[:, R:]
    for lev in range(log2b + 1, log2c + 1):
        coupling = ((ri ^ ci) >> (lev - 1)) == 1
        for d in grp:
            d["y"] = _dot(d["t"], jnp.where(coupling, d["m"], 0.0))
        for d in grp:
            d["t"] = d["t"] + _dot(d["y"], d["t"])
    for d in grp:
        d["x"] = _dot(d["t"], d["x"])

    for d in grp:
        qe = d["qst"] * d["eg"]
        d["xs"] = [_dot(jnp.concatenate([d["x"][hh * C:(hh + 1) * C, D_V:], qe[hh * C:(hh + 1) * C]], axis=0),
                        s_scr[d["s0"] + h]) for hh, h in enumerate(d["heads"])]
    for d in grp:
        d["vnew"] = jnp.concatenate(
            [d["x"][hh * C:(hh + 1) * C, :D_V] - xs[:C] for hh, xs in enumerate(d["xs"])], axis=0)
        qs = jnp.concatenate([xs[C:] for xs in d["xs"]], axis=0)
        d["o"] = qs + _dot(d["attn"], d["vnew"])

    row_head = lax.broadcasted_iota(jnp.int32, (R, D_V), 0) >> log2c
    for d in grp:
        kt = d["kst"] * jnp.exp(d["glast"] - d["gcol"])
        egl = jnp.exp(d["glast"])
        for hh, h in enumerate(d["heads"]):
            sl = slice(hh * C, (hh + 1) * C)
            if C >= 16:
                upd = _dot_tn(kt[sl], d["vnew"][sl])
            else:
                upd = _dot_tn(kt, jnp.where(row_head == hh, d["vnew"], 0.0))
            s_scr[d["s0"] + h] = s_scr[d["s0"] + h] * egl[hh * C:hh * C + 1] + upd

    for d in grp:
        o = d["o"]
        zst = jnp.concatenate([z_ref[d["rows"], h * D_V:(h + 1) * D_V] for h in d["heads"]], axis=0)
        on = o * lax.rsqrt(jnp.mean(o * o, axis=-1, keepdims=True) + NORM_EPS) * gam * _silu(zst)
        for hh, h in enumerate(d["heads"]):
            o_ref[d["rows"], h * D_V:(h + 1) * D_V] = on[hh * C:(hh + 1) * C].astype(o_ref.dtype)

    @pl.when(c == pl.num_programs(1) - 1)
    def _fin():
        for s in range(nseq):
            sout_ref[s] = s_scr[s * H_V:(s + 1) * H_V]


def _gdn(proj, ba, conv_w, a_log, dt_bias, gamma_a, row0, B, L, C, nseq, buf8=None, s0=None, o_all=None):
    has_state = s0 is not None
    hg = STACK_ROWS // C
    ng = H_V // hg
    nrow = 16
    nc = L // C
    R = STACK_ROWS
    assert nseq == 1 or nc == 1
    rows_blk = nseq * C

    def arrange(v):
        v = v.reshape(B, nc, C, ng, hg).transpose(0, 1, 3, 4, 2).reshape(B, nc, ng, R)
        return jnp.pad(v, ((0, 0), (0, 0), (0, nrow - ng), (0, 0)))

    rows = ba[row0:row0 + B * L]
    ba_arr = jnp.concatenate([arrange(rows[:, :H_V]), arrange(rows[:, H_V:2 * H_V])], axis=2)

    def arrange_param(p):
        v = jnp.broadcast_to(p.reshape(ng, hg, 1), (ng, hg, C)).reshape(ng, R)
        return jnp.pad(v, ((0, nrow - ng), (0, 0)))

    ad = jnp.concatenate([arrange_param(a_log), arrange_param(dt_bias)], axis=0)
    rb0 = row0 // rows_blk
    in_specs = [
        pl.BlockSpec((rows_blk, QKV_W), lambda b, c: (rb0 + b * nc + c, 0)),
        pl.BlockSpec((rows_blk, V_W), lambda b, c: (rb0 + b * nc + c, Z_BLK)),
        pl.BlockSpec((nseq, 1, 2 * nrow, R), lambda b, c: (b, c, 0, 0)),
        pl.BlockSpec((CONV_W, QKV_W), lambda b, c: (0, 0)),
        pl.BlockSpec((2 * nrow, R), lambda b, c: (0, 0)),
        pl.BlockSpec((1, D_V), lambda b, c: (0, 0)),
    ]
    args = [proj, proj, ba_arr, conv_w, ad, gamma_a.reshape(1, D_V)]
    aliases = {}
    if has_state:
        in_specs += [pl.BlockSpec((nseq, 8, QKV_W), lambda b, c: (b, 0, 0)),
                     pl.BlockSpec((nseq, H_V, D_K, D_V), lambda b, c: (b, 0, 0, 0)),
                     pl.BlockSpec(memory_space=pl.ANY)]
        args += [buf8, s0, o_all]
        aliases = {len(args) - 1: 0}
    return pl.pallas_call(
        functools.partial(_gdn_kernel, C=C, hg=hg, nseq=nseq, has_state=has_state),
        grid=(B // nseq, nc),
        in_specs=in_specs,
        out_specs=[pl.BlockSpec((rows_blk, V_W), lambda b, c: (rb0 + b * nc + c, 0)),
                   pl.BlockSpec((nseq, H_V, D_K, D_V), lambda b, c: (b, 0, 0, 0))],
        out_shape=[jax.ShapeDtypeStruct((proj.shape[0], V_W), BF16),
                   jax.ShapeDtypeStruct((B, H_V, D_K, D_V), F32)],
        scratch_shapes=[pltpu.VMEM((nseq * H_V, D_K, D_V), F32),
                        pltpu.VMEM((nseq * 8, QKV_W), F32),
                        pltpu.VMEM((rows_blk, QKV_W), F32)],
        input_output_aliases=aliases,
        compiler_params=_cparams(("parallel", "arbitrary")),
        name="gdn_state" if has_state else "gdn_fresh",
    )(*args)


def _pool_kernel(*refs, tc, start_pos, has_state):
    if has_state:
        u_ref, buf_ref, _, o_ref, tail_scr = refs
    else:
        u_ref, o_ref, tail_scr = refs
    hist = POOL_BUF + 1
    c = pl.program_id(1)

    @pl.when(c == 0)
    def _init():
        if has_state:
            tail_scr[...] = buf_ref[0]
        else:
            tail_scr[...] = jnp.zeros(tail_scr.shape, F32)

    u = u_ref[...]
    xe = jnp.concatenate([tail_scr[...], u], axis=0)
    tail_scr[...] = xe[tc:tc + hist]
    pos1 = (start_pos + 1 + c * tc + lax.broadcasted_iota(jnp.int32, (tc, 1), 0)).astype(F32)
    for gi, win in enumerate(POOL_WINDOWS):
        sl = slice(gi * POOL_GROUP_W, (gi + 1) * POOL_GROUP_W)
        s = xe[:, sl]
        shift = 1
        while shift < win:
            s = s + pltpu.roll(s, shift, 0)
            shift *= 2
        cnt = jnp.minimum(float(win), pos1)
        o_ref[:, sl] = s[hist:hist + tc] / cnt - u[:, sl]


def _pool(proj, row0, B, L, tc, start_pos, buf16=None, pooled_all=None):
    has_state = buf16 is not None
    nc = L // tc
    rb0 = row0 // tc
    in_specs = [pl.BlockSpec((tc, D_MODEL), lambda b, c: (rb0 + b * nc + c, U_BLK))]
    args = [proj]
    aliases = {}
    if has_state:
        in_specs += [pl.BlockSpec((1, POOL_BUF + 1, D_MODEL), lambda b, c: (b, 0, 0)),
                     pl.BlockSpec(memory_space=pl.ANY)]
        args += [buf16, pooled_all]
        aliases = {len(args) - 1: 0}
    return pl.pallas_call(
        functools.partial(_pool_kernel, tc=tc, start_pos=start_pos, has_state=has_state),
        grid=(B, nc),
        in_specs=in_specs,
        out_specs=pl.BlockSpec((tc, D_MODEL), lambda b, c: (rb0 + b * nc + c, 0)),
        out_shape=jax.ShapeDtypeStruct((proj.shape[0], D_MODEL), F32),
        scratch_shapes=[pltpu.VMEM((POOL_BUF + 1, D_MODEL), F32)],
        input_output_aliases=aliases,
        compiler_params=_cparams(("parallel", "arbitrary")),
        name="pool_state" if has_state else "pool_fresh",
    )(*args)


def _post_kernel(pooled_ref, o_ref, ga_ref, gb_ref, xa_ref, xb_ref, wp_ref, ps_ref, wbb_ref, wba_ref, wo_ref,
                 g_ref, b_ref, h_ref, hp_ref, *, alpha, tiles_a):
    x = jnp.where(pl.program_id(0) < tiles_a, xa_ref[...], xb_ref[...])
    pooled = pooled_ref[...]
    mixed = jnp.concatenate(
        [_dot(pooled[:, gi * POOL_GROUP_W:(gi + 1) * POOL_GROUP_W], wp_ref[gi]) for gi in range(len(POOL_WINDOWS))],
        axis=1) * ps_ref[...]
    branch_b = _dot(mixed, wbb_ref[...])
    branch_a = _dot(o_ref[...], wba_ref[...])
    merged = _sigmoid(ga_ref[...]) * branch_a + _sigmoid(gb_ref[...]) * branch_b
    h = _layer_norm(alpha * x + _dot(merged, wo_ref[...]), g_ref[...], b_ref[...])
    h_ref[...] = h
    _store_pieces(hp_ref, h)


def _post(pooled, o_gated, proj, x_a, x_b, w_pool, pool_scale, w_br_b, w_br_a, w_out, ln_g, ln_b, alpha, tm):
    t = x_a.shape[0] + x_b.shape[0]
    tiles_a = x_a.shape[0] // tm
    row = lambda i: (i, 0)
    const2 = lambda i: (0, 0)
    return pl.pallas_call(
        functools.partial(_post_kernel, alpha=alpha, tiles_a=tiles_a),
        grid=(t // tm,),
        in_specs=[pl.BlockSpec((tm, D_MODEL), row),
                  pl.BlockSpec((tm, V_W), row),
                  pl.BlockSpec((tm, D_MODEL), lambda i: (i, U_BLK + 1)),
                  pl.BlockSpec((tm, D_MODEL), lambda i: (i, U_BLK + 2)),
                  pl.BlockSpec((tm, D_MODEL), lambda i: (jnp.minimum(i, tiles_a - 1), 0)),
                  pl.BlockSpec((tm, D_MODEL), lambda i: (jnp.maximum(i - tiles_a, 0), 0)),
                  pl.BlockSpec(w_pool.shape, lambda i: (0, 0, 0)),
                  pl.BlockSpec((1, D_MODEL), const2),
                  pl.BlockSpec(w_br_b.shape, const2),
                  pl.BlockSpec(w_br_a.shape, const2),
                  pl.BlockSpec(w_out.shape, const2),
                  pl.BlockSpec((1, D_MODEL), const2),
                  pl.BlockSpec((1, D_MODEL), const2)],
        out_specs=[pl.BlockSpec((tm, D_MODEL), row), pl.BlockSpec((N_PIECE, tm, PIECE_W), lambda i: (0, i, 0))],
        out_shape=[jax.ShapeDtypeStruct((t, D_MODEL), F32),
                   jax.ShapeDtypeStruct((N_PIECE, t, PIECE_W), jnp.uint32)],
        compiler_params=_cparams(("parallel",)),
        name="post_mixers",
    )(pooled, o_gated, proj, proj, x_a, x_b, w_pool, pool_scale, w_br_b, w_br_a, w_out, ln_g, ln_b)


def _router_kernel(h_ref, wt_ref, bias_ref, idx_ref, wts_ref, rank_ref, cnt_ref, carry_scr, *, tm):
    i = pl.program_id(0)

    @pl.when(i == 0)
    def _init():
        carry_scr[...] = jnp.zeros(carry_scr.shape, F32)

    logits = _dot3_nt(wt_ref[...], h_ref[...])
    sc = _sigmoid(logits)
    ch = sc + bias_ref[...]
    neg = -jnp.inf
    e_in = lax.broadcasted_iota(jnp.int32, (GROUP_SZ, tm), 0)
    gs_rows = []
    for g in range(N_GROUP):
        blk = ch[g * GROUP_SZ:(g + 1) * GROUP_SZ]
        m1 = jnp.max(blk, axis=0, keepdims=True)
        i1 = jnp.min(jnp.where(blk == m1, e_in, GROUP_SZ), axis=0, keepdims=True)
        m2 = jnp.max(jnp.where(e_in == i1, neg, blk), axis=0, keepdims=True)
        gs_rows.append(m1 + m2)
    gs = jnp.concatenate(gs_rows, axis=0)
    g_io = lax.broadcasted_iota(jnp.int32, (N_GROUP, tm), 0)
    e_io = lax.broadcasted_iota(jnp.int32, (N_EXPERTS, tm), 0)
    e_grp = e_io >> int(math.log2(GROUP_SZ))
    masked = jnp.full((N_EXPERTS, tm), neg, F32)
    for _ in range(TOPK_GROUP):
        mx = jnp.max(gs, axis=0, keepdims=True)
        gi = jnp.min(jnp.where(gs == mx, g_io, N_GROUP), axis=0, keepdims=True)
        gs = jnp.where(g_io == gi, neg, gs)
        masked = jnp.where(e_grp == gi, ch, masked)
    idx_rows, w_rows = [], []
    onehot = jnp.zeros((N_EXPERTS, tm), F32)
    for _ in range(TOP_K):
        mx = jnp.max(masked, axis=0, keepdims=True)
        ei = jnp.min(jnp.where(masked == mx, e_io, N_EXPERTS), axis=0, keepdims=True)
        hit = e_io == ei
        idx_rows.append(ei)
        w_rows.append(jnp.sum(jnp.where(hit, sc, 0.0), axis=0, keepdims=True))
        onehot = jnp.where(hit, 1.0, onehot)
        masked = jnp.where(hit, neg, masked)
    wsel = jnp.concatenate(w_rows, axis=0)
    wts_ref[...] = wsel / jnp.sum(wsel, axis=0, keepdims=True) * ROUTED_SCALE
    idx_ref[...] = jnp.concatenate(idx_rows, axis=0)

    tr = lax.broadcasted_iota(jnp.int32, (tm, tm), 0)
    tc_ = lax.broadcasted_iota(jnp.int32, (tm, tm), 1)
    before = jnp.where(tr < tc_, 1.0, 0.0).astype(BF16)
    cum = jnp.dot(onehot.astype(BF16), before, preferred_element_type=F32) + carry_scr[...]
    rank_ref[...] = jnp.concatenate(
        [jnp.sum(jnp.where(e_io == ei, cum, 0.0), axis=0, keepdims=True) for ei in idx_rows],
        axis=0).astype(jnp.int32)
    carry_scr[...] = carry_scr[...] + jnp.sum(onehot, axis=1, keepdims=True)
    cnt_ref[...] = carry_scr[...]


def _router(h, w_router_t, bias_col, tm):
    t = h.shape[0]
    return pl.pallas_call(
        functools.partial(_router_kernel, tm=tm),
        grid=(t // tm,),
        in_specs=[pl.BlockSpec((tm, D_MODEL), lambda i: (i, 0)),
                  pl.BlockSpec((N_EXPERTS, D_MODEL), lambda i: (0, 0)),
                  pl.BlockSpec((N_EXPERTS, 1), lambda i: (0, 0))],
        out_specs=[pl.BlockSpec((TOP_K, tm), lambda i: (0, i)),
                   pl.BlockSpec((TOP_K, tm), lambda i: (0, i)),
                   pl.BlockSpec((TOP_K, tm), lambda i: (0, i)),
                   pl.BlockSpec((N_EXPERTS, 1), lambda i: (0, 0))],
        out_shape=[jax.ShapeDtypeStruct((TOP_K, t), jnp.int32),
                   jax.ShapeDtypeStruct((TOP_K, t), F32),
                   jax.ShapeDtypeStruct((TOP_K, t), jnp.int32),
                   jax.ShapeDtypeStruct((N_EXPERTS, 1), F32)],
        scratch_shapes=[pltpu.VMEM((N_EXPERTS, 1), F32)],
        compiler_params=_cparams(("arbitrary",)),
        name="router",
    )(h, w_router_t, bias_col)


def _dest_kernel(idx_ref, rank_ref, start_ref, dest_ref, *, tm, n_rows):
    e_io = lax.broadcasted_iota(jnp.int32, (N_EXPERTS, tm), 0)
    starts = start_ref[...]
    rows = []
    for k in range(TOP_K):
        seg = jnp.sum(jnp.where(e_io == idx_ref[k:k + 1, :], starts, 0.0), axis=0, keepdims=True)
        rows.append(seg.astype(jnp.int32) + rank_ref[k:k + 1, :])
    base = jnp.concatenate(rows, axis=0)
    for p in range(N_PIECE):
        dest_ref[p] = base + p * n_rows


def _dest(idx_t, rank_t, seg_start_col, n_rows, tm):
    t = idx_t.shape[1]
    blk = pl.BlockSpec((TOP_K, tm), lambda i: (0, i))
    return pl.pallas_call(
        functools.partial(_dest_kernel, tm=tm, n_rows=n_rows),
        grid=(t // tm,),
        in_specs=[blk, blk, pl.BlockSpec((N_EXPERTS, 1), lambda i: (0, 0))],
        out_specs=pl.BlockSpec((N_PIECE, TOP_K, tm), lambda i: (0, 0, i)),
        out_shape=jax.ShapeDtypeStruct((N_PIECE, TOP_K, t), jnp.int32),
        compiler_params=_cparams(("parallel",)),
        name="dispatch_rows",
    )(idx_t, rank_t, seg_start_col)


def _expert_kernel(be_ref, nu_ref, x_ref, wg_ref, wu_ref, wd_ref, y_ref, wgu_scr, wd_scr):
    i = pl.program_id(0)
    live = i < nu_ref[0]
    new_expert = (i == 0) | (be_ref[i] != be_ref[jnp.maximum(i - 1, 0)])

    @pl.when(live & new_expert)
    def _cache_weights():
        wgu_scr[:, :D_EXPERT] = wg_ref[0].astype(BF16)
        wgu_scr[:, D_EXPERT:] = wu_ref[0].astype(BF16)
        wd_scr[...] = wd_ref[0].astype(BF16)

    @pl.when(live)
    def _():
        n_sub = 2
        sub = EXPERT_BM // n_sub
        xs = [jnp.concatenate([c.astype(BF16) for c in
                               _load_pieces([x_ref[p, s * sub:(s + 1) * sub] for p in range(N_PIECE)])], axis=1)
              for s in range(n_sub)]
        gus = [jnp.dot(x, wgu_scr[...], preferred_element_type=F32) for x in xs]
        acts = [(_silu(gu[:, :D_EXPERT]) * gu[:, D_EXPERT:]).astype(BF16) for gu in gus]
        ys = [jnp.dot(a, wd_scr[...], preferred_element_type=F32) for a in acts]
        for s, y in enumerate(ys):
            for p in range(N_PIECE):
                y_ref[p, s * sub:(s + 1) * sub] = _pack_halves(y[:, 2 * p * PIECE_W:2 * (p + 1) * PIECE_W])


def _experts(xs, blk_exp, n_used, w_gate, w_up, w_down):
    n_rows = xs.shape[1]
    n_blocks = n_rows // EXPERT_BM

    def live(i, nu):
        return jnp.minimum(i, nu[0] - 1)

    grid_spec = pltpu.PrefetchScalarGridSpec(
        num_scalar_prefetch=2,
        grid=(n_blocks,),
        in_specs=[pl.BlockSpec((N_PIECE, EXPERT_BM, PIECE_W), lambda i, be, nu: (0, live(i, nu), 0)),
                  pl.BlockSpec((1, D_MODEL, D_EXPERT), lambda i, be, nu: (be[live(i, nu)], 0, 0)),
                  pl.BlockSpec((1, D_MODEL, D_EXPERT), lambda i, be, nu: (be[live(i, nu)], 0, 0)),
                  pl.BlockSpec((1, D_EXPERT, D_MODEL), lambda i, be, nu: (be[live(i, nu)], 0, 0))],
        out_specs=pl.BlockSpec((N_PIECE, EXPERT_BM, PIECE_W), lambda i, be, nu: (0, live(i, nu), 0)),
        scratch_shapes=[pltpu.VMEM((D_MODEL, 2 * D_EXPERT), BF16), pltpu.VMEM((D_EXPERT, D_MODEL), BF16)],
    )
    return pl.pallas_call(
        _expert_kernel,
        grid_spec=grid_spec,
        out_shape=jax.ShapeDtypeStruct((N_PIECE, n_rows, PIECE_W), jnp.uint32),
        compiler_params=_cparams(("arbitrary",)),
        name="experts",
    )(blk_exp, n_used, xs, w_gate, w_up, w_down)


SC_WINDOW = 128
V7X_SC_CORES = 2
V7X_SC_SUBCORES = 16


def _sc_mesh():
    return plsc.VectorSubcoreMesh(core_axis_name="core", subcore_axis_name="subcore",
                                  num_cores=V7X_SC_CORES, num_subcores=V7X_SC_SUBCORES)


def _sc_scatter_rows(src, dest, n_rows, seg, repeat):
    d = src.shape[1]
    n_idx = dest.shape[0]
    seg_blocks = seg // SC_WINDOW
    dest2 = dest.reshape(1, n_idx)

    def src_block(i):
        return ((i // (repeat * seg_blocks)) * seg_blocks + i % seg_blocks, 0)

    @functools.partial(pl.kernel, out_type=jax.ShapeDtypeStruct((n_rows, d), src.dtype), mesh=_sc_mesh(),
                       scratch_types=[], name="sc_dispatch")
    def run(src_hbm, idx_hbm, out_hbm):
        def body(rows_vmem, idx_vmem):
            pltpu.sync_copy(rows_vmem, out_hbm.at[idx_vmem.at[0]])

        pltpu.emit_pipeline(
            body,
            grid=(n_idx // SC_WINDOW,),
            in_specs=[pl.BlockSpec((SC_WINDOW, d), src_block),
                      pl.BlockSpec((1, SC_WINDOW), lambda i: (0, i))],
            out_specs=[],
            core_axis_name=("core", "subcore"),
            dimension_semantics=(pltpu.PARALLEL,),
        )(src_hbm, idx_hbm)

    return run(src, dest2)


def _sc_gather_rows(table, idx):
    d = table.shape[1]
    n_idx = idx.shape[0]
    idx2 = idx.reshape(1, n_idx)

    @functools.partial(pl.kernel, out_type=jax.ShapeDtypeStruct((n_idx, d), table.dtype), mesh=_sc_mesh(),
                       scratch_types=[], name="sc_combine_gather")
    def run(table_hbm, idx_hbm, out_hbm):
        def body(idx_vmem, rows_vmem):
            pltpu.sync_copy(table_hbm.at[idx_vmem.at[0]], rows_vmem)

        pltpu.emit_pipeline(
            body,
            grid=(n_idx // SC_WINDOW,),
            in_specs=[pl.BlockSpec((1, SC_WINDOW), lambda i: (0, i))],
            out_specs=[pl.BlockSpec((SC_WINDOW, d), lambda i: (i, 0))],
            core_axis_name=("core", "subcore"),
            dimension_semantics=(pltpu.PARALLEL,),
        )(idx_hbm, out_hbm)

    return run(table, idx2)


def _combine_kernel(yg_ref, wts_ref, h_ref, wgu_ref, wd_ref, g_ref, b_ref, outa_ref, outb_ref, *, alpha, tiles_a):
    wts = wts_ref[...]
    acc = None
    for k in range(TOP_K):
        cols = _load_pieces([yg_ref[p, k] for p in range(N_PIECE)])
        wk = wts[:, k:k + 1]
        acc = [c * wk for c in cols] if acc is None else [a + c * wk for a, c in zip(acc, cols)]
    routed = jnp.concatenate(acc, axis=1)
    h = h_ref[...]
    gu = _dot(h, wgu_ref[...])
    shared = _dot(_silu(gu[:, :D_EXPERT]) * gu[:, D_EXPERT:], wd_ref[...])
    y = _layer_norm(alpha * h + (routed + shared), g_ref[...], b_ref[...])
    i = pl.program_id(0)

    @pl.when(i < tiles_a)
    def _():
        outa_ref[...] = y

    @pl.when(i >= tiles_a)
    def _():
        outb_ref[...] = y


def _combine(yg, wts, h, w_sh_gu, w_sh_down, ln_g, ln_b, alpha, tm, t_a):
    t = h.shape[0]
    tiles_a = t_a // tm
    row = lambda i: (i, 0)
    const2 = lambda i: (0, 0)
    return pl.pallas_call(
        functools.partial(_combine_kernel, alpha=alpha, tiles_a=tiles_a),
        grid=(t // tm,),
        in_specs=[pl.BlockSpec((N_PIECE, TOP_K, tm, PIECE_W), lambda i: (0, 0, i, 0)),
                  pl.BlockSpec((tm, TOP_K), row),
                  pl.BlockSpec((tm, D_MODEL), row),
                  pl.BlockSpec(w_sh_gu.shape, const2),
                  pl.BlockSpec(w_sh_down.shape, const2),
                  pl.BlockSpec((1, D_MODEL), const2),
                  pl.BlockSpec((1, D_MODEL), const2)],
        out_specs=[pl.BlockSpec((tm, D_MODEL), lambda i: (jnp.minimum(i, tiles_a - 1), 0)),
                   pl.BlockSpec((tm, D_MODEL), lambda i: (jnp.maximum(i - tiles_a, 0), 0))],
        out_shape=[jax.ShapeDtypeStruct((t_a, D_MODEL), F32), jax.ShapeDtypeStruct((t - t_a, D_MODEL), F32)],
        compiler_params=_cparams(("arbitrary",)),
        name="combine_ln2",
    )(yg, wts, h, w_sh_gu, w_sh_down, ln_g, ln_b)


def _layer(xp, xs, s_delta, s_conv, s_pool, w_in, conv_w, a_log, dt_bias, gamma_a, w_br_a, w_pool,
           pool_scale, w_br_b, w_out, ln1_g, ln1_b, w_router, router_bias, w_exp_gate, w_exp_up,
           w_exp_down, w_sh_gate, w_sh_up, w_sh_down, ln2_g, ln2_b, alpha):
    Bp, Lp, _ = xp.shape
    Bs, Ls, _ = xs.shape
    Tp, Ts = Bp * Lp, Bs * Ls
    T = Tp + Ts
    x_p = xp.reshape(Tp, D_MODEL)
    x_s = xs.reshape(Ts, D_MODEL)
    xb = jnp.concatenate([x_p.astype(BF16), x_s.astype(BF16)], axis=0)

    o_z, o_b, o_a, o_u = QKV_W, QKV_W + V_W, QKV_W + V_W + H_V, QKV_W + V_W + 2 * H_V
    w_main = jnp.concatenate([w_in[:, :o_b], w_in[:, o_u:]], axis=1).astype(BF16)
    w_ba = jnp.pad(w_in[:, o_b:o_u], ((0, 0), (0, 128 - 2 * H_V))).astype(BF16)
    proj = _matmul(xb, w_main, 1024, 512, F32, "in_proj")
    ba = _matmul(xb, w_ba, 1024, 128, F32, "in_proj_beta_a")

    o_gated, sd_p = _gdn(proj, ba, conv_w, a_log, dt_bias, gamma_a, 0, Bp, Lp, 64, 1)
    buf8 = jnp.pad(s_conv, ((0, 0), (8 - (CONV_W - 1), 0), (0, 0)))
    o_gated, sd_s = _gdn(proj, ba, conv_w, a_log, dt_bias, gamma_a, Tp, Bs, Ls, Ls, GDN_STATE_SEQS,
                         buf8=buf8, s0=s_delta, o_all=o_gated)

    pooled = _pool(proj, 0, Bp, Lp, 256, 0)
    buf16 = jnp.pad(s_pool, ((0, 0), (1, 0), (0, 0)))
    pooled = _pool(proj, Tp, Bs, Ls, Ls, PAST_LEN, buf16=buf16, pooled_all=pooled)

    h, hp = _post(pooled, o_gated, proj, x_p, x_s, w_pool.astype(BF16), pool_scale.reshape(1, D_MODEL),
                  w_br_b.astype(BF16), w_br_a.astype(BF16), w_out.astype(BF16),
                  ln1_g.reshape(1, D_MODEL), ln1_b.reshape(1, D_MODEL), alpha, 256)

    idx_t, wts_t, rank_t, cnt = _router(h, w_router.T, router_bias.reshape(N_EXPERTS, 1), 512)
    counts = cnt[:, 0].astype(jnp.int32)
    padded = ((counts + EXPERT_BM - 1) // EXPERT_BM) * EXPERT_BM
    pends = jnp.cumsum(padded)
    pstarts = pends - padded
    n_blocks = (T * TOP_K + N_EXPERTS * (EXPERT_BM - 1) + EXPERT_BM - 1) // EXPERT_BM
    n_rows = n_blocks * EXPERT_BM
    piece_rows = _dest(idx_t, rank_t, pstarts.astype(F32).reshape(N_EXPERTS, 1), n_rows, 512)
    blk_start = jnp.arange(n_blocks, dtype=jnp.int32) * EXPERT_BM
    blk_exp = jnp.minimum(jnp.sum((pends[None, :] <= blk_start[:, None]).astype(jnp.int32), axis=1),
                          N_EXPERTS - 1)
    n_used = (pends[-1:] // EXPERT_BM).astype(jnp.int32)

    piece_idx = piece_rows.reshape(N_PIECE * TOP_K * T)
    x_sorted = _sc_scatter_rows(hp.reshape(N_PIECE * T, PIECE_W), piece_idx, N_PIECE * n_rows, T, TOP_K)
    y_sorted = _experts(x_sorted.reshape(N_PIECE, n_rows, PIECE_W), blk_exp, n_used,
                        w_exp_gate, w_exp_up, w_exp_down)
    yg = _sc_gather_rows(y_sorted.reshape(N_PIECE * n_rows, PIECE_W), piece_idx)
    yg = yg.reshape(N_PIECE, TOP_K, T, PIECE_W)

    w_sh_gu = jnp.concatenate([w_sh_gate, w_sh_up], axis=1).astype(BF16)
    y_p, y_s = _combine(yg, wts_t.T, h, w_sh_gu, w_sh_down.astype(BF16),
                        ln2_g.reshape(1, D_MODEL), ln2_b.reshape(1, D_MODEL), alpha, 256, Tp)

    def last_rows(group_row0, B, L, n, col0, width):
        rows = group_row0 + jnp.arange(B, dtype=jnp.int32)[:, None] * L + (L - n) + jnp.arange(n, dtype=jnp.int32)
        return jnp.take(proj, rows.reshape(-1), axis=0)[:, col0:col0 + width].reshape(B, n, width)

    nc_ = CONV_W - 1
    new_conv_p = last_rows(0, Bp, Lp, nc_, 0, QKV_W)
    new_pool_p = last_rows(0, Bp, Lp, POOL_BUF, o_b, D_MODEL)
    qkv_s = proj[Tp:, :QKV_W].reshape(Bs, Ls, QKV_W)
    u_s = proj[Tp:, o_b:o_b + D_MODEL].reshape(Bs, Ls, D_MODEL)
    new_conv_s = jnp.concatenate([s_conv, qkv_s], axis=1)[:, -nc_:]
    new_pool_s = jnp.concatenate([s_pool, u_s], axis=1)[:, -POOL_BUF:]
    return (y_p.reshape(Bp, Lp, D_MODEL), y_s.reshape(Bs, Ls, D_MODEL),
            sd_p, new_conv_p, new_pool_p, sd_s, new_conv_s, new_pool_s)


def kernel(x_prompt, x_sample, state_delta, state_conv, state_pool, w_in, conv_w, a_log, dt_bias, gamma_a,
           w_br_a, w_pool, pool_scale, w_br_b, w_out, ln1_g, ln1_b, w_router, router_bias,
           w_exp_gate, w_exp_up, w_exp_down, w_sh_gate, w_sh_up, w_sh_down, ln2_g, ln2_b):
    depth = w_in.shape[0]
    alpha = (2 * depth) ** 0.25
    yp, ys = x_prompt, x_sample
    outs = [[] for _ in range(6)]
    for l in range(depth):
        res = _layer(yp, ys, state_delta[l], state_conv[l], state_pool[l], w_in[l], conv_w[l], a_log[l],
                     dt_bias[l], gamma_a[l], w_br_a[l], w_pool[l], pool_scale[l], w_br_b[l], w_out[l],
                     ln1_g[l], ln1_b[l], w_router[l], router_bias[l], w_exp_gate[l], w_exp_up[l],
                     w_exp_down[l], w_sh_gate[l], w_sh_up[l], w_sh_down[l], ln2_g[l], ln2_b[l], alpha)
        yp, ys = res[0], res[1]
        for lst, v in zip(outs, res[2:]):
            lst.append(v)
    return (yp, ys) + tuple(jnp.stack(v) for v in outs)
```

```python
import functools
import math

import jax
import jax.numpy as jnp
from jax import lax
from jax.experimental import pallas as pl
from jax.experimental.pallas import tpu as pltpu
from jax.experimental.pallas import tpu_sc as plsc

F32 = jnp.float32
BF16 = jnp.bfloat16

D_MODEL = 1024
H_K = 8
D_K = 128
H_V = 16
D_V = 128
Q_W = H_K * D_K
V_W = H_V * D_V
QKV_W = 2 * Q_W + V_W
CONV_W = 4
POOL_WINDOWS = (2, 4, 8, 16)
POOL_GROUP_W = D_MODEL // len(POOL_WINDOWS)
POOL_BUF = max(POOL_WINDOWS) - 1
N_EXPERTS = 256
TOP_K = 8
N_GROUP = 8
TOPK_GROUP = 4
GROUP_SZ = N_EXPERTS // N_GROUP
D_EXPERT = D_MODEL // 4
ROUTED_SCALE = 2.5
LN_EPS = 1e-5
NORM_EPS = 1e-6
PAST_LEN = 16384

MAIN_W = QKV_W + V_W + 3 * D_MODEL
Z_BLK = QKV_W // V_W
U_BLK = (QKV_W + V_W) // D_MODEL

STACK_ROWS = 128
SOLVE_BASE = 16
EXPERT_BM = 512
GDN_STATE_SEQS = 4
POOL_STATE_SEQS = 16
VMEM_LIMIT = 56 * 1024 * 1024


def _cparams(sem):
    return pltpu.CompilerParams(dimension_semantics=sem, vmem_limit_bytes=VMEM_LIMIT)


def _sigmoid(x):
    return 1.0 / (1.0 + jnp.exp(-x))


def _silu(x):
    return x * _sigmoid(x)


def _softplus(x):
    return jnp.maximum(x, 0.0) + jnp.log(1.0 + jnp.exp(-jnp.abs(x)))


def _dot(a, b):
    return jnp.dot(a.astype(BF16), b.astype(BF16), preferred_element_type=F32)


def _dot_nt(a, b):
    return lax.dot_general(a.astype(BF16), b.astype(BF16), (((1,), (1,)), ((), ())),
                           preferred_element_type=F32)


def _dot_tn(a, b):
    return lax.dot_general(a.astype(BF16), b.astype(BF16), (((0,), (0,)), ((), ())),
                           preferred_element_type=F32)


def _split(a):
    hi = a.astype(BF16)
    lo = (a - hi.astype(F32)).astype(BF16)
    return hi, lo


def _dot3_nt(a, b):
    ah, al = _split(a)
    bh, bl = _split(b)
    d = functools.partial(lax.dot_general, dimension_numbers=(((1,), (1,)), ((), ())),
                          preferred_element_type=F32)
    return d(ah, bh) + (d(ah, bl) + d(al, bh))


def _pack_halves(x):
    n = x.shape[1] // 2
    hi = lax.bitcast_convert_type(x[:, :n].astype(BF16).astype(F32), jnp.uint32)
    lo = lax.bitcast_convert_type(x[:, n:].astype(BF16).astype(F32), jnp.uint32)
    return (hi & jnp.uint32(0xFFFF0000)) | (lo >> 16)


def _unpack_halves(w):
    hi = lax.bitcast_convert_type(w & jnp.uint32(0xFFFF0000), F32)
    lo = lax.bitcast_convert_type(w << 16, F32)
    return hi, lo


N_PIECE = 2
PIECE_W = D_MODEL // (2 * N_PIECE)


def _store_pieces(ref, x):
    for p in range(N_PIECE):
        ref[p] = _pack_halves(x[:, 2 * p * PIECE_W:2 * (p + 1) * PIECE_W])


def _load_pieces(pieces):
    cols = []
    for w in pieces:
        cols.extend(_unpack_halves(w))
    return cols


def _layer_norm(x, g, b):
    mu = jnp.mean(x, axis=-1, keepdims=True)
    xc = x - mu
    var = jnp.mean(xc * xc, axis=-1, keepdims=True)
    return xc * lax.rsqrt(var + LN_EPS) * g + b


def _in_proj_kernel(xa_ref, xb_ref, w_ref, wba_ref, o_ref, ba_ref, x_scr, *, tiles_a):
    @pl.when(pl.program_id(1) == 0)
    def _():
        x = jnp.where(pl.program_id(0) < tiles_a, xa_ref[...], xb_ref[...]).astype(BF16)
        x_scr[...] = x
        ba_ref[...] = jnp.dot(x, wba_ref[...], preferred_element_type=F32)

    o_ref[...] = jnp.dot(x_scr[...], w_ref[...], preferred_element_type=F32).astype(o_ref.dtype)


def _in_proj(x_a, x_b, w_main, w_ba, tm, tn):
    k = x_a.shape[1]
    t = x_a.shape[0] + x_b.shape[0]
    n = w_main.shape[1]
    tiles_a = x_a.shape[0] // tm
    return pl.pallas_call(
        functools.partial(_in_proj_kernel, tiles_a=tiles_a),
        grid=(t // tm, n // tn),
        in_specs=[pl.BlockSpec((tm, k), lambda i, j: (jnp.minimum(i, tiles_a - 1), 0)),
                  pl.BlockSpec((tm, k), lambda i, j: (jnp.maximum(i - tiles_a, 0), 0)),
                  pl.BlockSpec((k, tn), lambda i, j: (0, j)),
                  pl.BlockSpec(w_ba.shape, lambda i, j: (0, 0))],
        out_specs=[pl.BlockSpec((tm, tn), lambda i, j: (i, j)),
                   pl.BlockSpec((tm, w_ba.shape[1]), lambda i, j: (i, 0))],
        out_shape=[jax.ShapeDtypeStruct((t, n), BF16), jax.ShapeDtypeStruct((t, w_ba.shape[1]), F32)],
        scratch_shapes=[pltpu.VMEM((tm, k), BF16)],
        compiler_params=_cparams(("parallel", "arbitrary")),
        name="in_proj",
    )(x_a, x_b, w_main, w_ba)


def _mm_kernel(x_ref, w_ref, o_ref):
    o_ref[...] = jnp.dot(x_ref[...], w_ref[...], preferred_element_type=F32).astype(o_ref.dtype)


def _matmul(x, w, tm, tn, out_dtype, name):
    t, k = x.shape
    n = w.shape[1]
    return pl.pallas_call(
        _mm_kernel,
        grid=(t // tm, n // tn),
        in_specs=[pl.BlockSpec((tm, k), lambda i, j: (i, 0)),
                  pl.BlockSpec((k, tn), lambda i, j: (0, j))],
        out_specs=pl.BlockSpec((tm, tn), lambda i, j: (i, j)),
        out_shape=jax.ShapeDtypeStruct((t, n), out_dtype),
        compiler_params=_cparams(("parallel", "parallel")),
        name=name,
    )(x, w)


def _gdn_kernel(*refs, C, hg, nseq, has_state):
    if has_state:
        (qkv_ref, z_ref, ba_ref, cw_ref, ad_ref, gam_ref, buf_ref, s0_ref, _,
         o_ref, sout_ref, s_scr, tail_scr, act_scr) = refs
    else:
        (qkv_ref, z_ref, ba_ref, cw_ref, ad_ref, gam_ref,
         o_ref, sout_ref, s_scr, tail_scr, act_scr) = refs
    R = hg * C
    ng = H_V // hg
    log2c = int(math.log2(C))
    c = pl.program_id(1)

    @pl.when(c == 0)
    def _init():
        if has_state:
            for s in range(nseq):
                s_scr[s * H_V:(s + 1) * H_V] = s0_ref[s]
                tail_scr[s * 8:(s + 1) * 8] = buf_ref[s]
        else:
            s_scr[...] = jnp.zeros(s_scr.shape, F32)
            tail_scr[...] = jnp.zeros(tail_scr.shape, F32)

    cw = cw_ref[...]
    qkv_all = qkv_ref[...].astype(F32)
    for s in range(nseq):
        x = qkv_all[s * C:(s + 1) * C]
        xe = jnp.concatenate([tail_scr[s * 8:(s + 1) * 8], x], axis=0)
        y = xe[8:8 + C] * cw[3:4]
        for j in range(CONV_W - 1):
            y = y + xe[5 + j:5 + j + C] * cw[j:j + 1]
        tail_scr[s * 8:(s + 1) * 8] = x[C - 8:C]
        act_scr[s * C:(s + 1) * C] = _silu(y)

    ri = lax.broadcasted_iota(jnp.int32, (R, R), 0)
    ci = lax.broadcasted_iota(jnp.int32, (R, R), 1)
    same = (ri >> log2c) == (ci >> log2c)
    incl = same & (ri >= ci)
    strict = same & (ri > ci)
    eye = ri == ci
    is_last = ci == (((ri >> log2c) << log2c) + (C - 1))
    eye_f = jnp.where(eye, 1.0, 0.0).astype(F32)
    cum_u = jnp.where(same & (ri <= ci), 1.0, 0.0).astype(BF16)

    nrow = ba_ref.shape[2] // 2
    adv = ad_ref[...]
    dd = functools.partial(jnp.dot, preferred_element_type=F32)
    beta_seq, G_seq = [], []
    for s in range(nseq):
        bav = ba_ref[s, 0]
        beta_seq.append(_sigmoid(bav[0:nrow]))
        g_rows = -jnp.exp(adv[0:nrow]) * _softplus(bav[nrow:2 * nrow] + adv[nrow:2 * nrow])
        g1 = g_rows.astype(BF16)
        r1 = g_rows - g1.astype(F32)
        g2 = r1.astype(BF16)
        g3 = (r1 - g2.astype(F32)).astype(BF16)
        G_seq.append(dd(g1, cum_u) + (dd(g2, cum_u) + dd(g3, cum_u)))

    gam = gam_ref[...]
    normed = {}

    def qk_head(s, kind, kh):
        key = (s, kind, kh)
        if key not in normed:
            off = (0 if kind == "q" else Q_W) + kh * D_K
            v = act_scr[s * C:(s + 1) * C, off:off + D_K]
            v = v * lax.rsqrt(jnp.sum(v * v, axis=-1, keepdims=True) + NORM_EPS)
            if kind == "q":
                v = v * (D_K ** -0.5)
            normed[key] = v
        return normed[key]

    rep = H_V // H_K
    grp = []
    for s in range(nseq):
        rows = slice(s * C, (s + 1) * C)
        for g in range(ng):
            heads = [g * hg + hh for hh in range(hg)]
            qst = jnp.concatenate([qk_head(s, "q", h // rep) for h in heads], axis=0)
            kst = jnp.concatenate([qk_head(s, "k", h // rep) for h in heads], axis=0)
            vst = jnp.concatenate([act_scr[rows, 2 * Q_W + h * D_V:2 * Q_W + (h + 1) * D_V] for h in heads], axis=0)
            grow = G_seq[s][g:g + 1]
            brow = beta_seq[s][g:g + 1]
            gcol = jnp.sum(eye_f * grow, axis=1, keepdims=True)
            bcol = jnp.sum(eye_f * brow, axis=1, keepdims=True)
            glast = jnp.sum(jnp.where(is_last, grow, 0.0), axis=1, keepdims=True)
            dm = jnp.where(incl, jnp.exp(jnp.minimum(gcol - grow, 0.0)), 0.0)
            eg = jnp.exp(gcol)
            grp.append(dict(rows=rows, s0=s * H_V, heads=heads, qst=qst, kst=kst, gcol=gcol, bcol=bcol,
                            glast=glast, dm=dm, eg=eg,
                            x=jnp.concatenate([vst * bcol, kst * (bcol * eg)], axis=1)))

    for d in grp:
        d["kk"] = _dot_nt(d["kst"], d["kst"])
        d["qk"] = _dot_nt(d["qst"], d["kst"])
    log2b = min(log2c, int(math.log2(SOLVE_BASE)))
    same_base = ((ri ^ ci) >> log2b) == 0
    for d in grp:
        d["m"] = jnp.where(strict, -(d["kk"] * d["bcol"] * d["dm"]), 0.0)
        d["attn"] = jnp.where(incl, d["qk"] * d["dm"], 0.0)
        d["pw"] = jnp.where(same_base, d["m"], 0.0)
        d["t"] = eye_f + d["pw"]

    for r in range(log2b):
        last = r == log2b - 1
        for d in grp:
            if r == 0:
                if not last:
                    d["pw"] = _dot(d["pw"], d["pw"])
            elif last:
                d["t"] = d["t"] + _dot(d["pw"], d["t"])
            else:
                out = _dot(d["pw"], jnp.concatenate([d["pw"], d["t"]], axis=1))
                d["pw"] = out[:, :R]
                d["t"] = d["t"] + out[:, R:]
    for lev in range(log2b + 1, log2c + 1):
        coupling = ((ri ^ ci) >> (lev - 1)) == 1
        for d in grp:
            d["y"] = _dot(d["t"], jnp.where(coupling, d["m"], 0.0))
        for d in grp:
            d["t"] = d["t"] + _dot(d["y"], d["t"])
    for d in grp:
        d["x"] = _dot(d["t"], d["x"])

    for d in grp:
        qe = d["qst"] * d["eg"]
        d["xs"] = [_dot(jnp.concatenate([d["x"][hh * C:(hh + 1) * C, D_V:], qe[hh * C:(hh + 1) * C]], axis=0),
                        s_scr[d["s0"] + h]) for hh, h in enumerate(d["heads"])]
    for d in grp:
        d["vnew"] = jnp.concatenate(
            [d["x"][hh * C:(hh + 1) * C, :D_V] - xs[:C] for hh, xs in enumerate(d["xs"])], axis=0)
        qs = jnp.concatenate([xs[C:] for xs in d["xs"]], axis=0)
        d["o"] = qs + _dot(d["attn"], d["vnew"])

    row_head = lax.broadcasted_iota(jnp.int32, (R, D_V), 0) >> log2c
    for d in grp:
        kt = d["kst"] * jnp.exp(d["glast"] - d["gcol"])
        egl = jnp.exp(d["glast"])
        for hh, h in enumerate(d["heads"]):
            sl = slice(hh * C, (hh + 1) * C)
            if C >= 16:
                upd = _dot_tn(kt[sl], d["vnew"][sl])
            else:
                upd = _dot_tn(kt, jnp.where(row_head == hh, d["vnew"], 0.0))
            s_scr[d["s0"] + h] = s_scr[d["s0"] + h] * egl[hh * C:hh * C + 1] + upd

    z_all = z_ref[...].astype(F32)
    for d in grp:
        o = d["o"]
        zst = jnp.concatenate([z_all[d["rows"], h * D_V:(h + 1) * D_V] for h in d["heads"]], axis=0)
        on = o * lax.rsqrt(jnp.mean(o * o, axis=-1, keepdims=True) + NORM_EPS) * gam * _silu(zst)
        for hh, h in enumerate(d["heads"]):
            o_ref[d["rows"], h * D_V:(h + 1) * D_V] = on[hh * C:(hh + 1) * C].astype(o_ref.dtype)

    @pl.when(c == pl.num_programs(1) - 1)
    def _fin():
        for s in range(nseq):
            sout_ref[s] = s_scr[s * H_V:(s + 1) * H_V]


def _gdn(proj, ba, conv_w, a_log, dt_bias, gamma_a, row0, B, L, C, nseq, buf8=None, s0=None, o_all=None):
    has_state = s0 is not None
    hg = STACK_ROWS // C
    ng = H_V // hg
    nrow = 16
    nc = L // C
    R = STACK_ROWS
    assert nseq == 1 or nc == 1
    rows_blk = nseq * C

    def arrange(v):
        v = v.reshape(B, nc, C, ng, hg).transpose(0, 1, 3, 4, 2).reshape(B, nc, ng, R)
        return jnp.pad(v, ((0, 0), (0, 0), (0, nrow - ng), (0, 0)))

    rows = ba[row0:row0 + B * L]
    ba_arr = jnp.concatenate([arrange(rows[:, :H_V]), arrange(rows[:, H_V:2 * H_V])], axis=2)

    def arrange_param(p):
        v = jnp.broadcast_to(p.reshape(ng, hg, 1), (ng, hg, C)).reshape(ng, R)
        return jnp.pad(v, ((0, nrow - ng), (0, 0)))

    ad = jnp.concatenate([arrange_param(a_log), arrange_param(dt_bias)], axis=0)
    rb0 = row0 // rows_blk
    in_specs = [
        pl.BlockSpec((rows_blk, QKV_W), lambda b, c: (rb0 + b * nc + c, 0)),
        pl.BlockSpec((rows_blk, V_W), lambda b, c: (rb0 + b * nc + c, Z_BLK)),
        pl.BlockSpec((nseq, 1, 2 * nrow, R), lambda b, c: (b, c, 0, 0)),
        pl.BlockSpec((CONV_W, QKV_W), lambda b, c: (0, 0)),
        pl.BlockSpec((2 * nrow, R), lambda b, c: (0, 0)),
        pl.BlockSpec((1, D_V), lambda b, c: (0, 0)),
    ]
    args = [proj, proj, ba_arr, conv_w, ad, gamma_a.reshape(1, D_V)]
    aliases = {}
    if has_state:
        in_specs += [pl.BlockSpec((nseq, 8, QKV_W), lambda b, c: (b, 0, 0)),
                     pl.BlockSpec((nseq, H_V, D_K, D_V), lambda b, c: (b, 0, 0, 0)),
                     pl.BlockSpec(memory_space=pl.ANY)]
        args += [buf8, s0, o_all]
        aliases = {len(args) - 1: 0}
    return pl.pallas_call(
        functools.partial(_gdn_kernel, C=C, hg=hg, nseq=nseq, has_state=has_state),
        grid=(B // nseq, nc),
        in_specs=in_specs,
        out_specs=[pl.BlockSpec((rows_blk, V_W), lambda b, c: (rb0 + b * nc + c, 0)),
                   pl.BlockSpec((nseq, H_V, D_K, D_V), lambda b, c: (b, 0, 0, 0))],
        out_shape=[jax.ShapeDtypeStruct((proj.shape[0], V_W), BF16),
                   jax.ShapeDtypeStruct((B, H_V, D_K, D_V), F32)],
        scratch_shapes=[pltpu.VMEM((nseq * H_V, D_K, D_V), F32),
                        pltpu.VMEM((nseq * 8, QKV_W), F32),
                        pltpu.VMEM((rows_blk, QKV_W), F32)],
        input_output_aliases=aliases,
        compiler_params=_cparams(("parallel", "arbitrary")),
        name="gdn_state" if has_state else "gdn_fresh",
    )(*args)


def _pool_kernel(*refs, tc, nseq, start_pos, has_state):
    if has_state:
        u_ref, buf_ref, _, o_ref, tail_scr = refs
    else:
        u_ref, o_ref, tail_scr = refs
    hist = POOL_BUF + 1
    c = pl.program_id(1)

    @pl.when(c == 0)
    def _init():
        if has_state:
            for s in range(nseq):
                tail_scr[s * hist:(s + 1) * hist] = buf_ref[s]
        else:
            tail_scr[...] = jnp.zeros(tail_scr.shape, F32)

    u_all = u_ref[...].astype(F32)
    pos1 = (start_pos + 1 + c * tc + lax.broadcasted_iota(jnp.int32, (tc, 1), 0)).astype(F32)
    for s in range(nseq):
        u = u_all[s * tc:(s + 1) * tc]
        xe = jnp.concatenate([tail_scr[s * hist:(s + 1) * hist], u], axis=0)
        tail_scr[s * hist:(s + 1) * hist] = xe[tc:tc + hist]
        for gi, win in enumerate(POOL_WINDOWS):
            sl = slice(gi * POOL_GROUP_W, (gi + 1) * POOL_GROUP_W)
            acc = xe[:, sl]
            shift = 1
            while shift < win:
                acc = acc + pltpu.roll(acc, shift, 0)
                shift *= 2
            cnt = jnp.minimum(float(win), pos1)
            o_ref[s * tc:(s + 1) * tc, sl] = acc[hist:hist + tc] / cnt - u[:, sl]


def _pool(proj, row0, B, L, tc, nseq, start_pos, buf16=None, pooled_all=None):
    has_state = buf16 is not None
    nc = L // tc
    assert nseq == 1 or nc == 1
    rows_blk = nseq * tc
    rb0 = row0 // rows_blk
    in_specs = [pl.BlockSpec((rows_blk, D_MODEL), lambda b, c: (rb0 + b * nc + c, U_BLK))]
    args = [proj]
    aliases = {}
    if has_state:
        in_specs += [pl.BlockSpec((nseq, POOL_BUF + 1, D_MODEL), lambda b, c: (b, 0, 0)),
                     pl.BlockSpec(memory_space=pl.ANY)]
        args += [buf16, pooled_all]
        aliases = {len(args) - 1: 0}
    return pl.pallas_call(
        functools.partial(_pool_kernel, tc=tc, nseq=nseq, start_pos=start_pos, has_state=has_state),
        grid=(B // nseq, nc),
        in_specs=in_specs,
        out_specs=pl.BlockSpec((rows_blk, D_MODEL), lambda b, c: (rb0 + b * nc + c, 0)),
        out_shape=jax.ShapeDtypeStruct((proj.shape[0], D_MODEL), F32),
        scratch_shapes=[pltpu.VMEM((nseq * (POOL_BUF + 1), D_MODEL), F32)],
        input_output_aliases=aliases,
        compiler_params=_cparams(("parallel", "arbitrary")),
        name="pool_state" if has_state else "pool_fresh",
    )(*args)


def _post_kernel(pooled_ref, o_ref, ga_ref, gb_ref, xa_ref, xb_ref, wp_ref, ps_ref, wbb_ref, wba_ref, wo_ref,
                 g_ref, b_ref, h_ref, hp_ref, *, alpha, tiles_a):
    x = jnp.where(pl.program_id(0) < tiles_a, xa_ref[...], xb_ref[...])
    pooled = pooled_ref[...]
    mixed = jnp.concatenate(
        [_dot(pooled[:, gi * POOL_GROUP_W:(gi + 1) * POOL_GROUP_W], wp_ref[gi]) for gi in range(len(POOL_WINDOWS))],
        axis=1) * ps_ref[...]
    branch_b = _dot(mixed, wbb_ref[...])
    branch_a = _dot(o_ref[...], wba_ref[...])
    merged = _sigmoid(ga_ref[...].astype(F32)) * branch_a + _sigmoid(gb_ref[...].astype(F32)) * branch_b
    h = _layer_norm(alpha * x + _dot(merged, wo_ref[...]), g_ref[...], b_ref[...])
    h_ref[...] = h
    _store_pieces(hp_ref, h)


def _post(pooled, o_gated, proj, x_a, x_b, w_pool, pool_scale, w_br_b, w_br_a, w_out, ln_g, ln_b, alpha, tm):
    t = x_a.shape[0] + x_b.shape[0]
    tiles_a = x_a.shape[0] // tm
    row = lambda i: (i, 0)
    const2 = lambda i: (0, 0)
    return pl.pallas_call(
        functools.partial(_post_kernel, alpha=alpha, tiles_a=tiles_a),
        grid=(t // tm,),
        in_specs=[pl.BlockSpec((tm, D_MODEL), row),
                  pl.BlockSpec((tm, V_W), row),
                  pl.BlockSpec((tm, D_MODEL), lambda i: (i, U_BLK + 1)),
                  pl.BlockSpec((tm, D_MODEL), lambda i: (i, U_BLK + 2)),
                  pl.BlockSpec((tm, D_MODEL), lambda i: (jnp.minimum(i, tiles_a - 1), 0)),
                  pl.BlockSpec((tm, D_MODEL), lambda i: (jnp.maximum(i - tiles_a, 0), 0)),
                  pl.BlockSpec(w_pool.shape, lambda i: (0, 0, 0)),
                  pl.BlockSpec((1, D_MODEL), const2),
                  pl.BlockSpec(w_br_b.shape, const2),
                  pl.BlockSpec(w_br_a.shape, const2),
                  pl.BlockSpec(w_out.shape, const2),
                  pl.BlockSpec((1, D_MODEL), const2),
                  pl.BlockSpec((1, D_MODEL), const2)],
        out_specs=[pl.BlockSpec((tm, D_MODEL), row), pl.BlockSpec((N_PIECE, tm, PIECE_W), lambda i: (0, i, 0))],
        out_shape=[jax.ShapeDtypeStruct((t, D_MODEL), F32),
                   jax.ShapeDtypeStruct((N_PIECE, t, PIECE_W), jnp.uint32)],
        compiler_params=_cparams(("parallel",)),
        name="post_mixers",
    )(pooled, o_gated, proj, proj, x_a, x_b, w_pool, pool_scale, w_br_b, w_br_a, w_out, ln_g, ln_b)


def _router_kernel(h_ref, wt_ref, bias_ref, idx_ref, wts_ref, rank_ref, cnt_ref, carry_scr, *, tm):
    i = pl.program_id(0)

    @pl.when(i == 0)
    def _init():
        carry_scr[...] = jnp.zeros(carry_scr.shape, F32)

    logits = _dot3_nt(wt_ref[...], h_ref[...])
    sc = _sigmoid(logits)
    ch = sc + bias_ref[...]
    neg = -jnp.inf
    e_in = lax.broadcasted_iota(jnp.int32, (GROUP_SZ, tm), 0)
    gs_rows = []
    for g in range(N_GROUP):
        blk = ch[g * GROUP_SZ:(g + 1) * GROUP_SZ]
        m1 = jnp.max(blk, axis=0, keepdims=True)
        i1 = jnp.min(jnp.where(blk == m1, e_in, GROUP_SZ), axis=0, keepdims=True)
        m2 = jnp.max(jnp.where(e_in == i1, neg, blk), axis=0, keepdims=True)
        gs_rows.append(m1 + m2)
    gs = jnp.concatenate(gs_rows, axis=0)
    g_io = lax.broadcasted_iota(jnp.int32, (N_GROUP, tm), 0)
    e_io = lax.broadcasted_iota(jnp.int32, (N_EXPERTS, tm), 0)
    e_grp = e_io >> int(math.log2(GROUP_SZ))
    masked = jnp.full((N_EXPERTS, tm), neg, F32)
    for _ in range(TOPK_GROUP):
        mx = jnp.max(gs, axis=0, keepdims=True)
        gi = jnp.min(jnp.where(gs == mx, g_io, N_GROUP), axis=0, keepdims=True)
        gs = jnp.where(g_io == gi, neg, gs)
        masked = jnp.where(e_grp == gi, ch, masked)
    idx_rows, w_rows = [], []
    onehot = jnp.zeros((N_EXPERTS, tm), F32)
    for _ in range(TOP_K):
        mx = jnp.max(masked, axis=0, keepdims=True)
        ei = jnp.min(jnp.where(masked == mx, e_io, N_EXPERTS), axis=0, keepdims=True)
        hit = e_io == ei
        idx_rows.append(ei)
        w_rows.append(jnp.sum(jnp.where(hit, sc, 0.0), axis=0, keepdims=True))
        onehot = jnp.where(hit, 1.0, onehot)
        masked = jnp.where(hit, neg, masked)
    wsel = jnp.concatenate(w_rows, axis=0)
    wts_ref[...] = wsel / jnp.sum(wsel, axis=0, keepdims=True) * ROUTED_SCALE
    idx_ref[...] = jnp.concatenate(idx_rows, axis=0)

    tr = lax.broadcasted_iota(jnp.int32, (tm, tm), 0)
    tc_ = lax.broadcasted_iota(jnp.int32, (tm, tm), 1)
    before = jnp.where(tr < tc_, 1.0, 0.0).astype(BF16)
    cum = jnp.dot(onehot.astype(BF16), before, preferred_element_type=F32) + carry_scr[...]
    rank_ref[...] = jnp.concatenate(
        [jnp.sum(jnp.where(e_io == ei, cum, 0.0), axis=0, keepdims=True) for ei in idx_rows],
        axis=0).astype(jnp.int32)
    carry_scr[...] = carry_scr[...] + jnp.sum(onehot, axis=1, keepdims=True)
    cnt_ref[...] = carry_scr[...]


def _router(h, w_router_t, bias_col, tm):
    t = h.shape[0]
    return pl.pallas_call(
        functools.partial(_router_kernel, tm=tm),
        grid=(t // tm,),
        in_specs=[pl.BlockSpec((tm, D_MODEL), lambda i: (i, 0)),
                  pl.BlockSpec((N_EXPERTS, D_MODEL), lambda i: (0, 0)),
                  pl.BlockSpec((N_EXPERTS, 1), lambda i: (0, 0))],
        out_specs=[pl.BlockSpec((TOP_K, tm), lambda i: (0, i)),
                   pl.BlockSpec((TOP_K, tm), lambda i: (0, i)),
                   pl.BlockSpec((TOP_K, tm), lambda i: (0, i)),
                   pl.BlockSpec((N_EXPERTS, 1), lambda i: (0, 0))],
        out_shape=[jax.ShapeDtypeStruct((TOP_K, t), jnp.int32),
                   jax.ShapeDtypeStruct((TOP_K, t), F32),
                   jax.ShapeDtypeStruct((TOP_K, t), jnp.int32),
                   jax.ShapeDtypeStruct((N_EXPERTS, 1), F32)],
        scratch_shapes=[pltpu.VMEM((N_EXPERTS, 1), F32)],
        compiler_params=_cparams(("arbitrary",)),
        name="router",
    )(h, w_router_t, bias_col)


def _dest_kernel(idx_ref, rank_ref, start_ref, dest_ref, *, tm, n_rows):
    e_io = lax.broadcasted_iota(jnp.int32, (N_EXPERTS, tm), 0)
    starts = start_ref[...]
    rows = []
    for k in range(TOP_K):
        seg = jnp.sum(jnp.where(e_io == idx_ref[k:k + 1, :], starts, 0.0), axis=0, keepdims=True)
        rows.append(seg.astype(jnp.int32) + rank_ref[k:k + 1, :])
    base = jnp.concatenate(rows, axis=0)
    for p in range(N_PIECE):
        dest_ref[p] = base + p * n_rows


def _dest(idx_t, rank_t, seg_start_col, n_rows, tm):
    t = idx_t.shape[1]
    blk = pl.BlockSpec((TOP_K, tm), lambda i: (0, i))
    return pl.pallas_call(
        functools.partial(_dest_kernel, tm=tm, n_rows=n_rows),
        grid=(t // tm,),
        in_specs=[blk, blk, pl.BlockSpec((N_EXPERTS, 1), lambda i: (0, 0))],
        out_specs=pl.BlockSpec((N_PIECE, TOP_K, tm), lambda i: (0, 0, i)),
        out_shape=jax.ShapeDtypeStruct((N_PIECE, TOP_K, t), jnp.int32),
        compiler_params=_cparams(("parallel",)),
        name="dispatch_rows",
    )(idx_t, rank_t, seg_start_col)


def _expert_kernel(be_ref, nu_ref, x_ref, wg_ref, wu_ref, wd_ref, y_ref, wgu_scr, wd_scr):
    i = pl.program_id(0)
    live = i < nu_ref[0]
    new_expert = (i == 0) | (be_ref[i] != be_ref[jnp.maximum(i - 1, 0)])

    @pl.when(live & new_expert)
    def _cache_weights():
        wgu_scr[:, :D_EXPERT] = wg_ref[0].astype(BF16)
        wgu_scr[:, D_EXPERT:] = wu_ref[0].astype(BF16)
        wd_scr[...] = wd_ref[0].astype(BF16)

    @pl.when(live)
    def _():
        n_sub = 2
        sub = EXPERT_BM // n_sub
        xs = [jnp.concatenate([c.astype(BF16) for c in
                               _load_pieces([x_ref[p, s * sub:(s + 1) * sub] for p in range(N_PIECE)])], axis=1)
              for s in range(n_sub)]
        gus = [jnp.dot(x, wgu_scr[...], preferred_element_type=F32) for x in xs]
        acts = [(_silu(gu[:, :D_EXPERT]) * gu[:, D_EXPERT:]).astype(BF16) for gu in gus]
        ys = [jnp.dot(a, wd_scr[...], preferred_element_type=F32) for a in acts]
        for s, y in enumerate(ys):
            for p in range(N_PIECE):
                y_ref[p, s * sub:(s + 1) * sub] = _pack_halves(y[:, 2 * p * PIECE_W:2 * (p + 1) * PIECE_W])


def _experts(xs, blk_exp, n_used, w_gate, w_up, w_down):
    n_rows = xs.shape[1]
    n_blocks = n_rows // EXPERT_BM

    def live(i, nu):
        return jnp.minimum(i, nu[0] - 1)

    grid_spec = pltpu.PrefetchScalarGridSpec(
        num_scalar_prefetch=2,
        grid=(n_blocks,),
        in_specs=[pl.BlockSpec((N_PIECE, EXPERT_BM, PIECE_W), lambda i, be, nu: (0, live(i, nu), 0)),
                  pl.BlockSpec((1, D_MODEL, D_EXPERT), lambda i, be, nu: (be[live(i, nu)], 0, 0)),
                  pl.BlockSpec((1, D_MODEL, D_EXPERT), lambda i, be, nu: (be[live(i, nu)], 0, 0)),
                  pl.BlockSpec((1, D_EXPERT, D_MODEL), lambda i, be, nu: (be[live(i, nu)], 0, 0))],
        out_specs=pl.BlockSpec((N_PIECE, EXPERT_BM, PIECE_W), lambda i, be, nu: (0, live(i, nu), 0)),
        scratch_shapes=[pltpu.VMEM((D_MODEL, 2 * D_EXPERT), BF16), pltpu.VMEM((D_EXPERT, D_MODEL), BF16)],
    )
    return pl.pallas_call(
        _expert_kernel,
        grid_spec=grid_spec,
        out_shape=jax.ShapeDtypeStruct((N_PIECE, n_rows, PIECE_W), jnp.uint32),
        compiler_params=_cparams(("arbitrary",)),
        name="experts",
    )(blk_exp, n_used, xs, w_gate, w_up, w_down)


SC_WINDOW = 128
V7X_SC_CORES = 2
V7X_SC_SUBCORES = 16


def _sc_mesh():
    return plsc.VectorSubcoreMesh(core_axis_name="core", subcore_axis_name="subcore",
                                  num_cores=V7X_SC_CORES, num_subcores=V7X_SC_SUBCORES)


def _sc_scatter_rows(src, dest, n_rows, seg, repeat):
    d = src.shape[1]
    n_idx = dest.shape[0]
    seg_blocks = seg // SC_WINDOW
    dest2 = dest.reshape(1, n_idx)

    def src_block(i):
        return ((i // (repeat * seg_blocks)) * seg_blocks + i % seg_blocks, 0)

    @functools.partial(pl.kernel, out_type=jax.ShapeDtypeStruct((n_rows, d), src.dtype), mesh=_sc_mesh(),
                       scratch_types=[], name="sc_dispatch")
    def run(src_hbm, idx_hbm, out_hbm):
        def body(rows_vmem, idx_vmem):
            pltpu.sync_copy(rows_vmem, out_hbm.at[idx_vmem.at[0]])

        pltpu.emit_pipeline(
            body,
            grid=(n_idx // SC_WINDOW,),
            in_specs=[pl.BlockSpec((SC_WINDOW, d), src_block),
                      pl.BlockSpec((1, SC_WINDOW), lambda i: (0, i))],
            out_specs=[],
            core_axis_name=("core", "subcore"),
            dimension_semantics=(pltpu.PARALLEL,),
        )(src_hbm, idx_hbm)

    return run(src, dest2)


def _sc_gather_rows(table, idx):
    d = table.shape[1]
    n_idx = idx.shape[0]
    idx2 = idx.reshape(1, n_idx)

    @functools.partial(pl.kernel, out_type=jax.ShapeDtypeStruct((n_idx, d), table.dtype), mesh=_sc_mesh(),
                       scratch_types=[], name="sc_combine_gather")
    def run(table_hbm, idx_hbm, out_hbm):
        def body(idx_vmem, rows_vmem):
            pltpu.sync_copy(table_hbm.at[idx_vmem.at[0]], rows_vmem)

        pltpu.emit_pipeline(
            body,
            grid=(n_idx // SC_WINDOW,),
            in_specs=[pl.BlockSpec((1, SC_WINDOW), lambda i: (0, i))],
            out_specs=[pl.BlockSpec((SC_WINDOW, d), lambda i: (i, 0))],
            core_axis_name=("core", "subcore"),
            dimension_semantics=(pltpu.PARALLEL,),
        )(idx_hbm, out_hbm)

    return run(table, idx2)


def _combine_kernel(yg_ref, wts_ref, h_ref, wgu_ref, wd_ref, g_ref, b_ref, outa_ref, outb_ref, *, alpha, tiles_a):
    wts = wts_ref[...]
    acc = None
    for k in range(TOP_K):
        cols = _load_pieces([yg_ref[p, k] for p in range(N_PIECE)])
        wk = wts[:, k:k + 1]
        acc = [c * wk for c in cols] if acc is None else [a + c * wk for a, c in zip(acc, cols)]
    routed = jnp.concatenate(acc, axis=1)
    h = h_ref[...]
    gu = _dot(h, wgu_ref[...])
    shared = _dot(_silu(gu[:, :D_EXPERT]) * gu[:, D_EXPERT:], wd_ref[...])
    y = _layer_norm(alpha * h + (routed + shared), g_ref[...], b_ref[...])
    i = pl.program_id(0)

    @pl.when(i < tiles_a)
    def _():
        outa_ref[...] = y

    @pl.when(i >= tiles_a)
    def _():
        outb_ref[...] = y


def _combine(yg, wts, h, w_sh_gu, w_sh_down, ln_g, ln_b, alpha, tm, t_a):
    t = h.shape[0]
    tiles_a = t_a // tm
    row = lambda i: (i, 0)
    const2 = lambda i: (0, 0)
    return pl.pallas_call(
        functools.partial(_combine_kernel, alpha=alpha, tiles_a=tiles_a),
        grid=(t // tm,),
        in_specs=[pl.BlockSpec((N_PIECE, TOP_K, tm, PIECE_W), lambda i: (0, 0, i, 0)),
                  pl.BlockSpec((tm, TOP_K), row),
                  pl.BlockSpec((tm, D_MODEL), row),
                  pl.BlockSpec(w_sh_gu.shape, const2),
                  pl.BlockSpec(w_sh_down.shape, const2),
                  pl.BlockSpec((1, D_MODEL), const2),
                  pl.BlockSpec((1, D_MODEL), const2)],
        out_specs=[pl.BlockSpec((tm, D_MODEL), lambda i: (jnp.minimum(i, tiles_a - 1), 0)),
                   pl.BlockSpec((tm, D_MODEL), lambda i: (jnp.maximum(i - tiles_a, 0), 0))],
        out_shape=[jax.ShapeDtypeStruct((t_a, D_MODEL), F32), jax.ShapeDtypeStruct((t - t_a, D_MODEL), F32)],
        compiler_params=_cparams(("arbitrary",)),
        name="combine_ln2",
    )(yg, wts, h, w_sh_gu, w_sh_down, ln_g, ln_b)


def _layer(xp, xs, s_delta, s_conv, s_pool, w_in, conv_w, a_log, dt_bias, gamma_a, w_br_a, w_pool,
           pool_scale, w_br_b, w_out, ln1_g, ln1_b, w_router, router_bias, w_exp_gate, w_exp_up,
           w_exp_down, w_sh_gate, w_sh_up, w_sh_down, ln2_g, ln2_b, alpha):
    Bp, Lp, _ = xp.shape
    Bs, Ls, _ = xs.shape
    Tp, Ts = Bp * Lp, Bs * Ls
    T = Tp + Ts
    x_p = xp.reshape(Tp, D_MODEL)
    x_s = xs.reshape(Ts, D_MODEL)

    o_z, o_b, o_a, o_u = QKV_W, QKV_W + V_W, QKV_W + V_W + H_V, QKV_W + V_W + 2 * H_V
    w_main = jnp.concatenate([w_in[:, :o_b], w_in[:, o_u:]], axis=1).astype(BF16)
    w_ba = jnp.pad(w_in[:, o_b:o_u], ((0, 0), (0, 128 - 2 * H_V))).astype(BF16)
    proj, ba = _in_proj(x_p, x_s, w_main, w_ba, 1024, 512)

    o_gated, sd_p = _gdn(proj, ba, conv_w, a_log, dt_bias, gamma_a, 0, Bp, Lp, 64, 1)
    buf8 = jnp.pad(s_conv, ((0, 0), (8 - (CONV_W - 1), 0), (0, 0)))
    o_gated, sd_s = _gdn(proj, ba, conv_w, a_log, dt_bias, gamma_a, Tp, Bs, Ls, Ls, GDN_STATE_SEQS,
                         buf8=buf8, s0=s_delta, o_all=o_gated)

    pooled = _pool(proj, 0, Bp, Lp, 256, 1, 0)
    buf16 = jnp.pad(s_pool, ((0, 0), (1, 0), (0, 0)))
    pooled = _pool(proj, Tp, Bs, Ls, Ls, POOL_STATE_SEQS, PAST_LEN, buf16=buf16, pooled_all=pooled)

    h, hp = _post(pooled, o_gated, proj, x_p, x_s, w_pool.astype(BF16), pool_scale.reshape(1, D_MODEL),
                  w_br_b.astype(BF16), w_br_a.astype(BF16), w_out.astype(BF16),
                  ln1_g.reshape(1, D_MODEL), ln1_b.reshape(1, D_MODEL), alpha, 256)

    idx_t, wts_t, rank_t, cnt = _router(h, w_router.T, router_bias.reshape(N_EXPERTS, 1), 512)
    counts = cnt[:, 0].astype(jnp.int32)
    padded = ((counts + EXPERT_BM - 1) // EXPERT_BM) * EXPERT_BM
    pends = jnp.cumsum(padded)
    pstarts = pends - padded
    n_blocks = (T * TOP_K + N_EXPERTS * (EXPERT_BM - 1) + EXPERT_BM - 1) // EXPERT_BM
    n_rows = n_blocks * EXPERT_BM
    piece_rows = _dest(idx_t, rank_t, pstarts.astype(F32).reshape(N_EXPERTS, 1), n_rows, 512)
    blk_start = jnp.arange(n_blocks, dtype=jnp.int32) * EXPERT_BM
    blk_exp = jnp.minimum(jnp.sum((pends[None, :] <= blk_start[:, None]).astype(jnp.int32), axis=1),
                          N_EXPERTS - 1)
    n_used = (pends[-1:] // EXPERT_BM).astype(jnp.int32)

    piece_idx = piece_rows.reshape(N_PIECE * TOP_K * T)
    x_sorted = _sc_scatter_rows(hp.reshape(N_PIECE * T, PIECE_W), piece_idx, N_PIECE * n_rows, T, TOP_K)
    y_sorted = _experts(x_sorted.reshape(N_PIECE, n_rows, PIECE_W), blk_exp, n_used,
                        w_exp_gate, w_exp_up, w_exp_down)
    yg = _sc_gather_rows(y_sorted.reshape(N_PIECE * n_rows, PIECE_W), piece_idx)
    yg = yg.reshape(N_PIECE, TOP_K, T, PIECE_W)

    w_sh_gu = jnp.concatenate([w_sh_gate, w_sh_up], axis=1).astype(BF16)
    y_p, y_s = _combine(yg, wts_t.T, h, w_sh_gu, w_sh_down.astype(BF16),
                        ln2_g.reshape(1, D_MODEL), ln2_b.reshape(1, D_MODEL), alpha, 256, Tp)

    keep = max(POOL_BUF, CONV_W - 1)

    def last_rows(x2d, B, L):
        n = min(L, keep)
        rows = jnp.arange(B, dtype=jnp.int32)[:, None] * L + (L - n) + jnp.arange(n, dtype=jnp.int32)
        return jnp.take(x2d, rows.reshape(-1), axis=0).astype(BF16), n

    xt_p, n_p = last_rows(x_p, Bp, Lp)
    xt_s, n_s = last_rows(x_s, Bs, Ls)
    n_tail = Bp * n_p + Bs * n_s
    n_pad = -(-n_tail // 128) * 128
    xt = jnp.pad(jnp.concatenate([xt_p, xt_s], axis=0), ((0, n_pad - n_tail), (0, 0)))
    w_tail = jnp.concatenate([w_in[:, :QKV_W], w_in[:, o_u:o_u + D_MODEL]], axis=1).astype(BF16)
    tail = _matmul(xt, w_tail, n_pad, 512, F32, "in_proj_state_rows")
    tail_p = tail[:Bp * n_p].reshape(Bp, n_p, QKV_W + D_MODEL)
    tail_s = tail[Bp * n_p:n_tail].reshape(Bs, n_s, QKV_W + D_MODEL)

    def new_buffers(rows, old_conv, old_pool):
        conv = jnp.concatenate([old_conv, rows[:, :, :QKV_W]], axis=1)[:, -(CONV_W - 1):]
        pool = jnp.concatenate([old_pool, rows[:, :, QKV_W:]], axis=1)[:, -POOL_BUF:]
        return conv, pool

    new_conv_p, new_pool_p = new_buffers(tail_p, jnp.zeros((Bp, CONV_W - 1, QKV_W), F32),
                                         jnp.zeros((Bp, POOL_BUF, D_MODEL), F32))
    new_conv_s, new_pool_s = new_buffers(tail_s, s_conv, s_pool)
    return (y_p.reshape(Bp, Lp, D_MODEL), y_s.reshape(Bs, Ls, D_MODEL),
            sd_p, new_conv_p, new_pool_p, sd_s, new_conv_s, new_pool_s)


def kernel(x_prompt, x_sample, state_delta, state_conv, state_pool, w_in, conv_w, a_log, dt_bias, gamma_a,
           w_br_a, w_pool, pool_scale, w_br_b, w_out, ln1_g, ln1_b, w_router, router_bias,
           w_exp_gate, w_exp_up, w_exp_down, w_sh_gate, w_sh_up, w_sh_down, ln2_g, ln2_b):
    depth = w_in.shape[0]
    alpha = (2 * depth) ** 0.25
    yp, ys = x_prompt, x_sample
    outs = [[] for _ in range(6)]
    for l in range(depth):
        res = _layer(yp, ys, state_delta[l], state_conv[l], state_pool[l], w_in[l], conv_w[l], a_log[l],
                     dt_bias[l], gamma_a[l], w_br_a[l], w_pool[l], pool_scale[l], w_br_b[l], w_out[l],
                     ln1_g[l], ln1_b[l], w_router[l], router_bias[l], w_exp_gate[l], w_exp_up[l],
                     w_exp_down[l], w_sh_gate[l], w_sh_up[l], w_sh_down[l], ln2_g[l], ln2_b[l], alpha)
        yp, ys = res[0], res[1]
        for lst, v in zip(outs, res[2:]):
            lst.append(v)
    return (yp, ys) + tuple(jnp.stack(v) for v in outs)
```

```python
import functools
import math

import jax
import jax.numpy as jnp
from jax import lax
from jax.experimental import pallas as pl
from jax.experimental.pallas import tpu as pltpu
from jax.experimental.pallas import tpu_sc as plsc

F32 = jnp.float32
BF16 = jnp.bfloat16

D_MODEL = 1024
H_K = 8
D_K = 128
H_V = 16
D_V = 128
Q_W = H_K * D_K
V_W = H_V * D_V
QKV_W = 2 * Q_W + V_W
CONV_W = 4
POOL_WINDOWS = (2, 4, 8, 16)
POOL_GROUP_W = D_MODEL // len(POOL_WINDOWS)
POOL_BUF = max(POOL_WINDOWS) - 1
N_EXPERTS = 256
TOP_K = 8
N_GROUP = 8
TOPK_GROUP = 4
GROUP_SZ = N_EXPERTS // N_GROUP
D_EXPERT = D_MODEL // 4
ROUTED_SCALE = 2.5
LN_EPS = 1e-5
NORM_EPS = 1e-6
PAST_LEN = 16384

MAIN_W = QKV_W + V_W + 3 * D_MODEL
Z_BLK = QKV_W // V_W
U_BLK = (QKV_W + V_W) // D_MODEL

STACK_ROWS = 128
SOLVE_BASE = 16
EXPERT_BM = 512
GDN_STATE_SEQS = 4
POOL_STATE_SEQS = 16
VMEM_LIMIT = 56 * 1024 * 1024


def _cparams(sem):
    return pltpu.CompilerParams(dimension_semantics=sem, vmem_limit_bytes=VMEM_LIMIT)


def _sigmoid(x):
    return 0.5 * jnp.tanh(0.5 * x) + 0.5


def _silu(x):
    return x * _sigmoid(x)


def _softplus(x):
    return jnp.maximum(x, 0.0) + jnp.log(1.0 + jnp.exp(-jnp.abs(x)))


def _dot(a, b):
    return jnp.dot(a.astype(BF16), b.astype(BF16), preferred_element_type=F32)


def _dot_nt(a, b):
    return lax.dot_general(a.astype(BF16), b.astype(BF16), (((1,), (1,)), ((), ())),
                           preferred_element_type=F32)


def _dot_tn(a, b):
    return lax.dot_general(a.astype(BF16), b.astype(BF16), (((0,), (0,)), ((), ())),
                           preferred_element_type=F32)


def _split(a):
    hi = a.astype(BF16)
    lo = (a - hi.astype(F32)).astype(BF16)
    return hi, lo


def _dot3_nt(a, b):
    ah, al = _split(a)
    bh, bl = _split(b)
    d = functools.partial(lax.dot_general, dimension_numbers=(((1,), (1,)), ((), ())),
                          preferred_element_type=F32)
    return d(ah, bh) + (d(ah, bl) + d(al, bh))


def _pack_halves(x):
    n = x.shape[1] // 2
    hi = lax.bitcast_convert_type(x[:, :n].astype(BF16).astype(F32), jnp.uint32)
    lo = lax.bitcast_convert_type(x[:, n:].astype(BF16).astype(F32), jnp.uint32)
    return (hi & jnp.uint32(0xFFFF0000)) | (lo >> 16)


def _unpack_halves(w):
    hi = lax.bitcast_convert_type(w & jnp.uint32(0xFFFF0000), F32)
    lo = lax.bitcast_convert_type(w << 16, F32)
    return hi, lo


N_PIECE = 2
PIECE_W = D_MODEL // (2 * N_PIECE)


def _store_pieces(ref, x):
    for p in range(N_PIECE):
        ref[p] = _pack_halves(x[:, 2 * p * PIECE_W:2 * (p + 1) * PIECE_W])


def _load_pieces(pieces):
    cols = []
    for w in pieces:
        cols.extend(_unpack_halves(w))
    return cols


def _layer_norm(x, g, b):
    mu = jnp.mean(x, axis=-1, keepdims=True)
    xc = x - mu
    var = jnp.mean(xc * xc, axis=-1, keepdims=True)
    return xc * lax.rsqrt(var + LN_EPS) * g + b


def _in_proj_kernel(xa_ref, xb_ref, w_ref, wba_ref, o_ref, ba_ref, x_scr, *, tiles_a):
    @pl.when(pl.program_id(1) == 0)
    def _():
        x = jnp.where(pl.program_id(0) < tiles_a, xa_ref[...], xb_ref[...]).astype(BF16)
        x_scr[...] = x
        ba_ref[...] = jnp.dot(x, wba_ref[...], preferred_element_type=F32)

    o_ref[...] = jnp.dot(x_scr[...], w_ref[...], preferred_element_type=F32).astype(o_ref.dtype)


def _in_proj(x_a, x_b, w_main, w_ba, tm, tn):
    k = x_a.shape[1]
    t = x_a.shape[0] + x_b.shape[0]
    n = w_main.shape[1]
    tiles_a = x_a.shape[0] // tm
    return pl.pallas_call(
        functools.partial(_in_proj_kernel, tiles_a=tiles_a),
        grid=(t // tm, n // tn),
        in_specs=[pl.BlockSpec((tm, k), lambda i, j: (jnp.minimum(i, tiles_a - 1), 0)),
                  pl.BlockSpec((tm, k), lambda i, j: (jnp.maximum(i - tiles_a, 0), 0)),
                  pl.BlockSpec((k, tn), lambda i, j: (0, j)),
                  pl.BlockSpec(w_ba.shape, lambda i, j: (0, 0))],
        out_specs=[pl.BlockSpec((tm, tn), lambda i, j: (i, j)),
                   pl.BlockSpec((tm, w_ba.shape[1]), lambda i, j: (i, 0))],
        out_shape=[jax.ShapeDtypeStruct((t, n), BF16), jax.ShapeDtypeStruct((t, w_ba.shape[1]), F32)],
        scratch_shapes=[pltpu.VMEM((tm, k), BF16)],
        compiler_params=_cparams(("parallel", "arbitrary")),
        name="in_proj",
    )(x_a, x_b, w_main, w_ba)


def _mm_kernel(x_ref, w_ref, o_ref):
    o_ref[...] = jnp.dot(x_ref[...], w_ref[...], preferred_element_type=F32).astype(o_ref.dtype)


def _matmul(x, w, tm, tn, out_dtype, name):
    t, k = x.shape
    n = w.shape[1]
    return pl.pallas_call(
        _mm_kernel,
        grid=(t // tm, n // tn),
        in_specs=[pl.BlockSpec((tm, k), lambda i, j: (i, 0)),
                  pl.BlockSpec((k, tn), lambda i, j: (0, j))],
        out_specs=pl.BlockSpec((tm, tn), lambda i, j: (i, j)),
        out_shape=jax.ShapeDtypeStruct((t, n), out_dtype),
        compiler_params=_cparams(("parallel", "parallel")),
        name=name,
    )(x, w)


def _gdn_kernel(*refs, C, hg, nseq, has_state):
    if has_state:
        (qkv_ref, z_ref, ba_ref, cw_ref, ad_ref, gam_ref, buf_ref, s0_ref, _,
         o_ref, sout_ref, s_scr, tail_scr, act_scr) = refs
    else:
        (qkv_ref, z_ref, ba_ref, cw_ref, ad_ref, gam_ref,
         o_ref, sout_ref, s_scr, tail_scr, act_scr) = refs
    R = hg * C
    ng = H_V // hg
    log2c = int(math.log2(C))
    c = pl.program_id(1)

    @pl.when(c == 0)
    def _init():
        if has_state:
            for s in range(nseq):
                s_scr[s * H_V:(s + 1) * H_V] = s0_ref[s]
                tail_scr[s * 8:(s + 1) * 8] = buf_ref[s]
        else:
            s_scr[...] = jnp.zeros(s_scr.shape, F32)
            tail_scr[...] = jnp.zeros(tail_scr.shape, F32)

    cw = cw_ref[...]
    qkv_all = qkv_ref[...].astype(F32)
    n_prev = CONV_W - 1
    if not has_state:
        ext_pad = 128
        sr = lax.broadcasted_iota(jnp.int32, (n_prev * C, ext_pad), 0)
        sc = lax.broadcasted_iota(jnp.int32, (n_prev * C, ext_pad), 1)
        shift_sel = jnp.zeros((n_prev * C, ext_pad), F32)
        for j in range(n_prev):
            in_tap = (sr >= j * C) & (sr < (j + 1) * C)
            shift_sel = jnp.where(in_tap & (sc == sr - j * C + 8 - n_prev + j), 1.0, shift_sel)
        shift_sel = shift_sel.astype(BF16)
    for s in range(nseq):
        x = qkv_all[s * C:(s + 1) * C]
        tail = tail_scr[s * 8:(s + 1) * 8]
        if has_state:
            xe = jnp.concatenate([tail, x], axis=0)
            y = x * cw[n_prev:CONV_W]
            for j in range(n_prev):
                y = y + xe[8 - n_prev + j:8 - n_prev + j + C] * cw[j:j + 1]
        else:
            xe = jnp.concatenate([tail, x, jnp.zeros((ext_pad - 8 - C, QKV_W), F32)], axis=0).astype(BF16)
            moved = jnp.dot(shift_sel, xe, preferred_element_type=F32)
            y = x * cw[n_prev:CONV_W]
            for j in range(n_prev):
                y = y + moved[j * C:(j + 1) * C] * cw[j:j + 1]
        tail_scr[s * 8:(s + 1) * 8] = x[C - 8:C]
        act_scr[s * C:(s + 1) * C] = _silu(y)

    ri = lax.broadcasted_iota(jnp.int32, (R, R), 0)
    ci = lax.broadcasted_iota(jnp.int32, (R, R), 1)
    same = (ri >> log2c) == (ci >> log2c)
    incl = same & (ri >= ci)
    strict = same & (ri > ci)
    eye = ri == ci
    is_last = ci == (((ri >> log2c) << log2c) + (C - 1))
    eye_f = jnp.where(eye, 1.0, 0.0).astype(F32)
    cum_u = jnp.where(same & (ri <= ci), 1.0, 0.0).astype(BF16)

    nrow = ba_ref.shape[2] // 2
    adv = ad_ref[...]
    dd = functools.partial(jnp.dot, preferred_element_type=F32)
    beta_seq, G_seq = [], []
    for s in range(nseq):
        bav = ba_ref[s, 0]
        beta_seq.append(_sigmoid(bav[0:nrow]))
        g_rows = -jnp.exp(adv[0:nrow]) * _softplus(bav[nrow:2 * nrow] + adv[nrow:2 * nrow])
        g1 = g_rows.astype(BF16)
        r1 = g_rows - g1.astype(F32)
        g2 = r1.astype(BF16)
        g3 = (r1 - g2.astype(F32)).astype(BF16)
        G_seq.append(dd(g1, cum_u) + (dd(g2, cum_u) + dd(g3, cum_u)))

    gam = gam_ref[...]
    normed = {}

    def qk_head(s, kind, kh):
        key = (s, kind, kh)
        if key not in normed:
            off = (0 if kind == "q" else Q_W) + kh * D_K
            v = act_scr[s * C:(s + 1) * C, off:off + D_K]
            v = v * lax.rsqrt(jnp.sum(v * v, axis=-1, keepdims=True) + NORM_EPS)
            if kind == "q":
                v = v * (D_K ** -0.5)
            normed[key] = v
        return normed[key]

    rep = H_V // H_K
    grp = []
    for s in range(nseq):
        rows = slice(s * C, (s + 1) * C)
        for g in range(ng):
            heads = [g * hg + hh for hh in range(hg)]
            qst = jnp.concatenate([qk_head(s, "q", h // rep) for h in heads], axis=0)
            kst = jnp.concatenate([qk_head(s, "k", h // rep) for h in heads], axis=0)
            vst = jnp.concatenate([act_scr[rows, 2 * Q_W + h * D_V:2 * Q_W + (h + 1) * D_V] for h in heads], axis=0)
            grow = G_seq[s][g:g + 1]
            brow = beta_seq[s][g:g + 1]
            gcol = jnp.sum(eye_f * grow, axis=1, keepdims=True)
            bcol = jnp.sum(eye_f * brow, axis=1, keepdims=True)
            glast = jnp.sum(jnp.where(is_last, grow, 0.0), axis=1, keepdims=True)
            dm = jnp.where(incl, jnp.exp(jnp.minimum(gcol - grow, 0.0)), 0.0)
            eg = jnp.exp(gcol)
            grp.append(dict(rows=rows, s0=s * H_V, heads=heads, qst=qst, kst=kst, gcol=gcol, bcol=bcol,
                            glast=glast, dm=dm, eg=eg,
                            x=jnp.concatenate([vst * bcol, kst * (bcol * eg)], axis=1)))

    for d in grp:
        d["kk"] = _dot_nt(d["kst"], d["kst"])
        d["qk"] = _dot_nt(d["qst"], d["kst"])
    log2b = min(log2c, int(math.log2(SOLVE_BASE)))
    same_base = ((ri ^ ci) >> log2b) == 0
    for d in grp:
        d["m"] = jnp.where(strict, -(d["kk"] * d["bcol"] * d["dm"]), 0.0)
        d["attn"] = jnp.where(incl, d["qk"] * d["dm"], 0.0)
        d["pw"] = jnp.where(same_base, d["m"], 0.0)
        d["t"] = eye_f + d["pw"]

    for r in range(log2b):
        last = r == log2b - 1
        for d in grp:
            if r == 0:
                if not last:
                    d["pw"] = _dot(d["pw"], d["pw"])
            elif last:
                d["t"] = d["t"] + _dot(d["pw"], d["t"])
            else:
                out = _dot(d["pw"], jnp.concatenate([d["pw"], d["t"]], axis=1))
                d["pw"] = out[:, :R]
                d["t"] = d["t"] + out[:, R:]
    for lev in range(log2b + 1, log2c + 1):
        coupling = ((ri ^ ci) >> (lev - 1)) == 1
        for d in grp:
            d["y"] = _dot(d["t"], jnp.where(coupling, d["m"], 0.0))
        for d in grp:
            d["t"] = d["t"] + _dot(d["y"], d["t"])
    for d in grp:
        d["x"] = _dot(d["t"], d["x"])

    for d in grp:
        qe = d["qst"] * d["eg"]
        d["xs"] = [_dot(jnp.concatenate([d["x"][hh * C:(hh + 1) * C, D_V:], qe[hh * C:(hh + 1) * C]], axis=0),
                        s_scr[d["s0"] + h]) for hh, h in enumerate(d["heads"])]
    for d in grp:
        d["vnew"] = jnp.concatenate(
            [d["x"][hh * C:(hh + 1) * C, :D_V] - xs[:C] for hh, xs in enumerate(d["xs"])], axis=0)
        qs = jnp.concatenate([xs[C:] for xs in d["xs"]], axis=0)
        d["o"] = qs + _dot(d["attn"], d["vnew"])

    row_head = lax.broadcasted_iota(jnp.int32, (R, D_V), 0) >> log2c
    for d in grp:
        kt = d["kst"] * jnp.exp(d["glast"] - d["gcol"])
        egl = jnp.exp(d["glast"])
        for hh, h in enumerate(d["heads"]):
            sl = slice(hh * C, (hh + 1) * C)
            if C >= 16:
                upd = _dot_tn(kt[sl], d["vnew"][sl])
            else:
                upd = _dot_tn(kt, jnp.where(row_head == hh, d["vnew"], 0.0))
            s_scr[d["s0"] + h] = s_scr[d["s0"] + h] * egl[hh * C:hh * C + 1] + upd

    z_all = z_ref[...].astype(F32)
    for d in grp:
        o = d["o"]
        zst = jnp.concatenate([z_all[d["rows"], h * D_V:(h + 1) * D_V] for h in d["heads"]], axis=0)
        on = o * lax.rsqrt(jnp.mean(o * o, axis=-1, keepdims=True) + NORM_EPS) * gam * _silu(zst)
        for hh, h in enumerate(d["heads"]):
            o_ref[d["rows"], h * D_V:(h + 1) * D_V] = on[hh * C:(hh + 1) * C].astype(o_ref.dtype)

    @pl.when(c == pl.num_programs(1) - 1)
    def _fin():
        for s in range(nseq):
            sout_ref[s] = s_scr[s * H_V:(s + 1) * H_V]


def _gdn(proj, ba, conv_w, a_log, dt_bias, gamma_a, row0, B, L, C, nseq, buf8=None, s0=None, o_all=None):
    has_state = s0 is not None
    hg = STACK_ROWS // C
    ng = H_V // hg
    nrow = 16
    nc = L // C
    R = STACK_ROWS
    assert nseq == 1 or nc == 1
    rows_blk = nseq * C

    def arrange(v):
        v = v.reshape(B, nc, C, ng, hg).transpose(0, 1, 3, 4, 2).reshape(B, nc, ng, R)
        return jnp.pad(v, ((0, 0), (0, 0), (0, nrow - ng), (0, 0)))

    rows = ba[row0:row0 + B * L]
    ba_arr = jnp.concatenate([arrange(rows[:, :H_V]), arrange(rows[:, H_V:2 * H_V])], axis=2)

    def arrange_param(p):
        v = jnp.broadcast_to(p.reshape(ng, hg, 1), (ng, hg, C)).reshape(ng, R)
        return jnp.pad(v, ((0, nrow - ng), (0, 0)))

    ad = jnp.concatenate([arrange_param(a_log), arrange_param(dt_bias)], axis=0)
    rb0 = row0 // rows_blk
    in_specs = [
        pl.BlockSpec((rows_blk, QKV_W), lambda b, c: (rb0 + b * nc + c, 0)),
        pl.BlockSpec((rows_blk, V_W), lambda b, c: (rb0 + b * nc + c, Z_BLK)),
        pl.BlockSpec((nseq, 1, 2 * nrow, R), lambda b, c: (b, c, 0, 0)),
        pl.BlockSpec((CONV_W, QKV_W), lambda b, c: (0, 0)),
        pl.BlockSpec((2 * nrow, R), lambda b, c: (0, 0)),
        pl.BlockSpec((1, D_V), lambda b, c: (0, 0)),
    ]
    args = [proj, proj, ba_arr, conv_w, ad, gamma_a.reshape(1, D_V)]
    aliases = {}
    if has_state:
        in_specs += [pl.BlockSpec((nseq, 8, QKV_W), lambda b, c: (b, 0, 0)),
                     pl.BlockSpec((nseq, H_V, D_K, D_V), lambda b, c: (b, 0, 0, 0)),
                     pl.BlockSpec(memory_space=pl.ANY)]
        args += [buf8, s0, o_all]
        aliases = {len(args) - 1: 0}
    return pl.pallas_call(
        functools.partial(_gdn_kernel, C=C, hg=hg, nseq=nseq, has_state=has_state),
        grid=(B // nseq, nc),
        in_specs=in_specs,
        out_specs=[pl.BlockSpec((rows_blk, V_W), lambda b, c: (rb0 + b * nc + c, 0)),
                   pl.BlockSpec((nseq, H_V, D_K, D_V), lambda b, c: (b, 0, 0, 0))],
        out_shape=[jax.ShapeDtypeStruct((proj.shape[0], V_W), BF16),
                   jax.ShapeDtypeStruct((B, H_V, D_K, D_V), F32)],
        scratch_shapes=[pltpu.VMEM((nseq * H_V, D_K, D_V), F32),
                        pltpu.VMEM((nseq * 8, QKV_W), F32),
                        pltpu.VMEM((rows_blk, QKV_W), F32)],
        input_output_aliases=aliases,
        compiler_params=_cparams(("parallel", "arbitrary")),
        name="gdn_state" if has_state else "gdn_fresh",
    )(*args)


def _pool_kernel(*refs, tc, nseq, start_pos, has_state):
    if has_state:
        u_ref, buf_ref, _, o_ref, tail_scr = refs
    else:
        u_ref, o_ref, tail_scr = refs
    hist = POOL_BUF + 1
    c = pl.program_id(1)

    @pl.when(c == 0)
    def _init():
        if has_state:
            for s in range(nseq):
                tail_scr[s * hist:(s + 1) * hist] = buf_ref[s]
        else:
            tail_scr[...] = jnp.zeros(tail_scr.shape, F32)

    u_all = u_ref[...].astype(F32)
    pos1 = (start_pos + 1 + c * tc + lax.broadcasted_iota(jnp.int32, (tc, 1), 0)).astype(F32)
    for s in range(nseq):
        u = u_all[s * tc:(s + 1) * tc]
        xe = jnp.concatenate([tail_scr[s * hist:(s + 1) * hist], u], axis=0)
        tail_scr[s * hist:(s + 1) * hist] = xe[tc:tc + hist]
        for gi, win in enumerate(POOL_WINDOWS):
            sl = slice(gi * POOL_GROUP_W, (gi + 1) * POOL_GROUP_W)
            acc = xe[:, sl]
            shift = 1
            while shift < win:
                acc = acc + pltpu.roll(acc, shift, 0)
                shift *= 2
            cnt = jnp.minimum(float(win), pos1)
            o_ref[s * tc:(s + 1) * tc, sl] = acc[hist:hist + tc] / cnt - u[:, sl]


def _pool(proj, row0, B, L, tc, nseq, start_pos, buf16=None, pooled_all=None):
    has_state = buf16 is not None
    nc = L // tc
    assert nseq == 1 or nc == 1
    rows_blk = nseq * tc
    rb0 = row0 // rows_blk
    in_specs = [pl.BlockSpec((rows_blk, D_MODEL), lambda b, c: (rb0 + b * nc + c, U_BLK))]
    args = [proj]
    aliases = {}
    if has_state:
        in_specs += [pl.BlockSpec((nseq, POOL_BUF + 1, D_MODEL), lambda b, c: (b, 0, 0)),
                     pl.BlockSpec(memory_space=pl.ANY)]
        args += [buf16, pooled_all]
        aliases = {len(args) - 1: 0}
    return pl.pallas_call(
        functools.partial(_pool_kernel, tc=tc, nseq=nseq, start_pos=start_pos, has_state=has_state),
        grid=(B // nseq, nc),
        in_specs=in_specs,
        out_specs=pl.BlockSpec((rows_blk, D_MODEL), lambda b, c: (rb0 + b * nc + c, 0)),
        out_shape=jax.ShapeDtypeStruct((proj.shape[0], D_MODEL), F32),
        scratch_shapes=[pltpu.VMEM((nseq * (POOL_BUF + 1), D_MODEL), F32)],
        input_output_aliases=aliases,
        compiler_params=_cparams(("parallel", "arbitrary")),
        name="pool_state" if has_state else "pool_fresh",
    )(*args)


def _post_kernel(pooled_ref, o_ref, ga_ref, gb_ref, xa_ref, xb_ref, wp_ref, ps_ref, wbb_ref, wba_ref, wo_ref,
                 g_ref, b_ref, h_ref, hp_ref, *, alpha, tiles_a):
    x = jnp.where(pl.program_id(0) < tiles_a, xa_ref[...], xb_ref[...])
    pooled = pooled_ref[...]
    mixed = jnp.concatenate(
        [_dot(pooled[:, gi * POOL_GROUP_W:(gi + 1) * POOL_GROUP_W], wp_ref[gi]) for gi in range(len(POOL_WINDOWS))],
        axis=1) * ps_ref[...]
    branch_b = _dot(mixed, wbb_ref[...])
    branch_a = _dot(o_ref[...], wba_ref[...])
    merged = _sigmoid(ga_ref[...].astype(F32)) * branch_a + _sigmoid(gb_ref[...].astype(F32)) * branch_b
    h = _layer_norm(alpha * x + _dot(merged, wo_ref[...]), g_ref[...], b_ref[...])
    h_ref[...] = h
    _store_pieces(hp_ref, h)


def _post(pooled, o_gated, proj, x_a, x_b, w_pool, pool_scale, w_br_b, w_br_a, w_out, ln_g, ln_b, alpha, tm):
    t = x_a.shape[0] + x_b.shape[0]
    tiles_a = x_a.shape[0] // tm
    row = lambda i: (i, 0)
    const2 = lambda i: (0, 0)
    return pl.pallas_call(
        functools.partial(_post_kernel, alpha=alpha, tiles_a=tiles_a),
        grid=(t // tm,),
        in_specs=[pl.BlockSpec((tm, D_MODEL), row),
                  pl.BlockSpec((tm, V_W), row),
                  pl.BlockSpec((tm, D_MODEL), lambda i: (i, U_BLK + 1)),
                  pl.BlockSpec((tm, D_MODEL), lambda i: (i, U_BLK + 2)),
                  pl.BlockSpec((tm, D_MODEL), lambda i: (jnp.minimum(i, tiles_a - 1), 0)),
                  pl.BlockSpec((tm, D_MODEL), lambda i: (jnp.maximum(i - tiles_a, 0), 0)),
                  pl.BlockSpec(w_pool.shape, lambda i: (0, 0, 0)),
                  pl.BlockSpec((1, D_MODEL), const2),
                  pl.BlockSpec(w_br_b.shape, const2),
                  pl.BlockSpec(w_br_a.shape, const2),
                  pl.BlockSpec(w_out.shape, const2),
                  pl.BlockSpec((1, D_MODEL), const2),
                  pl.BlockSpec((1, D_MODEL), const2)],
        out_specs=[pl.BlockSpec((tm, D_MODEL), row), pl.BlockSpec((N_PIECE, tm, PIECE_W), lambda i: (0, i, 0))],
        out_shape=[jax.ShapeDtypeStruct((t, D_MODEL), F32),
                   jax.ShapeDtypeStruct((N_PIECE, t, PIECE_W), jnp.uint32)],
        compiler_params=_cparams(("parallel",)),
        name="post_mixers",
    )(pooled, o_gated, proj, proj, x_a, x_b, w_pool, pool_scale, w_br_b, w_br_a, w_out, ln_g, ln_b)


def _router_kernel(h_ref, wt_ref, bias_ref, idx_ref, wts_ref, rank_ref, cnt_ref, carry_scr, *, tm):
    i = pl.program_id(0)

    @pl.when(i == 0)
    def _init():
        carry_scr[...] = jnp.zeros(carry_scr.shape, F32)

    logits = _dot3_nt(wt_ref[...], h_ref[...])
    sc = _sigmoid(logits)
    ch = sc + bias_ref[...]
    neg = -jnp.inf
    e_in = lax.broadcasted_iota(jnp.int32, (GROUP_SZ, tm), 0)
    gs_rows = []
    for g in range(N_GROUP):
        blk = ch[g * GROUP_SZ:(g + 1) * GROUP_SZ]
        m1 = jnp.max(blk, axis=0, keepdims=True)
        i1 = jnp.min(jnp.where(blk == m1, e_in, GROUP_SZ), axis=0, keepdims=True)
        m2 = jnp.max(jnp.where(e_in == i1, neg, blk), axis=0, keepdims=True)
        gs_rows.append(m1 + m2)
    gs = jnp.concatenate(gs_rows, axis=0)
    g_io = lax.broadcasted_iota(jnp.int32, (N_GROUP, tm), 0)
    e_io = lax.broadcasted_iota(jnp.int32, (N_EXPERTS, tm), 0)
    e_grp = e_io >> int(math.log2(GROUP_SZ))
    masked = jnp.full((N_EXPERTS, tm), neg, F32)
    for _ in range(TOPK_GROUP):
        mx = jnp.max(gs, axis=0, keepdims=True)
        gi = jnp.min(jnp.where(gs == mx, g_io, N_GROUP), axis=0, keepdims=True)
        gs = jnp.where(g_io == gi, neg, gs)
        masked = jnp.where(e_grp == gi, ch, masked)
    idx_rows, w_rows = [], []
    onehot = jnp.zeros((N_EXPERTS, tm), F32)
    for _ in range(TOP_K):
        mx = jnp.max(masked, axis=0, keepdims=True)
        ei = jnp.min(jnp.where(masked == mx, e_io, N_EXPERTS), axis=0, keepdims=True)
        hit = e_io == ei
        idx_rows.append(ei)
        w_rows.append(jnp.sum(jnp.where(hit, sc, 0.0), axis=0, keepdims=True))
        onehot = jnp.where(hit, 1.0, onehot)
        masked = jnp.where(hit, neg, masked)
    wsel = jnp.concatenate(w_rows, axis=0)
    wts_ref[...] = wsel / jnp.sum(wsel, axis=0, keepdims=True) * ROUTED_SCALE
    idx_ref[...] = jnp.concatenate(idx_rows, axis=0)

    tr = lax.broadcasted_iota(jnp.int32, (tm, tm), 0)
    tc_ = lax.broadcasted_iota(jnp.int32, (tm, tm), 1)
    before = jnp.where(tr < tc_, 1.0, 0.0).astype(BF16)
    cum = jnp.dot(onehot.astype(BF16), before, preferred_element_type=F32) + carry_scr[...]
    rank_ref[...] = jnp.concatenate(
        [jnp.sum(jnp.where(e_io == ei, cum, 0.0), axis=0, keepdims=True) for ei in idx_rows],
        axis=0).astype(jnp.int32)
    carry_scr[...] = carry_scr[...] + jnp.sum(onehot, axis=1, keepdims=True)
    cnt_ref[...] = carry_scr[...]


def _router(h, w_router_t, bias_col, tm):
    t = h.shape[0]
    return pl.pallas_call(
        functools.partial(_router_kernel, tm=tm),
        grid=(t // tm,),
        in_specs=[pl.BlockSpec((tm, D_MODEL), lambda i: (i, 0)),
                  pl.BlockSpec((N_EXPERTS, D_MODEL), lambda i: (0, 0)),
                  pl.BlockSpec((N_EXPERTS, 1), lambda i: (0, 0))],
        out_specs=[pl.BlockSpec((TOP_K, tm), lambda i: (0, i)),
                   pl.BlockSpec((TOP_K, tm), lambda i: (0, i)),
                   pl.BlockSpec((TOP_K, tm), lambda i: (0, i)),
                   pl.BlockSpec((N_EXPERTS, 1), lambda i: (0, 0))],
        out_shape=[jax.ShapeDtypeStruct((TOP_K, t), jnp.int32),
                   jax.ShapeDtypeStruct((TOP_K, t), F32),
                   jax.ShapeDtypeStruct((TOP_K, t), jnp.int32),
                   jax.ShapeDtypeStruct((N_EXPERTS, 1), F32)],
        scratch_shapes=[pltpu.VMEM((N_EXPERTS, 1), F32)],
        compiler_params=_cparams(("arbitrary",)),
        name="router",
    )(h, w_router_t, bias_col)


def _dest_kernel(idx_ref, rank_ref, start_ref, dest_ref, *, tm, n_rows):
    e_io = lax.broadcasted_iota(jnp.int32, (N_EXPERTS, tm), 0)
    starts = start_ref[...]
    rows = []
    for k in range(TOP_K):
        seg = jnp.sum(jnp.where(e_io == idx_ref[k:k + 1, :], starts, 0.0), axis=0, keepdims=True)
        rows.append(seg.astype(jnp.int32) + rank_ref[k:k + 1, :])
    base = jnp.concatenate(rows, axis=0)
    for p in range(N_PIECE):
        dest_ref[p] = base + p * n_rows


def _dest(idx_t, rank_t, seg_start_col, n_rows, tm):
    t = idx_t.shape[1]
    blk = pl.BlockSpec((TOP_K, tm), lambda i: (0, i))
    return pl.pallas_call(
        functools.partial(_dest_kernel, tm=tm, n_rows=n_rows),
        grid=(t // tm,),
        in_specs=[blk, blk, pl.BlockSpec((N_EXPERTS, 1), lambda i: (0, 0))],
        out_specs=pl.BlockSpec((N_PIECE, TOP_K, tm), lambda i: (0, 0, i)),
        out_shape=jax.ShapeDtypeStruct((N_PIECE, TOP_K, t), jnp.int32),
        compiler_params=_cparams(("parallel",)),
        name="dispatch_rows",
    )(idx_t, rank_t, seg_start_col)


def _expert_kernel(be_ref, nu_ref, x_ref, wg_ref, wu_ref, wd_ref, y_ref, wgu_scr, wd_scr):
    i = pl.program_id(0)
    live = i < nu_ref[0]
    new_expert = (i == 0) | (be_ref[i] != be_ref[jnp.maximum(i - 1, 0)])

    @pl.when(live & new_expert)
    def _cache_weights():
        wgu_scr[:, :D_EXPERT] = wg_ref[0].astype(BF16)
        wgu_scr[:, D_EXPERT:] = wu_ref[0].astype(BF16)
        wd_scr[...] = wd_ref[0].astype(BF16)

    @pl.when(live)
    def _():
        n_sub = 2
        sub = EXPERT_BM // n_sub
        xs = [jnp.concatenate([c.astype(BF16) for c in
                               _load_pieces([x_ref[p, s * sub:(s + 1) * sub] for p in range(N_PIECE)])], axis=1)
              for s in range(n_sub)]
        gus = [jnp.dot(x, wgu_scr[...], preferred_element_type=F32) for x in xs]
        acts = [(_silu(gu[:, :D_EXPERT]) * gu[:, D_EXPERT:]).astype(BF16) for gu in gus]
        ys = [jnp.dot(a, wd_scr[...], preferred_element_type=F32) for a in acts]
        for s, y in enumerate(ys):
            for p in range(N_PIECE):
                y_ref[p, s * sub:(s + 1) * sub] = _pack_halves(y[:, 2 * p * PIECE_W:2 * (p + 1) * PIECE_W])


def _experts(xs, blk_exp, n_used, w_gate, w_up, w_down):
    n_rows = xs.shape[1]
    n_blocks = n_rows // EXPERT_BM

    def live(i, nu):
        return jnp.minimum(i, nu[0] - 1)

    grid_spec = pltpu.PrefetchScalarGridSpec(
        num_scalar_prefetch=2,
        grid=(n_blocks,),
        in_specs=[pl.BlockSpec((N_PIECE, EXPERT_BM, PIECE_W), lambda i, be, nu: (0, live(i, nu), 0)),
                  pl.BlockSpec((1, D_MODEL, D_EXPERT), lambda i, be, nu: (be[live(i, nu)], 0, 0)),
                  pl.BlockSpec((1, D_MODEL, D_EXPERT), lambda i, be, nu: (be[live(i, nu)], 0, 0)),
                  pl.BlockSpec((1, D_EXPERT, D_MODEL), lambda i, be, nu: (be[live(i, nu)], 0, 0))],
        out_specs=pl.BlockSpec((N_PIECE, EXPERT_BM, PIECE_W), lambda i, be, nu: (0, live(i, nu), 0)),
        scratch_shapes=[pltpu.VMEM((D_MODEL, 2 * D_EXPERT), BF16), pltpu.VMEM((D_EXPERT, D_MODEL), BF16)],
    )
    return pl.pallas_call(
        _expert_kernel,
        grid_spec=grid_spec,
        out_shape=jax.ShapeDtypeStruct((N_PIECE, n_rows, PIECE_W), jnp.uint32),
        compiler_params=_cparams(("arbitrary",)),
        name="experts",
    )(blk_exp, n_used, xs, w_gate, w_up, w_down)


SC_WINDOW = 128
V7X_SC_CORES = 2
V7X_SC_SUBCORES = 16


def _sc_mesh():
    return plsc.VectorSubcoreMesh(core_axis_name="core", subcore_axis_name="subcore",
                                  num_cores=V7X_SC_CORES, num_subcores=V7X_SC_SUBCORES)


def _sc_scatter_rows(src, dest, n_rows, seg, repeat):
    d = src.shape[1]
    n_idx = dest.shape[0]
    seg_blocks = seg // SC_WINDOW
    dest2 = dest.reshape(1, n_idx)

    def src_block(i):
        return ((i // (repeat * seg_blocks)) * seg_blocks + i % seg_blocks, 0)

    @functools.partial(pl.kernel, out_type=jax.ShapeDtypeStruct((n_rows, d), src.dtype), mesh=_sc_mesh(),
                       scratch_types=[], name="sc_dispatch")
    def run(src_hbm, idx_hbm, out_hbm):
        def body(rows_vmem, idx_vmem):
            pltpu.sync_copy(rows_vmem, out_hbm.at[idx_vmem.at[0]])

        pltpu.emit_pipeline(
            body,
            grid=(n_idx // SC_WINDOW,),
            in_specs=[pl.BlockSpec((SC_WINDOW, d), src_block),
                      pl.BlockSpec((1, SC_WINDOW), lambda i: (0, i))],
            out_specs=[],
            core_axis_name=("core", "subcore"),
            dimension_semantics=(pltpu.PARALLEL,),
        )(src_hbm, idx_hbm)

    return run(src, dest2)


def _sc_gather_rows(table, idx):
    d = table.shape[1]
    n_idx = idx.shape[0]
    idx2 = idx.reshape(1, n_idx)

    @functools.partial(pl.kernel, out_type=jax.ShapeDtypeStruct((n_idx, d), table.dtype), mesh=_sc_mesh(),
                       scratch_types=[], name="sc_combine_gather")
    def run(table_hbm, idx_hbm, out_hbm):
        def body(idx_vmem, rows_vmem):
            pltpu.sync_copy(table_hbm.at[idx_vmem.at[0]], rows_vmem)

        pltpu.emit_pipeline(
            body,
            grid=(n_idx // SC_WINDOW,),
            in_specs=[pl.BlockSpec((1, SC_WINDOW), lambda i: (0, i))],
            out_specs=[pl.BlockSpec((SC_WINDOW, d), lambda i: (i, 0))],
            core_axis_name=("core", "subcore"),
            dimension_semantics=(pltpu.PARALLEL,),
        )(idx_hbm, out_hbm)

    return run(table, idx2)


def _combine_kernel(yg_ref, wts_ref, h_ref, wgu_ref, wd_ref, g_ref, b_ref, outa_ref, outb_ref, *, alpha, tiles_a):
    wts = wts_ref[...]
    acc = None
    for k in range(TOP_K):
        cols = _load_pieces([yg_ref[p, k] for p in range(N_PIECE)])
        wk = wts[:, k:k + 1]
        acc = [c * wk for c in cols] if acc is None else [a + c * wk for a, c in zip(acc, cols)]
    routed = jnp.concatenate(acc, axis=1)
    h = h_ref[...]
    gu = _dot(h, wgu_ref[...])
    shared = _dot(_silu(gu[:, :D_EXPERT]) * gu[:, D_EXPERT:], wd_ref[...])
    y = _layer_norm(alpha * h + (routed + shared), g_ref[...], b_ref[...])
    i = pl.program_id(0)

    @pl.when(i < tiles_a)
    def _():
        outa_ref[...] = y

    @pl.when(i >= tiles_a)
    def _():
        outb_ref[...] = y


def _combine(yg, wts, h, w_sh_gu, w_sh_down, ln_g, ln_b, alpha, tm, t_a):
    t = h.shape[0]
    tiles_a = t_a // tm
    row = lambda i: (i, 0)
    const2 = lambda i: (0, 0)
    return pl.pallas_call(
        functools.partial(_combine_kernel, alpha=alpha, tiles_a=tiles_a),
        grid=(t // tm,),
        in_specs=[pl.BlockSpec((N_PIECE, TOP_K, tm, PIECE_W), lambda i: (0, 0, i, 0)),
                  pl.BlockSpec((tm, TOP_K), row),
                  pl.BlockSpec((tm, D_MODEL), row),
                  pl.BlockSpec(w_sh_gu.shape, const2),
                  pl.BlockSpec(w_sh_down.shape, const2),
                  pl.BlockSpec((1, D_MODEL), const2),
                  pl.BlockSpec((1, D_MODEL), const2)],
        out_specs=[pl.BlockSpec((tm, D_MODEL), lambda i: (jnp.minimum(i, tiles_a - 1), 0)),
                   pl.BlockSpec((tm, D_MODEL), lambda i: (jnp.maximum(i - tiles_a, 0), 0))],
        out_shape=[jax.ShapeDtypeStruct((t_a, D_MODEL), F32), jax.ShapeDtypeStruct((t - t_a, D_MODEL), F32)],
        compiler_params=_cparams(("arbitrary",)),
        name="combine_ln2",
    )(yg, wts, h, w_sh_gu, w_sh_down, ln_g, ln_b)


def _layer(xp, xs, s_delta, s_conv, s_pool, w_in, conv_w, a_log, dt_bias, gamma_a, w_br_a, w_pool,
           pool_scale, w_br_b, w_out, ln1_g, ln1_b, w_router, router_bias, w_exp_gate, w_exp_up,
           w_exp_down, w_sh_gate, w_sh_up, w_sh_down, ln2_g, ln2_b, alpha):
    Bp, Lp, _ = xp.shape
    Bs, Ls, _ = xs.shape
    Tp, Ts = Bp * Lp, Bs * Ls
    T = Tp + Ts
    x_p = xp.reshape(Tp, D_MODEL)
    x_s = xs.reshape(Ts, D_MODEL)

    o_z, o_b, o_a, o_u = QKV_W, QKV_W + V_W, QKV_W + V_W + H_V, QKV_W + V_W + 2 * H_V
    w_main = jnp.concatenate([w_in[:, :o_b], w_in[:, o_u:]], axis=1).astype(BF16)
    w_ba = jnp.pad(w_in[:, o_b:o_u], ((0, 0), (0, 128 - 2 * H_V))).astype(BF16)
    proj, ba = _in_proj(x_p, x_s, w_main, w_ba, 1024, 1024)

    o_gated, sd_p = _gdn(proj, ba, conv_w, a_log, dt_bias, gamma_a, 0, Bp, Lp, 64, 1)
    buf8 = jnp.pad(s_conv, ((0, 0), (8 - (CONV_W - 1), 0), (0, 0)))
    o_gated, sd_s = _gdn(proj, ba, conv_w, a_log, dt_bias, gamma_a, Tp, Bs, Ls, Ls, GDN_STATE_SEQS,
                         buf8=buf8, s0=s_delta, o_all=o_gated)

    pooled = _pool(proj, 0, Bp, Lp, 256, 1, 0)
    buf16 = jnp.pad(s_pool, ((0, 0), (1, 0), (0, 0)))
    pooled = _pool(proj, Tp, Bs, Ls, Ls, POOL_STATE_SEQS, PAST_LEN, buf16=buf16, pooled_all=pooled)

    h, hp = _post(pooled, o_gated, proj, x_p, x_s, w_pool.astype(BF16), pool_scale.reshape(1, D_MODEL),
                  w_br_b.astype(BF16), w_br_a.astype(BF16), w_out.astype(BF16),
                  ln1_g.reshape(1, D_MODEL), ln1_b.reshape(1, D_MODEL), alpha, 256)

    idx_t, wts_t, rank_t, cnt = _router(h, w_router.T, router_bias.reshape(N_EXPERTS, 1), 512)
    counts = cnt[:, 0].astype(jnp.int32)
    padded = ((counts + EXPERT_BM - 1) // EXPERT_BM) * EXPERT_BM
    pends = jnp.cumsum(padded)
    pstarts = pends - padded
    n_blocks = (T * TOP_K + N_EXPERTS * (EXPERT_BM - 1) + EXPERT_BM - 1) // EXPERT_BM
    n_rows = n_blocks * EXPERT_BM
    piece_rows = _dest(idx_t, rank_t, pstarts.astype(F32).reshape(N_EXPERTS, 1), n_rows, 512)
    blk_start = jnp.arange(n_blocks, dtype=jnp.int32) * EXPERT_BM
    blk_exp = jnp.minimum(jnp.sum((pends[None, :] <= blk_start[:, None]).astype(jnp.int32), axis=1),
                          N_EXPERTS - 1)
    n_used = (pends[-1:] // EXPERT_BM).astype(jnp.int32)

    piece_idx = piece_rows.reshape(N_PIECE * TOP_K * T)
    x_sorted = _sc_scatter_rows(hp.reshape(N_PIECE * T, PIECE_W), piece_idx, N_PIECE * n_rows, T, TOP_K)
    y_sorted = _experts(x_sorted.reshape(N_PIECE, n_rows, PIECE_W), blk_exp, n_used,
                        w_exp_gate, w_exp_up, w_exp_down)
    yg = _sc_gather_rows(y_sorted.reshape(N_PIECE * n_rows, PIECE_W), piece_idx)
    yg = yg.reshape(N_PIECE, TOP_K, T, PIECE_W)

    w_sh_gu = jnp.concatenate([w_sh_gate, w_sh_up], axis=1).astype(BF16)
    y_p, y_s = _combine(yg, wts_t.T, h, w_sh_gu, w_sh_down.astype(BF16),
                        ln2_g.reshape(1, D_MODEL), ln2_b.reshape(1, D_MODEL), alpha, 256, Tp)

    keep = max(POOL_BUF, CONV_W - 1)

    def last_rows(x2d, B, L):
        n = min(L, keep)
        rows = jnp.arange(B, dtype=jnp.int32)[:, None] * L + (L - n) + jnp.arange(n, dtype=jnp.int32)
        return jnp.take(x2d, rows.reshape(-1), axis=0).astype(BF16), n

    xt_p, n_p = last_rows(x_p, Bp, Lp)
    xt_s, n_s = last_rows(x_s, Bs, Ls)
    n_tail = Bp * n_p + Bs * n_s
    n_pad = -(-n_tail // 128) * 128
    xt = jnp.pad(jnp.concatenate([xt_p, xt_s], axis=0), ((0, n_pad - n_tail), (0, 0)))
    w_tail = jnp.concatenate([w_in[:, :QKV_W], w_in[:, o_u:o_u + D_MODEL]], axis=1).astype(BF16)
    tail = _matmul(xt, w_tail, n_pad, 512, F32, "in_proj_state_rows")
    tail_p = tail[:Bp * n_p].reshape(Bp, n_p, QKV_W + D_MODEL)
    tail_s = tail[Bp * n_p:n_tail].reshape(Bs, n_s, QKV_W + D_MODEL)

    def new_buffers(rows, old_conv, old_pool):
        conv = jnp.concatenate([old_conv, rows[:, :, :QKV_W]], axis=1)[:, -(CONV_W - 1):]
        pool = jnp.concatenate([old_pool, rows[:, :, QKV_W:]], axis=1)[:, -POOL_BUF:]
        return conv, pool

    new_conv_p, new_pool_p = new_buffers(tail_p, jnp.zeros((Bp, CONV_W - 1, QKV_W), F32),
                                         jnp.zeros((Bp, POOL_BUF, D_MODEL), F32))
    new_conv_s, new_pool_s = new_buffers(tail_s, s_conv, s_pool)
    return (y_p.reshape(Bp, Lp, D_MODEL), y_s.reshape(Bs, Ls, D_MODEL),
            sd_p, new_conv_p, new_pool_p, sd_s, new_conv_s, new_pool_s)


def kernel(x_prompt, x_sample, state_delta, state_conv, state_pool, w_in, conv_w, a_log, dt_bias, gamma_a,
           w_br_a, w_pool, pool_scale, w_br_b, w_out, ln1_g, ln1_b, w_router, router_bias,
           w_exp_gate, w_exp_up, w_exp_down, w_sh_gate, w_sh_up, w_sh_down, ln2_g, ln2_b):
    depth = w_in.shape[0]
    alpha = (2 * depth) ** 0.25
    yp, ys = x_prompt, x_sample
    outs = [[] for _ in range(6)]
    for l in range(depth):
        res = _layer(yp, ys, state_delta[l], state_conv[l], state_pool[l], w_in[l], conv_w[l], a_log[l],
                     dt_bias[l], gamma_a[l], w_br_a[l], w_pool[l], pool_scale[l], w_br_b[l], w_out[l],
                     ln1_g[l], ln1_b[l], w_router[l], router_bias[l], w_exp_gate[l], w_exp_up[l],
                     w_exp_down[l], w_sh_gate[l], w_sh_up[l], w_sh_down[l], ln2_g[l], ln2_b[l], alpha)
        yp, ys = res[0], res[1]
        for lst, v in zip(outs, res[2:]):
            lst.append(v)
    return (yp, ys) + tuple(jnp.stack(v) for v in outs)
```

```python
import functools
import math

import jax
import jax.numpy as jnp
from jax import lax
from jax.experimental import pallas as pl
from jax.experimental.pallas import tpu as pltpu
from jax.experimental.pallas import tpu_sc as plsc

F32 = jnp.float32
BF16 = jnp.bfloat16

D_MODEL = 1024
H_K = 8
D_K = 128
H_V = 16
D_V = 128
Q_W = H_K * D_K
V_W = H_V * D_V
QKV_W = 2 * Q_W + V_W
CONV_W = 4
POOL_WINDOWS = (2, 4, 8, 16)
POOL_GROUP_W = D_MODEL // len(POOL_WINDOWS)
POOL_BUF = max(POOL_WINDOWS) - 1
N_EXPERTS = 256
TOP_K = 8
N_GROUP = 8
TOPK_GROUP = 4
GROUP_SZ = N_EXPERTS // N_GROUP
D_EXPERT = D_MODEL // 4
ROUTED_SCALE = 2.5
LN_EPS = 1e-5
NORM_EPS = 1e-6
PAST_LEN = 16384

MAIN_W = QKV_W + V_W + 3 * D_MODEL
Z_BLK = QKV_W // V_W
U_BLK = (QKV_W + V_W) // D_MODEL

STACK_ROWS = 128
SOLVE_BASE = 16
GDN_WAVE = 4
EXPERT_BM = 512
GDN_STATE_SEQS = 4
POOL_STATE_SEQS = 16
VMEM_LIMIT = 56 * 1024 * 1024


def _cparams(sem):
    return pltpu.CompilerParams(dimension_semantics=sem, vmem_limit_bytes=VMEM_LIMIT)


def _sigmoid(x):
    return 0.5 * jnp.tanh(0.5 * x) + 0.5


def _silu(x):
    return x * _sigmoid(x)


def _softplus(x):
    return jnp.maximum(x, 0.0) + jnp.log(1.0 + jnp.exp(-jnp.abs(x)))


def _dot(a, b):
    return jnp.dot(a.astype(BF16), b.astype(BF16), preferred_element_type=F32)


def _dot_nt(a, b):
    return lax.dot_general(a.astype(BF16), b.astype(BF16), (((1,), (1,)), ((), ())),
                           preferred_element_type=F32)


def _dot_tn(a, b):
    return lax.dot_general(a.astype(BF16), b.astype(BF16), (((0,), (0,)), ((), ())),
                           preferred_element_type=F32)


def _split(a):
    hi = a.astype(BF16)
    lo = (a - hi.astype(F32)).astype(BF16)
    return hi, lo


def _dot3_nt(a, b):
    ah, al = _split(a)
    bh, bl = _split(b)
    d = functools.partial(lax.dot_general, dimension_numbers=(((1,), (1,)), ((), ())),
                          preferred_element_type=F32)
    return d(ah, bh) + (d(ah, bl) + d(al, bh))


def _pack_halves(x):
    n = x.shape[1] // 2
    hi = lax.bitcast_convert_type(x[:, :n].astype(BF16).astype(F32), jnp.uint32)
    lo = lax.bitcast_convert_type(x[:, n:].astype(BF16).astype(F32), jnp.uint32)
    return (hi & jnp.uint32(0xFFFF0000)) | (lo >> 16)


def _unpack_halves(w):
    hi = lax.bitcast_convert_type(w & jnp.uint32(0xFFFF0000), F32)
    lo = lax.bitcast_convert_type(w << 16, F32)
    return hi, lo


N_PIECE = 2
PIECE_W = D_MODEL // (2 * N_PIECE)


def _store_pieces(ref, x):
    for p in range(N_PIECE):
        ref[p] = _pack_halves(x[:, 2 * p * PIECE_W:2 * (p + 1) * PIECE_W])


def _load_pieces(pieces):
    cols = []
    for w in pieces:
        cols.extend(_unpack_halves(w))
    return cols


def _layer_norm(x, g, b):
    mu = jnp.mean(x, axis=-1, keepdims=True)
    xc = x - mu
    var = jnp.mean(xc * xc, axis=-1, keepdims=True)
    return xc * lax.rsqrt(var + LN_EPS) * g + b


def _in_proj_kernel(xa_ref, xb_ref, w_ref, wba_ref, o_ref, ba_ref, x_scr, *, tiles_a):
    @pl.when(pl.program_id(1) == 0)
    def _():
        x = jnp.where(pl.program_id(0) < tiles_a, xa_ref[...], xb_ref[...]).astype(BF16)
        x_scr[...] = x
        ba_ref[...] = jnp.dot(x, wba_ref[...], preferred_element_type=F32)

    o_ref[...] = jnp.dot(x_scr[...], w_ref[...], preferred_element_type=F32).astype(o_ref.dtype)


def _in_proj(x_a, x_b, w_main, w_ba, tm, tn):
    k = x_a.shape[1]
    t = x_a.shape[0] + x_b.shape[0]
    n = w_main.shape[1]
    tiles_a = x_a.shape[0] // tm
    return pl.pallas_call(
        functools.partial(_in_proj_kernel, tiles_a=tiles_a),
        grid=(t // tm, n // tn),
        in_specs=[pl.BlockSpec((tm, k), lambda i, j: (jnp.minimum(i, tiles_a - 1), 0)),
                  pl.BlockSpec((tm, k), lambda i, j: (jnp.maximum(i - tiles_a, 0), 0)),
                  pl.BlockSpec((k, tn), lambda i, j: (0, j)),
                  pl.BlockSpec(w_ba.shape, lambda i, j: (0, 0))],
        out_specs=[pl.BlockSpec((tm, tn), lambda i, j: (i, j)),
                   pl.BlockSpec((tm, w_ba.shape[1]), lambda i, j: (i, 0))],
        out_shape=[jax.ShapeDtypeStruct((t, n), BF16), jax.ShapeDtypeStruct((t, w_ba.shape[1]), F32)],
        scratch_shapes=[pltpu.VMEM((tm, k), BF16)],
        compiler_params=_cparams(("parallel", "arbitrary")),
        name="in_proj",
    )(x_a, x_b, w_main, w_ba)


def _mm_kernel(x_ref, w_ref, o_ref):
    o_ref[...] = jnp.dot(x_ref[...], w_ref[...], preferred_element_type=F32).astype(o_ref.dtype)


def _matmul(x, w, tm, tn, out_dtype, name):
    t, k = x.shape
    n = w.shape[1]
    return pl.pallas_call(
        _mm_kernel,
        grid=(t // tm, n // tn),
        in_specs=[pl.BlockSpec((tm, k), lambda i, j: (i, 0)),
                  pl.BlockSpec((k, tn), lambda i, j: (0, j))],
        out_specs=pl.BlockSpec((tm, tn), lambda i, j: (i, j)),
        out_shape=jax.ShapeDtypeStruct((t, n), out_dtype),
        compiler_params=_cparams(("parallel", "parallel")),
        name=name,
    )(x, w)


def _gdn_kernel(*refs, C, hg, nseq, has_state):
    if has_state:
        (qkv_ref, z_ref, ba_ref, cw_ref, ad_ref, gam_ref, buf_ref, s0_ref, _,
         o_ref, sout_ref, s_scr, tail_scr, act_scr) = refs
    else:
        (qkv_ref, z_ref, ba_ref, cw_ref, ad_ref, gam_ref,
         o_ref, sout_ref, s_scr, tail_scr, act_scr) = refs
    R = hg * C
    ng = H_V // hg
    log2c = int(math.log2(C))
    c = pl.program_id(1)

    @pl.when(c == 0)
    def _init():
        if has_state:
            for s in range(nseq):
                s_scr[s * H_V:(s + 1) * H_V] = s0_ref[s]
                tail_scr[s * 8:(s + 1) * 8] = buf_ref[s]
        else:
            s_scr[...] = jnp.zeros(s_scr.shape, F32)
            tail_scr[...] = jnp.zeros(tail_scr.shape, F32)

    cw = cw_ref[...]
    qkv_all = qkv_ref[...].astype(F32)
    n_prev = CONV_W - 1
    if not has_state:
        ext_pad = 128
        sr = lax.broadcasted_iota(jnp.int32, (n_prev * C, ext_pad), 0)
        sc = lax.broadcasted_iota(jnp.int32, (n_prev * C, ext_pad), 1)
        shift_sel = jnp.zeros((n_prev * C, ext_pad), F32)
        for j in range(n_prev):
            in_tap = (sr >= j * C) & (sr < (j + 1) * C)
            shift_sel = jnp.where(in_tap & (sc == sr - j * C + 8 - n_prev + j), 1.0, shift_sel)
        shift_sel = shift_sel.astype(BF16)
    for s in range(nseq):
        x = qkv_all[s * C:(s + 1) * C]
        tail = tail_scr[s * 8:(s + 1) * 8]
        if has_state:
            xe = jnp.concatenate([tail, x], axis=0)
            y = x * cw[n_prev:CONV_W]
            for j in range(n_prev):
                y = y + xe[8 - n_prev + j:8 - n_prev + j + C] * cw[j:j + 1]
        else:
            xe = jnp.concatenate([tail, x, jnp.zeros((ext_pad - 8 - C, QKV_W), F32)], axis=0).astype(BF16)
            moved = jnp.dot(shift_sel, xe, preferred_element_type=F32)
            y = x * cw[n_prev:CONV_W]
            for j in range(n_prev):
                y = y + moved[j * C:(j + 1) * C] * cw[j:j + 1]
        tail_scr[s * 8:(s + 1) * 8] = x[C - 8:C]
        act_scr[s * C:(s + 1) * C] = _silu(y)

    ri = lax.broadcasted_iota(jnp.int32, (R, R), 0)
    ci = lax.broadcasted_iota(jnp.int32, (R, R), 1)
    same = (ri >> log2c) == (ci >> log2c)
    incl = same & (ri >= ci)
    strict = same & (ri > ci)
    eye = ri == ci
    is_last = ci == (((ri >> log2c) << log2c) + (C - 1))
    eye_f = jnp.where(eye, 1.0, 0.0).astype(F32)
    cum_u = jnp.where(same & (ri <= ci), 1.0, 0.0).astype(BF16)

    nrow = ba_ref.shape[2] // 2
    adv = ad_ref[...]
    dd = functools.partial(jnp.dot, preferred_element_type=F32)
    beta_seq, G_seq = [], []
    for s in range(nseq):
        bav = ba_ref[s, 0]
        beta_seq.append(_sigmoid(bav[0:nrow]))
        g_rows = -jnp.exp(adv[0:nrow]) * _softplus(bav[nrow:2 * nrow] + adv[nrow:2 * nrow])
        g1 = g_rows.astype(BF16)
        r1 = g_rows - g1.astype(F32)
        g2 = r1.astype(BF16)
        g3 = (r1 - g2.astype(F32)).astype(BF16)
        G_seq.append(dd(g1, cum_u) + (dd(g2, cum_u) + dd(g3, cum_u)))

    gam = gam_ref[...]
    normed = {}

    def qk_head(s, kind, kh):
        key = (s, kind, kh)
        if key not in normed:
            off = (0 if kind == "q" else Q_W) + kh * D_K
            v = act_scr[s * C:(s + 1) * C, off:off + D_K]
            v = v * lax.rsqrt(jnp.sum(v * v, axis=-1, keepdims=True) + NORM_EPS)
            if kind == "q":
                v = v * (D_K ** -0.5)
            normed[key] = v
        return normed[key]

    rep = H_V // H_K
    log2b = min(log2c, int(math.log2(SOLVE_BASE)))
    same_base = ((ri ^ ci) >> log2b) == 0
    row_head = lax.broadcasted_iota(jnp.int32, (R, D_V), 0) >> log2c
    z_all = z_ref[...].astype(F32)
    specs = [(s, g) for s in range(nseq) for g in range(ng)]
    for w0 in range(0, len(specs), GDN_WAVE):
        grp = []
        for s, g in specs[w0:w0 + GDN_WAVE]:
            rows = slice(s * C, (s + 1) * C)
            heads = [g * hg + hh for hh in range(hg)]
            qst = jnp.concatenate([qk_head(s, "q", h // rep) for h in heads], axis=0)
            kst = jnp.concatenate([qk_head(s, "k", h // rep) for h in heads], axis=0)
            vst = jnp.concatenate([act_scr[rows, 2 * Q_W + h * D_V:2 * Q_W + (h + 1) * D_V] for h in heads], axis=0)
            grow = G_seq[s][g:g + 1]
            brow = beta_seq[s][g:g + 1]
            gcol = jnp.sum(eye_f * grow, axis=1, keepdims=True)
            bcol = jnp.sum(eye_f * brow, axis=1, keepdims=True)
            glast = jnp.sum(jnp.where(is_last, grow, 0.0), axis=1, keepdims=True)
            dm = jnp.where(incl, jnp.exp(jnp.minimum(gcol - grow, 0.0)), 0.0)
            eg = jnp.exp(gcol)
            grp.append(dict(rows=rows, s0=s * H_V, heads=heads, qst=qst, kst=kst, gcol=gcol, bcol=bcol,
                            glast=glast, dm=dm, eg=eg,
                            x=jnp.concatenate([vst * bcol, kst * (bcol * eg)], axis=1)))

        for d in grp:
            d["kk"] = _dot_nt(d["kst"], d["kst"])
            d["qk"] = _dot_nt(d["qst"], d["kst"])
        for d in grp:
            d["m"] = jnp.where(strict, -(d["kk"] * d["bcol"] * d["dm"]), 0.0)
            d["attn"] = jnp.where(incl, d["qk"] * d["dm"], 0.0)
            d["pw"] = jnp.where(same_base, d["m"], 0.0)
            d["t"] = eye_f + d["pw"]

        for r in range(log2b):
            last = r == log2b - 1
            for d in grp:
                if r == 0:
                    if not last:
                        d["pw"] = _dot(d["pw"], d["pw"])
                elif last:
                    d["t"] = d["t"] + _dot(d["pw"], d["t"])
                else:
                    out = _dot(d["pw"], jnp.concatenate([d["pw"], d["t"]], axis=1))
                    d["pw"] = out[:, :R]
                    d["t"] = d["t"] + out[:, R:]
        for lev in range(log2b + 1, log2c + 1):
            coupling = ((ri ^ ci) >> (lev - 1)) == 1
            for d in grp:
                d["y"] = _dot(d["t"], jnp.where(coupling, d["m"], 0.0))
            for d in grp:
                d["t"] = d["t"] + _dot(d["y"], d["t"])
        for d in grp:
            d["x"] = _dot(d["t"], d["x"])

        for d in grp:
            qe = d["qst"] * d["eg"]
            d["xs"] = [_dot(jnp.concatenate([d["x"][hh * C:(hh + 1) * C, D_V:], qe[hh * C:(hh + 1) * C]], axis=0),
                            s_scr[d["s0"] + h]) for hh, h in enumerate(d["heads"])]
        for d in grp:
            d["vnew"] = jnp.concatenate(
                [d["x"][hh * C:(hh + 1) * C, :D_V] - xs[:C] for hh, xs in enumerate(d["xs"])], axis=0)
            qs = jnp.concatenate([xs[C:] for xs in d["xs"]], axis=0)
            d["o"] = qs + _dot(d["attn"], d["vnew"])

        for d in grp:
            kt = d["kst"] * jnp.exp(d["glast"] - d["gcol"])
            egl = jnp.exp(d["glast"])
            for hh, h in enumerate(d["heads"]):
                sl = slice(hh * C, (hh + 1) * C)
                if C >= 16:
                    upd = _dot_tn(kt[sl], d["vnew"][sl])
                else:
                    upd = _dot_tn(kt, jnp.where(row_head == hh, d["vnew"], 0.0))
                s_scr[d["s0"] + h] = s_scr[d["s0"] + h] * egl[hh * C:hh * C + 1] + upd

        for d in grp:
            o = d["o"]
            zst = jnp.concatenate([z_all[d["rows"], h * D_V:(h + 1) * D_V] for h in d["heads"]], axis=0)
            on = o * lax.rsqrt(jnp.mean(o * o, axis=-1, keepdims=True) + NORM_EPS) * gam * _silu(zst)
            for hh, h in enumerate(d["heads"]):
                o_ref[d["rows"], h * D_V:(h + 1) * D_V] = on[hh * C:(hh + 1) * C].astype(o_ref.dtype)

    @pl.when(c == pl.num_programs(1) - 1)
    def _fin():
        for s in range(nseq):
            sout_ref[s] = s_scr[s * H_V:(s + 1) * H_V]


def _gdn(proj, ba, conv_w, a_log, dt_bias, gamma_a, row0, B, L, C, nseq, buf8=None, s0=None, o_all=None):
    has_state = s0 is not None
    hg = STACK_ROWS // C
    ng = H_V // hg
    nrow = 16
    nc = L // C
    R = STACK_ROWS
    assert nseq == 1 or nc == 1
    rows_blk = nseq * C

    def arrange(v):
        v = v.reshape(B, nc, C, ng, hg).transpose(0, 1, 3, 4, 2).reshape(B, nc, ng, R)
        return jnp.pad(v, ((0, 0), (0, 0), (0, nrow - ng), (0, 0)))

    rows = ba[row0:row0 + B * L]
    ba_arr = jnp.concatenate([arrange(rows[:, :H_V]), arrange(rows[:, H_V:2 * H_V])], axis=2)

    def arrange_param(p):
        v = jnp.broadcast_to(p.reshape(ng, hg, 1), (ng, hg, C)).reshape(ng, R)
        return jnp.pad(v, ((0, nrow - ng), (0, 0)))

    ad = jnp.concatenate([arrange_param(a_log), arrange_param(dt_bias)], axis=0)
    rb0 = row0 // rows_blk
    in_specs = [
        pl.BlockSpec((rows_blk, QKV_W), lambda b, c: (rb0 + b * nc + c, 0)),
        pl.BlockSpec((rows_blk, V_W), lambda b, c: (rb0 + b * nc + c, Z_BLK)),
        pl.BlockSpec((nseq, 1, 2 * nrow, R), lambda b, c: (b, c, 0, 0)),
        pl.BlockSpec((CONV_W, QKV_W), lambda b, c: (0, 0)),
        pl.BlockSpec((2 * nrow, R), lambda b, c: (0, 0)),
        pl.BlockSpec((1, D_V), lambda b, c: (0, 0)),
    ]
    args = [proj, proj, ba_arr, conv_w, ad, gamma_a.reshape(1, D_V)]
    aliases = {}
    if has_state:
        in_specs += [pl.BlockSpec((nseq, 8, QKV_W), lambda b, c: (b, 0, 0)),
                     pl.BlockSpec((nseq, H_V, D_K, D_V), lambda b, c: (b, 0, 0, 0)),
                     pl.BlockSpec(memory_space=pl.ANY)]
        args += [buf8, s0, o_all]
        aliases = {len(args) - 1: 0}
    return pl.pallas_call(
        functools.partial(_gdn_kernel, C=C, hg=hg, nseq=nseq, has_state=has_state),
        grid=(B // nseq, nc),
        in_specs=in_specs,
        out_specs=[pl.BlockSpec((rows_blk, V_W), lambda b, c: (rb0 + b * nc + c, 0)),
                   pl.BlockSpec((nseq, H_V, D_K, D_V), lambda b, c: (b, 0, 0, 0))],
        out_shape=[jax.ShapeDtypeStruct((proj.shape[0], V_W), BF16),
                   jax.ShapeDtypeStruct((B, H_V, D_K, D_V), F32)],
        scratch_shapes=[pltpu.VMEM((nseq * H_V, D_K, D_V), F32),
                        pltpu.VMEM((nseq * 8, QKV_W), F32),
                        pltpu.VMEM((rows_blk, QKV_W), F32)],
        input_output_aliases=aliases,
        compiler_params=_cparams(("parallel", "arbitrary")),
        name="gdn_state" if has_state else "gdn_fresh",
    )(*args)


def _pool_kernel(*refs, tc, nseq, start_pos, has_state):
    if has_state:
        u_ref, buf_ref, _, o_ref, tail_scr = refs
    else:
        u_ref, o_ref, tail_scr = refs
    hist = POOL_BUF + 1
    c = pl.program_id(1)

    @pl.when(c == 0)
    def _init():
        if has_state:
            for s in range(nseq):
                tail_scr[s * hist:(s + 1) * hist] = buf_ref[s]
        else:
            tail_scr[...] = jnp.zeros(tail_scr.shape, F32)

    u_all = u_ref[...].astype(F32)
    pos1 = (start_pos + 1 + c * tc + lax.broadcasted_iota(jnp.int32, (tc, 1), 0)).astype(F32)
    for s in range(nseq):
        u = u_all[s * tc:(s + 1) * tc]
        xe = jnp.concatenate([tail_scr[s * hist:(s + 1) * hist], u], axis=0)
        tail_scr[s * hist:(s + 1) * hist] = xe[tc:tc + hist]
        for gi, win in enumerate(POOL_WINDOWS):
            sl = slice(gi * POOL_GROUP_W, (gi + 1) * POOL_GROUP_W)
            acc = xe[:, sl]
            shift = 1
            while shift < win:
                acc = acc + pltpu.roll(acc, shift, 0)
                shift *= 2
            cnt = jnp.minimum(float(win), pos1)
            o_ref[s * tc:(s + 1) * tc, sl] = acc[hist:hist + tc] / cnt - u[:, sl]


def _pool(proj, row0, B, L, tc, nseq, start_pos, buf16=None, pooled_all=None):
    has_state = buf16 is not None
    nc = L // tc
    assert nseq == 1 or nc == 1
    rows_blk = nseq * tc
    rb0 = row0 // rows_blk
    in_specs = [pl.BlockSpec((rows_blk, D_MODEL), lambda b, c: (rb0 + b * nc + c, U_BLK))]
    args = [proj]
    aliases = {}
    if has_state:
        in_specs += [pl.BlockSpec((nseq, POOL_BUF + 1, D_MODEL), lambda b, c: (b, 0, 0)),
                     pl.BlockSpec(memory_space=pl.ANY)]
        args += [buf16, pooled_all]
        aliases = {len(args) - 1: 0}
    return pl.pallas_call(
        functools.partial(_pool_kernel, tc=tc, nseq=nseq, start_pos=start_pos, has_state=has_state),
        grid=(B // nseq, nc),
        in_specs=in_specs,
        out_specs=pl.BlockSpec((rows_blk, D_MODEL), lambda b, c: (rb0 + b * nc + c, 0)),
        out_shape=jax.ShapeDtypeStruct((proj.shape[0], D_MODEL), F32),
        scratch_shapes=[pltpu.VMEM((nseq * (POOL_BUF + 1), D_MODEL), F32)],
        input_output_aliases=aliases,
        compiler_params=_cparams(("parallel", "arbitrary")),
        name="pool_state" if has_state else "pool_fresh",
    )(*args)


def _post_kernel(pooled_ref, o_ref, ga_ref, gb_ref, xa_ref, xb_ref, wp_ref, ps_ref, wbb_ref, wba_ref, wo_ref,
                 g_ref, b_ref, h_ref, hp_ref, *, alpha, tiles_a):
    x = jnp.where(pl.program_id(0) < tiles_a, xa_ref[...], xb_ref[...])
    pooled = pooled_ref[...]
    mixed = jnp.concatenate(
        [_dot(pooled[:, gi * POOL_GROUP_W:(gi + 1) * POOL_GROUP_W], wp_ref[gi]) for gi in range(len(POOL_WINDOWS))],
        axis=1) * ps_ref[...]
    branch_b = _dot(mixed, wbb_ref[...])
    branch_a = _dot(o_ref[...], wba_ref[...])
    merged = _sigmoid(ga_ref[...].astype(F32)) * branch_a + _sigmoid(gb_ref[...].astype(F32)) * branch_b
    h = _layer_norm(alpha * x + _dot(merged, wo_ref[...]), g_ref[...], b_ref[...])
    h_ref[...] = h
    _store_pieces(hp_ref, h)


def _post(pooled, o_gated, proj, x_a, x_b, w_pool, pool_scale, w_br_b, w_br_a, w_out, ln_g, ln_b, alpha, tm):
    t = x_a.shape[0] + x_b.shape[0]
    tiles_a = x_a.shape[0] // tm
    row = lambda i: (i, 0)
    const2 = lambda i: (0, 0)
    return pl.pallas_call(
        functools.partial(_post_kernel, alpha=alpha, tiles_a=tiles_a),
        grid=(t // tm,),
        in_specs=[pl.BlockSpec((tm, D_MODEL), row),
                  pl.BlockSpec((tm, V_W), row),
                  pl.BlockSpec((tm, D_MODEL), lambda i: (i, U_BLK + 1)),
                  pl.BlockSpec((tm, D_MODEL), lambda i: (i, U_BLK + 2)),
                  pl.BlockSpec((tm, D_MODEL), lambda i: (jnp.minimum(i, tiles_a - 1), 0)),
                  pl.BlockSpec((tm, D_MODEL), lambda i: (jnp.maximum(i - tiles_a, 0), 0)),
                  pl.BlockSpec(w_pool.shape, lambda i: (0, 0, 0)),
                  pl.BlockSpec((1, D_MODEL), const2),
                  pl.BlockSpec(w_br_b.shape, const2),
                  pl.BlockSpec(w_br_a.shape, const2),
                  pl.BlockSpec(w_out.shape, const2),
                  pl.BlockSpec((1, D_MODEL), const2),
                  pl.BlockSpec((1, D_MODEL), const2)],
        out_specs=[pl.BlockSpec((tm, D_MODEL), row), pl.BlockSpec((N_PIECE, tm, PIECE_W), lambda i: (0, i, 0))],
        out_shape=[jax.ShapeDtypeStruct((t, D_MODEL), F32),
                   jax.ShapeDtypeStruct((N_PIECE, t, PIECE_W), jnp.uint32)],
        compiler_params=_cparams(("parallel",)),
        name="post_mixers",
    )(pooled, o_gated, proj, proj, x_a, x_b, w_pool, pool_scale, w_br_b, w_br_a, w_out, ln_g, ln_b)


def _router_kernel(h_ref, wt_ref, bias_ref, idx_ref, wts_ref, rank_ref, cnt_ref, carry_scr, *, tm):
    i = pl.program_id(0)

    @pl.when(i == 0)
    def _init():
        carry_scr[...] = jnp.zeros(carry_scr.shape, F32)

    logits = _dot3_nt(wt_ref[...], h_ref[...])
    sc = _sigmoid(logits)
    ch = sc + bias_ref[...]
    neg = -jnp.inf
    e_in = lax.broadcasted_iota(jnp.int32, (GROUP_SZ, tm), 0)
    gs_rows = []
    for g in range(N_GROUP):
        blk = ch[g * GROUP_SZ:(g + 1) * GROUP_SZ]
        m1 = jnp.max(blk, axis=0, keepdims=True)
        i1 = jnp.min(jnp.where(blk == m1, e_in, GROUP_SZ), axis=0, keepdims=True)
        m2 = jnp.max(jnp.where(e_in == i1, neg, blk), axis=0, keepdims=True)
        gs_rows.append(m1 + m2)
    gs = jnp.concatenate(gs_rows, axis=0)
    g_io = lax.broadcasted_iota(jnp.int32, (N_GROUP, tm), 0)
    e_io = lax.broadcasted_iota(jnp.int32, (N_EXPERTS, tm), 0)
    e_grp = e_io >> int(math.log2(GROUP_SZ))
    masked = jnp.full((N_EXPERTS, tm), neg, F32)
    for _ in range(TOPK_GROUP):
        mx = jnp.max(gs, axis=0, keepdims=True)
        gi = jnp.min(jnp.where(gs == mx, g_io, N_GROUP), axis=0, keepdims=True)
        gs = jnp.where(g_io == gi, neg, gs)
        masked = jnp.where(e_grp == gi, ch, masked)
    idx_rows, w_rows = [], []
    onehot = jnp.zeros((N_EXPERTS, tm), F32)
    for _ in range(TOP_K):
        mx = jnp.max(masked, axis=0, keepdims=True)
        ei = jnp.min(jnp.where(masked == mx, e_io, N_EXPERTS), axis=0, keepdims=True)
        hit = e_io == ei
        idx_rows.append(ei)
        w_rows.append(jnp.sum(jnp.where(hit, sc, 0.0), axis=0, keepdims=True))
        onehot = jnp.where(hit, 1.0, onehot)
        masked = jnp.where(hit, neg, masked)
    wsel = jnp.concatenate(w_rows, axis=0)
    wts_ref[...] = wsel / jnp.sum(wsel, axis=0, keepdims=True) * ROUTED_SCALE
    idx_ref[...] = jnp.concatenate(idx_rows, axis=0)

    tr = lax.broadcasted_iota(jnp.int32, (tm, tm), 0)
    tc_ = lax.broadcasted_iota(jnp.int32, (tm, tm), 1)
    before = jnp.where(tr < tc_, 1.0, 0.0).astype(BF16)
    cum = jnp.dot(onehot.astype(BF16), before, preferred_element_type=F32) + carry_scr[...]
    rank_ref[...] = jnp.concatenate(
        [jnp.sum(jnp.where(e_io == ei, cum, 0.0), axis=0, keepdims=True) for ei in idx_rows],
        axis=0).astype(jnp.int32)
    carry_scr[...] = carry_scr[...] + jnp.sum(onehot, axis=1, keepdims=True)
    cnt_ref[...] = carry_scr[...]


def _router(h, w_router_t, bias_col, tm):
    t = h.shape[0]
    return pl.pallas_call(
        functools.partial(_router_kernel, tm=tm),
        grid=(t // tm,),
        in_specs=[pl.BlockSpec((tm, D_MODEL), lambda i: (i, 0)),
                  pl.BlockSpec((N_EXPERTS, D_MODEL), lambda i: (0, 0)),
                  pl.BlockSpec((N_EXPERTS, 1), lambda i: (0, 0))],
        out_specs=[pl.BlockSpec((TOP_K, tm), lambda i: (0, i)),
                   pl.BlockSpec((TOP_K, tm), lambda i: (0, i)),
                   pl.BlockSpec((TOP_K, tm), lambda i: (0, i)),
                   pl.BlockSpec((N_EXPERTS, 1), lambda i: (0, 0))],
        out_shape=[jax.ShapeDtypeStruct((TOP_K, t), jnp.int32),
                   jax.ShapeDtypeStruct((TOP_K, t), F32),
                   jax.ShapeDtypeStruct((TOP_K, t), jnp.int32),
                   jax.ShapeDtypeStruct((N_EXPERTS, 1), F32)],
        scratch_shapes=[pltpu.VMEM((N_EXPERTS, 1), F32)],
        compiler_params=_cparams(("arbitrary",)),
        name="router",
    )(h, w_router_t, bias_col)


def _dest_kernel(idx_ref, rank_ref, start_ref, dest_ref, *, tm, n_rows):
    e_io = lax.broadcasted_iota(jnp.int32, (N_EXPERTS, tm), 0)
    starts = start_ref[...]
    rows = []
    for k in range(TOP_K):
        seg = jnp.sum(jnp.where(e_io == idx_ref[k:k + 1, :], starts, 0.0), axis=0, keepdims=True)
        rows.append(seg.astype(jnp.int32) + rank_ref[k:k + 1, :])
    base = jnp.concatenate(rows, axis=0)
    for p in range(N_PIECE):
        dest_ref[p] = base + p * n_rows


def _dest(idx_t, rank_t, seg_start_col, n_rows, tm):
    t = idx_t.shape[1]
    blk = pl.BlockSpec((TOP_K, tm), lambda i: (0, i))
    return pl.pallas_call(
        functools.partial(_dest_kernel, tm=tm, n_rows=n_rows),
        grid=(t // tm,),
        in_specs=[blk, blk, pl.BlockSpec((N_EXPERTS, 1), lambda i: (0, 0))],
        out_specs=pl.BlockSpec((N_PIECE, TOP_K, tm), lambda i: (0, 0, i)),
        out_shape=jax.ShapeDtypeStruct((N_PIECE, TOP_K, t), jnp.int32),
        compiler_params=_cparams(("parallel",)),
        name="dispatch_rows",
    )(idx_t, rank_t, seg_start_col)


def _expert_kernel(be_ref, nu_ref, x_ref, wg_ref, wu_ref, wd_ref, y_ref, wgu_scr, wd_scr):
    i = pl.program_id(0)
    live = i < nu_ref[0]
    new_expert = (i == 0) | (be_ref[i] != be_ref[jnp.maximum(i - 1, 0)])

    @pl.when(live & new_expert)
    def _cache_weights():
        wgu_scr[:, :D_EXPERT] = wg_ref[0].astype(BF16)
        wgu_scr[:, D_EXPERT:] = wu_ref[0].astype(BF16)
        wd_scr[...] = wd_ref[0].astype(BF16)

    @pl.when(live)
    def _():
        n_sub = 2
        sub = EXPERT_BM // n_sub
        xs = [jnp.concatenate([c.astype(BF16) for c in
                               _load_pieces([x_ref[p, s * sub:(s + 1) * sub] for p in range(N_PIECE)])], axis=1)
              for s in range(n_sub)]
        gus = [jnp.dot(x, wgu_scr[...], preferred_element_type=F32) for x in xs]
        acts = [(_silu(gu[:, :D_EXPERT]) * gu[:, D_EXPERT:]).astype(BF16) for gu in gus]
        ys = [jnp.dot(a, wd_scr[...], preferred_element_type=F32) for a in acts]
        for s, y in enumerate(ys):
            for p in range(N_PIECE):
                y_ref[p, s * sub:(s + 1) * sub] = _pack_halves(y[:, 2 * p * PIECE_W:2 * (p + 1) * PIECE_W])


def _experts(xs, blk_exp, n_used, w_gate, w_up, w_down):
    n_rows = xs.shape[1]
    n_blocks = n_rows // EXPERT_BM

    def live(i, nu):
        return jnp.minimum(i, nu[0] - 1)

    grid_spec = pltpu.PrefetchScalarGridSpec(
        num_scalar_prefetch=2,
        grid=(n_blocks,),
        in_specs=[pl.BlockSpec((N_PIECE, EXPERT_BM, PIECE_W), lambda i, be, nu: (0, live(i, nu), 0)),
                  pl.BlockSpec((1, D_MODEL, D_EXPERT), lambda i, be, nu: (be[live(i, nu)], 0, 0)),
                  pl.BlockSpec((1, D_MODEL, D_EXPERT), lambda i, be, nu: (be[live(i, nu)], 0, 0)),
                  pl.BlockSpec((1, D_EXPERT, D_MODEL), lambda i, be, nu: (be[live(i, nu)], 0, 0))],
        out_specs=pl.BlockSpec((N_PIECE, EXPERT_BM, PIECE_W), lambda i, be, nu: (0, live(i, nu), 0)),
        scratch_shapes=[pltpu.VMEM((D_MODEL, 2 * D_EXPERT), BF16), pltpu.VMEM((D_EXPERT, D_MODEL), BF16)],
    )
    return pl.pallas_call(
        _expert_kernel,
        grid_spec=grid_spec,
        out_shape=jax.ShapeDtypeStruct((N_PIECE, n_rows, PIECE_W), jnp.uint32),
        compiler_params=_cparams(("arbitrary",)),
        name="experts",
    )(blk_exp, n_used, xs, w_gate, w_up, w_down)


SC_WINDOW = 128
V7X_SC_CORES = 2
V7X_SC_SUBCORES = 16


def _sc_mesh():
    return plsc.VectorSubcoreMesh(core_axis_name="core", subcore_axis_name="subcore",
                                  num_cores=V7X_SC_CORES, num_subcores=V7X_SC_SUBCORES)


def _sc_scatter_rows(src, dest, n_rows, seg, repeat):
    d = src.shape[1]
    n_idx = dest.shape[0]
    seg_blocks = seg // SC_WINDOW
    dest2 = dest.reshape(1, n_idx)

    def src_block(i):
        return ((i // (repeat * seg_blocks)) * seg_blocks + i % seg_blocks, 0)

    @functools.partial(pl.kernel, out_type=jax.ShapeDtypeStruct((n_rows, d), src.dtype), mesh=_sc_mesh(),
                       scratch_types=[], name="sc_dispatch")
    def run(src_hbm, idx_hbm, out_hbm):
        def body(rows_vmem, idx_vmem):
            pltpu.sync_copy(rows_vmem, out_hbm.at[idx_vmem.at[0]])

        pltpu.emit_pipeline(
            body,
            grid=(n_idx // SC_WINDOW,),
            in_specs=[pl.BlockSpec((SC_WINDOW, d), src_block),
                      pl.BlockSpec((1, SC_WINDOW), lambda i: (0, i))],
            out_specs=[],
            core_axis_name=("core", "subcore"),
            dimension_semantics=(pltpu.PARALLEL,),
        )(src_hbm, idx_hbm)

    return run(src, dest2)


def _sc_gather_rows(table, idx):
    d = table.shape[1]
    n_idx = idx.shape[0]
    idx2 = idx.reshape(1, n_idx)

    @functools.partial(pl.kernel, out_type=jax.ShapeDtypeStruct((n_idx, d), table.dtype), mesh=_sc_mesh(),
                       scratch_types=[], name="sc_combine_gather")
    def run(table_hbm, idx_hbm, out_hbm):
        def body(idx_vmem, rows_vmem):
            pltpu.sync_copy(table_hbm.at[idx_vmem.at[0]], rows_vmem)

        pltpu.emit_pipeline(
            body,
            grid=(n_idx // SC_WINDOW,),
            in_specs=[pl.BlockSpec((1, SC_WINDOW), lambda i: (0, i))],
            out_specs=[pl.BlockSpec((SC_WINDOW, d), lambda i: (i, 0))],
            core_axis_name=("core", "subcore"),
            dimension_semantics=(pltpu.PARALLEL,),
        )(idx_hbm, out_hbm)

    return run(table, idx2)


def _combine_kernel(yg_ref, wts_ref, h_ref, wgu_ref, wd_ref, g_ref, b_ref, outa_ref, outb_ref, *, alpha, tiles_a):
    wts = wts_ref[...]
    acc = None
    for k in range(TOP_K):
        cols = _load_pieces([yg_ref[p, k] for p in range(N_PIECE)])
        wk = wts[:, k:k + 1]
        acc = [c * wk for c in cols] if acc is None else [a + c * wk for a, c in zip(acc, cols)]
    routed = jnp.concatenate(acc, axis=1)
    h = h_ref[...]
    gu = _dot(h, wgu_ref[...])
    shared = _dot(_silu(gu[:, :D_EXPERT]) * gu[:, D_EXPERT:], wd_ref[...])
    y = _layer_norm(alpha * h + (routed + shared), g_ref[...], b_ref[...])
    i = pl.program_id(0)

    @pl.when(i < tiles_a)
    def _():
        outa_ref[...] = y

    @pl.when(i >= tiles_a)
    def _():
        outb_ref[...] = y


def _combine(yg, wts, h, w_sh_gu, w_sh_down, ln_g, ln_b, alpha, tm, t_a):
    t = h.shape[0]
    tiles_a = t_a // tm
    row = lambda i: (i, 0)
    const2 = lambda i: (0, 0)
    return pl.pallas_call(
        functools.partial(_combine_kernel, alpha=alpha, tiles_a=tiles_a),
        grid=(t // tm,),
        in_specs=[pl.BlockSpec((N_PIECE, TOP_K, tm, PIECE_W), lambda i: (0, 0, i, 0)),
                  pl.BlockSpec((tm, TOP_K), row),
                  pl.BlockSpec((tm, D_MODEL), row),
                  pl.BlockSpec(w_sh_gu.shape, const2),
                  pl.BlockSpec(w_sh_down.shape, const2),
                  pl.BlockSpec((1, D_MODEL), const2),
                  pl.BlockSpec((1, D_MODEL), const2)],
        out_specs=[pl.BlockSpec((tm, D_MODEL), lambda i: (jnp.minimum(i, tiles_a - 1), 0)),
                   pl.BlockSpec((tm, D_MODEL), lambda i: (jnp.maximum(i - tiles_a, 0), 0))],
        out_shape=[jax.ShapeDtypeStruct((t_a, D_MODEL), F32), jax.ShapeDtypeStruct((t - t_a, D_MODEL), F32)],
        compiler_params=_cparams(("arbitrary",)),
        name="combine_ln2",
    )(yg, wts, h, w_sh_gu, w_sh_down, ln_g, ln_b)


def _layer(xp, xs, s_delta, s_conv, s_pool, w_in, conv_w, a_log, dt_bias, gamma_a, w_br_a, w_pool,
           pool_scale, w_br_b, w_out, ln1_g, ln1_b, w_router, router_bias, w_exp_gate, w_exp_up,
           w_exp_down, w_sh_gate, w_sh_up, w_sh_down, ln2_g, ln2_b, alpha):
    Bp, Lp, _ = xp.shape
    Bs, Ls, _ = xs.shape
    Tp, Ts = Bp * Lp, Bs * Ls
    T = Tp + Ts
    x_p = xp.reshape(Tp, D_MODEL)
    x_s = xs.reshape(Ts, D_MODEL)

    o_z, o_b, o_a, o_u = QKV_W, QKV_W + V_W, QKV_W + V_W + H_V, QKV_W + V_W + 2 * H_V
    w_main = jnp.concatenate([w_in[:, :o_b], w_in[:, o_u:]], axis=1).astype(BF16)
    w_ba = jnp.pad(w_in[:, o_b:o_u], ((0, 0), (0, 128 - 2 * H_V))).astype(BF16)
    proj, ba = _in_proj(x_p, x_s, w_main, w_ba, 1024, 1024)

    o_gated, sd_p = _gdn(proj, ba, conv_w, a_log, dt_bias, gamma_a, 0, Bp, Lp, 64, 1)
    buf8 = jnp.pad(s_conv, ((0, 0), (8 - (CONV_W - 1), 0), (0, 0)))
    o_gated, sd_s = _gdn(proj, ba, conv_w, a_log, dt_bias, gamma_a, Tp, Bs, Ls, Ls, GDN_STATE_SEQS,
                         buf8=buf8, s0=s_delta, o_all=o_gated)

    pooled = _pool(proj, 0, Bp, Lp, 256, 1, 0)
    buf16 = jnp.pad(s_pool, ((0, 0), (1, 0), (0, 0)))
    pooled = _pool(proj, Tp, Bs, Ls, Ls, POOL_STATE_SEQS, PAST_LEN, buf16=buf16, pooled_all=pooled)

    h, hp = _post(pooled, o_gated, proj, x_p, x_s, w_pool.astype(BF16), pool_scale.reshape(1, D_MODEL),
                  w_br_b.astype(BF16), w_br_a.astype(BF16), w_out.astype(BF16),
                  ln1_g.reshape(1, D_MODEL), ln1_b.reshape(1, D_MODEL), alpha, 256)

    idx_t, wts_t, rank_t, cnt = _router(h, w_router.T, router_bias.reshape(N_EXPERTS, 1), 512)
    counts = cnt[:, 0].astype(jnp.int32)
    padded = ((counts + EXPERT_BM - 1) // EXPERT_BM) * EXPERT_BM
    pends = jnp.cumsum(padded)
    pstarts = pends - padded
    n_blocks = (T * TOP_K + N_EXPERTS * (EXPERT_BM - 1) + EXPERT_BM - 1) // EXPERT_BM
    n_rows = n_blocks * EXPERT_BM
    piece_rows = _dest(idx_t, rank_t, pstarts.astype(F32).reshape(N_EXPERTS, 1), n_rows, 512)
    blk_start = jnp.arange(n_blocks, dtype=jnp.int32) * EXPERT_BM
    blk_exp = jnp.minimum(jnp.sum((pends[None, :] <= blk_start[:, None]).astype(jnp.int32), axis=1),
                          N_EXPERTS - 1)
    n_used = (pends[-1:] // EXPERT_BM).astype(jnp.int32)

    piece_idx = piece_rows.reshape(N_PIECE * TOP_K * T)
    x_sorted = _sc_scatter_rows(hp.reshape(N_PIECE * T, PIECE_W), piece_idx, N_PIECE * n_rows, T, TOP_K)
    y_sorted = _experts(x_sorted.reshape(N_PIECE, n_rows, PIECE_W), blk_exp, n_used,
                        w_exp_gate, w_exp_up, w_exp_down)
    yg = _sc_gather_rows(y_sorted.reshape(N_PIECE * n_rows, PIECE_W), piece_idx)
    yg = yg.reshape(N_PIECE, TOP_K, T, PIECE_W)

    w_sh_gu = jnp.concatenate([w_sh_gate, w_sh_up], axis=1).astype(BF16)
    y_p, y_s = _combine(yg, wts_t.T, h, w_sh_gu, w_sh_down.astype(BF16),
                        ln2_g.reshape(1, D_MODEL), ln2_b.reshape(1, D_MODEL), alpha, 256, Tp)

    keep = max(POOL_BUF, CONV_W - 1)

    def last_rows(x2d, B, L):
        n = min(L, keep)
        rows = jnp.arange(B, dtype=jnp.int32)[:, None] * L + (L - n) + jnp.arange(n, dtype=jnp.int32)
        return jnp.take(x2d, rows.reshape(-1), axis=0).astype(BF16), n

    xt_p, n_p = last_rows(x_p, Bp, Lp)
    xt_s, n_s = last_rows(x_s, Bs, Ls)
    n_tail = Bp * n_p + Bs * n_s
    n_pad = -(-n_tail // 128) * 128
    xt = jnp.pad(jnp.concatenate([xt_p, xt_s], axis=0), ((0, n_pad - n_tail), (0, 0)))
    w_tail = jnp.concatenate([w_in[:, :QKV_W], w_in[:, o_u:o_u + D_MODEL]], axis=1).astype(BF16)
    tail = _matmul(xt, w_tail, n_pad, 512, F32, "in_proj_state_rows")
    tail_p = tail[:Bp * n_p].reshape(Bp, n_p, QKV_W + D_MODEL)
    tail_s = tail[Bp * n_p:n_tail].reshape(Bs, n_s, QKV_W + D_MODEL)

    def new_buffers(rows, old_conv, old_pool):
        conv = jnp.concatenate([old_conv, rows[:, :, :QKV_W]], axis=1)[:, -(CONV_W - 1):]
        pool = jnp.concatenate([old_pool, rows[:, :, QKV_W:]], axis=1)[:, -POOL_BUF:]
        return conv, pool

    new_conv_p, new_pool_p = new_buffers(tail_p, jnp.zeros((Bp, CONV_W - 1, QKV_W), F32),
                                         jnp.zeros((Bp, POOL_BUF, D_MODEL), F32))
    new_conv_s, new_pool_s = new_buffers(tail_s, s_conv, s_pool)
    return (y_p.reshape(Bp, Lp, D_MODEL), y_s.reshape(Bs, Ls, D_MODEL),
            sd_p, new_conv_p, new_pool_p, sd_s, new_conv_s, new_pool_s)


def kernel(x_prompt, x_sample, state_delta, state_conv, state_pool, w_in, conv_w, a_log, dt_bias, gamma_a,
           w_br_a, w_pool, pool_scale, w_br_b, w_out, ln1_g, ln1_b, w_router, router_bias,
           w_exp_gate, w_exp_up, w_exp_down, w_sh_gate, w_sh_up, w_sh_down, ln2_g, ln2_b):
    depth = w_in.shape[0]
    alpha = (2 * depth) ** 0.25
    yp, ys = x_prompt, x_sample
    outs = [[] for _ in range(6)]
    for l in range(depth):
        res = _layer(yp, ys, state_delta[l], state_conv[l], state_pool[l], w_in[l], conv_w[l], a_log[l],
                     dt_bias[l], gamma_a[l], w_br_a[l], w_pool[l], pool_scale[l], w_br_b[l], w_out[l],
                     ln1_g[l], ln1_b[l], w_router[l], router_bias[l], w_exp_gate[l], w_exp_up[l],
                     w_exp_down[l], w_sh_gate[l], w_sh_up[l], w_sh_down[l], ln2_g[l], ln2_b[l], alpha)
        yp, ys = res[0], res[1]
        for lst, v in zip(outs, res[2:]):
            lst.append(v)
    return (yp, ys) + tuple(jnp.stack(v) for v in outs)
```

```python
import functools
import math

import jax
import jax.numpy as jnp
from jax import lax
from jax.experimental import pallas as pl
from jax.experimental.pallas import tpu as pltpu
from jax.experimental.pallas import tpu_sc as plsc

F32 = jnp.float32
BF16 = jnp.bfloat16

D_MODEL = 1024
H_K = 8
D_K = 128
H_V = 16
D_V = 128
Q_W = H_K * D_K
V_W = H_V * D_V
QKV_W = 2 * Q_W + V_W
CONV_W = 4
POOL_WINDOWS = (2, 4, 8, 16)
POOL_GROUP_W = D_MODEL // len(POOL_WINDOWS)
POOL_BUF = max(POOL_WINDOWS) - 1
N_EXPERTS = 256
TOP_K = 8
N_GROUP = 8
TOPK_GROUP = 4
GROUP_SZ = N_EXPERTS // N_GROUP
D_EXPERT = D_MODEL // 4
ROUTED_SCALE = 2.5
LN_EPS = 1e-5
NORM_EPS = 1e-6
PAST_LEN = 16384

MAIN_W = QKV_W + V_W + 3 * D_MODEL
Z_BLK = QKV_W // V_W
U_BLK = (QKV_W + V_W) // D_MODEL

STACK_ROWS = 128
SOLVE_BASE = 16
GDN_WAVE = 8
EXPERT_BM = 512
GDN_STATE_SEQS = 4
POOL_STATE_SEQS = 16
VMEM_LIMIT = 56 * 1024 * 1024


def _cparams(sem):
    return pltpu.CompilerParams(dimension_semantics=sem, vmem_limit_bytes=VMEM_LIMIT)


def _sigmoid(x):
    return 0.5 * jnp.tanh(0.5 * x) + 0.5


def _silu(x):
    return x * _sigmoid(x)


def _softplus(x):
    return jnp.maximum(x, 0.0) + jnp.log(1.0 + jnp.exp(-jnp.abs(x)))


def _dot(a, b):
    return jnp.dot(a.astype(BF16), b.astype(BF16), preferred_element_type=F32)


def _dot_nt(a, b):
    return lax.dot_general(a.astype(BF16), b.astype(BF16), (((1,), (1,)), ((), ())),
                           preferred_element_type=F32)


def _dot_tn(a, b):
    return lax.dot_general(a.astype(BF16), b.astype(BF16), (((0,), (0,)), ((), ())),
                           preferred_element_type=F32)


def _split(a):
    hi = a.astype(BF16)
    lo = (a - hi.astype(F32)).astype(BF16)
    return hi, lo


def _dot3_nt(a, b):
    ah, al = _split(a)
    bh, bl = _split(b)
    d = functools.partial(lax.dot_general, dimension_numbers=(((1,), (1,)), ((), ())),
                          preferred_element_type=F32)
    return d(ah, bh) + (d(ah, bl) + d(al, bh))


def _pack_halves(x):
    n = x.shape[1] // 2
    hi = lax.bitcast_convert_type(x[:, :n].astype(BF16).astype(F32), jnp.uint32)
    lo = lax.bitcast_convert_type(x[:, n:].astype(BF16).astype(F32), jnp.uint32)
    return (hi & jnp.uint32(0xFFFF0000)) | (lo >> 16)


def _unpack_halves(w):
    hi = lax.bitcast_convert_type(w & jnp.uint32(0xFFFF0000), F32)
    lo = lax.bitcast_convert_type(w << 16, F32)
    return hi, lo


N_PIECE = 2
PIECE_W = D_MODEL // (2 * N_PIECE)


def _store_pieces(ref, x):
    for p in range(N_PIECE):
        ref[p] = _pack_halves(x[:, 2 * p * PIECE_W:2 * (p + 1) * PIECE_W])


def _load_pieces(pieces):
    cols = []
    for w in pieces:
        cols.extend(_unpack_halves(w))
    return cols


def _layer_norm(x, g, b):
    mu = jnp.mean(x, axis=-1, keepdims=True)
    xc = x - mu
    var = jnp.mean(xc * xc, axis=-1, keepdims=True)
    return xc * lax.rsqrt(var + LN_EPS) * g + b


def _in_proj_kernel(xa_ref, xb_ref, w_ref, wba_ref, o_ref, ba_ref, x_scr, *, tiles_a):
    @pl.when(pl.program_id(1) == 0)
    def _():
        x = jnp.where(pl.program_id(0) < tiles_a, xa_ref[...], xb_ref[...]).astype(BF16)
        x_scr[...] = x
        ba_ref[...] = jnp.dot(x, wba_ref[...], preferred_element_type=F32)

    o_ref[...] = jnp.dot(x_scr[...], w_ref[...], preferred_element_type=F32).astype(o_ref.dtype)


def _in_proj(x_a, x_b, w_main, w_ba, tm, tn):
    k = x_a.shape[1]
    t = x_a.shape[0] + x_b.shape[0]
    n = w_main.shape[1]
    tiles_a = x_a.shape[0] // tm
    return pl.pallas_call(
        functools.partial(_in_proj_kernel, tiles_a=tiles_a),
        grid=(t // tm, n // tn),
        in_specs=[pl.BlockSpec((tm, k), lambda i, j: (jnp.minimum(i, tiles_a - 1), 0)),
                  pl.BlockSpec((tm, k), lambda i, j: (jnp.maximum(i - tiles_a, 0), 0)),
                  pl.BlockSpec((k, tn), lambda i, j: (0, j)),
                  pl.BlockSpec(w_ba.shape, lambda i, j: (0, 0))],
        out_specs=[pl.BlockSpec((tm, tn), lambda i, j: (i, j)),
                   pl.BlockSpec((tm, w_ba.shape[1]), lambda i, j: (i, 0))],
        out_shape=[jax.ShapeDtypeStruct((t, n), BF16), jax.ShapeDtypeStruct((t, w_ba.shape[1]), F32)],
        scratch_shapes=[pltpu.VMEM((tm, k), BF16)],
        compiler_params=_cparams(("parallel", "arbitrary")),
        name="in_proj",
    )(x_a, x_b, w_main, w_ba)


def _mm_kernel(x_ref, w_ref, o_ref):
    o_ref[...] = jnp.dot(x_ref[...], w_ref[...], preferred_element_type=F32).astype(o_ref.dtype)


def _matmul(x, w, tm, tn, out_dtype, name):
    t, k = x.shape
    n = w.shape[1]
    return pl.pallas_call(
        _mm_kernel,
        grid=(t // tm, n // tn),
        in_specs=[pl.BlockSpec((tm, k), lambda i, j: (i, 0)),
                  pl.BlockSpec((k, tn), lambda i, j: (0, j))],
        out_specs=pl.BlockSpec((tm, tn), lambda i, j: (i, j)),
        out_shape=jax.ShapeDtypeStruct((t, n), out_dtype),
        compiler_params=_cparams(("parallel", "parallel")),
        name=name,
    )(x, w)


def _gdn_kernel(*refs, C, hg, nseq, has_state):
    if has_state:
        (qkv_ref, z_ref, ba_ref, cw_ref, ad_ref, gam_ref, buf_ref, s0_ref,
         o_ref, sout_ref, s_scr, tail_scr, act_scr) = refs
    else:
        (qkv_ref, z_ref, ba_ref, cw_ref, ad_ref, gam_ref,
         o_ref, sout_ref, s_scr, tail_scr, act_scr) = refs
    R = hg * C
    ng = H_V // hg
    log2c = int(math.log2(C))
    c = pl.program_id(1)

    @pl.when(c == 0)
    def _init():
        if has_state:
            for s in range(nseq):
                s_scr[s * H_V:(s + 1) * H_V] = s0_ref[s]
                tail_scr[s * 8:(s + 1) * 8] = buf_ref[s]
        else:
            s_scr[...] = jnp.zeros(s_scr.shape, F32)
            tail_scr[...] = jnp.zeros(tail_scr.shape, F32)

    cw = cw_ref[...]
    qkv_all = qkv_ref[...].astype(F32)
    n_prev = CONV_W - 1
    if not has_state:
        ext_pad = 128
        sr = lax.broadcasted_iota(jnp.int32, (n_prev * C, ext_pad), 0)
        sc = lax.broadcasted_iota(jnp.int32, (n_prev * C, ext_pad), 1)
        shift_sel = jnp.zeros((n_prev * C, ext_pad), F32)
        for j in range(n_prev):
            in_tap = (sr >= j * C) & (sr < (j + 1) * C)
            shift_sel = jnp.where(in_tap & (sc == sr - j * C + 8 - n_prev + j), 1.0, shift_sel)
        shift_sel = shift_sel.astype(BF16)
    for s in range(nseq):
        x = qkv_all[s * C:(s + 1) * C]
        tail = tail_scr[s * 8:(s + 1) * 8]
        if has_state:
            xe = jnp.concatenate([tail, x], axis=0)
            y = x * cw[n_prev:CONV_W]
            for j in range(n_prev):
                y = y + xe[8 - n_prev + j:8 - n_prev + j + C] * cw[j:j + 1]
        else:
            xe = jnp.concatenate([tail, x, jnp.zeros((ext_pad - 8 - C, QKV_W), F32)], axis=0).astype(BF16)
            moved = jnp.dot(shift_sel, xe, preferred_element_type=F32)
            y = x * cw[n_prev:CONV_W]
            for j in range(n_prev):
                y = y + moved[j * C:(j + 1) * C] * cw[j:j + 1]
        tail_scr[s * 8:(s + 1) * 8] = x[C - 8:C]
        act_scr[s * C:(s + 1) * C] = _silu(y)

    ri = lax.broadcasted_iota(jnp.int32, (R, R), 0)
    ci = lax.broadcasted_iota(jnp.int32, (R, R), 1)
    same = (ri >> log2c) == (ci >> log2c)
    incl = same & (ri >= ci)
    strict = same & (ri > ci)
    eye = ri == ci
    is_last = ci == (((ri >> log2c) << log2c) + (C - 1))
    eye_f = jnp.where(eye, 1.0, 0.0).astype(F32)
    cum_u = jnp.where(same & (ri <= ci), 1.0, 0.0).astype(BF16)

    nrow = ba_ref.shape[2] // 2
    adv = ad_ref[...]
    dd = functools.partial(jnp.dot, preferred_element_type=F32)
    beta_seq, G_seq = [], []
    for s in range(nseq):
        bav = ba_ref[s, 0]
        beta_seq.append(_sigmoid(bav[0:nrow]))
        g_rows = -jnp.exp(adv[0:nrow]) * _softplus(bav[nrow:2 * nrow] + adv[nrow:2 * nrow])
        g1 = g_rows.astype(BF16)
        r1 = g_rows - g1.astype(F32)
        g2 = r1.astype(BF16)
        g3 = (r1 - g2.astype(F32)).astype(BF16)
        G_seq.append(dd(g1, cum_u) + (dd(g2, cum_u) + dd(g3, cum_u)))

    gam = gam_ref[...]
    normed = {}

    def qk_head(s, kind, kh):
        key = (s, kind, kh)
        if key not in normed:
            off = (0 if kind == "q" else Q_W) + kh * D_K
            v = act_scr[s * C:(s + 1) * C, off:off + D_K]
            v = v * lax.rsqrt(jnp.sum(v * v, axis=-1, keepdims=True) + NORM_EPS)
            if kind == "q":
                v = v * (D_K ** -0.5)
            normed[key] = v
        return normed[key]

    rep = H_V // H_K
    log2b = min(log2c, int(math.log2(SOLVE_BASE)))
    same_base = ((ri ^ ci) >> log2b) == 0
    row_head = lax.broadcasted_iota(jnp.int32, (R, D_V), 0) >> log2c
    z_all = z_ref[...].astype(F32)
    specs = [(s, g) for s in range(nseq) for g in range(ng)]
    for w0 in range(0, len(specs), GDN_WAVE):
        grp = []
        for s, g in specs[w0:w0 + GDN_WAVE]:
            rows = slice(s * C, (s + 1) * C)
            heads = [g * hg + hh for hh in range(hg)]
            qst = jnp.concatenate([qk_head(s, "q", h // rep) for h in heads], axis=0)
            kst = jnp.concatenate([qk_head(s, "k", h // rep) for h in heads], axis=0)
            vst = jnp.concatenate([act_scr[rows, 2 * Q_W + h * D_V:2 * Q_W + (h + 1) * D_V] for h in heads], axis=0)
            grow = G_seq[s][g:g + 1]
            brow = beta_seq[s][g:g + 1]
            gcol = jnp.sum(eye_f * grow, axis=1, keepdims=True)
            bcol = jnp.sum(eye_f * brow, axis=1, keepdims=True)
            glast = jnp.sum(jnp.where(is_last, grow, 0.0), axis=1, keepdims=True)
            dm = jnp.where(incl, jnp.exp(jnp.minimum(gcol - grow, 0.0)), 0.0)
            eg = jnp.exp(gcol)
            grp.append(dict(rows=rows, s0=s * H_V, heads=heads, qst=qst, kst=kst, gcol=gcol, bcol=bcol,
                            glast=glast, dm=dm, eg=eg,
                            x=jnp.concatenate([vst * bcol, kst * (bcol * eg)], axis=1)))

        for d in grp:
            d["kk"] = _dot_nt(d["kst"], d["kst"])
            d["qk"] = _dot_nt(d["qst"], d["kst"])
        for d in grp:
            d["m"] = jnp.where(strict, -(d["kk"] * d["bcol"] * d["dm"]), 0.0)
            d["attn"] = jnp.where(incl, d["qk"] * d["dm"], 0.0)
            d["pw"] = jnp.where(same_base, d["m"], 0.0)
            d["t"] = eye_f + d["pw"]

        for r in range(log2b):
            last = r == log2b - 1
            for d in grp:
                if r == 0:
                    if not last:
                        d["pw"] = _dot(d["pw"], d["pw"])
                elif last:
                    d["t"] = d["t"] + _dot(d["pw"], d["t"])
                else:
                    out = _dot(d["pw"], jnp.concatenate([d["pw"], d["t"]], axis=1))
                    d["pw"] = out[:, :R]
                    d["t"] = d["t"] + out[:, R:]
        for lev in range(log2b + 1, log2c + 1):
            coupling = ((ri ^ ci) >> (lev - 1)) == 1
            for d in grp:
                d["y"] = _dot(d["t"], jnp.where(coupling, d["m"], 0.0))
            for d in grp:
                d["t"] = d["t"] + _dot(d["y"], d["t"])
        for d in grp:
            d["x"] = _dot(d["t"], d["x"])

        for d in grp:
            qe = d["qst"] * d["eg"]
            d["xs"] = [_dot(jnp.concatenate([d["x"][hh * C:(hh + 1) * C, D_V:], qe[hh * C:(hh + 1) * C]], axis=0),
                            s_scr[d["s0"] + h]) for hh, h in enumerate(d["heads"])]
        for d in grp:
            d["vnew"] = jnp.concatenate(
                [d["x"][hh * C:(hh + 1) * C, :D_V] - xs[:C] for hh, xs in enumerate(d["xs"])], axis=0)
            qs = jnp.concatenate([xs[C:] for xs in d["xs"]], axis=0)
            d["o"] = qs + _dot(d["attn"], d["vnew"])

        for d in grp:
            kt = d["kst"] * jnp.exp(d["glast"] - d["gcol"])
            egl = jnp.exp(d["glast"])
            for hh, h in enumerate(d["heads"]):
                sl = slice(hh * C, (hh + 1) * C)
                if C >= 16:
                    upd = _dot_tn(kt[sl], d["vnew"][sl])
                else:
                    upd = _dot_tn(kt, jnp.where(row_head == hh, d["vnew"], 0.0))
                s_scr[d["s0"] + h] = s_scr[d["s0"] + h] * egl[hh * C:hh * C + 1] + upd

        for d in grp:
            o = d["o"]
            zst = jnp.concatenate([z_all[d["rows"], h * D_V:(h + 1) * D_V] for h in d["heads"]], axis=0)
            on = o * lax.rsqrt(jnp.mean(o * o, axis=-1, keepdims=True) + NORM_EPS) * gam * _silu(zst)
            for hh, h in enumerate(d["heads"]):
                o_ref[d["rows"], h * D_V:(h + 1) * D_V] = on[hh * C:(hh + 1) * C].astype(o_ref.dtype)

    @pl.when(c == pl.num_programs(1) - 1)
    def _fin():
        for s in range(nseq):
            sout_ref[s] = s_scr[s * H_V:(s + 1) * H_V]


def _gdn(proj, ba, conv_w, a_log, dt_bias, gamma_a, row0, B, L, C, nseq, buf8=None, s0=None):
    has_state = s0 is not None
    hg = STACK_ROWS // C
    ng = H_V // hg
    nrow = 16
    nc = L // C
    R = STACK_ROWS
    assert nseq == 1 or nc == 1
    rows_blk = nseq * C

    def arrange(v):
        v = v.reshape(B, nc, C, ng, hg).transpose(0, 1, 3, 4, 2).reshape(B, nc, ng, R)
        return jnp.pad(v, ((0, 0), (0, 0), (0, nrow - ng), (0, 0)))

    rows = ba[row0:row0 + B * L]
    ba_arr = jnp.concatenate([arrange(rows[:, :H_V]), arrange(rows[:, H_V:2 * H_V])], axis=2)

    def arrange_param(p):
        v = jnp.broadcast_to(p.reshape(ng, hg, 1), (ng, hg, C)).reshape(ng, R)
        return jnp.pad(v, ((0, nrow - ng), (0, 0)))

    ad = jnp.concatenate([arrange_param(a_log), arrange_param(dt_bias)], axis=0)
    rb0 = row0 // rows_blk
    in_specs = [
        pl.BlockSpec((rows_blk, QKV_W), lambda b, c: (rb0 + b * nc + c, 0)),
        pl.BlockSpec((rows_blk, V_W), lambda b, c: (rb0 + b * nc + c, Z_BLK)),
        pl.BlockSpec((nseq, 1, 2 * nrow, R), lambda b, c: (b, c, 0, 0)),
        pl.BlockSpec((CONV_W, QKV_W), lambda b, c: (0, 0)),
        pl.BlockSpec((2 * nrow, R), lambda b, c: (0, 0)),
        pl.BlockSpec((1, D_V), lambda b, c: (0, 0)),
    ]
    args = [proj, proj, ba_arr, conv_w, ad, gamma_a.reshape(1, D_V)]
    if has_state:
        in_specs += [pl.BlockSpec((nseq, 8, QKV_W), lambda b, c: (b, 0, 0)),
                     pl.BlockSpec((nseq, H_V, D_K, D_V), lambda b, c: (b, 0, 0, 0))]
        args += [buf8, s0]
    return pl.pallas_call(
        functools.partial(_gdn_kernel, C=C, hg=hg, nseq=nseq, has_state=has_state),
        grid=(B // nseq, nc),
        in_specs=in_specs,
        out_specs=[pl.BlockSpec((rows_blk, V_W), lambda b, c: (b * nc + c, 0)),
                   pl.BlockSpec((nseq, H_V, D_K, D_V), lambda b, c: (b, 0, 0, 0))],
        out_shape=[jax.ShapeDtypeStruct((B * L, V_W), BF16),
                   jax.ShapeDtypeStruct((B, H_V, D_K, D_V), F32)],
        scratch_shapes=[pltpu.VMEM((nseq * H_V, D_K, D_V), F32),
                        pltpu.VMEM((nseq * 8, QKV_W), F32),
                        pltpu.VMEM((rows_blk, QKV_W), F32)],
        compiler_params=_cparams(("parallel", "arbitrary")),
        name="gdn_state" if has_state else "gdn_fresh",
    )(*args)


def _pool_kernel(*refs, tc, nseq, start_pos, has_state):
    if has_state:
        u_ref, buf_ref, o_ref, tail_scr = refs
    else:
        u_ref, o_ref, tail_scr = refs
    hist = POOL_BUF + 1
    c = pl.program_id(1)

    @pl.when(c == 0)
    def _init():
        if has_state:
            for s in range(nseq):
                tail_scr[s * hist:(s + 1) * hist] = buf_ref[s]
        else:
            tail_scr[...] = jnp.zeros(tail_scr.shape, F32)

    u_all = u_ref[...].astype(F32)
    pos1 = (start_pos + 1 + c * tc + lax.broadcasted_iota(jnp.int32, (tc, 1), 0)).astype(F32)
    for s in range(nseq):
        u = u_all[s * tc:(s + 1) * tc]
        xe = jnp.concatenate([tail_scr[s * hist:(s + 1) * hist], u], axis=0)
        tail_scr[s * hist:(s + 1) * hist] = xe[tc:tc + hist]
        for gi, win in enumerate(POOL_WINDOWS):
            sl = slice(gi * POOL_GROUP_W, (gi + 1) * POOL_GROUP_W)
            acc = xe[:, sl]
            shift = 1
            while shift < win:
                acc = acc + pltpu.roll(acc, shift, 0)
                shift *= 2
            cnt = jnp.minimum(float(win), pos1)
            o_ref[s * tc:(s + 1) * tc, sl] = acc[hist:hist + tc] / cnt - u[:, sl]


def _pool(proj, row0, B, L, tc, nseq, start_pos, buf16=None):
    has_state = buf16 is not None
    nc = L // tc
    assert nseq == 1 or nc == 1
    rows_blk = nseq * tc
    rb0 = row0 // rows_blk
    in_specs = [pl.BlockSpec((rows_blk, D_MODEL), lambda b, c: (rb0 + b * nc + c, U_BLK))]
    args = [proj]
    if has_state:
        in_specs.append(pl.BlockSpec((nseq, POOL_BUF + 1, D_MODEL), lambda b, c: (b, 0, 0)))
        args.append(buf16)
    return pl.pallas_call(
        functools.partial(_pool_kernel, tc=tc, nseq=nseq, start_pos=start_pos, has_state=has_state),
        grid=(B // nseq, nc),
        in_specs=in_specs,
        out_specs=pl.BlockSpec((rows_blk, D_MODEL), lambda b, c: (b * nc + c, 0)),
        out_shape=jax.ShapeDtypeStruct((B * L, D_MODEL), F32),
        scratch_shapes=[pltpu.VMEM((nseq * (POOL_BUF + 1), D_MODEL), F32)],
        compiler_params=_cparams(("parallel", "arbitrary")),
        name="pool_state" if has_state else "pool_fresh",
    )(*args)


def _post_kernel(pa_ref, pb_ref, oa_ref, ob_ref, ga_ref, gb_ref, xa_ref, xb_ref, wp_ref, ps_ref, wbb_ref, wba_ref,
                 wo_ref, g_ref, b_ref, h_ref, hp_ref, *, alpha, tiles_a):
    first = pl.program_id(0) < tiles_a
    x = jnp.where(first, xa_ref[...], xb_ref[...])
    pooled = jnp.where(first, pa_ref[...], pb_ref[...])
    o_gated = jnp.where(first, oa_ref[...], ob_ref[...])
    mixed = jnp.concatenate(
        [_dot(pooled[:, gi * POOL_GROUP_W:(gi + 1) * POOL_GROUP_W], wp_ref[gi]) for gi in range(len(POOL_WINDOWS))],
        axis=1) * ps_ref[...]
    branch_b = _dot(mixed, wbb_ref[...])
    branch_a = _dot(o_gated, wba_ref[...])
    merged = _sigmoid(ga_ref[...].astype(F32)) * branch_a + _sigmoid(gb_ref[...].astype(F32)) * branch_b
    h = _layer_norm(alpha * x + _dot(merged, wo_ref[...]), g_ref[...], b_ref[...])
    h_ref[...] = h
    _store_pieces(hp_ref, h)


def _post(pooled_a, pooled_b, o_a, o_b, proj, x_a, x_b, w_pool, pool_scale, w_br_b, w_br_a, w_out, ln_g, ln_b,
          alpha, tm):
    t = x_a.shape[0] + x_b.shape[0]
    tiles_a = x_a.shape[0] // tm
    row = lambda i: (i, 0)
    row_a = lambda i: (jnp.minimum(i, tiles_a - 1), 0)
    row_b = lambda i: (jnp.maximum(i - tiles_a, 0), 0)
    const2 = lambda i: (0, 0)
    return pl.pallas_call(
        functools.partial(_post_kernel, alpha=alpha, tiles_a=tiles_a),
        grid=(t // tm,),
        in_specs=[pl.BlockSpec((tm, D_MODEL), row_a),
                  pl.BlockSpec((tm, D_MODEL), row_b),
                  pl.BlockSpec((tm, V_W), row_a),
                  pl.BlockSpec((tm, V_W), row_b),
                  pl.BlockSpec((tm, D_MODEL), lambda i: (i, U_BLK + 1)),
                  pl.BlockSpec((tm, D_MODEL), lambda i: (i, U_BLK + 2)),
                  pl.BlockSpec((tm, D_MODEL), row_a),
                  pl.BlockSpec((tm, D_MODEL), row_b),
                  pl.BlockSpec(w_pool.shape, lambda i: (0, 0, 0)),
                  pl.BlockSpec((1, D_MODEL), const2),
                  pl.BlockSpec(w_br_b.shape, const2),
                  pl.BlockSpec(w_br_a.shape, const2),
                  pl.BlockSpec(w_out.shape, const2),
                  pl.BlockSpec((1, D_MODEL), const2),
                  pl.BlockSpec((1, D_MODEL), const2)],
        out_specs=[pl.BlockSpec((tm, D_MODEL), row), pl.BlockSpec((N_PIECE, tm, PIECE_W), lambda i: (0, i, 0))],
        out_shape=[jax.ShapeDtypeStruct((t, D_MODEL), F32),
                   jax.ShapeDtypeStruct((N_PIECE, t, PIECE_W), jnp.uint32)],
        compiler_params=_cparams(("parallel",)),
        name="post_mixers",
    )(pooled_a, pooled_b, o_a, o_b, proj, proj, x_a, x_b, w_pool, pool_scale, w_br_b, w_br_a, w_out, ln_g, ln_b)


def _router_kernel(h_ref, wt_ref, bias_ref, idx_ref, wts_ref, rank_ref, cnt_ref, carry_scr, *, tm):
    i = pl.program_id(0)

    @pl.when(i == 0)
    def _init():
        carry_scr[...] = jnp.zeros(carry_scr.shape, F32)

    logits = _dot3_nt(wt_ref[...], h_ref[...])
    sc = _sigmoid(logits)
    ch = sc + bias_ref[...]
    neg = -jnp.inf
    e_in = lax.broadcasted_iota(jnp.int32, (GROUP_SZ, tm), 0)
    gs_rows = []
    for g in range(N_GROUP):
        blk = ch[g * GROUP_SZ:(g + 1) * GROUP_SZ]
        m1 = jnp.max(blk, axis=0, keepdims=True)
        i1 = jnp.min(jnp.where(blk == m1, e_in, GROUP_SZ), axis=0, keepdims=True)
        m2 = jnp.max(jnp.where(e_in == i1, neg, blk), axis=0, keepdims=True)
        gs_rows.append(m1 + m2)
    gs = jnp.concatenate(gs_rows, axis=0)
    g_io = lax.broadcasted_iota(jnp.int32, (N_GROUP, tm), 0)
    e_io = lax.broadcasted_iota(jnp.int32, (N_EXPERTS, tm), 0)
    e_grp = e_io >> int(math.log2(GROUP_SZ))
    masked = jnp.full((N_EXPERTS, tm), neg, F32)
    for _ in range(TOPK_GROUP):
        mx = jnp.max(gs, axis=0, keepdims=True)
        gi = jnp.min(jnp.where(gs == mx, g_io, N_GROUP), axis=0, keepdims=True)
        gs = jnp.where(g_io == gi, neg, gs)
        masked = jnp.where(e_grp == gi, ch, masked)
    idx_rows, w_rows = [], []
    onehot = jnp.zeros((N_EXPERTS, tm), F32)
    for _ in range(TOP_K):
        mx = jnp.max(masked, axis=0, keepdims=True)
        ei = jnp.min(jnp.where(masked == mx, e_io, N_EXPERTS), axis=0, keepdims=True)
        hit = e_io == ei
        idx_rows.append(ei)
        w_rows.append(jnp.sum(jnp.where(hit, sc, 0.0), axis=0, keepdims=True))
        onehot = jnp.where(hit, 1.0, onehot)
        masked = jnp.where(hit, neg, masked)
    wsel = jnp.concatenate(w_rows, axis=0)
    wts_ref[...] = wsel / jnp.sum(wsel, axis=0, keepdims=True) * ROUTED_SCALE
    idx_ref[...] = jnp.concatenate(idx_rows, axis=0)

    tr = lax.broadcasted_iota(jnp.int32, (tm, tm), 0)
    tc_ = lax.broadcasted_iota(jnp.int32, (tm, tm), 1)
    before = jnp.where(tr < tc_, 1.0, 0.0).astype(BF16)
    cum = jnp.dot(onehot.astype(BF16), before, preferred_element_type=F32) + carry_scr[...]
    rank_ref[...] = jnp.concatenate(
        [jnp.sum(jnp.where(e_io == ei, cum, 0.0), axis=0, keepdims=True) for ei in idx_rows],
        axis=0).astype(jnp.int32)
    carry_scr[...] = carry_scr[...] + jnp.sum(onehot, axis=1, keepdims=True)
    cnt_ref[...] = carry_scr[...]


def _router(h, w_router_t, bias_col, tm):
    t = h.shape[0]
    return pl.pallas_call(
        functools.partial(_router_kernel, tm=tm),
        grid=(t // tm,),
        in_specs=[pl.BlockSpec((tm, D_MODEL), lambda i: (i, 0)),
                  pl.BlockSpec((N_EXPERTS, D_MODEL), lambda i: (0, 0)),
                  pl.BlockSpec((N_EXPERTS, 1), lambda i: (0, 0))],
        out_specs=[pl.BlockSpec((TOP_K, tm), lambda i: (0, i)),
                   pl.BlockSpec((TOP_K, tm), lambda i: (0, i)),
                   pl.BlockSpec((TOP_K, tm), lambda i: (0, i)),
                   pl.BlockSpec((N_EXPERTS, 1), lambda i: (0, 0))],
        out_shape=[jax.ShapeDtypeStruct((TOP_K, t), jnp.int32),
                   jax.ShapeDtypeStruct((TOP_K, t), F32),
                   jax.ShapeDtypeStruct((TOP_K, t), jnp.int32),
                   jax.ShapeDtypeStruct((N_EXPERTS, 1), F32)],
        scratch_shapes=[pltpu.VMEM((N_EXPERTS, 1), F32)],
        compiler_params=_cparams(("arbitrary",)),
        name="router",
    )(h, w_router_t, bias_col)


def _dest_kernel(idx_ref, rank_ref, start_ref, dest_ref, *, tm, n_rows):
    e_io = lax.broadcasted_iota(jnp.int32, (N_EXPERTS, tm), 0)
    starts = start_ref[...]
    rows = []
    for k in range(TOP_K):
        seg = jnp.sum(jnp.where(e_io == idx_ref[k:k + 1, :], starts, 0.0), axis=0, keepdims=True)
        rows.append(seg.astype(jnp.int32) + rank_ref[k:k + 1, :])
    base = jnp.concatenate(rows, axis=0)
    for p in range(N_PIECE):
        dest_ref[p] = base + p * n_rows


def _dest(idx_t, rank_t, seg_start_col, n_rows, tm):
    t = idx_t.shape[1]
    blk = pl.BlockSpec((TOP_K, tm), lambda i: (0, i))
    return pl.pallas_call(
        functools.partial(_dest_kernel, tm=tm, n_rows=n_rows),
        grid=(t // tm,),
        in_specs=[blk, blk, pl.BlockSpec((N_EXPERTS, 1), lambda i: (0, 0))],
        out_specs=pl.BlockSpec((N_PIECE, TOP_K, tm), lambda i: (0, 0, i)),
        out_shape=jax.ShapeDtypeStruct((N_PIECE, TOP_K, t), jnp.int32),
        compiler_params=_cparams(("parallel",)),
        name="dispatch_rows",
    )(idx_t, rank_t, seg_start_col)


def _expert_kernel(be_ref, nu_ref, first_ref, nxt_ref, slot_ref, x_ref, wg_hbm, wu_hbm, wd_hbm, y_ref,
                   wg_buf, wu_buf, wd_buf, wgu_scr, wd_scr, sems):
    i = pl.program_id(0)
    live = i < nu_ref[0]

    def fetch(e, slot):
        return [pltpu.make_async_copy(wg_hbm.at[e], wg_buf.at[slot], sems.at[slot, 0]),
                pltpu.make_async_copy(wu_hbm.at[e], wu_buf.at[slot], sems.at[slot, 1]),
                pltpu.make_async_copy(wd_hbm.at[e], wd_buf.at[slot], sems.at[slot, 2])]

    @pl.when(live & (first_ref[i] == 1))
    def _new_expert():
        slot = slot_ref[i]

        @pl.when(i == 0)
        def _():
            for cp in fetch(be_ref[i], slot):
                cp.start()

        @pl.when(nxt_ref[i] >= 0)
        def _():
            for cp in fetch(nxt_ref[i], 1 - slot):
                cp.start()

        for cp in fetch(be_ref[i], slot):
            cp.wait()
        wgu_scr[:, :D_EXPERT] = wg_buf[slot].astype(BF16)
        wgu_scr[:, D_EXPERT:] = wu_buf[slot].astype(BF16)
        wd_scr[...] = wd_buf[slot].astype(BF16)

    @pl.when(live)
    def _():
        n_sub = 2
        sub = EXPERT_BM // n_sub
        xs = [jnp.concatenate([c.astype(BF16) for c in
                               _load_pieces([x_ref[p, s * sub:(s + 1) * sub] for p in range(N_PIECE)])], axis=1)
              for s in range(n_sub)]
        gus = [jnp.dot(x, wgu_scr[...], preferred_element_type=F32) for x in xs]
        acts = [(_silu(gu[:, :D_EXPERT]) * gu[:, D_EXPERT:]).astype(BF16) for gu in gus]
        ys = [jnp.dot(a, wd_scr[...], preferred_element_type=F32) for a in acts]
        for s, y in enumerate(ys):
            for p in range(N_PIECE):
                y_ref[p, s * sub:(s + 1) * sub] = _pack_halves(y[:, 2 * p * PIECE_W:2 * (p + 1) * PIECE_W])


def _experts(xs, blk_exp, n_used, blk_first, blk_next, blk_slot, w_gate, w_up, w_down):
    n_rows = xs.shape[1]
    n_blocks = n_rows // EXPERT_BM

    def row_block(i, be, nu, *_):
        return (0, jnp.minimum(i, nu[0] - 1), 0)

    grid_spec = pltpu.PrefetchScalarGridSpec(
        num_scalar_prefetch=5,
        grid=(n_blocks,),
        in_specs=[pl.BlockSpec((N_PIECE, EXPERT_BM, PIECE_W), row_block),
                  pl.BlockSpec(memory_space=pl.ANY),
                  pl.BlockSpec(memory_space=pl.ANY),
                  pl.BlockSpec(memory_space=pl.ANY)],
        out_specs=pl.BlockSpec((N_PIECE, EXPERT_BM, PIECE_W), row_block),
        scratch_shapes=[pltpu.VMEM((2, D_MODEL, D_EXPERT), F32),
                        pltpu.VMEM((2, D_MODEL, D_EXPERT), F32),
                        pltpu.VMEM((2, D_EXPERT, D_MODEL), F32),
                        pltpu.VMEM((D_MODEL, 2 * D_EXPERT), BF16),
                        pltpu.VMEM((D_EXPERT, D_MODEL), BF16),
                        pltpu.SemaphoreType.DMA((2, 3))],
    )
    return pl.pallas_call(
        _expert_kernel,
        grid_spec=grid_spec,
        out_shape=jax.ShapeDtypeStruct((N_PIECE, n_rows, PIECE_W), jnp.uint32),
        compiler_params=_cparams(("arbitrary",)),
        name="experts",
    )(blk_exp, n_used, blk_first, blk_next, blk_slot, xs, w_gate, w_up, w_down)


SC_WINDOW = 128
V7X_SC_CORES = 2
V7X_SC_SUBCORES = 16


def _sc_mesh():
    return plsc.VectorSubcoreMesh(core_axis_name="core", subcore_axis_name="subcore",
                                  num_cores=V7X_SC_CORES, num_subcores=V7X_SC_SUBCORES)


def _sc_scatter_rows(src, dest, n_rows, seg, repeat):
    d = src.shape[1]
    n_idx = dest.shape[0]
    seg_blocks = seg // SC_WINDOW
    dest2 = dest.reshape(1, n_idx)

    def src_block(i):
        return ((i // (repeat * seg_blocks)) * seg_blocks + i % seg_blocks, 0)

    @functools.partial(pl.kernel, out_type=jax.ShapeDtypeStruct((n_rows, d), src.dtype), mesh=_sc_mesh(),
                       scratch_types=[], name="sc_dispatch")
    def run(src_hbm, idx_hbm, out_hbm):
        def body(rows_vmem, idx_vmem):
            pltpu.sync_copy(rows_vmem, out_hbm.at[idx_vmem.at[0]])

        pltpu.emit_pipeline(
            body,
            grid=(n_idx // SC_WINDOW,),
            in_specs=[pl.BlockSpec((SC_WINDOW, d), src_block),
                      pl.BlockSpec((1, SC_WINDOW), lambda i: (0, i))],
            out_specs=[],
            core_axis_name=("core", "subcore"),
            dimension_semantics=(pltpu.PARALLEL,),
        )(src_hbm, idx_hbm)

    return run(src, dest2)


def _sc_gather_rows(table, idx):
    d = table.shape[1]
    n_idx = idx.shape[0]
    idx2 = idx.reshape(1, n_idx)

    @functools.partial(pl.kernel, out_type=jax.ShapeDtypeStruct((n_idx, d), table.dtype), mesh=_sc_mesh(),
                       scratch_types=[], name="sc_combine_gather")
    def run(table_hbm, idx_hbm, out_hbm):
        def body(idx_vmem, rows_vmem):
            pltpu.sync_copy(table_hbm.at[idx_vmem.at[0]], rows_vmem)

        pltpu.emit_pipeline(
            body,
            grid=(n_idx // SC_WINDOW,),
            in_specs=[pl.BlockSpec((1, SC_WINDOW), lambda i: (0, i))],
            out_specs=[pl.BlockSpec((SC_WINDOW, d), lambda i: (i, 0))],
            core_axis_name=("core", "subcore"),
            dimension_semantics=(pltpu.PARALLEL,),
        )(idx_hbm, out_hbm)

    return run(table, idx2)


def _combine_kernel(yg_ref, wts_ref, h_ref, wgu_ref, wd_ref, g_ref, b_ref, outa_ref, outb_ref, *, alpha, tiles_a):
    wts = wts_ref[...]
    acc = None
    for k in range(TOP_K):
        cols = _load_pieces([yg_ref[p, k] for p in range(N_PIECE)])
        wk = wts[:, k:k + 1]
        acc = [c * wk for c in cols] if acc is None else [a + c * wk for a, c in zip(acc, cols)]
    routed = jnp.concatenate(acc, axis=1)
    h = h_ref[...]
    gu = _dot(h, wgu_ref[...])
    shared = _dot(_silu(gu[:, :D_EXPERT]) * gu[:, D_EXPERT:], wd_ref[...])
    y = _layer_norm(alpha * h + (routed + shared), g_ref[...], b_ref[...])
    i = pl.program_id(0)

    @pl.when(i < tiles_a)
    def _():
        outa_ref[...] = y

    @pl.when(i >= tiles_a)
    def _():
        outb_ref[...] = y


def _combine(yg, wts, h, w_sh_gu, w_sh_down, ln_g, ln_b, alpha, tm, t_a):
    t = h.shape[0]
    tiles_a = t_a // tm
    row = lambda i: (i, 0)
    const2 = lambda i: (0, 0)
    return pl.pallas_call(
        functools.partial(_combine_kernel, alpha=alpha, tiles_a=tiles_a),
        grid=(t // tm,),
        in_specs=[pl.BlockSpec((N_PIECE, TOP_K, tm, PIECE_W), lambda i: (0, 0, i, 0)),
                  pl.BlockSpec((tm, TOP_K), row),
                  pl.BlockSpec((tm, D_MODEL), row),
                  pl.BlockSpec(w_sh_gu.shape, const2),
                  pl.BlockSpec(w_sh_down.shape, const2),
                  pl.BlockSpec((1, D_MODEL), const2),
                  pl.BlockSpec((1, D_MODEL), const2)],
        out_specs=[pl.BlockSpec((tm, D_MODEL), lambda i: (jnp.minimum(i, tiles_a - 1), 0)),
                   pl.BlockSpec((tm, D_MODEL), lambda i: (jnp.maximum(i - tiles_a, 0), 0))],
        out_shape=[jax.ShapeDtypeStruct((t_a, D_MODEL), F32), jax.ShapeDtypeStruct((t - t_a, D_MODEL), F32)],
        compiler_params=_cparams(("arbitrary",)),
        name="combine_ln2",
    )(yg, wts, h, w_sh_gu, w_sh_down, ln_g, ln_b)


def _layer(xp, xs, s_delta, s_conv, s_pool, w_in, conv_w, a_log, dt_bias, gamma_a, w_br_a, w_pool,
           pool_scale, w_br_b, w_out, ln1_g, ln1_b, w_router, router_bias, w_exp_gate, w_exp_up,
           w_exp_down, w_sh_gate, w_sh_up, w_sh_down, ln2_g, ln2_b, alpha):
    Bp, Lp, _ = xp.shape
    Bs, Ls, _ = xs.shape
    Tp, Ts = Bp * Lp, Bs * Ls
    T = Tp + Ts
    x_p = xp.reshape(Tp, D_MODEL)
    x_s = xs.reshape(Ts, D_MODEL)

    o_z, o_b, o_a, o_u = QKV_W, QKV_W + V_W, QKV_W + V_W + H_V, QKV_W + V_W + 2 * H_V
    w_main = jnp.concatenate([w_in[:, :o_b], w_in[:, o_u:]], axis=1).astype(BF16)
    w_ba = jnp.pad(w_in[:, o_b:o_u], ((0, 0), (0, 128 - 2 * H_V))).astype(BF16)
    proj, ba = _in_proj(x_p, x_s, w_main, w_ba, 1024, 1024)

    o_p, sd_p = _gdn(proj, ba, conv_w, a_log, dt_bias, gamma_a, 0, Bp, Lp, 64, 1)
    buf8 = jnp.pad(s_conv, ((0, 0), (8 - (CONV_W - 1), 0), (0, 0)))
    o_s, sd_s = _gdn(proj, ba, conv_w, a_log, dt_bias, gamma_a, Tp, Bs, Ls, Ls, GDN_STATE_SEQS,
                     buf8=buf8, s0=s_delta)

    pooled_p = _pool(proj, 0, Bp, Lp, 256, 1, 0)
    buf16 = jnp.pad(s_pool, ((0, 0), (1, 0), (0, 0)))
    pooled_s = _pool(proj, Tp, Bs, Ls, Ls, POOL_STATE_SEQS, PAST_LEN, buf16=buf16)

    h, hp = _post(pooled_p, pooled_s, o_p, o_s, proj, x_p, x_s, w_pool.astype(BF16), pool_scale.reshape(1, D_MODEL),
                  w_br_b.astype(BF16), w_br_a.astype(BF16), w_out.astype(BF16),
                  ln1_g.reshape(1, D_MODEL), ln1_b.reshape(1, D_MODEL), alpha, 256)

    idx_t, wts_t, rank_t, cnt = _router(h, w_router.T, router_bias.reshape(N_EXPERTS, 1), 512)
    counts = cnt[:, 0].astype(jnp.int32)
    padded = ((counts + EXPERT_BM - 1) // EXPERT_BM) * EXPERT_BM
    pends = jnp.cumsum(padded)
    pstarts = pends - padded
    n_blocks = (T * TOP_K + N_EXPERTS * (EXPERT_BM - 1) + EXPERT_BM - 1) // EXPERT_BM
    n_rows = n_blocks * EXPERT_BM
    piece_rows = _dest(idx_t, rank_t, pstarts.astype(F32).reshape(N_EXPERTS, 1), n_rows, 512)
    blk_start = jnp.arange(n_blocks, dtype=jnp.int32) * EXPERT_BM
    blk_exp = jnp.minimum(jnp.sum((pends[None, :] <= blk_start[:, None]).astype(jnp.int32), axis=1),
                          N_EXPERTS - 1)
    n_used = (pends[-1:] // EXPERT_BM).astype(jnp.int32)
    has_rows = counts > 0
    e_ids = jnp.arange(N_EXPERTS, dtype=jnp.int32)
    slot_of = (jnp.cumsum(has_rows.astype(jnp.int32)) - 1) & 1
    later = jnp.where(has_rows, e_ids, N_EXPERTS)
    next_of = jnp.concatenate([lax.cummin(later, reverse=True)[1:], jnp.full((1,), N_EXPERTS, jnp.int32)])
    next_of = jnp.where(next_of < N_EXPERTS, next_of, -1)
    blk_first = (blk_start == pstarts[blk_exp]).astype(jnp.int32)
    blk_next = next_of[blk_exp]
    blk_slot = slot_of[blk_exp]

    piece_idx = piece_rows.reshape(N_PIECE * TOP_K * T)
    x_sorted = _sc_scatter_rows(hp.reshape(N_PIECE * T, PIECE_W), piece_idx, N_PIECE * n_rows, T, TOP_K)
    y_sorted = _experts(x_sorted.reshape(N_PIECE, n_rows, PIECE_W), blk_exp, n_used, blk_first, blk_next,
                        blk_slot, w_exp_gate, w_exp_up, w_exp_down)
    yg = _sc_gather_rows(y_sorted.reshape(N_PIECE * n_rows, PIECE_W), piece_idx)
    yg = yg.reshape(N_PIECE, TOP_K, T, PIECE_W)

    w_sh_gu = jnp.concatenate([w_sh_gate, w_sh_up], axis=1).astype(BF16)
    y_p, y_s = _combine(yg, wts_t.T, h, w_sh_gu, w_sh_down.astype(BF16),
                        ln2_g.reshape(1, D_MODEL), ln2_b.reshape(1, D_MODEL), alpha, 256, Tp)

    keep = max(POOL_BUF, CONV_W - 1)

    def last_rows(x2d, B, L):
        n = min(L, keep)
        rows = jnp.arange(B, dtype=jnp.int32)[:, None] * L + (L - n) + jnp.arange(n, dtype=jnp.int32)
        return jnp.take(x2d, rows.reshape(-1), axis=0).astype(BF16), n

    xt_p, n_p = last_rows(x_p, Bp, Lp)
    xt_s, n_s = last_rows(x_s, Bs, Ls)
    n_tail = Bp * n_p + Bs * n_s
    n_pad = -(-n_tail // 128) * 128
    xt = jnp.pad(jnp.concatenate([xt_p, xt_s], axis=0), ((0, n_pad - n_tail), (0, 0)))
    w_tail = jnp.concatenate([w_in[:, :QKV_W], w_in[:, o_u:o_u + D_MODEL]], axis=1).astype(BF16)
    tail = _matmul(xt, w_tail, n_pad, 512, F32, "in_proj_state_rows")
    tail_p = tail[:Bp * n_p].reshape(Bp, n_p, QKV_W + D_MODEL)
    tail_s = tail[Bp * n_p:n_tail].reshape(Bs, n_s, QKV_W + D_MODEL)

    def new_buffers(rows, old_conv, old_pool):
        conv = jnp.concatenate([old_conv, rows[:, :, :QKV_W]], axis=1)[:, -(CONV_W - 1):]
        pool = jnp.concatenate([old_pool, rows[:, :, QKV_W:]], axis=1)[:, -POOL_BUF:]
        return conv, pool

    new_conv_p, new_pool_p = new_buffers(tail_p, jnp.zeros((Bp, CONV_W - 1, QKV_W), F32),
                                         jnp.zeros((Bp, POOL_BUF, D_MODEL), F32))
    new_conv_s, new_pool_s = new_buffers(tail_s, s_conv, s_pool)
    return (y_p.reshape(Bp, Lp, D_MODEL), y_s.reshape(Bs, Ls, D_MODEL),
            sd_p, new_conv_p, new_pool_p, sd_s, new_conv_s, new_pool_s)


def kernel(x_prompt, x_sample, state_delta, state_conv, state_pool, w_in, conv_w, a_log, dt_bias, gamma_a,
           w_br_a, w_pool, pool_scale, w_br_b, w_out, ln1_g, ln1_b, w_router, router_bias,
           w_exp_gate, w_exp_up, w_exp_down, w_sh_gate, w_sh_up, w_sh_down, ln2_g, ln2_b):
    depth = w_in.shape[0]
    alpha = (2 * depth) ** 0.25
    yp, ys = x_prompt, x_sample
    outs = [[] for _ in range(6)]
    for l in range(depth):
        res = _layer(yp, ys, state_delta[l], state_conv[l], state_pool[l], w_in[l], conv_w[l], a_log[l],
                     dt_bias[l], gamma_a[l], w_br_a[l], w_pool[l], pool_scale[l], w_br_b[l], w_out[l],
                     ln1_g[l], ln1_b[l], w_router[l], router_bias[l], w_exp_gate[l], w_exp_up[l],
                     w_exp_down[l], w_sh_gate[l], w_sh_up[l], w_sh_down[l], ln2_g[l], ln2_b[l], alpha)
        yp, ys = res[0], res[1]
        for lst, v in zip(outs, res[2:]):
            lst.append(v)
    return (yp, ys) + tuple(jnp.stack(v) for v in outs)
```

```python
import functools
import math

import jax
import jax.numpy as jnp
from jax import lax
from jax.experimental import pallas as pl
from jax.experimental.pallas import tpu as pltpu
from jax.experimental.pallas import tpu_sc as plsc

F32 = jnp.float32
BF16 = jnp.bfloat16

D_MODEL = 1024
H_K = 8
D_K = 128
H_V = 16
D_V = 128
Q_W = H_K * D_K
V_W = H_V * D_V
QKV_W = 2 * Q_W + V_W
CONV_W = 4
POOL_WINDOWS = (2, 4, 8, 16)
POOL_GROUP_W = D_MODEL // len(POOL_WINDOWS)
POOL_BUF = max(POOL_WINDOWS) - 1
N_EXPERTS = 256
TOP_K = 8
N_GROUP = 8
TOPK_GROUP = 4
GROUP_SZ = N_EXPERTS // N_GROUP
D_EXPERT = D_MODEL // 4
ROUTED_SCALE = 2.5
LN_EPS = 1e-5
NORM_EPS = 1e-6
PAST_LEN = 16384

MAIN_W = QKV_W + V_W + 3 * D_MODEL
Z_BLK = QKV_W // V_W
U_BLK = (QKV_W + V_W) // D_MODEL

STACK_ROWS = 128
SOLVE_BASE = 16
GDN_WAVE = 8
EXPERT_BM = 512
GDN_STATE_SEQS = 4
POOL_STATE_SEQS = 16
VMEM_LIMIT = 56 * 1024 * 1024


def _cparams(sem):
    return pltpu.CompilerParams(dimension_semantics=sem, vmem_limit_bytes=VMEM_LIMIT)


def _sigmoid(x):
    return 0.5 * jnp.tanh(0.5 * x) + 0.5


def _silu(x):
    return x * _sigmoid(x)


def _softplus(x):
    return jnp.maximum(x, 0.0) + jnp.log(1.0 + jnp.exp(-jnp.abs(x)))


def _dot(a, b):
    return jnp.dot(a.astype(BF16), b.astype(BF16), preferred_element_type=F32)


def _dot_nt(a, b):
    return lax.dot_general(a.astype(BF16), b.astype(BF16), (((1,), (1,)), ((), ())),
                           preferred_element_type=F32)


def _dot_tn(a, b):
    return lax.dot_general(a.astype(BF16), b.astype(BF16), (((0,), (0,)), ((), ())),
                           preferred_element_type=F32)


def _split(a):
    hi = a.astype(BF16)
    lo = (a - hi.astype(F32)).astype(BF16)
    return hi, lo


def _dot3_nt(a, b):
    ah, al = _split(a)
    bh, bl = _split(b)
    d = functools.partial(lax.dot_general, dimension_numbers=(((1,), (1,)), ((), ())),
                          preferred_element_type=F32)
    return d(ah, bh) + (d(ah, bl) + d(al, bh))


def _pack_halves(x):
    n = x.shape[1] // 2
    hi = lax.bitcast_convert_type(x[:, :n].astype(BF16).astype(F32), jnp.uint32)
    lo = lax.bitcast_convert_type(x[:, n:].astype(BF16).astype(F32), jnp.uint32)
    return (hi & jnp.uint32(0xFFFF0000)) | (lo >> 16)


def _unpack_halves(w):
    hi = lax.bitcast_convert_type(w & jnp.uint32(0xFFFF0000), F32)
    lo = lax.bitcast_convert_type(w << 16, F32)
    return hi, lo


N_PIECE = 2
PIECE_W = D_MODEL // (2 * N_PIECE)


def _store_pieces(ref, x):
    for p in range(N_PIECE):
        ref[p] = _pack_halves(x[:, 2 * p * PIECE_W:2 * (p + 1) * PIECE_W])


def _load_pieces(pieces):
    cols = []
    for w in pieces:
        cols.extend(_unpack_halves(w))
    return cols


def _layer_norm(x, g, b):
    mu = jnp.mean(x, axis=-1, keepdims=True)
    xc = x - mu
    var = jnp.mean(xc * xc, axis=-1, keepdims=True)
    return xc * lax.rsqrt(var + LN_EPS) * g + b


def _in_proj_kernel(xa_ref, xb_ref, w_ref, wba_ref, o_ref, ba_ref, x_scr, *, tiles_a):
    @pl.when(pl.program_id(1) == 0)
    def _():
        x = jnp.where(pl.program_id(0) < tiles_a, xa_ref[...], xb_ref[...]).astype(BF16)
        x_scr[...] = x
        ba_ref[...] = jnp.dot(x, wba_ref[...], preferred_element_type=F32)

    o_ref[...] = jnp.dot(x_scr[...], w_ref[...], preferred_element_type=F32).astype(o_ref.dtype)


def _in_proj(x_a, x_b, w_main, w_ba, tm, tn):
    k = x_a.shape[1]
    t = x_a.shape[0] + x_b.shape[0]
    n = w_main.shape[1]
    tiles_a = x_a.shape[0] // tm
    return pl.pallas_call(
        functools.partial(_in_proj_kernel, tiles_a=tiles_a),
        grid=(t // tm, n // tn),
        in_specs=[pl.BlockSpec((tm, k), lambda i, j: (jnp.minimum(i, tiles_a - 1), 0)),
                  pl.BlockSpec((tm, k), lambda i, j: (jnp.maximum(i - tiles_a, 0), 0)),
                  pl.BlockSpec((k, tn), lambda i, j: (0, j)),
                  pl.BlockSpec(w_ba.shape, lambda i, j: (0, 0))],
        out_specs=[pl.BlockSpec((tm, tn), lambda i, j: (i, j)),
                   pl.BlockSpec((tm, w_ba.shape[1]), lambda i, j: (i, 0))],
        out_shape=[jax.ShapeDtypeStruct((t, n), BF16), jax.ShapeDtypeStruct((t, w_ba.shape[1]), F32)],
        scratch_shapes=[pltpu.VMEM((tm, k), BF16)],
        compiler_params=_cparams(("parallel", "arbitrary")),
        name="in_proj",
    )(x_a, x_b, w_main, w_ba)


def _mm_kernel(x_ref, w_ref, o_ref):
    o_ref[...] = jnp.dot(x_ref[...], w_ref[...], preferred_element_type=F32).astype(o_ref.dtype)


def _matmul(x, w, col0, n, tm, tn, out_dtype, name):
    t, k = x.shape
    cb0 = col0 // tn
    return pl.pallas_call(
        _mm_kernel,
        grid=(t // tm, n // tn),
        in_specs=[pl.BlockSpec((tm, k), lambda i, j: (i, 0)),
                  pl.BlockSpec((k, tn), lambda i, j: (0, cb0 + j))],
        out_specs=pl.BlockSpec((tm, tn), lambda i, j: (i, j)),
        out_shape=jax.ShapeDtypeStruct((t, n), out_dtype),
        compiler_params=_cparams(("parallel", "parallel")),
        name=name,
    )(x, w)


def _gdn_kernel(*refs, C, hg, nseq, has_state):
    if has_state:
        (qkv_ref, z_ref, ba_ref, cw_ref, ad_ref, gam_ref, buf_ref, s0_ref,
         o_ref, sout_ref, s_scr, tail_scr, act_scr) = refs
    else:
        (qkv_ref, z_ref, ba_ref, cw_ref, ad_ref, gam_ref,
         o_ref, sout_ref, s_scr, tail_scr, act_scr) = refs
    R = hg * C
    ng = H_V // hg
    log2c = int(math.log2(C))
    c = pl.program_id(1)

    @pl.when(c == 0)
    def _init():
        if has_state:
            tail_scr[...] = jnp.zeros(tail_scr.shape, F32)
            for s in range(nseq):
                s_scr[s * H_V:(s + 1) * H_V] = s0_ref[s]
                tail_scr[(s + 1) * 8 - (CONV_W - 1):(s + 1) * 8] = buf_ref[s]
        else:
            s_scr[...] = jnp.zeros(s_scr.shape, F32)
            tail_scr[...] = jnp.zeros(tail_scr.shape, F32)

    cw = cw_ref[...]
    qkv_all = qkv_ref[...].astype(F32)
    n_prev = CONV_W - 1
    if not has_state:
        ext_pad = 128
        sr = lax.broadcasted_iota(jnp.int32, (n_prev * C, ext_pad), 0)
        sc = lax.broadcasted_iota(jnp.int32, (n_prev * C, ext_pad), 1)
        shift_sel = jnp.zeros((n_prev * C, ext_pad), F32)
        for j in range(n_prev):
            in_tap = (sr >= j * C) & (sr < (j + 1) * C)
            shift_sel = jnp.where(in_tap & (sc == sr - j * C + 8 - n_prev + j), 1.0, shift_sel)
        shift_sel = shift_sel.astype(BF16)
    for s in range(nseq):
        x = qkv_all[s * C:(s + 1) * C]
        tail = tail_scr[s * 8:(s + 1) * 8]
        if has_state:
            xe = jnp.concatenate([tail, x], axis=0)
            y = x * cw[n_prev:CONV_W]
            for j in range(n_prev):
                y = y + xe[8 - n_prev + j:8 - n_prev + j + C] * cw[j:j + 1]
        else:
            xe = jnp.concatenate([tail, x, jnp.zeros((ext_pad - 8 - C, QKV_W), F32)], axis=0).astype(BF16)
            moved = jnp.dot(shift_sel, xe, preferred_element_type=F32)
            y = x * cw[n_prev:CONV_W]
            for j in range(n_prev):
                y = y + moved[j * C:(j + 1) * C] * cw[j:j + 1]
        tail_scr[s * 8:(s + 1) * 8] = x[C - 8:C]
        act_scr[s * C:(s + 1) * C] = _silu(y)

    ri = lax.broadcasted_iota(jnp.int32, (R, R), 0)
    ci = lax.broadcasted_iota(jnp.int32, (R, R), 1)
    same = (ri >> log2c) == (ci >> log2c)
    incl = same & (ri >= ci)
    strict = same & (ri > ci)
    eye = ri == ci
    is_last = ci == (((ri >> log2c) << log2c) + (C - 1))
    eye_f = jnp.where(eye, 1.0, 0.0).astype(F32)
    cum_u = jnp.where(same & (ri <= ci), 1.0, 0.0).astype(BF16)

    nrow = ba_ref.shape[2] // 2
    adv = ad_ref[...]
    dd = functools.partial(jnp.dot, preferred_element_type=F32)
    beta_seq, G_seq = [], []
    for s in range(nseq):
        bav = ba_ref[s, 0]
        beta_seq.append(_sigmoid(bav[0:nrow]))
        g_rows = -jnp.exp(adv[0:nrow]) * _softplus(bav[nrow:2 * nrow] + adv[nrow:2 * nrow])
        g1 = g_rows.astype(BF16)
        r1 = g_rows - g1.astype(F32)
        g2 = r1.astype(BF16)
        g3 = (r1 - g2.astype(F32)).astype(BF16)
        G_seq.append(dd(g1, cum_u) + (dd(g2, cum_u) + dd(g3, cum_u)))

    gam = gam_ref[...]
    normed = {}

    def qk_head(s, kind, kh):
        key = (s, kind, kh)
        if key not in normed:
            off = (0 if kind == "q" else Q_W) + kh * D_K
            v = act_scr[s * C:(s + 1) * C, off:off + D_K]
            v = v * lax.rsqrt(jnp.sum(v * v, axis=-1, keepdims=True) + NORM_EPS)
            if kind == "q":
                v = v * (D_K ** -0.5)
            normed[key] = v
        return normed[key]

    rep = H_V // H_K
    log2b = min(log2c, int(math.log2(SOLVE_BASE)))
    same_base = ((ri ^ ci) >> log2b) == 0
    row_head = lax.broadcasted_iota(jnp.int32, (R, D_V), 0) >> log2c
    z_all = z_ref[...].astype(F32)
    specs = [(s, g) for s in range(nseq) for g in range(ng)]
    for w0 in range(0, len(specs), GDN_WAVE):
        grp = []
        for s, g in specs[w0:w0 + GDN_WAVE]:
            rows = slice(s * C, (s + 1) * C)
            heads = [g * hg + hh for hh in range(hg)]
            qst = jnp.concatenate([qk_head(s, "q", h // rep) for h in heads], axis=0)
            kst = jnp.concatenate([qk_head(s, "k", h // rep) for h in heads], axis=0)
            vst = jnp.concatenate([act_scr[rows, 2 * Q_W + h * D_V:2 * Q_W + (h + 1) * D_V] for h in heads], axis=0)
            grow = G_seq[s][g:g + 1]
            brow = beta_seq[s][g:g + 1]
            gcol = jnp.sum(eye_f * grow, axis=1, keepdims=True)
            bcol = jnp.sum(eye_f * brow, axis=1, keepdims=True)
            glast = jnp.sum(jnp.where(is_last, grow, 0.0), axis=1, keepdims=True)
            dm = jnp.where(incl, jnp.exp(jnp.minimum(gcol - grow, 0.0)), 0.0)
            eg = jnp.exp(gcol)
            grp.append(dict(rows=rows, s0=s * H_V, heads=heads, qst=qst, kst=kst, gcol=gcol, bcol=bcol,
                            glast=glast, dm=dm, eg=eg,
                            x=jnp.concatenate([vst * bcol, kst * (bcol * eg)], axis=1)))

        for d in grp:
            d["kk"] = _dot_nt(d["kst"], d["kst"])
            d["qk"] = _dot_nt(d["qst"], d["kst"])
        for d in grp:
            d["m"] = jnp.where(strict, -(d["kk"] * d["bcol"] * d["dm"]), 0.0)
            d["attn"] = jnp.where(incl, d["qk"] * d["dm"], 0.0)
            d["pw"] = jnp.where(same_base, d["m"], 0.0)
            d["t"] = eye_f + d["pw"]

        for r in range(log2b):
            last = r == log2b - 1
            for d in grp:
                if r == 0:
                    if not last:
                        d["pw"] = _dot(d["pw"], d["pw"])
                elif last:
                    d["t"] = d["t"] + _dot(d["pw"], d["t"])
                else:
                    out = _dot(d["pw"], jnp.concatenate([d["pw"], d["t"]], axis=1))
                    d["pw"] = out[:, :R]
                    d["t"] = d["t"] + out[:, R:]
        for lev in range(log2b + 1, log2c + 1):
            coupling = ((ri ^ ci) >> (lev - 1)) == 1
            for d in grp:
                d["y"] = _dot(d["t"], jnp.where(coupling, d["m"], 0.0))
            for d in grp:
                d["t"] = d["t"] + _dot(d["y"], d["t"])
        for d in grp:
            d["x"] = _dot(d["t"], d["x"])

        for d in grp:
            qe = d["qst"] * d["eg"]
            d["xs"] = [_dot(jnp.concatenate([d["x"][hh * C:(hh + 1) * C, D_V:], qe[hh * C:(hh + 1) * C]], axis=0),
                            s_scr[d["s0"] + h]) for hh, h in enumerate(d["heads"])]
        for d in grp:
            d["vnew"] = jnp.concatenate(
                [d["x"][hh * C:(hh + 1) * C, :D_V] - xs[:C] for hh, xs in enumerate(d["xs"])], axis=0)
            qs = jnp.concatenate([xs[C:] for xs in d["xs"]], axis=0)
            d["o"] = qs + _dot(d["attn"], d["vnew"])

        for d in grp:
            kt = d["kst"] * jnp.exp(d["glast"] - d["gcol"])
            egl = jnp.exp(d["glast"])
            for hh, h in enumerate(d["heads"]):
                sl = slice(hh * C, (hh + 1) * C)
                if C >= 16:
                    upd = _dot_tn(kt[sl], d["vnew"][sl])
                else:
                    upd = _dot_tn(kt, jnp.where(row_head == hh, d["vnew"], 0.0))
                s_scr[d["s0"] + h] = s_scr[d["s0"] + h] * egl[hh * C:hh * C + 1] + upd

        for d in grp:
            o = d["o"]
            zst = jnp.concatenate([z_all[d["rows"], h * D_V:(h + 1) * D_V] for h in d["heads"]], axis=0)
            on = o * lax.rsqrt(jnp.mean(o * o, axis=-1, keepdims=True) + NORM_EPS) * gam * _silu(zst)
            for hh, h in enumerate(d["heads"]):
                o_ref[d["rows"], h * D_V:(h + 1) * D_V] = on[hh * C:(hh + 1) * C].astype(o_ref.dtype)

    @pl.when(c == pl.num_programs(1) - 1)
    def _fin():
        for s in range(nseq):
            sout_ref[s] = s_scr[s * H_V:(s + 1) * H_V]


def _gdn(proj, ba, conv_w, a_log, dt_bias, gamma_a, row0, B, L, C, nseq, conv_buf=None, s0=None):
    has_state = s0 is not None
    hg = STACK_ROWS // C
    ng = H_V // hg
    nrow = 16
    nc = L // C
    R = STACK_ROWS
    assert nseq == 1 or nc == 1
    rows_blk = nseq * C

    def arrange(v):
        v = v.reshape(B, nc, C, ng, hg).transpose(0, 1, 3, 4, 2).reshape(B, nc, ng, R)
        return jnp.pad(v, ((0, 0), (0, 0), (0, nrow - ng), (0, 0)))

    rows = ba[row0:row0 + B * L]
    ba_arr = jnp.concatenate([arrange(rows[:, :H_V]), arrange(rows[:, H_V:2 * H_V])], axis=2)

    def arrange_param(p):
        v = jnp.broadcast_to(p.reshape(ng, hg, 1), (ng, hg, C)).reshape(ng, R)
        return jnp.pad(v, ((0, nrow - ng), (0, 0)))

    ad = jnp.concatenate([arrange_param(a_log), arrange_param(dt_bias)], axis=0)
    rb0 = row0 // rows_blk
    in_specs = [
        pl.BlockSpec((rows_blk, QKV_W), lambda b, c: (rb0 + b * nc + c, 0)),
        pl.BlockSpec((rows_blk, V_W), lambda b, c: (rb0 + b * nc + c, Z_BLK)),
        pl.BlockSpec((nseq, 1, 2 * nrow, R), lambda b, c: (b, c, 0, 0)),
        pl.BlockSpec((CONV_W, QKV_W), lambda b, c: (0, 0)),
        pl.BlockSpec((2 * nrow, R), lambda b, c: (0, 0)),
        pl.BlockSpec((1, D_V), lambda b, c: (0, 0)),
    ]
    args = [proj, proj, ba_arr, conv_w, ad, gamma_a.reshape(1, D_V)]
    if has_state:
        in_specs += [pl.BlockSpec((nseq, CONV_W - 1, QKV_W), lambda b, c: (b, 0, 0)),
                     pl.BlockSpec((nseq, H_V, D_K, D_V), lambda b, c: (b, 0, 0, 0))]
        args += [conv_buf, s0]
    return pl.pallas_call(
        functools.partial(_gdn_kernel, C=C, hg=hg, nseq=nseq, has_state=has_state),
        grid=(B // nseq, nc),
        in_specs=in_specs,
        out_specs=[pl.BlockSpec((rows_blk, V_W), lambda b, c: (b * nc + c, 0)),
                   pl.BlockSpec((nseq, H_V, D_K, D_V), lambda b, c: (b, 0, 0, 0))],
        out_shape=[jax.ShapeDtypeStruct((B * L, V_W), BF16),
                   jax.ShapeDtypeStruct((B, H_V, D_K, D_V), F32)],
        scratch_shapes=[pltpu.VMEM((nseq * H_V, D_K, D_V), F32),
                        pltpu.VMEM((nseq * 8, QKV_W), F32),
                        pltpu.VMEM((rows_blk, QKV_W), F32)],
        compiler_params=_cparams(("parallel", "arbitrary")),
        name="gdn_state" if has_state else "gdn_fresh",
    )(*args)


def _pool_kernel(*refs, tc, nseq, start_pos, has_state):
    if has_state:
        u_ref, buf_ref, o_ref, tail_scr = refs
    else:
        u_ref, o_ref, tail_scr = refs
    hist = POOL_BUF + 1
    c = pl.program_id(1)

    @pl.when(c == 0)
    def _init():
        tail_scr[...] = jnp.zeros(tail_scr.shape, F32)
        if has_state:
            for s in range(nseq):
                tail_scr[s * hist + 1:(s + 1) * hist] = buf_ref[s]

    u_all = u_ref[...].astype(F32)
    pos1 = (start_pos + 1 + c * tc + lax.broadcasted_iota(jnp.int32, (tc, 1), 0)).astype(F32)
    for s in range(nseq):
        u = u_all[s * tc:(s + 1) * tc]
        xe = jnp.concatenate([tail_scr[s * hist:(s + 1) * hist], u], axis=0)
        tail_scr[s * hist:(s + 1) * hist] = xe[tc:tc + hist]
        for gi, win in enumerate(POOL_WINDOWS):
            sl = slice(gi * POOL_GROUP_W, (gi + 1) * POOL_GROUP_W)
            acc = xe[:, sl]
            shift = 1
            while shift < win:
                acc = acc + pltpu.roll(acc, shift, 0)
                shift *= 2
            cnt = jnp.minimum(float(win), pos1)
            o_ref[s * tc:(s + 1) * tc, sl] = acc[hist:hist + tc] / cnt - u[:, sl]


def _pool(proj, row0, B, L, tc, nseq, start_pos, pool_buf=None):
    has_state = pool_buf is not None
    nc = L // tc
    assert nseq == 1 or nc == 1
    rows_blk = nseq * tc
    rb0 = row0 // rows_blk
    in_specs = [pl.BlockSpec((rows_blk, D_MODEL), lambda b, c: (rb0 + b * nc + c, U_BLK))]
    args = [proj]
    if has_state:
        in_specs.append(pl.BlockSpec((nseq, POOL_BUF, D_MODEL), lambda b, c: (b, 0, 0)))
        args.append(pool_buf)
    return pl.pallas_call(
        functools.partial(_pool_kernel, tc=tc, nseq=nseq, start_pos=start_pos, has_state=has_state),
        grid=(B // nseq, nc),
        in_specs=in_specs,
        out_specs=pl.BlockSpec((rows_blk, D_MODEL), lambda b, c: (b * nc + c, 0)),
        out_shape=jax.ShapeDtypeStruct((B * L, D_MODEL), F32),
        scratch_shapes=[pltpu.VMEM((nseq * (POOL_BUF + 1), D_MODEL), F32)],
        compiler_params=_cparams(("parallel", "arbitrary")),
        name="pool_state" if has_state else "pool_fresh",
    )(*args)


def _post_kernel(pa_ref, pb_ref, oa_ref, ob_ref, ga_ref, gb_ref, xa_ref, xb_ref, wp_ref, ps_ref, wbb_ref, wba_ref,
                 wo_ref, g_ref, b_ref, h_ref, hp_ref, *, alpha, tiles_a):
    first = pl.program_id(0) < tiles_a
    x = jnp.where(first, xa_ref[...], xb_ref[...])
    pooled = jnp.where(first, pa_ref[...], pb_ref[...])
    o_gated = jnp.where(first, oa_ref[...], ob_ref[...])
    mixed = jnp.concatenate(
        [_dot(pooled[:, gi * POOL_GROUP_W:(gi + 1) * POOL_GROUP_W], wp_ref[gi]) for gi in range(len(POOL_WINDOWS))],
        axis=1) * ps_ref[...]
    branch_b = _dot(mixed, wbb_ref[...])
    branch_a = _dot(o_gated, wba_ref[...])
    merged = _sigmoid(ga_ref[...].astype(F32)) * branch_a + _sigmoid(gb_ref[...].astype(F32)) * branch_b
    h = _layer_norm(alpha * x + _dot(merged, wo_ref[...]), g_ref[...], b_ref[...])
    h_ref[...] = h
    _store_pieces(hp_ref, h)


def _post(pooled_a, pooled_b, o_a, o_b, proj, x_a, x_b, w_pool, pool_scale, w_br_b, w_br_a, w_out, ln_g, ln_b,
          alpha, tm):
    t = x_a.shape[0] + x_b.shape[0]
    tiles_a = x_a.shape[0] // tm
    row = lambda i: (i, 0)
    row_a = lambda i: (jnp.minimum(i, tiles_a - 1), 0)
    row_b = lambda i: (jnp.maximum(i - tiles_a, 0), 0)
    const2 = lambda i: (0, 0)
    return pl.pallas_call(
        functools.partial(_post_kernel, alpha=alpha, tiles_a=tiles_a),
        grid=(t // tm,),
        in_specs=[pl.BlockSpec((tm, D_MODEL), row_a),
                  pl.BlockSpec((tm, D_MODEL), row_b),
                  pl.BlockSpec((tm, V_W), row_a),
                  pl.BlockSpec((tm, V_W), row_b),
                  pl.BlockSpec((tm, D_MODEL), lambda i: (i, U_BLK + 1)),
                  pl.BlockSpec((tm, D_MODEL), lambda i: (i, U_BLK + 2)),
                  pl.BlockSpec((tm, D_MODEL), row_a),
                  pl.BlockSpec((tm, D_MODEL), row_b),
                  pl.BlockSpec(w_pool.shape, lambda i: (0, 0, 0)),
                  pl.BlockSpec((1, D_MODEL), const2),
                  pl.BlockSpec(w_br_b.shape, const2),
                  pl.BlockSpec(w_br_a.shape, const2),
                  pl.BlockSpec(w_out.shape, const2),
                  pl.BlockSpec((1, D_MODEL), const2),
                  pl.BlockSpec((1, D_MODEL), const2)],
        out_specs=[pl.BlockSpec((tm, D_MODEL), row), pl.BlockSpec((N_PIECE, tm, PIECE_W), lambda i: (0, i, 0))],
        out_shape=[jax.ShapeDtypeStruct((t, D_MODEL), F32),
                   jax.ShapeDtypeStruct((N_PIECE, t, PIECE_W), jnp.uint32)],
        compiler_params=_cparams(("parallel",)),
        name="post_mixers",
    )(pooled_a, pooled_b, o_a, o_b, proj, proj, x_a, x_b, w_pool, pool_scale, w_br_b, w_br_a, w_out, ln_g, ln_b)


def _router_kernel(h_ref, wt_ref, bias_ref, idx_ref, wts_ref, rank_ref, cnt_ref, carry_scr, *, tm):
    i = pl.program_id(0)

    @pl.when(i == 0)
    def _init():
        carry_scr[...] = jnp.zeros(carry_scr.shape, F32)

    logits = _dot3_nt(wt_ref[...], h_ref[...])
    sc = _sigmoid(logits)
    ch = sc + bias_ref[...]
    neg = -jnp.inf
    e_in = lax.broadcasted_iota(jnp.int32, (GROUP_SZ, tm), 0)
    gs_rows = []
    for g in range(N_GROUP):
        blk = ch[g * GROUP_SZ:(g + 1) * GROUP_SZ]
        m1 = jnp.max(blk, axis=0, keepdims=True)
        i1 = jnp.min(jnp.where(blk == m1, e_in, GROUP_SZ), axis=0, keepdims=True)
        m2 = jnp.max(jnp.where(e_in == i1, neg, blk), axis=0, keepdims=True)
        gs_rows.append(m1 + m2)
    gs = jnp.concatenate(gs_rows, axis=0)
    g_io = lax.broadcasted_iota(jnp.int32, (N_GROUP, tm), 0)
    e_io = lax.broadcasted_iota(jnp.int32, (N_EXPERTS, tm), 0)
    e_grp = e_io >> int(math.log2(GROUP_SZ))
    masked = jnp.full((N_EXPERTS, tm), neg, F32)
    for _ in range(TOPK_GROUP):
        mx = jnp.max(gs, axis=0, keepdims=True)
        gi = jnp.min(jnp.where(gs == mx, g_io, N_GROUP), axis=0, keepdims=True)
        gs = jnp.where(g_io == gi, neg, gs)
        masked = jnp.where(e_grp == gi, ch, masked)
    idx_rows, w_rows = [], []
    onehot = jnp.zeros((N_EXPERTS, tm), F32)
    for _ in range(TOP_K):
        mx = jnp.max(masked, axis=0, keepdims=True)
        ei = jnp.min(jnp.where(masked == mx, e_io, N_EXPERTS), axis=0, keepdims=True)
        hit = e_io == ei
        idx_rows.append(ei)
        w_rows.append(jnp.sum(jnp.where(hit, sc, 0.0), axis=0, keepdims=True))
        onehot = jnp.where(hit, 1.0, onehot)
        masked = jnp.where(hit, neg, masked)
    wsel = jnp.concatenate(w_rows, axis=0)
    wts_ref[...] = wsel / jnp.sum(wsel, axis=0, keepdims=True) * ROUTED_SCALE
    idx_ref[...] = jnp.concatenate(idx_rows, axis=0)

    tr = lax.broadcasted_iota(jnp.int32, (tm, tm), 0)
    tc_ = lax.broadcasted_iota(jnp.int32, (tm, tm), 1)
    before = jnp.where(tr < tc_, 1.0, 0.0).astype(BF16)
    cum = jnp.dot(onehot.astype(BF16), before, preferred_element_type=F32) + carry_scr[...]
    rank_ref[...] = jnp.concatenate(
        [jnp.sum(jnp.where(e_io == ei, cum, 0.0), axis=0, keepdims=True) for ei in idx_rows],
        axis=0).astype(jnp.int32)
    carry_scr[...] = carry_scr[...] + jnp.sum(onehot, axis=1, keepdims=True)
    cnt_ref[...] = carry_scr[...]


def _router(h, w_router_t, bias_col, tm):
    t = h.shape[0]
    return pl.pallas_call(
        functools.partial(_router_kernel, tm=tm),
        grid=(t // tm,),
        in_specs=[pl.BlockSpec((tm, D_MODEL), lambda i: (i, 0)),
                  pl.BlockSpec((N_EXPERTS, D_MODEL), lambda i: (0, 0)),
                  pl.BlockSpec((N_EXPERTS, 1), lambda i: (0, 0))],
        out_specs=[pl.BlockSpec((TOP_K, tm), lambda i: (0, i)),
                   pl.BlockSpec((TOP_K, tm), lambda i: (0, i)),
                   pl.BlockSpec((TOP_K, tm), lambda i: (0, i)),
                   pl.BlockSpec((N_EXPERTS, 1), lambda i: (0, 0))],
        out_shape=[jax.ShapeDtypeStruct((TOP_K, t), jnp.int32),
                   jax.ShapeDtypeStruct((TOP_K, t), F32),
                   jax.ShapeDtypeStruct((TOP_K, t), jnp.int32),
                   jax.ShapeDtypeStruct((N_EXPERTS, 1), F32)],
        scratch_shapes=[pltpu.VMEM((N_EXPERTS, 1), F32)],
        compiler_params=_cparams(("arbitrary",)),
        name="router",
    )(h, w_router_t, bias_col)


def _dest_kernel(idx_ref, rank_ref, start_ref, dest_ref, *, tm, n_rows):
    e_io = lax.broadcasted_iota(jnp.int32, (N_EXPERTS, tm), 0)
    starts = start_ref[...]
    rows = []
    for k in range(TOP_K):
        seg = jnp.sum(jnp.where(e_io == idx_ref[k:k + 1, :], starts, 0.0), axis=0, keepdims=True)
        rows.append(seg.astype(jnp.int32) + rank_ref[k:k + 1, :])
    base = jnp.concatenate(rows, axis=0)
    for p in range(N_PIECE):
        dest_ref[p] = base + p * n_rows


def _dest(idx_t, rank_t, seg_start_col, n_rows, tm):
    t = idx_t.shape[1]
    blk = pl.BlockSpec((TOP_K, tm), lambda i: (0, i))
    return pl.pallas_call(
        functools.partial(_dest_kernel, tm=tm, n_rows=n_rows),
        grid=(t // tm,),
        in_specs=[blk, blk, pl.BlockSpec((N_EXPERTS, 1), lambda i: (0, 0))],
        out_specs=pl.BlockSpec((N_PIECE, TOP_K, tm), lambda i: (0, 0, i)),
        out_shape=jax.ShapeDtypeStruct((N_PIECE, TOP_K, t), jnp.int32),
        compiler_params=_cparams(("parallel",)),
        name="dispatch_rows",
    )(idx_t, rank_t, seg_start_col)


def _expert_kernel(be_ref, nu_ref, first_ref, nxt_ref, slot_ref, x_ref, wg_hbm, wu_hbm, wd_hbm, y_ref,
                   wg_buf, wu_buf, wd_buf, wgu_scr, wd_scr, sems):
    i = pl.program_id(0)
    live = i < nu_ref[0]

    def fetch(e, slot):
        return [pltpu.make_async_copy(wg_hbm.at[e], wg_buf.at[slot], sems.at[slot, 0]),
                pltpu.make_async_copy(wu_hbm.at[e], wu_buf.at[slot], sems.at[slot, 1]),
                pltpu.make_async_copy(wd_hbm.at[e], wd_buf.at[slot], sems.at[slot, 2])]

    @pl.when(live & (first_ref[i] == 1))
    def _new_expert():
        slot = slot_ref[i]

        @pl.when(i == 0)
        def _():
            for cp in fetch(be_ref[i], slot):
                cp.start()

        @pl.when(nxt_ref[i] >= 0)
        def _():
            for cp in fetch(nxt_ref[i], 1 - slot):
                cp.start()

        for cp in fetch(be_ref[i], slot):
            cp.wait()
        wgu_scr[:, :D_EXPERT] = wg_buf[slot].astype(BF16)
        wgu_scr[:, D_EXPERT:] = wu_buf[slot].astype(BF16)
        wd_scr[...] = wd_buf[slot].astype(BF16)

    @pl.when(live)
    def _():
        n_sub = 2
        sub = EXPERT_BM // n_sub
        xs = [jnp.concatenate([c.astype(BF16) for c in
                               _load_pieces([x_ref[p, s * sub:(s + 1) * sub] for p in range(N_PIECE)])], axis=1)
              for s in range(n_sub)]
        gus = [jnp.dot(x, wgu_scr[...], preferred_element_type=F32) for x in xs]
        acts = [(_silu(gu[:, :D_EXPERT]) * gu[:, D_EXPERT:]).astype(BF16) for gu in gus]
        ys = [jnp.dot(a, wd_scr[...], preferred_element_type=F32) for a in acts]
        for s, y in enumerate(ys):
            for p in range(N_PIECE):
                y_ref[p, s * sub:(s + 1) * sub] = _pack_halves(y[:, 2 * p * PIECE_W:2 * (p + 1) * PIECE_W])


def _experts(xs, blk_exp, n_used, blk_first, blk_next, blk_slot, w_gate, w_up, w_down):
    n_rows = xs.shape[1]
    n_blocks = n_rows // EXPERT_BM

    def row_block(i, be, nu, *_):
        return (0, jnp.minimum(i, nu[0] - 1), 0)

    grid_spec = pltpu.PrefetchScalarGridSpec(
        num_scalar_prefetch=5,
        grid=(n_blocks,),
        in_specs=[pl.BlockSpec((N_PIECE, EXPERT_BM, PIECE_W), row_block),
                  pl.BlockSpec(memory_space=pl.ANY),
                  pl.BlockSpec(memory_space=pl.ANY),
                  pl.BlockSpec(memory_space=pl.ANY)],
        out_specs=pl.BlockSpec((N_PIECE, EXPERT_BM, PIECE_W), row_block),
        scratch_shapes=[pltpu.VMEM((2, D_MODEL, D_EXPERT), F32),
                        pltpu.VMEM((2, D_MODEL, D_EXPERT), F32),
                        pltpu.VMEM((2, D_EXPERT, D_MODEL), F32),
                        pltpu.VMEM((D_MODEL, 2 * D_EXPERT), BF16),
                        pltpu.VMEM((D_EXPERT, D_MODEL), BF16),
                        pltpu.SemaphoreType.DMA((2, 3))],
    )
    return pl.pallas_call(
        _expert_kernel,
        grid_spec=grid_spec,
        out_shape=jax.ShapeDtypeStruct((N_PIECE, n_rows, PIECE_W), jnp.uint32),
        compiler_params=_cparams(("arbitrary",)),
        name="experts",
    )(blk_exp, n_used, blk_first, blk_next, blk_slot, xs, w_gate, w_up, w_down)


SC_WINDOW = 128
V7X_SC_CORES = 2
V7X_SC_SUBCORES = 16


def _sc_mesh():
    return plsc.VectorSubcoreMesh(core_axis_name="core", subcore_axis_name="subcore",
                                  num_cores=V7X_SC_CORES, num_subcores=V7X_SC_SUBCORES)


def _sc_scatter_rows(src, dest, n_rows, seg, repeat):
    d = src.shape[1]
    n_idx = dest.shape[0]
    seg_blocks = seg // SC_WINDOW
    dest2 = dest.reshape(1, n_idx)

    def src_block(i):
        return ((i // (repeat * seg_blocks)) * seg_blocks + i % seg_blocks, 0)

    @functools.partial(pl.kernel, out_type=jax.ShapeDtypeStruct((n_rows, d), src.dtype), mesh=_sc_mesh(),
                       scratch_types=[], name="sc_dispatch")
    def run(src_hbm, idx_hbm, out_hbm):
        def body(rows_vmem, idx_vmem):
            pltpu.sync_copy(rows_vmem, out_hbm.at[idx_vmem.at[0]])

        pltpu.emit_pipeline(
            body,
            grid=(n_idx // SC_WINDOW,),
            in_specs=[pl.BlockSpec((SC_WINDOW, d), src_block),
                      pl.BlockSpec((1, SC_WINDOW), lambda i: (0, i))],
            out_specs=[],
            core_axis_name=("core", "subcore"),
            dimension_semantics=(pltpu.PARALLEL,),
        )(src_hbm, idx_hbm)

    return run(src, dest2)


def _sc_gather_rows(table, idx):
    d = table.shape[1]
    n_idx = idx.shape[0]
    idx2 = idx.reshape(1, n_idx)

    @functools.partial(pl.kernel, out_type=jax.ShapeDtypeStruct((n_idx, d), table.dtype), mesh=_sc_mesh(),
                       scratch_types=[], name="sc_combine_gather")
    def run(table_hbm, idx_hbm, out_hbm):
        def body(idx_vmem, rows_vmem):
            pltpu.sync_copy(table_hbm.at[idx_vmem.at[0]], rows_vmem)

        pltpu.emit_pipeline(
            body,
            grid=(n_idx // SC_WINDOW,),
            in_specs=[pl.BlockSpec((1, SC_WINDOW), lambda i: (0, i))],
            out_specs=[pl.BlockSpec((SC_WINDOW, d), lambda i: (i, 0))],
            core_axis_name=("core", "subcore"),
            dimension_semantics=(pltpu.PARALLEL,),
        )(idx_hbm, out_hbm)

    return run(table, idx2)


def _combine_kernel(yg_ref, wts_ref, h_ref, wgu_ref, wd_ref, g_ref, b_ref, outa_ref, outb_ref, *, alpha, tiles_a):
    wts = wts_ref[...]
    acc = None
    for k in range(TOP_K):
        cols = _load_pieces([yg_ref[p, k] for p in range(N_PIECE)])
        wk = wts[:, k:k + 1]
        acc = [c * wk for c in cols] if acc is None else [a + c * wk for a, c in zip(acc, cols)]
    routed = jnp.concatenate(acc, axis=1)
    h = h_ref[...]
    gu = _dot(h, wgu_ref[...])
    shared = _dot(_silu(gu[:, :D_EXPERT]) * gu[:, D_EXPERT:], wd_ref[...])
    y = _layer_norm(alpha * h + (routed + shared), g_ref[...], b_ref[...])
    i = pl.program_id(0)

    @pl.when(i < tiles_a)
    def _():
        outa_ref[...] = y

    @pl.when(i >= tiles_a)
    def _():
        outb_ref[...] = y


def _combine(yg, wts, h, w_sh_gu, w_sh_down, ln_g, ln_b, alpha, tm, t_a):
    t = h.shape[0]
    tiles_a = t_a // tm
    row = lambda i: (i, 0)
    const2 = lambda i: (0, 0)
    return pl.pallas_call(
        functools.partial(_combine_kernel, alpha=alpha, tiles_a=tiles_a),
        grid=(t // tm,),
        in_specs=[pl.BlockSpec((N_PIECE, TOP_K, tm, PIECE_W), lambda i: (0, 0, i, 0)),
                  pl.BlockSpec((tm, TOP_K), row),
                  pl.BlockSpec((tm, D_MODEL), row),
                  pl.BlockSpec(w_sh_gu.shape, const2),
                  pl.BlockSpec(w_sh_down.shape, const2),
                  pl.BlockSpec((1, D_MODEL), const2),
                  pl.BlockSpec((1, D_MODEL), const2)],
        out_specs=[pl.BlockSpec((tm, D_MODEL), lambda i: (jnp.minimum(i, tiles_a - 1), 0)),
                   pl.BlockSpec((tm, D_MODEL), lambda i: (jnp.maximum(i - tiles_a, 0), 0))],
        out_shape=[jax.ShapeDtypeStruct((t_a, D_MODEL), F32), jax.ShapeDtypeStruct((t - t_a, D_MODEL), F32)],
        compiler_params=_cparams(("arbitrary",)),
        name="combine_ln2",
    )(yg, wts, h, w_sh_gu, w_sh_down, ln_g, ln_b)


def _layer(xp, xs, s_delta, s_conv, s_pool, w_in, conv_w, a_log, dt_bias, gamma_a, w_br_a, w_pool,
           pool_scale, w_br_b, w_out, ln1_g, ln1_b, w_router, router_bias, w_exp_gate, w_exp_up,
           w_exp_down, w_sh_gate, w_sh_up, w_sh_down, ln2_g, ln2_b, alpha):
    Bp, Lp, _ = xp.shape
    Bs, Ls, _ = xs.shape
    Tp, Ts = Bp * Lp, Bs * Ls
    T = Tp + Ts
    x_p = xp.reshape(Tp, D_MODEL)
    x_s = xs.reshape(Ts, D_MODEL)

    o_z, o_b, o_a, o_u = QKV_W, QKV_W + V_W, QKV_W + V_W + H_V, QKV_W + V_W + 2 * H_V
    w_main = jnp.concatenate([w_in[:, :o_b], w_in[:, o_u:]], axis=1).astype(BF16)
    w_ba = jnp.pad(w_in[:, o_b:o_u], ((0, 0), (0, 128 - 2 * H_V))).astype(BF16)
    proj, ba = _in_proj(x_p, x_s, w_main, w_ba, 1024, 1024)

    o_p, sd_p = _gdn(proj, ba, conv_w, a_log, dt_bias, gamma_a, 0, Bp, Lp, 64, 1)
    o_s, sd_s = _gdn(proj, ba, conv_w, a_log, dt_bias, gamma_a, Tp, Bs, Ls, Ls, GDN_STATE_SEQS,
                     conv_buf=s_conv, s0=s_delta)

    pooled_p = _pool(proj, 0, Bp, Lp, 256, 1, 0)
    pooled_s = _pool(proj, Tp, Bs, Ls, Ls, POOL_STATE_SEQS, PAST_LEN, pool_buf=s_pool)

    h, hp = _post(pooled_p, pooled_s, o_p, o_s, proj, x_p, x_s, w_pool.astype(BF16), pool_scale.reshape(1, D_MODEL),
                  w_br_b.astype(BF16), w_br_a.astype(BF16), w_out.astype(BF16),
                  ln1_g.reshape(1, D_MODEL), ln1_b.reshape(1, D_MODEL), alpha, 256)

    idx_t, wts_t, rank_t, cnt = _router(h, w_router.T, router_bias.reshape(N_EXPERTS, 1), 512)
    counts = cnt[:, 0].astype(jnp.int32)
    padded = ((counts + EXPERT_BM - 1) // EXPERT_BM) * EXPERT_BM
    pends = jnp.cumsum(padded)
    pstarts = pends - padded
    n_blocks = (T * TOP_K + N_EXPERTS * (EXPERT_BM - 1) + EXPERT_BM - 1) // EXPERT_BM
    n_rows = n_blocks * EXPERT_BM
    piece_rows = _dest(idx_t, rank_t, pstarts.astype(F32).reshape(N_EXPERTS, 1), n_rows, 512)
    blk_start = jnp.arange(n_blocks, dtype=jnp.int32) * EXPERT_BM
    blk_exp = jnp.minimum(jnp.sum((pends[None, :] <= blk_start[:, None]).astype(jnp.int32), axis=1),
                          N_EXPERTS - 1)
    n_used = (pends[-1:] // EXPERT_BM).astype(jnp.int32)
    has_rows = counts > 0
    e_ids = jnp.arange(N_EXPERTS, dtype=jnp.int32)
    slot_of = (jnp.cumsum(has_rows.astype(jnp.int32)) - 1) & 1
    later = jnp.where(has_rows, e_ids, N_EXPERTS)
    next_of = jnp.concatenate([lax.cummin(later, reverse=True)[1:], jnp.full((1,), N_EXPERTS, jnp.int32)])
    next_of = jnp.where(next_of < N_EXPERTS, next_of, -1)
    blk_first = (blk_start == pstarts[blk_exp]).astype(jnp.int32)
    blk_next = next_of[blk_exp]
    blk_slot = slot_of[blk_exp]

    piece_idx = piece_rows.reshape(N_PIECE * TOP_K * T)
    x_sorted = _sc_scatter_rows(hp.reshape(N_PIECE * T, PIECE_W), piece_idx, N_PIECE * n_rows, T, TOP_K)
    y_sorted = _experts(x_sorted.reshape(N_PIECE, n_rows, PIECE_W), blk_exp, n_used, blk_first, blk_next,
                        blk_slot, w_exp_gate, w_exp_up, w_exp_down)
    yg = _sc_gather_rows(y_sorted.reshape(N_PIECE * n_rows, PIECE_W), piece_idx)
    yg = yg.reshape(N_PIECE, TOP_K, T, PIECE_W)

    w_sh_gu = jnp.concatenate([w_sh_gate, w_sh_up], axis=1).astype(BF16)
    y_p, y_s = _combine(yg, wts_t.T, h, w_sh_gu, w_sh_down.astype(BF16),
                        ln2_g.reshape(1, D_MODEL), ln2_b.reshape(1, D_MODEL), alpha, 256, Tp)

    def carried_rows(keep, col0, width, name, old_p, old_s):
        parts, counts = [], []
        for x2d, B, L in ((x_p, Bp, Lp), (x_s, Bs, Ls)):
            n = min(L, keep)
            rows = jnp.arange(B, dtype=jnp.int32)[:, None] * L + (L - n) + jnp.arange(n, dtype=jnp.int32)
            parts.append(jnp.take(x2d, rows.reshape(-1), axis=0).astype(BF16))
            counts.append((B, n))
        n_tail = sum(B * n for B, n in counts)
        n_pad = -(-n_tail // 128) * 128
        xt = jnp.pad(jnp.concatenate(parts, axis=0), ((0, n_pad - n_tail), (0, 0)))
        tail = _matmul(xt, w_main, col0, width, n_pad, 1024, F32, name)
        outs, r0 = [], 0
        for (B, n), old in zip(counts, (old_p, old_s)):
            new = tail[r0:r0 + B * n].reshape(B, n, width)
            outs.append(new if n == keep else jnp.concatenate([old[:, n:], new], axis=1))
            r0 += B * n
        return outs

    new_conv_p, new_conv_s = carried_rows(CONV_W - 1, 0, QKV_W, "in_proj_conv_rows",
                                          jnp.zeros((Bp, CONV_W - 1, QKV_W), F32), s_conv)
    new_pool_p, new_pool_s = carried_rows(POOL_BUF, U_BLK * D_MODEL, D_MODEL, "in_proj_pool_rows",
                                          jnp.zeros((Bp, POOL_BUF, D_MODEL), F32), s_pool)
    return (y_p.reshape(Bp, Lp, D_MODEL), y_s.reshape(Bs, Ls, D_MODEL),
            sd_p, new_conv_p, new_pool_p, sd_s, new_conv_s, new_pool_s)


def kernel(x_prompt, x_sample, state_delta, state_conv, state_pool, w_in, conv_w, a_log, dt_bias, gamma_a,
           w_br_a, w_pool, pool_scale, w_br_b, w_out, ln1_g, ln1_b, w_router, router_bias,
           w_exp_gate, w_exp_up, w_exp_down, w_sh_gate, w_sh_up, w_sh_down, ln2_g, ln2_b):
    depth = w_in.shape[0]
    alpha = (2 * depth) ** 0.25
    yp, ys = x_prompt, x_sample
    outs = [[] for _ in range(6)]
    for l in range(depth):
        res = _layer(yp, ys, state_delta[l], state_conv[l], state_pool[l], w_in[l], conv_w[l], a_log[l],
                     dt_bias[l], gamma_a[l], w_br_a[l], w_pool[l], pool_scale[l], w_br_b[l], w_out[l],
                     ln1_g[l], ln1_b[l], w_router[l], router_bias[l], w_exp_gate[l], w_exp_up[l],
                     w_exp_down[l], w_sh_gate[l], w_sh_up[l], w_sh_down[l], ln2_g[l], ln2_b[l], alpha)
        yp, ys = res[0], res[1]
        for lst, v in zip(outs, res[2:]):
            lst.append(v)
    return (yp, ys) + tuple(jnp.stack(v) for v in outs)
```

```python
import functools
import math

import jax
import jax.numpy as jnp
from jax import lax
from jax.experimental import pallas as pl
from jax.experimental.pallas import tpu as pltpu
from jax.experimental.pallas import tpu_sc as plsc

F32 = jnp.float32
BF16 = jnp.bfloat16

D_MODEL = 1024
H_K = 8
D_K = 128
H_V = 16
D_V = 128
Q_W = H_K * D_K
V_W = H_V * D_V
QKV_W = 2 * Q_W + V_W
CONV_W = 4
POOL_WINDOWS = (2, 4, 8, 16)
POOL_GROUP_W = D_MODEL // len(POOL_WINDOWS)
POOL_BUF = max(POOL_WINDOWS) - 1
N_EXPERTS = 256
TOP_K = 8
N_GROUP = 8
TOPK_GROUP = 4
GROUP_SZ = N_EXPERTS // N_GROUP
D_EXPERT = D_MODEL // 4
ROUTED_SCALE = 2.5
LN_EPS = 1e-5
NORM_EPS = 1e-6
PAST_LEN = 16384

MAIN_W = QKV_W + V_W + 3 * D_MODEL
Z_BLK = QKV_W // V_W
U_BLK = (QKV_W + V_W) // D_MODEL

STACK_ROWS = 128
SOLVE_BASE = 16
GDN_WAVE = 8
EXPERT_BM = 256
GDN_STATE_SEQS = 4
POOL_STATE_SEQS = 16
VMEM_LIMIT = 56 * 1024 * 1024


def _cparams(sem):
    return pltpu.CompilerParams(dimension_semantics=sem, vmem_limit_bytes=VMEM_LIMIT)


def _sigmoid(x):
    return 0.5 * jnp.tanh(0.5 * x) + 0.5


def _silu(x):
    return x * _sigmoid(x)


def _softplus(x):
    return jnp.maximum(x, 0.0) + jnp.log(1.0 + jnp.exp(-jnp.abs(x)))


def _dot(a, b):
    return jnp.dot(a.astype(BF16), b.astype(BF16), preferred_element_type=F32)


def _dot_nt(a, b):
    return lax.dot_general(a.astype(BF16), b.astype(BF16), (((1,), (1,)), ((), ())),
                           preferred_element_type=F32)


def _dot_tn(a, b):
    return lax.dot_general(a.astype(BF16), b.astype(BF16), (((0,), (0,)), ((), ())),
                           preferred_element_type=F32)


def _split(a):
    hi = a.astype(BF16)
    lo = (a - hi.astype(F32)).astype(BF16)
    return hi, lo


def _dot3_nt(a, b):
    ah, al = _split(a)
    bh, bl = _split(b)
    d = functools.partial(lax.dot_general, dimension_numbers=(((1,), (1,)), ((), ())),
                          preferred_element_type=F32)
    return d(ah, bh) + (d(ah, bl) + d(al, bh))


def _pack_halves(x):
    n = x.shape[1] // 2
    hi = lax.bitcast_convert_type(x[:, :n].astype(BF16).astype(F32), jnp.uint32)
    lo = lax.bitcast_convert_type(x[:, n:].astype(BF16).astype(F32), jnp.uint32)
    return (hi & jnp.uint32(0xFFFF0000)) | (lo >> 16)


def _unpack_halves(w):
    hi = lax.bitcast_convert_type(w & jnp.uint32(0xFFFF0000), F32)
    lo = lax.bitcast_convert_type(w << 16, F32)
    return hi, lo


N_PIECE = 2
PIECE_W = D_MODEL // (2 * N_PIECE)


def _store_pieces(ref, x):
    for p in range(N_PIECE):
        ref[p] = _pack_halves(x[:, 2 * p * PIECE_W:2 * (p + 1) * PIECE_W])


def _load_pieces(pieces):
    cols = []
    for w in pieces:
        cols.extend(_unpack_halves(w))
    return cols


def _layer_norm(x, g, b):
    mu = jnp.mean(x, axis=-1, keepdims=True)
    xc = x - mu
    var = jnp.mean(xc * xc, axis=-1, keepdims=True)
    return xc * lax.rsqrt(var + LN_EPS) * g + b


def _in_proj_kernel(xa_ref, xb_ref, w_ref, wba_ref, o_ref, ba_ref, x_scr, *, tiles_a):
    @pl.when(pl.program_id(1) == 0)
    def _():
        x = jnp.where(pl.program_id(0) < tiles_a, xa_ref[...], xb_ref[...]).astype(BF16)
        x_scr[...] = x
        ba_ref[...] = jnp.dot(x, wba_ref[...], preferred_element_type=F32)

    o_ref[...] = jnp.dot(x_scr[...], w_ref[...], preferred_element_type=F32).astype(o_ref.dtype)


def _in_proj(x_a, x_b, w_main, w_ba, tm, tn):
    k = x_a.shape[1]
    t = x_a.shape[0] + x_b.shape[0]
    n = w_main.shape[1]
    tiles_a = x_a.shape[0] // tm
    return pl.pallas_call(
        functools.partial(_in_proj_kernel, tiles_a=tiles_a),
        grid=(t // tm, n // tn),
        in_specs=[pl.BlockSpec((tm, k), lambda i, j: (jnp.minimum(i, tiles_a - 1), 0)),
                  pl.BlockSpec((tm, k), lambda i, j: (jnp.maximum(i - tiles_a, 0), 0)),
                  pl.BlockSpec((k, tn), lambda i, j: (0, j)),
                  pl.BlockSpec(w_ba.shape, lambda i, j: (0, 0))],
        out_specs=[pl.BlockSpec((tm, tn), lambda i, j: (i, j)),
                   pl.BlockSpec((tm, w_ba.shape[1]), lambda i, j: (i, 0))],
        out_shape=[jax.ShapeDtypeStruct((t, n), BF16), jax.ShapeDtypeStruct((t, w_ba.shape[1]), F32)],
        scratch_shapes=[pltpu.VMEM((tm, k), BF16)],
        compiler_params=_cparams(("parallel", "arbitrary")),
        name="in_proj",
    )(x_a, x_b, w_main, w_ba)


def _mm_kernel(x_ref, w_ref, o_ref):
    o_ref[...] = jnp.dot(x_ref[...], w_ref[...], preferred_element_type=F32).astype(o_ref.dtype)


def _matmul(x, w, col0, n, tm, tn, out_dtype, name):
    t, k = x.shape
    cb0 = col0 // tn
    return pl.pallas_call(
        _mm_kernel,
        grid=(t // tm, n // tn),
        in_specs=[pl.BlockSpec((tm, k), lambda i, j: (i, 0)),
                  pl.BlockSpec((k, tn), lambda i, j: (0, cb0 + j))],
        out_specs=pl.BlockSpec((tm, tn), lambda i, j: (i, j)),
        out_shape=jax.ShapeDtypeStruct((t, n), out_dtype),
        compiler_params=_cparams(("parallel", "parallel")),
        name=name,
    )(x, w)


def _gdn_kernel(*refs, C, hg, nseq, has_state):
    if has_state:
        (qkv_ref, z_ref, ba_ref, cw_ref, ad_ref, gam_ref, buf_ref, s0_ref,
         o_ref, sout_ref, s_scr, tail_scr, act_scr) = refs
    else:
        (qkv_ref, z_ref, ba_ref, cw_ref, ad_ref, gam_ref,
         o_ref, sout_ref, s_scr, tail_scr, act_scr) = refs
    R = hg * C
    ng = H_V // hg
    log2c = int(math.log2(C))
    c = pl.program_id(1)

    @pl.when(c == 0)
    def _init():
        if has_state:
            tail_scr[...] = jnp.zeros(tail_scr.shape, F32)
            for s in range(nseq):
                s_scr[s * H_V:(s + 1) * H_V] = s0_ref[s]
                tail_scr[(s + 1) * 8 - (CONV_W - 1):(s + 1) * 8] = buf_ref[s]
        else:
            s_scr[...] = jnp.zeros(s_scr.shape, F32)
            tail_scr[...] = jnp.zeros(tail_scr.shape, F32)

    cw = cw_ref[...]
    qkv_all = qkv_ref[...].astype(F32)
    n_prev = CONV_W - 1
    if not has_state:
        ext_pad = 128
        sr = lax.broadcasted_iota(jnp.int32, (n_prev * C, ext_pad), 0)
        sc = lax.broadcasted_iota(jnp.int32, (n_prev * C, ext_pad), 1)
        shift_sel = jnp.zeros((n_prev * C, ext_pad), F32)
        for j in range(n_prev):
            in_tap = (sr >= j * C) & (sr < (j + 1) * C)
            shift_sel = jnp.where(in_tap & (sc == sr - j * C + 8 - n_prev + j), 1.0, shift_sel)
        shift_sel = shift_sel.astype(BF16)
    for s in range(nseq):
        x = qkv_all[s * C:(s + 1) * C]
        tail = tail_scr[s * 8:(s + 1) * 8]
        if has_state:
            xe = jnp.concatenate([tail, x], axis=0)
            y = x * cw[n_prev:CONV_W]
            for j in range(n_prev):
                y = y + xe[8 - n_prev + j:8 - n_prev + j + C] * cw[j:j + 1]
        else:
            xe = jnp.concatenate([tail, x, jnp.zeros((ext_pad - 8 - C, QKV_W), F32)], axis=0).astype(BF16)
            moved = jnp.dot(shift_sel, xe, preferred_element_type=F32)
            y = x * cw[n_prev:CONV_W]
            for j in range(n_prev):
                y = y + moved[j * C:(j + 1) * C] * cw[j:j + 1]
        tail_scr[s * 8:(s + 1) * 8] = x[C - 8:C]
        act_scr[s * C:(s + 1) * C] = _silu(y)
        for off, scale in ((0, D_K ** -0.5), (Q_W, 1.0)):
            for kh in range(H_K):
                cols = slice(off + kh * D_K, off + (kh + 1) * D_K)
                v = act_scr[s * C:(s + 1) * C, cols]
                act_scr[s * C:(s + 1) * C, cols] = v * (
                    lax.rsqrt(jnp.sum(v * v, axis=-1, keepdims=True) + NORM_EPS) * scale)

    ri = lax.broadcasted_iota(jnp.int32, (R, R), 0)
    ci = lax.broadcasted_iota(jnp.int32, (R, R), 1)
    same = (ri >> log2c) == (ci >> log2c)
    incl = same & (ri >= ci)
    strict = same & (ri > ci)
    eye = ri == ci
    is_last = ci == (((ri >> log2c) << log2c) + (C - 1))
    eye_f = jnp.where(eye, 1.0, 0.0).astype(F32)
    cum_u = jnp.where(same & (ri <= ci), 1.0, 0.0).astype(BF16)

    nrow = ba_ref.shape[2] // 2
    adv = ad_ref[...]
    dd = functools.partial(jnp.dot, preferred_element_type=F32)
    beta_seq, G_seq = [], []
    for s in range(nseq):
        bav = ba_ref[s, 0]
        beta_seq.append(_sigmoid(bav[0:nrow]))
        g_rows = -jnp.exp(adv[0:nrow]) * _softplus(bav[nrow:2 * nrow] + adv[nrow:2 * nrow])
        g1 = g_rows.astype(BF16)
        r1 = g_rows - g1.astype(F32)
        g2 = r1.astype(BF16)
        g3 = (r1 - g2.astype(F32)).astype(BF16)
        G_seq.append(dd(g1, cum_u) + (dd(g2, cum_u) + dd(g3, cum_u)))

    gam = gam_ref[...]
    rep = H_V // H_K

    def stacked(s, heads, off, per_k_head):
        rows = slice(s * C, (s + 1) * C)
        blocks = [(h // rep if per_k_head else h) for h in heads]
        return jnp.concatenate([act_scr[rows, off + b * D_K:off + (b + 1) * D_K] for b in blocks], axis=0)

    log2b = min(log2c, int(math.log2(SOLVE_BASE)))
    same_base = ((ri ^ ci) >> log2b) == 0
    row_head = lax.broadcasted_iota(jnp.int32, (R, D_V), 0) >> log2c
    z_all = z_ref[...].astype(F32)
    specs = [(s, g) for s in range(nseq) for g in range(ng)]
    for w0 in range(0, len(specs), GDN_WAVE):
        grp = []
        for s, g in specs[w0:w0 + GDN_WAVE]:
            heads = [g * hg + hh for hh in range(hg)]
            kst = stacked(s, heads, Q_W, True)
            grp.append(dict(s=s, g=g, rows=slice(s * C, (s + 1) * C), s0=s * H_V, heads=heads,
                            kk=_dot_nt(kst, kst), qk=_dot_nt(stacked(s, heads, 0, True), kst)))
        for d in grp:
            grow = G_seq[d["s"]][d["g"]:d["g"] + 1]
            brow = beta_seq[d["s"]][d["g"]:d["g"] + 1]
            gcol = jnp.sum(eye_f * grow, axis=1, keepdims=True)
            bcol = jnp.sum(eye_f * brow, axis=1, keepdims=True)
            glast = jnp.sum(jnp.where(is_last, grow, 0.0), axis=1, keepdims=True)
            dm = jnp.where(incl, jnp.exp(jnp.minimum(gcol - grow, 0.0)), 0.0)
            d.update(gcol=gcol, bcol=bcol, glast=glast, eg=jnp.exp(gcol))
            d["m"] = jnp.where(strict, -(d["kk"] * bcol * dm), 0.0)
            d["attn"] = jnp.where(incl, d["qk"] * dm, 0.0)
            d["pw"] = jnp.where(same_base, d["m"], 0.0)
            d["t"] = eye_f + d["pw"]

        for r in range(log2b):
            last = r == log2b - 1
            for d in grp:
                if r == 0:
                    if not last:
                        d["pw"] = _dot(d["pw"], d["pw"])
                elif last:
                    d["t"] = d["t"] + _dot(d["pw"], d["t"])
                else:
                    out = _dot(d["pw"], jnp.concatenate([d["pw"], d["t"]], axis=1))
                    d["pw"] = out[:, :R]
                    d["t"] = d["t"] + out[:, R:]
        for lev in range(log2b + 1, log2c + 1):
            coupling = ((ri ^ ci) >> (lev - 1)) == 1
            for d in grp:
                d["y"] = _dot(d["t"], jnp.where(coupling, d["m"], 0.0))
            for d in grp:
                d["t"] = d["t"] + _dot(d["y"], d["t"])
        for d in grp:
            rhs = jnp.concatenate([stacked(d["s"], d["heads"], 2 * Q_W, False) * d["bcol"],
                                   stacked(d["s"], d["heads"], Q_W, True) * (d["bcol"] * d["eg"])], axis=1)
            d["x"] = _dot(d["t"], rhs)

        for d in grp:
            qe = stacked(d["s"], d["heads"], 0, True) * d["eg"]
            d["xs"] = [_dot(jnp.concatenate([d["x"][hh * C:(hh + 1) * C, D_V:], qe[hh * C:(hh + 1) * C]], axis=0),
                            s_scr[d["s0"] + h]) for hh, h in enumerate(d["heads"])]
        for d in grp:
            d["vnew"] = jnp.concatenate(
                [d["x"][hh * C:(hh + 1) * C, :D_V] - xs[:C] for hh, xs in enumerate(d["xs"])], axis=0)
            qs = jnp.concatenate([xs[C:] for xs in d["xs"]], axis=0)
            d["o"] = qs + _dot(d["attn"], d["vnew"])

        for d in grp:
            kt = stacked(d["s"], d["heads"], Q_W, True) * jnp.exp(d["glast"] - d["gcol"])
            egl = jnp.exp(d["glast"])
            for hh, h in enumerate(d["heads"]):
                sl = slice(hh * C, (hh + 1) * C)
                if C >= 16:
                    upd = _dot_tn(kt[sl], d["vnew"][sl])
                else:
                    upd = _dot_tn(kt, jnp.where(row_head == hh, d["vnew"], 0.0))
                s_scr[d["s0"] + h] = s_scr[d["s0"] + h] * egl[hh * C:hh * C + 1] + upd

        for d in grp:
            o = d["o"]
            zst = jnp.concatenate([z_all[d["rows"], h * D_V:(h + 1) * D_V] for h in d["heads"]], axis=0)
            on = o * lax.rsqrt(jnp.mean(o * o, axis=-1, keepdims=True) + NORM_EPS) * gam * _silu(zst)
            for hh, h in enumerate(d["heads"]):
                o_ref[d["rows"], h * D_V:(h + 1) * D_V] = on[hh * C:(hh + 1) * C].astype(o_ref.dtype)

    @pl.when(c == pl.num_programs(1) - 1)
    def _fin():
        for s in range(nseq):
            sout_ref[s] = s_scr[s * H_V:(s + 1) * H_V]


def _gdn(proj, ba, conv_w, a_log, dt_bias, gamma_a, row0, B, L, C, nseq, conv_buf=None, s0=None):
    has_state = s0 is not None
    hg = STACK_ROWS // C
    ng = H_V // hg
    nrow = 16
    nc = L // C
    R = STACK_ROWS
    assert nseq == 1 or nc == 1
    rows_blk = nseq * C

    def arrange(v):
        v = v.reshape(B, nc, C, ng, hg).transpose(0, 1, 3, 4, 2).reshape(B, nc, ng, R)
        return jnp.pad(v, ((0, 0), (0, 0), (0, nrow - ng), (0, 0)))

    rows = ba[row0:row0 + B * L]
    ba_arr = jnp.concatenate([arrange(rows[:, :H_V]), arrange(rows[:, H_V:2 * H_V])], axis=2)

    def arrange_param(p):
        v = jnp.broadcast_to(p.reshape(ng, hg, 1), (ng, hg, C)).reshape(ng, R)
        return jnp.pad(v, ((0, nrow - ng), (0, 0)))

    ad = jnp.concatenate([arrange_param(a_log), arrange_param(dt_bias)], axis=0)
    rb0 = row0 // rows_blk
    in_specs = [
        pl.BlockSpec((rows_blk, QKV_W), lambda b, c: (rb0 + b * nc + c, 0)),
        pl.BlockSpec((rows_blk, V_W), lambda b, c: (rb0 + b * nc + c, Z_BLK)),
        pl.BlockSpec((nseq, 1, 2 * nrow, R), lambda b, c: (b, c, 0, 0)),
        pl.BlockSpec((CONV_W, QKV_W), lambda b, c: (0, 0)),
        pl.BlockSpec((2 * nrow, R), lambda b, c: (0, 0)),
        pl.BlockSpec((1, D_V), lambda b, c: (0, 0)),
    ]
    args = [proj, proj, ba_arr, conv_w, ad, gamma_a.reshape(1, D_V)]
    if has_state:
        in_specs += [pl.BlockSpec((nseq, CONV_W - 1, QKV_W), lambda b, c: (b, 0, 0)),
                     pl.BlockSpec((nseq, H_V, D_K, D_V), lambda b, c: (b, 0, 0, 0))]
        args += [conv_buf, s0]
    return pl.pallas_call(
        functools.partial(_gdn_kernel, C=C, hg=hg, nseq=nseq, has_state=has_state),
        grid=(B // nseq, nc),
        in_specs=in_specs,
        out_specs=[pl.BlockSpec((rows_blk, V_W), lambda b, c: (b * nc + c, 0)),
                   pl.BlockSpec((nseq, H_V, D_K, D_V), lambda b, c: (b, 0, 0, 0))],
        out_shape=[jax.ShapeDtypeStruct((B * L, V_W), BF16),
                   jax.ShapeDtypeStruct((B, H_V, D_K, D_V), F32)],
        scratch_shapes=[pltpu.VMEM((nseq * H_V, D_K, D_V), F32),
                        pltpu.VMEM((nseq * 8, QKV_W), F32),
                        pltpu.VMEM((rows_blk, QKV_W), F32)],
        compiler_params=_cparams(("parallel", "arbitrary")),
        name="gdn_state" if has_state else "gdn_fresh",
    )(*args)


def _pool_kernel(*refs, tc, nseq, start_pos, has_state):
    if has_state:
        u_ref, buf_ref, o_ref, tail_scr = refs
    else:
        u_ref, o_ref, tail_scr = refs
    hist = POOL_BUF + 1
    c = pl.program_id(1)

    @pl.when(c == 0)
    def _init():
        tail_scr[...] = jnp.zeros(tail_scr.shape, F32)
        if has_state:
            for s in range(nseq):
                tail_scr[s * hist + 1:(s + 1) * hist] = buf_ref[s]

    u_all = u_ref[...].astype(F32)
    pos1 = (start_pos + 1 + c * tc + lax.broadcasted_iota(jnp.int32, (tc, 1), 0)).astype(F32)
    for s in range(nseq):
        u = u_all[s * tc:(s + 1) * tc]
        xe = jnp.concatenate([tail_scr[s * hist:(s + 1) * hist], u], axis=0)
        tail_scr[s * hist:(s + 1) * hist] = xe[tc:tc + hist]
        for gi, win in enumerate(POOL_WINDOWS):
            sl = slice(gi * POOL_GROUP_W, (gi + 1) * POOL_GROUP_W)
            acc = xe[:, sl]
            shift = 1
            while shift < win:
                acc = acc + pltpu.roll(acc, shift, 0)
                shift *= 2
            cnt = jnp.minimum(float(win), pos1)
            o_ref[s * tc:(s + 1) * tc, sl] = acc[hist:hist + tc] / cnt - u[:, sl]


def _pool(proj, row0, B, L, tc, nseq, start_pos, pool_buf=None):
    has_state = pool_buf is not None
    nc = L // tc
    assert nseq == 1 or nc == 1
    rows_blk = nseq * tc
    rb0 = row0 // rows_blk
    in_specs = [pl.BlockSpec((rows_blk, D_MODEL), lambda b, c: (rb0 + b * nc + c, U_BLK))]
    args = [proj]
    if has_state:
        in_specs.append(pl.BlockSpec((nseq, POOL_BUF, D_MODEL), lambda b, c: (b, 0, 0)))
        args.append(pool_buf)
    return pl.pallas_call(
        functools.partial(_pool_kernel, tc=tc, nseq=nseq, start_pos=start_pos, has_state=has_state),
        grid=(B // nseq, nc),
        in_specs=in_specs,
        out_specs=pl.BlockSpec((rows_blk, D_MODEL), lambda b, c: (b * nc + c, 0)),
        out_shape=jax.ShapeDtypeStruct((B * L, D_MODEL), F32),
        scratch_shapes=[pltpu.VMEM((nseq * (POOL_BUF + 1), D_MODEL), F32)],
        compiler_params=_cparams(("parallel", "arbitrary")),
        name="pool_state" if has_state else "pool_fresh",
    )(*args)


def _post_kernel(pa_ref, pb_ref, oa_ref, ob_ref, ga_ref, gb_ref, xa_ref, xb_ref, wp_ref, ps_ref, wbb_ref, wba_ref,
                 wo_ref, g_ref, b_ref, h_ref, hp_ref, *, alpha, tiles_a):
    first = pl.program_id(0) < tiles_a
    x = jnp.where(first, xa_ref[...], xb_ref[...])
    pooled = jnp.where(first, pa_ref[...], pb_ref[...])
    o_gated = jnp.where(first, oa_ref[...], ob_ref[...])
    mixed = jnp.concatenate(
        [_dot(pooled[:, gi * POOL_GROUP_W:(gi + 1) * POOL_GROUP_W], wp_ref[gi]) for gi in range(len(POOL_WINDOWS))],
        axis=1) * ps_ref[...]
    branch_b = _dot(mixed, wbb_ref[...])
    branch_a = _dot(o_gated, wba_ref[...])
    merged = _sigmoid(ga_ref[...].astype(F32)) * branch_a + _sigmoid(gb_ref[...].astype(F32)) * branch_b
    h = _layer_norm(alpha * x + _dot(merged, wo_ref[...]), g_ref[...], b_ref[...])
    h_ref[...] = h
    _store_pieces(hp_ref, h)


def _post(pooled_a, pooled_b, o_a, o_b, proj, x_a, x_b, w_pool, pool_scale, w_br_b, w_br_a, w_out, ln_g, ln_b,
          alpha, tm):
    t = x_a.shape[0] + x_b.shape[0]
    tiles_a = x_a.shape[0] // tm
    row = lambda i: (i, 0)
    row_a = lambda i: (jnp.minimum(i, tiles_a - 1), 0)
    row_b = lambda i: (jnp.maximum(i - tiles_a, 0), 0)
    const2 = lambda i: (0, 0)
    return pl.pallas_call(
        functools.partial(_post_kernel, alpha=alpha, tiles_a=tiles_a),
        grid=(t // tm,),
        in_specs=[pl.BlockSpec((tm, D_MODEL), row_a),
                  pl.BlockSpec((tm, D_MODEL), row_b),
                  pl.BlockSpec((tm, V_W), row_a),
                  pl.BlockSpec((tm, V_W), row_b),
                  pl.BlockSpec((tm, D_MODEL), lambda i: (i, U_BLK + 1)),
                  pl.BlockSpec((tm, D_MODEL), lambda i: (i, U_BLK + 2)),
                  pl.BlockSpec((tm, D_MODEL), row_a),
                  pl.BlockSpec((tm, D_MODEL), row_b),
                  pl.BlockSpec(w_pool.shape, lambda i: (0, 0, 0)),
                  pl.BlockSpec((1, D_MODEL), const2),
                  pl.BlockSpec(w_br_b.shape, const2),
                  pl.BlockSpec(w_br_a.shape, const2),
                  pl.BlockSpec(w_out.shape, const2),
                  pl.BlockSpec((1, D_MODEL), const2),
                  pl.BlockSpec((1, D_MODEL), const2)],
        out_specs=[pl.BlockSpec((tm, D_MODEL), row), pl.BlockSpec((N_PIECE, tm, PIECE_W), lambda i: (0, i, 0))],
        out_shape=[jax.ShapeDtypeStruct((t, D_MODEL), F32),
                   jax.ShapeDtypeStruct((N_PIECE, t, PIECE_W), jnp.uint32)],
        compiler_params=_cparams(("parallel",)),
        name="post_mixers",
    )(pooled_a, pooled_b, o_a, o_b, proj, proj, x_a, x_b, w_pool, pool_scale, w_br_b, w_br_a, w_out, ln_g, ln_b)


def _router_kernel(h_ref, wt_ref, bias_ref, idx_ref, wts_ref, rank_ref, cnt_ref, carry_scr, *, tm):
    i = pl.program_id(0)

    @pl.when(i == 0)
    def _init():
        carry_scr[...] = jnp.zeros(carry_scr.shape, F32)

    logits = _dot3_nt(wt_ref[...], h_ref[...])
    sc = _sigmoid(logits)
    ch = sc + bias_ref[...]
    neg = -jnp.inf
    e_in = lax.broadcasted_iota(jnp.int32, (GROUP_SZ, tm), 0)
    gs_rows = []
    for g in range(N_GROUP):
        blk = ch[g * GROUP_SZ:(g + 1) * GROUP_SZ]
        m1 = jnp.max(blk, axis=0, keepdims=True)
        i1 = jnp.min(jnp.where(blk == m1, e_in, GROUP_SZ), axis=0, keepdims=True)
        m2 = jnp.max(jnp.where(e_in == i1, neg, blk), axis=0, keepdims=True)
        gs_rows.append(m1 + m2)
    gs = jnp.concatenate(gs_rows, axis=0)
    g_io = lax.broadcasted_iota(jnp.int32, (N_GROUP, tm), 0)
    e_io = lax.broadcasted_iota(jnp.int32, (N_EXPERTS, tm), 0)
    e_grp = e_io >> int(math.log2(GROUP_SZ))
    masked = jnp.full((N_EXPERTS, tm), neg, F32)
    for _ in range(TOPK_GROUP):
        mx = jnp.max(gs, axis=0, keepdims=True)
        gi = jnp.min(jnp.where(gs == mx, g_io, N_GROUP), axis=0, keepdims=True)
        gs = jnp.where(g_io == gi, neg, gs)
        masked = jnp.where(e_grp == gi, ch, masked)
    idx_rows, w_rows = [], []
    onehot = jnp.zeros((N_EXPERTS, tm), F32)
    for _ in range(TOP_K):
        mx = jnp.max(masked, axis=0, keepdims=True)
        ei = jnp.min(jnp.where(masked == mx, e_io, N_EXPERTS), axis=0, keepdims=True)
        hit = e_io == ei
        idx_rows.append(ei)
        w_rows.append(jnp.sum(jnp.where(hit, sc, 0.0), axis=0, keepdims=True))
        onehot = jnp.where(hit, 1.0, onehot)
        masked = jnp.where(hit, neg, masked)
    wsel = jnp.concatenate(w_rows, axis=0)
    wts_ref[...] = wsel / jnp.sum(wsel, axis=0, keepdims=True) * ROUTED_SCALE
    idx_ref[...] = jnp.concatenate(idx_rows, axis=0)

    tr = lax.broadcasted_iota(jnp.int32, (tm, tm), 0)
    tc_ = lax.broadcasted_iota(jnp.int32, (tm, tm), 1)
    before = jnp.where(tr < tc_, 1.0, 0.0).astype(BF16)
    cum = jnp.dot(onehot.astype(BF16), before, preferred_element_type=F32) + carry_scr[...]
    rank_ref[...] = jnp.concatenate(
        [jnp.sum(jnp.where(e_io == ei, cum, 0.0), axis=0, keepdims=True) for ei in idx_rows],
        axis=0).astype(jnp.int32)
    carry_scr[...] = carry_scr[...] + jnp.sum(onehot, axis=1, keepdims=True)
    cnt_ref[...] = carry_scr[...]


def _router(h, w_router_t, bias_col, tm):
    t = h.shape[0]
    return pl.pallas_call(
        functools.partial(_router_kernel, tm=tm),
        grid=(t // tm,),
        in_specs=[pl.BlockSpec((tm, D_MODEL), lambda i: (i, 0)),
                  pl.BlockSpec((N_EXPERTS, D_MODEL), lambda i: (0, 0)),
                  pl.BlockSpec((N_EXPERTS, 1), lambda i: (0, 0))],
        out_specs=[pl.BlockSpec((TOP_K, tm), lambda i: (0, i)),
                   pl.BlockSpec((TOP_K, tm), lambda i: (0, i)),
                   pl.BlockSpec((TOP_K, tm), lambda i: (0, i)),
                   pl.BlockSpec((N_EXPERTS, 1), lambda i: (0, 0))],
        out_shape=[jax.ShapeDtypeStruct((TOP_K, t), jnp.int32),
                   jax.ShapeDtypeStruct((TOP_K, t), F32),
                   jax.ShapeDtypeStruct((TOP_K, t), jnp.int32),
                   jax.ShapeDtypeStruct((N_EXPERTS, 1), F32)],
        scratch_shapes=[pltpu.VMEM((N_EXPERTS, 1), F32)],
        compiler_params=_cparams(("arbitrary",)),
        name="router",
    )(h, w_router_t, bias_col)


def _dest_kernel(idx_ref, rank_ref, start_ref, dest_ref, *, tm, n_rows):
    e_io = lax.broadcasted_iota(jnp.int32, (N_EXPERTS, tm), 0)
    starts = start_ref[...]
    rows = []
    for k in range(TOP_K):
        seg = jnp.sum(jnp.where(e_io == idx_ref[k:k + 1, :], starts, 0.0), axis=0, keepdims=True)
        rows.append(seg.astype(jnp.int32) + rank_ref[k:k + 1, :])
    base = jnp.concatenate(rows, axis=0)
    for p in range(N_PIECE):
        dest_ref[p] = base + p * n_rows


def _dest(idx_t, rank_t, seg_start_col, n_rows, tm):
    t = idx_t.shape[1]
    blk = pl.BlockSpec((TOP_K, tm), lambda i: (0, i))
    return pl.pallas_call(
        functools.partial(_dest_kernel, tm=tm, n_rows=n_rows),
        grid=(t // tm,),
        in_specs=[blk, blk, pl.BlockSpec((N_EXPERTS, 1), lambda i: (0, 0))],
        out_specs=pl.BlockSpec((N_PIECE, TOP_K, tm), lambda i: (0, 0, i)),
        out_shape=jax.ShapeDtypeStruct((N_PIECE, TOP_K, t), jnp.int32),
        compiler_params=_cparams(("parallel",)),
        name="dispatch_rows",
    )(idx_t, rank_t, seg_start_col)


def _expert_kernel(be_ref, nu_ref, first_ref, nxt_ref, slot_ref, x_ref, wg_hbm, wu_hbm, wd_hbm, y_ref,
                   wg_buf, wu_buf, wd_buf, wgu_scr, wd_scr, sems):
    i = pl.program_id(0)
    live = i < nu_ref[0]

    def fetch(e, slot):
        return [pltpu.make_async_copy(wg_hbm.at[e], wg_buf.at[slot], sems.at[slot, 0]),
                pltpu.make_async_copy(wu_hbm.at[e], wu_buf.at[slot], sems.at[slot, 1]),
                pltpu.make_async_copy(wd_hbm.at[e], wd_buf.at[slot], sems.at[slot, 2])]

    @pl.when(live & (first_ref[i] == 1))
    def _new_expert():
        slot = slot_ref[i]

        @pl.when(i == 0)
        def _():
            for cp in fetch(be_ref[i], slot):
                cp.start()

        @pl.when(nxt_ref[i] >= 0)
        def _():
            for cp in fetch(nxt_ref[i], 1 - slot):
                cp.start()

        for cp in fetch(be_ref[i], slot):
            cp.wait()
        wgu_scr[:, :D_EXPERT] = wg_buf[slot].astype(BF16)
        wgu_scr[:, D_EXPERT:] = wu_buf[slot].astype(BF16)
        wd_scr[...] = wd_buf[slot].astype(BF16)

    @pl.when(live)
    def _():
        n_sub = 2
        sub = EXPERT_BM // n_sub
        xs = [jnp.concatenate([c.astype(BF16) for c in
                               _load_pieces([x_ref[p, s * sub:(s + 1) * sub] for p in range(N_PIECE)])], axis=1)
              for s in range(n_sub)]
        gus = [jnp.dot(x, wgu_scr[...], preferred_element_type=F32) for x in xs]
        acts = [(_silu(gu[:, :D_EXPERT]) * gu[:, D_EXPERT:]).astype(BF16) for gu in gus]
        ys = [jnp.dot(a, wd_scr[...], preferred_element_type=F32) for a in acts]
        for s, y in enumerate(ys):
            for p in range(N_PIECE):
                y_ref[p, s * sub:(s + 1) * sub] = _pack_halves(y[:, 2 * p * PIECE_W:2 * (p + 1) * PIECE_W])


def _experts(xs, blk_exp, n_used, blk_first, blk_next, blk_slot, w_gate, w_up, w_down):
    n_rows = xs.shape[1]
    n_blocks = n_rows // EXPERT_BM

    def row_block(i, be, nu, *_):
        return (0, jnp.minimum(i, nu[0] - 1), 0)

    grid_spec = pltpu.PrefetchScalarGridSpec(
        num_scalar_prefetch=5,
        grid=(n_blocks,),
        in_specs=[pl.BlockSpec((N_PIECE, EXPERT_BM, PIECE_W), row_block),
                  pl.BlockSpec(memory_space=pl.ANY),
                  pl.BlockSpec(memory_space=pl.ANY),
                  pl.BlockSpec(memory_space=pl.ANY)],
        out_specs=pl.BlockSpec((N_PIECE, EXPERT_BM, PIECE_W), row_block),
        scratch_shapes=[pltpu.VMEM((2, D_MODEL, D_EXPERT), F32),
                        pltpu.VMEM((2, D_MODEL, D_EXPERT), F32),
                        pltpu.VMEM((2, D_EXPERT, D_MODEL), F32),
                        pltpu.VMEM((D_MODEL, 2 * D_EXPERT), BF16),
                        pltpu.VMEM((D_EXPERT, D_MODEL), BF16),
                        pltpu.SemaphoreType.DMA((2, 3))],
    )
    return pl.pallas_call(
        _expert_kernel,
        grid_spec=grid_spec,
        out_shape=jax.ShapeDtypeStruct((N_PIECE, n_rows, PIECE_W), jnp.uint32),
        compiler_params=_cparams(("arbitrary",)),
        name="experts",
    )(blk_exp, n_used, blk_first, blk_next, blk_slot, xs, w_gate, w_up, w_down)


SC_WINDOW = 128
V7X_SC_CORES = 2
V7X_SC_SUBCORES = 16


def _sc_mesh():
    return plsc.VectorSubcoreMesh(core_axis_name="core", subcore_axis_name="subcore",
                                  num_cores=V7X_SC_CORES, num_subcores=V7X_SC_SUBCORES)


def _sc_scatter_rows(src, dest, n_rows, seg, repeat):
    d = src.shape[1]
    n_idx = dest.shape[0]
    seg_blocks = seg // SC_WINDOW
    dest2 = dest.reshape(1, n_idx)

    def src_block(i):
        return ((i // (repeat * seg_blocks)) * seg_blocks + i % seg_blocks, 0)

    @functools.partial(pl.kernel, out_type=jax.ShapeDtypeStruct((n_rows, d), src.dtype), mesh=_sc_mesh(),
                       scratch_types=[], name="sc_dispatch")
    def run(src_hbm, idx_hbm, out_hbm):
        def body(rows_vmem, idx_vmem):
            pltpu.sync_copy(rows_vmem, out_hbm.at[idx_vmem.at[0]])

        pltpu.emit_pipeline(
            body,
            grid=(n_idx // SC_WINDOW,),
            in_specs=[pl.BlockSpec((SC_WINDOW, d), src_block),
                      pl.BlockSpec((1, SC_WINDOW), lambda i: (0, i))],
            out_specs=[],
            core_axis_name=("core", "subcore"),
            dimension_semantics=(pltpu.PARALLEL,),
        )(src_hbm, idx_hbm)

    return run(src, dest2)


def _sc_gather_rows(table, idx):
    d = table.shape[1]
    n_idx = idx.shape[0]
    idx2 = idx.reshape(1, n_idx)

    @functools.partial(pl.kernel, out_type=jax.ShapeDtypeStruct((n_idx, d), table.dtype), mesh=_sc_mesh(),
                       scratch_types=[], name="sc_combine_gather")
    def run(table_hbm, idx_hbm, out_hbm):
        def body(idx_vmem, rows_vmem):
            pltpu.sync_copy(table_hbm.at[idx_vmem.at[0]], rows_vmem)

        pltpu.emit_pipeline(
            body,
            grid=(n_idx // SC_WINDOW,),
            in_specs=[pl.BlockSpec((1, SC_WINDOW), lambda i: (0, i))],
            out_specs=[pl.BlockSpec((SC_WINDOW, d), lambda i: (i, 0))],
            core_axis_name=("core", "subcore"),
            dimension_semantics=(pltpu.PARALLEL,),
        )(idx_hbm, out_hbm)

    return run(table, idx2)


def _combine_kernel(yg_ref, wts_ref, h_ref, wgu_ref, wd_ref, g_ref, b_ref, outa_ref, outb_ref, *, alpha, tiles_a):
    wts = wts_ref[...]
    acc = None
    for k in range(TOP_K):
        cols = _load_pieces([yg_ref[p, k] for p in range(N_PIECE)])
        wk = wts[:, k:k + 1]
        acc = [c * wk for c in cols] if acc is None else [a + c * wk for a, c in zip(acc, cols)]
    routed = jnp.concatenate(acc, axis=1)
    h = h_ref[...]
    gu = _dot(h, wgu_ref[...])
    shared = _dot(_silu(gu[:, :D_EXPERT]) * gu[:, D_EXPERT:], wd_ref[...])
    y = _layer_norm(alpha * h + (routed + shared), g_ref[...], b_ref[...])
    i = pl.program_id(0)

    @pl.when(i < tiles_a)
    def _():
        outa_ref[...] = y

    @pl.when(i >= tiles_a)
    def _():
        outb_ref[...] = y


def _combine(yg, wts, h, w_sh_gu, w_sh_down, ln_g, ln_b, alpha, tm, t_a):
    t = h.shape[0]
    tiles_a = t_a // tm
    row = lambda i: (i, 0)
    const2 = lambda i: (0, 0)
    return pl.pallas_call(
        functools.partial(_combine_kernel, alpha=alpha, tiles_a=tiles_a),
        grid=(t // tm,),
        in_specs=[pl.BlockSpec((N_PIECE, TOP_K, tm, PIECE_W), lambda i: (0, 0, i, 0)),
                  pl.BlockSpec((tm, TOP_K), row),
                  pl.BlockSpec((tm, D_MODEL), row),
                  pl.BlockSpec(w_sh_gu.shape, const2),
                  pl.BlockSpec(w_sh_down.shape, const2),
                  pl.BlockSpec((1, D_MODEL), const2),
                  pl.BlockSpec((1, D_MODEL), const2)],
        out_specs=[pl.BlockSpec((tm, D_MODEL), lambda i: (jnp.minimum(i, tiles_a - 1), 0)),
                   pl.BlockSpec((tm, D_MODEL), lambda i: (jnp.maximum(i - tiles_a, 0), 0))],
        out_shape=[jax.ShapeDtypeStruct((t_a, D_MODEL), F32), jax.ShapeDtypeStruct((t - t_a, D_MODEL), F32)],
        compiler_params=_cparams(("arbitrary",)),
        name="combine_ln2",
    )(yg, wts, h, w_sh_gu, w_sh_down, ln_g, ln_b)


def _layer(xp, xs, s_delta, s_conv, s_pool, w_in, conv_w, a_log, dt_bias, gamma_a, w_br_a, w_pool,
           pool_scale, w_br_b, w_out, ln1_g, ln1_b, w_router, router_bias, w_exp_gate, w_exp_up,
           w_exp_down, w_sh_gate, w_sh_up, w_sh_down, ln2_g, ln2_b, alpha):
    Bp, Lp, _ = xp.shape
    Bs, Ls, _ = xs.shape
    Tp, Ts = Bp * Lp, Bs * Ls
    T = Tp + Ts
    x_p = xp.reshape(Tp, D_MODEL)
    x_s = xs.reshape(Ts, D_MODEL)

    o_z, o_b, o_a, o_u = QKV_W, QKV_W + V_W, QKV_W + V_W + H_V, QKV_W + V_W + 2 * H_V
    w_main = jnp.concatenate([w_in[:, :o_b], w_in[:, o_u:]], axis=1).astype(BF16)
    w_ba = jnp.pad(w_in[:, o_b:o_u], ((0, 0), (0, 128 - 2 * H_V))).astype(BF16)
    proj, ba = _in_proj(x_p, x_s, w_main, w_ba, 1024, 1024)

    o_p, sd_p = _gdn(proj, ba, conv_w, a_log, dt_bias, gamma_a, 0, Bp, Lp, 64, 1)
    o_s, sd_s = _gdn(proj, ba, conv_w, a_log, dt_bias, gamma_a, Tp, Bs, Ls, Ls, GDN_STATE_SEQS,
                     conv_buf=s_conv, s0=s_delta)

    pooled_p = _pool(proj, 0, Bp, Lp, 256, 1, 0)
    pooled_s = _pool(proj, Tp, Bs, Ls, Ls, POOL_STATE_SEQS, PAST_LEN, pool_buf=s_pool)

    h, hp = _post(pooled_p, pooled_s, o_p, o_s, proj, x_p, x_s, w_pool.astype(BF16), pool_scale.reshape(1, D_MODEL),
                  w_br_b.astype(BF16), w_br_a.astype(BF16), w_out.astype(BF16),
                  ln1_g.reshape(1, D_MODEL), ln1_b.reshape(1, D_MODEL), alpha, 256)

    idx_t, wts_t, rank_t, cnt = _router(h, w_router.T, router_bias.reshape(N_EXPERTS, 1), 512)
    counts = cnt[:, 0].astype(jnp.int32)
    padded = ((counts + EXPERT_BM - 1) // EXPERT_BM) * EXPERT_BM
    pends = jnp.cumsum(padded)
    pstarts = pends - padded
    n_blocks = (T * TOP_K + N_EXPERTS * (EXPERT_BM - 1) + EXPERT_BM - 1) // EXPERT_BM
    n_rows = n_blocks * EXPERT_BM
    piece_rows = _dest(idx_t, rank_t, pstarts.astype(F32).reshape(N_EXPERTS, 1), n_rows, 512)
    blk_start = jnp.arange(n_blocks, dtype=jnp.int32) * EXPERT_BM
    blk_exp = jnp.minimum(jnp.sum((pends[None, :] <= blk_start[:, None]).astype(jnp.int32), axis=1),
                          N_EXPERTS - 1)
    n_used = (pends[-1:] // EXPERT_BM).astype(jnp.int32)
    has_rows = counts > 0
    e_ids = jnp.arange(N_EXPERTS, dtype=jnp.int32)
    slot_of = (jnp.cumsum(has_rows.astype(jnp.int32)) - 1) & 1
    later = jnp.where(has_rows, e_ids, N_EXPERTS)
    next_of = jnp.concatenate([lax.cummin(later, reverse=True)[1:], jnp.full((1,), N_EXPERTS, jnp.int32)])
    next_of = jnp.where(next_of < N_EXPERTS, next_of, -1)
    blk_first = (blk_start == pstarts[blk_exp]).astype(jnp.int32)
    blk_next = next_of[blk_exp]
    blk_slot = slot_of[blk_exp]

    piece_idx = piece_rows.reshape(N_PIECE * TOP_K * T)
    x_sorted = _sc_scatter_rows(hp.reshape(N_PIECE * T, PIECE_W), piece_idx, N_PIECE * n_rows, T, TOP_K)
    y_sorted = _experts(x_sorted.reshape(N_PIECE, n_rows, PIECE_W), blk_exp, n_used, blk_first, blk_next,
                        blk_slot, w_exp_gate, w_exp_up, w_exp_down)
    yg = _sc_gather_rows(y_sorted.reshape(N_PIECE * n_rows, PIECE_W), piece_idx)
    yg = yg.reshape(N_PIECE, TOP_K, T, PIECE_W)

    w_sh_gu = jnp.concatenate([w_sh_gate, w_sh_up], axis=1).astype(BF16)
    y_p, y_s = _combine(yg, wts_t.T, h, w_sh_gu, w_sh_down.astype(BF16),
                        ln2_g.reshape(1, D_MODEL), ln2_b.reshape(1, D_MODEL), alpha, 256, Tp)

    def carried_rows(keep, col0, width, name, old_p, old_s):
        parts, counts = [], []
        for x2d, B, L in ((x_p, Bp, Lp), (x_s, Bs, Ls)):
            n = min(L, keep)
            rows = jnp.arange(B, dtype=jnp.int32)[:, None] * L + (L - n) + jnp.arange(n, dtype=jnp.int32)
            parts.append(jnp.take(x2d, rows.reshape(-1), axis=0).astype(BF16))
            counts.append((B, n))
        n_tail = sum(B * n for B, n in counts)
        n_pad = -(-n_tail // 128) * 128
        xt = jnp.pad(jnp.concatenate(parts, axis=0), ((0, n_pad - n_tail), (0, 0)))
        tail = _matmul(xt, w_main, col0, width, n_pad, 1024, F32, name)
        outs, r0 = [], 0
        for (B, n), old in zip(counts, (old_p, old_s)):
            new = tail[r0:r0 + B * n].reshape(B, n, width)
            outs.append(new if n == keep else jnp.concatenate([old[:, n:], new], axis=1))
            r0 += B * n
        return outs

    new_conv_p, new_conv_s = carried_rows(CONV_W - 1, 0, QKV_W, "in_proj_conv_rows",
                                          jnp.zeros((Bp, CONV_W - 1, QKV_W), F32), s_conv)
    new_pool_p, new_pool_s = carried_rows(POOL_BUF, U_BLK * D_MODEL, D_MODEL, "in_proj_pool_rows",
                                          jnp.zeros((Bp, POOL_BUF, D_MODEL), F32), s_pool)
    return (y_p.reshape(Bp, Lp, D_MODEL), y_s.reshape(Bs, Ls, D_MODEL),
            sd_p, new_conv_p, new_pool_p, sd_s, new_conv_s, new_pool_s)


def kernel(x_prompt, x_sample, state_delta, state_conv, state_pool, w_in, conv_w, a_log, dt_bias, gamma_a,
           w_br_a, w_pool, pool_scale, w_br_b, w_out, ln1_g, ln1_b, w_router, router_bias,
           w_exp_gate, w_exp_up, w_exp_down, w_sh_gate, w_sh_up, w_sh_down, ln2_g, ln2_b):
    depth = w_in.shape[0]
    alpha = (2 * depth) ** 0.25
    yp, ys = x_prompt, x_sample
    outs = [[] for _ in range(6)]
    for l in range(depth):
        res = _layer(yp, ys, state_delta[l], state_conv[l], state_pool[l], w_in[l], conv_w[l], a_log[l],
                     dt_bias[l], gamma_a[l], w_br_a[l], w_pool[l], pool_scale[l], w_br_b[l], w_out[l],
                     ln1_g[l], ln1_b[l], w_router[l], router_bias[l], w_exp_gate[l], w_exp_up[l],
                     w_exp_down[l], w_sh_gate[l], w_sh_up[l], w_sh_down[l], ln2_g[l], ln2_b[l], alpha)
        yp, ys = res[0], res[1]
        for lst, v in zip(outs, res[2:]):
            lst.append(v)
    return (yp, ys) + tuple(jnp.stack(v) for v in outs)
```

```python
import functools
import math

import jax
import jax.numpy as jnp
from jax import lax
from jax.experimental import pallas as pl
from jax.experimental.pallas import tpu as pltpu
from jax.experimental.pallas import tpu_sc as plsc

F32 = jnp.float32
BF16 = jnp.bfloat16

D_MODEL = 1024
H_K = 8
D_K = 128
H_V = 16
D_V = 128
Q_W = H_K * D_K
V_W = H_V * D_V
QKV_W = 2 * Q_W + V_W
CONV_W = 4
POOL_WINDOWS = (2, 4, 8, 16)
POOL_GROUP_W = D_MODEL // len(POOL_WINDOWS)
POOL_BUF = max(POOL_WINDOWS) - 1
N_EXPERTS = 256
TOP_K = 8
N_GROUP = 8
TOPK_GROUP = 4
GROUP_SZ = N_EXPERTS // N_GROUP
D_EXPERT = D_MODEL // 4
ROUTED_SCALE = 2.5
LN_EPS = 1e-5
NORM_EPS = 1e-6
PAST_LEN = 16384

MAIN_W = QKV_W + V_W + 3 * D_MODEL
Z_BLK = QKV_W // V_W
U_BLK = (QKV_W + V_W) // D_MODEL

STACK_ROWS = 128
SOLVE_BASE = 16
GDN_WAVE = 8
EXPERT_BM = 512
GDN_STATE_SEQS = 4
POOL_STATE_SEQS = 16
VMEM_LIMIT = 56 * 1024 * 1024


def _cparams(sem):
    return pltpu.CompilerParams(dimension_semantics=sem, vmem_limit_bytes=VMEM_LIMIT)


def _sigmoid(x):
    return 0.5 * jnp.tanh(0.5 * x) + 0.5


def _silu(x):
    return x * _sigmoid(x)


def _softplus(x):
    return jnp.maximum(x, 0.0) + jnp.log(1.0 + jnp.exp(-jnp.abs(x)))


def _dot(a, b):
    return jnp.dot(a.astype(BF16), b.astype(BF16), preferred_element_type=F32)


def _dot_nt(a, b):
    return lax.dot_general(a.astype(BF16), b.astype(BF16), (((1,), (1,)), ((), ())),
                           preferred_element_type=F32)


def _dot_tn(a, b):
    return lax.dot_general(a.astype(BF16), b.astype(BF16), (((0,), (0,)), ((), ())),
                           preferred_element_type=F32)


def _split(a):
    hi = a.astype(BF16)
    lo = (a - hi.astype(F32)).astype(BF16)
    return hi, lo


def _dot3_nt(a, b):
    ah, al = _split(a)
    bh, bl = _split(b)
    d = functools.partial(lax.dot_general, dimension_numbers=(((1,), (1,)), ((), ())),
                          preferred_element_type=F32)
    return d(ah, bh) + (d(ah, bl) + d(al, bh))


def _pack_halves(x):
    n = x.shape[1] // 2
    hi = lax.bitcast_convert_type(x[:, :n].astype(BF16).astype(F32), jnp.uint32)
    lo = lax.bitcast_convert_type(x[:, n:].astype(BF16).astype(F32), jnp.uint32)
    return (hi & jnp.uint32(0xFFFF0000)) | (lo >> 16)


def _unpack_halves(w):
    hi = lax.bitcast_convert_type(w & jnp.uint32(0xFFFF0000), F32)
    lo = lax.bitcast_convert_type(w << 16, F32)
    return hi, lo


N_PIECE = 2
PIECE_W = D_MODEL // (2 * N_PIECE)


def _store_pieces(ref, x):
    for p in range(N_PIECE):
        ref[p] = _pack_halves(x[:, 2 * p * PIECE_W:2 * (p + 1) * PIECE_W])


def _load_pieces(pieces):
    cols = []
    for w in pieces:
        cols.extend(_unpack_halves(w))
    return cols


def _layer_norm(x, g, b):
    mu = jnp.mean(x, axis=-1, keepdims=True)
    xc = x - mu
    var = jnp.mean(xc * xc, axis=-1, keepdims=True)
    return xc * lax.rsqrt(var + LN_EPS) * g + b


def _in_proj_kernel(xa_ref, xb_ref, w_ref, wba_ref, o_ref, ba_ref, x_scr, *, tiles_a):
    @pl.when(pl.program_id(1) == 0)
    def _():
        x = jnp.where(pl.program_id(0) < tiles_a, xa_ref[...], xb_ref[...]).astype(BF16)
        x_scr[...] = x
        ba_ref[...] = jnp.dot(x, wba_ref[...], preferred_element_type=F32)

    o_ref[...] = jnp.dot(x_scr[...], w_ref[...], preferred_element_type=F32).astype(o_ref.dtype)


def _in_proj(x_a, x_b, w_main, w_ba, tm, tn):
    k = x_a.shape[1]
    t = x_a.shape[0] + x_b.shape[0]
    n = w_main.shape[1]
    tiles_a = x_a.shape[0] // tm
    return pl.pallas_call(
        functools.partial(_in_proj_kernel, tiles_a=tiles_a),
        grid=(t // tm, n // tn),
        in_specs=[pl.BlockSpec((tm, k), lambda i, j: (jnp.minimum(i, tiles_a - 1), 0)),
                  pl.BlockSpec((tm, k), lambda i, j: (jnp.maximum(i - tiles_a, 0), 0)),
                  pl.BlockSpec((k, tn), lambda i, j: (0, j)),
                  pl.BlockSpec(w_ba.shape, lambda i, j: (0, 0))],
        out_specs=[pl.BlockSpec((tm, tn), lambda i, j: (i, j)),
                   pl.BlockSpec((tm, w_ba.shape[1]), lambda i, j: (i, 0))],
        out_shape=[jax.ShapeDtypeStruct((t, n), BF16), jax.ShapeDtypeStruct((t, w_ba.shape[1]), F32)],
        scratch_shapes=[pltpu.VMEM((tm, k), BF16)],
        compiler_params=_cparams(("parallel", "arbitrary")),
        name="in_proj",
    )(x_a, x_b, w_main, w_ba)


def _mm_kernel(x_ref, w_ref, o_ref):
    o_ref[...] = jnp.dot(x_ref[...], w_ref[...], preferred_element_type=F32).astype(o_ref.dtype)


def _matmul(x, w, col0, n, tm, tn, out_dtype, name):
    t, k = x.shape
    cb0 = col0 // tn
    return pl.pallas_call(
        _mm_kernel,
        grid=(t // tm, n // tn),
        in_specs=[pl.BlockSpec((tm, k), lambda i, j: (i, 0)),
                  pl.BlockSpec((k, tn), lambda i, j: (0, cb0 + j))],
        out_specs=pl.BlockSpec((tm, tn), lambda i, j: (i, j)),
        out_shape=jax.ShapeDtypeStruct((t, n), out_dtype),
        compiler_params=_cparams(("parallel", "parallel")),
        name=name,
    )(x, w)


def _gdn_kernel(*refs, C, hg, nseq, has_state):
    if has_state:
        (qkv_ref, z_ref, ba_ref, cw_ref, ad_ref, gam_ref, buf_ref, s0_ref,
         o_ref, sout_ref, s_scr, tail_scr, act_scr) = refs
    else:
        (qkv_ref, z_ref, ba_ref, cw_ref, ad_ref, gam_ref,
         o_ref, sout_ref, s_scr, tail_scr, act_scr) = refs
    R = hg * C
    ng = H_V // hg
    log2c = int(math.log2(C))
    c = pl.program_id(1)

    @pl.when(c == 0)
    def _init():
        if has_state:
            tail_scr[...] = jnp.zeros(tail_scr.shape, F32)
            for s in range(nseq):
                s_scr[s * H_V:(s + 1) * H_V] = s0_ref[s]
                tail_scr[(s + 1) * 8 - (CONV_W - 1):(s + 1) * 8] = buf_ref[s]
        else:
            s_scr[...] = jnp.zeros(s_scr.shape, F32)
            tail_scr[...] = jnp.zeros(tail_scr.shape, F32)

    cw = cw_ref[...]
    qkv_all = qkv_ref[...].astype(F32)
    n_prev = CONV_W - 1
    if not has_state:
        ext_pad = 128
        sr = lax.broadcasted_iota(jnp.int32, (n_prev * C, ext_pad), 0)
        sc = lax.broadcasted_iota(jnp.int32, (n_prev * C, ext_pad), 1)
        shift_sel = jnp.zeros((n_prev * C, ext_pad), F32)
        for j in range(n_prev):
            in_tap = (sr >= j * C) & (sr < (j + 1) * C)
            shift_sel = jnp.where(in_tap & (sc == sr - j * C + 8 - n_prev + j), 1.0, shift_sel)
        shift_sel = shift_sel.astype(BF16)
    for s in range(nseq):
        x = qkv_all[s * C:(s + 1) * C]
        tail = tail_scr[s * 8:(s + 1) * 8]
        if has_state:
            xe = jnp.concatenate([tail, x], axis=0)
            y = x * cw[n_prev:CONV_W]
            for j in range(n_prev):
                y = y + xe[8 - n_prev + j:8 - n_prev + j + C] * cw[j:j + 1]
        else:
            xe = jnp.concatenate([tail, x, jnp.zeros((ext_pad - 8 - C, QKV_W), F32)], axis=0).astype(BF16)
            moved = jnp.dot(shift_sel, xe, preferred_element_type=F32)
            y = x * cw[n_prev:CONV_W]
            for j in range(n_prev):
                y = y + moved[j * C:(j + 1) * C] * cw[j:j + 1]
        tail_scr[s * 8:(s + 1) * 8] = x[C - 8:C]
        act_scr[s * C:(s + 1) * C] = _silu(y)
        for off, scale in ((0, D_K ** -0.5), (Q_W, 1.0)):
            for kh in range(H_K):
                cols = slice(off + kh * D_K, off + (kh + 1) * D_K)
                v = act_scr[s * C:(s + 1) * C, cols]
                act_scr[s * C:(s + 1) * C, cols] = v * (
                    lax.rsqrt(jnp.sum(v * v, axis=-1, keepdims=True) + NORM_EPS) * scale)

    ri = lax.broadcasted_iota(jnp.int32, (R, R), 0)
    ci = lax.broadcasted_iota(jnp.int32, (R, R), 1)
    same = (ri >> log2c) == (ci >> log2c)
    incl = same & (ri >= ci)
    strict = same & (ri > ci)
    eye = ri == ci
    is_last = ci == (((ri >> log2c) << log2c) + (C - 1))
    eye_f = jnp.where(eye, 1.0, 0.0).astype(F32)
    cum_u = jnp.where(same & (ri <= ci), 1.0, 0.0).astype(BF16)

    nrow = ba_ref.shape[2] // 2
    adv = ad_ref[...]
    dd = functools.partial(jnp.dot, preferred_element_type=F32)
    beta_seq, G_seq = [], []
    for s in range(nseq):
        bav = ba_ref[s, 0]
        beta_seq.append(_sigmoid(bav[0:nrow]))
        g_rows = -jnp.exp(adv[0:nrow]) * _softplus(bav[nrow:2 * nrow] + adv[nrow:2 * nrow])
        g1 = g_rows.astype(BF16)
        r1 = g_rows - g1.astype(F32)
        g2 = r1.astype(BF16)
        g3 = (r1 - g2.astype(F32)).astype(BF16)
        G_seq.append(dd(g1, cum_u) + (dd(g2, cum_u) + dd(g3, cum_u)))

    gam = gam_ref[...]
    rep = H_V // H_K

    def stacked(s, heads, off, per_k_head):
        rows = slice(s * C, (s + 1) * C)
        blocks = [(h // rep if per_k_head else h) for h in heads]
        return jnp.concatenate([act_scr[rows, off + b * D_K:off + (b + 1) * D_K] for b in blocks], axis=0)

    log2b = min(log2c, int(math.log2(SOLVE_BASE)))
    same_base = ((ri ^ ci) >> log2b) == 0
    row_head = lax.broadcasted_iota(jnp.int32, (R, D_V), 0) >> log2c
    z_all = z_ref[...].astype(F32)
    specs = [(s, g) for s in range(nseq) for g in range(ng)]
    for w0 in range(0, len(specs), GDN_WAVE):
        grp = []
        for s, g in specs[w0:w0 + GDN_WAVE]:
            heads = [g * hg + hh for hh in range(hg)]
            kst = stacked(s, heads, Q_W, True)
            grp.append(dict(s=s, g=g, rows=slice(s * C, (s + 1) * C), s0=s * H_V, heads=heads,
                            kk=_dot_nt(kst, kst), qk=_dot_nt(stacked(s, heads, 0, True), kst)))
        for d in grp:
            grow = G_seq[d["s"]][d["g"]:d["g"] + 1]
            brow = beta_seq[d["s"]][d["g"]:d["g"] + 1]
            gcol = jnp.sum(eye_f * grow, axis=1, keepdims=True)
            bcol = jnp.sum(eye_f * brow, axis=1, keepdims=True)
            glast = jnp.sum(jnp.where(is_last, grow, 0.0), axis=1, keepdims=True)
            dm = jnp.where(incl, jnp.exp(jnp.minimum(gcol - grow, 0.0)), 0.0)
            d.update(gcol=gcol, bcol=bcol, glast=glast, eg=jnp.exp(gcol))
            d["m"] = jnp.where(strict, -(d["kk"] * bcol * dm), 0.0)
            d["attn"] = jnp.where(incl, d["qk"] * dm, 0.0)
            d["pw"] = jnp.where(same_base, d["m"], 0.0)
            d["t"] = eye_f + d["pw"]

        for r in range(log2b):
            last = r == log2b - 1
            for d in grp:
                if r == 0:
                    if not last:
                        d["pw"] = _dot(d["pw"], d["pw"])
                elif last:
                    d["t"] = d["t"] + _dot(d["pw"], d["t"])
                else:
                    out = _dot(d["pw"], jnp.concatenate([d["pw"], d["t"]], axis=1))
                    d["pw"] = out[:, :R]
                    d["t"] = d["t"] + out[:, R:]
        for lev in range(log2b + 1, log2c + 1):
            coupling = ((ri ^ ci) >> (lev - 1)) == 1
            for d in grp:
                d["y"] = _dot(d["t"], jnp.where(coupling, d["m"], 0.0))
            for d in grp:
                d["t"] = d["t"] + _dot(d["y"], d["t"])
        for d in grp:
            rhs = jnp.concatenate([stacked(d["s"], d["heads"], 2 * Q_W, False) * d["bcol"],
                                   stacked(d["s"], d["heads"], Q_W, True) * (d["bcol"] * d["eg"])], axis=1)
            d["x"] = _dot(d["t"], rhs)

        for d in grp:
            qe = stacked(d["s"], d["heads"], 0, True) * d["eg"]
            d["xs"] = [_dot(jnp.concatenate([d["x"][hh * C:(hh + 1) * C, D_V:], qe[hh * C:(hh + 1) * C]], axis=0),
                            s_scr[d["s0"] + h]) for hh, h in enumerate(d["heads"])]
        for d in grp:
            d["vnew"] = jnp.concatenate(
                [d["x"][hh * C:(hh + 1) * C, :D_V] - xs[:C] for hh, xs in enumerate(d["xs"])], axis=0)
            qs = jnp.concatenate([xs[C:] for xs in d["xs"]], axis=0)
            d["o"] = qs + _dot(d["attn"], d["vnew"])

        for d in grp:
            kt = stacked(d["s"], d["heads"], Q_W, True) * jnp.exp(d["glast"] - d["gcol"])
            egl = jnp.exp(d["glast"])
            for hh, h in enumerate(d["heads"]):
                sl = slice(hh * C, (hh + 1) * C)
                if C >= 16:
                    upd = _dot_tn(kt[sl], d["vnew"][sl])
                else:
                    upd = _dot_tn(kt, jnp.where(row_head == hh, d["vnew"], 0.0))
                s_scr[d["s0"] + h] = s_scr[d["s0"] + h] * egl[hh * C:hh * C + 1] + upd

        for d in grp:
            o = d["o"]
            zst = jnp.concatenate([z_all[d["rows"], h * D_V:(h + 1) * D_V] for h in d["heads"]], axis=0)
            on = o * lax.rsqrt(jnp.mean(o * o, axis=-1, keepdims=True) + NORM_EPS) * gam * _silu(zst)
            for hh, h in enumerate(d["heads"]):
                o_ref[d["rows"], h * D_V:(h + 1) * D_V] = on[hh * C:(hh + 1) * C].astype(o_ref.dtype)

    @pl.when(c == pl.num_programs(1) - 1)
    def _fin():
        for s in range(nseq):
            sout_ref[s] = s_scr[s * H_V:(s + 1) * H_V]


def _gdn(proj, ba, conv_w, a_log, dt_bias, gamma_a, row0, B, L, C, nseq, conv_buf=None, s0=None):
    has_state = s0 is not None
    hg = STACK_ROWS // C
    ng = H_V // hg
    nrow = 16
    nc = L // C
    R = STACK_ROWS
    assert nseq == 1 or nc == 1
    rows_blk = nseq * C

    def arrange(v):
        v = v.reshape(B, nc, C, ng, hg).transpose(0, 1, 3, 4, 2).reshape(B, nc, ng, R)
        return jnp.pad(v, ((0, 0), (0, 0), (0, nrow - ng), (0, 0)))

    rows = ba[row0:row0 + B * L]
    ba_arr = jnp.concatenate([arrange(rows[:, :H_V]), arrange(rows[:, H_V:2 * H_V])], axis=2)

    def arrange_param(p):
        v = jnp.broadcast_to(p.reshape(ng, hg, 1), (ng, hg, C)).reshape(ng, R)
        return jnp.pad(v, ((0, nrow - ng), (0, 0)))

    ad = jnp.concatenate([arrange_param(a_log), arrange_param(dt_bias)], axis=0)
    rb0 = row0 // rows_blk
    in_specs = [
        pl.BlockSpec((rows_blk, QKV_W), lambda b, c: (rb0 + b * nc + c, 0)),
        pl.BlockSpec((rows_blk, V_W), lambda b, c: (rb0 + b * nc + c, Z_BLK)),
        pl.BlockSpec((nseq, 1, 2 * nrow, R), lambda b, c: (b, c, 0, 0)),
        pl.BlockSpec((CONV_W, QKV_W), lambda b, c: (0, 0)),
        pl.BlockSpec((2 * nrow, R), lambda b, c: (0, 0)),
        pl.BlockSpec((1, D_V), lambda b, c: (0, 0)),
    ]
    args = [proj, proj, ba_arr, conv_w, ad, gamma_a.reshape(1, D_V)]
    if has_state:
        in_specs += [pl.BlockSpec((nseq, CONV_W - 1, QKV_W), lambda b, c: (b, 0, 0)),
                     pl.BlockSpec((nseq, H_V, D_K, D_V), lambda b, c: (b, 0, 0, 0))]
        args += [conv_buf, s0]
    return pl.pallas_call(
        functools.partial(_gdn_kernel, C=C, hg=hg, nseq=nseq, has_state=has_state),
        grid=(B // nseq, nc),
        in_specs=in_specs,
        out_specs=[pl.BlockSpec((rows_blk, V_W), lambda b, c: (b * nc + c, 0)),
                   pl.BlockSpec((nseq, H_V, D_K, D_V), lambda b, c: (b, 0, 0, 0))],
        out_shape=[jax.ShapeDtypeStruct((B * L, V_W), BF16),
                   jax.ShapeDtypeStruct((B, H_V, D_K, D_V), F32)],
        scratch_shapes=[pltpu.VMEM((nseq * H_V, D_K, D_V), F32),
                        pltpu.VMEM((nseq * 8, QKV_W), F32),
                        pltpu.VMEM((rows_blk, QKV_W), F32)],
        compiler_params=_cparams(("parallel", "arbitrary")),
        name="gdn_state" if has_state else "gdn_fresh",
    )(*args)


def _pool_kernel(*refs, tc, nseq, start_pos, has_state):
    if has_state:
        u_ref, buf_ref, o_ref, tail_scr = refs
    else:
        u_ref, o_ref, tail_scr = refs
    hist = POOL_BUF + 1
    c = pl.program_id(1)

    @pl.when(c == 0)
    def _init():
        tail_scr[...] = jnp.zeros(tail_scr.shape, F32)
        if has_state:
            for s in range(nseq):
                tail_scr[s * hist + 1:(s + 1) * hist] = buf_ref[s]

    u_all = u_ref[...].astype(F32)
    pos1 = (start_pos + 1 + c * tc + lax.broadcasted_iota(jnp.int32, (tc, 1), 0)).astype(F32)
    for s in range(nseq):
        u = u_all[s * tc:(s + 1) * tc]
        xe = jnp.concatenate([tail_scr[s * hist:(s + 1) * hist], u], axis=0)
        tail_scr[s * hist:(s + 1) * hist] = xe[tc:tc + hist]
        for gi, win in enumerate(POOL_WINDOWS):
            sl = slice(gi * POOL_GROUP_W, (gi + 1) * POOL_GROUP_W)
            acc = xe[:, sl]
            shift = 1
            while shift < win:
                acc = acc + pltpu.roll(acc, shift, 0)
                shift *= 2
            cnt = jnp.minimum(float(win), pos1)
            o_ref[s * tc:(s + 1) * tc, sl] = acc[hist:hist + tc] / cnt - u[:, sl]


def _pool(proj, row0, B, L, tc, nseq, start_pos, pool_buf=None):
    has_state = pool_buf is not None
    nc = L // tc
    assert nseq == 1 or nc == 1
    rows_blk = nseq * tc
    rb0 = row0 // rows_blk
    in_specs = [pl.BlockSpec((rows_blk, D_MODEL), lambda b, c: (rb0 + b * nc + c, U_BLK))]
    args = [proj]
    if has_state:
        in_specs.append(pl.BlockSpec((nseq, POOL_BUF, D_MODEL), lambda b, c: (b, 0, 0)))
        args.append(pool_buf)
    return pl.pallas_call(
        functools.partial(_pool_kernel, tc=tc, nseq=nseq, start_pos=start_pos, has_state=has_state),
        grid=(B // nseq, nc),
        in_specs=in_specs,
        out_specs=pl.BlockSpec((rows_blk, D_MODEL), lambda b, c: (b * nc + c, 0)),
        out_shape=jax.ShapeDtypeStruct((B * L, D_MODEL), F32),
        scratch_shapes=[pltpu.VMEM((nseq * (POOL_BUF + 1), D_MODEL), F32)],
        compiler_params=_cparams(("parallel", "arbitrary")),
        name="pool_state" if has_state else "pool_fresh",
    )(*args)


def _post_kernel(pa_ref, pb_ref, oa_ref, ob_ref, ga_ref, gb_ref, xa_ref, xb_ref, wp_ref, ps_ref, wbb_ref, wba_ref,
                 wo_ref, g_ref, b_ref, h_ref, hp_ref, *, alpha, tiles_a):
    first = pl.program_id(0) < tiles_a
    x = jnp.where(first, xa_ref[...], xb_ref[...])
    pooled = jnp.where(first, pa_ref[...], pb_ref[...])
    o_gated = jnp.where(first, oa_ref[...], ob_ref[...])
    mixed = jnp.concatenate(
        [_dot(pooled[:, gi * POOL_GROUP_W:(gi + 1) * POOL_GROUP_W], wp_ref[gi]) for gi in range(len(POOL_WINDOWS))],
        axis=1) * ps_ref[...]
    branch_b = _dot(mixed, wbb_ref[...])
    branch_a = _dot(o_gated, wba_ref[...])
    merged = _sigmoid(ga_ref[...].astype(F32)) * branch_a + _sigmoid(gb_ref[...].astype(F32)) * branch_b
    h = _layer_norm(alpha * x + _dot(merged, wo_ref[...]), g_ref[...], b_ref[...])
    h_ref[...] = h
    _store_pieces(hp_ref, h)


def _post(pooled_a, pooled_b, o_a, o_b, proj, x_a, x_b, w_pool, pool_scale, w_br_b, w_br_a, w_out, ln_g, ln_b,
          alpha, tm):
    t = x_a.shape[0] + x_b.shape[0]
    tiles_a = x_a.shape[0] // tm
    row = lambda i: (i, 0)
    row_a = lambda i: (jnp.minimum(i, tiles_a - 1), 0)
    row_b = lambda i: (jnp.maximum(i - tiles_a, 0), 0)
    const2 = lambda i: (0, 0)
    return pl.pallas_call(
        functools.partial(_post_kernel, alpha=alpha, tiles_a=tiles_a),
        grid=(t // tm,),
        in_specs=[pl.BlockSpec((tm, D_MODEL), row_a),
                  pl.BlockSpec((tm, D_MODEL), row_b),
                  pl.BlockSpec((tm, V_W), row_a),
                  pl.BlockSpec((tm, V_W), row_b),
                  pl.BlockSpec((tm, D_MODEL), lambda i: (i, U_BLK + 1)),
                  pl.BlockSpec((tm, D_MODEL), lambda i: (i, U_BLK + 2)),
                  pl.BlockSpec((tm, D_MODEL), row_a),
                  pl.BlockSpec((tm, D_MODEL), row_b),
                  pl.BlockSpec(w_pool.shape, lambda i: (0, 0, 0)),
                  pl.BlockSpec((1, D_MODEL), const2),
                  pl.BlockSpec(w_br_b.shape, const2),
                  pl.BlockSpec(w_br_a.shape, const2),
                  pl.BlockSpec(w_out.shape, const2),
                  pl.BlockSpec((1, D_MODEL), const2),
                  pl.BlockSpec((1, D_MODEL), const2)],
        out_specs=[pl.BlockSpec((tm, D_MODEL), row), pl.BlockSpec((N_PIECE, tm, PIECE_W), lambda i: (0, i, 0))],
        out_shape=[jax.ShapeDtypeStruct((t, D_MODEL), F32),
                   jax.ShapeDtypeStruct((N_PIECE, t, PIECE_W), jnp.uint32)],
        compiler_params=_cparams(("parallel",)),
        name="post_mixers",
    )(pooled_a, pooled_b, o_a, o_b, proj, proj, x_a, x_b, w_pool, pool_scale, w_br_b, w_br_a, w_out, ln_g, ln_b)


def _router_kernel(h_ref, wt_ref, bias_ref, idx_ref, wts_ref, rank_ref, cnt_ref, carry_scr, *, tm):
    i = pl.program_id(0)

    @pl.when(i == 0)
    def _init():
        carry_scr[...] = jnp.zeros(carry_scr.shape, F32)

    logits = _dot3_nt(wt_ref[...], h_ref[...])
    sc = _sigmoid(logits)
    ch = sc + bias_ref[...]
    neg = -jnp.inf
    e_in = lax.broadcasted_iota(jnp.int32, (GROUP_SZ, tm), 0)
    gs_rows = []
    for g in range(N_GROUP):
        blk = ch[g * GROUP_SZ:(g + 1) * GROUP_SZ]
        m1 = jnp.max(blk, axis=0, keepdims=True)
        i1 = jnp.min(jnp.where(blk == m1, e_in, GROUP_SZ), axis=0, keepdims=True)
        m2 = jnp.max(jnp.where(e_in == i1, neg, blk), axis=0, keepdims=True)
        gs_rows.append(m1 + m2)
    gs = jnp.concatenate(gs_rows, axis=0)
    g_io = lax.broadcasted_iota(jnp.int32, (N_GROUP, tm), 0)
    e_io = lax.broadcasted_iota(jnp.int32, (N_EXPERTS, tm), 0)
    e_grp = e_io >> int(math.log2(GROUP_SZ))
    masked = jnp.full((N_EXPERTS, tm), neg, F32)
    for _ in range(TOPK_GROUP):
        mx = jnp.max(gs, axis=0, keepdims=True)
        gi = jnp.min(jnp.where(gs == mx, g_io, N_GROUP), axis=0, keepdims=True)
        gs = jnp.where(g_io == gi, neg, gs)
        masked = jnp.where(e_grp == gi, ch, masked)
    idx_rows, w_rows = [], []
    onehot = jnp.zeros((N_EXPERTS, tm), F32)
    for _ in range(TOP_K):
        mx = jnp.max(masked, axis=0, keepdims=True)
        ei = jnp.min(jnp.where(masked == mx, e_io, N_EXPERTS), axis=0, keepdims=True)
        hit = e_io == ei
        idx_rows.append(ei)
        w_rows.append(jnp.sum(jnp.where(hit, sc, 0.0), axis=0, keepdims=True))
        onehot = jnp.where(hit, 1.0, onehot)
        masked = jnp.where(hit, neg, masked)
    wsel = jnp.concatenate(w_rows, axis=0)
    wts_ref[...] = wsel / jnp.sum(wsel, axis=0, keepdims=True) * ROUTED_SCALE
    idx_ref[...] = jnp.concatenate(idx_rows, axis=0)

    tr = lax.broadcasted_iota(jnp.int32, (tm, tm), 0)
    tc_ = lax.broadcasted_iota(jnp.int32, (tm, tm), 1)
    before = jnp.where(tr < tc_, 1.0, 0.0).astype(BF16)
    cum = jnp.dot(onehot.astype(BF16), before, preferred_element_type=F32) + carry_scr[...]
    rank_ref[...] = jnp.concatenate(
        [jnp.sum(jnp.where(e_io == ei, cum, 0.0), axis=0, keepdims=True) for ei in idx_rows],
        axis=0).astype(jnp.int32)
    carry_scr[...] = carry_scr[...] + jnp.sum(onehot, axis=1, keepdims=True)
    cnt_ref[...] = carry_scr[...]


def _router(h, w_router_t, bias_col, tm):
    t = h.shape[0]
    return pl.pallas_call(
        functools.partial(_router_kernel, tm=tm),
        grid=(t // tm,),
        in_specs=[pl.BlockSpec((tm, D_MODEL), lambda i: (i, 0)),
                  pl.BlockSpec((N_EXPERTS, D_MODEL), lambda i: (0, 0)),
                  pl.BlockSpec((N_EXPERTS, 1), lambda i: (0, 0))],
        out_specs=[pl.BlockSpec((TOP_K, tm), lambda i: (0, i)),
                   pl.BlockSpec((TOP_K, tm), lambda i: (0, i)),
                   pl.BlockSpec((TOP_K, tm), lambda i: (0, i)),
                   pl.BlockSpec((N_EXPERTS, 1), lambda i: (0, 0))],
        out_shape=[jax.ShapeDtypeStruct((TOP_K, t), jnp.int32),
                   jax.ShapeDtypeStruct((TOP_K, t), F32),
                   jax.ShapeDtypeStruct((TOP_K, t), jnp.int32),
                   jax.ShapeDtypeStruct((N_EXPERTS, 1), F32)],
        scratch_shapes=[pltpu.VMEM((N_EXPERTS, 1), F32)],
        compiler_params=_cparams(("arbitrary",)),
        name="router",
    )(h, w_router_t, bias_col)


def _dest_kernel(idx_ref, rank_ref, start_ref, dest_ref, *, tm, n_rows):
    e_io = lax.broadcasted_iota(jnp.int32, (N_EXPERTS, tm), 0)
    starts = start_ref[...]
    rows = []
    for k in range(TOP_K):
        seg = jnp.sum(jnp.where(e_io == idx_ref[k:k + 1, :], starts, 0.0), axis=0, keepdims=True)
        rows.append(seg.astype(jnp.int32) + rank_ref[k:k + 1, :])
    base = jnp.concatenate(rows, axis=0)
    for p in range(N_PIECE):
        dest_ref[p] = base + p * n_rows


def _dest(idx_t, rank_t, seg_start_col, n_rows, tm):
    t = idx_t.shape[1]
    blk = pl.BlockSpec((TOP_K, tm), lambda i: (0, i))
    return pl.pallas_call(
        functools.partial(_dest_kernel, tm=tm, n_rows=n_rows),
        grid=(t // tm,),
        in_specs=[blk, blk, pl.BlockSpec((N_EXPERTS, 1), lambda i: (0, 0))],
        out_specs=pl.BlockSpec((N_PIECE, TOP_K, tm), lambda i: (0, 0, i)),
        out_shape=jax.ShapeDtypeStruct((N_PIECE, TOP_K, t), jnp.int32),
        compiler_params=_cparams(("parallel",)),
        name="dispatch_rows",
    )(idx_t, rank_t, seg_start_col)


def _expert_kernel(be_ref, nu_ref, first_ref, nxt_ref, slot_ref, x_ref, wg_hbm, wu_hbm, wd_hbm, y_ref,
                   wg_buf, wu_buf, wd_buf, wgu_scr, wd_scr, sems):
    i = pl.program_id(0)
    live = i < nu_ref[0]

    def fetch(e, slot):
        return [pltpu.make_async_copy(wg_hbm.at[e], wg_buf.at[slot], sems.at[slot, 0]),
                pltpu.make_async_copy(wu_hbm.at[e], wu_buf.at[slot], sems.at[slot, 1]),
                pltpu.make_async_copy(wd_hbm.at[e], wd_buf.at[slot], sems.at[slot, 2])]

    @pl.when(live & (first_ref[i] == 1))
    def _new_expert():
        slot = slot_ref[i]

        @pl.when(i == 0)
        def _():
            for cp in fetch(be_ref[i], slot):
                cp.start()

        @pl.when(nxt_ref[i] >= 0)
        def _():
            for cp in fetch(nxt_ref[i], 1 - slot):
                cp.start()

        for cp in fetch(be_ref[i], slot):
            cp.wait()
        wgu_scr[:, :D_EXPERT] = wg_buf[slot].astype(BF16)
        wgu_scr[:, D_EXPERT:] = wu_buf[slot].astype(BF16)
        wd_scr[...] = wd_buf[slot].astype(BF16)

    @pl.when(live)
    def _():
        n_sub = 2
        sub = EXPERT_BM // n_sub
        xs = [jnp.concatenate([c.astype(BF16) for c in
                               _load_pieces([x_ref[p, s * sub:(s + 1) * sub] for p in range(N_PIECE)])], axis=1)
              for s in range(n_sub)]
        gus = [jnp.dot(x, wgu_scr[...], preferred_element_type=F32) for x in xs]
        acts = [(_silu(gu[:, :D_EXPERT]) * gu[:, D_EXPERT:]).astype(BF16) for gu in gus]
        ys = [jnp.dot(a, wd_scr[...], preferred_element_type=F32) for a in acts]
        for s, y in enumerate(ys):
            for p in range(N_PIECE):
                y_ref[p, s * sub:(s + 1) * sub] = _pack_halves(y[:, 2 * p * PIECE_W:2 * (p + 1) * PIECE_W])


def _experts(xs, blk_exp, n_used, blk_first, blk_next, blk_slot, w_gate, w_up, w_down):
    n_rows = xs.shape[1]
    n_blocks = n_rows // EXPERT_BM

    def row_block(i, be, nu, *_):
        return (0, jnp.minimum(i, nu[0] - 1), 0)

    grid_spec = pltpu.PrefetchScalarGridSpec(
        num_scalar_prefetch=5,
        grid=(n_blocks,),
        in_specs=[pl.BlockSpec((N_PIECE, EXPERT_BM, PIECE_W), row_block),
                  pl.BlockSpec(memory_space=pl.ANY),
                  pl.BlockSpec(memory_space=pl.ANY),
                  pl.BlockSpec(memory_space=pl.ANY)],
        out_specs=pl.BlockSpec((N_PIECE, EXPERT_BM, PIECE_W), row_block),
        scratch_shapes=[pltpu.VMEM((2, D_MODEL, D_EXPERT), F32),
                        pltpu.VMEM((2, D_MODEL, D_EXPERT), F32),
                        pltpu.VMEM((2, D_EXPERT, D_MODEL), F32),
                        pltpu.VMEM((D_MODEL, 2 * D_EXPERT), BF16),
                        pltpu.VMEM((D_EXPERT, D_MODEL), BF16),
                        pltpu.SemaphoreType.DMA((2, 3))],
    )
    return pl.pallas_call(
        _expert_kernel,
        grid_spec=grid_spec,
        out_shape=jax.ShapeDtypeStruct((N_PIECE, n_rows, PIECE_W), jnp.uint32),
        compiler_params=_cparams(("arbitrary",)),
        name="experts",
    )(blk_exp, n_used, blk_first, blk_next, blk_slot, xs, w_gate, w_up, w_down)


SC_WINDOW = 128
V7X_SC_CORES = 2
V7X_SC_SUBCORES = 16


def _sc_mesh():
    return plsc.VectorSubcoreMesh(core_axis_name="core", subcore_axis_name="subcore",
                                  num_cores=V7X_SC_CORES, num_subcores=V7X_SC_SUBCORES)


def _sc_scatter_rows(src, dest, n_rows, seg, repeat):
    d = src.shape[1]
    n_idx = dest.shape[0]
    seg_blocks = seg // SC_WINDOW
    dest2 = dest.reshape(1, n_idx)

    def src_block(i):
        return ((i // (repeat * seg_blocks)) * seg_blocks + i % seg_blocks, 0)

    @functools.partial(pl.kernel, out_type=jax.ShapeDtypeStruct((n_rows, d), src.dtype), mesh=_sc_mesh(),
                       scratch_types=[], name="sc_dispatch")
    def run(src_hbm, idx_hbm, out_hbm):
        def body(rows_vmem, idx_vmem):
            pltpu.sync_copy(rows_vmem, out_hbm.at[idx_vmem.at[0]])

        pltpu.emit_pipeline(
            body,
            grid=(n_idx // SC_WINDOW,),
            in_specs=[pl.BlockSpec((SC_WINDOW, d), src_block),
                      pl.BlockSpec((1, SC_WINDOW), lambda i: (0, i))],
            out_specs=[],
            core_axis_name=("core", "subcore"),
            dimension_semantics=(pltpu.PARALLEL,),
        )(src_hbm, idx_hbm)

    return run(src, dest2)


def _sc_gather_rows(table, idx):
    d = table.shape[1]
    n_idx = idx.shape[0]
    idx2 = idx.reshape(1, n_idx)

    @functools.partial(pl.kernel, out_type=jax.ShapeDtypeStruct((n_idx, d), table.dtype), mesh=_sc_mesh(),
                       scratch_types=[], name="sc_combine_gather")
    def run(table_hbm, idx_hbm, out_hbm):
        def body(idx_vmem, rows_vmem):
            pltpu.sync_copy(table_hbm.at[idx_vmem.at[0]], rows_vmem)

        pltpu.emit_pipeline(
            body,
            grid=(n_idx // SC_WINDOW,),
            in_specs=[pl.BlockSpec((1, SC_WINDOW), lambda i: (0, i))],
            out_specs=[pl.BlockSpec((SC_WINDOW, d), lambda i: (i, 0))],
            core_axis_name=("core", "subcore"),
            dimension_semantics=(pltpu.PARALLEL,),
        )(idx_hbm, out_hbm)

    return run(table, idx2)


def _combine_kernel(yg_ref, wts_ref, h_ref, wgu_ref, wd_ref, g_ref, b_ref, outa_ref, outb_ref, *, alpha, tiles_a):
    wts = wts_ref[...]
    acc = None
    for k in range(TOP_K):
        cols = _load_pieces([yg_ref[p, k] for p in range(N_PIECE)])
        wk = wts[:, k:k + 1]
        acc = [c * wk for c in cols] if acc is None else [a + c * wk for a, c in zip(acc, cols)]
    routed = jnp.concatenate(acc, axis=1)
    h = h_ref[...]
    gu = _dot(h, wgu_ref[...])
    shared = _dot(_silu(gu[:, :D_EXPERT]) * gu[:, D_EXPERT:], wd_ref[...])
    y = _layer_norm(alpha * h + (routed + shared), g_ref[...], b_ref[...])
    i = pl.program_id(0)

    @pl.when(i < tiles_a)
    def _():
        outa_ref[...] = y

    @pl.when(i >= tiles_a)
    def _():
        outb_ref[...] = y


def _combine(yg, wts, h, w_sh_gu, w_sh_down, ln_g, ln_b, alpha, tm, t_a):
    t = h.shape[0]
    tiles_a = t_a // tm
    row = lambda i: (i, 0)
    const2 = lambda i: (0, 0)
    return pl.pallas_call(
        functools.partial(_combine_kernel, alpha=alpha, tiles_a=tiles_a),
        grid=(t // tm,),
        in_specs=[pl.BlockSpec((N_PIECE, TOP_K, tm, PIECE_W), lambda i: (0, 0, i, 0)),
                  pl.BlockSpec((tm, TOP_K), row),
                  pl.BlockSpec((tm, D_MODEL), row),
                  pl.BlockSpec(w_sh_gu.shape, const2),
                  pl.BlockSpec(w_sh_down.shape, const2),
                  pl.BlockSpec((1, D_MODEL), const2),
                  pl.BlockSpec((1, D_MODEL), const2)],
        out_specs=[pl.BlockSpec((tm, D_MODEL), lambda i: (jnp.minimum(i, tiles_a - 1), 0)),
                   pl.BlockSpec((tm, D_MODEL), lambda i: (jnp.maximum(i - tiles_a, 0), 0))],
        out_shape=[jax.ShapeDtypeStruct((t_a, D_MODEL), F32), jax.ShapeDtypeStruct((t - t_a, D_MODEL), F32)],
        compiler_params=_cparams(("arbitrary",)),
        name="combine_ln2",
    )(yg, wts, h, w_sh_gu, w_sh_down, ln_g, ln_b)


def _layer(xp, xs, s_delta, s_conv, s_pool, w_in, conv_w, a_log, dt_bias, gamma_a, w_br_a, w_pool,
           pool_scale, w_br_b, w_out, ln1_g, ln1_b, w_router, router_bias, w_exp_gate, w_exp_up,
           w_exp_down, w_sh_gate, w_sh_up, w_sh_down, ln2_g, ln2_b, alpha):
    Bp, Lp, _ = xp.shape
    Bs, Ls, _ = xs.shape
    Tp, Ts = Bp * Lp, Bs * Ls
    T = Tp + Ts
    x_p = xp.reshape(Tp, D_MODEL)
    x_s = xs.reshape(Ts, D_MODEL)

    o_z, o_b, o_a, o_u = QKV_W, QKV_W + V_W, QKV_W + V_W + H_V, QKV_W + V_W + 2 * H_V
    w_main = jnp.concatenate([w_in[:, :o_b], w_in[:, o_u:]], axis=1).astype(BF16)
    w_ba = jnp.pad(w_in[:, o_b:o_u], ((0, 0), (0, 128 - 2 * H_V))).astype(BF16)
    proj, ba = _in_proj(x_p, x_s, w_main, w_ba, 1024, MAIN_W // 4)

    o_p, sd_p = _gdn(proj, ba, conv_w, a_log, dt_bias, gamma_a, 0, Bp, Lp, 64, 1)
    o_s, sd_s = _gdn(proj, ba, conv_w, a_log, dt_bias, gamma_a, Tp, Bs, Ls, Ls, GDN_STATE_SEQS,
                     conv_buf=s_conv, s0=s_delta)

    pooled_p = _pool(proj, 0, Bp, Lp, 256, 1, 0)
    pooled_s = _pool(proj, Tp, Bs, Ls, Ls, POOL_STATE_SEQS, PAST_LEN, pool_buf=s_pool)

    h, hp = _post(pooled_p, pooled_s, o_p, o_s, proj, x_p, x_s, w_pool.astype(BF16), pool_scale.reshape(1, D_MODEL),
                  w_br_b.astype(BF16), w_br_a.astype(BF16), w_out.astype(BF16),
                  ln1_g.reshape(1, D_MODEL), ln1_b.reshape(1, D_MODEL), alpha, 256)

    idx_t, wts_t, rank_t, cnt = _router(h, w_router.T, router_bias.reshape(N_EXPERTS, 1), 512)
    counts = cnt[:, 0].astype(jnp.int32)
    padded = ((counts + EXPERT_BM - 1) // EXPERT_BM) * EXPERT_BM
    pends = jnp.cumsum(padded)
    pstarts = pends - padded
    n_blocks = (T * TOP_K + N_EXPERTS * (EXPERT_BM - 1) + EXPERT_BM - 1) // EXPERT_BM
    n_rows = n_blocks * EXPERT_BM
    piece_rows = _dest(idx_t, rank_t, pstarts.astype(F32).reshape(N_EXPERTS, 1), n_rows, 512)
    blk_start = jnp.arange(n_blocks, dtype=jnp.int32) * EXPERT_BM
    blk_exp = jnp.minimum(jnp.sum((pends[None, :] <= blk_start[:, None]).astype(jnp.int32), axis=1),
                          N_EXPERTS - 1)
    n_used = (pends[-1:] // EXPERT_BM).astype(jnp.int32)
    has_rows = counts > 0
    e_ids = jnp.arange(N_EXPERTS, dtype=jnp.int32)
    slot_of = (jnp.cumsum(has_rows.astype(jnp.int32)) - 1) & 1
    later = jnp.where(has_rows, e_ids, N_EXPERTS)
    next_of = jnp.concatenate([lax.cummin(later, reverse=True)[1:], jnp.full((1,), N_EXPERTS, jnp.int32)])
    next_of = jnp.where(next_of < N_EXPERTS, next_of, -1)
    blk_first = (blk_start == pstarts[blk_exp]).astype(jnp.int32)
    blk_next = next_of[blk_exp]
    blk_slot = slot_of[blk_exp]

    piece_idx = piece_rows.reshape(N_PIECE * TOP_K * T)
    x_sorted = _sc_scatter_rows(hp.reshape(N_PIECE * T, PIECE_W), piece_idx, N_PIECE * n_rows, T, TOP_K)
    y_sorted = _experts(x_sorted.reshape(N_PIECE, n_rows, PIECE_W), blk_exp, n_used, blk_first, blk_next,
                        blk_slot, w_exp_gate, w_exp_up, w_exp_down)
    yg = _sc_gather_rows(y_sorted.reshape(N_PIECE * n_rows, PIECE_W), piece_idx)
    yg = yg.reshape(N_PIECE, TOP_K, T, PIECE_W)

    w_sh_gu = jnp.concatenate([w_sh_gate, w_sh_up], axis=1).astype(BF16)
    y_p, y_s = _combine(yg, wts_t.T, h, w_sh_gu, w_sh_down.astype(BF16),
                        ln2_g.reshape(1, D_MODEL), ln2_b.reshape(1, D_MODEL), alpha, 256, Tp)

    def carried_rows(keep, col0, width, name, old_p, old_s):
        parts, counts = [], []
        for x2d, B, L in ((x_p, Bp, Lp), (x_s, Bs, Ls)):
            n = min(L, keep)
            rows = jnp.arange(B, dtype=jnp.int32)[:, None] * L + (L - n) + jnp.arange(n, dtype=jnp.int32)
            parts.append(jnp.take(x2d, rows.reshape(-1), axis=0).astype(BF16))
            counts.append((B, n))
        n_tail = sum(B * n for B, n in counts)
        n_pad = -(-n_tail // 128) * 128
        xt = jnp.pad(jnp.concatenate(parts, axis=0), ((0, n_pad - n_tail), (0, 0)))
        tail = _matmul(xt, w_main, col0, width, n_pad, 1024, F32, name)
        outs, r0 = [], 0
        for (B, n), old in zip(counts, (old_p, old_s)):
            new = tail[r0:r0 + B * n].reshape(B, n, width)
            outs.append(new if n == keep else jnp.concatenate([old[:, n:], new], axis=1))
            r0 += B * n
        return outs

    new_conv_p, new_conv_s = carried_rows(CONV_W - 1, 0, QKV_W, "in_proj_conv_rows",
                                          jnp.zeros((Bp, CONV_W - 1, QKV_W), F32), s_conv)
    new_pool_p, new_pool_s = carried_rows(POOL_BUF, U_BLK * D_MODEL, D_MODEL, "in_proj_pool_rows",
                                          jnp.zeros((Bp, POOL_BUF, D_MODEL), F32), s_pool)
    return (y_p.reshape(Bp, Lp, D_MODEL), y_s.reshape(Bs, Ls, D_MODEL),
            sd_p, new_conv_p, new_pool_p, sd_s, new_conv_s, new_pool_s)


def kernel(x_prompt, x_sample, state_delta, state_conv, state_pool, w_in, conv_w, a_log, dt_bias, gamma_a,
           w_br_a, w_pool, pool_scale, w_br_b, w_out, ln1_g, ln1_b, w_router, router_bias,
           w_exp_gate, w_exp_up, w_exp_down, w_sh_gate, w_sh_up, w_sh_down, ln2_g, ln2_b):
    depth = w_in.shape[0]
    alpha = (2 * depth) ** 0.25
    yp, ys = x_prompt, x_sample
    outs = [[] for _ in range(6)]
    for l in range(depth):
        res = _layer(yp, ys, state_delta[l], state_conv[l], state_pool[l], w_in[l], conv_w[l], a_log[l],
                     dt_bias[l], gamma_a[l], w_br_a[l], w_pool[l], pool_scale[l], w_br_b[l], w_out[l],
                     ln1_g[l], ln1_b[l], w_router[l], router_bias[l], w_exp_gate[l], w_exp_up[l],
                     w_exp_down[l], w_sh_gate[l], w_sh_up[l], w_sh_down[l], ln2_g[l], ln2_b[l], alpha)
        yp, ys = res[0], res[1]
        for lst, v in zip(outs, res[2:]):
            lst.append(v)
    return (yp, ys) + tuple(jnp.stack(v) for v in outs)
```

```python
import functools
import math

import jax
import jax.numpy as jnp
from jax import lax
from jax.experimental import pallas as pl
from jax.experimental.pallas import tpu as pltpu
from jax.experimental.pallas import tpu_sc as plsc

F32 = jnp.float32
BF16 = jnp.bfloat16

D_MODEL = 1024
H_K = 8
D_K = 128
H_V = 16
D_V = 128
Q_W = H_K * D_K
V_W = H_V * D_V
QKV_W = 2 * Q_W + V_W
CONV_W = 4
POOL_WINDOWS = (2, 4, 8, 16)
POOL_GROUP_W = D_MODEL // len(POOL_WINDOWS)
POOL_BUF = max(POOL_WINDOWS) - 1
N_EXPERTS = 256
TOP_K = 8
N_GROUP = 8
TOPK_GROUP = 4
GROUP_SZ = N_EXPERTS // N_GROUP
D_EXPERT = D_MODEL // 4
ROUTED_SCALE = 2.5
LN_EPS = 1e-5
NORM_EPS = 1e-6
PAST_LEN = 16384

MAIN_W = QKV_W + V_W + 3 * D_MODEL
Z_BLK = QKV_W // V_W
U_BLK = (QKV_W + V_W) // D_MODEL

STACK_ROWS = 128
SOLVE_BASE = 16
GDN_WAVE = 8
EXPERT_BM = 512
GDN_STATE_SEQS = 4
POOL_STATE_SEQS = 16
VMEM_LIMIT = 56 * 1024 * 1024


def _cparams(sem):
    return pltpu.CompilerParams(dimension_semantics=sem, vmem_limit_bytes=VMEM_LIMIT)


def _sigmoid(x):
    return 0.5 * jnp.tanh(0.5 * x) + 0.5


def _silu(x):
    return x * _sigmoid(x)


def _softplus(x):
    return jnp.maximum(x, 0.0) + jnp.log(1.0 + jnp.exp(-jnp.abs(x)))


def _dot(a, b):
    return jnp.dot(a.astype(BF16), b.astype(BF16), preferred_element_type=F32)


def _dot_nt(a, b):
    return lax.dot_general(a.astype(BF16), b.astype(BF16), (((1,), (1,)), ((), ())),
                           preferred_element_type=F32)


def _dot_tn(a, b):
    return lax.dot_general(a.astype(BF16), b.astype(BF16), (((0,), (0,)), ((), ())),
                           preferred_element_type=F32)


def _block_diag(b1, b2):
    z = jnp.zeros(b1.shape, b1.dtype)
    return jnp.concatenate([jnp.concatenate([b1, z], axis=1), jnp.concatenate([z, b2], axis=1)], axis=0)


def _dot_pair(a1, b1, a2, b2):
    n = b1.shape[1]
    out = _dot(jnp.concatenate([a1, a2], axis=1), _block_diag(b1.astype(BF16), b2.astype(BF16)))
    return out[:, :n], out[:, n:]


def _dot_pair_nt(a1, b1, a2, b2):
    n = b1.shape[0]
    out = _dot_nt(jnp.concatenate([a1, a2], axis=1), _block_diag(b1.astype(BF16), b2.astype(BF16)))
    return out[:, :n], out[:, n:]


def _split(a):
    hi = a.astype(BF16)
    lo = (a - hi.astype(F32)).astype(BF16)
    return hi, lo


def _dot3_nt(a, b):
    ah, al = _split(a)
    bh, bl = _split(b)
    d = functools.partial(lax.dot_general, dimension_numbers=(((1,), (1,)), ((), ())),
                          preferred_element_type=F32)
    return d(ah, bh) + (d(ah, bl) + d(al, bh))


def _pack_halves(x):
    n = x.shape[1] // 2
    hi = lax.bitcast_convert_type(x[:, :n].astype(BF16).astype(F32), jnp.uint32)
    lo = lax.bitcast_convert_type(x[:, n:].astype(BF16).astype(F32), jnp.uint32)
    return (hi & jnp.uint32(0xFFFF0000)) | (lo >> 16)


def _unpack_halves(w):
    hi = lax.bitcast_convert_type(w & jnp.uint32(0xFFFF0000), F32)
    lo = lax.bitcast_convert_type(w << 16, F32)
    return hi, lo


N_PIECE = 2
PIECE_W = D_MODEL // (2 * N_PIECE)


def _store_pieces(ref, x):
    for p in range(N_PIECE):
        ref[p] = _pack_halves(x[:, 2 * p * PIECE_W:2 * (p + 1) * PIECE_W])


def _load_pieces(pieces):
    cols = []
    for w in pieces:
        cols.extend(_unpack_halves(w))
    return cols


def _layer_norm(x, g, b):
    mu = jnp.mean(x, axis=-1, keepdims=True)
    xc = x - mu
    var = jnp.mean(xc * xc, axis=-1, keepdims=True)
    return xc * lax.rsqrt(var + LN_EPS) * g + b


def _in_proj_kernel(xa_ref, xb_ref, w_ref, wba_ref, o_ref, ba_ref, x_scr, *, tiles_a):
    @pl.when(pl.program_id(1) == 0)
    def _():
        x = jnp.where(pl.program_id(0) < tiles_a, xa_ref[...], xb_ref[...]).astype(BF16)
        x_scr[...] = x
        ba_ref[...] = jnp.dot(x, wba_ref[...], preferred_element_type=F32)

    o_ref[...] = jnp.dot(x_scr[...], w_ref[...], preferred_element_type=F32).astype(o_ref.dtype)


def _in_proj(x_a, x_b, w_main, w_ba, tm, tn):
    k = x_a.shape[1]
    t = x_a.shape[0] + x_b.shape[0]
    n = w_main.shape[1]
    tiles_a = x_a.shape[0] // tm
    return pl.pallas_call(
        functools.partial(_in_proj_kernel, tiles_a=tiles_a),
        grid=(t // tm, n // tn),
        in_specs=[pl.BlockSpec((tm, k), lambda i, j: (jnp.minimum(i, tiles_a - 1), 0)),
                  pl.BlockSpec((tm, k), lambda i, j: (jnp.maximum(i - tiles_a, 0), 0)),
                  pl.BlockSpec((k, tn), lambda i, j: (0, j)),
                  pl.BlockSpec(w_ba.shape, lambda i, j: (0, 0))],
        out_specs=[pl.BlockSpec((tm, tn), lambda i, j: (i, j)),
                   pl.BlockSpec((tm, w_ba.shape[1]), lambda i, j: (i, 0))],
        out_shape=[jax.ShapeDtypeStruct((t, n), BF16), jax.ShapeDtypeStruct((t, w_ba.shape[1]), F32)],
        scratch_shapes=[pltpu.VMEM((tm, k), BF16)],
        compiler_params=_cparams(("parallel", "arbitrary")),
        name="in_proj",
    )(x_a, x_b, w_main, w_ba)


def _mm_kernel(x_ref, w_ref, o_ref):
    o_ref[...] = jnp.dot(x_ref[...], w_ref[...], preferred_element_type=F32).astype(o_ref.dtype)


def _matmul(x, w, col0, n, tm, tn, out_dtype, name):
    t, k = x.shape
    cb0 = col0 // tn
    return pl.pallas_call(
        _mm_kernel,
        grid=(t // tm, n // tn),
        in_specs=[pl.BlockSpec((tm, k), lambda i, j: (i, 0)),
                  pl.BlockSpec((k, tn), lambda i, j: (0, cb0 + j))],
        out_specs=pl.BlockSpec((tm, tn), lambda i, j: (i, j)),
        out_shape=jax.ShapeDtypeStruct((t, n), out_dtype),
        compiler_params=_cparams(("parallel", "parallel")),
        name=name,
    )(x, w)


def _gdn_kernel(*refs, C, hg, nseq, has_state):
    if has_state:
        (qkv_ref, z_ref, ba_ref, cw_ref, ad_ref, gam_ref, buf_ref, s0_ref,
         o_ref, sout_ref, s_scr, tail_scr, act_scr) = refs
    else:
        (qkv_ref, z_ref, ba_ref, cw_ref, ad_ref, gam_ref,
         o_ref, sout_ref, s_scr, tail_scr, act_scr) = refs
    R = hg * C
    ng = H_V // hg
    log2c = int(math.log2(C))
    c = pl.program_id(1)

    @pl.when(c == 0)
    def _init():
        if has_state:
            tail_scr[...] = jnp.zeros(tail_scr.shape, F32)
            for s in range(nseq):
                s_scr[s * H_V:(s + 1) * H_V] = s0_ref[s]
                tail_scr[(s + 1) * 8 - (CONV_W - 1):(s + 1) * 8] = buf_ref[s]
        else:
            s_scr[...] = jnp.zeros(s_scr.shape, F32)
            tail_scr[...] = jnp.zeros(tail_scr.shape, F32)

    cw = cw_ref[...]
    qkv_all = qkv_ref[...].astype(F32)
    n_prev = CONV_W - 1
    for s in range(nseq):
        x = qkv_all[s * C:(s + 1) * C]
        xe = jnp.concatenate([tail_scr[s * 8:(s + 1) * 8], x], axis=0)
        y = x * cw[n_prev:CONV_W]
        for j in range(n_prev):
            y = y + xe[8 - n_prev + j:8 - n_prev + j + C] * cw[j:j + 1]
        tail_scr[s * 8:(s + 1) * 8] = x[C - 8:C]
        act_scr[s * C:(s + 1) * C] = _silu(y)
        for off, scale in ((0, D_K ** -0.5), (Q_W, 1.0)):
            for kh in range(H_K):
                cols = slice(off + kh * D_K, off + (kh + 1) * D_K)
                v = act_scr[s * C:(s + 1) * C, cols]
                act_scr[s * C:(s + 1) * C, cols] = v * (
                    lax.rsqrt(jnp.sum(v * v, axis=-1, keepdims=True) + NORM_EPS) * scale)

    ri = lax.broadcasted_iota(jnp.int32, (R, R), 0)
    ci = lax.broadcasted_iota(jnp.int32, (R, R), 1)
    same = (ri >> log2c) == (ci >> log2c)
    incl = same & (ri >= ci)
    strict = same & (ri > ci)
    eye = ri == ci
    is_last = ci == (((ri >> log2c) << log2c) + (C - 1))
    eye_f = jnp.where(eye, 1.0, 0.0).astype(F32)
    cum_u = jnp.where(same & (ri <= ci), 1.0, 0.0).astype(BF16)

    nrow = ba_ref.shape[2] // 2
    adv = ad_ref[...]
    dd = functools.partial(jnp.dot, preferred_element_type=F32)
    beta_seq, G_seq = [], []
    for s in range(nseq):
        bav = ba_ref[s, 0]
        beta_seq.append(_sigmoid(bav[0:nrow]))
        g_rows = -jnp.exp(adv[0:nrow]) * _softplus(bav[nrow:2 * nrow] + adv[nrow:2 * nrow])
        g1 = g_rows.astype(BF16)
        r1 = g_rows - g1.astype(F32)
        g2 = r1.astype(BF16)
        g3 = (r1 - g2.astype(F32)).astype(BF16)
        G_seq.append(dd(g1, cum_u) + (dd(g2, cum_u) + dd(g3, cum_u)))

    gam = gam_ref[...]
    rep = H_V // H_K

    def stacked(s, heads, off, per_k_head):
        rows = slice(s * C, (s + 1) * C)
        blocks = [(h // rep if per_k_head else h) for h in heads]
        return jnp.concatenate([act_scr[rows, off + b * D_K:off + (b + 1) * D_K] for b in blocks], axis=0)

    log2b = min(log2c, int(math.log2(SOLVE_BASE)))
    same_base = ((ri ^ ci) >> log2b) == 0
    row_head = lax.broadcasted_iota(jnp.int32, (R, D_V), 0) >> log2c
    z_all = z_ref[...].astype(F32)
    specs = [(s, g) for s in range(nseq) for g in range(ng)]
    for w0 in range(0, len(specs), GDN_WAVE):
        grp = [dict(s=s, g=g, rows=slice(s * C, (s + 1) * C), s0=s * H_V, heads=[g * hg + hh for hh in range(hg)])
               for s, g in specs[w0:w0 + GDN_WAVE]]
        pairs = [(grp[i], grp[i + 1]) for i in range(0, len(grp), 2)]
        for da, db in pairs:
            ka, kb = (stacked(d["s"], d["heads"], Q_W, True) for d in (da, db))
            qa, qb = (stacked(d["s"], d["heads"], 0, True) for d in (da, db))
            da["kk"], db["kk"] = _dot_pair_nt(ka, ka, kb, kb)
            da["qk"], db["qk"] = _dot_pair_nt(qa, ka, qb, kb)
        for d in grp:
            grow = G_seq[d["s"]][d["g"]:d["g"] + 1]
            brow = beta_seq[d["s"]][d["g"]:d["g"] + 1]
            gcol = jnp.sum(eye_f * grow, axis=1, keepdims=True)
            bcol = jnp.sum(eye_f * brow, axis=1, keepdims=True)
            glast = jnp.sum(jnp.where(is_last, grow, 0.0), axis=1, keepdims=True)
            dm = jnp.where(incl, jnp.exp(jnp.minimum(gcol - grow, 0.0)), 0.0)
            d.update(gcol=gcol, bcol=bcol, glast=glast, eg=jnp.exp(gcol))
            d["m"] = jnp.where(strict, -(d["kk"] * bcol * dm), 0.0)
            d["attn"] = jnp.where(incl, d["qk"] * dm, 0.0)
            d["pw"] = jnp.where(same_base, d["m"], 0.0)
            d["t"] = eye_f + d["pw"]

        for r in range(log2b):
            last = r == log2b - 1
            if r == 0:
                if not last:
                    for da, db in pairs:
                        da["pw"], db["pw"] = _dot_pair(da["pw"], da["pw"], db["pw"], db["pw"])
            elif last:
                for da, db in pairs:
                    ua, ub = _dot_pair(da["pw"], da["t"], db["pw"], db["t"])
                    da["t"], db["t"] = da["t"] + ua, db["t"] + ub
            else:
                for d in grp:
                    out = _dot(d["pw"], jnp.concatenate([d["pw"], d["t"]], axis=1))
                    d["pw"] = out[:, :R]
                    d["t"] = d["t"] + out[:, R:]
        for lev in range(log2b + 1, log2c + 1):
            coupling = ((ri ^ ci) >> (lev - 1)) == 1
            for da, db in pairs:
                da["y"], db["y"] = _dot_pair(da["t"], jnp.where(coupling, da["m"], 0.0),
                                             db["t"], jnp.where(coupling, db["m"], 0.0))
            for da, db in pairs:
                ua, ub = _dot_pair(da["y"], da["t"], db["y"], db["t"])
                da["t"], db["t"] = da["t"] + ua, db["t"] + ub
        for d in grp:
            rhs = jnp.concatenate([stacked(d["s"], d["heads"], 2 * Q_W, False) * d["bcol"],
                                   stacked(d["s"], d["heads"], Q_W, True) * (d["bcol"] * d["eg"])], axis=1)
            d["x"] = _dot(d["t"], rhs)

        for d in grp:
            qe = stacked(d["s"], d["heads"], 0, True) * d["eg"]
            lhs = [jnp.concatenate([d["x"][hh * C:(hh + 1) * C, D_V:], qe[hh * C:(hh + 1) * C]], axis=0)
                   for hh in range(hg)]
            d["xs"] = []
            for hh in range(0, hg, 2):
                d["xs"] += _dot_pair(lhs[hh], s_scr[d["s0"] + d["heads"][hh]],
                                     lhs[hh + 1], s_scr[d["s0"] + d["heads"][hh + 1]])
        for d in grp:
            d["vnew"] = jnp.concatenate(
                [d["x"][hh * C:(hh + 1) * C, :D_V] - xs[:C] for hh, xs in enumerate(d["xs"])], axis=0)
        for da, db in pairs:
            oa, ob = _dot_pair(da["attn"], da["vnew"], db["attn"], db["vnew"])
            da["o"] = jnp.concatenate([xs[C:] for xs in da["xs"]], axis=0) + oa
            db["o"] = jnp.concatenate([xs[C:] for xs in db["xs"]], axis=0) + ob

        for d in grp:
            kt = stacked(d["s"], d["heads"], Q_W, True) * jnp.exp(d["glast"] - d["gcol"])
            egl = jnp.exp(d["glast"])
            for hh in range(0, hg, 2):
                two = jnp.concatenate([jnp.where(row_head == hh + j, d["vnew"], 0.0) for j in range(2)], axis=1)
                upd = _dot_tn(kt, two)
                for j in range(2):
                    h = d["s0"] + d["heads"][hh + j]
                    s_scr[h] = s_scr[h] * egl[(hh + j) * C:(hh + j) * C + 1] + upd[:, j * D_V:(j + 1) * D_V]

        for d in grp:
            o = d["o"]
            zst = jnp.concatenate([z_all[d["rows"], h * D_V:(h + 1) * D_V] for h in d["heads"]], axis=0)
            on = o * lax.rsqrt(jnp.mean(o * o, axis=-1, keepdims=True) + NORM_EPS) * gam * _silu(zst)
            for hh, h in enumerate(d["heads"]):
                o_ref[d["rows"], h * D_V:(h + 1) * D_V] = on[hh * C:(hh + 1) * C].astype(o_ref.dtype)

    @pl.when(c == pl.num_programs(1) - 1)
    def _fin():
        for s in range(nseq):
            sout_ref[s] = s_scr[s * H_V:(s + 1) * H_V]


def _gdn(proj, ba, conv_w, a_log, dt_bias, gamma_a, row0, B, L, C, nseq, conv_buf=None, s0=None):
    has_state = s0 is not None
    hg = STACK_ROWS // C
    ng = H_V // hg
    nrow = 16
    nc = L // C
    R = STACK_ROWS
    assert nseq == 1 or nc == 1
    rows_blk = nseq * C

    def arrange(v):
        v = v.reshape(B, nc, C, ng, hg).transpose(0, 1, 3, 4, 2).reshape(B, nc, ng, R)
        return jnp.pad(v, ((0, 0), (0, 0), (0, nrow - ng), (0, 0)))

    rows = ba[row0:row0 + B * L]
    ba_arr = jnp.concatenate([arrange(rows[:, :H_V]), arrange(rows[:, H_V:2 * H_V])], axis=2)

    def arrange_param(p):
        v = jnp.broadcast_to(p.reshape(ng, hg, 1), (ng, hg, C)).reshape(ng, R)
        return jnp.pad(v, ((0, nrow - ng), (0, 0)))

    ad = jnp.concatenate([arrange_param(a_log), arrange_param(dt_bias)], axis=0)
    rb0 = row0 // rows_blk
    in_specs = [
        pl.BlockSpec((rows_blk, QKV_W), lambda b, c: (rb0 + b * nc + c, 0)),
        pl.BlockSpec((rows_blk, V_W), lambda b, c: (rb0 + b * nc + c, Z_BLK)),
        pl.BlockSpec((nseq, 1, 2 * nrow, R), lambda b, c: (b, c, 0, 0)),
        pl.BlockSpec((CONV_W, QKV_W), lambda b, c: (0, 0)),
        pl.BlockSpec((2 * nrow, R), lambda b, c: (0, 0)),
        pl.BlockSpec((1, D_V), lambda b, c: (0, 0)),
    ]
    args = [proj, proj, ba_arr, conv_w, ad, gamma_a.reshape(1, D_V)]
    if has_state:
        in_specs += [pl.BlockSpec((nseq, CONV_W - 1, QKV_W), lambda b, c: (b, 0, 0)),
                     pl.BlockSpec((nseq, H_V, D_K, D_V), lambda b, c: (b, 0, 0, 0))]
        args += [conv_buf, s0]
    return pl.pallas_call(
        functools.partial(_gdn_kernel, C=C, hg=hg, nseq=nseq, has_state=has_state),
        grid=(B // nseq, nc),
        in_specs=in_specs,
        out_specs=[pl.BlockSpec((rows_blk, V_W), lambda b, c: (b * nc + c, 0)),
                   pl.BlockSpec((nseq, H_V, D_K, D_V), lambda b, c: (b, 0, 0, 0))],
        out_shape=[jax.ShapeDtypeStruct((B * L, V_W), BF16),
                   jax.ShapeDtypeStruct((B, H_V, D_K, D_V), F32)],
        scratch_shapes=[pltpu.VMEM((nseq * H_V, D_K, D_V), F32),
                        pltpu.VMEM((nseq * 8, QKV_W), F32),
                        pltpu.VMEM((rows_blk, QKV_W), F32)],
        compiler_params=_cparams(("parallel", "arbitrary")),
        name="gdn_state" if has_state else "gdn_fresh",
    )(*args)


def _pool_kernel(*refs, tc, nseq, start_pos, has_state):
    if has_state:
        u_ref, buf_ref, o_ref, tail_scr = refs
    else:
        u_ref, o_ref, tail_scr = refs
    hist = POOL_BUF + 1
    c = pl.program_id(1)

    @pl.when(c == 0)
    def _init():
        tail_scr[...] = jnp.zeros(tail_scr.shape, F32)
        if has_state:
            for s in range(nseq):
                tail_scr[s * hist + 1:(s + 1) * hist] = buf_ref[s]

    u_all = u_ref[...].astype(F32)
    pos1 = (start_pos + 1 + c * tc + lax.broadcasted_iota(jnp.int32, (tc, 1), 0)).astype(F32)
    for s in range(nseq):
        u = u_all[s * tc:(s + 1) * tc]
        xe = jnp.concatenate([tail_scr[s * hist:(s + 1) * hist], u], axis=0)
        tail_scr[s * hist:(s + 1) * hist] = xe[tc:tc + hist]
        for gi, win in enumerate(POOL_WINDOWS):
            sl = slice(gi * POOL_GROUP_W, (gi + 1) * POOL_GROUP_W)
            acc = xe[:, sl]
            shift = 1
            while shift < win:
                acc = acc + pltpu.roll(acc, shift, 0)
                shift *= 2
            cnt = jnp.minimum(float(win), pos1)
            o_ref[s * tc:(s + 1) * tc, sl] = acc[hist:hist + tc] / cnt - u[:, sl]


def _pool(proj, row0, B, L, tc, nseq, start_pos, pool_buf=None):
    has_state = pool_buf is not None
    nc = L // tc
    assert nseq == 1 or nc == 1
    rows_blk = nseq * tc
    rb0 = row0 // rows_blk
    in_specs = [pl.BlockSpec((rows_blk, D_MODEL), lambda b, c: (rb0 + b * nc + c, U_BLK))]
    args = [proj]
    if has_state:
        in_specs.append(pl.BlockSpec((nseq, POOL_BUF, D_MODEL), lambda b, c: (b, 0, 0)))
        args.append(pool_buf)
    return pl.pallas_call(
        functools.partial(_pool_kernel, tc=tc, nseq=nseq, start_pos=start_pos, has_state=has_state),
        grid=(B // nseq, nc),
        in_specs=in_specs,
        out_specs=pl.BlockSpec((rows_blk, D_MODEL), lambda b, c: (b * nc + c, 0)),
        out_shape=jax.ShapeDtypeStruct((B * L, D_MODEL), F32),
        scratch_shapes=[pltpu.VMEM((nseq * (POOL_BUF + 1), D_MODEL), F32)],
        compiler_params=_cparams(("parallel", "arbitrary")),
        name="pool_state" if has_state else "pool_fresh",
    )(*args)


def _post_kernel(pa_ref, pb_ref, oa_ref, ob_ref, ga_ref, gb_ref, xa_ref, xb_ref, wp_ref, ps_ref, wbb_ref, wba_ref,
                 wo_ref, g_ref, b_ref, h_ref, hp_ref, *, alpha, tiles_a):
    first = pl.program_id(0) < tiles_a
    x = jnp.where(first, xa_ref[...], xb_ref[...])
    pooled = jnp.where(first, pa_ref[...], pb_ref[...])
    o_gated = jnp.where(first, oa_ref[...], ob_ref[...])
    mixed = jnp.concatenate(
        [_dot(pooled[:, gi * POOL_GROUP_W:(gi + 1) * POOL_GROUP_W], wp_ref[gi]) for gi in range(len(POOL_WINDOWS))],
        axis=1) * ps_ref[...]
    branch_b = _dot(mixed, wbb_ref[...])
    branch_a = _dot(o_gated, wba_ref[...])
    merged = _sigmoid(ga_ref[...].astype(F32)) * branch_a + _sigmoid(gb_ref[...].astype(F32)) * branch_b
    h = _layer_norm(alpha * x + _dot(merged, wo_ref[...]), g_ref[...], b_ref[...])
    h_ref[...] = h
    _store_pieces(hp_ref, h)


def _post(pooled_a, pooled_b, o_a, o_b, proj, x_a, x_b, w_pool, pool_scale, w_br_b, w_br_a, w_out, ln_g, ln_b,
          alpha, tm):
    t = x_a.shape[0] + x_b.shape[0]
    tiles_a = x_a.shape[0] // tm
    row = lambda i: (i, 0)
    row_a = lambda i: (jnp.minimum(i, tiles_a - 1), 0)
    row_b = lambda i: (jnp.maximum(i - tiles_a, 0), 0)
    const2 = lambda i: (0, 0)
    return pl.pallas_call(
        functools.partial(_post_kernel, alpha=alpha, tiles_a=tiles_a),
        grid=(t // tm,),
        in_specs=[pl.BlockSpec((tm, D_MODEL), row_a),
                  pl.BlockSpec((tm, D_MODEL), row_b),
                  pl.BlockSpec((tm, V_W), row_a),
                  pl.BlockSpec((tm, V_W), row_b),
                  pl.BlockSpec((tm, D_MODEL), lambda i: (i, U_BLK + 1)),
                  pl.BlockSpec((tm, D_MODEL), lambda i: (i, U_BLK + 2)),
                  pl.BlockSpec((tm, D_MODEL), row_a),
                  pl.BlockSpec((tm, D_MODEL), row_b),
                  pl.BlockSpec(w_pool.shape, lambda i: (0, 0, 0)),
                  pl.BlockSpec((1, D_MODEL), const2),
                  pl.BlockSpec(w_br_b.shape, const2),
                  pl.BlockSpec(w_br_a.shape, const2),
                  pl.BlockSpec(w_out.shape, const2),
                  pl.BlockSpec((1, D_MODEL), const2),
                  pl.BlockSpec((1, D_MODEL), const2)],
        out_specs=[pl.BlockSpec((tm, D_MODEL), row), pl.BlockSpec((N_PIECE, tm, PIECE_W), lambda i: (0, i, 0))],
        out_shape=[jax.ShapeDtypeStruct((t, D_MODEL), F32),
                   jax.ShapeDtypeStruct((N_PIECE, t, PIECE_W), jnp.uint32)],
        compiler_params=_cparams(("parallel",)),
        name="post_mixers",
    )(pooled_a, pooled_b, o_a, o_b, proj, proj, x_a, x_b, w_pool, pool_scale, w_br_b, w_br_a, w_out, ln_g, ln_b)


def _router_kernel(h_ref, wt_ref, bias_ref, idx_ref, wts_ref, rank_ref, cnt_ref, carry_scr, *, tm):
    i = pl.program_id(0)

    @pl.when(i == 0)
    def _init():
        carry_scr[...] = jnp.zeros(carry_scr.shape, F32)

    logits = _dot3_nt(wt_ref[...], h_ref[...])
    sc = _sigmoid(logits)
    ch = sc + bias_ref[...]
    neg = -jnp.inf
    e_in = lax.broadcasted_iota(jnp.int32, (GROUP_SZ, tm), 0)
    gs_rows = []
    for g in range(N_GROUP):
        blk = ch[g * GROUP_SZ:(g + 1) * GROUP_SZ]
        m1 = jnp.max(blk, axis=0, keepdims=True)
        i1 = jnp.min(jnp.where(blk == m1, e_in, GROUP_SZ), axis=0, keepdims=True)
        m2 = jnp.max(jnp.where(e_in == i1, neg, blk), axis=0, keepdims=True)
        gs_rows.append(m1 + m2)
    gs = jnp.concatenate(gs_rows, axis=0)
    g_io = lax.broadcasted_iota(jnp.int32, (N_GROUP, tm), 0)
    e_io = lax.broadcasted_iota(jnp.int32, (N_EXPERTS, tm), 0)
    e_grp = e_io >> int(math.log2(GROUP_SZ))
    masked = jnp.full((N_EXPERTS, tm), neg, F32)
    for _ in range(TOPK_GROUP):
        mx = jnp.max(gs, axis=0, keepdims=True)
        gi = jnp.min(jnp.where(gs == mx, g_io, N_GROUP), axis=0, keepdims=True)
        gs = jnp.where(g_io == gi, neg, gs)
        masked = jnp.where(e_grp == gi, ch, masked)
    idx_rows, w_rows = [], []
    onehot = jnp.zeros((N_EXPERTS, tm), F32)
    for _ in range(TOP_K):
        mx = jnp.max(masked, axis=0, keepdims=True)
        ei = jnp.min(jnp.where(masked == mx, e_io, N_EXPERTS), axis=0, keepdims=True)
        hit = e_io == ei
        idx_rows.append(ei)
        w_rows.append(jnp.sum(jnp.where(hit, sc, 0.0), axis=0, keepdims=True))
        onehot = jnp.where(hit, 1.0, onehot)
        masked = jnp.where(hit, neg, masked)
    wsel = jnp.concatenate(w_rows, axis=0)
    wts_ref[...] = wsel / jnp.sum(wsel, axis=0, keepdims=True) * ROUTED_SCALE
    idx_ref[...] = jnp.concatenate(idx_rows, axis=0)

    tr = lax.broadcasted_iota(jnp.int32, (tm, tm), 0)
    tc_ = lax.broadcasted_iota(jnp.int32, (tm, tm), 1)
    before = jnp.where(tr < tc_, 1.0, 0.0).astype(BF16)
    cum = jnp.dot(onehot.astype(BF16), before, preferred_element_type=F32) + carry_scr[...]
    rank_ref[...] = jnp.concatenate(
        [jnp.sum(jnp.where(e_io == ei, cum, 0.0), axis=0, keepdims=True) for ei in idx_rows],
        axis=0).astype(jnp.int32)
    carry_scr[...] = carry_scr[...] + jnp.sum(onehot, axis=1, keepdims=True)
    cnt_ref[...] = carry_scr[...]


def _router(h, w_router_t, bias_col, tm):
    t = h.shape[0]
    return pl.pallas_call(
        functools.partial(_router_kernel, tm=tm),
        grid=(t // tm,),
        in_specs=[pl.BlockSpec((tm, D_MODEL), lambda i: (i, 0)),
                  pl.BlockSpec((N_EXPERTS, D_MODEL), lambda i: (0, 0)),
                  pl.BlockSpec((N_EXPERTS, 1), lambda i: (0, 0))],
        out_specs=[pl.BlockSpec((TOP_K, tm), lambda i: (0, i)),
                   pl.BlockSpec((TOP_K, tm), lambda i: (0, i)),
                   pl.BlockSpec((TOP_K, tm), lambda i: (0, i)),
                   pl.BlockSpec((N_EXPERTS, 1), lambda i: (0, 0))],
        out_shape=[jax.ShapeDtypeStruct((TOP_K, t), jnp.int32),
                   jax.ShapeDtypeStruct((TOP_K, t), F32),
                   jax.ShapeDtypeStruct((TOP_K, t), jnp.int32),
                   jax.ShapeDtypeStruct((N_EXPERTS, 1), F32)],
        scratch_shapes=[pltpu.VMEM((N_EXPERTS, 1), F32)],
        compiler_params=_cparams(("arbitrary",)),
        name="router",
    )(h, w_router_t, bias_col)


def _dest_kernel(idx_ref, rank_ref, start_ref, dest_ref, *, tm, n_rows):
    e_io = lax.broadcasted_iota(jnp.int32, (N_EXPERTS, tm), 0)
    starts = start_ref[...]
    rows = []
    for k in range(TOP_K):
        seg = jnp.sum(jnp.where(e_io == idx_ref[k:k + 1, :], starts, 0.0), axis=0, keepdims=True)
        rows.append(seg.astype(jnp.int32) + rank_ref[k:k + 1, :])
    base = jnp.concatenate(rows, axis=0)
    for p in range(N_PIECE):
        dest_ref[p] = base + p * n_rows


def _dest(idx_t, rank_t, seg_start_col, n_rows, tm):
    t = idx_t.shape[1]
    blk = pl.BlockSpec((TOP_K, tm), lambda i: (0, i))
    return pl.pallas_call(
        functools.partial(_dest_kernel, tm=tm, n_rows=n_rows),
        grid=(t // tm,),
        in_specs=[blk, blk, pl.BlockSpec((N_EXPERTS, 1), lambda i: (0, 0))],
        out_specs=pl.BlockSpec((N_PIECE, TOP_K, tm), lambda i: (0, 0, i)),
        out_shape=jax.ShapeDtypeStruct((N_PIECE, TOP_K, t), jnp.int32),
        compiler_params=_cparams(("parallel",)),
        name="dispatch_rows",
    )(idx_t, rank_t, seg_start_col)


def _expert_kernel(be_ref, nu_ref, first_ref, nxt_ref, slot_ref, x_ref, wg_hbm, wu_hbm, wd_hbm, y_ref,
                   wg_buf, wu_buf, wd_buf, wgu_scr, wd_scr, sems):
    i = pl.program_id(0)
    live = i < nu_ref[0]

    def fetch(e, slot):
        return [pltpu.make_async_copy(wg_hbm.at[e], wg_buf.at[slot], sems.at[slot, 0]),
                pltpu.make_async_copy(wu_hbm.at[e], wu_buf.at[slot], sems.at[slot, 1]),
                pltpu.make_async_copy(wd_hbm.at[e], wd_buf.at[slot], sems.at[slot, 2])]

    @pl.when(live & (first_ref[i] == 1))
    def _new_expert():
        slot = slot_ref[i]

        @pl.when(i == 0)
        def _():
            for cp in fetch(be_ref[i], slot):
                cp.start()

        @pl.when(nxt_ref[i] >= 0)
        def _():
            for cp in fetch(nxt_ref[i], 1 - slot):
                cp.start()

        for cp in fetch(be_ref[i], slot):
            cp.wait()
        wgu_scr[:, :D_EXPERT] = wg_buf[slot].astype(BF16)
        wgu_scr[:, D_EXPERT:] = wu_buf[slot].astype(BF16)
        wd_scr[...] = wd_buf[slot].astype(BF16)

    @pl.when(live)
    def _():
        n_sub = 2
        sub = EXPERT_BM // n_sub
        xs = [jnp.concatenate([c.astype(BF16) for c in
                               _load_pieces([x_ref[p, s * sub:(s + 1) * sub] for p in range(N_PIECE)])], axis=1)
              for s in range(n_sub)]
        gus = [jnp.dot(x, wgu_scr[...], preferred_element_type=F32) for x in xs]
        acts = [(_silu(gu[:, :D_EXPERT]) * gu[:, D_EXPERT:]).astype(BF16) for gu in gus]
        ys = [jnp.dot(a, wd_scr[...], preferred_element_type=F32) for a in acts]
        for s, y in enumerate(ys):
            for p in range(N_PIECE):
                y_ref[p, s * sub:(s + 1) * sub] = _pack_halves(y[:, 2 * p * PIECE_W:2 * (p + 1) * PIECE_W])


def _experts(xs, blk_exp, n_used, blk_first, blk_next, blk_slot, w_gate, w_up, w_down):
    n_rows = xs.shape[1]
    n_blocks = n_rows // EXPERT_BM

    def row_block(i, be, nu, *_):
        return (0, jnp.minimum(i, nu[0] - 1), 0)

    grid_spec = pltpu.PrefetchScalarGridSpec(
        num_scalar_prefetch=5,
        grid=(n_blocks,),
        in_specs=[pl.BlockSpec((N_PIECE, EXPERT_BM, PIECE_W), row_block),
                  pl.BlockSpec(memory_space=pl.ANY),
                  pl.BlockSpec(memory_space=pl.ANY),
                  pl.BlockSpec(memory_space=pl.ANY)],
        out_specs=pl.BlockSpec((N_PIECE, EXPERT_BM, PIECE_W), row_block),
        scratch_shapes=[pltpu.VMEM((2, D_MODEL, D_EXPERT), F32),
                        pltpu.VMEM((2, D_MODEL, D_EXPERT), F32),
                        pltpu.VMEM((2, D_EXPERT, D_MODEL), F32),
                        pltpu.VMEM((D_MODEL, 2 * D_EXPERT), BF16),
                        pltpu.VMEM((D_EXPERT, D_MODEL), BF16),
                        pltpu.SemaphoreType.DMA((2, 3))],
    )
    return pl.pallas_call(
        _expert_kernel,
        grid_spec=grid_spec,
        out_shape=jax.ShapeDtypeStruct((N_PIECE, n_rows, PIECE_W), jnp.uint32),
        compiler_params=_cparams(("arbitrary",)),
        name="experts",
    )(blk_exp, n_used, blk_first, blk_next, blk_slot, xs, w_gate, w_up, w_down)


SC_WINDOW = 128
V7X_SC_CORES = 2
V7X_SC_SUBCORES = 16


def _sc_mesh():
    return plsc.VectorSubcoreMesh(core_axis_name="core", subcore_axis_name="subcore",
                                  num_cores=V7X_SC_CORES, num_subcores=V7X_SC_SUBCORES)


def _sc_scatter_rows(src, dest, n_rows, seg, repeat):
    d = src.shape[1]
    n_idx = dest.shape[0]
    seg_blocks = seg // SC_WINDOW
    dest2 = dest.reshape(1, n_idx)

    def src_block(i):
        return ((i // (repeat * seg_blocks)) * seg_blocks + i % seg_blocks, 0)

    @functools.partial(pl.kernel, out_type=jax.ShapeDtypeStruct((n_rows, d), src.dtype), mesh=_sc_mesh(),
                       scratch_types=[], name="sc_dispatch")
    def run(src_hbm, idx_hbm, out_hbm):
        def body(rows_vmem, idx_vmem):
            pltpu.sync_copy(rows_vmem, out_hbm.at[idx_vmem.at[0]])

        pltpu.emit_pipeline(
            body,
            grid=(n_idx // SC_WINDOW,),
            in_specs=[pl.BlockSpec((SC_WINDOW, d), src_block),
                      pl.BlockSpec((1, SC_WINDOW), lambda i: (0, i))],
            out_specs=[],
            core_axis_name=("core", "subcore"),
            dimension_semantics=(pltpu.PARALLEL,),
        )(src_hbm, idx_hbm)

    return run(src, dest2)


def _sc_gather_rows(table, idx):
    d = table.shape[1]
    n_idx = idx.shape[0]
    idx2 = idx.reshape(1, n_idx)

    @functools.partial(pl.kernel, out_type=jax.ShapeDtypeStruct((n_idx, d), table.dtype), mesh=_sc_mesh(),
                       scratch_types=[], name="sc_combine_gather")
    def run(table_hbm, idx_hbm, out_hbm):
        def body(idx_vmem, rows_vmem):
            pltpu.sync_copy(table_hbm.at[idx_vmem.at[0]], rows_vmem)

        pltpu.emit_pipeline(
            body,
            grid=(n_idx // SC_WINDOW,),
            in_specs=[pl.BlockSpec((1, SC_WINDOW), lambda i: (0, i))],
            out_specs=[pl.BlockSpec((SC_WINDOW, d), lambda i: (i, 0))],
            core_axis_name=("core", "subcore"),
            dimension_semantics=(pltpu.PARALLEL,),
        )(idx_hbm, out_hbm)

    return run(table, idx2)


def _combine_kernel(yg_ref, wts_ref, h_ref, wgu_ref, wd_ref, g_ref, b_ref, outa_ref, outb_ref, *, alpha, tiles_a):
    wts = wts_ref[...]
    acc = None
    for k in range(TOP_K):
        cols = _load_pieces([yg_ref[p, k] for p in range(N_PIECE)])
        wk = wts[:, k:k + 1]
        acc = [c * wk for c in cols] if acc is None else [a + c * wk for a, c in zip(acc, cols)]
    routed = jnp.concatenate(acc, axis=1)
    h = h_ref[...]
    gu = _dot(h, wgu_ref[...])
    shared = _dot(_silu(gu[:, :D_EXPERT]) * gu[:, D_EXPERT:], wd_ref[...])
    y = _layer_norm(alpha * h + (routed + shared), g_ref[...], b_ref[...])
    i = pl.program_id(0)

    @pl.when(i < tiles_a)
    def _():
        outa_ref[...] = y

    @pl.when(i >= tiles_a)
    def _():
        outb_ref[...] = y


def _combine(yg, wts, h, w_sh_gu, w_sh_down, ln_g, ln_b, alpha, tm, t_a):
    t = h.shape[0]
    tiles_a = t_a // tm
    row = lambda i: (i, 0)
    const2 = lambda i: (0, 0)
    return pl.pallas_call(
        functools.partial(_combine_kernel, alpha=alpha, tiles_a=tiles_a),
        grid=(t // tm,),
        in_specs=[pl.BlockSpec((N_PIECE, TOP_K, tm, PIECE_W), lambda i: (0, 0, i, 0)),
                  pl.BlockSpec((tm, TOP_K), row),
                  pl.BlockSpec((tm, D_MODEL), row),
                  pl.BlockSpec(w_sh_gu.shape, const2),
                  pl.BlockSpec(w_sh_down.shape, const2),
                  pl.BlockSpec((1, D_MODEL), const2),
                  pl.BlockSpec((1, D_MODEL), const2)],
        out_specs=[pl.BlockSpec((tm, D_MODEL), lambda i: (jnp.minimum(i, tiles_a - 1), 0)),
                   pl.BlockSpec((tm, D_MODEL), lambda i: (jnp.maximum(i - tiles_a, 0), 0))],
        out_shape=[jax.ShapeDtypeStruct((t_a, D_MODEL), F32), jax.ShapeDtypeStruct((t - t_a, D_MODEL), F32)],
        compiler_params=_cparams(("arbitrary",)),
        name="combine_ln2",
    )(yg, wts, h, w_sh_gu, w_sh_down, ln_g, ln_b)


def _layer(xp, xs, s_delta, s_conv, s_pool, w_in, conv_w, a_log, dt_bias, gamma_a, w_br_a, w_pool,
           pool_scale, w_br_b, w_out, ln1_g, ln1_b, w_router, router_bias, w_exp_gate, w_exp_up,
           w_exp_down, w_sh_gate, w_sh_up, w_sh_down, ln2_g, ln2_b, alpha):
    Bp, Lp, _ = xp.shape
    Bs, Ls, _ = xs.shape
    Tp, Ts = Bp * Lp, Bs * Ls
    T = Tp + Ts
    x_p = xp.reshape(Tp, D_MODEL)
    x_s = xs.reshape(Ts, D_MODEL)

    o_z, o_b, o_a, o_u = QKV_W, QKV_W + V_W, QKV_W + V_W + H_V, QKV_W + V_W + 2 * H_V
    w_main = jnp.concatenate([w_in[:, :o_b], w_in[:, o_u:]], axis=1).astype(BF16)
    w_ba = jnp.pad(w_in[:, o_b:o_u], ((0, 0), (0, 128 - 2 * H_V))).astype(BF16)
    proj, ba = _in_proj(x_p, x_s, w_main, w_ba, 1024, MAIN_W // 3)

    o_p, sd_p = _gdn(proj, ba, conv_w, a_log, dt_bias, gamma_a, 0, Bp, Lp, 64, 1)
    o_s, sd_s = _gdn(proj, ba, conv_w, a_log, dt_bias, gamma_a, Tp, Bs, Ls, Ls, GDN_STATE_SEQS,
                     conv_buf=s_conv, s0=s_delta)

    pooled_p = _pool(proj, 0, Bp, Lp, 256, 1, 0)
    pooled_s = _pool(proj, Tp, Bs, Ls, Ls, POOL_STATE_SEQS, PAST_LEN, pool_buf=s_pool)

    h, hp = _post(pooled_p, pooled_s, o_p, o_s, proj, x_p, x_s, w_pool.astype(BF16), pool_scale.reshape(1, D_MODEL),
                  w_br_b.astype(BF16), w_br_a.astype(BF16), w_out.astype(BF16),
                  ln1_g.reshape(1, D_MODEL), ln1_b.reshape(1, D_MODEL), alpha, 256)

    idx_t, wts_t, rank_t, cnt = _router(h, w_router.T, router_bias.reshape(N_EXPERTS, 1), 512)
    counts = cnt[:, 0].astype(jnp.int32)
    padded = ((counts + EXPERT_BM - 1) // EXPERT_BM) * EXPERT_BM
    pends = jnp.cumsum(padded)
    pstarts = pends - padded
    n_blocks = (T * TOP_K + N_EXPERTS * (EXPERT_BM - 1) + EXPERT_BM - 1) // EXPERT_BM
    n_rows = n_blocks * EXPERT_BM
    piece_rows = _dest(idx_t, rank_t, pstarts.astype(F32).reshape(N_EXPERTS, 1), n_rows, 512)
    blk_start = jnp.arange(n_blocks, dtype=jnp.int32) * EXPERT_BM
    blk_exp = jnp.minimum(jnp.sum((pends[None, :] <= blk_start[:, None]).astype(jnp.int32), axis=1),
                          N_EXPERTS - 1)
    n_used = (pends[-1:] // EXPERT_BM).astype(jnp.int32)
    has_rows = counts > 0
    e_ids = jnp.arange(N_EXPERTS, dtype=jnp.int32)
    slot_of = (jnp.cumsum(has_rows.astype(jnp.int32)) - 1) & 1
    later = jnp.where(has_rows, e_ids, N_EXPERTS)
    next_of = jnp.concatenate([lax.cummin(later, reverse=True)[1:], jnp.full((1,), N_EXPERTS, jnp.int32)])
    next_of = jnp.where(next_of < N_EXPERTS, next_of, -1)
    blk_first = (blk_start == pstarts[blk_exp]).astype(jnp.int32)
    blk_next = next_of[blk_exp]
    blk_slot = slot_of[blk_exp]

    piece_idx = piece_rows.reshape(N_PIECE * TOP_K * T)
    x_sorted = _sc_scatter_rows(hp.reshape(N_PIECE * T, PIECE_W), piece_idx, N_PIECE * n_rows, T, TOP_K)
    y_sorted = _experts(x_sorted.reshape(N_PIECE, n_rows, PIECE_W), blk_exp, n_used, blk_first, blk_next,
                        blk_slot, w_exp_gate, w_exp_up, w_exp_down)
    yg = _sc_gather_rows(y_sorted.reshape(N_PIECE * n_rows, PIECE_W), piece_idx)
    yg = yg.reshape(N_PIECE, TOP_K, T, PIECE_W)

    w_sh_gu = jnp.concatenate([w_sh_gate, w_sh_up], axis=1).astype(BF16)
    y_p, y_s = _combine(yg, wts_t.T, h, w_sh_gu, w_sh_down.astype(BF16),
                        ln2_g.reshape(1, D_MODEL), ln2_b.reshape(1, D_MODEL), alpha, 512, Tp)

    def carried_rows(keep, col0, width, name, old_p, old_s):
        parts, counts = [], []
        for x2d, B, L in ((x_p, Bp, Lp), (x_s, Bs, Ls)):
            n = min(L, keep)
            rows = jnp.arange(B, dtype=jnp.int32)[:, None] * L + (L - n) + jnp.arange(n, dtype=jnp.int32)
            parts.append(jnp.take(x2d, rows.reshape(-1), axis=0).astype(BF16))
            counts.append((B, n))
        n_tail = sum(B * n for B, n in counts)
        n_pad = -(-n_tail // 128) * 128
        xt = jnp.pad(jnp.concatenate(parts, axis=0), ((0, n_pad - n_tail), (0, 0)))
        tail = _matmul(xt, w_main, col0, width, n_pad, 1024, F32, name)
        outs, r0 = [], 0
        for (B, n), old in zip(counts, (old_p, old_s)):
            new = tail[r0:r0 + B * n].reshape(B, n, width)
            outs.append(new if n == keep else jnp.concatenate([old[:, n:], new], axis=1))
            r0 += B * n
        return outs

    new_conv_p, new_conv_s = carried_rows(CONV_W - 1, 0, QKV_W, "in_proj_conv_rows",
                                          jnp.zeros((Bp, CONV_W - 1, QKV_W), F32), s_conv)
    new_pool_p, new_pool_s = carried_rows(POOL_BUF, U_BLK * D_MODEL, D_MODEL, "in_proj_pool_rows",
                                          jnp.zeros((Bp, POOL_BUF, D_MODEL), F32), s_pool)
    return (y_p.reshape(Bp, Lp, D_MODEL), y_s.reshape(Bs, Ls, D_MODEL),
            sd_p, new_conv_p, new_pool_p, sd_s, new_conv_s, new_pool_s)


def kernel(x_prompt, x_sample, state_delta, state_conv, state_pool, w_in, conv_w, a_log, dt_bias, gamma_a,
           w_br_a, w_pool, pool_scale, w_br_b, w_out, ln1_g, ln1_b, w_router, router_bias,
           w_exp_gate, w_exp_up, w_exp_down, w_sh_gate, w_sh_up, w_sh_down, ln2_g, ln2_b):
    depth = w_in.shape[0]
    alpha = (2 * depth) ** 0.25
    yp, ys = x_prompt, x_sample
    outs = [[] for _ in range(6)]
    for l in range(depth):
        res = _layer(yp, ys, state_delta[l], state_conv[l], state_pool[l], w_in[l], conv_w[l], a_log[l],
                     dt_bias[l], gamma_a[l], w_br_a[l], w_pool[l], pool_scale[l], w_br_b[l], w_out[l],
                     ln1_g[l], ln1_b[l], w_router[l], router_bias[l], w_exp_gate[l], w_exp_up[l],
                     w_exp_down[l], w_sh_gate[l], w_sh_up[l], w_sh_down[l], ln2_g[l], ln2_b[l], alpha)
        yp, ys = res[0], res[1]
        for lst, v in zip(outs, res[2:]):
            lst.append(v)
    return (yp, ys) + tuple(jnp.stack(v) for v in outs)
```

```python
import functools
import math

import jax
import jax.numpy as jnp
from jax import lax
from jax.experimental import pallas as pl
from jax.experimental.pallas import tpu as pltpu
from jax.experimental.pallas import tpu_sc as plsc

F32 = jnp.float32
BF16 = jnp.bfloat16

D_MODEL = 1024
H_K = 8
D_K = 128
H_V = 16
D_V = 128
Q_W = H_K * D_K
V_W = H_V * D_V
QKV_W = 2 * Q_W + V_W
CONV_W = 4
POOL_WINDOWS = (2, 4, 8, 16)
POOL_GROUP_W = D_MODEL // len(POOL_WINDOWS)
POOL_BUF = max(POOL_WINDOWS) - 1
N_EXPERTS = 256
TOP_K = 8
N_GROUP = 8
TOPK_GROUP = 4
GROUP_SZ = N_EXPERTS // N_GROUP
D_EXPERT = D_MODEL // 4
ROUTED_SCALE = 2.5
LN_EPS = 1e-5
NORM_EPS = 1e-6
PAST_LEN = 16384

MAIN_W = QKV_W + V_W + 3 * D_MODEL
Z_BLK = QKV_W // V_W
U_BLK = (QKV_W + V_W) // D_MODEL

STACK_ROWS = 128
SOLVE_BASE = 16
GDN_FRESH_CHUNKS = 2
EXPERT_BM = 512
GDN_STATE_SEQS = 4
POOL_STATE_SEQS = 16
VMEM_LIMIT = 56 * 1024 * 1024


def _cparams(sem):
    return pltpu.CompilerParams(dimension_semantics=sem, vmem_limit_bytes=VMEM_LIMIT)


def _sigmoid(x):
    return 0.5 * jnp.tanh(0.5 * x) + 0.5


def _silu(x):
    return x * _sigmoid(x)


def _softplus(x):
    return jnp.maximum(x, 0.0) + jnp.log(1.0 + jnp.exp(-jnp.abs(x)))


def _dot(a, b):
    return jnp.dot(a.astype(BF16), b.astype(BF16), preferred_element_type=F32)


def _dot_nt(a, b):
    return lax.dot_general(a.astype(BF16), b.astype(BF16), (((1,), (1,)), ((), ())),
                           preferred_element_type=F32)


def _dot_tn(a, b):
    return lax.dot_general(a.astype(BF16), b.astype(BF16), (((0,), (0,)), ((), ())),
                           preferred_element_type=F32)


def _split(a):
    hi = a.astype(BF16)
    lo = (a - hi.astype(F32)).astype(BF16)
    return hi, lo


def _dot3_nt(a, b):
    ah, al = _split(a)
    bh, bl = _split(b)
    d = functools.partial(lax.dot_general, dimension_numbers=(((1,), (1,)), ((), ())),
                          preferred_element_type=F32)
    return d(ah, bh) + (d(ah, bl) + d(al, bh))


def _pack_halves(x):
    n = x.shape[1] // 2
    hi = lax.bitcast_convert_type(x[:, :n].astype(BF16).astype(F32), jnp.uint32)
    lo = lax.bitcast_convert_type(x[:, n:].astype(BF16).astype(F32), jnp.uint32)
    return (hi & jnp.uint32(0xFFFF0000)) | (lo >> 16)


def _unpack_halves(w):
    hi = lax.bitcast_convert_type(w & jnp.uint32(0xFFFF0000), F32)
    lo = lax.bitcast_convert_type(w << 16, F32)
    return hi, lo


N_PIECE = 2
PIECE_W = D_MODEL // (2 * N_PIECE)


def _store_pieces(ref, x):
    for p in range(N_PIECE):
        ref[p] = _pack_halves(x[:, 2 * p * PIECE_W:2 * (p + 1) * PIECE_W])


def _load_pieces(pieces):
    cols = []
    for w in pieces:
        cols.extend(_unpack_halves(w))
    return cols


def _layer_norm(x, g, b):
    mu = jnp.mean(x, axis=-1, keepdims=True)
    xc = x - mu
    var = jnp.mean(xc * xc, axis=-1, keepdims=True)
    return xc * lax.rsqrt(var + LN_EPS) * g + b


def _in_proj_kernel(xa_ref, xb_ref, w_ref, wba_ref, o_ref, ba_ref, x_scr, *, tiles_a):
    @pl.when(pl.program_id(1) == 0)
    def _():
        x = jnp.where(pl.program_id(0) < tiles_a, xa_ref[...], xb_ref[...]).astype(BF16)
        x_scr[...] = x
        ba_ref[...] = jnp.dot(x, wba_ref[...], preferred_element_type=F32)

    o_ref[...] = jnp.dot(x_scr[...], w_ref[...], preferred_element_type=F32).astype(o_ref.dtype)


def _in_proj(x_a, x_b, w_main, w_ba, tm, tn):
    k = x_a.shape[1]
    t = x_a.shape[0] + x_b.shape[0]
    n = w_main.shape[1]
    tiles_a = x_a.shape[0] // tm
    return pl.pallas_call(
        functools.partial(_in_proj_kernel, tiles_a=tiles_a),
        grid=(t // tm, n // tn),
        in_specs=[pl.BlockSpec((tm, k), lambda i, j: (jnp.minimum(i, tiles_a - 1), 0)),
                  pl.BlockSpec((tm, k), lambda i, j: (jnp.maximum(i - tiles_a, 0), 0)),
                  pl.BlockSpec((k, tn), lambda i, j: (0, j)),
                  pl.BlockSpec(w_ba.shape, lambda i, j: (0, 0))],
        out_specs=[pl.BlockSpec((tm, tn), lambda i, j: (i, j)),
                   pl.BlockSpec((tm, w_ba.shape[1]), lambda i, j: (i, 0))],
        out_shape=[jax.ShapeDtypeStruct((t, n), BF16), jax.ShapeDtypeStruct((t, w_ba.shape[1]), F32)],
        scratch_shapes=[pltpu.VMEM((tm, k), BF16)],
        compiler_params=_cparams(("parallel", "arbitrary")),
        name="in_proj",
    )(x_a, x_b, w_main, w_ba)


def _mm_kernel(x_ref, w_ref, o_ref):
    o_ref[...] = jnp.dot(x_ref[...], w_ref[...], preferred_element_type=F32).astype(o_ref.dtype)


def _matmul(x, w, col0, n, tm, tn, out_dtype, name):
    t, k = x.shape
    cb0 = col0 // tn
    return pl.pallas_call(
        _mm_kernel,
        grid=(t // tm, n // tn),
        in_specs=[pl.BlockSpec((tm, k), lambda i, j: (i, 0)),
                  pl.BlockSpec((k, tn), lambda i, j: (0, cb0 + j))],
        out_specs=pl.BlockSpec((tm, tn), lambda i, j: (i, j)),
        out_shape=jax.ShapeDtypeStruct((t, n), out_dtype),
        compiler_params=_cparams(("parallel", "parallel")),
        name=name,
    )(x, w)


def _gdn_kernel(*refs, C, hg, nseq, nck, has_state):
    if has_state:
        (qkv_ref, z_ref, ba_ref, cw_ref, ad_ref, gam_ref, buf_ref, s0_ref,
         o_ref, sout_ref, s_scr, tail_scr, act_scr) = refs
    else:
        (qkv_ref, z_ref, ba_ref, cw_ref, ad_ref, gam_ref,
         o_ref, sout_ref, s_scr, tail_scr, act_scr) = refs
    R = hg * C
    ng = H_V // hg
    log2c = int(math.log2(C))
    c = pl.program_id(1)

    @pl.when(c == 0)
    def _init():
        if has_state:
            tail_scr[...] = jnp.zeros(tail_scr.shape, F32)
            for s in range(nseq):
                s_scr[s * H_V:(s + 1) * H_V] = s0_ref[s]
                tail_scr[(s + 1) * 8 - (CONV_W - 1):(s + 1) * 8] = buf_ref[s]
        else:
            s_scr[...] = jnp.zeros(s_scr.shape, F32)
            tail_scr[...] = jnp.zeros(tail_scr.shape, F32)

    cw = cw_ref[...]
    qkv_all = qkv_ref[...].astype(F32)
    n_prev = CONV_W - 1
    CB = nck * C
    for s in range(nseq):
        x = qkv_all[s * CB:(s + 1) * CB]
        xe = jnp.concatenate([tail_scr[s * 8:(s + 1) * 8], x], axis=0)
        y = x * cw[n_prev:CONV_W]
        for j in range(n_prev):
            y = y + xe[8 - n_prev + j:8 - n_prev + j + CB] * cw[j:j + 1]
        tail_scr[s * 8:(s + 1) * 8] = x[CB - 8:CB]
        act_scr[s * CB:(s + 1) * CB] = _silu(y)
        for off, scale in ((0, D_K ** -0.5), (Q_W, 1.0)):
            for kh in range(H_K):
                cols = slice(off + kh * D_K, off + (kh + 1) * D_K)
                v = act_scr[s * CB:(s + 1) * CB, cols]
                act_scr[s * CB:(s + 1) * CB, cols] = v * (
                    lax.rsqrt(jnp.sum(v * v, axis=-1, keepdims=True) + NORM_EPS) * scale)

    ri = lax.broadcasted_iota(jnp.int32, (R, R), 0)
    ci = lax.broadcasted_iota(jnp.int32, (R, R), 1)
    same = (ri >> log2c) == (ci >> log2c)
    incl = same & (ri >= ci)
    strict = same & (ri > ci)
    eye = ri == ci
    is_last = ci == (((ri >> log2c) << log2c) + (C - 1))
    eye_f = jnp.where(eye, 1.0, 0.0).astype(F32)
    cum_u = jnp.where(same & (ri <= ci), 1.0, 0.0).astype(BF16)

    nrow = ba_ref.shape[2] // 2
    adv = ad_ref[...]
    dd = functools.partial(jnp.dot, preferred_element_type=F32)
    beta_of, G_of = {}, {}
    for s in range(nseq):
        for k in range(nck):
            bav = ba_ref[s, k]
            beta_of[s, k] = _sigmoid(bav[0:nrow])
            g_rows = -jnp.exp(adv[0:nrow]) * _softplus(bav[nrow:2 * nrow] + adv[nrow:2 * nrow])
            g1 = g_rows.astype(BF16)
            r1 = g_rows - g1.astype(F32)
            g2 = r1.astype(BF16)
            g3 = (r1 - g2.astype(F32)).astype(BF16)
            G_of[s, k] = dd(g1, cum_u) + (dd(g2, cum_u) + dd(g3, cum_u))

    gam = gam_ref[...]
    rep = H_V // H_K

    def stacked(d, off, per_k_head):
        blocks = [(h // rep if per_k_head else h) for h in d["heads"]]
        return jnp.concatenate([act_scr[d["rows"], off + b * D_K:off + (b + 1) * D_K] for b in blocks], axis=0)

    log2b = min(log2c, int(math.log2(SOLVE_BASE)))
    same_base = ((ri ^ ci) >> log2b) == 0
    row_head = lax.broadcasted_iota(jnp.int32, (R, D_V), 0) >> log2c
    z_all = z_ref[...].astype(F32)
    grp = [dict(s=s, k=k, g=g, rows=slice(s * CB + k * C, s * CB + (k + 1) * C), s0=s * H_V,
                heads=[g * hg + hh for hh in range(hg)])
           for k in range(nck) for s in range(nseq) for g in range(ng)]
    for d in grp:
        kst = stacked(d, Q_W, True)
        d["kk"] = _dot_nt(kst, kst)
        d["qk"] = _dot_nt(stacked(d, 0, True), kst)
    for d in grp:
        grow = G_of[d["s"], d["k"]][d["g"]:d["g"] + 1]
        brow = beta_of[d["s"], d["k"]][d["g"]:d["g"] + 1]
        gcol = jnp.sum(eye_f * grow, axis=1, keepdims=True)
        bcol = jnp.sum(eye_f * brow, axis=1, keepdims=True)
        glast = jnp.sum(jnp.where(is_last, grow, 0.0), axis=1, keepdims=True)
        dm = jnp.where(incl, jnp.exp(jnp.minimum(gcol - grow, 0.0)), 0.0)
        d.update(gcol=gcol, bcol=bcol, glast=glast, eg=jnp.exp(gcol))
        d["m"] = jnp.where(strict, -(d["kk"] * bcol * dm), 0.0)
        d["attn"] = jnp.where(incl, d["qk"] * dm, 0.0)
        d["pw"] = jnp.where(same_base, d["m"], 0.0)
        d["t"] = eye_f + d["pw"]

    for r in range(log2b):
        last = r == log2b - 1
        for d in grp:
            if r == 0:
                if not last:
                    d["pw"] = _dot(d["pw"], d["pw"])
            elif last:
                d["t"] = d["t"] + _dot(d["pw"], d["t"])
            else:
                out = _dot(d["pw"], jnp.concatenate([d["pw"], d["t"]], axis=1))
                d["pw"] = out[:, :R]
                d["t"] = d["t"] + out[:, R:]
    for lev in range(log2b + 1, log2c + 1):
        coupling = ((ri ^ ci) >> (lev - 1)) == 1
        for d in grp:
            d["y"] = _dot(d["t"], jnp.where(coupling, d["m"], 0.0))
        for d in grp:
            d["t"] = d["t"] + _dot(d["y"], d["t"])
    for d in grp:
        rhs = jnp.concatenate([stacked(d, 2 * Q_W, False) * d["bcol"],
                               stacked(d, Q_W, True) * (d["bcol"] * d["eg"])], axis=1)
        d["x"] = _dot(d["t"], rhs)

    for k in range(nck):
        now = [d for d in grp if d["k"] == k]
        for d in now:
            qe = stacked(d, 0, True) * d["eg"]
            d["xs"] = [_dot(jnp.concatenate([d["x"][hh * C:(hh + 1) * C, D_V:], qe[hh * C:(hh + 1) * C]], axis=0),
                            s_scr[d["s0"] + h]) for hh, h in enumerate(d["heads"])]
        for d in now:
            d["vnew"] = jnp.concatenate(
                [d["x"][hh * C:(hh + 1) * C, :D_V] - xs[:C] for hh, xs in enumerate(d["xs"])], axis=0)
            d["o"] = jnp.concatenate([xs[C:] for xs in d["xs"]], axis=0) + _dot(d["attn"], d["vnew"])
        for d in now:
            kt = stacked(d, Q_W, True) * jnp.exp(d["glast"] - d["gcol"])
            egl = jnp.exp(d["glast"])
            for hh, h in enumerate(d["heads"]):
                sl = slice(hh * C, (hh + 1) * C)
                if C >= 16:
                    upd = _dot_tn(kt[sl], d["vnew"][sl])
                else:
                    upd = _dot_tn(kt, jnp.where(row_head == hh, d["vnew"], 0.0))
                s_scr[d["s0"] + h] = s_scr[d["s0"] + h] * egl[hh * C:hh * C + 1] + upd

    for d in grp:
        o = d["o"]
        zst = jnp.concatenate([z_all[d["rows"], h * D_V:(h + 1) * D_V] for h in d["heads"]], axis=0)
        on = o * lax.rsqrt(jnp.mean(o * o, axis=-1, keepdims=True) + NORM_EPS) * gam * _silu(zst)
        for hh, h in enumerate(d["heads"]):
            o_ref[d["rows"], h * D_V:(h + 1) * D_V] = on[hh * C:(hh + 1) * C].astype(o_ref.dtype)

    @pl.when(c == pl.num_programs(1) - 1)
    def _fin():
        for s in range(nseq):
            sout_ref[s] = s_scr[s * H_V:(s + 1) * H_V]


def _gdn(proj, ba, conv_w, a_log, dt_bias, gamma_a, row0, B, L, C, nseq, nck, conv_buf=None, s0=None):
    has_state = s0 is not None
    hg = STACK_ROWS // C
    ng = H_V // hg
    nrow = 16
    nc = L // C
    R = STACK_ROWS
    assert nseq == 1 or nc == nck
    rows_blk = nseq * nck * C
    steps = nc // nck

    def arrange(v):
        v = v.reshape(B, nc, C, ng, hg).transpose(0, 1, 3, 4, 2).reshape(B, nc, ng, R)
        return jnp.pad(v, ((0, 0), (0, 0), (0, nrow - ng), (0, 0)))

    rows = ba[row0:row0 + B * L]
    ba_arr = jnp.concatenate([arrange(rows[:, :H_V]), arrange(rows[:, H_V:2 * H_V])], axis=2)

    def arrange_param(p):
        v = jnp.broadcast_to(p.reshape(ng, hg, 1), (ng, hg, C)).reshape(ng, R)
        return jnp.pad(v, ((0, nrow - ng), (0, 0)))

    ad = jnp.concatenate([arrange_param(a_log), arrange_param(dt_bias)], axis=0)
    rb0 = row0 // rows_blk
    in_specs = [
        pl.BlockSpec((rows_blk, QKV_W), lambda b, c: (rb0 + b * steps + c, 0)),
        pl.BlockSpec((rows_blk, V_W), lambda b, c: (rb0 + b * steps + c, Z_BLK)),
        pl.BlockSpec((nseq, nck, 2 * nrow, R), lambda b, c: (b, c, 0, 0)),
        pl.BlockSpec((CONV_W, QKV_W), lambda b, c: (0, 0)),
        pl.BlockSpec((2 * nrow, R), lambda b, c: (0, 0)),
        pl.BlockSpec((1, D_V), lambda b, c: (0, 0)),
    ]
    args = [proj, proj, ba_arr, conv_w, ad, gamma_a.reshape(1, D_V)]
    if has_state:
        in_specs += [pl.BlockSpec((nseq, CONV_W - 1, QKV_W), lambda b, c: (b, 0, 0)),
                     pl.BlockSpec((nseq, H_V, D_K, D_V), lambda b, c: (b, 0, 0, 0))]
        args += [conv_buf, s0]
    return pl.pallas_call(
        functools.partial(_gdn_kernel, C=C, hg=hg, nseq=nseq, nck=nck, has_state=has_state),
        grid=(B // nseq, steps),
        in_specs=in_specs,
        out_specs=[pl.BlockSpec((rows_blk, V_W), lambda b, c: (b * steps + c, 0)),
                   pl.BlockSpec((nseq, H_V, D_K, D_V), lambda b, c: (b, 0, 0, 0))],
        out_shape=[jax.ShapeDtypeStruct((B * L, V_W), BF16),
                   jax.ShapeDtypeStruct((B, H_V, D_K, D_V), F32)],
        scratch_shapes=[pltpu.VMEM((nseq * H_V, D_K, D_V), F32),
                        pltpu.VMEM((nseq * 8, QKV_W), F32),
                        pltpu.VMEM((rows_blk, QKV_W), F32)],
        compiler_params=_cparams(("parallel", "arbitrary")),
        name="gdn_state" if has_state else "gdn_fresh",
    )(*args)


def _pool_kernel(*refs, tc, nseq, start_pos, has_state):
    if has_state:
        u_ref, buf_ref, o_ref, tail_scr = refs
    else:
        u_ref, o_ref, tail_scr = refs
    hist = POOL_BUF + 1
    c = pl.program_id(1)

    @pl.when(c == 0)
    def _init():
        tail_scr[...] = jnp.zeros(tail_scr.shape, F32)
        if has_state:
            for s in range(nseq):
                tail_scr[s * hist + 1:(s + 1) * hist] = buf_ref[s]

    u_all = u_ref[...].astype(F32)
    pos1 = (start_pos + 1 + c * tc + lax.broadcasted_iota(jnp.int32, (tc, 1), 0)).astype(F32)
    for s in range(nseq):
        u = u_all[s * tc:(s + 1) * tc]
        xe = jnp.concatenate([tail_scr[s * hist:(s + 1) * hist], u], axis=0)
        tail_scr[s * hist:(s + 1) * hist] = xe[tc:tc + hist]
        for gi, win in enumerate(POOL_WINDOWS):
            sl = slice(gi * POOL_GROUP_W, (gi + 1) * POOL_GROUP_W)
            acc = xe[:, sl]
            shift = 1
            while shift < win:
                acc = acc + pltpu.roll(acc, shift, 0)
                shift *= 2
            cnt = jnp.minimum(float(win), pos1)
            o_ref[s * tc:(s + 1) * tc, sl] = acc[hist:hist + tc] / cnt - u[:, sl]


def _pool(proj, row0, B, L, tc, nseq, start_pos, pool_buf=None):
    has_state = pool_buf is not None
    nc = L // tc
    assert nseq == 1 or nc == 1
    rows_blk = nseq * tc
    rb0 = row0 // rows_blk
    in_specs = [pl.BlockSpec((rows_blk, D_MODEL), lambda b, c: (rb0 + b * nc + c, U_BLK))]
    args = [proj]
    if has_state:
        in_specs.append(pl.BlockSpec((nseq, POOL_BUF, D_MODEL), lambda b, c: (b, 0, 0)))
        args.append(pool_buf)
    return pl.pallas_call(
        functools.partial(_pool_kernel, tc=tc, nseq=nseq, start_pos=start_pos, has_state=has_state),
        grid=(B // nseq, nc),
        in_specs=in_specs,
        out_specs=pl.BlockSpec((rows_blk, D_MODEL), lambda b, c: (b * nc + c, 0)),
        out_shape=jax.ShapeDtypeStruct((B * L, D_MODEL), F32),
        scratch_shapes=[pltpu.VMEM((nseq * (POOL_BUF + 1), D_MODEL), F32)],
        compiler_params=_cparams(("parallel", "arbitrary")),
        name="pool_state" if has_state else "pool_fresh",
    )(*args)


def _post_kernel(pa_ref, pb_ref, oa_ref, ob_ref, ga_ref, gb_ref, xa_ref, xb_ref, wp_ref, ps_ref, wbb_ref, wba_ref,
                 wo_ref, g_ref, b_ref, h_ref, hp_ref, *, alpha, tiles_a):
    first = pl.program_id(0) < tiles_a
    x = jnp.where(first, xa_ref[...], xb_ref[...])
    pooled = jnp.where(first, pa_ref[...], pb_ref[...])
    o_gated = jnp.where(first, oa_ref[...], ob_ref[...])
    mixed = jnp.concatenate(
        [_dot(pooled[:, gi * POOL_GROUP_W:(gi + 1) * POOL_GROUP_W], wp_ref[gi]) for gi in range(len(POOL_WINDOWS))],
        axis=1) * ps_ref[...]
    branch_b = _dot(mixed, wbb_ref[...])
    branch_a = _dot(o_gated, wba_ref[...])
    merged = _sigmoid(ga_ref[...].astype(F32)) * branch_a + _sigmoid(gb_ref[...].astype(F32)) * branch_b
    h = _layer_norm(alpha * x + _dot(merged, wo_ref[...]), g_ref[...], b_ref[...])
    h_ref[...] = h
    _store_pieces(hp_ref, h)


def _post(pooled_a, pooled_b, o_a, o_b, proj, x_a, x_b, w_pool, pool_scale, w_br_b, w_br_a, w_out, ln_g, ln_b,
          alpha, tm):
    t = x_a.shape[0] + x_b.shape[0]
    tiles_a = x_a.shape[0] // tm
    row = lambda i: (i, 0)
    row_a = lambda i: (jnp.minimum(i, tiles_a - 1), 0)
    row_b = lambda i: (jnp.maximum(i - tiles_a, 0), 0)
    const2 = lambda i: (0, 0)
    return pl.pallas_call(
        functools.partial(_post_kernel, alpha=alpha, tiles_a=tiles_a),
        grid=(t // tm,),
        in_specs=[pl.BlockSpec((tm, D_MODEL), row_a),
                  pl.BlockSpec((tm, D_MODEL), row_b),
                  pl.BlockSpec((tm, V_W), row_a),
                  pl.BlockSpec((tm, V_W), row_b),
                  pl.BlockSpec((tm, D_MODEL), lambda i: (i, U_BLK + 1)),
                  pl.BlockSpec((tm, D_MODEL), lambda i: (i, U_BLK + 2)),
                  pl.BlockSpec((tm, D_MODEL), row_a),
                  pl.BlockSpec((tm, D_MODEL), row_b),
                  pl.BlockSpec(w_pool.shape, lambda i: (0, 0, 0)),
                  pl.BlockSpec((1, D_MODEL), const2),
                  pl.BlockSpec(w_br_b.shape, const2),
                  pl.BlockSpec(w_br_a.shape, const2),
                  pl.BlockSpec(w_out.shape, const2),
                  pl.BlockSpec((1, D_MODEL), const2),
                  pl.BlockSpec((1, D_MODEL), const2)],
        out_specs=[pl.BlockSpec((tm, D_MODEL), row), pl.BlockSpec((N_PIECE, tm, PIECE_W), lambda i: (0, i, 0))],
        out_shape=[jax.ShapeDtypeStruct((t, D_MODEL), F32),
                   jax.ShapeDtypeStruct((N_PIECE, t, PIECE_W), jnp.uint32)],
        compiler_params=_cparams(("parallel",)),
        name="post_mixers",
    )(pooled_a, pooled_b, o_a, o_b, proj, proj, x_a, x_b, w_pool, pool_scale, w_br_b, w_br_a, w_out, ln_g, ln_b)


def _router_kernel(h_ref, wt_ref, bias_ref, idx_ref, wts_ref, rank_ref, cnt_ref, carry_scr, *, tm):
    i = pl.program_id(0)

    @pl.when(i == 0)
    def _init():
        carry_scr[...] = jnp.zeros(carry_scr.shape, F32)

    logits = _dot3_nt(wt_ref[...], h_ref[...])
    sc = _sigmoid(logits)
    ch = sc + bias_ref[...]
    neg = -jnp.inf
    e_in = lax.broadcasted_iota(jnp.int32, (GROUP_SZ, tm), 0)
    gs_rows = []
    for g in range(N_GROUP):
        blk = ch[g * GROUP_SZ:(g + 1) * GROUP_SZ]
        m1 = jnp.max(blk, axis=0, keepdims=True)
        i1 = jnp.min(jnp.where(blk == m1, e_in, GROUP_SZ), axis=0, keepdims=True)
        m2 = jnp.max(jnp.where(e_in == i1, neg, blk), axis=0, keepdims=True)
        gs_rows.append(m1 + m2)
    gs = jnp.concatenate(gs_rows, axis=0)
    g_io = lax.broadcasted_iota(jnp.int32, (N_GROUP, tm), 0)
    e_io = lax.broadcasted_iota(jnp.int32, (N_EXPERTS, tm), 0)
    e_grp = e_io >> int(math.log2(GROUP_SZ))
    masked = jnp.full((N_EXPERTS, tm), neg, F32)
    for _ in range(TOPK_GROUP):
        mx = jnp.max(gs, axis=0, keepdims=True)
        gi = jnp.min(jnp.where(gs == mx, g_io, N_GROUP), axis=0, keepdims=True)
        gs = jnp.where(g_io == gi, neg, gs)
        masked = jnp.where(e_grp == gi, ch, masked)
    idx_rows, w_rows = [], []
    onehot = jnp.zeros((N_EXPERTS, tm), F32)
    for _ in range(TOP_K):
        mx = jnp.max(masked, axis=0, keepdims=True)
        ei = jnp.min(jnp.where(masked == mx, e_io, N_EXPERTS), axis=0, keepdims=True)
        hit = e_io == ei
        idx_rows.append(ei)
        w_rows.append(jnp.sum(jnp.where(hit, sc, 0.0), axis=0, keepdims=True))
        onehot = jnp.where(hit, 1.0, onehot)
        masked = jnp.where(hit, neg, masked)
    wsel = jnp.concatenate(w_rows, axis=0)
    wts_ref[...] = wsel / jnp.sum(wsel, axis=0, keepdims=True) * ROUTED_SCALE
    idx_ref[...] = jnp.concatenate(idx_rows, axis=0)

    tr = lax.broadcasted_iota(jnp.int32, (tm, tm), 0)
    tc_ = lax.broadcasted_iota(jnp.int32, (tm, tm), 1)
    before = jnp.where(tr < tc_, 1.0, 0.0).astype(BF16)
    cum = jnp.dot(onehot.astype(BF16), before, preferred_element_type=F32) + carry_scr[...]
    rank_ref[...] = jnp.concatenate(
        [jnp.sum(jnp.where(e_io == ei, cum, 0.0), axis=0, keepdims=True) for ei in idx_rows],
        axis=0).astype(jnp.int32)
    carry_scr[...] = carry_scr[...] + jnp.sum(onehot, axis=1, keepdims=True)
    cnt_ref[...] = carry_scr[...]


def _router(h, w_router_t, bias_col, tm):
    t = h.shape[0]
    return pl.pallas_call(
        functools.partial(_router_kernel, tm=tm),
        grid=(t // tm,),
        in_specs=[pl.BlockSpec((tm, D_MODEL), lambda i: (i, 0)),
                  pl.BlockSpec((N_EXPERTS, D_MODEL), lambda i: (0, 0)),
                  pl.BlockSpec((N_EXPERTS, 1), lambda i: (0, 0))],
        out_specs=[pl.BlockSpec((TOP_K, tm), lambda i: (0, i)),
                   pl.BlockSpec((TOP_K, tm), lambda i: (0, i)),
                   pl.BlockSpec((TOP_K, tm), lambda i: (0, i)),
                   pl.BlockSpec((N_EXPERTS, 1), lambda i: (0, 0))],
        out_shape=[jax.ShapeDtypeStruct((TOP_K, t), jnp.int32),
                   jax.ShapeDtypeStruct((TOP_K, t), F32),
                   jax.ShapeDtypeStruct((TOP_K, t), jnp.int32),
                   jax.ShapeDtypeStruct((N_EXPERTS, 1), F32)],
        scratch_shapes=[pltpu.VMEM((N_EXPERTS, 1), F32)],
        compiler_params=_cparams(("arbitrary",)),
        name="router",
    )(h, w_router_t, bias_col)


def _dest_kernel(idx_ref, rank_ref, start_ref, dest_ref, *, tm, n_rows):
    e_io = lax.broadcasted_iota(jnp.int32, (N_EXPERTS, tm), 0)
    starts = start_ref[...]
    rows = []
    for k in range(TOP_K):
        seg = jnp.sum(jnp.where(e_io == idx_ref[k:k + 1, :], starts, 0.0), axis=0, keepdims=True)
        rows.append(seg.astype(jnp.int32) + rank_ref[k:k + 1, :])
    base = jnp.concatenate(rows, axis=0)
    for p in range(N_PIECE):
        dest_ref[p] = base + p * n_rows


def _dest(idx_t, rank_t, seg_start_col, n_rows, tm):
    t = idx_t.shape[1]
    blk = pl.BlockSpec((TOP_K, tm), lambda i: (0, i))
    return pl.pallas_call(
        functools.partial(_dest_kernel, tm=tm, n_rows=n_rows),
        grid=(t // tm,),
        in_specs=[blk, blk, pl.BlockSpec((N_EXPERTS, 1), lambda i: (0, 0))],
        out_specs=pl.BlockSpec((N_PIECE, TOP_K, tm), lambda i: (0, 0, i)),
        out_shape=jax.ShapeDtypeStruct((N_PIECE, TOP_K, t), jnp.int32),
        compiler_params=_cparams(("parallel",)),
        name="dispatch_rows",
    )(idx_t, rank_t, seg_start_col)


def _expert_kernel(be_ref, nu_ref, first_ref, nxt_ref, slot_ref, x_ref, wg_hbm, wu_hbm, wd_hbm, y_ref,
                   wg_buf, wu_buf, wd_buf, wgu_scr, wd_scr, sems):
    i = pl.program_id(0)
    live = i < nu_ref[0]

    def fetch(e, slot):
        return [pltpu.make_async_copy(wg_hbm.at[e], wg_buf.at[slot], sems.at[slot, 0]),
                pltpu.make_async_copy(wu_hbm.at[e], wu_buf.at[slot], sems.at[slot, 1]),
                pltpu.make_async_copy(wd_hbm.at[e], wd_buf.at[slot], sems.at[slot, 2])]

    @pl.when(live & (first_ref[i] == 1))
    def _new_expert():
        slot = slot_ref[i]

        @pl.when(i == 0)
        def _():
            for cp in fetch(be_ref[i], slot):
                cp.start()

        @pl.when(nxt_ref[i] >= 0)
        def _():
            for cp in fetch(nxt_ref[i], 1 - slot):
                cp.start()

        for cp in fetch(be_ref[i], slot):
            cp.wait()
        wgu_scr[:, :D_EXPERT] = wg_buf[slot].astype(BF16)
        wgu_scr[:, D_EXPERT:] = wu_buf[slot].astype(BF16)
        wd_scr[...] = wd_buf[slot].astype(BF16)

    @pl.when(live)
    def _():
        n_sub = 2
        sub = EXPERT_BM // n_sub
        xs = [jnp.concatenate([c.astype(BF16) for c in
                               _load_pieces([x_ref[p, s * sub:(s + 1) * sub] for p in range(N_PIECE)])], axis=1)
              for s in range(n_sub)]
        gus = [jnp.dot(x, wgu_scr[...], preferred_element_type=F32) for x in xs]
        acts = [(_silu(gu[:, :D_EXPERT]) * gu[:, D_EXPERT:]).astype(BF16) for gu in gus]
        ys = [jnp.dot(a, wd_scr[...], preferred_element_type=F32) for a in acts]
        for s, y in enumerate(ys):
            for p in range(N_PIECE):
                y_ref[p, s * sub:(s + 1) * sub] = _pack_halves(y[:, 2 * p * PIECE_W:2 * (p + 1) * PIECE_W])


def _experts(xs, blk_exp, n_used, blk_first, blk_next, blk_slot, w_gate, w_up, w_down):
    n_rows = xs.shape[1]
    n_blocks = n_rows // EXPERT_BM

    def row_block(i, be, nu, *_):
        return (0, jnp.minimum(i, nu[0] - 1), 0)

    grid_spec = pltpu.PrefetchScalarGridSpec(
        num_scalar_prefetch=5,
        grid=(n_blocks,),
        in_specs=[pl.BlockSpec((N_PIECE, EXPERT_BM, PIECE_W), row_block),
                  pl.BlockSpec(memory_space=pl.ANY),
                  pl.BlockSpec(memory_space=pl.ANY),
                  pl.BlockSpec(memory_space=pl.ANY)],
        out_specs=pl.BlockSpec((N_PIECE, EXPERT_BM, PIECE_W), row_block),
        scratch_shapes=[pltpu.VMEM((2, D_MODEL, D_EXPERT), F32),
                        pltpu.VMEM((2, D_MODEL, D_EXPERT), F32),
                        pltpu.VMEM((2, D_EXPERT, D_MODEL), F32),
                        pltpu.VMEM((D_MODEL, 2 * D_EXPERT), BF16),
                        pltpu.VMEM((D_EXPERT, D_MODEL), BF16),
                        pltpu.SemaphoreType.DMA((2, 3))],
    )
    return pl.pallas_call(
        _expert_kernel,
        grid_spec=grid_spec,
        out_shape=jax.ShapeDtypeStruct((N_PIECE, n_rows, PIECE_W), jnp.uint32),
        compiler_params=_cparams(("arbitrary",)),
        name="experts",
    )(blk_exp, n_used, blk_first, blk_next, blk_slot, xs, w_gate, w_up, w_down)


SC_WINDOW = 128
V7X_SC_CORES = 2
V7X_SC_SUBCORES = 16


def _sc_mesh():
    return plsc.VectorSubcoreMesh(core_axis_name="core", subcore_axis_name="subcore",
                                  num_cores=V7X_SC_CORES, num_subcores=V7X_SC_SUBCORES)


def _sc_scatter_rows(src, dest, n_rows, seg, repeat):
    d = src.shape[1]
    n_idx = dest.shape[0]
    seg_blocks = seg // SC_WINDOW
    dest2 = dest.reshape(1, n_idx)

    def src_block(i):
        return ((i // (repeat * seg_blocks)) * seg_blocks + i % seg_blocks, 0)

    @functools.partial(pl.kernel, out_type=jax.ShapeDtypeStruct((n_rows, d), src.dtype), mesh=_sc_mesh(),
                       scratch_types=[], name="sc_dispatch")
    def run(src_hbm, idx_hbm, out_hbm):
        def body(rows_vmem, idx_vmem):
            pltpu.sync_copy(rows_vmem, out_hbm.at[idx_vmem.at[0]])

        pltpu.emit_pipeline(
            body,
            grid=(n_idx // SC_WINDOW,),
            in_specs=[pl.BlockSpec((SC_WINDOW, d), src_block),
                      pl.BlockSpec((1, SC_WINDOW), lambda i: (0, i))],
            out_specs=[],
            core_axis_name=("core", "subcore"),
            dimension_semantics=(pltpu.PARALLEL,),
        )(src_hbm, idx_hbm)

    return run(src, dest2)


def _sc_gather_rows(table, idx):
    d = table.shape[1]
    n_idx = idx.shape[0]
    idx2 = idx.reshape(1, n_idx)

    @functools.partial(pl.kernel, out_type=jax.ShapeDtypeStruct((n_idx, d), table.dtype), mesh=_sc_mesh(),
                       scratch_types=[], name="sc_combine_gather")
    def run(table_hbm, idx_hbm, out_hbm):
        def body(idx_vmem, rows_vmem):
            pltpu.sync_copy(table_hbm.at[idx_vmem.at[0]], rows_vmem)

        pltpu.emit_pipeline(
            body,
            grid=(n_idx // SC_WINDOW,),
            in_specs=[pl.BlockSpec((1, SC_WINDOW), lambda i: (0, i))],
            out_specs=[pl.BlockSpec((SC_WINDOW, d), lambda i: (i, 0))],
            core_axis_name=("core", "subcore"),
            dimension_semantics=(pltpu.PARALLEL,),
        )(idx_hbm, out_hbm)

    return run(table, idx2)


def _combine_kernel(yg_ref, wts_ref, h_ref, wgu_ref, wd_ref, g_ref, b_ref, outa_ref, outb_ref, *, alpha, tiles_a):
    wts = wts_ref[...]
    acc = None
    for k in range(TOP_K):
        cols = _load_pieces([yg_ref[p, k] for p in range(N_PIECE)])
        wk = wts[:, k:k + 1]
        acc = [c * wk for c in cols] if acc is None else [a + c * wk for a, c in zip(acc, cols)]
    routed = jnp.concatenate(acc, axis=1)
    h = h_ref[...]
    gu = _dot(h, wgu_ref[...])
    shared = _dot(_silu(gu[:, :D_EXPERT]) * gu[:, D_EXPERT:], wd_ref[...])
    y = _layer_norm(alpha * h + (routed + shared), g_ref[...], b_ref[...])
    i = pl.program_id(0)

    @pl.when(i < tiles_a)
    def _():
        outa_ref[...] = y

    @pl.when(i >= tiles_a)
    def _():
        outb_ref[...] = y


def _combine(yg, wts, h, w_sh_gu, w_sh_down, ln_g, ln_b, alpha, tm, t_a):
    t = h.shape[0]
    tiles_a = t_a // tm
    row = lambda i: (i, 0)
    const2 = lambda i: (0, 0)
    return pl.pallas_call(
        functools.partial(_combine_kernel, alpha=alpha, tiles_a=tiles_a),
        grid=(t // tm,),
        in_specs=[pl.BlockSpec((N_PIECE, TOP_K, tm, PIECE_W), lambda i: (0, 0, i, 0)),
                  pl.BlockSpec((tm, TOP_K), row),
                  pl.BlockSpec((tm, D_MODEL), row),
                  pl.BlockSpec(w_sh_gu.shape, const2),
                  pl.BlockSpec(w_sh_down.shape, const2),
                  pl.BlockSpec((1, D_MODEL), const2),
                  pl.BlockSpec((1, D_MODEL), const2)],
        out_specs=[pl.BlockSpec((tm, D_MODEL), lambda i: (jnp.minimum(i, tiles_a - 1), 0)),
                   pl.BlockSpec((tm, D_MODEL), lambda i: (jnp.maximum(i - tiles_a, 0), 0))],
        out_shape=[jax.ShapeDtypeStruct((t_a, D_MODEL), F32), jax.ShapeDtypeStruct((t - t_a, D_MODEL), F32)],
        compiler_params=_cparams(("arbitrary",)),
        name="combine_ln2",
    )(yg, wts, h, w_sh_gu, w_sh_down, ln_g, ln_b)


def _layer(xp, xs, s_delta, s_conv, s_pool, w_in, conv_w, a_log, dt_bias, gamma_a, w_br_a, w_pool,
           pool_scale, w_br_b, w_out, ln1_g, ln1_b, w_router, router_bias, w_exp_gate, w_exp_up,
           w_exp_down, w_sh_gate, w_sh_up, w_sh_down, ln2_g, ln2_b, alpha):
    Bp, Lp, _ = xp.shape
    Bs, Ls, _ = xs.shape
    Tp, Ts = Bp * Lp, Bs * Ls
    T = Tp + Ts
    x_p = xp.reshape(Tp, D_MODEL)
    x_s = xs.reshape(Ts, D_MODEL)

    o_z, o_b, o_a, o_u = QKV_W, QKV_W + V_W, QKV_W + V_W + H_V, QKV_W + V_W + 2 * H_V
    w_main = jnp.concatenate([w_in[:, :o_b], w_in[:, o_u:]], axis=1).astype(BF16)
    w_ba = jnp.pad(w_in[:, o_b:o_u], ((0, 0), (0, 128 - 2 * H_V))).astype(BF16)
    proj, ba = _in_proj(x_p, x_s, w_main, w_ba, 1024, MAIN_W // 3)

    o_p, sd_p = _gdn(proj, ba, conv_w, a_log, dt_bias, gamma_a, 0, Bp, Lp, 64, 1, GDN_FRESH_CHUNKS)
    o_s, sd_s = _gdn(proj, ba, conv_w, a_log, dt_bias, gamma_a, Tp, Bs, Ls, Ls, GDN_STATE_SEQS, 1,
                     conv_buf=s_conv, s0=s_delta)

    pooled_p = _pool(proj, 0, Bp, Lp, 256, 1, 0)
    pooled_s = _pool(proj, Tp, Bs, Ls, Ls, POOL_STATE_SEQS, PAST_LEN, pool_buf=s_pool)

    h, hp = _post(pooled_p, pooled_s, o_p, o_s, proj, x_p, x_s, w_pool.astype(BF16), pool_scale.reshape(1, D_MODEL),
                  w_br_b.astype(BF16), w_br_a.astype(BF16), w_out.astype(BF16),
                  ln1_g.reshape(1, D_MODEL), ln1_b.reshape(1, D_MODEL), alpha, 256)

    idx_t, wts_t, rank_t, cnt = _router(h, w_router.T, router_bias.reshape(N_EXPERTS, 1), 512)
    counts = cnt[:, 0].astype(jnp.int32)
    padded = ((counts + EXPERT_BM - 1) // EXPERT_BM) * EXPERT_BM
    pends = jnp.cumsum(padded)
    pstarts = pends - padded
    n_blocks = (T * TOP_K + N_EXPERTS * (EXPERT_BM - 1) + EXPERT_BM - 1) // EXPERT_BM
    n_rows = n_blocks * EXPERT_BM
    piece_rows = _dest(idx_t, rank_t, pstarts.astype(F32).reshape(N_EXPERTS, 1), n_rows, 512)
    blk_start = jnp.arange(n_blocks, dtype=jnp.int32) * EXPERT_BM
    blk_exp = jnp.minimum(jnp.sum((pends[None, :] <= blk_start[:, None]).astype(jnp.int32), axis=1),
                          N_EXPERTS - 1)
    n_used = (pends[-1:] // EXPERT_BM).astype(jnp.int32)
    has_rows = counts > 0
    e_ids = jnp.arange(N_EXPERTS, dtype=jnp.int32)
    slot_of = (jnp.cumsum(has_rows.astype(jnp.int32)) - 1) & 1
    later = jnp.where(has_rows, e_ids, N_EXPERTS)
    next_of = jnp.concatenate([lax.cummin(later, reverse=True)[1:], jnp.full((1,), N_EXPERTS, jnp.int32)])
    next_of = jnp.where(next_of < N_EXPERTS, next_of, -1)
    blk_first = (blk_start == pstarts[blk_exp]).astype(jnp.int32)
    blk_next = next_of[blk_exp]
    blk_slot = slot_of[blk_exp]

    piece_idx = piece_rows.reshape(N_PIECE * TOP_K * T)
    x_sorted = _sc_scatter_rows(hp.reshape(N_PIECE * T, PIECE_W), piece_idx, N_PIECE * n_rows, T, TOP_K)
    y_sorted = _experts(x_sorted.reshape(N_PIECE, n_rows, PIECE_W), blk_exp, n_used, blk_first, blk_next,
                        blk_slot, w_exp_gate, w_exp_up, w_exp_down)
    yg = _sc_gather_rows(y_sorted.reshape(N_PIECE * n_rows, PIECE_W), piece_idx)
    yg = yg.reshape(N_PIECE, TOP_K, T, PIECE_W)

    w_sh_gu = jnp.concatenate([w_sh_gate, w_sh_up], axis=1).astype(BF16)
    y_p, y_s = _combine(yg, wts_t.T, h, w_sh_gu, w_sh_down.astype(BF16),
                        ln2_g.reshape(1, D_MODEL), ln2_b.reshape(1, D_MODEL), alpha, 512, Tp)

    def carried_rows(keep, col0, width, name, old_p, old_s):
        parts, counts = [], []
        for x2d, B, L in ((x_p, Bp, Lp), (x_s, Bs, Ls)):
            n = min(L, keep)
            rows = jnp.arange(B, dtype=jnp.int32)[:, None] * L + (L - n) + jnp.arange(n, dtype=jnp.int32)
            parts.append(jnp.take(x2d, rows.reshape(-1), axis=0).astype(BF16))
            counts.append((B, n))
        n_tail = sum(B * n for B, n in counts)
        n_pad = -(-n_tail // 128) * 128
        xt = jnp.pad(jnp.concatenate(parts, axis=0), ((0, n_pad - n_tail), (0, 0)))
        tail = _matmul(xt, w_main, col0, width, n_pad, 1024, F32, name)
        outs, r0 = [], 0
        for (B, n), old in zip(counts, (old_p, old_s)):
            new = tail[r0:r0 + B * n].reshape(B, n, width)
            outs.append(new if n == keep else jnp.concatenate([old[:, n:], new], axis=1))
            r0 += B * n
        return outs

    new_conv_p, new_conv_s = carried_rows(CONV_W - 1, 0, QKV_W, "in_proj_conv_rows",
                                          jnp.zeros((Bp, CONV_W - 1, QKV_W), F32), s_conv)
    new_pool_p, new_pool_s = carried_rows(POOL_BUF, U_BLK * D_MODEL, D_MODEL, "in_proj_pool_rows",
                                          jnp.zeros((Bp, POOL_BUF, D_MODEL), F32), s_pool)
    return (y_p.reshape(Bp, Lp, D_MODEL), y_s.reshape(Bs, Ls, D_MODEL),
            sd_p, new_conv_p, new_pool_p, sd_s, new_conv_s, new_pool_s)


def kernel(x_prompt, x_sample, state_delta, state_conv, state_pool, w_in, conv_w, a_log, dt_bias, gamma_a,
           w_br_a, w_pool, pool_scale, w_br_b, w_out, ln1_g, ln1_b, w_router, router_bias,
           w_exp_gate, w_exp_up, w_exp_down, w_sh_gate, w_sh_up, w_sh_down, ln2_g, ln2_b):
    depth = w_in.shape[0]
    alpha = (2 * depth) ** 0.25
    yp, ys = x_prompt, x_sample
    outs = [[] for _ in range(6)]
    for l in range(depth):
        res = _layer(yp, ys, state_delta[l], state_conv[l], state_pool[l], w_in[l], conv_w[l], a_log[l],
                     dt_bias[l], gamma_a[l], w_br_a[l], w_pool[l], pool_scale[l], w_br_b[l], w_out[l],
                     ln1_g[l], ln1_b[l], w_router[l], router_bias[l], w_exp_gate[l], w_exp_up[l],
                     w_exp_down[l], w_sh_gate[l], w_sh_up[l], w_sh_down[l], ln2_g[l], ln2_b[l], alpha)
        yp, ys = res[0], res[1]
        for lst, v in zip(outs, res[2:]):
            lst.append(v)
    return (yp, ys) + tuple(jnp.stack(v) for v in outs)
```

```python
import functools
import math

import jax
import jax.numpy as jnp
from jax import lax
from jax.experimental import pallas as pl
from jax.experimental.pallas import tpu as pltpu
from jax.experimental.pallas import tpu_sc as plsc

F32 = jnp.float32
BF16 = jnp.bfloat16

D_MODEL = 1024
H_K = 8
D_K = 128
H_V = 16
D_V = 128
Q_W = H_K * D_K
V_W = H_V * D_V
QKV_W = 2 * Q_W + V_W
CONV_W = 4
POOL_WINDOWS = (2, 4, 8, 16)
POOL_GROUP_W = D_MODEL // len(POOL_WINDOWS)
POOL_BUF = max(POOL_WINDOWS) - 1
N_EXPERTS = 256
TOP_K = 8
N_GROUP = 8
TOPK_GROUP = 4
GROUP_SZ = N_EXPERTS // N_GROUP
D_EXPERT = D_MODEL // 4
ROUTED_SCALE = 2.5
LN_EPS = 1e-5
NORM_EPS = 1e-6
PAST_LEN = 16384

MAIN_W = QKV_W + V_W + 3 * D_MODEL
Z_BLK = QKV_W // V_W
U_BLK = (QKV_W + V_W) // D_MODEL

STACK_ROWS = 128
SOLVE_BASE = 16
GDN_FRESH_CHUNKS = 4
EXPERT_BM = 512
GDN_STATE_SEQS = 8
POOL_STATE_SEQS = 16
VMEM_LIMIT = 56 * 1024 * 1024


def _cparams(sem):
    return pltpu.CompilerParams(dimension_semantics=sem, vmem_limit_bytes=VMEM_LIMIT)


def _sigmoid(x):
    return 0.5 * jnp.tanh(0.5 * x) + 0.5


def _silu(x):
    return x * _sigmoid(x)


def _softplus(x):
    return jnp.maximum(x, 0.0) + jnp.log(1.0 + jnp.exp(-jnp.abs(x)))


def _dot(a, b):
    return jnp.dot(a.astype(BF16), b.astype(BF16), preferred_element_type=F32)


def _dot_nt(a, b):
    return lax.dot_general(a.astype(BF16), b.astype(BF16), (((1,), (1,)), ((), ())),
                           preferred_element_type=F32)


def _dot_tn(a, b):
    return lax.dot_general(a.astype(BF16), b.astype(BF16), (((0,), (0,)), ((), ())),
                           preferred_element_type=F32)


def _split(a):
    hi = a.astype(BF16)
    lo = (a - hi.astype(F32)).astype(BF16)
    return hi, lo


def _dot3_nt(a, b):
    ah, al = _split(a)
    bh, bl = _split(b)
    d = functools.partial(lax.dot_general, dimension_numbers=(((1,), (1,)), ((), ())),
                          preferred_element_type=F32)
    return d(ah, bh) + (d(ah, bl) + d(al, bh))


def _pack_halves(x):
    n = x.shape[1] // 2
    hi = lax.bitcast_convert_type(x[:, :n].astype(BF16).astype(F32), jnp.uint32)
    lo = lax.bitcast_convert_type(x[:, n:].astype(BF16).astype(F32), jnp.uint32)
    return (hi & jnp.uint32(0xFFFF0000)) | (lo >> 16)


def _unpack_halves(w):
    hi = lax.bitcast_convert_type(w & jnp.uint32(0xFFFF0000), F32)
    lo = lax.bitcast_convert_type(w << 16, F32)
    return hi, lo


N_PIECE = 2
PIECE_W = D_MODEL // (2 * N_PIECE)


def _store_pieces(ref, x):
    for p in range(N_PIECE):
        ref[p] = _pack_halves(x[:, 2 * p * PIECE_W:2 * (p + 1) * PIECE_W])


def _load_pieces(pieces):
    cols = []
    for w in pieces:
        cols.extend(_unpack_halves(w))
    return cols


def _layer_norm(x, g, b):
    mu = jnp.mean(x, axis=-1, keepdims=True)
    xc = x - mu
    var = jnp.mean(xc * xc, axis=-1, keepdims=True)
    return xc * lax.rsqrt(var + LN_EPS) * g + b


def _in_proj_kernel(xa_ref, xb_ref, w_ref, wba_ref, o_ref, ba_ref, x_scr, *, tiles_a):
    @pl.when(pl.program_id(1) == 0)
    def _():
        x = jnp.where(pl.program_id(0) < tiles_a, xa_ref[...], xb_ref[...]).astype(BF16)
        x_scr[...] = x
        ba_ref[...] = jnp.dot(x, wba_ref[...], preferred_element_type=F32)

    o_ref[...] = jnp.dot(x_scr[...], w_ref[...], preferred_element_type=F32).astype(o_ref.dtype)


def _in_proj(x_a, x_b, w_main, w_ba, tm, tn):
    k = x_a.shape[1]
    t = x_a.shape[0] + x_b.shape[0]
    n = w_main.shape[1]
    tiles_a = x_a.shape[0] // tm
    return pl.pallas_call(
        functools.partial(_in_proj_kernel, tiles_a=tiles_a),
        grid=(t // tm, n // tn),
        in_specs=[pl.BlockSpec((tm, k), lambda i, j: (jnp.minimum(i, tiles_a - 1), 0)),
                  pl.BlockSpec((tm, k), lambda i, j: (jnp.maximum(i - tiles_a, 0), 0)),
                  pl.BlockSpec((k, tn), lambda i, j: (0, j)),
                  pl.BlockSpec(w_ba.shape, lambda i, j: (0, 0))],
        out_specs=[pl.BlockSpec((tm, tn), lambda i, j: (i, j)),
                   pl.BlockSpec((tm, w_ba.shape[1]), lambda i, j: (i, 0))],
        out_shape=[jax.ShapeDtypeStruct((t, n), BF16), jax.ShapeDtypeStruct((t, w_ba.shape[1]), F32)],
        scratch_shapes=[pltpu.VMEM((tm, k), BF16)],
        compiler_params=_cparams(("parallel", "arbitrary")),
        name="in_proj",
    )(x_a, x_b, w_main, w_ba)


def _mm_kernel(x_ref, w_ref, o_ref):
    o_ref[...] = jnp.dot(x_ref[...], w_ref[...], preferred_element_type=F32).astype(o_ref.dtype)


def _matmul(x, w, col0, n, tm, tn, out_dtype, name):
    t, k = x.shape
    cb0 = col0 // tn
    return pl.pallas_call(
        _mm_kernel,
        grid=(t // tm, n // tn),
        in_specs=[pl.BlockSpec((tm, k), lambda i, j: (i, 0)),
                  pl.BlockSpec((k, tn), lambda i, j: (0, cb0 + j))],
        out_specs=pl.BlockSpec((tm, tn), lambda i, j: (i, j)),
        out_shape=jax.ShapeDtypeStruct((t, n), out_dtype),
        compiler_params=_cparams(("parallel", "parallel")),
        name=name,
    )(x, w)


def _gdn_kernel(*refs, C, hg, nseq, nck, has_state):
    if has_state:
        (qkv_ref, z_ref, ba_ref, cw_ref, ad_ref, gam_ref, buf_ref, s0_ref,
         o_ref, sout_ref, s_scr, tail_scr, act_scr) = refs
    else:
        (qkv_ref, z_ref, ba_ref, cw_ref, ad_ref, gam_ref,
         o_ref, sout_ref, s_scr, tail_scr, act_scr) = refs
    R = hg * C
    ng = H_V // hg
    log2c = int(math.log2(C))
    c = pl.program_id(1)

    @pl.when(c == 0)
    def _init():
        if has_state:
            tail_scr[...] = jnp.zeros(tail_scr.shape, F32)
            for s in range(nseq):
                s_scr[s * H_V:(s + 1) * H_V] = s0_ref[s]
                tail_scr[(s + 1) * 8 - (CONV_W - 1):(s + 1) * 8] = buf_ref[s]
        else:
            s_scr[...] = jnp.zeros(s_scr.shape, F32)
            tail_scr[...] = jnp.zeros(tail_scr.shape, F32)

    cw = cw_ref[...]
    qkv_all = qkv_ref[...].astype(F32)
    n_prev = CONV_W - 1
    CB = nck * C
    for s in range(nseq):
        x = qkv_all[s * CB:(s + 1) * CB]
        xe = jnp.concatenate([tail_scr[s * 8:(s + 1) * 8], x], axis=0)
        y = x * cw[n_prev:CONV_W]
        for j in range(n_prev):
            y = y + xe[8 - n_prev + j:8 - n_prev + j + CB] * cw[j:j + 1]
        tail_scr[s * 8:(s + 1) * 8] = x[CB - 8:CB]
        act_scr[s * CB:(s + 1) * CB] = _silu(y)
        for off, scale in ((0, D_K ** -0.5), (Q_W, 1.0)):
            for kh in range(H_K):
                cols = slice(off + kh * D_K, off + (kh + 1) * D_K)
                v = act_scr[s * CB:(s + 1) * CB, cols]
                act_scr[s * CB:(s + 1) * CB, cols] = v * (
                    lax.rsqrt(jnp.sum(v * v, axis=-1, keepdims=True) + NORM_EPS) * scale)

    ri = lax.broadcasted_iota(jnp.int32, (R, R), 0)
    ci = lax.broadcasted_iota(jnp.int32, (R, R), 1)
    same = (ri >> log2c) == (ci >> log2c)
    incl = same & (ri >= ci)
    strict = same & (ri > ci)
    eye = ri == ci
    is_last = ci == (((ri >> log2c) << log2c) + (C - 1))
    eye_f = jnp.where(eye, 1.0, 0.0).astype(F32)
    cum_u = jnp.where(same & (ri <= ci), 1.0, 0.0).astype(BF16)

    nrow = ba_ref.shape[2] // 2
    adv = ad_ref[...]
    dd = functools.partial(jnp.dot, preferred_element_type=F32)
    beta_of, G_of = {}, {}
    for s in range(nseq):
        for k in range(nck):
            bav = ba_ref[s, k]
            beta_of[s, k] = _sigmoid(bav[0:nrow])
            g_rows = -jnp.exp(adv[0:nrow]) * _softplus(bav[nrow:2 * nrow] + adv[nrow:2 * nrow])
            g1 = g_rows.astype(BF16)
            r1 = g_rows - g1.astype(F32)
            g2 = r1.astype(BF16)
            g3 = (r1 - g2.astype(F32)).astype(BF16)
            G_of[s, k] = dd(g1, cum_u) + (dd(g2, cum_u) + dd(g3, cum_u))

    gam = gam_ref[...]
    rep = H_V // H_K

    def stacked(d, off, per_k_head):
        blocks = [(h // rep if per_k_head else h) for h in d["heads"]]
        return jnp.concatenate([act_scr[d["rows"], off + b * D_K:off + (b + 1) * D_K] for b in blocks], axis=0)

    log2b = min(log2c, int(math.log2(SOLVE_BASE)))
    same_base = ((ri ^ ci) >> log2b) == 0
    row_head = lax.broadcasted_iota(jnp.int32, (R, D_V), 0) >> log2c
    z_all = z_ref[...].astype(F32)
    grp = [dict(s=s, k=k, g=g, rows=slice(s * CB + k * C, s * CB + (k + 1) * C), s0=s * H_V,
                heads=[g * hg + hh for hh in range(hg)])
           for k in range(nck) for s in range(nseq) for g in range(ng)]
    for d in grp:
        kst = stacked(d, Q_W, True)
        d["kk"] = _dot_nt(kst, kst)
        d["qk"] = _dot_nt(stacked(d, 0, True), kst)
    for d in grp:
        grow = G_of[d["s"], d["k"]][d["g"]:d["g"] + 1]
        brow = beta_of[d["s"], d["k"]][d["g"]:d["g"] + 1]
        gcol = jnp.sum(eye_f * grow, axis=1, keepdims=True)
        bcol = jnp.sum(eye_f * brow, axis=1, keepdims=True)
        glast = jnp.sum(jnp.where(is_last, grow, 0.0), axis=1, keepdims=True)
        dm = jnp.where(incl, jnp.exp(jnp.minimum(gcol - grow, 0.0)), 0.0)
        d.update(gcol=gcol, bcol=bcol, glast=glast, eg=jnp.exp(gcol))
        d["m"] = jnp.where(strict, -(d["kk"] * bcol * dm), 0.0)
        d["attn"] = jnp.where(incl, d["qk"] * dm, 0.0)
        d["pw"] = jnp.where(same_base, d["m"], 0.0)
        d["t"] = eye_f + d["pw"]

    for r in range(log2b):
        last = r == log2b - 1
        for d in grp:
            if r == 0:
                if not last:
                    d["pw"] = _dot(d["pw"], d["pw"])
            elif last:
                d["t"] = d["t"] + _dot(d["pw"], d["t"])
            else:
                out = _dot(d["pw"], jnp.concatenate([d["pw"], d["t"]], axis=1))
                d["pw"] = out[:, :R]
                d["t"] = d["t"] + out[:, R:]
    for lev in range(log2b + 1, log2c + 1):
        coupling = ((ri ^ ci) >> (lev - 1)) == 1
        for d in grp:
            d["y"] = _dot(d["t"], jnp.where(coupling, d["m"], 0.0))
        for d in grp:
            d["t"] = d["t"] + _dot(d["y"], d["t"])
    for d in grp:
        rhs = jnp.concatenate([stacked(d, 2 * Q_W, False) * d["bcol"],
                               stacked(d, Q_W, True) * (d["bcol"] * d["eg"])], axis=1)
        d["x"] = _dot(d["t"], rhs)

    for k in range(nck):
        now = [d for d in grp if d["k"] == k]
        for d in now:
            qe = stacked(d, 0, True) * d["eg"]
            d["xs"] = [_dot(jnp.concatenate([d["x"][hh * C:(hh + 1) * C, D_V:], qe[hh * C:(hh + 1) * C]], axis=0),
                            s_scr[d["s0"] + h]) for hh, h in enumerate(d["heads"])]
        for d in now:
            d["vnew"] = jnp.concatenate(
                [d["x"][hh * C:(hh + 1) * C, :D_V] - xs[:C] for hh, xs in enumerate(d["xs"])], axis=0)
            d["o"] = jnp.concatenate([xs[C:] for xs in d["xs"]], axis=0) + _dot(d["attn"], d["vnew"])
        for d in now:
            kt = stacked(d, Q_W, True) * jnp.exp(d["glast"] - d["gcol"])
            egl = jnp.exp(d["glast"])
            for hh, h in enumerate(d["heads"]):
                sl = slice(hh * C, (hh + 1) * C)
                if C >= 16:
                    upd = _dot_tn(kt[sl], d["vnew"][sl])
                else:
                    upd = _dot_tn(kt, jnp.where(row_head == hh, d["vnew"], 0.0))
                s_scr[d["s0"] + h] = s_scr[d["s0"] + h] * egl[hh * C:hh * C + 1] + upd

    for d in grp:
        o = d["o"]
        zst = jnp.concatenate([z_all[d["rows"], h * D_V:(h + 1) * D_V] for h in d["heads"]], axis=0)
        on = o * lax.rsqrt(jnp.mean(o * o, axis=-1, keepdims=True) + NORM_EPS) * gam * _silu(zst)
        for hh, h in enumerate(d["heads"]):
            o_ref[d["rows"], h * D_V:(h + 1) * D_V] = on[hh * C:(hh + 1) * C].astype(o_ref.dtype)

    @pl.when(c == pl.num_programs(1) - 1)
    def _fin():
        for s in range(nseq):
            sout_ref[s] = s_scr[s * H_V:(s + 1) * H_V]


def _gdn(proj, ba, conv_w, a_log, dt_bias, gamma_a, row0, B, L, C, nseq, nck, conv_buf=None, s0=None):
    has_state = s0 is not None
    hg = STACK_ROWS // C
    ng = H_V // hg
    nrow = 16
    nc = L // C
    R = STACK_ROWS
    assert nseq == 1 or nc == nck
    rows_blk = nseq * nck * C
    steps = nc // nck

    def arrange(v):
        v = v.reshape(B, nc, C, ng, hg).transpose(0, 1, 3, 4, 2).reshape(B, nc, ng, R)
        return jnp.pad(v, ((0, 0), (0, 0), (0, nrow - ng), (0, 0)))

    rows = ba[row0:row0 + B * L]
    ba_arr = jnp.concatenate([arrange(rows[:, :H_V]), arrange(rows[:, H_V:2 * H_V])], axis=2)

    def arrange_param(p):
        v = jnp.broadcast_to(p.reshape(ng, hg, 1), (ng, hg, C)).reshape(ng, R)
        return jnp.pad(v, ((0, nrow - ng), (0, 0)))

    ad = jnp.concatenate([arrange_param(a_log), arrange_param(dt_bias)], axis=0)
    rb0 = row0 // rows_blk
    in_specs = [
        pl.BlockSpec((rows_blk, QKV_W), lambda b, c: (rb0 + b * steps + c, 0)),
        pl.BlockSpec((rows_blk, V_W), lambda b, c: (rb0 + b * steps + c, Z_BLK)),
        pl.BlockSpec((nseq, nck, 2 * nrow, R), lambda b, c: (b, c, 0, 0)),
        pl.BlockSpec((CONV_W, QKV_W), lambda b, c: (0, 0)),
        pl.BlockSpec((2 * nrow, R), lambda b, c: (0, 0)),
        pl.BlockSpec((1, D_V), lambda b, c: (0, 0)),
    ]
    args = [proj, proj, ba_arr, conv_w, ad, gamma_a.reshape(1, D_V)]
    if has_state:
        in_specs += [pl.BlockSpec((nseq, CONV_W - 1, QKV_W), lambda b, c: (b, 0, 0)),
                     pl.BlockSpec((nseq, H_V, D_K, D_V), lambda b, c: (b, 0, 0, 0))]
        args += [conv_buf, s0]
    return pl.pallas_call(
        functools.partial(_gdn_kernel, C=C, hg=hg, nseq=nseq, nck=nck, has_state=has_state),
        grid=(B // nseq, steps),
        in_specs=in_specs,
        out_specs=[pl.BlockSpec((rows_blk, V_W), lambda b, c: (b * steps + c, 0)),
                   pl.BlockSpec((nseq, H_V, D_K, D_V), lambda b, c: (b, 0, 0, 0))],
        out_shape=[jax.ShapeDtypeStruct((B * L, V_W), BF16),
                   jax.ShapeDtypeStruct((B, H_V, D_K, D_V), F32)],
        scratch_shapes=[pltpu.VMEM((nseq * H_V, D_K, D_V), F32),
                        pltpu.VMEM((nseq * 8, QKV_W), F32),
                        pltpu.VMEM((rows_blk, QKV_W), F32)],
        compiler_params=_cparams(("parallel", "arbitrary")),
        name="gdn_state" if has_state else "gdn_fresh",
    )(*args)


def _pool_kernel(*refs, tc, nseq, start_pos, has_state):
    if has_state:
        u_ref, buf_ref, o_ref, tail_scr = refs
    else:
        u_ref, o_ref, tail_scr = refs
    hist = POOL_BUF + 1
    c = pl.program_id(1)

    @pl.when(c == 0)
    def _init():
        tail_scr[...] = jnp.zeros(tail_scr.shape, F32)
        if has_state:
            for s in range(nseq):
                tail_scr[s * hist + 1:(s + 1) * hist] = buf_ref[s]

    u_all = u_ref[...].astype(F32)
    pos1 = (start_pos + 1 + c * tc + lax.broadcasted_iota(jnp.int32, (tc, 1), 0)).astype(F32)
    for s in range(nseq):
        u = u_all[s * tc:(s + 1) * tc]
        xe = jnp.concatenate([tail_scr[s * hist:(s + 1) * hist], u], axis=0)
        tail_scr[s * hist:(s + 1) * hist] = xe[tc:tc + hist]
        for gi, win in enumerate(POOL_WINDOWS):
            sl = slice(gi * POOL_GROUP_W, (gi + 1) * POOL_GROUP_W)
            acc = xe[:, sl]
            shift = 1
            while shift < win:
                acc = acc + pltpu.roll(acc, shift, 0)
                shift *= 2
            cnt = jnp.minimum(float(win), pos1)
            o_ref[s * tc:(s + 1) * tc, sl] = acc[hist:hist + tc] / cnt - u[:, sl]


def _pool(proj, row0, B, L, tc, nseq, start_pos, pool_buf=None):
    has_state = pool_buf is not None
    nc = L // tc
    assert nseq == 1 or nc == 1
    rows_blk = nseq * tc
    rb0 = row0 // rows_blk
    in_specs = [pl.BlockSpec((rows_blk, D_MODEL), lambda b, c: (rb0 + b * nc + c, U_BLK))]
    args = [proj]
    if has_state:
        in_specs.append(pl.BlockSpec((nseq, POOL_BUF, D_MODEL), lambda b, c: (b, 0, 0)))
        args.append(pool_buf)
    return pl.pallas_call(
        functools.partial(_pool_kernel, tc=tc, nseq=nseq, start_pos=start_pos, has_state=has_state),
        grid=(B // nseq, nc),
        in_specs=in_specs,
        out_specs=pl.BlockSpec((rows_blk, D_MODEL), lambda b, c: (b * nc + c, 0)),
        out_shape=jax.ShapeDtypeStruct((B * L, D_MODEL), F32),
        scratch_shapes=[pltpu.VMEM((nseq * (POOL_BUF + 1), D_MODEL), F32)],
        compiler_params=_cparams(("parallel", "arbitrary")),
        name="pool_state" if has_state else "pool_fresh",
    )(*args)


def _post_kernel(pa_ref, pb_ref, oa_ref, ob_ref, ga_ref, gb_ref, xa_ref, xb_ref, wp_ref, ps_ref, wbb_ref, wba_ref,
                 wo_ref, g_ref, b_ref, h_ref, hp_ref, *, alpha, tiles_a):
    first = pl.program_id(0) < tiles_a
    x = jnp.where(first, xa_ref[...], xb_ref[...])
    pooled = jnp.where(first, pa_ref[...], pb_ref[...])
    o_gated = jnp.where(first, oa_ref[...], ob_ref[...])
    mixed = jnp.concatenate(
        [_dot(pooled[:, gi * POOL_GROUP_W:(gi + 1) * POOL_GROUP_W], wp_ref[gi]) for gi in range(len(POOL_WINDOWS))],
        axis=1) * ps_ref[...]
    branch_b = _dot(mixed, wbb_ref[...])
    branch_a = _dot(o_gated, wba_ref[...])
    merged = _sigmoid(ga_ref[...].astype(F32)) * branch_a + _sigmoid(gb_ref[...].astype(F32)) * branch_b
    h = _layer_norm(alpha * x + _dot(merged, wo_ref[...]), g_ref[...], b_ref[...])
    h_ref[...] = h
    _store_pieces(hp_ref, h)


def _post(pooled_a, pooled_b, o_a, o_b, proj, x_a, x_b, w_pool, pool_scale, w_br_b, w_br_a, w_out, ln_g, ln_b,
          alpha, tm):
    t = x_a.shape[0] + x_b.shape[0]
    tiles_a = x_a.shape[0] // tm
    row = lambda i: (i, 0)
    row_a = lambda i: (jnp.minimum(i, tiles_a - 1), 0)
    row_b = lambda i: (jnp.maximum(i - tiles_a, 0), 0)
    const2 = lambda i: (0, 0)
    return pl.pallas_call(
        functools.partial(_post_kernel, alpha=alpha, tiles_a=tiles_a),
        grid=(t // tm,),
        in_specs=[pl.BlockSpec((tm, D_MODEL), row_a),
                  pl.BlockSpec((tm, D_MODEL), row_b),
                  pl.BlockSpec((tm, V_W), row_a),
                  pl.BlockSpec((tm, V_W), row_b),
                  pl.BlockSpec((tm, D_MODEL), lambda i: (i, U_BLK + 1)),
                  pl.BlockSpec((tm, D_MODEL), lambda i: (i, U_BLK + 2)),
                  pl.BlockSpec((tm, D_MODEL), row_a),
                  pl.BlockSpec((tm, D_MODEL), row_b),
                  pl.BlockSpec(w_pool.shape, lambda i: (0, 0, 0)),
                  pl.BlockSpec((1, D_MODEL), const2),
                  pl.BlockSpec(w_br_b.shape, const2),
                  pl.BlockSpec(w_br_a.shape, const2),
                  pl.BlockSpec(w_out.shape, const2),
                  pl.BlockSpec((1, D_MODEL), const2),
                  pl.BlockSpec((1, D_MODEL), const2)],
        out_specs=[pl.BlockSpec((tm, D_MODEL), row), pl.BlockSpec((N_PIECE, tm, PIECE_W), lambda i: (0, i, 0))],
        out_shape=[jax.ShapeDtypeStruct((t, D_MODEL), F32),
                   jax.ShapeDtypeStruct((N_PIECE, t, PIECE_W), jnp.uint32)],
        compiler_params=_cparams(("parallel",)),
        name="post_mixers",
    )(pooled_a, pooled_b, o_a, o_b, proj, proj, x_a, x_b, w_pool, pool_scale, w_br_b, w_br_a, w_out, ln_g, ln_b)


def _router_kernel(h_ref, wt_ref, bias_ref, idx_ref, wts_ref, rank_ref, cnt_ref, carry_scr, *, tm):
    i = pl.program_id(0)

    @pl.when(i == 0)
    def _init():
        carry_scr[...] = jnp.zeros(carry_scr.shape, F32)

    logits = _dot3_nt(wt_ref[...], h_ref[...])
    sc = _sigmoid(logits)
    ch = sc + bias_ref[...]
    neg = -jnp.inf
    e_in = lax.broadcasted_iota(jnp.int32, (GROUP_SZ, tm), 0)
    gs_rows = []
    for g in range(N_GROUP):
        blk = ch[g * GROUP_SZ:(g + 1) * GROUP_SZ]
        m1 = jnp.max(blk, axis=0, keepdims=True)
        i1 = jnp.min(jnp.where(blk == m1, e_in, GROUP_SZ), axis=0, keepdims=True)
        m2 = jnp.max(jnp.where(e_in == i1, neg, blk), axis=0, keepdims=True)
        gs_rows.append(m1 + m2)
    gs = jnp.concatenate(gs_rows, axis=0)
    g_io = lax.broadcasted_iota(jnp.int32, (N_GROUP, tm), 0)
    e_io = lax.broadcasted_iota(jnp.int32, (N_EXPERTS, tm), 0)
    e_grp = e_io >> int(math.log2(GROUP_SZ))
    masked = jnp.full((N_EXPERTS, tm), neg, F32)
    for _ in range(TOPK_GROUP):
        mx = jnp.max(gs, axis=0, keepdims=True)
        gi = jnp.min(jnp.where(gs == mx, g_io, N_GROUP), axis=0, keepdims=True)
        gs = jnp.where(g_io == gi, neg, gs)
        masked = jnp.where(e_grp == gi, ch, masked)
    idx_rows, w_rows = [], []
    onehot = jnp.zeros((N_EXPERTS, tm), F32)
    for _ in range(TOP_K):
        mx = jnp.max(masked, axis=0, keepdims=True)
        ei = jnp.min(jnp.where(masked == mx, e_io, N_EXPERTS), axis=0, keepdims=True)
        hit = e_io == ei
        idx_rows.append(ei)
        w_rows.append(jnp.sum(jnp.where(hit, sc, 0.0), axis=0, keepdims=True))
        onehot = jnp.where(hit, 1.0, onehot)
        masked = jnp.where(hit, neg, masked)
    wsel = jnp.concatenate(w_rows, axis=0)
    wts_ref[...] = wsel / jnp.sum(wsel, axis=0, keepdims=True) * ROUTED_SCALE
    idx_ref[...] = jnp.concatenate(idx_rows, axis=0)

    tr = lax.broadcasted_iota(jnp.int32, (tm, tm), 0)
    tc_ = lax.broadcasted_iota(jnp.int32, (tm, tm), 1)
    before = jnp.where(tr < tc_, 1.0, 0.0).astype(BF16)
    cum = jnp.dot(onehot.astype(BF16), before, preferred_element_type=F32) + carry_scr[...]
    rank_ref[...] = jnp.concatenate(
        [jnp.sum(jnp.where(e_io == ei, cum, 0.0), axis=0, keepdims=True) for ei in idx_rows],
        axis=0).astype(jnp.int32)
    carry_scr[...] = carry_scr[...] + jnp.sum(onehot, axis=1, keepdims=True)
    cnt_ref[...] = carry_scr[...]


def _router(h, w_router_t, bias_col, tm):
    t = h.shape[0]
    return pl.pallas_call(
        functools.partial(_router_kernel, tm=tm),
        grid=(t // tm,),
        in_specs=[pl.BlockSpec((tm, D_MODEL), lambda i: (i, 0)),
                  pl.BlockSpec((N_EXPERTS, D_MODEL), lambda i: (0, 0)),
                  pl.BlockSpec((N_EXPERTS, 1), lambda i: (0, 0))],
        out_specs=[pl.BlockSpec((TOP_K, tm), lambda i: (0, i)),
                   pl.BlockSpec((TOP_K, tm), lambda i: (0, i)),
                   pl.BlockSpec((TOP_K, tm), lambda i: (0, i)),
                   pl.BlockSpec((N_EXPERTS, 1), lambda i: (0, 0))],
        out_shape=[jax.ShapeDtypeStruct((TOP_K, t), jnp.int32),
                   jax.ShapeDtypeStruct((TOP_K, t), F32),
                   jax.ShapeDtypeStruct((TOP_K, t), jnp.int32),
                   jax.ShapeDtypeStruct((N_EXPERTS, 1), F32)],
        scratch_shapes=[pltpu.VMEM((N_EXPERTS, 1), F32)],
        compiler_params=_cparams(("arbitrary",)),
        name="router",
    )(h, w_router_t, bias_col)


def _dest_kernel(idx_ref, rank_ref, start_ref, dest_ref, *, tm, n_rows):
    e_io = lax.broadcasted_iota(jnp.int32, (N_EXPERTS, tm), 0)
    starts = start_ref[...]
    rows = []
    for k in range(TOP_K):
        seg = jnp.sum(jnp.where(e_io == idx_ref[k:k + 1, :], starts, 0.0), axis=0, keepdims=True)
        rows.append(seg.astype(jnp.int32) + rank_ref[k:k + 1, :])
    base = jnp.concatenate(rows, axis=0)
    for p in range(N_PIECE):
        dest_ref[p] = base + p * n_rows


def _dest(idx_t, rank_t, seg_start_col, n_rows, tm):
    t = idx_t.shape[1]
    blk = pl.BlockSpec((TOP_K, tm), lambda i: (0, i))
    return pl.pallas_call(
        functools.partial(_dest_kernel, tm=tm, n_rows=n_rows),
        grid=(t // tm,),
        in_specs=[blk, blk, pl.BlockSpec((N_EXPERTS, 1), lambda i: (0, 0))],
        out_specs=pl.BlockSpec((N_PIECE, TOP_K, tm), lambda i: (0, 0, i)),
        out_shape=jax.ShapeDtypeStruct((N_PIECE, TOP_K, t), jnp.int32),
        compiler_params=_cparams(("parallel",)),
        name="dispatch_rows",
    )(idx_t, rank_t, seg_start_col)


def _expert_kernel(be_ref, nu_ref, first_ref, nxt_ref, slot_ref, x_ref, wg_hbm, wu_hbm, wd_hbm, y_ref,
                   wg_buf, wu_buf, wd_buf, wgu_scr, wd_scr, sems):
    i = pl.program_id(0)
    live = i < nu_ref[0]

    def fetch(e, slot):
        return [pltpu.make_async_copy(wg_hbm.at[e], wg_buf.at[slot], sems.at[slot, 0]),
                pltpu.make_async_copy(wu_hbm.at[e], wu_buf.at[slot], sems.at[slot, 1]),
                pltpu.make_async_copy(wd_hbm.at[e], wd_buf.at[slot], sems.at[slot, 2])]

    @pl.when(live & (first_ref[i] == 1))
    def _new_expert():
        slot = slot_ref[i]

        @pl.when(i == 0)
        def _():
            for cp in fetch(be_ref[i], slot):
                cp.start()

        @pl.when(nxt_ref[i] >= 0)
        def _():
            for cp in fetch(nxt_ref[i], 1 - slot):
                cp.start()

        for cp in fetch(be_ref[i], slot):
            cp.wait()
        wgu_scr[:, :D_EXPERT] = wg_buf[slot].astype(BF16)
        wgu_scr[:, D_EXPERT:] = wu_buf[slot].astype(BF16)
        wd_scr[...] = wd_buf[slot].astype(BF16)

    @pl.when(live)
    def _():
        n_sub = 2
        sub = EXPERT_BM // n_sub
        xs = [jnp.concatenate([c.astype(BF16) for c in
                               _load_pieces([x_ref[p, s * sub:(s + 1) * sub] for p in range(N_PIECE)])], axis=1)
              for s in range(n_sub)]
        gus = [jnp.dot(x, wgu_scr[...], preferred_element_type=F32) for x in xs]
        acts = [(_silu(gu[:, :D_EXPERT]) * gu[:, D_EXPERT:]).astype(BF16) for gu in gus]
        ys = [jnp.dot(a, wd_scr[...], preferred_element_type=F32) for a in acts]
        for s, y in enumerate(ys):
            for p in range(N_PIECE):
                y_ref[p, s * sub:(s + 1) * sub] = _pack_halves(y[:, 2 * p * PIECE_W:2 * (p + 1) * PIECE_W])


def _experts(xs, blk_exp, n_used, blk_first, blk_next, blk_slot, w_gate, w_up, w_down):
    n_rows = xs.shape[1]
    n_blocks = n_rows // EXPERT_BM

    def row_block(i, be, nu, *_):
        return (0, jnp.minimum(i, nu[0] - 1), 0)

    grid_spec = pltpu.PrefetchScalarGridSpec(
        num_scalar_prefetch=5,
        grid=(n_blocks,),
        in_specs=[pl.BlockSpec((N_PIECE, EXPERT_BM, PIECE_W), row_block),
                  pl.BlockSpec(memory_space=pl.ANY),
                  pl.BlockSpec(memory_space=pl.ANY),
                  pl.BlockSpec(memory_space=pl.ANY)],
        out_specs=pl.BlockSpec((N_PIECE, EXPERT_BM, PIECE_W), row_block),
        scratch_shapes=[pltpu.VMEM((2, D_MODEL, D_EXPERT), F32),
                        pltpu.VMEM((2, D_MODEL, D_EXPERT), F32),
                        pltpu.VMEM((2, D_EXPERT, D_MODEL), F32),
                        pltpu.VMEM((D_MODEL, 2 * D_EXPERT), BF16),
                        pltpu.VMEM((D_EXPERT, D_MODEL), BF16),
                        pltpu.SemaphoreType.DMA((2, 3))],
    )
    return pl.pallas_call(
        _expert_kernel,
        grid_spec=grid_spec,
        out_shape=jax.ShapeDtypeStruct((N_PIECE, n_rows, PIECE_W), jnp.uint32),
        compiler_params=_cparams(("arbitrary",)),
        name="experts",
    )(blk_exp, n_used, blk_first, blk_next, blk_slot, xs, w_gate, w_up, w_down)


SC_WINDOW = 128
V7X_SC_CORES = 2
V7X_SC_SUBCORES = 16


def _sc_mesh():
    return plsc.VectorSubcoreMesh(core_axis_name="core", subcore_axis_name="subcore",
                                  num_cores=V7X_SC_CORES, num_subcores=V7X_SC_SUBCORES)


def _sc_scatter_rows(src, dest, n_rows, seg, repeat):
    d = src.shape[1]
    n_idx = dest.shape[0]
    seg_blocks = seg // SC_WINDOW
    dest2 = dest.reshape(1, n_idx)

    def src_block(i):
        return ((i // (repeat * seg_blocks)) * seg_blocks + i % seg_blocks, 0)

    @functools.partial(pl.kernel, out_type=jax.ShapeDtypeStruct((n_rows, d), src.dtype), mesh=_sc_mesh(),
                       scratch_types=[], name="sc_dispatch")
    def run(src_hbm, idx_hbm, out_hbm):
        def body(rows_vmem, idx_vmem):
            pltpu.sync_copy(rows_vmem, out_hbm.at[idx_vmem.at[0]])

        pltpu.emit_pipeline(
            body,
            grid=(n_idx // SC_WINDOW,),
            in_specs=[pl.BlockSpec((SC_WINDOW, d), src_block),
                      pl.BlockSpec((1, SC_WINDOW), lambda i: (0, i))],
            out_specs=[],
            core_axis_name=("core", "subcore"),
            dimension_semantics=(pltpu.PARALLEL,),
        )(src_hbm, idx_hbm)

    return run(src, dest2)


def _sc_gather_rows(table, idx):
    d = table.shape[1]
    n_idx = idx.shape[0]
    idx2 = idx.reshape(1, n_idx)

    @functools.partial(pl.kernel, out_type=jax.ShapeDtypeStruct((n_idx, d), table.dtype), mesh=_sc_mesh(),
                       scratch_types=[], name="sc_combine_gather")
    def run(table_hbm, idx_hbm, out_hbm):
        def body(idx_vmem, rows_vmem):
            pltpu.sync_copy(table_hbm.at[idx_vmem.at[0]], rows_vmem)

        pltpu.emit_pipeline(
            body,
            grid=(n_idx // SC_WINDOW,),
            in_specs=[pl.BlockSpec((1, SC_WINDOW), lambda i: (0, i))],
            out_specs=[pl.BlockSpec((SC_WINDOW, d), lambda i: (i, 0))],
            core_axis_name=("core", "subcore"),
            dimension_semantics=(pltpu.PARALLEL,),
        )(idx_hbm, out_hbm)

    return run(table, idx2)


def _combine_kernel(yg_ref, wts_ref, h_ref, wgu_ref, wd_ref, g_ref, b_ref, outa_ref, outb_ref, *, alpha, tiles_a):
    wts = wts_ref[...]
    acc = None
    for k in range(TOP_K):
        cols = _load_pieces([yg_ref[p, k] for p in range(N_PIECE)])
        wk = wts[:, k:k + 1]
        acc = [c * wk for c in cols] if acc is None else [a + c * wk for a, c in zip(acc, cols)]
    routed = jnp.concatenate(acc, axis=1)
    h = h_ref[...]
    gu = _dot(h, wgu_ref[...])
    shared = _dot(_silu(gu[:, :D_EXPERT]) * gu[:, D_EXPERT:], wd_ref[...])
    y = _layer_norm(alpha * h + (routed + shared), g_ref[...], b_ref[...])
    i = pl.program_id(0)

    @pl.when(i < tiles_a)
    def _():
        outa_ref[...] = y

    @pl.when(i >= tiles_a)
    def _():
        outb_ref[...] = y


def _combine(yg, wts, h, w_sh_gu, w_sh_down, ln_g, ln_b, alpha, tm, t_a):
    t = h.shape[0]
    tiles_a = t_a // tm
    row = lambda i: (i, 0)
    const2 = lambda i: (0, 0)
    return pl.pallas_call(
        functools.partial(_combine_kernel, alpha=alpha, tiles_a=tiles_a),
        grid=(t // tm,),
        in_specs=[pl.BlockSpec((N_PIECE, TOP_K, tm, PIECE_W), lambda i: (0, 0, i, 0)),
                  pl.BlockSpec((tm, TOP_K), row),
                  pl.BlockSpec((tm, D_MODEL), row),
                  pl.BlockSpec(w_sh_gu.shape, const2),
                  pl.BlockSpec(w_sh_down.shape, const2),
                  pl.BlockSpec((1, D_MODEL), const2),
                  pl.BlockSpec((1, D_MODEL), const2)],
        out_specs=[pl.BlockSpec((tm, D_MODEL), lambda i: (jnp.minimum(i, tiles_a - 1), 0)),
                   pl.BlockSpec((tm, D_MODEL), lambda i: (jnp.maximum(i - tiles_a, 0), 0))],
        out_shape=[jax.ShapeDtypeStruct((t_a, D_MODEL), F32), jax.ShapeDtypeStruct((t - t_a, D_MODEL), F32)],
        compiler_params=_cparams(("arbitrary",)),
        name="combine_ln2",
    )(yg, wts, h, w_sh_gu, w_sh_down, ln_g, ln_b)


def _layer(xp, xs, s_delta, s_conv, s_pool, w_in, conv_w, a_log, dt_bias, gamma_a, w_br_a, w_pool,
           pool_scale, w_br_b, w_out, ln1_g, ln1_b, w_router, router_bias, w_exp_gate, w_exp_up,
           w_exp_down, w_sh_gate, w_sh_up, w_sh_down, ln2_g, ln2_b, alpha):
    Bp, Lp, _ = xp.shape
    Bs, Ls, _ = xs.shape
    Tp, Ts = Bp * Lp, Bs * Ls
    T = Tp + Ts
    x_p = xp.reshape(Tp, D_MODEL)
    x_s = xs.reshape(Ts, D_MODEL)

    o_z, o_b, o_a, o_u = QKV_W, QKV_W + V_W, QKV_W + V_W + H_V, QKV_W + V_W + 2 * H_V
    w_main = jnp.concatenate([w_in[:, :o_b], w_in[:, o_u:]], axis=1).astype(BF16)
    w_ba = jnp.pad(w_in[:, o_b:o_u], ((0, 0), (0, 128 - 2 * H_V))).astype(BF16)
    proj, ba = _in_proj(x_p, x_s, w_main, w_ba, 1024, MAIN_W // 3)

    o_p, sd_p = _gdn(proj, ba, conv_w, a_log, dt_bias, gamma_a, 0, Bp, Lp, 64, 1, GDN_FRESH_CHUNKS)
    o_s, sd_s = _gdn(proj, ba, conv_w, a_log, dt_bias, gamma_a, Tp, Bs, Ls, Ls, GDN_STATE_SEQS, 1,
                     conv_buf=s_conv, s0=s_delta)

    pooled_p = _pool(proj, 0, Bp, Lp, 256, 1, 0)
    pooled_s = _pool(proj, Tp, Bs, Ls, Ls, POOL_STATE_SEQS, PAST_LEN, pool_buf=s_pool)

    h, hp = _post(pooled_p, pooled_s, o_p, o_s, proj, x_p, x_s, w_pool.astype(BF16), pool_scale.reshape(1, D_MODEL),
                  w_br_b.astype(BF16), w_br_a.astype(BF16), w_out.astype(BF16),
                  ln1_g.reshape(1, D_MODEL), ln1_b.reshape(1, D_MODEL), alpha, 256)

    idx_t, wts_t, rank_t, cnt = _router(h, w_router.T, router_bias.reshape(N_EXPERTS, 1), 512)
    counts = cnt[:, 0].astype(jnp.int32)
    padded = ((counts + EXPERT_BM - 1) // EXPERT_BM) * EXPERT_BM
    pends = jnp.cumsum(padded)
    pstarts = pends - padded
    n_blocks = (T * TOP_K + N_EXPERTS * (EXPERT_BM - 1) + EXPERT_BM - 1) // EXPERT_BM
    n_rows = n_blocks * EXPERT_BM
    piece_rows = _dest(idx_t, rank_t, pstarts.astype(F32).reshape(N_EXPERTS, 1), n_rows, 512)
    blk_start = jnp.arange(n_blocks, dtype=jnp.int32) * EXPERT_BM
    blk_exp = jnp.minimum(jnp.sum((pends[None, :] <= blk_start[:, None]).astype(jnp.int32), axis=1),
                          N_EXPERTS - 1)
    n_used = (pends[-1:] // EXPERT_BM).astype(jnp.int32)
    has_rows = counts > 0
    e_ids = jnp.arange(N_EXPERTS, dtype=jnp.int32)
    slot_of = (jnp.cumsum(has_rows.astype(jnp.int32)) - 1) & 1
    later = jnp.where(has_rows, e_ids, N_EXPERTS)
    next_of = jnp.concatenate([lax.cummin(later, reverse=True)[1:], jnp.full((1,), N_EXPERTS, jnp.int32)])
    next_of = jnp.where(next_of < N_EXPERTS, next_of, -1)
    blk_first = (blk_start == pstarts[blk_exp]).astype(jnp.int32)
    blk_next = next_of[blk_exp]
    blk_slot = slot_of[blk_exp]

    piece_idx = piece_rows.reshape(N_PIECE * TOP_K * T)
    x_sorted = _sc_scatter_rows(hp.reshape(N_PIECE * T, PIECE_W), piece_idx, N_PIECE * n_rows, T, TOP_K)
    y_sorted = _experts(x_sorted.reshape(N_PIECE, n_rows, PIECE_W), blk_exp, n_used, blk_first, blk_next,
                        blk_slot, w_exp_gate, w_exp_up, w_exp_down)
    yg = _sc_gather_rows(y_sorted.reshape(N_PIECE * n_rows, PIECE_W), piece_idx)
    yg = yg.reshape(N_PIECE, TOP_K, T, PIECE_W)

    w_sh_gu = jnp.concatenate([w_sh_gate, w_sh_up], axis=1).astype(BF16)
    y_p, y_s = _combine(yg, wts_t.T, h, w_sh_gu, w_sh_down.astype(BF16),
                        ln2_g.reshape(1, D_MODEL), ln2_b.reshape(1, D_MODEL), alpha, 512, Tp)

    def carried_rows(keep, col0, width, name, old_p, old_s):
        parts, counts = [], []
        for x2d, B, L in ((x_p, Bp, Lp), (x_s, Bs, Ls)):
            n = min(L, keep)
            rows = jnp.arange(B, dtype=jnp.int32)[:, None] * L + (L - n) + jnp.arange(n, dtype=jnp.int32)
            parts.append(jnp.take(x2d, rows.reshape(-1), axis=0).astype(BF16))
            counts.append((B, n))
        n_tail = sum(B * n for B, n in counts)
        n_pad = -(-n_tail // 128) * 128
        xt = jnp.pad(jnp.concatenate(parts, axis=0), ((0, n_pad - n_tail), (0, 0)))
        tail = _matmul(xt, w_main, col0, width, n_pad, 1024, F32, name)
        outs, r0 = [], 0
        for (B, n), old in zip(counts, (old_p, old_s)):
            new = tail[r0:r0 + B * n].reshape(B, n, width)
            outs.append(new if n == keep else jnp.concatenate([old[:, n:], new], axis=1))
            r0 += B * n
        return outs

    new_conv_p, new_conv_s = carried_rows(CONV_W - 1, 0, QKV_W, "in_proj_conv_rows",
                                          jnp.zeros((Bp, CONV_W - 1, QKV_W), F32), s_conv)
    new_pool_p, new_pool_s = carried_rows(POOL_BUF, U_BLK * D_MODEL, D_MODEL, "in_proj_pool_rows",
                                          jnp.zeros((Bp, POOL_BUF, D_MODEL), F32), s_pool)
    return (y_p.reshape(Bp, Lp, D_MODEL), y_s.reshape(Bs, Ls, D_MODEL),
            sd_p, new_conv_p, new_pool_p, sd_s, new_conv_s, new_pool_s)


def kernel(x_prompt, x_sample, state_delta, state_conv, state_pool, w_in, conv_w, a_log, dt_bias, gamma_a,
           w_br_a, w_pool, pool_scale, w_br_b, w_out, ln1_g, ln1_b, w_router, router_bias,
           w_exp_gate, w_exp_up, w_exp_down, w_sh_gate, w_sh_up, w_sh_down, ln2_g, ln2_b):
    depth = w_in.shape[0]
    alpha = (2 * depth) ** 0.25
    yp, ys = x_prompt, x_sample
    outs = [[] for _ in range(6)]
    for l in range(depth):
        res = _layer(yp, ys, state_delta[l], state_conv[l], state_pool[l], w_in[l], conv_w[l], a_log[l],
                     dt_bias[l], gamma_a[l], w_br_a[l], w_pool[l], pool_scale[l], w_br_b[l], w_out[l],
                     ln1_g[l], ln1_b[l], w_router[l], router_bias[l], w_exp_gate[l], w_exp_up[l],
                     w_exp_down[l], w_sh_gate[l], w_sh_up[l], w_sh_down[l], ln2_g[l], ln2_b[l], alpha)
        yp, ys = res[0], res[1]
        for lst, v in zip(outs, res[2:]):
            lst.append(v)
    return (yp, ys) + tuple(jnp.stack(v) for v in outs)
```

```python
import functools
import math

import jax
import jax.numpy as jnp
from jax import lax
from jax.experimental import pallas as pl
from jax.experimental.pallas import tpu as pltpu
from jax.experimental.pallas import tpu_sc as plsc

F32 = jnp.float32
BF16 = jnp.bfloat16

D_MODEL = 1024
H_K = 8
D_K = 128
H_V = 16
D_V = 128
Q_W = H_K * D_K
V_W = H_V * D_V
QKV_W = 2 * Q_W + V_W
CONV_W = 4
POOL_WINDOWS = (2, 4, 8, 16)
POOL_GROUP_W = D_MODEL // len(POOL_WINDOWS)
POOL_BUF = max(POOL_WINDOWS) - 1
N_EXPERTS = 256
TOP_K = 8
N_GROUP = 8
TOPK_GROUP = 4
GROUP_SZ = N_EXPERTS // N_GROUP
D_EXPERT = D_MODEL // 4
ROUTED_SCALE = 2.5
LN_EPS = 1e-5
NORM_EPS = 1e-6
PAST_LEN = 16384

MAIN_W = QKV_W + V_W + 3 * D_MODEL
Z_BLK = QKV_W // V_W
U_BLK = (QKV_W + V_W) // D_MODEL

IN_PROJ_TM = 1024
IN_PROJ_TN = MAIN_W // 3
GDN_CHUNK = 64
POOL_TILE = 256
POST_TM = 256
ROUTER_TM = 512
COMBINE_TM = 512

STACK_ROWS = 128
SOLVE_BASE = 16
GDN_FRESH_CHUNKS = 4
EXPERT_BM = 512
GDN_STATE_SEQS = 8
POOL_STATE_SEQS = 16
VMEM_LIMIT = 56 * 1024 * 1024


def _cparams(sem):
    return pltpu.CompilerParams(dimension_semantics=sem, vmem_limit_bytes=VMEM_LIMIT)


def _sigmoid(x):
    return 0.5 * jnp.tanh(0.5 * x) + 0.5


def _silu(x):
    return x * _sigmoid(x)


def _softplus(x):
    return jnp.maximum(x, 0.0) + jnp.log(1.0 + jnp.exp(-jnp.abs(x)))


def _dot(a, b):
    return jnp.dot(a.astype(BF16), b.astype(BF16), preferred_element_type=F32)


def _dot_nt(a, b):
    return lax.dot_general(a.astype(BF16), b.astype(BF16), (((1,), (1,)), ((), ())),
                           preferred_element_type=F32)


def _dot_tn(a, b):
    return lax.dot_general(a.astype(BF16), b.astype(BF16), (((0,), (0,)), ((), ())),
                           preferred_element_type=F32)


def _split(a):
    hi = a.astype(BF16)
    lo = (a - hi.astype(F32)).astype(BF16)
    return hi, lo


def _dot3_nt(a, b):
    ah, al = _split(a)
    bh, bl = _split(b)
    d = functools.partial(lax.dot_general, dimension_numbers=(((1,), (1,)), ((), ())),
                          preferred_element_type=F32)
    return d(ah, bh) + (d(ah, bl) + d(al, bh))


def _pack_halves(x):
    n = x.shape[1] // 2
    hi = lax.bitcast_convert_type(x[:, :n].astype(BF16).astype(F32), jnp.uint32)
    lo = lax.bitcast_convert_type(x[:, n:].astype(BF16).astype(F32), jnp.uint32)
    return (hi & jnp.uint32(0xFFFF0000)) | (lo >> 16)


def _unpack_halves(w):
    hi = lax.bitcast_convert_type(w & jnp.uint32(0xFFFF0000), F32)
    lo = lax.bitcast_convert_type(w << 16, F32)
    return hi, lo


N_PIECE = 2
PIECE_W = D_MODEL // (2 * N_PIECE)


def _store_pieces(ref, x):
    for p in range(N_PIECE):
        ref[p] = _pack_halves(x[:, 2 * p * PIECE_W:2 * (p + 1) * PIECE_W])


def _load_pieces(pieces):
    cols = []
    for w in pieces:
        cols.extend(_unpack_halves(w))
    return cols


def _layer_norm(x, g, b):
    mu = jnp.mean(x, axis=-1, keepdims=True)
    xc = x - mu
    var = jnp.mean(xc * xc, axis=-1, keepdims=True)
    return xc * lax.rsqrt(var + LN_EPS) * g + b


def _in_proj_kernel(xa_ref, xb_ref, w_ref, wba_ref, o_ref, ba_ref, x_scr, *, tiles_a):
    @pl.when(pl.program_id(1) == 0)
    def _():
        x = jnp.where(pl.program_id(0) < tiles_a, xa_ref[...], xb_ref[...]).astype(BF16)
        x_scr[...] = x
        ba_ref[...] = jnp.dot(x, wba_ref[...], preferred_element_type=F32)

    o_ref[...] = jnp.dot(x_scr[...], w_ref[...], preferred_element_type=F32).astype(o_ref.dtype)


def _in_proj(x_a, x_b, w_main, w_ba, tm, tn):
    k = x_a.shape[1]
    t = x_a.shape[0] + x_b.shape[0]
    n = w_main.shape[1]
    tiles_a = x_a.shape[0] // tm
    return pl.pallas_call(
        functools.partial(_in_proj_kernel, tiles_a=tiles_a),
        grid=(t // tm, n // tn),
        in_specs=[pl.BlockSpec((tm, k), lambda i, j: (jnp.minimum(i, tiles_a - 1), 0)),
                  pl.BlockSpec((tm, k), lambda i, j: (jnp.maximum(i - tiles_a, 0), 0)),
                  pl.BlockSpec((k, tn), lambda i, j: (0, j)),
                  pl.BlockSpec(w_ba.shape, lambda i, j: (0, 0))],
        out_specs=[pl.BlockSpec((tm, tn), lambda i, j: (i, j)),
                   pl.BlockSpec((tm, w_ba.shape[1]), lambda i, j: (i, 0))],
        out_shape=[jax.ShapeDtypeStruct((t, n), BF16), jax.ShapeDtypeStruct((t, w_ba.shape[1]), F32)],
        scratch_shapes=[pltpu.VMEM((tm, k), BF16)],
        compiler_params=_cparams(("parallel", "arbitrary")),
        name="in_proj",
    )(x_a, x_b, w_main, w_ba)


def _mm_kernel(x_ref, w_ref, o_ref):
    o_ref[...] = jnp.dot(x_ref[...], w_ref[...], preferred_element_type=F32).astype(o_ref.dtype)


def _matmul(x, w, col0, n, tm, tn, out_dtype, name):
    t, k = x.shape
    cb0 = col0 // tn
    return pl.pallas_call(
        _mm_kernel,
        grid=(t // tm, n // tn),
        in_specs=[pl.BlockSpec((tm, k), lambda i, j: (i, 0)),
                  pl.BlockSpec((k, tn), lambda i, j: (0, cb0 + j))],
        out_specs=pl.BlockSpec((tm, tn), lambda i, j: (i, j)),
        out_shape=jax.ShapeDtypeStruct((t, n), out_dtype),
        compiler_params=_cparams(("parallel", "parallel")),
        name=name,
    )(x, w)


def _gdn_kernel(*refs, C, hg, nseq, nck, has_state):
    if has_state:
        (qkv_ref, z_ref, ba_ref, cw_ref, ad_ref, gam_ref, buf_ref, s0_ref,
         o_ref, sout_ref, s_scr, tail_scr, act_scr) = refs
    else:
        (qkv_ref, z_ref, ba_ref, cw_ref, ad_ref, gam_ref,
         o_ref, sout_ref, s_scr, tail_scr, act_scr) = refs
    R = hg * C
    ng = H_V // hg
    log2c = int(math.log2(C))
    c = pl.program_id(1)

    @pl.when(c == 0)
    def _init():
        if has_state:
            tail_scr[...] = jnp.zeros(tail_scr.shape, F32)
            for s in range(nseq):
                s_scr[s * H_V:(s + 1) * H_V] = s0_ref[s]
                tail_scr[(s + 1) * 8 - (CONV_W - 1):(s + 1) * 8] = buf_ref[s]
        else:
            s_scr[...] = jnp.zeros(s_scr.shape, F32)
            tail_scr[...] = jnp.zeros(tail_scr.shape, F32)

    cw = cw_ref[...]
    qkv_all = qkv_ref[...].astype(F32)
    n_prev = CONV_W - 1
    CB = nck * C
    if not has_state:
        ext_pad = 128
        sr = lax.broadcasted_iota(jnp.int32, (n_prev * C, ext_pad), 0)
        sc = lax.broadcasted_iota(jnp.int32, (n_prev * C, ext_pad), 1)
        shift_sel = jnp.zeros((n_prev * C, ext_pad), F32)
        for j in range(n_prev):
            in_tap = (sr >= j * C) & (sr < (j + 1) * C)
            shift_sel = jnp.where(in_tap & (sc == sr - j * C + 8 - n_prev + j), 1.0, shift_sel)
        shift_sel = shift_sel.astype(BF16)
        ext_zeros = jnp.zeros((ext_pad - 8 - C, QKV_W), F32)
    for s in range(nseq):
        x = qkv_all[s * CB:(s + 1) * CB]
        xe = jnp.concatenate([tail_scr[s * 8:(s + 1) * 8], x], axis=0)
        y = x * cw[n_prev:CONV_W]
        if has_state:
            for j in range(n_prev):
                y = y + xe[8 - n_prev + j:8 - n_prev + j + CB] * cw[j:j + 1]
        else:
            moved = [jnp.dot(shift_sel, jnp.concatenate([xe[k * C:k * C + 8 + C], ext_zeros], axis=0).astype(BF16),
                             preferred_element_type=F32) for k in range(nck)]
            for j in range(n_prev):
                y = y + jnp.concatenate([m[j * C:(j + 1) * C] for m in moved], axis=0) * cw[j:j + 1]
        tail_scr[s * 8:(s + 1) * 8] = x[CB - 8:CB]
        act_scr[s * CB:(s + 1) * CB] = _silu(y)
        for off, scale in ((0, D_K ** -0.5), (Q_W, 1.0)):
            for kh in range(H_K):
                cols = slice(off + kh * D_K, off + (kh + 1) * D_K)
                v = act_scr[s * CB:(s + 1) * CB, cols]
                act_scr[s * CB:(s + 1) * CB, cols] = v * (
                    lax.rsqrt(jnp.sum(v * v, axis=-1, keepdims=True) + NORM_EPS) * scale)

    ri = lax.broadcasted_iota(jnp.int32, (R, R), 0)
    ci = lax.broadcasted_iota(jnp.int32, (R, R), 1)
    same = (ri >> log2c) == (ci >> log2c)
    incl = same & (ri >= ci)
    strict = same & (ri > ci)
    eye = ri == ci
    is_last = ci == (((ri >> log2c) << log2c) + (C - 1))
    eye_f = jnp.where(eye, 1.0, 0.0).astype(F32)
    cum_u = jnp.where(same & (ri <= ci), 1.0, 0.0).astype(BF16)

    nrow = ba_ref.shape[2] // 2
    adv = ad_ref[...]
    dd = functools.partial(jnp.dot, preferred_element_type=F32)
    beta_of, G_of = {}, {}
    for s in range(nseq):
        for k in range(nck):
            bav = ba_ref[s, k]
            beta_of[s, k] = _sigmoid(bav[0:nrow])
            g_rows = -jnp.exp(adv[0:nrow]) * _softplus(bav[nrow:2 * nrow] + adv[nrow:2 * nrow])
            g1 = g_rows.astype(BF16)
            r1 = g_rows - g1.astype(F32)
            g2 = r1.astype(BF16)
            g3 = (r1 - g2.astype(F32)).astype(BF16)
            G_of[s, k] = dd(g1, cum_u) + (dd(g2, cum_u) + dd(g3, cum_u))

    gam = gam_ref[...]
    rep = H_V // H_K

    def stacked(d, off, per_k_head):
        blocks = [(h // rep if per_k_head else h) for h in d["heads"]]
        return jnp.concatenate([act_scr[d["rows"], off + b * D_K:off + (b + 1) * D_K] for b in blocks], axis=0)

    log2b = min(log2c, int(math.log2(SOLVE_BASE)))
    same_base = ((ri ^ ci) >> log2b) == 0
    row_head = lax.broadcasted_iota(jnp.int32, (R, D_V), 0) >> log2c
    z_all = z_ref[...].astype(F32)
    grp = [dict(s=s, k=k, g=g, rows=slice(s * CB + k * C, s * CB + (k + 1) * C), s0=s * H_V,
                heads=[g * hg + hh for hh in range(hg)])
           for k in range(nck) for s in range(nseq) for g in range(ng)]
    for d in grp:
        kst = stacked(d, Q_W, True)
        d["kk"] = _dot_nt(kst, kst)
        d["qk"] = _dot_nt(stacked(d, 0, True), kst)
    for d in grp:
        grow = G_of[d["s"], d["k"]][d["g"]:d["g"] + 1]
        brow = beta_of[d["s"], d["k"]][d["g"]:d["g"] + 1]
        gcol = jnp.sum(eye_f * grow, axis=1, keepdims=True)
        bcol = jnp.sum(eye_f * brow, axis=1, keepdims=True)
        glast = jnp.sum(jnp.where(is_last, grow, 0.0), axis=1, keepdims=True)
        dm = jnp.where(incl, jnp.exp(jnp.minimum(gcol - grow, 0.0)), 0.0)
        d.update(gcol=gcol, bcol=bcol, glast=glast, eg=jnp.exp(gcol))
        d["m"] = jnp.where(strict, -(d["kk"] * bcol * dm), 0.0)
        d["attn"] = jnp.where(incl, d["qk"] * dm, 0.0)
        d["pw"] = jnp.where(same_base, d["m"], 0.0)
        d["t"] = eye_f + d["pw"]

    for r in range(log2b):
        last = r == log2b - 1
        for d in grp:
            if r == 0:
                if not last:
                    d["pw"] = _dot(d["pw"], d["pw"])
            elif last:
                d["t"] = d["t"] + _dot(d["pw"], d["t"])
            else:
                out = _dot(d["pw"], jnp.concatenate([d["pw"], d["t"]], axis=1))
                d["pw"] = out[:, :R]
                d["t"] = d["t"] + out[:, R:]
    for lev in range(log2b + 1, log2c + 1):
        coupling = ((ri ^ ci) >> (lev - 1)) == 1
        for d in grp:
            d["y"] = _dot(d["t"], jnp.where(coupling, d["m"], 0.0))
        for d in grp:
            d["t"] = d["t"] + _dot(d["y"], d["t"])
    for d in grp:
        rhs = jnp.concatenate([stacked(d, 2 * Q_W, False) * d["bcol"],
                               stacked(d, Q_W, True) * (d["bcol"] * d["eg"])], axis=1)
        d["x"] = _dot(d["t"], rhs)

    for k in range(nck):
        now = [d for d in grp if d["k"] == k]
        for d in now:
            qe = stacked(d, 0, True) * d["eg"]
            d["xs"] = [_dot(jnp.concatenate([d["x"][hh * C:(hh + 1) * C, D_V:], qe[hh * C:(hh + 1) * C]], axis=0),
                            s_scr[d["s0"] + h]) for hh, h in enumerate(d["heads"])]
        for d in now:
            d["vnew"] = jnp.concatenate(
                [d["x"][hh * C:(hh + 1) * C, :D_V] - xs[:C] for hh, xs in enumerate(d["xs"])], axis=0)
            d["o"] = jnp.concatenate([xs[C:] for xs in d["xs"]], axis=0) + _dot(d["attn"], d["vnew"])
        for d in now:
            kt = stacked(d, Q_W, True) * jnp.exp(d["glast"] - d["gcol"])
            egl = jnp.exp(d["glast"])
            for hh, h in enumerate(d["heads"]):
                sl = slice(hh * C, (hh + 1) * C)
                if C >= 16:
                    upd = _dot_tn(kt[sl], d["vnew"][sl])
                else:
                    upd = _dot_tn(kt, jnp.where(row_head == hh, d["vnew"], 0.0))
                s_scr[d["s0"] + h] = s_scr[d["s0"] + h] * egl[hh * C:hh * C + 1] + upd

    for d in grp:
        o = d["o"]
        zst = jnp.concatenate([z_all[d["rows"], h * D_V:(h + 1) * D_V] for h in d["heads"]], axis=0)
        on = o * lax.rsqrt(jnp.mean(o * o, axis=-1, keepdims=True) + NORM_EPS) * gam * _silu(zst)
        for hh, h in enumerate(d["heads"]):
            o_ref[d["rows"], h * D_V:(h + 1) * D_V] = on[hh * C:(hh + 1) * C].astype(o_ref.dtype)

    @pl.when(c == pl.num_programs(1) - 1)
    def _fin():
        for s in range(nseq):
            sout_ref[s] = s_scr[s * H_V:(s + 1) * H_V]


def _gdn(proj, ba, conv_w, a_log, dt_bias, gamma_a, row0, B, L, C, nseq, nck, conv_buf=None, s0=None):
    has_state = s0 is not None
    hg = STACK_ROWS // C
    ng = H_V // hg
    nrow = 16
    nc = L // C
    R = STACK_ROWS
    assert nseq == 1 or nc == nck
    rows_blk = nseq * nck * C
    steps = nc // nck

    def arrange(v):
        v = v.reshape(B, nc, C, ng, hg).transpose(0, 1, 3, 4, 2).reshape(B, nc, ng, R)
        return jnp.pad(v, ((0, 0), (0, 0), (0, nrow - ng), (0, 0)))

    rows = ba[row0:row0 + B * L]
    ba_arr = jnp.concatenate([arrange(rows[:, :H_V]), arrange(rows[:, H_V:2 * H_V])], axis=2)

    def arrange_param(p):
        v = jnp.broadcast_to(p.reshape(ng, hg, 1), (ng, hg, C)).reshape(ng, R)
        return jnp.pad(v, ((0, nrow - ng), (0, 0)))

    ad = jnp.concatenate([arrange_param(a_log), arrange_param(dt_bias)], axis=0)
    rb0 = row0 // rows_blk
    in_specs = [
        pl.BlockSpec((rows_blk, QKV_W), lambda b, c: (rb0 + b * steps + c, 0)),
        pl.BlockSpec((rows_blk, V_W), lambda b, c: (rb0 + b * steps + c, Z_BLK)),
        pl.BlockSpec((nseq, nck, 2 * nrow, R), lambda b, c: (b, c, 0, 0)),
        pl.BlockSpec((CONV_W, QKV_W), lambda b, c: (0, 0)),
        pl.BlockSpec((2 * nrow, R), lambda b, c: (0, 0)),
        pl.BlockSpec((1, D_V), lambda b, c: (0, 0)),
    ]
    args = [proj, proj, ba_arr, conv_w, ad, gamma_a.reshape(1, D_V)]
    if has_state:
        in_specs += [pl.BlockSpec((nseq, CONV_W - 1, QKV_W), lambda b, c: (b, 0, 0)),
                     pl.BlockSpec((nseq, H_V, D_K, D_V), lambda b, c: (b, 0, 0, 0))]
        args += [conv_buf, s0]
    return pl.pallas_call(
        functools.partial(_gdn_kernel, C=C, hg=hg, nseq=nseq, nck=nck, has_state=has_state),
        grid=(B // nseq, steps),
        in_specs=in_specs,
        out_specs=[pl.BlockSpec((rows_blk, V_W), lambda b, c: (b * steps + c, 0)),
                   pl.BlockSpec((nseq, H_V, D_K, D_V), lambda b, c: (b, 0, 0, 0))],
        out_shape=[jax.ShapeDtypeStruct((B * L, V_W), BF16),
                   jax.ShapeDtypeStruct((B, H_V, D_K, D_V), F32)],
        scratch_shapes=[pltpu.VMEM((nseq * H_V, D_K, D_V), F32),
                        pltpu.VMEM((nseq * 8, QKV_W), F32),
                        pltpu.VMEM((rows_blk, QKV_W), F32)],
        compiler_params=_cparams(("parallel", "arbitrary")),
        name="gdn_state" if has_state else "gdn_fresh",
    )(*args)


def _pool_kernel(*refs, tc, nseq, start_pos, has_state):
    if has_state:
        u_ref, buf_ref, o_ref, tail_scr = refs
    else:
        u_ref, o_ref, tail_scr = refs
    hist = POOL_BUF + 1
    c = pl.program_id(1)

    @pl.when(c == 0)
    def _init():
        tail_scr[...] = jnp.zeros(tail_scr.shape, F32)
        if has_state:
            for s in range(nseq):
                tail_scr[s * hist + 1:(s + 1) * hist] = buf_ref[s]

    u_all = u_ref[...].astype(F32)
    pos1 = (start_pos + 1 + c * tc + lax.broadcasted_iota(jnp.int32, (tc, 1), 0)).astype(F32)
    for s in range(nseq):
        u = u_all[s * tc:(s + 1) * tc]
        xe = jnp.concatenate([tail_scr[s * hist:(s + 1) * hist], u], axis=0)
        tail_scr[s * hist:(s + 1) * hist] = xe[tc:tc + hist]
        for gi, win in enumerate(POOL_WINDOWS):
            sl = slice(gi * POOL_GROUP_W, (gi + 1) * POOL_GROUP_W)
            acc = xe[:, sl]
            shift = 1
            while shift < win:
                acc = acc + pltpu.roll(acc, shift, 0)
                shift *= 2
            cnt = jnp.minimum(float(win), pos1)
            o_ref[s * tc:(s + 1) * tc, sl] = acc[hist:hist + tc] / cnt - u[:, sl]


def _pool(proj, row0, B, L, tc, nseq, start_pos, pool_buf=None):
    has_state = pool_buf is not None
    nc = L // tc
    assert nseq == 1 or nc == 1
    rows_blk = nseq * tc
    rb0 = row0 // rows_blk
    in_specs = [pl.BlockSpec((rows_blk, D_MODEL), lambda b, c: (rb0 + b * nc + c, U_BLK))]
    args = [proj]
    if has_state:
        in_specs.append(pl.BlockSpec((nseq, POOL_BUF, D_MODEL), lambda b, c: (b, 0, 0)))
        args.append(pool_buf)
    return pl.pallas_call(
        functools.partial(_pool_kernel, tc=tc, nseq=nseq, start_pos=start_pos, has_state=has_state),
        grid=(B // nseq, nc),
        in_specs=in_specs,
        out_specs=pl.BlockSpec((rows_blk, D_MODEL), lambda b, c: (b * nc + c, 0)),
        out_shape=jax.ShapeDtypeStruct((B * L, D_MODEL), F32),
        scratch_shapes=[pltpu.VMEM((nseq * (POOL_BUF + 1), D_MODEL), F32)],
        compiler_params=_cparams(("parallel", "arbitrary")),
        name="pool_state" if has_state else "pool_fresh",
    )(*args)


def _post_kernel(pa_ref, pb_ref, oa_ref, ob_ref, ga_ref, gb_ref, xa_ref, xb_ref, wp_ref, ps_ref, wbb_ref, wba_ref,
                 wo_ref, g_ref, b_ref, h_ref, hp_ref, *, alpha, tiles_a):
    first = pl.program_id(0) < tiles_a
    x = jnp.where(first, xa_ref[...], xb_ref[...])
    pooled = jnp.where(first, pa_ref[...], pb_ref[...])
    o_gated = jnp.where(first, oa_ref[...], ob_ref[...])
    mixed = jnp.concatenate(
        [_dot(pooled[:, gi * POOL_GROUP_W:(gi + 1) * POOL_GROUP_W], wp_ref[gi]) for gi in range(len(POOL_WINDOWS))],
        axis=1) * ps_ref[...]
    branch_b = _dot(mixed, wbb_ref[...])
    branch_a = _dot(o_gated, wba_ref[...])
    merged = _sigmoid(ga_ref[...].astype(F32)) * branch_a + _sigmoid(gb_ref[...].astype(F32)) * branch_b
    h = _layer_norm(alpha * x + _dot(merged, wo_ref[...]), g_ref[...], b_ref[...])
    h_ref[...] = h
    _store_pieces(hp_ref, h)


def _post(pooled_a, pooled_b, o_a, o_b, proj, x_a, x_b, w_pool, pool_scale, w_br_b, w_br_a, w_out, ln_g, ln_b,
          alpha, tm):
    t = x_a.shape[0] + x_b.shape[0]
    tiles_a = x_a.shape[0] // tm
    row = lambda i: (i, 0)
    row_a = lambda i: (jnp.minimum(i, tiles_a - 1), 0)
    row_b = lambda i: (jnp.maximum(i - tiles_a, 0), 0)
    const2 = lambda i: (0, 0)
    return pl.pallas_call(
        functools.partial(_post_kernel, alpha=alpha, tiles_a=tiles_a),
        grid=(t // tm,),
        in_specs=[pl.BlockSpec((tm, D_MODEL), row_a),
                  pl.BlockSpec((tm, D_MODEL), row_b),
                  pl.BlockSpec((tm, V_W), row_a),
                  pl.BlockSpec((tm, V_W), row_b),
                  pl.BlockSpec((tm, D_MODEL), lambda i: (i, U_BLK + 1)),
                  pl.BlockSpec((tm, D_MODEL), lambda i: (i, U_BLK + 2)),
                  pl.BlockSpec((tm, D_MODEL), row_a),
                  pl.BlockSpec((tm, D_MODEL), row_b),
                  pl.BlockSpec(w_pool.shape, lambda i: (0, 0, 0)),
                  pl.BlockSpec((1, D_MODEL), const2),
                  pl.BlockSpec(w_br_b.shape, const2),
                  pl.BlockSpec(w_br_a.shape, const2),
                  pl.BlockSpec(w_out.shape, const2),
                  pl.BlockSpec((1, D_MODEL), const2),
                  pl.BlockSpec((1, D_MODEL), const2)],
        out_specs=[pl.BlockSpec((tm, D_MODEL), row), pl.BlockSpec((N_PIECE, tm, PIECE_W), lambda i: (0, i, 0))],
        out_shape=[jax.ShapeDtypeStruct((t, D_MODEL), F32),
                   jax.ShapeDtypeStruct((N_PIECE, t, PIECE_W), jnp.uint32)],
        compiler_params=_cparams(("parallel",)),
        name="post_mixers",
    )(pooled_a, pooled_b, o_a, o_b, proj, proj, x_a, x_b, w_pool, pool_scale, w_br_b, w_br_a, w_out, ln_g, ln_b)


def _router_kernel(h_ref, wt_ref, bias_ref, idx_ref, wts_ref, rank_ref, cnt_ref, carry_scr, *, tm):
    i = pl.program_id(0)

    @pl.when(i == 0)
    def _init():
        carry_scr[...] = jnp.zeros(carry_scr.shape, F32)

    logits = _dot3_nt(wt_ref[...], h_ref[...])
    sc = _sigmoid(logits)
    ch = sc + bias_ref[...]
    neg = -jnp.inf
    e_in = lax.broadcasted_iota(jnp.int32, (GROUP_SZ, tm), 0)
    gs_rows = []
    for g in range(N_GROUP):
        blk = ch[g * GROUP_SZ:(g + 1) * GROUP_SZ]
        m1 = jnp.max(blk, axis=0, keepdims=True)
        i1 = jnp.min(jnp.where(blk == m1, e_in, GROUP_SZ), axis=0, keepdims=True)
        m2 = jnp.max(jnp.where(e_in == i1, neg, blk), axis=0, keepdims=True)
        gs_rows.append(m1 + m2)
    gs = jnp.concatenate(gs_rows, axis=0)
    g_io = lax.broadcasted_iota(jnp.int32, (N_GROUP, tm), 0)
    e_io = lax.broadcasted_iota(jnp.int32, (N_EXPERTS, tm), 0)
    e_grp = e_io >> int(math.log2(GROUP_SZ))
    masked = jnp.full((N_EXPERTS, tm), neg, F32)
    for _ in range(TOPK_GROUP):
        mx = jnp.max(gs, axis=0, keepdims=True)
        gi = jnp.min(jnp.where(gs == mx, g_io, N_GROUP), axis=0, keepdims=True)
        gs = jnp.where(g_io == gi, neg, gs)
        masked = jnp.where(e_grp == gi, ch, masked)
    idx_rows, w_rows = [], []
    onehot = jnp.zeros((N_EXPERTS, tm), F32)
    for _ in range(TOP_K):
        mx = jnp.max(masked, axis=0, keepdims=True)
        ei = jnp.min(jnp.where(masked == mx, e_io, N_EXPERTS), axis=0, keepdims=True)
        hit = e_io == ei
        idx_rows.append(ei)
        w_rows.append(jnp.sum(jnp.where(hit, sc, 0.0), axis=0, keepdims=True))
        onehot = jnp.where(hit, 1.0, onehot)
        masked = jnp.where(hit, neg, masked)
    wsel = jnp.concatenate(w_rows, axis=0)
    wts_ref[...] = wsel / jnp.sum(wsel, axis=0, keepdims=True) * ROUTED_SCALE
    idx_ref[...] = jnp.concatenate(idx_rows, axis=0)

    tr = lax.broadcasted_iota(jnp.int32, (tm, tm), 0)
    tc_ = lax.broadcasted_iota(jnp.int32, (tm, tm), 1)
    before = jnp.where(tr < tc_, 1.0, 0.0).astype(BF16)
    cum = jnp.dot(onehot.astype(BF16), before, preferred_element_type=F32) + carry_scr[...]
    rank_ref[...] = jnp.concatenate(
        [jnp.sum(jnp.where(e_io == ei, cum, 0.0), axis=0, keepdims=True) for ei in idx_rows],
        axis=0).astype(jnp.int32)
    carry_scr[...] = carry_scr[...] + jnp.sum(onehot, axis=1, keepdims=True)
    cnt_ref[...] = carry_scr[...]


def _router(h, w_router_t, bias_col, tm):
    t = h.shape[0]
    return pl.pallas_call(
        functools.partial(_router_kernel, tm=tm),
        grid=(t // tm,),
        in_specs=[pl.BlockSpec((tm, D_MODEL), lambda i: (i, 0)),
                  pl.BlockSpec((N_EXPERTS, D_MODEL), lambda i: (0, 0)),
                  pl.BlockSpec((N_EXPERTS, 1), lambda i: (0, 0))],
        out_specs=[pl.BlockSpec((TOP_K, tm), lambda i: (0, i)),
                   pl.BlockSpec((TOP_K, tm), lambda i: (0, i)),
                   pl.BlockSpec((TOP_K, tm), lambda i: (0, i)),
                   pl.BlockSpec((N_EXPERTS, 1), lambda i: (0, 0))],
        out_shape=[jax.ShapeDtypeStruct((TOP_K, t), jnp.int32),
                   jax.ShapeDtypeStruct((TOP_K, t), F32),
                   jax.ShapeDtypeStruct((TOP_K, t), jnp.int32),
                   jax.ShapeDtypeStruct((N_EXPERTS, 1), F32)],
        scratch_shapes=[pltpu.VMEM((N_EXPERTS, 1), F32)],
        compiler_params=_cparams(("arbitrary",)),
        name="router",
    )(h, w_router_t, bias_col)


def _dest_kernel(idx_ref, rank_ref, start_ref, dest_ref, *, tm, n_rows):
    e_io = lax.broadcasted_iota(jnp.int32, (N_EXPERTS, tm), 0)
    starts = start_ref[...]
    rows = []
    for k in range(TOP_K):
        seg = jnp.sum(jnp.where(e_io == idx_ref[k:k + 1, :], starts, 0.0), axis=0, keepdims=True)
        rows.append(seg.astype(jnp.int32) + rank_ref[k:k + 1, :])
    base = jnp.concatenate(rows, axis=0)
    for p in range(N_PIECE):
        dest_ref[p] = base + p * n_rows


def _dest(idx_t, rank_t, seg_start_col, n_rows, tm):
    t = idx_t.shape[1]
    blk = pl.BlockSpec((TOP_K, tm), lambda i: (0, i))
    return pl.pallas_call(
        functools.partial(_dest_kernel, tm=tm, n_rows=n_rows),
        grid=(t // tm,),
        in_specs=[blk, blk, pl.BlockSpec((N_EXPERTS, 1), lambda i: (0, 0))],
        out_specs=pl.BlockSpec((N_PIECE, TOP_K, tm), lambda i: (0, 0, i)),
        out_shape=jax.ShapeDtypeStruct((N_PIECE, TOP_K, t), jnp.int32),
        compiler_params=_cparams(("parallel",)),
        name="dispatch_rows",
    )(idx_t, rank_t, seg_start_col)


def _expert_kernel(be_ref, nu_ref, first_ref, nxt_ref, slot_ref, x_ref, wg_hbm, wu_hbm, wd_hbm, y_ref,
                   wg_buf, wu_buf, wd_buf, wgu_scr, wd_scr, sems):
    i = pl.program_id(0)
    live = i < nu_ref[0]

    def fetch(e, slot):
        return [pltpu.make_async_copy(wg_hbm.at[e], wg_buf.at[slot], sems.at[slot, 0]),
                pltpu.make_async_copy(wu_hbm.at[e], wu_buf.at[slot], sems.at[slot, 1]),
                pltpu.make_async_copy(wd_hbm.at[e], wd_buf.at[slot], sems.at[slot, 2])]

    @pl.when(live & (first_ref[i] == 1))
    def _new_expert():
        slot = slot_ref[i]

        @pl.when(i == 0)
        def _():
            for cp in fetch(be_ref[i], slot):
                cp.start()

        @pl.when(nxt_ref[i] >= 0)
        def _():
            for cp in fetch(nxt_ref[i], 1 - slot):
                cp.start()

        for cp in fetch(be_ref[i], slot):
            cp.wait()
        wgu_scr[:, :D_EXPERT] = wg_buf[slot].astype(BF16)
        wgu_scr[:, D_EXPERT:] = wu_buf[slot].astype(BF16)
        wd_scr[...] = wd_buf[slot].astype(BF16)

    @pl.when(live)
    def _():
        n_sub = 2
        sub = EXPERT_BM // n_sub
        xs = [jnp.concatenate([c.astype(BF16) for c in
                               _load_pieces([x_ref[p, s * sub:(s + 1) * sub] for p in range(N_PIECE)])], axis=1)
              for s in range(n_sub)]
        gus = [jnp.dot(x, wgu_scr[...], preferred_element_type=F32) for x in xs]
        acts = [(_silu(gu[:, :D_EXPERT]) * gu[:, D_EXPERT:]).astype(BF16) for gu in gus]
        ys = [jnp.dot(a, wd_scr[...], preferred_element_type=F32) for a in acts]
        for s, y in enumerate(ys):
            for p in range(N_PIECE):
                y_ref[p, s * sub:(s + 1) * sub] = _pack_halves(y[:, 2 * p * PIECE_W:2 * (p + 1) * PIECE_W])


def _experts(xs, blk_exp, n_used, blk_first, blk_next, blk_slot, w_gate, w_up, w_down):
    n_rows = xs.shape[1]
    n_blocks = n_rows // EXPERT_BM

    def row_block(i, be, nu, *_):
        return (0, jnp.minimum(i, nu[0] - 1), 0)

    grid_spec = pltpu.PrefetchScalarGridSpec(
        num_scalar_prefetch=5,
        grid=(n_blocks,),
        in_specs=[pl.BlockSpec((N_PIECE, EXPERT_BM, PIECE_W), row_block),
                  pl.BlockSpec(memory_space=pl.ANY),
                  pl.BlockSpec(memory_space=pl.ANY),
                  pl.BlockSpec(memory_space=pl.ANY)],
        out_specs=pl.BlockSpec((N_PIECE, EXPERT_BM, PIECE_W), row_block),
        scratch_shapes=[pltpu.VMEM((2, D_MODEL, D_EXPERT), F32),
                        pltpu.VMEM((2, D_MODEL, D_EXPERT), F32),
                        pltpu.VMEM((2, D_EXPERT, D_MODEL), F32),
                        pltpu.VMEM((D_MODEL, 2 * D_EXPERT), BF16),
                        pltpu.VMEM((D_EXPERT, D_MODEL), BF16),
                        pltpu.SemaphoreType.DMA((2, 3))],
    )
    return pl.pallas_call(
        _expert_kernel,
        grid_spec=grid_spec,
        out_shape=jax.ShapeDtypeStruct((N_PIECE, n_rows, PIECE_W), jnp.uint32),
        compiler_params=_cparams(("arbitrary",)),
        name="experts",
    )(blk_exp, n_used, blk_first, blk_next, blk_slot, xs, w_gate, w_up, w_down)


SC_WINDOW = 128
V7X_SC_CORES = 2
V7X_SC_SUBCORES = 16


def _sc_mesh():
    return plsc.VectorSubcoreMesh(core_axis_name="core", subcore_axis_name="subcore",
                                  num_cores=V7X_SC_CORES, num_subcores=V7X_SC_SUBCORES)


def _sc_scatter_rows(src, dest, n_rows, seg, repeat):
    d = src.shape[1]
    n_idx = dest.shape[0]
    seg_blocks = seg // SC_WINDOW
    dest2 = dest.reshape(1, n_idx)

    def src_block(i):
        return ((i // (repeat * seg_blocks)) * seg_blocks + i % seg_blocks, 0)

    @functools.partial(pl.kernel, out_type=jax.ShapeDtypeStruct((n_rows, d), src.dtype), mesh=_sc_mesh(),
                       scratch_types=[], name="sc_dispatch")
    def run(src_hbm, idx_hbm, out_hbm):
        def body(rows_vmem, idx_vmem):
            pltpu.sync_copy(rows_vmem, out_hbm.at[idx_vmem.at[0]])

        pltpu.emit_pipeline(
            body,
            grid=(n_idx // SC_WINDOW,),
            in_specs=[pl.BlockSpec((SC_WINDOW, d), src_block),
                      pl.BlockSpec((1, SC_WINDOW), lambda i: (0, i))],
            out_specs=[],
            core_axis_name=("core", "subcore"),
            dimension_semantics=(pltpu.PARALLEL,),
        )(src_hbm, idx_hbm)

    return run(src, dest2)


def _sc_gather_rows(table, idx):
    d = table.shape[1]
    n_idx = idx.shape[0]
    idx2 = idx.reshape(1, n_idx)

    @functools.partial(pl.kernel, out_type=jax.ShapeDtypeStruct((n_idx, d), table.dtype), mesh=_sc_mesh(),
                       scratch_types=[], name="sc_combine_gather")
    def run(table_hbm, idx_hbm, out_hbm):
        def body(idx_vmem, rows_vmem):
            pltpu.sync_copy(table_hbm.at[idx_vmem.at[0]], rows_vmem)

        pltpu.emit_pipeline(
            body,
            grid=(n_idx // SC_WINDOW,),
            in_specs=[pl.BlockSpec((1, SC_WINDOW), lambda i: (0, i))],
            out_specs=[pl.BlockSpec((SC_WINDOW, d), lambda i: (i, 0))],
            core_axis_name=("core", "subcore"),
            dimension_semantics=(pltpu.PARALLEL,),
        )(idx_hbm, out_hbm)

    return run(table, idx2)


def _combine_kernel(yg_ref, wts_ref, h_ref, wgu_ref, wd_ref, g_ref, b_ref, outa_ref, outb_ref, *, alpha, tiles_a):
    wts = wts_ref[...]
    acc = None
    for k in range(TOP_K):
        cols = _load_pieces([yg_ref[p, k] for p in range(N_PIECE)])
        wk = wts[:, k:k + 1]
        acc = [c * wk for c in cols] if acc is None else [a + c * wk for a, c in zip(acc, cols)]
    routed = jnp.concatenate(acc, axis=1)
    h = h_ref[...]
    gu = _dot(h, wgu_ref[...])
    shared = _dot(_silu(gu[:, :D_EXPERT]) * gu[:, D_EXPERT:], wd_ref[...])
    y = _layer_norm(alpha * h + (routed + shared), g_ref[...], b_ref[...])
    i = pl.program_id(0)

    @pl.when(i < tiles_a)
    def _():
        outa_ref[...] = y

    @pl.when(i >= tiles_a)
    def _():
        outb_ref[...] = y


def _combine(yg, wts, h, w_sh_gu, w_sh_down, ln_g, ln_b, alpha, tm, t_a):
    t = h.shape[0]
    tiles_a = t_a // tm
    row = lambda i: (i, 0)
    const2 = lambda i: (0, 0)
    return pl.pallas_call(
        functools.partial(_combine_kernel, alpha=alpha, tiles_a=tiles_a),
        grid=(t // tm,),
        in_specs=[pl.BlockSpec((N_PIECE, TOP_K, tm, PIECE_W), lambda i: (0, 0, i, 0)),
                  pl.BlockSpec((tm, TOP_K), row),
                  pl.BlockSpec((tm, D_MODEL), row),
                  pl.BlockSpec(w_sh_gu.shape, const2),
                  pl.BlockSpec(w_sh_down.shape, const2),
                  pl.BlockSpec((1, D_MODEL), const2),
                  pl.BlockSpec((1, D_MODEL), const2)],
        out_specs=[pl.BlockSpec((tm, D_MODEL), lambda i: (jnp.minimum(i, tiles_a - 1), 0)),
                   pl.BlockSpec((tm, D_MODEL), lambda i: (jnp.maximum(i - tiles_a, 0), 0))],
        out_shape=[jax.ShapeDtypeStruct((t_a, D_MODEL), F32), jax.ShapeDtypeStruct((t - t_a, D_MODEL), F32)],
        compiler_params=_cparams(("arbitrary",)),
        name="combine_ln2",
    )(yg, wts, h, w_sh_gu, w_sh_down, ln_g, ln_b)


def _layer(xp, xs, s_delta, s_conv, s_pool, w_in, conv_w, a_log, dt_bias, gamma_a, w_br_a, w_pool,
           pool_scale, w_br_b, w_out, ln1_g, ln1_b, w_router, router_bias, w_exp_gate, w_exp_up,
           w_exp_down, w_sh_gate, w_sh_up, w_sh_down, ln2_g, ln2_b, alpha):
    Bp, Lp, _ = xp.shape
    Bs, Ls, _ = xs.shape
    Tp, Ts = Bp * Lp, Bs * Ls
    T = Tp + Ts
    x_p = xp.reshape(Tp, D_MODEL)
    x_s = xs.reshape(Ts, D_MODEL)

    o_b, o_u = QKV_W + V_W, QKV_W + V_W + 2 * H_V
    w_main = jnp.concatenate([w_in[:, :o_b], w_in[:, o_u:]], axis=1).astype(BF16)
    w_ba = jnp.pad(w_in[:, o_b:o_u], ((0, 0), (0, 128 - 2 * H_V))).astype(BF16)
    proj, ba = _in_proj(x_p, x_s, w_main, w_ba, IN_PROJ_TM, IN_PROJ_TN)

    o_p, sd_p = _gdn(proj, ba, conv_w, a_log, dt_bias, gamma_a, 0, Bp, Lp, GDN_CHUNK, 1, GDN_FRESH_CHUNKS)
    o_s, sd_s = _gdn(proj, ba, conv_w, a_log, dt_bias, gamma_a, Tp, Bs, Ls, Ls, GDN_STATE_SEQS, 1,
                     conv_buf=s_conv, s0=s_delta)

    pooled_p = _pool(proj, 0, Bp, Lp, POOL_TILE, 1, 0)
    pooled_s = _pool(proj, Tp, Bs, Ls, Ls, POOL_STATE_SEQS, PAST_LEN, pool_buf=s_pool)

    h, hp = _post(pooled_p, pooled_s, o_p, o_s, proj, x_p, x_s, w_pool.astype(BF16), pool_scale.reshape(1, D_MODEL),
                  w_br_b.astype(BF16), w_br_a.astype(BF16), w_out.astype(BF16),
                  ln1_g.reshape(1, D_MODEL), ln1_b.reshape(1, D_MODEL), alpha, POST_TM)

    idx_t, wts_t, rank_t, cnt = _router(h, w_router.T, router_bias.reshape(N_EXPERTS, 1), ROUTER_TM)
    counts = cnt[:, 0].astype(jnp.int32)
    padded = ((counts + EXPERT_BM - 1) // EXPERT_BM) * EXPERT_BM
    pends = jnp.cumsum(padded)
    pstarts = pends - padded
    n_blocks = (T * TOP_K + N_EXPERTS * (EXPERT_BM - 1) + EXPERT_BM - 1) // EXPERT_BM
    n_rows = n_blocks * EXPERT_BM
    piece_rows = _dest(idx_t, rank_t, pstarts.astype(F32).reshape(N_EXPERTS, 1), n_rows, ROUTER_TM)
    blk_start = jnp.arange(n_blocks, dtype=jnp.int32) * EXPERT_BM
    blk_exp = jnp.minimum(jnp.sum((pends[None, :] <= blk_start[:, None]).astype(jnp.int32), axis=1),
                          N_EXPERTS - 1)
    n_used = (pends[-1:] // EXPERT_BM).astype(jnp.int32)
    has_rows = counts > 0
    e_ids = jnp.arange(N_EXPERTS, dtype=jnp.int32)
    slot_of = (jnp.cumsum(has_rows.astype(jnp.int32)) - 1) & 1
    later = jnp.where(has_rows, e_ids, N_EXPERTS)
    next_of = jnp.concatenate([lax.cummin(later, reverse=True)[1:], jnp.full((1,), N_EXPERTS, jnp.int32)])
    next_of = jnp.where(next_of < N_EXPERTS, next_of, -1)
    blk_first = (blk_start == pstarts[blk_exp]).astype(jnp.int32)
    blk_next = next_of[blk_exp]
    blk_slot = slot_of[blk_exp]

    piece_idx = piece_rows.reshape(N_PIECE * TOP_K * T)
    x_sorted = _sc_scatter_rows(hp.reshape(N_PIECE * T, PIECE_W), piece_idx, N_PIECE * n_rows, T, TOP_K)
    y_sorted = _experts(x_sorted.reshape(N_PIECE, n_rows, PIECE_W), blk_exp, n_used, blk_first, blk_next,
                        blk_slot, w_exp_gate, w_exp_up, w_exp_down)
    yg = _sc_gather_rows(y_sorted.reshape(N_PIECE * n_rows, PIECE_W), piece_idx)
    yg = yg.reshape(N_PIECE, TOP_K, T, PIECE_W)

    w_sh_gu = jnp.concatenate([w_sh_gate, w_sh_up], axis=1).astype(BF16)
    y_p, y_s = _combine(yg, wts_t.T, h, w_sh_gu, w_sh_down.astype(BF16),
                        ln2_g.reshape(1, D_MODEL), ln2_b.reshape(1, D_MODEL), alpha, COMBINE_TM, Tp)

    def carried_rows(keep, col0, width, name, old_p, old_s):
        parts, counts = [], []
        for x2d, B, L in ((x_p, Bp, Lp), (x_s, Bs, Ls)):
            n = min(L, keep)
            rows = jnp.arange(B, dtype=jnp.int32)[:, None] * L + (L - n) + jnp.arange(n, dtype=jnp.int32)
            parts.append(jnp.take(x2d, rows.reshape(-1), axis=0).astype(BF16))
            counts.append((B, n))
        n_tail = sum(B * n for B, n in counts)
        n_pad = -(-n_tail // 128) * 128
        xt = jnp.pad(jnp.concatenate(parts, axis=0), ((0, n_pad - n_tail), (0, 0)))
        tail = _matmul(xt, w_main, col0, width, n_pad, 1024, F32, name)
        outs, r0 = [], 0
        for (B, n), old in zip(counts, (old_p, old_s)):
            new = tail[r0:r0 + B * n].reshape(B, n, width)
            outs.append(new if n == keep else jnp.concatenate([old[:, n:], new], axis=1))
            r0 += B * n
        return outs

    new_conv_p, new_conv_s = carried_rows(CONV_W - 1, 0, QKV_W, "in_proj_conv_rows",
                                          jnp.zeros((Bp, CONV_W - 1, QKV_W), F32), s_conv)
    new_pool_p, new_pool_s = carried_rows(POOL_BUF, U_BLK * D_MODEL, D_MODEL, "in_proj_pool_rows",
                                          jnp.zeros((Bp, POOL_BUF, D_MODEL), F32), s_pool)
    return (y_p.reshape(Bp, Lp, D_MODEL), y_s.reshape(Bs, Ls, D_MODEL),
            sd_p, new_conv_p, new_pool_p, sd_s, new_conv_s, new_pool_s)


def kernel(x_prompt, x_sample, state_delta, state_conv, state_pool, w_in, conv_w, a_log, dt_bias, gamma_a,
           w_br_a, w_pool, pool_scale, w_br_b, w_out, ln1_g, ln1_b, w_router, router_bias,
           w_exp_gate, w_exp_up, w_exp_down, w_sh_gate, w_sh_up, w_sh_down, ln2_g, ln2_b):
    depth = w_in.shape[0]
    alpha = (2 * depth) ** 0.25
    yp, ys = x_prompt, x_sample
    outs = [[] for _ in range(6)]
    for l in range(depth):
        res = _layer(yp, ys, state_delta[l], state_conv[l], state_pool[l], w_in[l], conv_w[l], a_log[l],
                     dt_bias[l], gamma_a[l], w_br_a[l], w_pool[l], pool_scale[l], w_br_b[l], w_out[l],
                     ln1_g[l], ln1_b[l], w_router[l], router_bias[l], w_exp_gate[l], w_exp_up[l],
                     w_exp_down[l], w_sh_gate[l], w_sh_up[l], w_sh_down[l], ln2_g[l], ln2_b[l], alpha)
        yp, ys = res[0], res[1]
        for lst, v in zip(outs, res[2:]):
            lst.append(v)
    return (yp, ys) + tuple(jnp.stack(v) for v in outs)
```

```python
import functools
import math

import jax
import jax.numpy as jnp
from jax import lax
from jax.experimental import pallas as pl
from jax.experimental.pallas import tpu as pltpu
from jax.experimental.pallas import tpu_sc as plsc

F32 = jnp.float32
BF16 = jnp.bfloat16

D_MODEL = 1024
H_K = 8
D_K = 128
H_V = 16
D_V = 128
Q_W = H_K * D_K
V_W = H_V * D_V
QKV_W = 2 * Q_W + V_W
CONV_W = 4
POOL_WINDOWS = (2, 4, 8, 16)
POOL_GROUP_W = D_MODEL // len(POOL_WINDOWS)
POOL_BUF = max(POOL_WINDOWS) - 1
N_EXPERTS = 256
TOP_K = 8
N_GROUP = 8
TOPK_GROUP = 4
GROUP_SZ = N_EXPERTS // N_GROUP
D_EXPERT = D_MODEL // 4
ROUTED_SCALE = 2.5
LN_EPS = 1e-5
NORM_EPS = 1e-6
PAST_LEN = 16384

MAIN_W = QKV_W + V_W + 3 * D_MODEL
Z_BLK = QKV_W // V_W
U_BLK = (QKV_W + V_W) // D_MODEL

IN_PROJ_TM = 1024
IN_PROJ_TN = MAIN_W // 3
GDN_CHUNK = 64
POOL_TILE = 256
POST_TM = 256
ROUTER_TM = 512
COMBINE_TM = 512

STACK_ROWS = 128
SOLVE_BASE = 16
GDN_FRESH_CHUNKS = 4
EXPERT_BM = 512
GDN_STATE_SEQS = 8
POOL_STATE_SEQS = 16
VMEM_LIMIT = 56 * 1024 * 1024


def _cparams(sem):
    return pltpu.CompilerParams(dimension_semantics=sem, vmem_limit_bytes=VMEM_LIMIT)


def _sigmoid(x):
    return 0.5 * jnp.tanh(0.5 * x) + 0.5


def _silu(x):
    return x * _sigmoid(x)


def _softplus(x):
    return jnp.maximum(x, 0.0) + jnp.log(1.0 + jnp.exp(-jnp.abs(x)))


def _dot(a, b):
    return jnp.dot(a.astype(BF16), b.astype(BF16), preferred_element_type=F32)


def _dot_nt(a, b):
    return lax.dot_general(a.astype(BF16), b.astype(BF16), (((1,), (1,)), ((), ())),
                           preferred_element_type=F32)


def _dot_tn(a, b):
    return lax.dot_general(a.astype(BF16), b.astype(BF16), (((0,), (0,)), ((), ())),
                           preferred_element_type=F32)


def _split(a):
    hi = a.astype(BF16)
    lo = (a - hi.astype(F32)).astype(BF16)
    return hi, lo


def _dot3_nt(a, b):
    ah, al = _split(a)
    bh, bl = _split(b)
    d = functools.partial(lax.dot_general, dimension_numbers=(((1,), (1,)), ((), ())),
                          preferred_element_type=F32)
    return d(ah, bh) + (d(ah, bl) + d(al, bh))


def _pack_halves(x):
    n = x.shape[1] // 2
    hi = lax.bitcast_convert_type(x[:, :n].astype(BF16).astype(F32), jnp.uint32)
    lo = lax.bitcast_convert_type(x[:, n:].astype(BF16).astype(F32), jnp.uint32)
    return (hi & jnp.uint32(0xFFFF0000)) | (lo >> 16)


def _unpack_halves(w):
    hi = lax.bitcast_convert_type(w & jnp.uint32(0xFFFF0000), F32)
    lo = lax.bitcast_convert_type(w << 16, F32)
    return hi, lo


N_PIECE = 2
PIECE_W = D_MODEL // (2 * N_PIECE)


def _store_pieces(ref, x):
    for p in range(N_PIECE):
        ref[p] = _pack_halves(x[:, 2 * p * PIECE_W:2 * (p + 1) * PIECE_W])


def _load_pieces(pieces):
    cols = []
    for w in pieces:
        cols.extend(_unpack_halves(w))
    return cols


def _layer_norm(x, g, b):
    mu = jnp.mean(x, axis=-1, keepdims=True)
    xc = x - mu
    var = jnp.mean(xc * xc, axis=-1, keepdims=True)
    return xc * lax.rsqrt(var + LN_EPS) * g + b


def _in_proj_kernel(xa_ref, xb_ref, w_ref, wba_ref, o_ref, ba_ref, x_scr, *, tiles_a):
    @pl.when(pl.program_id(1) == 0)
    def _():
        x = jnp.where(pl.program_id(0) < tiles_a, xa_ref[...], xb_ref[...]).astype(BF16)
        x_scr[...] = x
        ba_ref[...] = jnp.dot(x, wba_ref[...], preferred_element_type=F32)

    o_ref[...] = jnp.dot(x_scr[...], w_ref[...], preferred_element_type=F32).astype(o_ref.dtype)


def _in_proj(x_a, x_b, w_main, w_ba, tm, tn):
    k = x_a.shape[1]
    t = x_a.shape[0] + x_b.shape[0]
    n = w_main.shape[1]
    tiles_a = x_a.shape[0] // tm
    return pl.pallas_call(
        functools.partial(_in_proj_kernel, tiles_a=tiles_a),
        grid=(t // tm, n // tn),
        in_specs=[pl.BlockSpec((tm, k), lambda i, j: (jnp.minimum(i, tiles_a - 1), 0)),
                  pl.BlockSpec((tm, k), lambda i, j: (jnp.maximum(i - tiles_a, 0), 0)),
                  pl.BlockSpec((k, tn), lambda i, j: (0, j)),
                  pl.BlockSpec(w_ba.shape, lambda i, j: (0, 0))],
        out_specs=[pl.BlockSpec((tm, tn), lambda i, j: (i, j)),
                   pl.BlockSpec((tm, w_ba.shape[1]), lambda i, j: (i, 0))],
        out_shape=[jax.ShapeDtypeStruct((t, n), BF16), jax.ShapeDtypeStruct((t, w_ba.shape[1]), F32)],
        scratch_shapes=[pltpu.VMEM((tm, k), BF16)],
        compiler_params=_cparams(("parallel", "arbitrary")),
        name="in_proj",
    )(x_a, x_b, w_main, w_ba)


def _mm_kernel(x_ref, w_ref, o_ref):
    o_ref[...] = jnp.dot(x_ref[...], w_ref[...], preferred_element_type=F32).astype(o_ref.dtype)


def _matmul(x, w, col0, n, tm, tn, out_dtype, name):
    t, k = x.shape
    cb0 = col0 // tn
    return pl.pallas_call(
        _mm_kernel,
        grid=(t // tm, n // tn),
        in_specs=[pl.BlockSpec((tm, k), lambda i, j: (i, 0)),
                  pl.BlockSpec((k, tn), lambda i, j: (0, cb0 + j))],
        out_specs=pl.BlockSpec((tm, tn), lambda i, j: (i, j)),
        out_shape=jax.ShapeDtypeStruct((t, n), out_dtype),
        compiler_params=_cparams(("parallel", "parallel")),
        name=name,
    )(x, w)


def _gdn_kernel(*refs, C, hg, nseq, nck, has_state):
    if has_state:
        (qkv_ref, z_ref, ba_ref, cw_ref, ad_ref, gam_ref, buf_ref, s0_ref,
         o_ref, sout_ref, s_scr, tail_scr, act_scr) = refs
    else:
        (qkv_ref, z_ref, ba_ref, cw_ref, ad_ref, gam_ref,
         o_ref, sout_ref, s_scr, tail_scr, act_scr) = refs
    R = hg * C
    ng = H_V // hg
    log2c = int(math.log2(C))
    c = pl.program_id(1)

    @pl.when(c == 0)
    def _init():
        if has_state:
            tail_scr[...] = jnp.zeros(tail_scr.shape, F32)
            for s in range(nseq):
                s_scr[s * H_V:(s + 1) * H_V] = s0_ref[s]
                tail_scr[(s + 1) * 8 - (CONV_W - 1):(s + 1) * 8] = buf_ref[s]
        else:
            s_scr[...] = jnp.zeros(s_scr.shape, F32)
            tail_scr[...] = jnp.zeros(tail_scr.shape, F32)

    cw = cw_ref[...]
    qkv_all = qkv_ref[...].astype(F32)
    n_prev = CONV_W - 1
    CB = nck * C
    for s in range(nseq):
        x = qkv_all[s * CB:(s + 1) * CB]
        xe = jnp.concatenate([tail_scr[s * 8:(s + 1) * 8], x], axis=0)
        y = x * cw[n_prev:CONV_W]
        for j in range(n_prev):
            y = y + xe[8 - n_prev + j:8 - n_prev + j + CB] * cw[j:j + 1]
        tail_scr[s * 8:(s + 1) * 8] = x[CB - 8:CB]
        act_scr[s * CB:(s + 1) * CB] = _silu(y)
        for off, scale in ((0, D_K ** -0.5), (Q_W, 1.0)):
            for kh in range(H_K):
                cols = slice(off + kh * D_K, off + (kh + 1) * D_K)
                v = act_scr[s * CB:(s + 1) * CB, cols]
                act_scr[s * CB:(s + 1) * CB, cols] = v * (
                    lax.rsqrt(jnp.sum(v * v, axis=-1, keepdims=True) + NORM_EPS) * scale)

    ri = lax.broadcasted_iota(jnp.int32, (R, R), 0)
    ci = lax.broadcasted_iota(jnp.int32, (R, R), 1)
    same = (ri >> log2c) == (ci >> log2c)
    incl = same & (ri >= ci)
    strict = same & (ri > ci)
    eye = ri == ci
    is_last = ci == (((ri >> log2c) << log2c) + (C - 1))
    eye_f = jnp.where(eye, 1.0, 0.0).astype(F32)
    cum_u = jnp.where(same & (ri <= ci), 1.0, 0.0).astype(BF16)

    nrow = ba_ref.shape[2] // 2
    adv = ad_ref[...]
    dd = functools.partial(jnp.dot, preferred_element_type=F32)
    beta_of, G_of = {}, {}
    for s in range(nseq):
        for k in range(nck):
            bav = ba_ref[s, k]
            beta_of[s, k] = _sigmoid(bav[0:nrow])
            g_rows = -jnp.exp(adv[0:nrow]) * _softplus(bav[nrow:2 * nrow] + adv[nrow:2 * nrow])
            g1 = g_rows.astype(BF16)
            r1 = g_rows - g1.astype(F32)
            g2 = r1.astype(BF16)
            g3 = (r1 - g2.astype(F32)).astype(BF16)
            G_of[s, k] = dd(g1, cum_u) + (dd(g2, cum_u) + dd(g3, cum_u))

    gam = gam_ref[...]
    rep = H_V // H_K

    def stacked(d, off, per_k_head):
        blocks = [(h // rep if per_k_head else h) for h in d["heads"]]
        return jnp.concatenate([act_scr[d["rows"], off + b * D_K:off + (b + 1) * D_K] for b in blocks], axis=0)

    log2b = min(log2c, int(math.log2(SOLVE_BASE)))
    same_base = ((ri ^ ci) >> log2b) == 0
    row_head = lax.broadcasted_iota(jnp.int32, (R, D_V), 0) >> log2c
    z_all = z_ref[...].astype(F32)
    grp = [dict(s=s, k=k, g=g, rows=slice(s * CB + k * C, s * CB + (k + 1) * C), s0=s * H_V,
                heads=[g * hg + hh for hh in range(hg)])
           for k in range(nck) for s in range(nseq) for g in range(ng)]
    for d in grp:
        kst = stacked(d, Q_W, True)
        d["kk"] = _dot_nt(kst, kst)
        d["qk"] = _dot_nt(stacked(d, 0, True), kst)
    for d in grp:
        grow = G_of[d["s"], d["k"]][d["g"]:d["g"] + 1]
        brow = beta_of[d["s"], d["k"]][d["g"]:d["g"] + 1]
        gcol = jnp.sum(eye_f * grow, axis=1, keepdims=True)
        bcol = jnp.sum(eye_f * brow, axis=1, keepdims=True)
        glast = jnp.sum(jnp.where(is_last, grow, 0.0), axis=1, keepdims=True)
        dm = jnp.where(incl, jnp.exp(jnp.minimum(gcol - grow, 0.0)), 0.0)
        d.update(gcol=gcol, bcol=bcol, glast=glast, eg=jnp.exp(gcol))
        d["m"] = jnp.where(strict, -(d["kk"] * bcol * dm), 0.0)
        d["attn"] = jnp.where(incl, d["qk"] * dm, 0.0)
        d["pw"] = jnp.where(same_base, d["m"], 0.0)
        d["t"] = eye_f + d["pw"]

    for r in range(log2b):
        last = r == log2b - 1
        for d in grp:
            if r == 0:
                if not last:
                    d["pw"] = _dot(d["pw"], d["pw"])
            elif last:
                d["t"] = d["t"] + _dot(d["pw"], d["t"])
            else:
                out = _dot(d["pw"], jnp.concatenate([d["pw"], d["t"]], axis=1))
                d["pw"] = out[:, :R]
                d["t"] = d["t"] + out[:, R:]
    for lev in range(log2b + 1, log2c + 1):
        coupling = ((ri ^ ci) >> (lev - 1)) == 1
        for d in grp:
            d["y"] = _dot(d["t"], jnp.where(coupling, d["m"], 0.0))
        for d in grp:
            d["t"] = d["t"] + _dot(d["y"], d["t"])
    for d in grp:
        rhs = jnp.concatenate([stacked(d, 2 * Q_W, False) * d["bcol"],
                               stacked(d, Q_W, True) * (d["bcol"] * d["eg"])], axis=1)
        d["x"] = _dot(d["t"], rhs)

    for k in range(nck):
        now = [d for d in grp if d["k"] == k]
        for d in now:
            qe = stacked(d, 0, True) * d["eg"]
            d["xs"] = [_dot(jnp.concatenate([d["x"][hh * C:(hh + 1) * C, D_V:], qe[hh * C:(hh + 1) * C]], axis=0),
                            s_scr[d["s0"] + h]) for hh, h in enumerate(d["heads"])]
        for d in now:
            d["vnew"] = jnp.concatenate(
                [d["x"][hh * C:(hh + 1) * C, :D_V] - xs[:C] for hh, xs in enumerate(d["xs"])], axis=0)
            d["o"] = jnp.concatenate([xs[C:] for xs in d["xs"]], axis=0) + _dot(d["attn"], d["vnew"])
        for d in now:
            kt = stacked(d, Q_W, True) * jnp.exp(d["glast"] - d["gcol"])
            egl = jnp.exp(d["glast"])
            for hh, h in enumerate(d["heads"]):
                sl = slice(hh * C, (hh + 1) * C)
                if C >= 16:
                    upd = _dot_tn(kt[sl], d["vnew"][sl])
                else:
                    upd = _dot_tn(kt, jnp.where(row_head == hh, d["vnew"], 0.0))
                s_scr[d["s0"] + h] = s_scr[d["s0"] + h] * egl[hh * C:hh * C + 1] + upd

    for d in grp:
        o = d["o"]
        zst = jnp.concatenate([z_all[d["rows"], h * D_V:(h + 1) * D_V] for h in d["heads"]], axis=0)
        on = o * lax.rsqrt(jnp.mean(o * o, axis=-1, keepdims=True) + NORM_EPS) * gam * _silu(zst)
        for hh, h in enumerate(d["heads"]):
            o_ref[d["rows"], h * D_V:(h + 1) * D_V] = on[hh * C:(hh + 1) * C].astype(o_ref.dtype)

    @pl.when(c == pl.num_programs(1) - 1)
    def _fin():
        for s in range(nseq):
            sout_ref[s] = s_scr[s * H_V:(s + 1) * H_V]


def _gdn(proj, ba, conv_w, a_log, dt_bias, gamma_a, row0, B, L, C, nseq, nck, conv_buf=None, s0=None):
    has_state = s0 is not None
    hg = STACK_ROWS // C
    ng = H_V // hg
    nrow = 16
    nc = L // C
    R = STACK_ROWS
    assert nseq == 1 or nc == nck
    rows_blk = nseq * nck * C
    steps = nc // nck

    def arrange(v):
        v = v.reshape(B, nc, C, ng, hg).transpose(0, 1, 3, 4, 2).reshape(B, nc, ng, R)
        return jnp.pad(v, ((0, 0), (0, 0), (0, nrow - ng), (0, 0)))

    rows = ba[row0:row0 + B * L]
    ba_arr = jnp.concatenate([arrange(rows[:, :H_V]), arrange(rows[:, H_V:2 * H_V])], axis=2)

    def arrange_param(p):
        v = jnp.broadcast_to(p.reshape(ng, hg, 1), (ng, hg, C)).reshape(ng, R)
        return jnp.pad(v, ((0, nrow - ng), (0, 0)))

    ad = jnp.concatenate([arrange_param(a_log), arrange_param(dt_bias)], axis=0)
    rb0 = row0 // rows_blk
    in_specs = [
        pl.BlockSpec((rows_blk, QKV_W), lambda b, c: (rb0 + b * steps + c, 0)),
        pl.BlockSpec((rows_blk, V_W), lambda b, c: (rb0 + b * steps + c, Z_BLK)),
        pl.BlockSpec((nseq, nck, 2 * nrow, R), lambda b, c: (b, c, 0, 0)),
        pl.BlockSpec((CONV_W, QKV_W), lambda b, c: (0, 0)),
        pl.BlockSpec((2 * nrow, R), lambda b, c: (0, 0)),
        pl.BlockSpec((1, D_V), lambda b, c: (0, 0)),
    ]
    args = [proj, proj, ba_arr, conv_w, ad, gamma_a.reshape(1, D_V)]
    if has_state:
        in_specs += [pl.BlockSpec((nseq, CONV_W - 1, QKV_W), lambda b, c: (b, 0, 0)),
                     pl.BlockSpec((nseq, H_V, D_K, D_V), lambda b, c: (b, 0, 0, 0))]
        args += [conv_buf, s0]
    return pl.pallas_call(
        functools.partial(_gdn_kernel, C=C, hg=hg, nseq=nseq, nck=nck, has_state=has_state),
        grid=(B // nseq, steps),
        in_specs=in_specs,
        out_specs=[pl.BlockSpec((rows_blk, V_W), lambda b, c: (b * steps + c, 0)),
                   pl.BlockSpec((nseq, H_V, D_K, D_V), lambda b, c: (b, 0, 0, 0))],
        out_shape=[jax.ShapeDtypeStruct((B * L, V_W), BF16),
                   jax.ShapeDtypeStruct((B, H_V, D_K, D_V), F32)],
        scratch_shapes=[pltpu.VMEM((nseq * H_V, D_K, D_V), F32),
                        pltpu.VMEM((nseq * 8, QKV_W), F32),
                        pltpu.VMEM((rows_blk, QKV_W), F32)],
        compiler_params=_cparams(("parallel", "arbitrary")),
        name="gdn_state" if has_state else "gdn_fresh",
    )(*args)


def _pool_kernel(*refs, tc, nseq, start_pos, has_state):
    if has_state:
        u_ref, buf_ref, o_ref, tail_scr = refs
    else:
        u_ref, o_ref, tail_scr = refs
    hist = POOL_BUF + 1
    c = pl.program_id(1)

    @pl.when(c == 0)
    def _init():
        tail_scr[...] = jnp.zeros(tail_scr.shape, F32)
        if has_state:
            for s in range(nseq):
                tail_scr[s * hist + 1:(s + 1) * hist] = buf_ref[s]

    u_all = u_ref[...].astype(F32)
    pos1 = (start_pos + 1 + c * tc + lax.broadcasted_iota(jnp.int32, (tc, 1), 0)).astype(F32)
    for s in range(nseq):
        u = u_all[s * tc:(s + 1) * tc]
        xe = jnp.concatenate([tail_scr[s * hist:(s + 1) * hist], u], axis=0)
        tail_scr[s * hist:(s + 1) * hist] = xe[tc:tc + hist]
        for gi, win in enumerate(POOL_WINDOWS):
            sl = slice(gi * POOL_GROUP_W, (gi + 1) * POOL_GROUP_W)
            acc = xe[:, sl]
            shift = 1
            while shift < win:
                acc = acc + pltpu.roll(acc, shift, 0)
                shift *= 2
            cnt = jnp.minimum(float(win), pos1)
            o_ref[s * tc:(s + 1) * tc, sl] = acc[hist:hist + tc] / cnt - u[:, sl]


def _pool(proj, row0, B, L, tc, nseq, start_pos, pool_buf=None):
    has_state = pool_buf is not None
    nc = L // tc
    assert nseq == 1 or nc == 1
    rows_blk = nseq * tc
    rb0 = row0 // rows_blk
    in_specs = [pl.BlockSpec((rows_blk, D_MODEL), lambda b, c: (rb0 + b * nc + c, U_BLK))]
    args = [proj]
    if has_state:
        in_specs.append(pl.BlockSpec((nseq, POOL_BUF, D_MODEL), lambda b, c: (b, 0, 0)))
        args.append(pool_buf)
    return pl.pallas_call(
        functools.partial(_pool_kernel, tc=tc, nseq=nseq, start_pos=start_pos, has_state=has_state),
        grid=(B // nseq, nc),
        in_specs=in_specs,
        out_specs=pl.BlockSpec((rows_blk, D_MODEL), lambda b, c: (b * nc + c, 0)),
        out_shape=jax.ShapeDtypeStruct((B * L, D_MODEL), F32),
        scratch_shapes=[pltpu.VMEM((nseq * (POOL_BUF + 1), D_MODEL), F32)],
        compiler_params=_cparams(("parallel", "arbitrary")),
        name="pool_state" if has_state else "pool_fresh",
    )(*args)


def _post_kernel(pa_ref, pb_ref, oa_ref, ob_ref, ga_ref, gb_ref, xa_ref, xb_ref, wp_ref, ps_ref, wbb_ref, wba_ref,
                 wo_ref, g_ref, b_ref, h_ref, hp_ref, *, alpha, tiles_a):
    first = pl.program_id(0) < tiles_a
    x = jnp.where(first, xa_ref[...], xb_ref[...])
    pooled = jnp.where(first, pa_ref[...], pb_ref[...])
    o_gated = jnp.where(first, oa_ref[...], ob_ref[...])
    mixed = jnp.concatenate(
        [_dot(pooled[:, gi * POOL_GROUP_W:(gi + 1) * POOL_GROUP_W], wp_ref[gi]) for gi in range(len(POOL_WINDOWS))],
        axis=1) * ps_ref[...]
    branch_b = _dot(mixed, wbb_ref[...])
    branch_a = _dot(o_gated, wba_ref[...])
    merged = _sigmoid(ga_ref[...].astype(F32)) * branch_a + _sigmoid(gb_ref[...].astype(F32)) * branch_b
    h = _layer_norm(alpha * x + _dot(merged, wo_ref[...]), g_ref[...], b_ref[...])
    h_ref[...] = h
    _store_pieces(hp_ref, h)


def _post(pooled_a, pooled_b, o_a, o_b, proj, x_a, x_b, w_pool, pool_scale, w_br_b, w_br_a, w_out, ln_g, ln_b,
          alpha, tm):
    t = x_a.shape[0] + x_b.shape[0]
    tiles_a = x_a.shape[0] // tm
    row = lambda i: (i, 0)
    row_a = lambda i: (jnp.minimum(i, tiles_a - 1), 0)
    row_b = lambda i: (jnp.maximum(i - tiles_a, 0), 0)
    const2 = lambda i: (0, 0)
    return pl.pallas_call(
        functools.partial(_post_kernel, alpha=alpha, tiles_a=tiles_a),
        grid=(t // tm,),
        in_specs=[pl.BlockSpec((tm, D_MODEL), row_a),
                  pl.BlockSpec((tm, D_MODEL), row_b),
                  pl.BlockSpec((tm, V_W), row_a),
                  pl.BlockSpec((tm, V_W), row_b),
                  pl.BlockSpec((tm, D_MODEL), lambda i: (i, U_BLK + 1)),
                  pl.BlockSpec((tm, D_MODEL), lambda i: (i, U_BLK + 2)),
                  pl.BlockSpec((tm, D_MODEL), row_a),
                  pl.BlockSpec((tm, D_MODEL), row_b),
                  pl.BlockSpec(w_pool.shape, lambda i: (0, 0, 0)),
                  pl.BlockSpec((1, D_MODEL), const2),
                  pl.BlockSpec(w_br_b.shape, const2),
                  pl.BlockSpec(w_br_a.shape, const2),
                  pl.BlockSpec(w_out.shape, const2),
                  pl.BlockSpec((1, D_MODEL), const2),
                  pl.BlockSpec((1, D_MODEL), const2)],
        out_specs=[pl.BlockSpec((tm, D_MODEL), row), pl.BlockSpec((N_PIECE, tm, PIECE_W), lambda i: (0, i, 0))],
        out_shape=[jax.ShapeDtypeStruct((t, D_MODEL), F32),
                   jax.ShapeDtypeStruct((N_PIECE, t, PIECE_W), jnp.uint32)],
        compiler_params=_cparams(("parallel",)),
        name="post_mixers",
    )(pooled_a, pooled_b, o_a, o_b, proj, proj, x_a, x_b, w_pool, pool_scale, w_br_b, w_br_a, w_out, ln_g, ln_b)


def _router_kernel(h_ref, wt_ref, bias_ref, idx_ref, wts_ref, rank_ref, cnt_ref, carry_scr, *, tm):
    i = pl.program_id(0)

    @pl.when(i == 0)
    def _init():
        carry_scr[...] = jnp.zeros(carry_scr.shape, F32)

    logits = _dot3_nt(wt_ref[...], h_ref[...])
    sc = _sigmoid(logits)
    ch = sc + bias_ref[...]
    neg = -jnp.inf
    e_in = lax.broadcasted_iota(jnp.int32, (GROUP_SZ, tm), 0)
    gs_rows = []
    for g in range(N_GROUP):
        blk = ch[g * GROUP_SZ:(g + 1) * GROUP_SZ]
        m1 = jnp.max(blk, axis=0, keepdims=True)
        i1 = jnp.min(jnp.where(blk == m1, e_in, GROUP_SZ), axis=0, keepdims=True)
        m2 = jnp.max(jnp.where(e_in == i1, neg, blk), axis=0, keepdims=True)
        gs_rows.append(m1 + m2)
    gs = jnp.concatenate(gs_rows, axis=0)
    g_io = lax.broadcasted_iota(jnp.int32, (N_GROUP, tm), 0)
    e_io = lax.broadcasted_iota(jnp.int32, (N_EXPERTS, tm), 0)
    e_grp = e_io >> int(math.log2(GROUP_SZ))
    masked = jnp.full((N_EXPERTS, tm), neg, F32)
    for _ in range(TOPK_GROUP):
        mx = jnp.max(gs, axis=0, keepdims=True)
        gi = jnp.min(jnp.where(gs == mx, g_io, N_GROUP), axis=0, keepdims=True)
        gs = jnp.where(g_io == gi, neg, gs)
        masked = jnp.where(e_grp == gi, ch, masked)
    idx_rows, w_rows = [], []
    onehot = jnp.zeros((N_EXPERTS, tm), F32)
    for _ in range(TOP_K):
        mx = jnp.max(masked, axis=0, keepdims=True)
        ei = jnp.min(jnp.where(masked == mx, e_io, N_EXPERTS), axis=0, keepdims=True)
        hit = e_io == ei
        idx_rows.append(ei)
        w_rows.append(jnp.sum(jnp.where(hit, sc, 0.0), axis=0, keepdims=True))
        onehot = jnp.where(hit, 1.0, onehot)
        masked = jnp.where(hit, neg, masked)
    wsel = jnp.concatenate(w_rows, axis=0)
    wts_ref[...] = wsel / jnp.sum(wsel, axis=0, keepdims=True) * ROUTED_SCALE
    idx_ref[...] = jnp.concatenate(idx_rows, axis=0)

    tr = lax.broadcasted_iota(jnp.int32, (tm, tm), 0)
    tc_ = lax.broadcasted_iota(jnp.int32, (tm, tm), 1)
    before = jnp.where(tr < tc_, 1.0, 0.0).astype(BF16)
    cum = jnp.dot(onehot.astype(BF16), before, preferred_element_type=F32) + carry_scr[...]
    rank_ref[...] = jnp.concatenate(
        [jnp.sum(jnp.where(e_io == ei, cum, 0.0), axis=0, keepdims=True) for ei in idx_rows],
        axis=0).astype(jnp.int32)
    carry_scr[...] = carry_scr[...] + jnp.sum(onehot, axis=1, keepdims=True)
    cnt_ref[...] = carry_scr[...]


def _router(h, w_router_t, bias_col, tm):
    t = h.shape[0]
    return pl.pallas_call(
        functools.partial(_router_kernel, tm=tm),
        grid=(t // tm,),
        in_specs=[pl.BlockSpec((tm, D_MODEL), lambda i: (i, 0)),
                  pl.BlockSpec((N_EXPERTS, D_MODEL), lambda i: (0, 0)),
                  pl.BlockSpec((N_EXPERTS, 1), lambda i: (0, 0))],
        out_specs=[pl.BlockSpec((TOP_K, tm), lambda i: (0, i)),
                   pl.BlockSpec((TOP_K, tm), lambda i: (0, i)),
                   pl.BlockSpec((TOP_K, tm), lambda i: (0, i)),
                   pl.BlockSpec((N_EXPERTS, 1), lambda i: (0, 0))],
        out_shape=[jax.ShapeDtypeStruct((TOP_K, t), jnp.int32),
                   jax.ShapeDtypeStruct((TOP_K, t), F32),
                   jax.ShapeDtypeStruct((TOP_K, t), jnp.int32),
                   jax.ShapeDtypeStruct((N_EXPERTS, 1), F32)],
        scratch_shapes=[pltpu.VMEM((N_EXPERTS, 1), F32)],
        compiler_params=_cparams(("arbitrary",)),
        name="router",
    )(h, w_router_t, bias_col)


def _dest_kernel(idx_ref, rank_ref, start_ref, dest_ref, *, tm, n_rows):
    e_io = lax.broadcasted_iota(jnp.int32, (N_EXPERTS, tm), 0)
    starts = start_ref[...]
    rows = []
    for k in range(TOP_K):
        seg = jnp.sum(jnp.where(e_io == idx_ref[k:k + 1, :], starts, 0.0), axis=0, keepdims=True)
        rows.append(seg.astype(jnp.int32) + rank_ref[k:k + 1, :])
    base = jnp.concatenate(rows, axis=0)
    for p in range(N_PIECE):
        dest_ref[p] = base + p * n_rows


def _dest(idx_t, rank_t, seg_start_col, n_rows, tm):
    t = idx_t.shape[1]
    blk = pl.BlockSpec((TOP_K, tm), lambda i: (0, i))
    return pl.pallas_call(
        functools.partial(_dest_kernel, tm=tm, n_rows=n_rows),
        grid=(t // tm,),
        in_specs=[blk, blk, pl.BlockSpec((N_EXPERTS, 1), lambda i: (0, 0))],
        out_specs=pl.BlockSpec((N_PIECE, TOP_K, tm), lambda i: (0, 0, i)),
        out_shape=jax.ShapeDtypeStruct((N_PIECE, TOP_K, t), jnp.int32),
        compiler_params=_cparams(("parallel",)),
        name="dispatch_rows",
    )(idx_t, rank_t, seg_start_col)


def _expert_kernel(be_ref, nu_ref, first_ref, nxt_ref, slot_ref, x_ref, wg_hbm, wu_hbm, wd_hbm, y_ref,
                   wg_buf, wu_buf, wd_buf, wgu_scr, wd_scr, sems):
    i = pl.program_id(0)
    live = i < nu_ref[0]

    def fetch(e, slot):
        return [pltpu.make_async_copy(wg_hbm.at[e], wg_buf.at[slot], sems.at[slot, 0]),
                pltpu.make_async_copy(wu_hbm.at[e], wu_buf.at[slot], sems.at[slot, 1]),
                pltpu.make_async_copy(wd_hbm.at[e], wd_buf.at[slot], sems.at[slot, 2])]

    @pl.when(live & (first_ref[i] == 1))
    def _new_expert():
        slot = slot_ref[i]

        @pl.when(i == 0)
        def _():
            for cp in fetch(be_ref[i], slot):
                cp.start()

        @pl.when(nxt_ref[i] >= 0)
        def _():
            for cp in fetch(nxt_ref[i], 1 - slot):
                cp.start()

        for cp in fetch(be_ref[i], slot):
            cp.wait()
        wgu_scr[:, :D_EXPERT] = wg_buf[slot].astype(BF16)
        wgu_scr[:, D_EXPERT:] = wu_buf[slot].astype(BF16)
        wd_scr[...] = wd_buf[slot].astype(BF16)

    @pl.when(live)
    def _():
        n_sub = 2
        sub = EXPERT_BM // n_sub
        xs = [jnp.concatenate([c.astype(BF16) for c in
                               _load_pieces([x_ref[p, s * sub:(s + 1) * sub] for p in range(N_PIECE)])], axis=1)
              for s in range(n_sub)]
        gus = [jnp.dot(x, wgu_scr[...], preferred_element_type=F32) for x in xs]
        acts = [(_silu(gu[:, :D_EXPERT]) * gu[:, D_EXPERT:]).astype(BF16) for gu in gus]
        ys = [jnp.dot(a, wd_scr[...], preferred_element_type=F32) for a in acts]
        for s, y in enumerate(ys):
            for p in range(N_PIECE):
                y_ref[p, s * sub:(s + 1) * sub] = _pack_halves(y[:, 2 * p * PIECE_W:2 * (p + 1) * PIECE_W])


def _experts(xs, blk_exp, n_used, blk_first, blk_next, blk_slot, w_gate, w_up, w_down):
    n_rows = xs.shape[1]
    n_blocks = n_rows // EXPERT_BM

    def row_block(i, be, nu, *_):
        return (0, jnp.minimum(i, nu[0] - 1), 0)

    grid_spec = pltpu.PrefetchScalarGridSpec(
        num_scalar_prefetch=5,
        grid=(n_blocks,),
        in_specs=[pl.BlockSpec((N_PIECE, EXPERT_BM, PIECE_W), row_block),
                  pl.BlockSpec(memory_space=pl.ANY),
                  pl.BlockSpec(memory_space=pl.ANY),
                  pl.BlockSpec(memory_space=pl.ANY)],
        out_specs=pl.BlockSpec((N_PIECE, EXPERT_BM, PIECE_W), row_block),
        scratch_shapes=[pltpu.VMEM((2, D_MODEL, D_EXPERT), F32),
                        pltpu.VMEM((2, D_MODEL, D_EXPERT), F32),
                        pltpu.VMEM((2, D_EXPERT, D_MODEL), F32),
                        pltpu.VMEM((D_MODEL, 2 * D_EXPERT), BF16),
                        pltpu.VMEM((D_EXPERT, D_MODEL), BF16),
                        pltpu.SemaphoreType.DMA((2, 3))],
    )
    return pl.pallas_call(
        _expert_kernel,
        grid_spec=grid_spec,
        out_shape=jax.ShapeDtypeStruct((N_PIECE, n_rows, PIECE_W), jnp.uint32),
        compiler_params=_cparams(("arbitrary",)),
        name="experts",
    )(blk_exp, n_used, blk_first, blk_next, blk_slot, xs, w_gate, w_up, w_down)


SC_WINDOW = 128
V7X_SC_CORES = 2
V7X_SC_SUBCORES = 16


def _sc_mesh():
    return plsc.VectorSubcoreMesh(core_axis_name="core", subcore_axis_name="subcore",
                                  num_cores=V7X_SC_CORES, num_subcores=V7X_SC_SUBCORES)


def _sc_scatter_rows(src, dest, n_rows, seg, repeat):
    d = src.shape[1]
    n_idx = dest.shape[0]
    seg_blocks = seg // SC_WINDOW
    dest2 = dest.reshape(1, n_idx)

    def src_block(i):
        return ((i // (repeat * seg_blocks)) * seg_blocks + i % seg_blocks, 0)

    @functools.partial(pl.kernel, out_type=jax.ShapeDtypeStruct((n_rows, d), src.dtype), mesh=_sc_mesh(),
                       scratch_types=[], name="sc_dispatch")
    def run(src_hbm, idx_hbm, out_hbm):
        def body(rows_vmem, idx_vmem):
            pltpu.sync_copy(rows_vmem, out_hbm.at[idx_vmem.at[0]])

        pltpu.emit_pipeline(
            body,
            grid=(n_idx // SC_WINDOW,),
            in_specs=[pl.BlockSpec((SC_WINDOW, d), src_block),
                      pl.BlockSpec((1, SC_WINDOW), lambda i: (0, i))],
            out_specs=[],
            core_axis_name=("core", "subcore"),
            dimension_semantics=(pltpu.PARALLEL,),
        )(src_hbm, idx_hbm)

    return run(src, dest2)


def _sc_gather_rows(table, idx):
    d = table.shape[1]
    n_idx = idx.shape[0]
    idx2 = idx.reshape(1, n_idx)

    @functools.partial(pl.kernel, out_type=jax.ShapeDtypeStruct((n_idx, d), table.dtype), mesh=_sc_mesh(),
                       scratch_types=[], name="sc_combine_gather")
    def run(table_hbm, idx_hbm, out_hbm):
        def body(idx_vmem, rows_vmem):
            pltpu.sync_copy(table_hbm.at[idx_vmem.at[0]], rows_vmem)

        pltpu.emit_pipeline(
            body,
            grid=(n_idx // SC_WINDOW,),
            in_specs=[pl.BlockSpec((1, SC_WINDOW), lambda i: (0, i))],
            out_specs=[pl.BlockSpec((SC_WINDOW, d), lambda i: (i, 0))],
            core_axis_name=("core", "subcore"),
            dimension_semantics=(pltpu.PARALLEL,),
        )(idx_hbm, out_hbm)

    return run(table, idx2)


def _combine_kernel(yg_ref, wts_ref, h_ref, wgu_ref, wd_ref, g_ref, b_ref, outa_ref, outb_ref, *, alpha, tiles_a):
    wts = wts_ref[...]
    acc = None
    for k in range(TOP_K):
        cols = _load_pieces([yg_ref[p, k] for p in range(N_PIECE)])
        wk = wts[:, k:k + 1]
        acc = [c * wk for c in cols] if acc is None else [a + c * wk for a, c in zip(acc, cols)]
    routed = jnp.concatenate(acc, axis=1)
    h = h_ref[...]
    gu = _dot(h, wgu_ref[...])
    shared = _dot(_silu(gu[:, :D_EXPERT]) * gu[:, D_EXPERT:], wd_ref[...])
    y = _layer_norm(alpha * h + (routed + shared), g_ref[...], b_ref[...])
    i = pl.program_id(0)

    @pl.when(i < tiles_a)
    def _():
        outa_ref[...] = y

    @pl.when(i >= tiles_a)
    def _():
        outb_ref[...] = y


def _combine(yg, wts, h, w_sh_gu, w_sh_down, ln_g, ln_b, alpha, tm, t_a):
    t = h.shape[0]
    tiles_a = t_a // tm
    row = lambda i: (i, 0)
    const2 = lambda i: (0, 0)
    return pl.pallas_call(
        functools.partial(_combine_kernel, alpha=alpha, tiles_a=tiles_a),
        grid=(t // tm,),
        in_specs=[pl.BlockSpec((N_PIECE, TOP_K, tm, PIECE_W), lambda i: (0, 0, i, 0)),
                  pl.BlockSpec((tm, TOP_K), row),
                  pl.BlockSpec((tm, D_MODEL), row),
                  pl.BlockSpec(w_sh_gu.shape, const2),
                  pl.BlockSpec(w_sh_down.shape, const2),
                  pl.BlockSpec((1, D_MODEL), const2),
                  pl.BlockSpec((1, D_MODEL), const2)],
        out_specs=[pl.BlockSpec((tm, D_MODEL), lambda i: (jnp.minimum(i, tiles_a - 1), 0)),
                   pl.BlockSpec((tm, D_MODEL), lambda i: (jnp.maximum(i - tiles_a, 0), 0))],
        out_shape=[jax.ShapeDtypeStruct((t_a, D_MODEL), F32), jax.ShapeDtypeStruct((t - t_a, D_MODEL), F32)],
        compiler_params=_cparams(("arbitrary",)),
        name="combine_ln2",
    )(yg, wts, h, w_sh_gu, w_sh_down, ln_g, ln_b)


def _layer(xp, xs, s_delta, s_conv, s_pool, w_in, conv_w, a_log, dt_bias, gamma_a, w_br_a, w_pool,
           pool_scale, w_br_b, w_out, ln1_g, ln1_b, w_router, router_bias, w_exp_gate, w_exp_up,
           w_exp_down, w_sh_gate, w_sh_up, w_sh_down, ln2_g, ln2_b, alpha):
    Bp, Lp, _ = xp.shape
    Bs, Ls, _ = xs.shape
    Tp, Ts = Bp * Lp, Bs * Ls
    T = Tp + Ts
    x_p = xp.reshape(Tp, D_MODEL)
    x_s = xs.reshape(Ts, D_MODEL)

    o_b, o_u = QKV_W + V_W, QKV_W + V_W + 2 * H_V
    w_main = jnp.concatenate([w_in[:, :o_b], w_in[:, o_u:]], axis=1).astype(BF16)
    w_ba = jnp.pad(w_in[:, o_b:o_u], ((0, 0), (0, 128 - 2 * H_V))).astype(BF16)
    proj, ba = _in_proj(x_p, x_s, w_main, w_ba, IN_PROJ_TM, IN_PROJ_TN)

    o_p, sd_p = _gdn(proj, ba, conv_w, a_log, dt_bias, gamma_a, 0, Bp, Lp, GDN_CHUNK, 1, GDN_FRESH_CHUNKS)
    o_s, sd_s = _gdn(proj, ba, conv_w, a_log, dt_bias, gamma_a, Tp, Bs, Ls, Ls, GDN_STATE_SEQS, 1,
                     conv_buf=s_conv, s0=s_delta)

    pooled_p = _pool(proj, 0, Bp, Lp, POOL_TILE, 1, 0)
    pooled_s = _pool(proj, Tp, Bs, Ls, Ls, POOL_STATE_SEQS, PAST_LEN, pool_buf=s_pool)

    h, hp = _post(pooled_p, pooled_s, o_p, o_s, proj, x_p, x_s, w_pool.astype(BF16), pool_scale.reshape(1, D_MODEL),
                  w_br_b.astype(BF16), w_br_a.astype(BF16), w_out.astype(BF16),
                  ln1_g.reshape(1, D_MODEL), ln1_b.reshape(1, D_MODEL), alpha, POST_TM)

    idx_t, wts_t, rank_t, cnt = _router(h, w_router.T, router_bias.reshape(N_EXPERTS, 1), ROUTER_TM)
    counts = cnt[:, 0].astype(jnp.int32)
    padded = ((counts + EXPERT_BM - 1) // EXPERT_BM) * EXPERT_BM
    pends = jnp.cumsum(padded)
    pstarts = pends - padded
    n_blocks = (T * TOP_K + N_EXPERTS * (EXPERT_BM - 1) + EXPERT_BM - 1) // EXPERT_BM
    n_rows = n_blocks * EXPERT_BM
    piece_rows = _dest(idx_t, rank_t, pstarts.astype(F32).reshape(N_EXPERTS, 1), n_rows, ROUTER_TM)
    blk_start = jnp.arange(n_blocks, dtype=jnp.int32) * EXPERT_BM
    blk_exp = jnp.minimum(jnp.sum((pends[None, :] <= blk_start[:, None]).astype(jnp.int32), axis=1),
                          N_EXPERTS - 1)
    n_used = (pends[-1:] // EXPERT_BM).astype(jnp.int32)
    has_rows = counts > 0
    e_ids = jnp.arange(N_EXPERTS, dtype=jnp.int32)
    slot_of = (jnp.cumsum(has_rows.astype(jnp.int32)) - 1) & 1
    later = jnp.where(has_rows, e_ids, N_EXPERTS)
    next_of = jnp.concatenate([lax.cummin(later, reverse=True)[1:], jnp.full((1,), N_EXPERTS, jnp.int32)])
    next_of = jnp.where(next_of < N_EXPERTS, next_of, -1)
    blk_first = (blk_start == pstarts[blk_exp]).astype(jnp.int32)
    blk_next = next_of[blk_exp]
    blk_slot = slot_of[blk_exp]

    piece_idx = piece_rows.reshape(N_PIECE * TOP_K * T)
    x_sorted = _sc_scatter_rows(hp.reshape(N_PIECE * T, PIECE_W), piece_idx, N_PIECE * n_rows, T, TOP_K)
    y_sorted = _experts(x_sorted.reshape(N_PIECE, n_rows, PIECE_W), blk_exp, n_used, blk_first, blk_next,
                        blk_slot, w_exp_gate, w_exp_up, w_exp_down)
    yg = _sc_gather_rows(y_sorted.reshape(N_PIECE * n_rows, PIECE_W), piece_idx)
    yg = yg.reshape(N_PIECE, TOP_K, T, PIECE_W)

    w_sh_gu = jnp.concatenate([w_sh_gate, w_sh_up], axis=1).astype(BF16)
    y_p, y_s = _combine(yg, wts_t.T, h, w_sh_gu, w_sh_down.astype(BF16),
                        ln2_g.reshape(1, D_MODEL), ln2_b.reshape(1, D_MODEL), alpha, COMBINE_TM, Tp)

    def carried_rows(keep, col0, width, name, old_p, old_s):
        parts, counts = [], []
        for x2d, B, L in ((x_p, Bp, Lp), (x_s, Bs, Ls)):
            n = min(L, keep)
            rows = jnp.arange(B, dtype=jnp.int32)[:, None] * L + (L - n) + jnp.arange(n, dtype=jnp.int32)
            parts.append(jnp.take(x2d, rows.reshape(-1), axis=0).astype(BF16))
            counts.append((B, n))
        n_tail = sum(B * n for B, n in counts)
        n_pad = -(-n_tail // 128) * 128
        xt = jnp.pad(jnp.concatenate(parts, axis=0), ((0, n_pad - n_tail), (0, 0)))
        tail = _matmul(xt, w_main, col0, width, n_pad, 1024, F32, name)
        outs, r0 = [], 0
        for (B, n), old in zip(counts, (old_p, old_s)):
            new = tail[r0:r0 + B * n].reshape(B, n, width)
            outs.append(new if n == keep else jnp.concatenate([old[:, n:], new], axis=1))
            r0 += B * n
        return outs

    new_conv_p, new_conv_s = carried_rows(CONV_W - 1, 0, QKV_W, "in_proj_conv_rows",
                                          jnp.zeros((Bp, CONV_W - 1, QKV_W), F32), s_conv)
    new_pool_p, new_pool_s = carried_rows(POOL_BUF, U_BLK * D_MODEL, D_MODEL, "in_proj_pool_rows",
                                          jnp.zeros((Bp, POOL_BUF, D_MODEL), F32), s_pool)
    return (y_p.reshape(Bp, Lp, D_MODEL), y_s.reshape(Bs, Ls, D_MODEL),
            sd_p, new_conv_p, new_pool_p, sd_s, new_conv_s, new_pool_s)


def kernel(x_prompt, x_sample, state_delta, state_conv, state_pool, w_in, conv_w, a_log, dt_bias, gamma_a,
           w_br_a, w_pool, pool_scale, w_br_b, w_out, ln1_g, ln1_b, w_router, router_bias,
           w_exp_gate, w_exp_up, w_exp_down, w_sh_gate, w_sh_up, w_sh_down, ln2_g, ln2_b):
    depth = w_in.shape[0]
    alpha = (2 * depth) ** 0.25
    yp, ys = x_prompt, x_sample
    outs = [[] for _ in range(6)]
    for l in range(depth):
        res = _layer(yp, ys, state_delta[l], state_conv[l], state_pool[l], w_in[l], conv_w[l], a_log[l],
                     dt_bias[l], gamma_a[l], w_br_a[l], w_pool[l], pool_scale[l], w_br_b[l], w_out[l],
                     ln1_g[l], ln1_b[l], w_router[l], router_bias[l], w_exp_gate[l], w_exp_up[l],
                     w_exp_down[l], w_sh_gate[l], w_sh_up[l], w_sh_down[l], ln2_g[l], ln2_b[l], alpha)
        yp, ys = res[0], res[1]
        for lst, v in zip(outs, res[2:]):
            lst.append(v)
    return (yp, ys) + tuple(jnp.stack(v) for v in outs)
```

```python
import functools
import math

import jax
import jax.numpy as jnp
from jax import lax
from jax.experimental import pallas as pl
from jax.experimental.pallas import tpu as pltpu
from jax.experimental.pallas import tpu_sc as plsc

F32 = jnp.float32
BF16 = jnp.bfloat16

D_MODEL = 1024
H_K = 8
D_K = 128
H_V = 16
D_V = 128
Q_W = H_K * D_K
V_W = H_V * D_V
QKV_W = 2 * Q_W + V_W
CONV_W = 4
POOL_WINDOWS = (2, 4, 8, 16)
POOL_GROUP_W = D_MODEL // len(POOL_WINDOWS)
POOL_BUF = max(POOL_WINDOWS) - 1
N_EXPERTS = 256
TOP_K = 8
N_GROUP = 8
TOPK_GROUP = 4
GROUP_SZ = N_EXPERTS // N_GROUP
D_EXPERT = D_MODEL // 4
ROUTED_SCALE = 2.5
LN_EPS = 1e-5
NORM_EPS = 1e-6
PAST_LEN = 16384

MAIN_W = QKV_W + V_W + 3 * D_MODEL
Z_BLK = QKV_W // V_W
U_BLK = (QKV_W + V_W) // D_MODEL

IN_PROJ_TM = 1024
IN_PROJ_TN = MAIN_W // 3
GDN_CHUNK = 64
POOL_TILE = 256
POST_TM = 256
ROUTER_TM = 512
COMBINE_TM = 512

STACK_ROWS = 128
SOLVE_BASE = 16
GDN_FRESH_CHUNKS = 4
EXPERT_BM = 512
GDN_STATE_SEQS = 8
POOL_STATE_SEQS = 16
VMEM_LIMIT = 56 * 1024 * 1024


def _cparams(sem):
    return pltpu.CompilerParams(dimension_semantics=sem, vmem_limit_bytes=VMEM_LIMIT)


def _sigmoid(x):
    return 0.5 * jnp.tanh(0.5 * x) + 0.5


def _silu(x):
    return x * _sigmoid(x)


def _softplus(x):
    return jnp.maximum(x, 0.0) + jnp.log(1.0 + jnp.exp(-jnp.abs(x)))


def _dot(a, b):
    return jnp.dot(a.astype(BF16), b.astype(BF16), preferred_element_type=F32)


def _dot_nt(a, b):
    return lax.dot_general(a.astype(BF16), b.astype(BF16), (((1,), (1,)), ((), ())),
                           preferred_element_type=F32)


def _dot_tn(a, b):
    return lax.dot_general(a.astype(BF16), b.astype(BF16), (((0,), (0,)), ((), ())),
                           preferred_element_type=F32)


def _split(a):
    hi = a.astype(BF16)
    lo = (a - hi.astype(F32)).astype(BF16)
    return hi, lo


def _dot3_nt(a, b):
    ah, al = _split(a)
    bh, bl = _split(b)
    d = functools.partial(lax.dot_general, dimension_numbers=(((1,), (1,)), ((), ())),
                          preferred_element_type=F32)
    return d(ah, bh) + (d(ah, bl) + d(al, bh))


def _pack_halves(x):
    n = x.shape[1] // 2
    hi = lax.bitcast_convert_type(x[:, :n].astype(BF16).astype(F32), jnp.uint32)
    lo = lax.bitcast_convert_type(x[:, n:].astype(BF16).astype(F32), jnp.uint32)
    return (hi & jnp.uint32(0xFFFF0000)) | (lo >> 16)


def _unpack_halves(w):
    hi = lax.bitcast_convert_type(w & jnp.uint32(0xFFFF0000), F32)
    lo = lax.bitcast_convert_type(w << 16, F32)
    return hi, lo


N_PIECE = 2
PIECE_W = D_MODEL // (2 * N_PIECE)


def _store_pieces(ref, x):
    for p in range(N_PIECE):
        ref[p] = _pack_halves(x[:, 2 * p * PIECE_W:2 * (p + 1) * PIECE_W])


def _load_pieces(pieces):
    cols = []
    for w in pieces:
        cols.extend(_unpack_halves(w))
    return cols


def _layer_norm(x, g, b):
    mu = jnp.mean(x, axis=-1, keepdims=True)
    xc = x - mu
    var = jnp.mean(xc * xc, axis=-1, keepdims=True)
    return xc * lax.rsqrt(var + LN_EPS) * g + b


def _in_proj_kernel(xa_ref, xb_ref, w_ref, wba_ref, o_ref, ba_ref, x_scr, *, tiles_a):
    @pl.when(pl.program_id(1) == 0)
    def _():
        x = jnp.where(pl.program_id(0) < tiles_a, xa_ref[...], xb_ref[...]).astype(BF16)
        x_scr[...] = x
        ba_ref[...] = jnp.dot(x, wba_ref[...], preferred_element_type=F32)

    o_ref[...] = jnp.dot(x_scr[...], w_ref[...], preferred_element_type=F32).astype(o_ref.dtype)


def _in_proj(x_a, x_b, w_main, w_ba, tm, tn):
    k = x_a.shape[1]
    t = x_a.shape[0] + x_b.shape[0]
    n = w_main.shape[1]
    tiles_a = x_a.shape[0] // tm
    return pl.pallas_call(
        functools.partial(_in_proj_kernel, tiles_a=tiles_a),
        grid=(t // tm, n // tn),
        in_specs=[pl.BlockSpec((tm, k), lambda i, j: (jnp.minimum(i, tiles_a - 1), 0)),
                  pl.BlockSpec((tm, k), lambda i, j: (jnp.maximum(i - tiles_a, 0), 0)),
                  pl.BlockSpec((k, tn), lambda i, j: (0, j)),
                  pl.BlockSpec(w_ba.shape, lambda i, j: (0, 0))],
        out_specs=[pl.BlockSpec((tm, tn), lambda i, j: (i, j)),
                   pl.BlockSpec((tm, w_ba.shape[1]), lambda i, j: (i, 0))],
        out_shape=[jax.ShapeDtypeStruct((t, n), BF16), jax.ShapeDtypeStruct((t, w_ba.shape[1]), F32)],
        scratch_shapes=[pltpu.VMEM((tm, k), BF16)],
        compiler_params=_cparams(("parallel", "arbitrary")),
        name="in_proj",
    )(x_a, x_b, w_main, w_ba)


def _mm_kernel(x_ref, w_ref, o_ref):
    o_ref[...] = jnp.dot(x_ref[...], w_ref[...], preferred_element_type=F32).astype(o_ref.dtype)


def _matmul(x, w, col0, n, tm, tn, out_dtype, name):
    t, k = x.shape
    cb0 = col0 // tn
    return pl.pallas_call(
        _mm_kernel,
        grid=(t // tm, n // tn),
        in_specs=[pl.BlockSpec((tm, k), lambda i, j: (i, 0)),
                  pl.BlockSpec((k, tn), lambda i, j: (0, cb0 + j))],
        out_specs=pl.BlockSpec((tm, tn), lambda i, j: (i, j)),
        out_shape=jax.ShapeDtypeStruct((t, n), out_dtype),
        compiler_params=_cparams(("parallel", "parallel")),
        name=name,
    )(x, w)


def _gdn_kernel(*refs, C, hg, nseq, nck, has_state):
    if has_state:
        (qkv_ref, z_ref, ba_ref, cw_ref, ad_ref, gam_ref, buf_ref, s0_ref,
         o_ref, sout_ref, s_scr, tail_scr, act_scr) = refs
    else:
        (qkv_ref, z_ref, ba_ref, cw_ref, ad_ref, gam_ref,
         o_ref, sout_ref, s_scr, tail_scr, act_scr) = refs
    R = hg * C
    ng = H_V // hg
    log2c = int(math.log2(C))
    c = pl.program_id(1)

    @pl.when(c == 0)
    def _init():
        if has_state:
            tail_scr[...] = jnp.zeros(tail_scr.shape, F32)
            for s in range(nseq):
                s_scr[s * H_V:(s + 1) * H_V] = s0_ref[s]
                tail_scr[(s + 1) * 8 - (CONV_W - 1):(s + 1) * 8] = buf_ref[s]
        else:
            s_scr[...] = jnp.zeros(s_scr.shape, F32)
            tail_scr[...] = jnp.zeros(tail_scr.shape, F32)

    cw = cw_ref[...]
    qkv_all = qkv_ref[...].astype(F32)
    n_prev = CONV_W - 1
    CB = nck * C
    for s in range(nseq):
        x = qkv_all[s * CB:(s + 1) * CB]
        xe = jnp.concatenate([tail_scr[s * 8:(s + 1) * 8], x], axis=0)
        y = x * cw[n_prev:CONV_W]
        for j in range(n_prev):
            y = y + xe[8 - n_prev + j:8 - n_prev + j + CB] * cw[j:j + 1]
        tail_scr[s * 8:(s + 1) * 8] = x[CB - 8:CB]
        act_scr[s * CB:(s + 1) * CB] = _silu(y)
        for off, scale in ((0, D_K ** -0.5), (Q_W, 1.0)):
            for kh in range(H_K):
                cols = slice(off + kh * D_K, off + (kh + 1) * D_K)
                v = act_scr[s * CB:(s + 1) * CB, cols]
                act_scr[s * CB:(s + 1) * CB, cols] = v * (
                    lax.rsqrt(jnp.sum(v * v, axis=-1, keepdims=True) + NORM_EPS) * scale)

    ri = lax.broadcasted_iota(jnp.int32, (R, R), 0)
    ci = lax.broadcasted_iota(jnp.int32, (R, R), 1)
    same = (ri >> log2c) == (ci >> log2c)
    incl = same & (ri >= ci)
    strict = same & (ri > ci)
    eye = ri == ci
    is_last = ci == (((ri >> log2c) << log2c) + (C - 1))
    eye_f = jnp.where(eye, 1.0, 0.0).astype(F32)
    cum_u = jnp.where(same & (ri <= ci), 1.0, 0.0).astype(BF16)

    nrow = ba_ref.shape[2] // 2
    adv = ad_ref[...]
    dd = functools.partial(jnp.dot, preferred_element_type=F32)
    beta_of, G_of = {}, {}
    for s in range(nseq):
        for k in range(nck):
            bav = ba_ref[s, k]
            beta_of[s, k] = _sigmoid(bav[0:nrow])
            g_rows = -jnp.exp(adv[0:nrow]) * _softplus(bav[nrow:2 * nrow] + adv[nrow:2 * nrow])
            g1 = g_rows.astype(BF16)
            r1 = g_rows - g1.astype(F32)
            g2 = r1.astype(BF16)
            g3 = (r1 - g2.astype(F32)).astype(BF16)
            G_of[s, k] = dd(g1, cum_u) + (dd(g2, cum_u) + dd(g3, cum_u))

    gam = gam_ref[...]
    rep = H_V // H_K

    def stacked(d, off, per_k_head):
        blocks = [(h // rep if per_k_head else h) for h in d["heads"]]
        return jnp.concatenate([act_scr[d["rows"], off + b * D_K:off + (b + 1) * D_K] for b in blocks], axis=0)

    log2b = min(log2c, int(math.log2(SOLVE_BASE)))
    same_base = ((ri ^ ci) >> log2b) == 0
    row_head = lax.broadcasted_iota(jnp.int32, (R, D_V), 0) >> log2c
    z_all = z_ref[...].astype(F32)
    grp = [dict(s=s, k=k, g=g, rows=slice(s * CB + k * C, s * CB + (k + 1) * C), s0=s * H_V,
                heads=[g * hg + hh for hh in range(hg)])
           for k in range(nck) for s in range(nseq) for g in range(ng)]
    for d in grp:
        kst = stacked(d, Q_W, True)
        d["kk"] = _dot_nt(kst, kst)
        d["qk"] = _dot_nt(stacked(d, 0, True), kst)
    for d in grp:
        grow = G_of[d["s"], d["k"]][d["g"]:d["g"] + 1]
        brow = beta_of[d["s"], d["k"]][d["g"]:d["g"] + 1]
        gcol = jnp.sum(eye_f * grow, axis=1, keepdims=True)
        bcol = jnp.sum(eye_f * brow, axis=1, keepdims=True)
        glast = jnp.sum(jnp.where(is_last, grow, 0.0), axis=1, keepdims=True)
        dm = jnp.where(incl, jnp.exp(jnp.minimum(gcol - grow, 0.0)), 0.0)
        d.update(gcol=gcol, bcol=bcol, glast=glast, eg=jnp.exp(gcol))
        d["m"] = jnp.where(strict, -(d["kk"] * bcol * dm), 0.0)
        d["attn"] = jnp.where(incl, d["qk"] * dm, 0.0)
        d["pw"] = jnp.where(same_base, d["m"], 0.0)
        d["t"] = eye_f + d["pw"]

    for r in range(log2b):
        last = r == log2b - 1
        for d in grp:
            if r == 0:
                if not last:
                    d["pw"] = _dot(d["pw"], d["pw"])
            elif last:
                d["t"] = d["t"] + _dot(d["pw"], d["t"])
            else:
                out = _dot(d["pw"], jnp.concatenate([d["pw"], d["t"]], axis=1))
                d["pw"] = out[:, :R]
                d["t"] = d["t"] + out[:, R:]
    for lev in range(log2b + 1, log2c + 1):
        coupling = ((ri ^ ci) >> (lev - 1)) == 1
        for d in grp:
            d["y"] = _dot(d["t"], jnp.where(coupling, d["m"], 0.0))
        for d in grp:
            d["t"] = d["t"] + _dot(d["y"], d["t"])
    for d in grp:
        rhs = jnp.concatenate([stacked(d, 2 * Q_W, False) * d["bcol"],
                               stacked(d, Q_W, True) * (d["bcol"] * d["eg"])], axis=1)
        d["x"] = _dot(d["t"], rhs)

    for k in range(nck):
        now = [d for d in grp if d["k"] == k]
        for d in now:
            qe = stacked(d, 0, True) * d["eg"]
            d["xs"] = [_dot(jnp.concatenate([d["x"][hh * C:(hh + 1) * C, D_V:], qe[hh * C:(hh + 1) * C]], axis=0),
                            s_scr[d["s0"] + h]) for hh, h in enumerate(d["heads"])]
        for d in now:
            d["vnew"] = jnp.concatenate(
                [d["x"][hh * C:(hh + 1) * C, :D_V] - xs[:C] for hh, xs in enumerate(d["xs"])], axis=0)
            d["o"] = jnp.concatenate([xs[C:] for xs in d["xs"]], axis=0) + _dot(d["attn"], d["vnew"])
        for d in now:
            kt = stacked(d, Q_W, True) * jnp.exp(d["glast"] - d["gcol"])
            egl = jnp.exp(d["glast"])
            for hh, h in enumerate(d["heads"]):
                sl = slice(hh * C, (hh + 1) * C)
                if C >= 16:
                    upd = _dot_tn(kt[sl], d["vnew"][sl])
                else:
                    upd = _dot_tn(kt, jnp.where(row_head == hh, d["vnew"], 0.0))
                s_scr[d["s0"] + h] = s_scr[d["s0"] + h] * egl[hh * C:hh * C + 1] + upd

    for d in grp:
        o = d["o"]
        zst = jnp.concatenate([z_all[d["rows"], h * D_V:(h + 1) * D_V] for h in d["heads"]], axis=0)
        on = o * lax.rsqrt(jnp.mean(o * o, axis=-1, keepdims=True) + NORM_EPS) * gam * _silu(zst)
        for hh, h in enumerate(d["heads"]):
            o_ref[d["rows"], h * D_V:(h + 1) * D_V] = on[hh * C:(hh + 1) * C].astype(o_ref.dtype)

    @pl.when(c == pl.num_programs(1) - 1)
    def _fin():
        for s in range(nseq):
            sout_ref[s] = s_scr[s * H_V:(s + 1) * H_V]


def _gdn(proj, ba, conv_w, a_log, dt_bias, gamma_a, row0, B, L, C, nseq, nck, conv_buf=None, s0=None):
    has_state = s0 is not None
    hg = STACK_ROWS // C
    ng = H_V // hg
    nrow = 16
    nc = L // C
    R = STACK_ROWS
    assert nseq == 1 or nc == nck
    rows_blk = nseq * nck * C
    steps = nc // nck

    def arrange(v):
        v = v.reshape(B, nc, C, ng, hg).transpose(0, 1, 3, 4, 2).reshape(B, nc, ng, R)
        return jnp.pad(v, ((0, 0), (0, 0), (0, nrow - ng), (0, 0)))

    rows = ba[row0:row0 + B * L]
    ba_arr = jnp.concatenate([arrange(rows[:, :H_V]), arrange(rows[:, H_V:2 * H_V])], axis=2)

    def arrange_param(p):
        v = jnp.broadcast_to(p.reshape(ng, hg, 1), (ng, hg, C)).reshape(ng, R)
        return jnp.pad(v, ((0, nrow - ng), (0, 0)))

    ad = jnp.concatenate([arrange_param(a_log), arrange_param(dt_bias)], axis=0)
    rb0 = row0 // rows_blk
    in_specs = [
        pl.BlockSpec((rows_blk, QKV_W), lambda b, c: (rb0 + b * steps + c, 0)),
        pl.BlockSpec((rows_blk, V_W), lambda b, c: (rb0 + b * steps + c, Z_BLK)),
        pl.BlockSpec((nseq, nck, 2 * nrow, R), lambda b, c: (b, c, 0, 0)),
        pl.BlockSpec((CONV_W, QKV_W), lambda b, c: (0, 0)),
        pl.BlockSpec((2 * nrow, R), lambda b, c: (0, 0)),
        pl.BlockSpec((1, D_V), lambda b, c: (0, 0)),
    ]
    args = [proj, proj, ba_arr, conv_w, ad, gamma_a.reshape(1, D_V)]
    if has_state:
        in_specs += [pl.BlockSpec((nseq, CONV_W - 1, QKV_W), lambda b, c: (b, 0, 0)),
                     pl.BlockSpec((nseq, H_V, D_K, D_V), lambda b, c: (b, 0, 0, 0))]
        args += [conv_buf, s0]
    return pl.pallas_call(
        functools.partial(_gdn_kernel, C=C, hg=hg, nseq=nseq, nck=nck, has_state=has_state),
        grid=(B // nseq, steps),
        in_specs=in_specs,
        out_specs=[pl.BlockSpec((rows_blk, V_W), lambda b, c: (b * steps + c, 0)),
                   pl.BlockSpec((nseq, H_V, D_K, D_V), lambda b, c: (b, 0, 0, 0))],
        out_shape=[jax.ShapeDtypeStruct((B * L, V_W), BF16),
                   jax.ShapeDtypeStruct((B, H_V, D_K, D_V), F32)],
        scratch_shapes=[pltpu.VMEM((nseq * H_V, D_K, D_V), F32),
                        pltpu.VMEM((nseq * 8, QKV_W), F32),
                        pltpu.VMEM((rows_blk, QKV_W), F32)],
        compiler_params=_cparams(("parallel", "arbitrary")),
        name="gdn_state" if has_state else "gdn_fresh",
    )(*args)


def _pool_kernel(*refs, tc, nseq, start_pos, has_state):
    if has_state:
        u_ref, buf_ref, o_ref, tail_scr = refs
    else:
        u_ref, o_ref, tail_scr = refs
    hist = POOL_BUF + 1
    c = pl.program_id(1)

    @pl.when(c == 0)
    def _init():
        tail_scr[...] = jnp.zeros(tail_scr.shape, F32)
        if has_state:
            for s in range(nseq):
                tail_scr[s * hist + 1:(s + 1) * hist] = buf_ref[s]

    u_all = u_ref[...].astype(F32)
    pos1 = (start_pos + 1 + c * tc + lax.broadcasted_iota(jnp.int32, (tc, 1), 0)).astype(F32)
    seq_out = []
    for s in range(nseq):
        u = u_all[s * tc:(s + 1) * tc]
        xe = jnp.concatenate([tail_scr[s * hist:(s + 1) * hist], u], axis=0)
        tail_scr[s * hist:(s + 1) * hist] = xe[tc:tc + hist]
        parts = []
        for gi, win in enumerate(POOL_WINDOWS):
            sl = slice(gi * POOL_GROUP_W, (gi + 1) * POOL_GROUP_W)
            acc = xe[:, sl]
            shift = 1
            while shift < win:
                acc = acc + pltpu.roll(acc, shift, 0)
                shift *= 2
            cnt = jnp.minimum(float(win), pos1)
            parts.append(acc[hist:hist + tc] / cnt - u[:, sl])
        seq_out.append(jnp.concatenate(parts, axis=1))
    o_ref[...] = jnp.concatenate(seq_out, axis=0).astype(o_ref.dtype)


def _pool(proj, row0, B, L, tc, nseq, start_pos, pool_buf=None):
    has_state = pool_buf is not None
    nc = L // tc
    assert nseq == 1 or nc == 1
    rows_blk = nseq * tc
    rb0 = row0 // rows_blk
    in_specs = [pl.BlockSpec((rows_blk, D_MODEL), lambda b, c: (rb0 + b * nc + c, U_BLK))]
    args = [proj]
    if has_state:
        in_specs.append(pl.BlockSpec((nseq, POOL_BUF, D_MODEL), lambda b, c: (b, 0, 0)))
        args.append(pool_buf)
    return pl.pallas_call(
        functools.partial(_pool_kernel, tc=tc, nseq=nseq, start_pos=start_pos, has_state=has_state),
        grid=(B // nseq, nc),
        in_specs=in_specs,
        out_specs=pl.BlockSpec((rows_blk, D_MODEL), lambda b, c: (b * nc + c, 0)),
        out_shape=jax.ShapeDtypeStruct((B * L, D_MODEL), BF16),
        scratch_shapes=[pltpu.VMEM((nseq * (POOL_BUF + 1), D_MODEL), F32)],
        compiler_params=_cparams(("parallel", "arbitrary")),
        name="pool_state" if has_state else "pool_fresh",
    )(*args)


def _post_kernel(pa_ref, pb_ref, oa_ref, ob_ref, ga_ref, gb_ref, xa_ref, xb_ref, wp_ref, ps_ref, wbb_ref, wba_ref,
                 wo_ref, g_ref, b_ref, h_ref, hp_ref, *, alpha, tiles_a):
    first = pl.program_id(0) < tiles_a
    x = jnp.where(first, xa_ref[...], xb_ref[...])
    pooled = jnp.where(first, pa_ref[...], pb_ref[...])
    o_gated = jnp.where(first, oa_ref[...], ob_ref[...])
    mixed = jnp.concatenate(
        [_dot(pooled[:, gi * POOL_GROUP_W:(gi + 1) * POOL_GROUP_W], wp_ref[gi]) for gi in range(len(POOL_WINDOWS))],
        axis=1) * ps_ref[...]
    branch_b = _dot(mixed, wbb_ref[...])
    branch_a = _dot(o_gated, wba_ref[...])
    merged = _sigmoid(ga_ref[...].astype(F32)) * branch_a + _sigmoid(gb_ref[...].astype(F32)) * branch_b
    h = _layer_norm(alpha * x + _dot(merged, wo_ref[...]), g_ref[...], b_ref[...])
    h_ref[...] = h
    _store_pieces(hp_ref, h)


def _post(pooled_a, pooled_b, o_a, o_b, proj, x_a, x_b, w_pool, pool_scale, w_br_b, w_br_a, w_out, ln_g, ln_b,
          alpha, tm):
    t = x_a.shape[0] + x_b.shape[0]
    tiles_a = x_a.shape[0] // tm
    row = lambda i: (i, 0)
    row_a = lambda i: (jnp.minimum(i, tiles_a - 1), 0)
    row_b = lambda i: (jnp.maximum(i - tiles_a, 0), 0)
    const2 = lambda i: (0, 0)
    return pl.pallas_call(
        functools.partial(_post_kernel, alpha=alpha, tiles_a=tiles_a),
        grid=(t // tm,),
        in_specs=[pl.BlockSpec((tm, D_MODEL), row_a),
                  pl.BlockSpec((tm, D_MODEL), row_b),
                  pl.BlockSpec((tm, V_W), row_a),
                  pl.BlockSpec((tm, V_W), row_b),
                  pl.BlockSpec((tm, D_MODEL), lambda i: (i, U_BLK + 1)),
                  pl.BlockSpec((tm, D_MODEL), lambda i: (i, U_BLK + 2)),
                  pl.BlockSpec((tm, D_MODEL), row_a),
                  pl.BlockSpec((tm, D_MODEL), row_b),
                  pl.BlockSpec(w_pool.shape, lambda i: (0, 0, 0)),
                  pl.BlockSpec((1, D_MODEL), const2),
                  pl.BlockSpec(w_br_b.shape, const2),
                  pl.BlockSpec(w_br_a.shape, const2),
                  pl.BlockSpec(w_out.shape, const2),
                  pl.BlockSpec((1, D_MODEL), const2),
                  pl.BlockSpec((1, D_MODEL), const2)],
        out_specs=[pl.BlockSpec((tm, D_MODEL), row), pl.BlockSpec((N_PIECE, tm, PIECE_W), lambda i: (0, i, 0))],
        out_shape=[jax.ShapeDtypeStruct((t, D_MODEL), F32),
                   jax.ShapeDtypeStruct((N_PIECE, t, PIECE_W), jnp.uint32)],
        compiler_params=_cparams(("parallel",)),
        name="post_mixers",
    )(pooled_a, pooled_b, o_a, o_b, proj, proj, x_a, x_b, w_pool, pool_scale, w_br_b, w_br_a, w_out, ln_g, ln_b)


def _router_kernel(h_ref, wt_ref, bias_ref, idx_ref, wts_ref, rank_ref, cnt_ref, carry_scr, *, tm):
    i = pl.program_id(0)

    @pl.when(i == 0)
    def _init():
        carry_scr[...] = jnp.zeros(carry_scr.shape, F32)

    logits = _dot3_nt(wt_ref[...], h_ref[...])
    sc = _sigmoid(logits)
    ch = sc + bias_ref[...]
    neg = -jnp.inf
    e_in = lax.broadcasted_iota(jnp.int32, (GROUP_SZ, tm), 0)
    gs_rows = []
    for g in range(N_GROUP):
        blk = ch[g * GROUP_SZ:(g + 1) * GROUP_SZ]
        m1 = jnp.max(blk, axis=0, keepdims=True)
        i1 = jnp.min(jnp.where(blk == m1, e_in, GROUP_SZ), axis=0, keepdims=True)
        m2 = jnp.max(jnp.where(e_in == i1, neg, blk), axis=0, keepdims=True)
        gs_rows.append(m1 + m2)
    gs = jnp.concatenate(gs_rows, axis=0)
    g_io = lax.broadcasted_iota(jnp.int32, (N_GROUP, tm), 0)
    e_io = lax.broadcasted_iota(jnp.int32, (N_EXPERTS, tm), 0)
    e_grp = e_io >> int(math.log2(GROUP_SZ))
    masked = jnp.full((N_EXPERTS, tm), neg, F32)
    for _ in range(TOPK_GROUP):
        mx = jnp.max(gs, axis=0, keepdims=True)
        gi = jnp.min(jnp.where(gs == mx, g_io, N_GROUP), axis=0, keepdims=True)
        gs = jnp.where(g_io == gi, neg, gs)
        masked = jnp.where(e_grp == gi, ch, masked)
    idx_rows, w_rows = [], []
    onehot = jnp.zeros((N_EXPERTS, tm), F32)
    for _ in range(TOP_K):
        mx = jnp.max(masked, axis=0, keepdims=True)
        ei = jnp.min(jnp.where(masked == mx, e_io, N_EXPERTS), axis=0, keepdims=True)
        hit = e_io == ei
        idx_rows.append(ei)
        w_rows.append(jnp.sum(jnp.where(hit, sc, 0.0), axis=0, keepdims=True))
        onehot = jnp.where(hit, 1.0, onehot)
        masked = jnp.where(hit, neg, masked)
    wsel = jnp.concatenate(w_rows, axis=0)
    wts_ref[...] = wsel / jnp.sum(wsel, axis=0, keepdims=True) * ROUTED_SCALE
    idx_ref[...] = jnp.concatenate(idx_rows, axis=0)

    tr = lax.broadcasted_iota(jnp.int32, (tm, tm), 0)
    tc_ = lax.broadcasted_iota(jnp.int32, (tm, tm), 1)
    before = jnp.where(tr < tc_, 1.0, 0.0).astype(BF16)
    cum = jnp.dot(onehot.astype(BF16), before, preferred_element_type=F32) + carry_scr[...]
    rank_ref[...] = jnp.concatenate(
        [jnp.sum(jnp.where(e_io == ei, cum, 0.0), axis=0, keepdims=True) for ei in idx_rows],
        axis=0).astype(jnp.int32)
    carry_scr[...] = carry_scr[...] + jnp.sum(onehot, axis=1, keepdims=True)
    cnt_ref[...] = carry_scr[...]


def _router(h, w_router_t, bias_col, tm):
    t = h.shape[0]
    return pl.pallas_call(
        functools.partial(_router_kernel, tm=tm),
        grid=(t // tm,),
        in_specs=[pl.BlockSpec((tm, D_MODEL), lambda i: (i, 0)),
                  pl.BlockSpec((N_EXPERTS, D_MODEL), lambda i: (0, 0)),
                  pl.BlockSpec((N_EXPERTS, 1), lambda i: (0, 0))],
        out_specs=[pl.BlockSpec((TOP_K, tm), lambda i: (0, i)),
                   pl.BlockSpec((TOP_K, tm), lambda i: (0, i)),
                   pl.BlockSpec((TOP_K, tm), lambda i: (0, i)),
                   pl.BlockSpec((N_EXPERTS, 1), lambda i: (0, 0))],
        out_shape=[jax.ShapeDtypeStruct((TOP_K, t), jnp.int32),
                   jax.ShapeDtypeStruct((TOP_K, t), F32),
                   jax.ShapeDtypeStruct((TOP_K, t), jnp.int32),
                   jax.ShapeDtypeStruct((N_EXPERTS, 1), F32)],
        scratch_shapes=[pltpu.VMEM((N_EXPERTS, 1), F32)],
        compiler_params=_cparams(("arbitrary",)),
        name="router",
    )(h, w_router_t, bias_col)


def _dest_kernel(idx_ref, rank_ref, start_ref, dest_ref, *, tm, n_rows):
    e_io = lax.broadcasted_iota(jnp.int32, (N_EXPERTS, tm), 0)
    starts = start_ref[...]
    rows = []
    for k in range(TOP_K):
        seg = jnp.sum(jnp.where(e_io == idx_ref[k:k + 1, :], starts, 0.0), axis=0, keepdims=True)
        rows.append(seg.astype(jnp.int32) + rank_ref[k:k + 1, :])
    base = jnp.concatenate(rows, axis=0)
    for p in range(N_PIECE):
        dest_ref[p] = base + p * n_rows


def _dest(idx_t, rank_t, seg_start_col, n_rows, tm):
    t = idx_t.shape[1]
    blk = pl.BlockSpec((TOP_K, tm), lambda i: (0, i))
    return pl.pallas_call(
        functools.partial(_dest_kernel, tm=tm, n_rows=n_rows),
        grid=(t // tm,),
        in_specs=[blk, blk, pl.BlockSpec((N_EXPERTS, 1), lambda i: (0, 0))],
        out_specs=pl.BlockSpec((N_PIECE, TOP_K, tm), lambda i: (0, 0, i)),
        out_shape=jax.ShapeDtypeStruct((N_PIECE, TOP_K, t), jnp.int32),
        compiler_params=_cparams(("parallel",)),
        name="dispatch_rows",
    )(idx_t, rank_t, seg_start_col)


def _expert_kernel(be_ref, nu_ref, first_ref, nxt_ref, slot_ref, x_ref, wg_hbm, wu_hbm, wd_hbm, y_ref,
                   wg_buf, wu_buf, wd_buf, wgu_scr, wd_scr, sems):
    i = pl.program_id(0)
    live = i < nu_ref[0]

    def fetch(e, slot):
        return [pltpu.make_async_copy(wg_hbm.at[e], wg_buf.at[slot], sems.at[slot, 0]),
                pltpu.make_async_copy(wu_hbm.at[e], wu_buf.at[slot], sems.at[slot, 1]),
                pltpu.make_async_copy(wd_hbm.at[e], wd_buf.at[slot], sems.at[slot, 2])]

    @pl.when(live & (first_ref[i] == 1))
    def _new_expert():
        slot = slot_ref[i]

        @pl.when(i == 0)
        def _():
            for cp in fetch(be_ref[i], slot):
                cp.start()

        @pl.when(nxt_ref[i] >= 0)
        def _():
            for cp in fetch(nxt_ref[i], 1 - slot):
                cp.start()

        for cp in fetch(be_ref[i], slot):
            cp.wait()
        wgu_scr[:, :D_EXPERT] = wg_buf[slot].astype(BF16)
        wgu_scr[:, D_EXPERT:] = wu_buf[slot].astype(BF16)
        wd_scr[...] = wd_buf[slot].astype(BF16)

    @pl.when(live)
    def _():
        n_sub = 2
        sub = EXPERT_BM // n_sub
        xs = [jnp.concatenate([c.astype(BF16) for c in
                               _load_pieces([x_ref[p, s * sub:(s + 1) * sub] for p in range(N_PIECE)])], axis=1)
              for s in range(n_sub)]
        gus = [jnp.dot(x, wgu_scr[...], preferred_element_type=F32) for x in xs]
        acts = [(_silu(gu[:, :D_EXPERT]) * gu[:, D_EXPERT:]).astype(BF16) for gu in gus]
        ys = [jnp.dot(a, wd_scr[...], preferred_element_type=F32) for a in acts]
        for s, y in enumerate(ys):
            for p in range(N_PIECE):
                y_ref[p, s * sub:(s + 1) * sub] = _pack_halves(y[:, 2 * p * PIECE_W:2 * (p + 1) * PIECE_W])


def _experts(xs, blk_exp, n_used, blk_first, blk_next, blk_slot, w_gate, w_up, w_down):
    n_rows = xs.shape[1]
    n_blocks = n_rows // EXPERT_BM

    def row_block(i, be, nu, *_):
        return (0, jnp.minimum(i, nu[0] - 1), 0)

    grid_spec = pltpu.PrefetchScalarGridSpec(
        num_scalar_prefetch=5,
        grid=(n_blocks,),
        in_specs=[pl.BlockSpec((N_PIECE, EXPERT_BM, PIECE_W), row_block),
                  pl.BlockSpec(memory_space=pl.ANY),
                  pl.BlockSpec(memory_space=pl.ANY),
                  pl.BlockSpec(memory_space=pl.ANY)],
        out_specs=pl.BlockSpec((N_PIECE, EXPERT_BM, PIECE_W), row_block),
        scratch_shapes=[pltpu.VMEM((2, D_MODEL, D_EXPERT), F32),
                        pltpu.VMEM((2, D_MODEL, D_EXPERT), F32),
                        pltpu.VMEM((2, D_EXPERT, D_MODEL), F32),
                        pltpu.VMEM((D_MODEL, 2 * D_EXPERT), BF16),
                        pltpu.VMEM((D_EXPERT, D_MODEL), BF16),
                        pltpu.SemaphoreType.DMA((2, 3))],
    )
    return pl.pallas_call(
        _expert_kernel,
        grid_spec=grid_spec,
        out_shape=jax.ShapeDtypeStruct((N_PIECE, n_rows, PIECE_W), jnp.uint32),
        compiler_params=_cparams(("arbitrary",)),
        name="experts",
    )(blk_exp, n_used, blk_first, blk_next, blk_slot, xs, w_gate, w_up, w_down)


SC_WINDOW = 128
V7X_SC_CORES = 2
V7X_SC_SUBCORES = 16


def _sc_mesh():
    return plsc.VectorSubcoreMesh(core_axis_name="core", subcore_axis_name="subcore",
                                  num_cores=V7X_SC_CORES, num_subcores=V7X_SC_SUBCORES)


def _sc_scatter_rows(src, dest, n_rows, seg, repeat):
    d = src.shape[1]
    n_idx = dest.shape[0]
    seg_blocks = seg // SC_WINDOW
    dest2 = dest.reshape(1, n_idx)

    def src_block(i):
        return ((i // (repeat * seg_blocks)) * seg_blocks + i % seg_blocks, 0)

    @functools.partial(pl.kernel, out_type=jax.ShapeDtypeStruct((n_rows, d), src.dtype), mesh=_sc_mesh(),
                       scratch_types=[], name="sc_dispatch")
    def run(src_hbm, idx_hbm, out_hbm):
        def body(rows_vmem, idx_vmem):
            pltpu.sync_copy(rows_vmem, out_hbm.at[idx_vmem.at[0]])

        pltpu.emit_pipeline(
            body,
            grid=(n_idx // SC_WINDOW,),
            in_specs=[pl.BlockSpec((SC_WINDOW, d), src_block),
                      pl.BlockSpec((1, SC_WINDOW), lambda i: (0, i))],
            out_specs=[],
            core_axis_name=("core", "subcore"),
            dimension_semantics=(pltpu.PARALLEL,),
        )(src_hbm, idx_hbm)

    return run(src, dest2)


def _sc_gather_rows(table, idx):
    d = table.shape[1]
    n_idx = idx.shape[0]
    idx2 = idx.reshape(1, n_idx)

    @functools.partial(pl.kernel, out_type=jax.ShapeDtypeStruct((n_idx, d), table.dtype), mesh=_sc_mesh(),
                       scratch_types=[], name="sc_combine_gather")
    def run(table_hbm, idx_hbm, out_hbm):
        def body(idx_vmem, rows_vmem):
            pltpu.sync_copy(table_hbm.at[idx_vmem.at[0]], rows_vmem)

        pltpu.emit_pipeline(
            body,
            grid=(n_idx // SC_WINDOW,),
            in_specs=[pl.BlockSpec((1, SC_WINDOW), lambda i: (0, i))],
            out_specs=[pl.BlockSpec((SC_WINDOW, d), lambda i: (i, 0))],
            core_axis_name=("core", "subcore"),
            dimension_semantics=(pltpu.PARALLEL,),
        )(idx_hbm, out_hbm)

    return run(table, idx2)


def _combine_kernel(yg_ref, wts_ref, h_ref, wgu_ref, wd_ref, g_ref, b_ref, outa_ref, outb_ref, *, alpha, tiles_a):
    wts = wts_ref[...]
    acc = None
    for k in range(TOP_K):
        cols = _load_pieces([yg_ref[p, k] for p in range(N_PIECE)])
        wk = wts[:, k:k + 1]
        acc = [c * wk for c in cols] if acc is None else [a + c * wk for a, c in zip(acc, cols)]
    routed = jnp.concatenate(acc, axis=1)
    h = h_ref[...]
    gu = _dot(h, wgu_ref[...])
    shared = _dot(_silu(gu[:, :D_EXPERT]) * gu[:, D_EXPERT:], wd_ref[...])
    y = _layer_norm(alpha * h + (routed + shared), g_ref[...], b_ref[...])
    i = pl.program_id(0)

    @pl.when(i < tiles_a)
    def _():
        outa_ref[...] = y

    @pl.when(i >= tiles_a)
    def _():
        outb_ref[...] = y


def _combine(yg, wts, h, w_sh_gu, w_sh_down, ln_g, ln_b, alpha, tm, t_a):
    t = h.shape[0]
    tiles_a = t_a // tm
    row = lambda i: (i, 0)
    const2 = lambda i: (0, 0)
    return pl.pallas_call(
        functools.partial(_combine_kernel, alpha=alpha, tiles_a=tiles_a),
        grid=(t // tm,),
        in_specs=[pl.BlockSpec((N_PIECE, TOP_K, tm, PIECE_W), lambda i: (0, 0, i, 0)),
                  pl.BlockSpec((tm, TOP_K), row),
                  pl.BlockSpec((tm, D_MODEL), row),
                  pl.BlockSpec(w_sh_gu.shape, const2),
                  pl.BlockSpec(w_sh_down.shape, const2),
                  pl.BlockSpec((1, D_MODEL), const2),
                  pl.BlockSpec((1, D_MODEL), const2)],
        out_specs=[pl.BlockSpec((tm, D_MODEL), lambda i: (jnp.minimum(i, tiles_a - 1), 0)),
                   pl.BlockSpec((tm, D_MODEL), lambda i: (jnp.maximum(i - tiles_a, 0), 0))],
        out_shape=[jax.ShapeDtypeStruct((t_a, D_MODEL), F32), jax.ShapeDtypeStruct((t - t_a, D_MODEL), F32)],
        compiler_params=_cparams(("arbitrary",)),
        name="combine_ln2",
    )(yg, wts, h, w_sh_gu, w_sh_down, ln_g, ln_b)


def _layer(xp, xs, s_delta, s_conv, s_pool, w_in, conv_w, a_log, dt_bias, gamma_a, w_br_a, w_pool,
           pool_scale, w_br_b, w_out, ln1_g, ln1_b, w_router, router_bias, w_exp_gate, w_exp_up,
           w_exp_down, w_sh_gate, w_sh_up, w_sh_down, ln2_g, ln2_b, alpha):
    Bp, Lp, _ = xp.shape
    Bs, Ls, _ = xs.shape
    Tp, Ts = Bp * Lp, Bs * Ls
    T = Tp + Ts
    x_p = xp.reshape(Tp, D_MODEL)
    x_s = xs.reshape(Ts, D_MODEL)

    o_b, o_u = QKV_W + V_W, QKV_W + V_W + 2 * H_V
    w_main = jnp.concatenate([w_in[:, :o_b], w_in[:, o_u:]], axis=1).astype(BF16)
    w_ba = jnp.pad(w_in[:, o_b:o_u], ((0, 0), (0, 128 - 2 * H_V))).astype(BF16)
    proj, ba = _in_proj(x_p, x_s, w_main, w_ba, IN_PROJ_TM, IN_PROJ_TN)

    o_p, sd_p = _gdn(proj, ba, conv_w, a_log, dt_bias, gamma_a, 0, Bp, Lp, GDN_CHUNK, 1, GDN_FRESH_CHUNKS)
    o_s, sd_s = _gdn(proj, ba, conv_w, a_log, dt_bias, gamma_a, Tp, Bs, Ls, Ls, GDN_STATE_SEQS, 1,
                     conv_buf=s_conv, s0=s_delta)

    pooled_p = _pool(proj, 0, Bp, Lp, POOL_TILE, 1, 0)
    pooled_s = _pool(proj, Tp, Bs, Ls, Ls, POOL_STATE_SEQS, PAST_LEN, pool_buf=s_pool)

    h, hp = _post(pooled_p, pooled_s, o_p, o_s, proj, x_p, x_s, w_pool.astype(BF16), pool_scale.reshape(1, D_MODEL),
                  w_br_b.astype(BF16), w_br_a.astype(BF16), w_out.astype(BF16),
                  ln1_g.reshape(1, D_MODEL), ln1_b.reshape(1, D_MODEL), alpha, POST_TM)

    idx_t, wts_t, rank_t, cnt = _router(h, w_router.T, router_bias.reshape(N_EXPERTS, 1), ROUTER_TM)
    counts = cnt[:, 0].astype(jnp.int32)
    padded = ((counts + EXPERT_BM - 1) // EXPERT_BM) * EXPERT_BM
    pends = jnp.cumsum(padded)
    pstarts = pends - padded
    n_blocks = (T * TOP_K + N_EXPERTS * (EXPERT_BM - 1) + EXPERT_BM - 1) // EXPERT_BM
    n_rows = n_blocks * EXPERT_BM
    piece_rows = _dest(idx_t, rank_t, pstarts.astype(F32).reshape(N_EXPERTS, 1), n_rows, ROUTER_TM)
    blk_start = jnp.arange(n_blocks, dtype=jnp.int32) * EXPERT_BM
    blk_exp = jnp.minimum(jnp.sum((pends[None, :] <= blk_start[:, None]).astype(jnp.int32), axis=1),
                          N_EXPERTS - 1)
    n_used = (pends[-1:] // EXPERT_BM).astype(jnp.int32)
    has_rows = counts > 0
    e_ids = jnp.arange(N_EXPERTS, dtype=jnp.int32)
    slot_of = (jnp.cumsum(has_rows.astype(jnp.int32)) - 1) & 1
    later = jnp.where(has_rows, e_ids, N_EXPERTS)
    next_of = jnp.concatenate([lax.cummin(later, reverse=True)[1:], jnp.full((1,), N_EXPERTS, jnp.int32)])
    next_of = jnp.where(next_of < N_EXPERTS, next_of, -1)
    blk_first = (blk_start == pstarts[blk_exp]).astype(jnp.int32)
    blk_next = next_of[blk_exp]
    blk_slot = slot_of[blk_exp]

    piece_idx = piece_rows.reshape(N_PIECE * TOP_K * T)
    x_sorted = _sc_scatter_rows(hp.reshape(N_PIECE * T, PIECE_W), piece_idx, N_PIECE * n_rows, T, TOP_K)
    y_sorted = _experts(x_sorted.reshape(N_PIECE, n_rows, PIECE_W), blk_exp, n_used, blk_first, blk_next,
                        blk_slot, w_exp_gate, w_exp_up, w_exp_down)
    yg = _sc_gather_rows(y_sorted.reshape(N_PIECE * n_rows, PIECE_W), piece_idx)
    yg = yg.reshape(N_PIECE, TOP_K, T, PIECE_W)

    w_sh_gu = jnp.concatenate([w_sh_gate, w_sh_up], axis=1).astype(BF16)
    y_p, y_s = _combine(yg, wts_t.T, h, w_sh_gu, w_sh_down.astype(BF16),
                        ln2_g.reshape(1, D_MODEL), ln2_b.reshape(1, D_MODEL), alpha, COMBINE_TM, Tp)

    def carried_rows(keep, col0, width, name, old_p, old_s):
        parts, counts = [], []
        for x2d, B, L in ((x_p, Bp, Lp), (x_s, Bs, Ls)):
            n = min(L, keep)
            rows = jnp.arange(B, dtype=jnp.int32)[:, None] * L + (L - n) + jnp.arange(n, dtype=jnp.int32)
            parts.append(jnp.take(x2d, rows.reshape(-1), axis=0).astype(BF16))
            counts.append((B, n))
        n_tail = sum(B * n for B, n in counts)
        n_pad = -(-n_tail // 128) * 128
        xt = jnp.pad(jnp.concatenate(parts, axis=0), ((0, n_pad - n_tail), (0, 0)))
        tail = _matmul(xt, w_main, col0, width, n_pad, 1024, F32, name)
        outs, r0 = [], 0
        for (B, n), old in zip(counts, (old_p, old_s)):
            new = tail[r0:r0 + B * n].reshape(B, n, width)
            outs.append(new if n == keep else jnp.concatenate([old[:, n:], new], axis=1))
            r0 += B * n
        return outs

    new_conv_p, new_conv_s = carried_rows(CONV_W - 1, 0, QKV_W, "in_proj_conv_rows",
                                          jnp.zeros((Bp, CONV_W - 1, QKV_W), F32), s_conv)
    new_pool_p, new_pool_s = carried_rows(POOL_BUF, U_BLK * D_MODEL, D_MODEL, "in_proj_pool_rows",
                                          jnp.zeros((Bp, POOL_BUF, D_MODEL), F32), s_pool)
    return (y_p.reshape(Bp, Lp, D_MODEL), y_s.reshape(Bs, Ls, D_MODEL),
            sd_p, new_conv_p, new_pool_p, sd_s, new_conv_s, new_pool_s)


def kernel(x_prompt, x_sample, state_delta, state_conv, state_pool, w_in, conv_w, a_log, dt_bias, gamma_a,
           w_br_a, w_pool, pool_scale, w_br_b, w_out, ln1_g, ln1_b, w_router, router_bias,
           w_exp_gate, w_exp_up, w_exp_down, w_sh_gate, w_sh_up, w_sh_down, ln2_g, ln2_b):
    depth = w_in.shape[0]
    alpha = (2 * depth) ** 0.25
    yp, ys = x_prompt, x_sample
    outs = [[] for _ in range(6)]
    for l in range(depth):
        res = _layer(yp, ys, state_delta[l], state_conv[l], state_pool[l], w_in[l], conv_w[l], a_log[l],
                     dt_bias[l], gamma_a[l], w_br_a[l], w_pool[l], pool_scale[l], w_br_b[l], w_out[l],
                     ln1_g[l], ln1_b[l], w_router[l], router_bias[l], w_exp_gate[l], w_exp_up[l],
                     w_exp_down[l], w_sh_gate[l], w_sh_up[l], w_sh_down[l], ln2_g[l], ln2_b[l], alpha)
        yp, ys = res[0], res[1]
        for lst, v in zip(outs, res[2:]):
            lst.append(v)
    return (yp, ys) + tuple(jnp.stack(v) for v in outs)
```

```python
import functools
import math

import jax
import jax.numpy as jnp
from jax import lax
from jax.experimental import pallas as pl
from jax.experimental.pallas import tpu as pltpu
from jax.experimental.pallas import tpu_sc as plsc

F32 = jnp.float32
BF16 = jnp.bfloat16

D_MODEL = 1024
H_K = 8
D_K = 128
H_V = 16
D_V = 128
Q_W = H_K * D_K
V_W = H_V * D_V
QKV_W = 2 * Q_W + V_W
CONV_W = 4
POOL_WINDOWS = (2, 4, 8, 16)
POOL_GROUP_W = D_MODEL // len(POOL_WINDOWS)
POOL_BUF = max(POOL_WINDOWS) - 1
N_EXPERTS = 256
TOP_K = 8
N_GROUP = 8
TOPK_GROUP = 4
GROUP_SZ = N_EXPERTS // N_GROUP
D_EXPERT = D_MODEL // 4
ROUTED_SCALE = 2.5
LN_EPS = 1e-5
NORM_EPS = 1e-6
PAST_LEN = 16384

MAIN_W = QKV_W + V_W + 3 * D_MODEL
Z_BLK = QKV_W // V_W
U_BLK = (QKV_W + V_W) // D_MODEL

IN_PROJ_TM = 1024
IN_PROJ_TN = MAIN_W // 3
GDN_CHUNK = 64
POOL_TILE = 256
POST_TM = 256
ROUTER_TM = 512
COMBINE_TM = 512

STACK_ROWS = 128
SOLVE_BASE = 16
GDN_FRESH_CHUNKS = 4
EXPERT_BM = 512
GDN_STATE_SEQS = 8
POOL_STATE_SEQS = 16
VMEM_LIMIT = 56 * 1024 * 1024


def _cparams(sem):
    return pltpu.CompilerParams(dimension_semantics=sem, vmem_limit_bytes=VMEM_LIMIT)


def _sigmoid(x):
    return 0.5 * jnp.tanh(0.5 * x) + 0.5


def _silu(x):
    return x * _sigmoid(x)


def _softplus(x):
    return jnp.maximum(x, 0.0) + jnp.log(1.0 + jnp.exp(-jnp.abs(x)))


def _dot(a, b):
    return jnp.dot(a.astype(BF16), b.astype(BF16), preferred_element_type=F32)


def _dot_nt(a, b):
    return lax.dot_general(a.astype(BF16), b.astype(BF16), (((1,), (1,)), ((), ())),
                           preferred_element_type=F32)


def _dot_tn(a, b):
    return lax.dot_general(a.astype(BF16), b.astype(BF16), (((0,), (0,)), ((), ())),
                           preferred_element_type=F32)


def _split(a):
    hi = a.astype(BF16)
    lo = (a - hi.astype(F32)).astype(BF16)
    return hi, lo


def _dot3_nt(a, b):
    ah, al = _split(a)
    bh, bl = _split(b)
    d = functools.partial(lax.dot_general, dimension_numbers=(((1,), (1,)), ((), ())),
                          preferred_element_type=F32)
    return d(ah, bh) + (d(ah, bl) + d(al, bh))


def _pack_halves(x):
    n = x.shape[1] // 2
    hi = lax.bitcast_convert_type(x[:, :n].astype(BF16).astype(F32), jnp.uint32)
    lo = lax.bitcast_convert_type(x[:, n:].astype(BF16).astype(F32), jnp.uint32)
    return (hi & jnp.uint32(0xFFFF0000)) | (lo >> 16)


def _unpack_halves(w):
    hi = lax.bitcast_convert_type(w & jnp.uint32(0xFFFF0000), F32)
    lo = lax.bitcast_convert_type(w << 16, F32)
    return hi, lo


N_PIECE = 2
PIECE_W = D_MODEL // (2 * N_PIECE)


def _store_pieces(ref, x):
    for p in range(N_PIECE):
        ref[p] = _pack_halves(x[:, 2 * p * PIECE_W:2 * (p + 1) * PIECE_W])


def _load_pieces(pieces):
    cols = []
    for w in pieces:
        cols.extend(_unpack_halves(w))
    return cols


def _layer_norm(x, g, b):
    mu = jnp.mean(x, axis=-1, keepdims=True)
    xc = x - mu
    var = jnp.mean(xc * xc, axis=-1, keepdims=True)
    return xc * lax.rsqrt(var + LN_EPS) * g + b


def _in_proj_kernel(xa_ref, xb_ref, w_ref, wba_ref, o_ref, ba_ref, x_scr, *, tiles_a):
    @pl.when(pl.program_id(1) == 0)
    def _():
        x = jnp.where(pl.program_id(0) < tiles_a, xa_ref[...], xb_ref[...]).astype(BF16)
        x_scr[...] = x
        ba_ref[...] = jnp.dot(x, wba_ref[...], preferred_element_type=F32)

    o_ref[...] = jnp.dot(x_scr[...], w_ref[...], preferred_element_type=F32).astype(o_ref.dtype)


def _in_proj(x_a, x_b, w_main, w_ba, tm, tn):
    k = x_a.shape[1]
    t = x_a.shape[0] + x_b.shape[0]
    n = w_main.shape[1]
    tiles_a = x_a.shape[0] // tm
    return pl.pallas_call(
        functools.partial(_in_proj_kernel, tiles_a=tiles_a),
        grid=(t // tm, n // tn),
        in_specs=[pl.BlockSpec((tm, k), lambda i, j: (jnp.minimum(i, tiles_a - 1), 0)),
                  pl.BlockSpec((tm, k), lambda i, j: (jnp.maximum(i - tiles_a, 0), 0)),
                  pl.BlockSpec((k, tn), lambda i, j: (0, j)),
                  pl.BlockSpec(w_ba.shape, lambda i, j: (0, 0))],
        out_specs=[pl.BlockSpec((tm, tn), lambda i, j: (i, j)),
                   pl.BlockSpec((tm, w_ba.shape[1]), lambda i, j: (i, 0))],
        out_shape=[jax.ShapeDtypeStruct((t, n), BF16), jax.ShapeDtypeStruct((t, w_ba.shape[1]), F32)],
        scratch_shapes=[pltpu.VMEM((tm, k), BF16)],
        compiler_params=_cparams(("parallel", "arbitrary")),
        name="in_proj",
    )(x_a, x_b, w_main, w_ba)


CARRY_SEQS = 16


def _carried_kernel(*refs, bs, n_new, keep, has_old):
    if has_old:
        x_ref, w_ref, old_ref, o_ref = refs
    else:
        x_ref, w_ref, o_ref = refs
    new = jnp.dot(x_ref[...], w_ref[...], preferred_element_type=F32)
    for s in range(bs):
        if has_old:
            o_ref[s, 0:keep - n_new] = old_ref[s, n_new:keep]
        o_ref[s, keep - n_new:keep] = new[s * n_new:(s + 1) * n_new]


def _carried(xt, w, col0, width, B, n_new, keep, old, name):
    k = xt.shape[1]
    tn = 1024
    cb0 = col0 // tn
    bs = CARRY_SEQS if B % CARRY_SEQS == 0 else B
    has_old = old is not None
    in_specs = [pl.BlockSpec((bs * n_new, k), lambda i, j: (i, 0)),
                pl.BlockSpec((k, tn), lambda i, j: (0, cb0 + j))]
    args = [xt, w]
    if has_old:
        in_specs.append(pl.BlockSpec((bs, keep, tn), lambda i, j: (i, 0, j)))
        args.append(old)
    return pl.pallas_call(
        functools.partial(_carried_kernel, bs=bs, n_new=n_new, keep=keep, has_old=has_old),
        grid=(B // bs, width // tn),
        in_specs=in_specs,
        out_specs=pl.BlockSpec((bs, keep, tn), lambda i, j: (i, 0, j)),
        out_shape=jax.ShapeDtypeStruct((B, keep, width), F32),
        compiler_params=_cparams(("parallel", "parallel")),
        name=name,
    )(*args)


def _gdn_kernel(*refs, C, hg, nseq, nck, has_state):
    if has_state:
        (qkv_ref, z_ref, ba_ref, cw_ref, ad_ref, gam_ref, buf_ref, s0_ref,
         o_ref, sout_ref, s_scr, tail_scr, act_scr) = refs
    else:
        (qkv_ref, z_ref, ba_ref, cw_ref, ad_ref, gam_ref,
         o_ref, sout_ref, s_scr, tail_scr, act_scr) = refs
    R = hg * C
    ng = H_V // hg
    log2c = int(math.log2(C))
    c = pl.program_id(1)

    @pl.when(c == 0)
    def _init():
        if has_state:
            tail_scr[...] = jnp.zeros(tail_scr.shape, F32)
            for s in range(nseq):
                s_scr[s * H_V:(s + 1) * H_V] = s0_ref[s]
                tail_scr[(s + 1) * 8 - (CONV_W - 1):(s + 1) * 8] = buf_ref[s]
        else:
            s_scr[...] = jnp.zeros(s_scr.shape, F32)
            tail_scr[...] = jnp.zeros(tail_scr.shape, F32)

    cw = cw_ref[...]
    qkv_all = qkv_ref[...].astype(F32)
    n_prev = CONV_W - 1
    CB = nck * C
    for s in range(nseq):
        x = qkv_all[s * CB:(s + 1) * CB]
        xe = jnp.concatenate([tail_scr[s * 8:(s + 1) * 8], x], axis=0)
        y = x * cw[n_prev:CONV_W]
        for j in range(n_prev):
            y = y + xe[8 - n_prev + j:8 - n_prev + j + CB] * cw[j:j + 1]
        tail_scr[s * 8:(s + 1) * 8] = x[CB - 8:CB]
        act_scr[s * CB:(s + 1) * CB] = _silu(y)
        for off, scale in ((0, D_K ** -0.5), (Q_W, 1.0)):
            for kh in range(H_K):
                cols = slice(off + kh * D_K, off + (kh + 1) * D_K)
                v = act_scr[s * CB:(s + 1) * CB, cols]
                act_scr[s * CB:(s + 1) * CB, cols] = v * (
                    lax.rsqrt(jnp.sum(v * v, axis=-1, keepdims=True) + NORM_EPS) * scale)

    ri = lax.broadcasted_iota(jnp.int32, (R, R), 0)
    ci = lax.broadcasted_iota(jnp.int32, (R, R), 1)
    same = (ri >> log2c) == (ci >> log2c)
    incl = same & (ri >= ci)
    strict = same & (ri > ci)
    eye = ri == ci
    is_last = ci == (((ri >> log2c) << log2c) + (C - 1))
    eye_f = jnp.where(eye, 1.0, 0.0).astype(F32)
    cum_u = jnp.where(same & (ri <= ci), 1.0, 0.0).astype(BF16)

    nrow = ba_ref.shape[2] // 2
    adv = ad_ref[...]
    dd = functools.partial(jnp.dot, preferred_element_type=F32)
    beta_of, G_of = {}, {}
    for s in range(nseq):
        for k in range(nck):
            bav = ba_ref[s, k]
            beta_of[s, k] = _sigmoid(bav[0:nrow])
            g_rows = -jnp.exp(adv[0:nrow]) * _softplus(bav[nrow:2 * nrow] + adv[nrow:2 * nrow])
            g1 = g_rows.astype(BF16)
            r1 = g_rows - g1.astype(F32)
            g2 = r1.astype(BF16)
            g3 = (r1 - g2.astype(F32)).astype(BF16)
            G_of[s, k] = dd(g1, cum_u) + (dd(g2, cum_u) + dd(g3, cum_u))

    gam = gam_ref[...]
    rep = H_V // H_K

    def stacked(d, off, per_k_head):
        blocks = [(h // rep if per_k_head else h) for h in d["heads"]]
        return jnp.concatenate([act_scr[d["rows"], off + b * D_K:off + (b + 1) * D_K] for b in blocks], axis=0)

    log2b = min(log2c, int(math.log2(SOLVE_BASE)))
    same_base = ((ri ^ ci) >> log2b) == 0
    row_head = lax.broadcasted_iota(jnp.int32, (R, D_V), 0) >> log2c
    z_all = z_ref[...].astype(F32)
    grp = [dict(s=s, k=k, g=g, rows=slice(s * CB + k * C, s * CB + (k + 1) * C), s0=s * H_V,
                heads=[g * hg + hh for hh in range(hg)])
           for k in range(nck) for s in range(nseq) for g in range(ng)]
    for d in grp:
        kst = stacked(d, Q_W, True)
        d["kk"] = _dot_nt(kst, kst)
        d["qk"] = _dot_nt(stacked(d, 0, True), kst)
    for d in grp:
        grow = G_of[d["s"], d["k"]][d["g"]:d["g"] + 1]
        brow = beta_of[d["s"], d["k"]][d["g"]:d["g"] + 1]
        gcol = jnp.sum(eye_f * grow, axis=1, keepdims=True)
        bcol = jnp.sum(eye_f * brow, axis=1, keepdims=True)
        glast = jnp.sum(jnp.where(is_last, grow, 0.0), axis=1, keepdims=True)
        dm = jnp.where(incl, jnp.exp(jnp.minimum(gcol - grow, 0.0)), 0.0)
        d.update(gcol=gcol, bcol=bcol, glast=glast, eg=jnp.exp(gcol))
        d["m"] = jnp.where(strict, -(d["kk"] * bcol * dm), 0.0)
        d["attn"] = jnp.where(incl, d["qk"] * dm, 0.0)
        d["pw"] = jnp.where(same_base, d["m"], 0.0)
        d["t"] = eye_f + d["pw"]

    for r in range(log2b):
        last = r == log2b - 1
        for d in grp:
            if r == 0:
                if not last:
                    d["pw"] = _dot(d["pw"], d["pw"])
            elif last:
                d["t"] = d["t"] + _dot(d["pw"], d["t"])
            else:
                out = _dot(d["pw"], jnp.concatenate([d["pw"], d["t"]], axis=1))
                d["pw"] = out[:, :R]
                d["t"] = d["t"] + out[:, R:]
    for lev in range(log2b + 1, log2c + 1):
        coupling = ((ri ^ ci) >> (lev - 1)) == 1
        for d in grp:
            d["y"] = _dot(d["t"], jnp.where(coupling, d["m"], 0.0))
        for d in grp:
            d["t"] = d["t"] + _dot(d["y"], d["t"])
    for d in grp:
        rhs = jnp.concatenate([stacked(d, 2 * Q_W, False) * d["bcol"],
                               stacked(d, Q_W, True) * (d["bcol"] * d["eg"])], axis=1)
        d["x"] = _dot(d["t"], rhs)

    for k in range(nck):
        now = [d for d in grp if d["k"] == k]
        for d in now:
            qe = stacked(d, 0, True) * d["eg"]
            d["xs"] = [_dot(jnp.concatenate([d["x"][hh * C:(hh + 1) * C, D_V:], qe[hh * C:(hh + 1) * C]], axis=0),
                            s_scr[d["s0"] + h]) for hh, h in enumerate(d["heads"])]
        for d in now:
            d["vnew"] = jnp.concatenate(
                [d["x"][hh * C:(hh + 1) * C, :D_V] - xs[:C] for hh, xs in enumerate(d["xs"])], axis=0)
            d["o"] = jnp.concatenate([xs[C:] for xs in d["xs"]], axis=0) + _dot(d["attn"], d["vnew"])
        for d in now:
            kt = stacked(d, Q_W, True) * jnp.exp(d["glast"] - d["gcol"])
            egl = jnp.exp(d["glast"])
            for hh, h in enumerate(d["heads"]):
                sl = slice(hh * C, (hh + 1) * C)
                if C >= 16:
                    upd = _dot_tn(kt[sl], d["vnew"][sl])
                else:
                    upd = _dot_tn(kt, jnp.where(row_head == hh, d["vnew"], 0.0))
                s_scr[d["s0"] + h] = s_scr[d["s0"] + h] * egl[hh * C:hh * C + 1] + upd

    for d in grp:
        o = d["o"]
        zst = jnp.concatenate([z_all[d["rows"], h * D_V:(h + 1) * D_V] for h in d["heads"]], axis=0)
        on = o * lax.rsqrt(jnp.mean(o * o, axis=-1, keepdims=True) + NORM_EPS) * gam * _silu(zst)
        for hh, h in enumerate(d["heads"]):
            o_ref[d["rows"], h * D_V:(h + 1) * D_V] = on[hh * C:(hh + 1) * C].astype(o_ref.dtype)

    @pl.when(c == pl.num_programs(1) - 1)
    def _fin():
        for s in range(nseq):
            sout_ref[s] = s_scr[s * H_V:(s + 1) * H_V]


def _gdn(proj, ba, conv_w, a_log, dt_bias, gamma_a, row0, B, L, C, nseq, nck, conv_buf=None, s0=None):
    has_state = s0 is not None
    hg = STACK_ROWS // C
    ng = H_V // hg
    nrow = 16
    nc = L // C
    R = STACK_ROWS
    assert nseq == 1 or nc == nck
    rows_blk = nseq * nck * C
    steps = nc // nck

    def arrange(v):
        v = v.reshape(B, nc, C, ng, hg).transpose(0, 1, 3, 4, 2).reshape(B, nc, ng, R)
        return jnp.pad(v, ((0, 0), (0, 0), (0, nrow - ng), (0, 0)))

    rows = ba[row0:row0 + B * L]
    ba_arr = jnp.concatenate([arrange(rows[:, :H_V]), arrange(rows[:, H_V:2 * H_V])], axis=2)

    def arrange_param(p):
        v = jnp.broadcast_to(p.reshape(ng, hg, 1), (ng, hg, C)).reshape(ng, R)
        return jnp.pad(v, ((0, nrow - ng), (0, 0)))

    ad = jnp.concatenate([arrange_param(a_log), arrange_param(dt_bias)], axis=0)
    rb0 = row0 // rows_blk
    in_specs = [
        pl.BlockSpec((rows_blk, QKV_W), lambda b, c: (rb0 + b * steps + c, 0)),
        pl.BlockSpec((rows_blk, V_W), lambda b, c: (rb0 + b * steps + c, Z_BLK)),
        pl.BlockSpec((nseq, nck, 2 * nrow, R), lambda b, c: (b, c, 0, 0)),
        pl.BlockSpec((CONV_W, QKV_W), lambda b, c: (0, 0)),
        pl.BlockSpec((2 * nrow, R), lambda b, c: (0, 0)),
        pl.BlockSpec((1, D_V), lambda b, c: (0, 0)),
    ]
    args = [proj, proj, ba_arr, conv_w, ad, gamma_a.reshape(1, D_V)]
    if has_state:
        in_specs += [pl.BlockSpec((nseq, CONV_W - 1, QKV_W), lambda b, c: (b, 0, 0)),
                     pl.BlockSpec((nseq, H_V, D_K, D_V), lambda b, c: (b, 0, 0, 0))]
        args += [conv_buf, s0]
    return pl.pallas_call(
        functools.partial(_gdn_kernel, C=C, hg=hg, nseq=nseq, nck=nck, has_state=has_state),
        grid=(B // nseq, steps),
        in_specs=in_specs,
        out_specs=[pl.BlockSpec((rows_blk, V_W), lambda b, c: (b * steps + c, 0)),
                   pl.BlockSpec((nseq, H_V, D_K, D_V), lambda b, c: (b, 0, 0, 0))],
        out_shape=[jax.ShapeDtypeStruct((B * L, V_W), BF16),
                   jax.ShapeDtypeStruct((B, H_V, D_K, D_V), F32)],
        scratch_shapes=[pltpu.VMEM((nseq * H_V, D_K, D_V), F32),
                        pltpu.VMEM((nseq * 8, QKV_W), F32),
                        pltpu.VMEM((rows_blk, QKV_W), F32)],
        compiler_params=_cparams(("parallel", "arbitrary")),
        name="gdn_state" if has_state else "gdn_fresh",
    )(*args)


def _pool_kernel(*refs, tc, nseq, start_pos, has_state):
    if has_state:
        u_ref, buf_ref, o_ref, tail_scr = refs
    else:
        u_ref, o_ref, tail_scr = refs
    hist = POOL_BUF + 1
    c = pl.program_id(1)

    @pl.when(c == 0)
    def _init():
        tail_scr[...] = jnp.zeros(tail_scr.shape, F32)
        if has_state:
            for s in range(nseq):
                tail_scr[s * hist + 1:(s + 1) * hist] = buf_ref[s]

    u_all = u_ref[...].astype(F32)
    pos1 = (start_pos + 1 + c * tc + lax.broadcasted_iota(jnp.int32, (tc, 1), 0)).astype(F32)
    seq_out = []
    for s in range(nseq):
        u = u_all[s * tc:(s + 1) * tc]
        xe = jnp.concatenate([tail_scr[s * hist:(s + 1) * hist], u], axis=0)
        tail_scr[s * hist:(s + 1) * hist] = xe[tc:tc + hist]
        parts = []
        for gi, win in enumerate(POOL_WINDOWS):
            sl = slice(gi * POOL_GROUP_W, (gi + 1) * POOL_GROUP_W)
            acc = xe[:, sl]
            shift = 1
            while shift < win:
                acc = acc + pltpu.roll(acc, shift, 0)
                shift *= 2
            cnt = jnp.minimum(float(win), pos1)
            parts.append(acc[hist:hist + tc] / cnt - u[:, sl])
        seq_out.append(jnp.concatenate(parts, axis=1))
    o_ref[...] = jnp.concatenate(seq_out, axis=0).astype(o_ref.dtype)


def _pool(proj, row0, B, L, tc, nseq, start_pos, pool_buf=None):
    has_state = pool_buf is not None
    nc = L // tc
    assert nseq == 1 or nc == 1
    rows_blk = nseq * tc
    rb0 = row0 // rows_blk
    in_specs = [pl.BlockSpec((rows_blk, D_MODEL), lambda b, c: (rb0 + b * nc + c, U_BLK))]
    args = [proj]
    if has_state:
        in_specs.append(pl.BlockSpec((nseq, POOL_BUF, D_MODEL), lambda b, c: (b, 0, 0)))
        args.append(pool_buf)
    return pl.pallas_call(
        functools.partial(_pool_kernel, tc=tc, nseq=nseq, start_pos=start_pos, has_state=has_state),
        grid=(B // nseq, nc),
        in_specs=in_specs,
        out_specs=pl.BlockSpec((rows_blk, D_MODEL), lambda b, c: (b * nc + c, 0)),
        out_shape=jax.ShapeDtypeStruct((B * L, D_MODEL), BF16),
        scratch_shapes=[pltpu.VMEM((nseq * (POOL_BUF + 1), D_MODEL), F32)],
        compiler_params=_cparams(("parallel", "arbitrary")),
        name="pool_state" if has_state else "pool_fresh",
    )(*args)


def _post_kernel(pa_ref, pb_ref, oa_ref, ob_ref, ga_ref, gb_ref, xa_ref, xb_ref, wp_ref, ps_ref, wbb_ref, wba_ref,
                 wo_ref, g_ref, b_ref, h_ref, hp_ref, *, alpha, tiles_a):
    first = pl.program_id(0) < tiles_a
    x = jnp.where(first, xa_ref[...], xb_ref[...])
    pooled = jnp.where(first, pa_ref[...], pb_ref[...])
    o_gated = jnp.where(first, oa_ref[...], ob_ref[...])
    mixed = jnp.concatenate(
        [_dot(pooled[:, gi * POOL_GROUP_W:(gi + 1) * POOL_GROUP_W], wp_ref[gi]) for gi in range(len(POOL_WINDOWS))],
        axis=1) * ps_ref[...]
    branch_b = _dot(mixed, wbb_ref[...])
    branch_a = _dot(o_gated, wba_ref[...])
    merged = _sigmoid(ga_ref[...].astype(F32)) * branch_a + _sigmoid(gb_ref[...].astype(F32)) * branch_b
    h = _layer_norm(alpha * x + _dot(merged, wo_ref[...]), g_ref[...], b_ref[...])
    h_ref[...] = h
    _store_pieces(hp_ref, h)


def _post(pooled_a, pooled_b, o_a, o_b, proj, x_a, x_b, w_pool, pool_scale, w_br_b, w_br_a, w_out, ln_g, ln_b,
          alpha, tm):
    t = x_a.shape[0] + x_b.shape[0]
    tiles_a = x_a.shape[0] // tm
    row = lambda i: (i, 0)
    row_a = lambda i: (jnp.minimum(i, tiles_a - 1), 0)
    row_b = lambda i: (jnp.maximum(i - tiles_a, 0), 0)
    const2 = lambda i: (0, 0)
    return pl.pallas_call(
        functools.partial(_post_kernel, alpha=alpha, tiles_a=tiles_a),
        grid=(t // tm,),
        in_specs=[pl.BlockSpec((tm, D_MODEL), row_a),
                  pl.BlockSpec((tm, D_MODEL), row_b),
                  pl.BlockSpec((tm, V_W), row_a),
                  pl.BlockSpec((tm, V_W), row_b),
                  pl.BlockSpec((tm, D_MODEL), lambda i: (i, U_BLK + 1)),
                  pl.BlockSpec((tm, D_MODEL), lambda i: (i, U_BLK + 2)),
                  pl.BlockSpec((tm, D_MODEL), row_a),
                  pl.BlockSpec((tm, D_MODEL), row_b),
                  pl.BlockSpec(w_pool.shape, lambda i: (0, 0, 0)),
                  pl.BlockSpec((1, D_MODEL), const2),
                  pl.BlockSpec(w_br_b.shape, const2),
                  pl.BlockSpec(w_br_a.shape, const2),
                  pl.BlockSpec(w_out.shape, const2),
                  pl.BlockSpec((1, D_MODEL), const2),
                  pl.BlockSpec((1, D_MODEL), const2)],
        out_specs=[pl.BlockSpec((tm, D_MODEL), row), pl.BlockSpec((N_PIECE, tm, PIECE_W), lambda i: (0, i, 0))],
        out_shape=[jax.ShapeDtypeStruct((t, D_MODEL), F32),
                   jax.ShapeDtypeStruct((N_PIECE, t, PIECE_W), jnp.uint32)],
        compiler_params=_cparams(("parallel",)),
        name="post_mixers",
    )(pooled_a, pooled_b, o_a, o_b, proj, proj, x_a, x_b, w_pool, pool_scale, w_br_b, w_br_a, w_out, ln_g, ln_b)


def _router_kernel(h_ref, wt_ref, bias_ref, idx_ref, wts_ref, rank_ref, cnt_ref, carry_scr, *, tm):
    i = pl.program_id(0)

    @pl.when(i == 0)
    def _init():
        carry_scr[...] = jnp.zeros(carry_scr.shape, F32)

    logits = _dot3_nt(wt_ref[...], h_ref[...])
    sc = _sigmoid(logits)
    ch = sc + bias_ref[...]
    neg = -jnp.inf
    e_in = lax.broadcasted_iota(jnp.int32, (GROUP_SZ, tm), 0)
    gs_rows = []
    for g in range(N_GROUP):
        blk = ch[g * GROUP_SZ:(g + 1) * GROUP_SZ]
        m1 = jnp.max(blk, axis=0, keepdims=True)
        i1 = jnp.min(jnp.where(blk == m1, e_in, GROUP_SZ), axis=0, keepdims=True)
        m2 = jnp.max(jnp.where(e_in == i1, neg, blk), axis=0, keepdims=True)
        gs_rows.append(m1 + m2)
    gs = jnp.concatenate(gs_rows, axis=0)
    g_io = lax.broadcasted_iota(jnp.int32, (N_GROUP, tm), 0)
    e_io = lax.broadcasted_iota(jnp.int32, (N_EXPERTS, tm), 0)
    e_grp = e_io >> int(math.log2(GROUP_SZ))
    masked = jnp.full((N_EXPERTS, tm), neg, F32)
    for _ in range(TOPK_GROUP):
        mx = jnp.max(gs, axis=0, keepdims=True)
        gi = jnp.min(jnp.where(gs == mx, g_io, N_GROUP), axis=0, keepdims=True)
        gs = jnp.where(g_io == gi, neg, gs)
        masked = jnp.where(e_grp == gi, ch, masked)
    idx_rows, w_rows = [], []
    onehot = jnp.zeros((N_EXPERTS, tm), F32)
    for _ in range(TOP_K):
        mx = jnp.max(masked, axis=0, keepdims=True)
        ei = jnp.min(jnp.where(masked == mx, e_io, N_EXPERTS), axis=0, keepdims=True)
        hit = e_io == ei
        idx_rows.append(ei)
        w_rows.append(jnp.sum(jnp.where(hit, sc, 0.0), axis=0, keepdims=True))
        onehot = jnp.where(hit, 1.0, onehot)
        masked = jnp.where(hit, neg, masked)
    wsel = jnp.concatenate(w_rows, axis=0)
    wts_ref[...] = wsel / jnp.sum(wsel, axis=0, keepdims=True) * ROUTED_SCALE
    idx_ref[...] = jnp.concatenate(idx_rows, axis=0)

    tr = lax.broadcasted_iota(jnp.int32, (tm, tm), 0)
    tc_ = lax.broadcasted_iota(jnp.int32, (tm, tm), 1)
    before = jnp.where(tr < tc_, 1.0, 0.0).astype(BF16)
    cum = jnp.dot(onehot.astype(BF16), before, preferred_element_type=F32) + carry_scr[...]
    rank_ref[...] = jnp.concatenate(
        [jnp.sum(jnp.where(e_io == ei, cum, 0.0), axis=0, keepdims=True) for ei in idx_rows],
        axis=0).astype(jnp.int32)
    carry_scr[...] = carry_scr[...] + jnp.sum(onehot, axis=1, keepdims=True)
    cnt_ref[...] = carry_scr[...]


def _router(h, w_router_t, bias_col, tm):
    t = h.shape[0]
    return pl.pallas_call(
        functools.partial(_router_kernel, tm=tm),
        grid=(t // tm,),
        in_specs=[pl.BlockSpec((tm, D_MODEL), lambda i: (i, 0)),
                  pl.BlockSpec((N_EXPERTS, D_MODEL), lambda i: (0, 0)),
                  pl.BlockSpec((N_EXPERTS, 1), lambda i: (0, 0))],
        out_specs=[pl.BlockSpec((TOP_K, tm), lambda i: (0, i)),
                   pl.BlockSpec((TOP_K, tm), lambda i: (0, i)),
                   pl.BlockSpec((TOP_K, tm), lambda i: (0, i)),
                   pl.BlockSpec((N_EXPERTS, 1), lambda i: (0, 0))],
        out_shape=[jax.ShapeDtypeStruct((TOP_K, t), jnp.int32),
                   jax.ShapeDtypeStruct((TOP_K, t), F32),
                   jax.ShapeDtypeStruct((TOP_K, t), jnp.int32),
                   jax.ShapeDtypeStruct((N_EXPERTS, 1), F32)],
        scratch_shapes=[pltpu.VMEM((N_EXPERTS, 1), F32)],
        compiler_params=_cparams(("arbitrary",)),
        name="router",
    )(h, w_router_t, bias_col)


def _dest_kernel(idx_ref, rank_ref, start_ref, dest_ref, *, tm, n_rows):
    e_io = lax.broadcasted_iota(jnp.int32, (N_EXPERTS, tm), 0)
    starts = start_ref[...]
    rows = []
    for k in range(TOP_K):
        seg = jnp.sum(jnp.where(e_io == idx_ref[k:k + 1, :], starts, 0.0), axis=0, keepdims=True)
        rows.append(seg.astype(jnp.int32) + rank_ref[k:k + 1, :])
    base = jnp.concatenate(rows, axis=0)
    for p in range(N_PIECE):
        dest_ref[p] = base + p * n_rows


def _dest(idx_t, rank_t, seg_start_col, n_rows, tm):
    t = idx_t.shape[1]
    blk = pl.BlockSpec((TOP_K, tm), lambda i: (0, i))
    return pl.pallas_call(
        functools.partial(_dest_kernel, tm=tm, n_rows=n_rows),
        grid=(t // tm,),
        in_specs=[blk, blk, pl.BlockSpec((N_EXPERTS, 1), lambda i: (0, 0))],
        out_specs=pl.BlockSpec((N_PIECE, TOP_K, tm), lambda i: (0, 0, i)),
        out_shape=jax.ShapeDtypeStruct((N_PIECE, TOP_K, t), jnp.int32),
        compiler_params=_cparams(("parallel",)),
        name="dispatch_rows",
    )(idx_t, rank_t, seg_start_col)


def _expert_kernel(be_ref, nu_ref, first_ref, nxt_ref, slot_ref, x_ref, wg_hbm, wu_hbm, wd_hbm, y_ref,
                   wg_buf, wu_buf, wd_buf, wgu_scr, wd_scr, sems):
    i = pl.program_id(0)
    live = i < nu_ref[0]

    def fetch(e, slot):
        return [pltpu.make_async_copy(wg_hbm.at[e], wg_buf.at[slot], sems.at[slot, 0]),
                pltpu.make_async_copy(wu_hbm.at[e], wu_buf.at[slot], sems.at[slot, 1]),
                pltpu.make_async_copy(wd_hbm.at[e], wd_buf.at[slot], sems.at[slot, 2])]

    @pl.when(live & (first_ref[i] == 1))
    def _new_expert():
        slot = slot_ref[i]

        @pl.when(i == 0)
        def _():
            for cp in fetch(be_ref[i], slot):
                cp.start()

        @pl.when(nxt_ref[i] >= 0)
        def _():
            for cp in fetch(nxt_ref[i], 1 - slot):
                cp.start()

        for cp in fetch(be_ref[i], slot):
            cp.wait()
        wgu_scr[:, :D_EXPERT] = wg_buf[slot].astype(BF16)
        wgu_scr[:, D_EXPERT:] = wu_buf[slot].astype(BF16)
        wd_scr[...] = wd_buf[slot].astype(BF16)

    @pl.when(live)
    def _():
        n_sub = 2
        sub = EXPERT_BM // n_sub
        xs = [jnp.concatenate([c.astype(BF16) for c in
                               _load_pieces([x_ref[p, s * sub:(s + 1) * sub] for p in range(N_PIECE)])], axis=1)
              for s in range(n_sub)]
        gus = [jnp.dot(x, wgu_scr[...], preferred_element_type=F32) for x in xs]
        acts = [(_silu(gu[:, :D_EXPERT]) * gu[:, D_EXPERT:]).astype(BF16) for gu in gus]
        ys = [jnp.dot(a, wd_scr[...], preferred_element_type=F32) for a in acts]
        for s, y in enumerate(ys):
            for p in range(N_PIECE):
                y_ref[p, s * sub:(s + 1) * sub] = _pack_halves(y[:, 2 * p * PIECE_W:2 * (p + 1) * PIECE_W])


def _experts(xs, blk_exp, n_used, blk_first, blk_next, blk_slot, w_gate, w_up, w_down):
    n_rows = xs.shape[1]
    n_blocks = n_rows // EXPERT_BM

    def row_block(i, be, nu, *_):
        return (0, jnp.minimum(i, nu[0] - 1), 0)

    grid_spec = pltpu.PrefetchScalarGridSpec(
        num_scalar_prefetch=5,
        grid=(n_blocks,),
        in_specs=[pl.BlockSpec((N_PIECE, EXPERT_BM, PIECE_W), row_block),
                  pl.BlockSpec(memory_space=pl.ANY),
                  pl.BlockSpec(memory_space=pl.ANY),
                  pl.BlockSpec(memory_space=pl.ANY)],
        out_specs=pl.BlockSpec((N_PIECE, EXPERT_BM, PIECE_W), row_block),
        scratch_shapes=[pltpu.VMEM((2, D_MODEL, D_EXPERT), F32),
                        pltpu.VMEM((2, D_MODEL, D_EXPERT), F32),
                        pltpu.VMEM((2, D_EXPERT, D_MODEL), F32),
                        pltpu.VMEM((D_MODEL, 2 * D_EXPERT), BF16),
                        pltpu.VMEM((D_EXPERT, D_MODEL), BF16),
                        pltpu.SemaphoreType.DMA((2, 3))],
    )
    return pl.pallas_call(
        _expert_kernel,
        grid_spec=grid_spec,
        out_shape=jax.ShapeDtypeStruct((N_PIECE, n_rows, PIECE_W), jnp.uint32),
        compiler_params=_cparams(("arbitrary",)),
        name="experts",
    )(blk_exp, n_used, blk_first, blk_next, blk_slot, xs, w_gate, w_up, w_down)


SC_WINDOW = 128
V7X_SC_CORES = 2
V7X_SC_SUBCORES = 16


def _sc_mesh():
    return plsc.VectorSubcoreMesh(core_axis_name="core", subcore_axis_name="subcore",
                                  num_cores=V7X_SC_CORES, num_subcores=V7X_SC_SUBCORES)


def _sc_scatter_rows(src, dest, n_rows, seg, repeat):
    d = src.shape[1]
    n_idx = dest.shape[0]
    seg_blocks = seg // SC_WINDOW
    dest2 = dest.reshape(1, n_idx)

    def src_block(i):
        return ((i // (repeat * seg_blocks)) * seg_blocks + i % seg_blocks, 0)

    @functools.partial(pl.kernel, out_type=jax.ShapeDtypeStruct((n_rows, d), src.dtype), mesh=_sc_mesh(),
                       scratch_types=[], name="sc_dispatch")
    def run(src_hbm, idx_hbm, out_hbm):
        def body(rows_vmem, idx_vmem):
            pltpu.sync_copy(rows_vmem, out_hbm.at[idx_vmem.at[0]])

        pltpu.emit_pipeline(
            body,
            grid=(n_idx // SC_WINDOW,),
            in_specs=[pl.BlockSpec((SC_WINDOW, d), src_block),
                      pl.BlockSpec((1, SC_WINDOW), lambda i: (0, i))],
            out_specs=[],
            core_axis_name=("core", "subcore"),
            dimension_semantics=(pltpu.PARALLEL,),
        )(src_hbm, idx_hbm)

    return run(src, dest2)


def _sc_gather_rows(table, idx):
    d = table.shape[1]
    n_idx = idx.shape[0]
    idx2 = idx.reshape(1, n_idx)

    @functools.partial(pl.kernel, out_type=jax.ShapeDtypeStruct((n_idx, d), table.dtype), mesh=_sc_mesh(),
                       scratch_types=[], name="sc_combine_gather")
    def run(table_hbm, idx_hbm, out_hbm):
        def body(idx_vmem, rows_vmem):
            pltpu.sync_copy(table_hbm.at[idx_vmem.at[0]], rows_vmem)

        pltpu.emit_pipeline(
            body,
            grid=(n_idx // SC_WINDOW,),
            in_specs=[pl.BlockSpec((1, SC_WINDOW), lambda i: (0, i))],
            out_specs=[pl.BlockSpec((SC_WINDOW, d), lambda i: (i, 0))],
            core_axis_name=("core", "subcore"),
            dimension_semantics=(pltpu.PARALLEL,),
        )(idx_hbm, out_hbm)

    return run(table, idx2)


def _combine_kernel(yg_ref, wts_ref, h_ref, wgu_ref, wd_ref, g_ref, b_ref, outa_ref, outb_ref, *, alpha, tiles_a):
    wts = wts_ref[...]
    acc = None
    for k in range(TOP_K):
        cols = _load_pieces([yg_ref[p, k] for p in range(N_PIECE)])
        wk = wts[:, k:k + 1]
        acc = [c * wk for c in cols] if acc is None else [a + c * wk for a, c in zip(acc, cols)]
    routed = jnp.concatenate(acc, axis=1)
    h = h_ref[...]
    gu = _dot(h, wgu_ref[...])
    shared = _dot(_silu(gu[:, :D_EXPERT]) * gu[:, D_EXPERT:], wd_ref[...])
    y = _layer_norm(alpha * h + (routed + shared), g_ref[...], b_ref[...])
    i = pl.program_id(0)

    @pl.when(i < tiles_a)
    def _():
        outa_ref[...] = y

    @pl.when(i >= tiles_a)
    def _():
        outb_ref[...] = y


def _combine(yg, wts, h, w_sh_gu, w_sh_down, ln_g, ln_b, alpha, tm, t_a):
    t = h.shape[0]
    tiles_a = t_a // tm
    row = lambda i: (i, 0)
    const2 = lambda i: (0, 0)
    return pl.pallas_call(
        functools.partial(_combine_kernel, alpha=alpha, tiles_a=tiles_a),
        grid=(t // tm,),
        in_specs=[pl.BlockSpec((N_PIECE, TOP_K, tm, PIECE_W), lambda i: (0, 0, i, 0)),
                  pl.BlockSpec((tm, TOP_K), row),
                  pl.BlockSpec((tm, D_MODEL), row),
                  pl.BlockSpec(w_sh_gu.shape, const2),
                  pl.BlockSpec(w_sh_down.shape, const2),
                  pl.BlockSpec((1, D_MODEL), const2),
                  pl.BlockSpec((1, D_MODEL), const2)],
        out_specs=[pl.BlockSpec((tm, D_MODEL), lambda i: (jnp.minimum(i, tiles_a - 1), 0)),
                   pl.BlockSpec((tm, D_MODEL), lambda i: (jnp.maximum(i - tiles_a, 0), 0))],
        out_shape=[jax.ShapeDtypeStruct((t_a, D_MODEL), F32), jax.ShapeDtypeStruct((t - t_a, D_MODEL), F32)],
        compiler_params=_cparams(("arbitrary",)),
        name="combine_ln2",
    )(yg, wts, h, w_sh_gu, w_sh_down, ln_g, ln_b)


def _layer(xp, xs, s_delta, s_conv, s_pool, w_in, conv_w, a_log, dt_bias, gamma_a, w_br_a, w_pool,
           pool_scale, w_br_b, w_out, ln1_g, ln1_b, w_router, router_bias, w_exp_gate, w_exp_up,
           w_exp_down, w_sh_gate, w_sh_up, w_sh_down, ln2_g, ln2_b, alpha):
    Bp, Lp, _ = xp.shape
    Bs, Ls, _ = xs.shape
    Tp, Ts = Bp * Lp, Bs * Ls
    T = Tp + Ts
    x_p = xp.reshape(Tp, D_MODEL)
    x_s = xs.reshape(Ts, D_MODEL)

    o_b, o_u = QKV_W + V_W, QKV_W + V_W + 2 * H_V
    w_main = jnp.concatenate([w_in[:, :o_b], w_in[:, o_u:]], axis=1).astype(BF16)
    w_ba = jnp.pad(w_in[:, o_b:o_u], ((0, 0), (0, 128 - 2 * H_V))).astype(BF16)
    proj, ba = _in_proj(x_p, x_s, w_main, w_ba, IN_PROJ_TM, IN_PROJ_TN)

    o_p, sd_p = _gdn(proj, ba, conv_w, a_log, dt_bias, gamma_a, 0, Bp, Lp, GDN_CHUNK, 1, GDN_FRESH_CHUNKS)
    o_s, sd_s = _gdn(proj, ba, conv_w, a_log, dt_bias, gamma_a, Tp, Bs, Ls, Ls, GDN_STATE_SEQS, 1,
                     conv_buf=s_conv, s0=s_delta)

    pooled_p = _pool(proj, 0, Bp, Lp, POOL_TILE, 1, 0)
    pooled_s = _pool(proj, Tp, Bs, Ls, Ls, POOL_STATE_SEQS, PAST_LEN, pool_buf=s_pool)

    h, hp = _post(pooled_p, pooled_s, o_p, o_s, proj, x_p, x_s, w_pool.astype(BF16), pool_scale.reshape(1, D_MODEL),
                  w_br_b.astype(BF16), w_br_a.astype(BF16), w_out.astype(BF16),
                  ln1_g.reshape(1, D_MODEL), ln1_b.reshape(1, D_MODEL), alpha, POST_TM)

    idx_t, wts_t, rank_t, cnt = _router(h, w_router.T, router_bias.reshape(N_EXPERTS, 1), ROUTER_TM)
    counts = cnt[:, 0].astype(jnp.int32)
    padded = ((counts + EXPERT_BM - 1) // EXPERT_BM) * EXPERT_BM
    pends = jnp.cumsum(padded)
    pstarts = pends - padded
    n_blocks = (T * TOP_K + N_EXPERTS * (EXPERT_BM - 1) + EXPERT_BM - 1) // EXPERT_BM
    n_rows = n_blocks * EXPERT_BM
    piece_rows = _dest(idx_t, rank_t, pstarts.astype(F32).reshape(N_EXPERTS, 1), n_rows, ROUTER_TM)
    blk_start = jnp.arange(n_blocks, dtype=jnp.int32) * EXPERT_BM
    blk_exp = jnp.minimum(jnp.sum((pends[None, :] <= blk_start[:, None]).astype(jnp.int32), axis=1),
                          N_EXPERTS - 1)
    n_used = (pends[-1:] // EXPERT_BM).astype(jnp.int32)
    has_rows = counts > 0
    e_ids = jnp.arange(N_EXPERTS, dtype=jnp.int32)
    slot_of = (jnp.cumsum(has_rows.astype(jnp.int32)) - 1) & 1
    later = jnp.where(has_rows, e_ids, N_EXPERTS)
    next_of = jnp.concatenate([lax.cummin(later, reverse=True)[1:], jnp.full((1,), N_EXPERTS, jnp.int32)])
    next_of = jnp.where(next_of < N_EXPERTS, next_of, -1)
    blk_first = (blk_start == pstarts[blk_exp]).astype(jnp.int32)
    blk_next = next_of[blk_exp]
    blk_slot = slot_of[blk_exp]

    piece_idx = piece_rows.reshape(N_PIECE * TOP_K * T)
    x_sorted = _sc_scatter_rows(hp.reshape(N_PIECE * T, PIECE_W), piece_idx, N_PIECE * n_rows, T, TOP_K)
    y_sorted = _experts(x_sorted.reshape(N_PIECE, n_rows, PIECE_W), blk_exp, n_used, blk_first, blk_next,
                        blk_slot, w_exp_gate, w_exp_up, w_exp_down)
    yg = _sc_gather_rows(y_sorted.reshape(N_PIECE * n_rows, PIECE_W), piece_idx)
    yg = yg.reshape(N_PIECE, TOP_K, T, PIECE_W)

    w_sh_gu = jnp.concatenate([w_sh_gate, w_sh_up], axis=1).astype(BF16)
    y_p, y_s = _combine(yg, wts_t.T, h, w_sh_gu, w_sh_down.astype(BF16),
                        ln2_g.reshape(1, D_MODEL), ln2_b.reshape(1, D_MODEL), alpha, COMBINE_TM, Tp)

    def carried_rows(keep, col0, width, name):
        outs = []
        for x2d, B, L, old in ((x_p, Bp, Lp, None), (x_s, Bs, Ls, (s_conv, s_pool)[keep == POOL_BUF])):
            n = min(L, keep)
            if old is None and n < keep:
                old = jnp.zeros((B, keep, width), F32)
            rows = jnp.arange(B, dtype=jnp.int32)[:, None] * L + (L - n) + jnp.arange(n, dtype=jnp.int32)
            xt = jnp.take(x2d, rows.reshape(-1), axis=0).astype(BF16)
            outs.append(_carried(xt, w_main, col0, width, B, n, keep, old if n < keep else None, name))
        return outs

    new_conv_p, new_conv_s = carried_rows(CONV_W - 1, 0, QKV_W, "in_proj_conv_rows")
    new_pool_p, new_pool_s = carried_rows(POOL_BUF, U_BLK * D_MODEL, D_MODEL, "in_proj_pool_rows")
    return (y_p.reshape(Bp, Lp, D_MODEL), y_s.reshape(Bs, Ls, D_MODEL),
            sd_p, new_conv_p, new_pool_p, sd_s, new_conv_s, new_pool_s)


def kernel(x_prompt, x_sample, state_delta, state_conv, state_pool, w_in, conv_w, a_log, dt_bias, gamma_a,
           w_br_a, w_pool, pool_scale, w_br_b, w_out, ln1_g, ln1_b, w_router, router_bias,
           w_exp_gate, w_exp_up, w_exp_down, w_sh_gate, w_sh_up, w_sh_down, ln2_g, ln2_b):
    depth = w_in.shape[0]
    alpha = (2 * depth) ** 0.25
    yp, ys = x_prompt, x_sample
    outs = [[] for _ in range(6)]
    for l in range(depth):
        res = _layer(yp, ys, state_delta[l], state_conv[l], state_pool[l], w_in[l], conv_w[l], a_log[l],
                     dt_bias[l], gamma_a[l], w_br_a[l], w_pool[l], pool_scale[l], w_br_b[l], w_out[l],
                     ln1_g[l], ln1_b[l], w_router[l], router_bias[l], w_exp_gate[l], w_exp_up[l],
                     w_exp_down[l], w_sh_gate[l], w_sh_up[l], w_sh_down[l], ln2_g[l], ln2_b[l], alpha)
        yp, ys = res[0], res[1]
        for lst, v in zip(outs, res[2:]):
            lst.append(v)
    return (yp, ys) + tuple(jnp.stack(v) for v in outs)
```

```python
import functools
import math

import jax
import jax.numpy as jnp
from jax import lax
from jax.experimental import pallas as pl
from jax.experimental.pallas import tpu as pltpu
from jax.experimental.pallas import tpu_sc as plsc

F32 = jnp.float32
BF16 = jnp.bfloat16

D_MODEL = 1024
H_K = 8
D_K = 128
H_V = 16
D_V = 128
Q_W = H_K * D_K
V_W = H_V * D_V
QKV_W = 2 * Q_W + V_W
CONV_W = 4
POOL_WINDOWS = (2, 4, 8, 16)
POOL_GROUP_W = D_MODEL // len(POOL_WINDOWS)
POOL_BUF = max(POOL_WINDOWS) - 1
N_EXPERTS = 256
TOP_K = 8
N_GROUP = 8
TOPK_GROUP = 4
GROUP_SZ = N_EXPERTS // N_GROUP
D_EXPERT = D_MODEL // 4
ROUTED_SCALE = 2.5
LN_EPS = 1e-5
NORM_EPS = 1e-6
PAST_LEN = 16384

MAIN_W = QKV_W + V_W + 3 * D_MODEL
Z_BLK = QKV_W // V_W
U_BLK = (QKV_W + V_W) // D_MODEL

IN_PROJ_TM = 1024
IN_PROJ_TN = MAIN_W // 3
GDN_CHUNK = 64
POOL_TILE = 256
POST_TM = 512
ROUTER_TM = 512
COMBINE_TM = 512

STACK_ROWS = 128
SOLVE_BASE = 16
GDN_FRESH_CHUNKS = 4
EXPERT_BM = 512
GDN_STATE_SEQS = 8
POOL_STATE_SEQS = 16
VMEM_LIMIT = 56 * 1024 * 1024


def _cparams(sem):
    return pltpu.CompilerParams(dimension_semantics=sem, vmem_limit_bytes=VMEM_LIMIT)


def _sigmoid(x):
    return 0.5 * jnp.tanh(0.5 * x) + 0.5


def _silu(x):
    return x * _sigmoid(x)


def _softplus(x):
    return jnp.maximum(x, 0.0) + jnp.log(1.0 + jnp.exp(-jnp.abs(x)))


def _dot(a, b):
    return jnp.dot(a.astype(BF16), b.astype(BF16), preferred_element_type=F32)


def _dot_nt(a, b):
    return lax.dot_general(a.astype(BF16), b.astype(BF16), (((1,), (1,)), ((), ())),
                           preferred_element_type=F32)


def _dot_tn(a, b):
    return lax.dot_general(a.astype(BF16), b.astype(BF16), (((0,), (0,)), ((), ())),
                           preferred_element_type=F32)


def _split(a):
    hi = a.astype(BF16)
    lo = (a - hi.astype(F32)).astype(BF16)
    return hi, lo


def _dot3_nt(a, b):
    ah, al = _split(a)
    bh, bl = _split(b)
    d = functools.partial(lax.dot_general, dimension_numbers=(((1,), (1,)), ((), ())),
                          preferred_element_type=F32)
    return d(ah, bh) + (d(ah, bl) + d(al, bh))


def _pack_halves(x):
    n = x.shape[1] // 2
    hi = lax.bitcast_convert_type(x[:, :n].astype(BF16).astype(F32), jnp.uint32)
    lo = lax.bitcast_convert_type(x[:, n:].astype(BF16).astype(F32), jnp.uint32)
    return (hi & jnp.uint32(0xFFFF0000)) | (lo >> 16)


def _unpack_halves(w):
    hi = lax.bitcast_convert_type(w & jnp.uint32(0xFFFF0000), F32)
    lo = lax.bitcast_convert_type(w << 16, F32)
    return hi, lo


N_PIECE = 2
PIECE_W = D_MODEL // (2 * N_PIECE)


def _store_pieces(ref, x):
    for p in range(N_PIECE):
        ref[p] = _pack_halves(x[:, 2 * p * PIECE_W:2 * (p + 1) * PIECE_W])


def _load_pieces(pieces):
    cols = []
    for w in pieces:
        cols.extend(_unpack_halves(w))
    return cols


def _layer_norm(x, g, b):
    mu = jnp.mean(x, axis=-1, keepdims=True)
    xc = x - mu
    var = jnp.mean(xc * xc, axis=-1, keepdims=True)
    return xc * lax.rsqrt(var + LN_EPS) * g + b


def _in_proj_kernel(xa_ref, xb_ref, w_ref, wba_ref, o_ref, ba_ref, x_scr, *, tiles_a):
    @pl.when(pl.program_id(1) == 0)
    def _():
        x = jnp.where(pl.program_id(0) < tiles_a, xa_ref[...], xb_ref[...]).astype(BF16)
        x_scr[...] = x
        ba_ref[...] = jnp.dot(x, wba_ref[...], preferred_element_type=F32)

    o_ref[...] = jnp.dot(x_scr[...], w_ref[...], preferred_element_type=F32).astype(o_ref.dtype)


def _in_proj(x_a, x_b, w_main, w_ba, tm, tn):
    k = x_a.shape[1]
    t = x_a.shape[0] + x_b.shape[0]
    n = w_main.shape[1]
    tiles_a = x_a.shape[0] // tm
    return pl.pallas_call(
        functools.partial(_in_proj_kernel, tiles_a=tiles_a),
        grid=(t // tm, n // tn),
        in_specs=[pl.BlockSpec((tm, k), lambda i, j: (jnp.minimum(i, tiles_a - 1), 0)),
                  pl.BlockSpec((tm, k), lambda i, j: (jnp.maximum(i - tiles_a, 0), 0)),
                  pl.BlockSpec((k, tn), lambda i, j: (0, j)),
                  pl.BlockSpec(w_ba.shape, lambda i, j: (0, 0))],
        out_specs=[pl.BlockSpec((tm, tn), lambda i, j: (i, j)),
                   pl.BlockSpec((tm, w_ba.shape[1]), lambda i, j: (i, 0))],
        out_shape=[jax.ShapeDtypeStruct((t, n), BF16), jax.ShapeDtypeStruct((t, w_ba.shape[1]), F32)],
        scratch_shapes=[pltpu.VMEM((tm, k), BF16)],
        compiler_params=_cparams(("parallel", "arbitrary")),
        name="in_proj",
    )(x_a, x_b, w_main, w_ba)


def _mm_kernel(x_ref, w_ref, o_ref):
    o_ref[...] = jnp.dot(x_ref[...], w_ref[...], preferred_element_type=F32).astype(o_ref.dtype)


def _matmul(x, w, col0, n, tm, tn, out_dtype, name):
    t, k = x.shape
    cb0 = col0 // tn
    return pl.pallas_call(
        _mm_kernel,
        grid=(t // tm, n // tn),
        in_specs=[pl.BlockSpec((tm, k), lambda i, j: (i, 0)),
                  pl.BlockSpec((k, tn), lambda i, j: (0, cb0 + j))],
        out_specs=pl.BlockSpec((tm, tn), lambda i, j: (i, j)),
        out_shape=jax.ShapeDtypeStruct((t, n), out_dtype),
        compiler_params=_cparams(("parallel", "parallel")),
        name=name,
    )(x, w)


def _gdn_kernel(*refs, C, hg, nseq, nck, has_state):
    if has_state:
        (qkv_ref, z_ref, ba_ref, cw_ref, ad_ref, gam_ref, buf_ref, s0_ref,
         o_ref, sout_ref, s_scr, tail_scr, act_scr) = refs
    else:
        (qkv_ref, z_ref, ba_ref, cw_ref, ad_ref, gam_ref,
         o_ref, sout_ref, s_scr, tail_scr, act_scr) = refs
    R = hg * C
    ng = H_V // hg
    log2c = int(math.log2(C))
    c = pl.program_id(1)

    @pl.when(c == 0)
    def _init():
        if has_state:
            tail_scr[...] = jnp.zeros(tail_scr.shape, F32)
            for s in range(nseq):
                s_scr[s * H_V:(s + 1) * H_V] = s0_ref[s]
                tail_scr[(s + 1) * 8 - (CONV_W - 1):(s + 1) * 8] = buf_ref[s]
        else:
            s_scr[...] = jnp.zeros(s_scr.shape, F32)
            tail_scr[...] = jnp.zeros(tail_scr.shape, F32)

    cw = cw_ref[...]
    qkv_all = qkv_ref[...].astype(F32)
    n_prev = CONV_W - 1
    CB = nck * C
    for s in range(nseq):
        x = qkv_all[s * CB:(s + 1) * CB]
        xe = jnp.concatenate([tail_scr[s * 8:(s + 1) * 8], x], axis=0)
        y = x * cw[n_prev:CONV_W]
        for j in range(n_prev):
            y = y + xe[8 - n_prev + j:8 - n_prev + j + CB] * cw[j:j + 1]
        tail_scr[s * 8:(s + 1) * 8] = x[CB - 8:CB]
        act_scr[s * CB:(s + 1) * CB] = _silu(y)
        for off, scale in ((0, D_K ** -0.5), (Q_W, 1.0)):
            for kh in range(H_K):
                cols = slice(off + kh * D_K, off + (kh + 1) * D_K)
                v = act_scr[s * CB:(s + 1) * CB, cols]
                act_scr[s * CB:(s + 1) * CB, cols] = v * (
                    lax.rsqrt(jnp.sum(v * v, axis=-1, keepdims=True) + NORM_EPS) * scale)

    ri = lax.broadcasted_iota(jnp.int32, (R, R), 0)
    ci = lax.broadcasted_iota(jnp.int32, (R, R), 1)
    same = (ri >> log2c) == (ci >> log2c)
    incl = same & (ri >= ci)
    strict = same & (ri > ci)
    eye = ri == ci
    is_last = ci == (((ri >> log2c) << log2c) + (C - 1))
    eye_f = jnp.where(eye, 1.0, 0.0).astype(F32)
    cum_u = jnp.where(same & (ri <= ci), 1.0, 0.0).astype(BF16)

    nrow = ba_ref.shape[2] // 2
    adv = ad_ref[...]
    dd = functools.partial(jnp.dot, preferred_element_type=F32)
    beta_of, G_of = {}, {}
    for s in range(nseq):
        for k in range(nck):
            bav = ba_ref[s, k]
            beta_of[s, k] = _sigmoid(bav[0:nrow])
            g_rows = -jnp.exp(adv[0:nrow]) * _softplus(bav[nrow:2 * nrow] + adv[nrow:2 * nrow])
            g1 = g_rows.astype(BF16)
            r1 = g_rows - g1.astype(F32)
            g2 = r1.astype(BF16)
            g3 = (r1 - g2.astype(F32)).astype(BF16)
            G_of[s, k] = dd(g1, cum_u) + (dd(g2, cum_u) + dd(g3, cum_u))

    gam = gam_ref[...]
    rep = H_V // H_K

    def stacked(d, off, per_k_head):
        blocks = [(h // rep if per_k_head else h) for h in d["heads"]]
        return jnp.concatenate([act_scr[d["rows"], off + b * D_K:off + (b + 1) * D_K] for b in blocks], axis=0)

    log2b = min(log2c, int(math.log2(SOLVE_BASE)))
    same_base = ((ri ^ ci) >> log2b) == 0
    row_head = lax.broadcasted_iota(jnp.int32, (R, D_V), 0) >> log2c
    z_all = z_ref[...].astype(F32)
    grp = [dict(s=s, k=k, g=g, rows=slice(s * CB + k * C, s * CB + (k + 1) * C), s0=s * H_V,
                heads=[g * hg + hh for hh in range(hg)])
           for k in range(nck) for s in range(nseq) for g in range(ng)]
    for d in grp:
        kst = stacked(d, Q_W, True)
        d["kk"] = _dot_nt(kst, kst)
        d["qk"] = _dot_nt(stacked(d, 0, True), kst)
    for d in grp:
        grow = G_of[d["s"], d["k"]][d["g"]:d["g"] + 1]
        brow = beta_of[d["s"], d["k"]][d["g"]:d["g"] + 1]
        gcol = jnp.sum(eye_f * grow, axis=1, keepdims=True)
        bcol = jnp.sum(eye_f * brow, axis=1, keepdims=True)
        glast = jnp.sum(jnp.where(is_last, grow, 0.0), axis=1, keepdims=True)
        dm = jnp.where(incl, jnp.exp(jnp.minimum(gcol - grow, 0.0)), 0.0)
        d.update(gcol=gcol, bcol=bcol, glast=glast, eg=jnp.exp(gcol))
        d["m"] = jnp.where(strict, -(d["kk"] * bcol * dm), 0.0)
        d["attn"] = jnp.where(incl, d["qk"] * dm, 0.0)
        d["pw"] = jnp.where(same_base, d["m"], 0.0)
        d["t"] = eye_f + d["pw"]

    for r in range(log2b):
        last = r == log2b - 1
        for d in grp:
            if r == 0:
                if not last:
                    d["pw"] = _dot(d["pw"], d["pw"])
            elif last:
                d["t"] = d["t"] + _dot(d["pw"], d["t"])
            else:
                out = _dot(d["pw"], jnp.concatenate([d["pw"], d["t"]], axis=1))
                d["pw"] = out[:, :R]
                d["t"] = d["t"] + out[:, R:]
    for lev in range(log2b + 1, log2c + 1):
        coupling = ((ri ^ ci) >> (lev - 1)) == 1
        for d in grp:
            d["y"] = _dot(d["t"], jnp.where(coupling, d["m"], 0.0))
        for d in grp:
            d["t"] = d["t"] + _dot(d["y"], d["t"])
    for d in grp:
        rhs = jnp.concatenate([stacked(d, 2 * Q_W, False) * d["bcol"],
                               stacked(d, Q_W, True) * (d["bcol"] * d["eg"])], axis=1)
        d["x"] = _dot(d["t"], rhs)

    for k in range(nck):
        now = [d for d in grp if d["k"] == k]
        for d in now:
            qe = stacked(d, 0, True) * d["eg"]
            d["xs"] = [_dot(jnp.concatenate([d["x"][hh * C:(hh + 1) * C, D_V:], qe[hh * C:(hh + 1) * C]], axis=0),
                            s_scr[d["s0"] + h]) for hh, h in enumerate(d["heads"])]
        for d in now:
            d["vnew"] = jnp.concatenate(
                [d["x"][hh * C:(hh + 1) * C, :D_V] - xs[:C] for hh, xs in enumerate(d["xs"])], axis=0)
            d["o"] = jnp.concatenate([xs[C:] for xs in d["xs"]], axis=0) + _dot(d["attn"], d["vnew"])
        for d in now:
            kt = stacked(d, Q_W, True) * jnp.exp(d["glast"] - d["gcol"])
            egl = jnp.exp(d["glast"])
            for hh, h in enumerate(d["heads"]):
                sl = slice(hh * C, (hh + 1) * C)
                if C >= 16:
                    upd = _dot_tn(kt[sl], d["vnew"][sl])
                else:
                    upd = _dot_tn(kt, jnp.where(row_head == hh, d["vnew"], 0.0))
                s_scr[d["s0"] + h] = s_scr[d["s0"] + h] * egl[hh * C:hh * C + 1] + upd

    for d in grp:
        o = d["o"]
        zst = jnp.concatenate([z_all[d["rows"], h * D_V:(h + 1) * D_V] for h in d["heads"]], axis=0)
        on = o * lax.rsqrt(jnp.mean(o * o, axis=-1, keepdims=True) + NORM_EPS) * gam * _silu(zst)
        for hh, h in enumerate(d["heads"]):
            o_ref[d["rows"], h * D_V:(h + 1) * D_V] = on[hh * C:(hh + 1) * C].astype(o_ref.dtype)

    @pl.when(c == pl.num_programs(1) - 1)
    def _fin():
        for s in range(nseq):
            sout_ref[s] = s_scr[s * H_V:(s + 1) * H_V]


def _gdn(proj, ba, conv_w, a_log, dt_bias, gamma_a, row0, B, L, C, nseq, nck, conv_buf=None, s0=None):
    has_state = s0 is not None
    hg = STACK_ROWS // C
    ng = H_V // hg
    nrow = 16
    nc = L // C
    R = STACK_ROWS
    assert nseq == 1 or nc == nck
    rows_blk = nseq * nck * C
    steps = nc // nck

    def arrange(v):
        v = v.reshape(B, nc, C, ng, hg).transpose(0, 1, 3, 4, 2).reshape(B, nc, ng, R)
        return jnp.pad(v, ((0, 0), (0, 0), (0, nrow - ng), (0, 0)))

    rows = ba[row0:row0 + B * L]
    ba_arr = jnp.concatenate([arrange(rows[:, :H_V]), arrange(rows[:, H_V:2 * H_V])], axis=2)

    def arrange_param(p):
        v = jnp.broadcast_to(p.reshape(ng, hg, 1), (ng, hg, C)).reshape(ng, R)
        return jnp.pad(v, ((0, nrow - ng), (0, 0)))

    ad = jnp.concatenate([arrange_param(a_log), arrange_param(dt_bias)], axis=0)
    rb0 = row0 // rows_blk
    in_specs = [
        pl.BlockSpec((rows_blk, QKV_W), lambda b, c: (rb0 + b * steps + c, 0)),
        pl.BlockSpec((rows_blk, V_W), lambda b, c: (rb0 + b * steps + c, Z_BLK)),
        pl.BlockSpec((nseq, nck, 2 * nrow, R), lambda b, c: (b, c, 0, 0)),
        pl.BlockSpec((CONV_W, QKV_W), lambda b, c: (0, 0)),
        pl.BlockSpec((2 * nrow, R), lambda b, c: (0, 0)),
        pl.BlockSpec((1, D_V), lambda b, c: (0, 0)),
    ]
    args = [proj, proj, ba_arr, conv_w, ad, gamma_a.reshape(1, D_V)]
    if has_state:
        in_specs += [pl.BlockSpec((nseq, CONV_W - 1, QKV_W), lambda b, c: (b, 0, 0)),
                     pl.BlockSpec((nseq, H_V, D_K, D_V), lambda b, c: (b, 0, 0, 0))]
        args += [conv_buf, s0]
    return pl.pallas_call(
        functools.partial(_gdn_kernel, C=C, hg=hg, nseq=nseq, nck=nck, has_state=has_state),
        grid=(B // nseq, steps),
        in_specs=in_specs,
        out_specs=[pl.BlockSpec((rows_blk, V_W), lambda b, c: (b * steps + c, 0)),
                   pl.BlockSpec((nseq, H_V, D_K, D_V), lambda b, c: (b, 0, 0, 0))],
        out_shape=[jax.ShapeDtypeStruct((B * L, V_W), BF16),
                   jax.ShapeDtypeStruct((B, H_V, D_K, D_V), F32)],
        scratch_shapes=[pltpu.VMEM((nseq * H_V, D_K, D_V), F32),
                        pltpu.VMEM((nseq * 8, QKV_W), F32),
                        pltpu.VMEM((rows_blk, QKV_W), F32)],
        compiler_params=_cparams(("parallel", "arbitrary")),
        name="gdn_state" if has_state else "gdn_fresh",
    )(*args)


def _pool_kernel(*refs, tc, nseq, start_pos, has_state):
    if has_state:
        u_ref, buf_ref, o_ref, tail_scr = refs
    else:
        u_ref, o_ref, tail_scr = refs
    hist = POOL_BUF + 1
    c = pl.program_id(1)

    @pl.when(c == 0)
    def _init():
        tail_scr[...] = jnp.zeros(tail_scr.shape, F32)
        if has_state:
            for s in range(nseq):
                tail_scr[s * hist + 1:(s + 1) * hist] = buf_ref[s]

    u_all = u_ref[...].astype(F32)
    pos1 = (start_pos + 1 + c * tc + lax.broadcasted_iota(jnp.int32, (tc, 1), 0)).astype(F32)
    seq_out = []
    for s in range(nseq):
        u = u_all[s * tc:(s + 1) * tc]
        xe = jnp.concatenate([tail_scr[s * hist:(s + 1) * hist], u], axis=0)
        tail_scr[s * hist:(s + 1) * hist] = xe[tc:tc + hist]
        parts = []
        for gi, win in enumerate(POOL_WINDOWS):
            sl = slice(gi * POOL_GROUP_W, (gi + 1) * POOL_GROUP_W)
            acc = xe[:, sl]
            shift = 1
            while shift < win:
                acc = acc + pltpu.roll(acc, shift, 0)
                shift *= 2
            cnt = jnp.minimum(float(win), pos1)
            parts.append(acc[hist:hist + tc] / cnt - u[:, sl])
        seq_out.append(jnp.concatenate(parts, axis=1))
    o_ref[...] = jnp.concatenate(seq_out, axis=0).astype(o_ref.dtype)


def _pool(proj, row0, B, L, tc, nseq, start_pos, pool_buf=None):
    has_state = pool_buf is not None
    nc = L // tc
    assert nseq == 1 or nc == 1
    rows_blk = nseq * tc
    rb0 = row0 // rows_blk
    in_specs = [pl.BlockSpec((rows_blk, D_MODEL), lambda b, c: (rb0 + b * nc + c, U_BLK))]
    args = [proj]
    if has_state:
        in_specs.append(pl.BlockSpec((nseq, POOL_BUF, D_MODEL), lambda b, c: (b, 0, 0)))
        args.append(pool_buf)
    return pl.pallas_call(
        functools.partial(_pool_kernel, tc=tc, nseq=nseq, start_pos=start_pos, has_state=has_state),
        grid=(B // nseq, nc),
        in_specs=in_specs,
        out_specs=pl.BlockSpec((rows_blk, D_MODEL), lambda b, c: (b * nc + c, 0)),
        out_shape=jax.ShapeDtypeStruct((B * L, D_MODEL), BF16),
        scratch_shapes=[pltpu.VMEM((nseq * (POOL_BUF + 1), D_MODEL), F32)],
        compiler_params=_cparams(("parallel", "arbitrary")),
        name="pool_state" if has_state else "pool_fresh",
    )(*args)


def _post_kernel(pa_ref, pb_ref, oa_ref, ob_ref, ga_ref, gb_ref, xa_ref, xb_ref, wp_ref, ps_ref, wbb_ref, wba_ref,
                 wo_ref, g_ref, b_ref, h_ref, hp_ref, *, alpha, tiles_a):
    first = pl.program_id(0) < tiles_a
    x = jnp.where(first, xa_ref[...], xb_ref[...])
    pooled = jnp.where(first, pa_ref[...], pb_ref[...])
    o_gated = jnp.where(first, oa_ref[...], ob_ref[...])
    mixed = jnp.concatenate(
        [_dot(pooled[:, gi * POOL_GROUP_W:(gi + 1) * POOL_GROUP_W], wp_ref[gi]) for gi in range(len(POOL_WINDOWS))],
        axis=1) * ps_ref[...]
    branch_b = _dot(mixed, wbb_ref[...])
    branch_a = _dot(o_gated, wba_ref[...])
    merged = _sigmoid(ga_ref[...].astype(F32)) * branch_a + _sigmoid(gb_ref[...].astype(F32)) * branch_b
    h = _layer_norm(alpha * x + _dot(merged, wo_ref[...]), g_ref[...], b_ref[...])
    h_ref[...] = h
    _store_pieces(hp_ref, h)


def _post(pooled_a, pooled_b, o_a, o_b, proj, x_a, x_b, w_pool, pool_scale, w_br_b, w_br_a, w_out, ln_g, ln_b,
          alpha, tm):
    t = x_a.shape[0] + x_b.shape[0]
    tiles_a = x_a.shape[0] // tm
    row = lambda i: (i, 0)
    row_a = lambda i: (jnp.minimum(i, tiles_a - 1), 0)
    row_b = lambda i: (jnp.maximum(i - tiles_a, 0), 0)
    const2 = lambda i: (0, 0)
    return pl.pallas_call(
        functools.partial(_post_kernel, alpha=alpha, tiles_a=tiles_a),
        grid=(t // tm,),
        in_specs=[pl.BlockSpec((tm, D_MODEL), row_a),
                  pl.BlockSpec((tm, D_MODEL), row_b),
                  pl.BlockSpec((tm, V_W), row_a),
                  pl.BlockSpec((tm, V_W), row_b),
                  pl.BlockSpec((tm, D_MODEL), lambda i: (i, U_BLK + 1)),
                  pl.BlockSpec((tm, D_MODEL), lambda i: (i, U_BLK + 2)),
                  pl.BlockSpec((tm, D_MODEL), row_a),
                  pl.BlockSpec((tm, D_MODEL), row_b),
                  pl.BlockSpec(w_pool.shape, lambda i: (0, 0, 0), pipeline_mode=pl.Buffered(1)),
                  pl.BlockSpec((1, D_MODEL), const2),
                  pl.BlockSpec(w_br_b.shape, const2, pipeline_mode=pl.Buffered(1)),
                  pl.BlockSpec(w_br_a.shape, const2, pipeline_mode=pl.Buffered(1)),
                  pl.BlockSpec(w_out.shape, const2, pipeline_mode=pl.Buffered(1)),
                  pl.BlockSpec((1, D_MODEL), const2),
                  pl.BlockSpec((1, D_MODEL), const2)],
        out_specs=[pl.BlockSpec((tm, D_MODEL), row), pl.BlockSpec((N_PIECE, tm, PIECE_W), lambda i: (0, i, 0))],
        out_shape=[jax.ShapeDtypeStruct((t, D_MODEL), F32),
                   jax.ShapeDtypeStruct((N_PIECE, t, PIECE_W), jnp.uint32)],
        compiler_params=_cparams(("parallel",)),
        name="post_mixers",
    )(pooled_a, pooled_b, o_a, o_b, proj, proj, x_a, x_b, w_pool, pool_scale, w_br_b, w_br_a, w_out, ln_g, ln_b)


def _router_kernel(h_ref, wt_ref, bias_ref, idx_ref, wts_ref, rank_ref, cnt_ref, carry_scr, *, tm):
    i = pl.program_id(0)

    @pl.when(i == 0)
    def _init():
        carry_scr[...] = jnp.zeros(carry_scr.shape, F32)

    logits = _dot3_nt(wt_ref[...], h_ref[...])
    sc = _sigmoid(logits)
    ch = sc + bias_ref[...]
    neg = -jnp.inf
    e_in = lax.broadcasted_iota(jnp.int32, (GROUP_SZ, tm), 0)
    gs_rows = []
    for g in range(N_GROUP):
        blk = ch[g * GROUP_SZ:(g + 1) * GROUP_SZ]
        m1 = jnp.max(blk, axis=0, keepdims=True)
        i1 = jnp.min(jnp.where(blk == m1, e_in, GROUP_SZ), axis=0, keepdims=True)
        m2 = jnp.max(jnp.where(e_in == i1, neg, blk), axis=0, keepdims=True)
        gs_rows.append(m1 + m2)
    gs = jnp.concatenate(gs_rows, axis=0)
    g_io = lax.broadcasted_iota(jnp.int32, (N_GROUP, tm), 0)
    e_io = lax.broadcasted_iota(jnp.int32, (N_EXPERTS, tm), 0)
    e_grp = e_io >> int(math.log2(GROUP_SZ))
    masked = jnp.full((N_EXPERTS, tm), neg, F32)
    for _ in range(TOPK_GROUP):
        mx = jnp.max(gs, axis=0, keepdims=True)
        gi = jnp.min(jnp.where(gs == mx, g_io, N_GROUP), axis=0, keepdims=True)
        gs = jnp.where(g_io == gi, neg, gs)
        masked = jnp.where(e_grp == gi, ch, masked)
    idx_rows, w_rows = [], []
    onehot = jnp.zeros((N_EXPERTS, tm), F32)
    for _ in range(TOP_K):
        mx = jnp.max(masked, axis=0, keepdims=True)
        ei = jnp.min(jnp.where(masked == mx, e_io, N_EXPERTS), axis=0, keepdims=True)
        hit = e_io == ei
        idx_rows.append(ei)
        w_rows.append(jnp.sum(jnp.where(hit, sc, 0.0), axis=0, keepdims=True))
        onehot = jnp.where(hit, 1.0, onehot)
        masked = jnp.where(hit, neg, masked)
    wsel = jnp.concatenate(w_rows, axis=0)
    wts_ref[...] = wsel / jnp.sum(wsel, axis=0, keepdims=True) * ROUTED_SCALE
    idx_ref[...] = jnp.concatenate(idx_rows, axis=0)

    tr = lax.broadcasted_iota(jnp.int32, (tm, tm), 0)
    tc_ = lax.broadcasted_iota(jnp.int32, (tm, tm), 1)
    before = jnp.where(tr < tc_, 1.0, 0.0).astype(BF16)
    cum = jnp.dot(onehot.astype(BF16), before, preferred_element_type=F32) + carry_scr[...]
    rank_ref[...] = jnp.concatenate(
        [jnp.sum(jnp.where(e_io == ei, cum, 0.0), axis=0, keepdims=True) for ei in idx_rows],
        axis=0).astype(jnp.int32)
    carry_scr[...] = carry_scr[...] + jnp.sum(onehot, axis=1, keepdims=True)
    cnt_ref[...] = carry_scr[...]


def _router(h, w_router_t, bias_col, tm):
    t = h.shape[0]
    return pl.pallas_call(
        functools.partial(_router_kernel, tm=tm),
        grid=(t // tm,),
        in_specs=[pl.BlockSpec((tm, D_MODEL), lambda i: (i, 0)),
                  pl.BlockSpec((N_EXPERTS, D_MODEL), lambda i: (0, 0)),
                  pl.BlockSpec((N_EXPERTS, 1), lambda i: (0, 0))],
        out_specs=[pl.BlockSpec((TOP_K, tm), lambda i: (0, i)),
                   pl.BlockSpec((TOP_K, tm), lambda i: (0, i)),
                   pl.BlockSpec((TOP_K, tm), lambda i: (0, i)),
                   pl.BlockSpec((N_EXPERTS, 1), lambda i: (0, 0))],
        out_shape=[jax.ShapeDtypeStruct((TOP_K, t), jnp.int32),
                   jax.ShapeDtypeStruct((TOP_K, t), F32),
                   jax.ShapeDtypeStruct((TOP_K, t), jnp.int32),
                   jax.ShapeDtypeStruct((N_EXPERTS, 1), F32)],
        scratch_shapes=[pltpu.VMEM((N_EXPERTS, 1), F32)],
        compiler_params=_cparams(("arbitrary",)),
        name="router",
    )(h, w_router_t, bias_col)


def _dest_kernel(idx_ref, rank_ref, start_ref, dest_ref, *, tm, n_rows):
    e_io = lax.broadcasted_iota(jnp.int32, (N_EXPERTS, tm), 0)
    starts = start_ref[...]
    rows = []
    for k in range(TOP_K):
        seg = jnp.sum(jnp.where(e_io == idx_ref[k:k + 1, :], starts, 0.0), axis=0, keepdims=True)
        rows.append(seg.astype(jnp.int32) + rank_ref[k:k + 1, :])
    base = jnp.concatenate(rows, axis=0)
    for p in range(N_PIECE):
        dest_ref[p] = base + p * n_rows


def _dest(idx_t, rank_t, seg_start_col, n_rows, tm):
    t = idx_t.shape[1]
    blk = pl.BlockSpec((TOP_K, tm), lambda i: (0, i))
    return pl.pallas_call(
        functools.partial(_dest_kernel, tm=tm, n_rows=n_rows),
        grid=(t // tm,),
        in_specs=[blk, blk, pl.BlockSpec((N_EXPERTS, 1), lambda i: (0, 0))],
        out_specs=pl.BlockSpec((N_PIECE, TOP_K, tm), lambda i: (0, 0, i)),
        out_shape=jax.ShapeDtypeStruct((N_PIECE, TOP_K, t), jnp.int32),
        compiler_params=_cparams(("parallel",)),
        name="dispatch_rows",
    )(idx_t, rank_t, seg_start_col)


def _expert_kernel(be_ref, nu_ref, first_ref, nxt_ref, slot_ref, x_ref, wg_hbm, wu_hbm, wd_hbm, y_ref,
                   wg_buf, wu_buf, wd_buf, wgu_scr, wd_scr, sems):
    i = pl.program_id(0)
    live = i < nu_ref[0]

    def fetch(e, slot):
        return [pltpu.make_async_copy(wg_hbm.at[e], wg_buf.at[slot], sems.at[slot, 0]),
                pltpu.make_async_copy(wu_hbm.at[e], wu_buf.at[slot], sems.at[slot, 1]),
                pltpu.make_async_copy(wd_hbm.at[e], wd_buf.at[slot], sems.at[slot, 2])]

    @pl.when(live & (first_ref[i] == 1))
    def _new_expert():
        slot = slot_ref[i]

        @pl.when(i == 0)
        def _():
            for cp in fetch(be_ref[i], slot):
                cp.start()

        @pl.when(nxt_ref[i] >= 0)
        def _():
            for cp in fetch(nxt_ref[i], 1 - slot):
                cp.start()

        for cp in fetch(be_ref[i], slot):
            cp.wait()
        wgu_scr[:, :D_EXPERT] = wg_buf[slot].astype(BF16)
        wgu_scr[:, D_EXPERT:] = wu_buf[slot].astype(BF16)
        wd_scr[...] = wd_buf[slot].astype(BF16)

    @pl.when(live)
    def _():
        n_sub = 2
        sub = EXPERT_BM // n_sub
        xs = [jnp.concatenate([c.astype(BF16) for c in
                               _load_pieces([x_ref[p, s * sub:(s + 1) * sub] for p in range(N_PIECE)])], axis=1)
              for s in range(n_sub)]
        gus = [jnp.dot(x, wgu_scr[...], preferred_element_type=F32) for x in xs]
        acts = [(_silu(gu[:, :D_EXPERT]) * gu[:, D_EXPERT:]).astype(BF16) for gu in gus]
        ys = [jnp.dot(a, wd_scr[...], preferred_element_type=F32) for a in acts]
        for s, y in enumerate(ys):
            for p in range(N_PIECE):
                y_ref[p, s * sub:(s + 1) * sub] = _pack_halves(y[:, 2 * p * PIECE_W:2 * (p + 1) * PIECE_W])


def _experts(xs, blk_exp, n_used, blk_first, blk_next, blk_slot, w_gate, w_up, w_down):
    n_rows = xs.shape[1]
    n_blocks = n_rows // EXPERT_BM

    def row_block(i, be, nu, *_):
        return (0, jnp.minimum(i, nu[0] - 1), 0)

    grid_spec = pltpu.PrefetchScalarGridSpec(
        num_scalar_prefetch=5,
        grid=(n_blocks,),
        in_specs=[pl.BlockSpec((N_PIECE, EXPERT_BM, PIECE_W), row_block),
                  pl.BlockSpec(memory_space=pl.ANY),
                  pl.BlockSpec(memory_space=pl.ANY),
                  pl.BlockSpec(memory_space=pl.ANY)],
        out_specs=pl.BlockSpec((N_PIECE, EXPERT_BM, PIECE_W), row_block),
        scratch_shapes=[pltpu.VMEM((2, D_MODEL, D_EXPERT), F32),
                        pltpu.VMEM((2, D_MODEL, D_EXPERT), F32),
                        pltpu.VMEM((2, D_EXPERT, D_MODEL), F32),
                        pltpu.VMEM((D_MODEL, 2 * D_EXPERT), BF16),
                        pltpu.VMEM((D_EXPERT, D_MODEL), BF16),
                        pltpu.SemaphoreType.DMA((2, 3))],
    )
    return pl.pallas_call(
        _expert_kernel,
        grid_spec=grid_spec,
        out_shape=jax.ShapeDtypeStruct((N_PIECE, n_rows, PIECE_W), jnp.uint32),
        compiler_params=_cparams(("arbitrary",)),
        name="experts",
    )(blk_exp, n_used, blk_first, blk_next, blk_slot, xs, w_gate, w_up, w_down)


SC_WINDOW = 128
V7X_SC_CORES = 2
V7X_SC_SUBCORES = 16


def _sc_mesh():
    return plsc.VectorSubcoreMesh(core_axis_name="core", subcore_axis_name="subcore",
                                  num_cores=V7X_SC_CORES, num_subcores=V7X_SC_SUBCORES)


def _sc_scatter_rows(src, dest, n_rows, seg, repeat):
    d = src.shape[1]
    n_idx = dest.shape[0]
    seg_blocks = seg // SC_WINDOW
    dest2 = dest.reshape(1, n_idx)

    def src_block(i):
        return ((i // (repeat * seg_blocks)) * seg_blocks + i % seg_blocks, 0)

    @functools.partial(pl.kernel, out_type=jax.ShapeDtypeStruct((n_rows, d), src.dtype), mesh=_sc_mesh(),
                       scratch_types=[], name="sc_dispatch")
    def run(src_hbm, idx_hbm, out_hbm):
        def body(rows_vmem, idx_vmem):
            pltpu.sync_copy(rows_vmem, out_hbm.at[idx_vmem.at[0]])

        pltpu.emit_pipeline(
            body,
            grid=(n_idx // SC_WINDOW,),
            in_specs=[pl.BlockSpec((SC_WINDOW, d), src_block),
                      pl.BlockSpec((1, SC_WINDOW), lambda i: (0, i))],
            out_specs=[],
            core_axis_name=("core", "subcore"),
            dimension_semantics=(pltpu.PARALLEL,),
        )(src_hbm, idx_hbm)

    return run(src, dest2)


def _sc_gather_rows(table, idx):
    d = table.shape[1]
    n_idx = idx.shape[0]
    idx2 = idx.reshape(1, n_idx)

    @functools.partial(pl.kernel, out_type=jax.ShapeDtypeStruct((n_idx, d), table.dtype), mesh=_sc_mesh(),
                       scratch_types=[], name="sc_combine_gather")
    def run(table_hbm, idx_hbm, out_hbm):
        def body(idx_vmem, rows_vmem):
            pltpu.sync_copy(table_hbm.at[idx_vmem.at[0]], rows_vmem)

        pltpu.emit_pipeline(
            body,
            grid=(n_idx // SC_WINDOW,),
            in_specs=[pl.BlockSpec((1, SC_WINDOW), lambda i: (0, i))],
            out_specs=[pl.BlockSpec((SC_WINDOW, d), lambda i: (i, 0))],
            core_axis_name=("core", "subcore"),
            dimension_semantics=(pltpu.PARALLEL,),
        )(idx_hbm, out_hbm)

    return run(table, idx2)


def _combine_kernel(yg_ref, wts_ref, h_ref, wgu_ref, wd_ref, g_ref, b_ref, outa_ref, outb_ref, *, alpha, tiles_a):
    wts = wts_ref[...]
    acc = None
    for k in range(TOP_K):
        cols = _load_pieces([yg_ref[p, k] for p in range(N_PIECE)])
        wk = wts[:, k:k + 1]
        acc = [c * wk for c in cols] if acc is None else [a + c * wk for a, c in zip(acc, cols)]
    routed = jnp.concatenate(acc, axis=1)
    h = h_ref[...]
    gu = _dot(h, wgu_ref[...])
    shared = _dot(_silu(gu[:, :D_EXPERT]) * gu[:, D_EXPERT:], wd_ref[...])
    y = _layer_norm(alpha * h + (routed + shared), g_ref[...], b_ref[...])
    i = pl.program_id(0)

    @pl.when(i < tiles_a)
    def _():
        outa_ref[...] = y

    @pl.when(i >= tiles_a)
    def _():
        outb_ref[...] = y


def _combine(yg, wts, h, w_sh_gu, w_sh_down, ln_g, ln_b, alpha, tm, t_a):
    t = h.shape[0]
    tiles_a = t_a // tm
    row = lambda i: (i, 0)
    const2 = lambda i: (0, 0)
    return pl.pallas_call(
        functools.partial(_combine_kernel, alpha=alpha, tiles_a=tiles_a),
        grid=(t // tm,),
        in_specs=[pl.BlockSpec((N_PIECE, TOP_K, tm, PIECE_W), lambda i: (0, 0, i, 0)),
                  pl.BlockSpec((tm, TOP_K), row),
                  pl.BlockSpec((tm, D_MODEL), row),
                  pl.BlockSpec(w_sh_gu.shape, const2),
                  pl.BlockSpec(w_sh_down.shape, const2),
                  pl.BlockSpec((1, D_MODEL), const2),
                  pl.BlockSpec((1, D_MODEL), const2)],
        out_specs=[pl.BlockSpec((tm, D_MODEL), lambda i: (jnp.minimum(i, tiles_a - 1), 0)),
                   pl.BlockSpec((tm, D_MODEL), lambda i: (jnp.maximum(i - tiles_a, 0), 0))],
        out_shape=[jax.ShapeDtypeStruct((t_a, D_MODEL), F32), jax.ShapeDtypeStruct((t - t_a, D_MODEL), F32)],
        compiler_params=_cparams(("arbitrary",)),
        name="combine_ln2",
    )(yg, wts, h, w_sh_gu, w_sh_down, ln_g, ln_b)


def _layer(xp, xs, s_delta, s_conv, s_pool, w_in, conv_w, a_log, dt_bias, gamma_a, w_br_a, w_pool,
           pool_scale, w_br_b, w_out, ln1_g, ln1_b, w_router, router_bias, w_exp_gate, w_exp_up,
           w_exp_down, w_sh_gate, w_sh_up, w_sh_down, ln2_g, ln2_b, alpha):
    Bp, Lp, _ = xp.shape
    Bs, Ls, _ = xs.shape
    Tp, Ts = Bp * Lp, Bs * Ls
    T = Tp + Ts
    x_p = xp.reshape(Tp, D_MODEL)
    x_s = xs.reshape(Ts, D_MODEL)

    o_b, o_u = QKV_W + V_W, QKV_W + V_W + 2 * H_V
    w_main = jnp.concatenate([w_in[:, :o_b], w_in[:, o_u:]], axis=1).astype(BF16)
    w_ba = jnp.pad(w_in[:, o_b:o_u], ((0, 0), (0, 128 - 2 * H_V))).astype(BF16)
    proj, ba = _in_proj(x_p, x_s, w_main, w_ba, IN_PROJ_TM, IN_PROJ_TN)

    o_p, sd_p = _gdn(proj, ba, conv_w, a_log, dt_bias, gamma_a, 0, Bp, Lp, GDN_CHUNK, 1, GDN_FRESH_CHUNKS)
    o_s, sd_s = _gdn(proj, ba, conv_w, a_log, dt_bias, gamma_a, Tp, Bs, Ls, Ls, GDN_STATE_SEQS, 1,
                     conv_buf=s_conv, s0=s_delta)

    pooled_p = _pool(proj, 0, Bp, Lp, POOL_TILE, 1, 0)
    pooled_s = _pool(proj, Tp, Bs, Ls, Ls, POOL_STATE_SEQS, PAST_LEN, pool_buf=s_pool)

    h, hp = _post(pooled_p, pooled_s, o_p, o_s, proj, x_p, x_s, w_pool.astype(BF16), pool_scale.reshape(1, D_MODEL),
                  w_br_b.astype(BF16), w_br_a.astype(BF16), w_out.astype(BF16),
                  ln1_g.reshape(1, D_MODEL), ln1_b.reshape(1, D_MODEL), alpha, POST_TM)

    idx_t, wts_t, rank_t, cnt = _router(h, w_router.T, router_bias.reshape(N_EXPERTS, 1), ROUTER_TM)
    counts = cnt[:, 0].astype(jnp.int32)
    padded = ((counts + EXPERT_BM - 1) // EXPERT_BM) * EXPERT_BM
    pends = jnp.cumsum(padded)
    pstarts = pends - padded
    n_blocks = (T * TOP_K + N_EXPERTS * (EXPERT_BM - 1) + EXPERT_BM - 1) // EXPERT_BM
    n_rows = n_blocks * EXPERT_BM
    piece_rows = _dest(idx_t, rank_t, pstarts.astype(F32).reshape(N_EXPERTS, 1), n_rows, ROUTER_TM)
    blk_start = jnp.arange(n_blocks, dtype=jnp.int32) * EXPERT_BM
    blk_exp = jnp.minimum(jnp.sum((pends[None, :] <= blk_start[:, None]).astype(jnp.int32), axis=1),
                          N_EXPERTS - 1)
    n_used = (pends[-1:] // EXPERT_BM).astype(jnp.int32)
    has_rows = counts > 0
    e_ids = jnp.arange(N_EXPERTS, dtype=jnp.int32)
    slot_of = (jnp.cumsum(has_rows.astype(jnp.int32)) - 1) & 1
    later = jnp.where(has_rows, e_ids, N_EXPERTS)
    next_of = jnp.concatenate([lax.cummin(later, reverse=True)[1:], jnp.full((1,), N_EXPERTS, jnp.int32)])
    next_of = jnp.where(next_of < N_EXPERTS, next_of, -1)
    blk_first = (blk_start == pstarts[blk_exp]).astype(jnp.int32)
    blk_next = next_of[blk_exp]
    blk_slot = slot_of[blk_exp]

    piece_idx = piece_rows.reshape(N_PIECE * TOP_K * T)
    x_sorted = _sc_scatter_rows(hp.reshape(N_PIECE * T, PIECE_W), piece_idx, N_PIECE * n_rows, T, TOP_K)
    y_sorted = _experts(x_sorted.reshape(N_PIECE, n_rows, PIECE_W), blk_exp, n_used, blk_first, blk_next,
                        blk_slot, w_exp_gate, w_exp_up, w_exp_down)
    yg = _sc_gather_rows(y_sorted.reshape(N_PIECE * n_rows, PIECE_W), piece_idx)
    yg = yg.reshape(N_PIECE, TOP_K, T, PIECE_W)

    w_sh_gu = jnp.concatenate([w_sh_gate, w_sh_up], axis=1).astype(BF16)
    y_p, y_s = _combine(yg, wts_t.T, h, w_sh_gu, w_sh_down.astype(BF16),
                        ln2_g.reshape(1, D_MODEL), ln2_b.reshape(1, D_MODEL), alpha, COMBINE_TM, Tp)

    def carried_rows(keep, col0, width, name, old_p, old_s):
        parts, counts = [], []
        for x2d, B, L in ((x_p, Bp, Lp), (x_s, Bs, Ls)):
            n = min(L, keep)
            rows = jnp.arange(B, dtype=jnp.int32)[:, None] * L + (L - n) + jnp.arange(n, dtype=jnp.int32)
            parts.append(jnp.take(x2d, rows.reshape(-1), axis=0).astype(BF16))
            counts.append((B, n))
        n_tail = sum(B * n for B, n in counts)
        n_pad = -(-n_tail // 128) * 128
        xt = jnp.pad(jnp.concatenate(parts, axis=0), ((0, n_pad - n_tail), (0, 0)))
        tail = _matmul(xt, w_main, col0, width, n_pad, 1024, F32, name)
        outs, r0 = [], 0
        for (B, n), old in zip(counts, (old_p, old_s)):
            new = tail[r0:r0 + B * n].reshape(B, n, width)
            outs.append(new if n == keep else jnp.concatenate([old[:, n:], new], axis=1))
            r0 += B * n
        return outs

    new_conv_p, new_conv_s = carried_rows(CONV_W - 1, 0, QKV_W, "in_proj_conv_rows",
                                          jnp.zeros((Bp, CONV_W - 1, QKV_W), F32), s_conv)
    new_pool_p, new_pool_s = carried_rows(POOL_BUF, U_BLK * D_MODEL, D_MODEL, "in_proj_pool_rows",
                                          jnp.zeros((Bp, POOL_BUF, D_MODEL), F32), s_pool)
    return (y_p.reshape(Bp, Lp, D_MODEL), y_s.reshape(Bs, Ls, D_MODEL),
            sd_p, new_conv_p, new_pool_p, sd_s, new_conv_s, new_pool_s)


def kernel(x_prompt, x_sample, state_delta, state_conv, state_pool, w_in, conv_w, a_log, dt_bias, gamma_a,
           w_br_a, w_pool, pool_scale, w_br_b, w_out, ln1_g, ln1_b, w_router, router_bias,
           w_exp_gate, w_exp_up, w_exp_down, w_sh_gate, w_sh_up, w_sh_down, ln2_g, ln2_b):
    depth = w_in.shape[0]
    alpha = (2 * depth) ** 0.25
    yp, ys = x_prompt, x_sample
    outs = [[] for _ in range(6)]
    for l in range(depth):
        res = _layer(yp, ys, state_delta[l], state_conv[l], state_pool[l], w_in[l], conv_w[l], a_log[l],
                     dt_bias[l], gamma_a[l], w_br_a[l], w_pool[l], pool_scale[l], w_br_b[l], w_out[l],
                     ln1_g[l], ln1_b[l], w_router[l], router_bias[l], w_exp_gate[l], w_exp_up[l],
                     w_exp_down[l], w_sh_gate[l], w_sh_up[l], w_sh_down[l], ln2_g[l], ln2_b[l], alpha)
        yp, ys = res[0], res[1]
        for lst, v in zip(outs, res[2:]):
            lst.append(v)
    return (yp, ys) + tuple(jnp.stack(v) for v in outs)
```

```python
import functools
import math

import jax
import jax.numpy as jnp
from jax import lax
from jax.experimental import pallas as pl
from jax.experimental.pallas import tpu as pltpu
from jax.experimental.pallas import tpu_sc as plsc

F32 = jnp.float32
BF16 = jnp.bfloat16

D_MODEL = 1024
H_K = 8
D_K = 128
H_V = 16
D_V = 128
Q_W = H_K * D_K
V_W = H_V * D_V
QKV_W = 2 * Q_W + V_W
CONV_W = 4
POOL_WINDOWS = (2, 4, 8, 16)
POOL_GROUP_W = D_MODEL // len(POOL_WINDOWS)
POOL_BUF = max(POOL_WINDOWS) - 1
N_EXPERTS = 256
TOP_K = 8
N_GROUP = 8
TOPK_GROUP = 4
GROUP_SZ = N_EXPERTS // N_GROUP
D_EXPERT = D_MODEL // 4
ROUTED_SCALE = 2.5
LN_EPS = 1e-5
NORM_EPS = 1e-6
PAST_LEN = 16384

MAIN_W = QKV_W + V_W + 3 * D_MODEL
Z_BLK = QKV_W // V_W
U_BLK = (QKV_W + V_W) // D_MODEL

IN_PROJ_TM = 1024
IN_PROJ_TN = MAIN_W // 3
GDN_CHUNK = 64
POOL_TILE = 1024
POST_TM = 512
ROUTER_TM = 512
COMBINE_TM = 512

STACK_ROWS = 128
SOLVE_BASE = 16
GDN_FRESH_CHUNKS = 4
EXPERT_BM = 512
GDN_STATE_SEQS = 8
POOL_STATE_SEQS = 16
VMEM_LIMIT = 56 * 1024 * 1024


def _cparams(sem):
    return pltpu.CompilerParams(dimension_semantics=sem, vmem_limit_bytes=VMEM_LIMIT)


def _sigmoid(x):
    return 0.5 * jnp.tanh(0.5 * x) + 0.5


def _silu(x):
    return x * _sigmoid(x)


def _softplus(x):
    return jnp.maximum(x, 0.0) + jnp.log(1.0 + jnp.exp(-jnp.abs(x)))


def _dot(a, b):
    return jnp.dot(a.astype(BF16), b.astype(BF16), preferred_element_type=F32)


def _dot_nt(a, b):
    return lax.dot_general(a.astype(BF16), b.astype(BF16), (((1,), (1,)), ((), ())),
                           preferred_element_type=F32)


def _dot_tn(a, b):
    return lax.dot_general(a.astype(BF16), b.astype(BF16), (((0,), (0,)), ((), ())),
                           preferred_element_type=F32)


def _split(a):
    hi = a.astype(BF16)
    lo = (a - hi.astype(F32)).astype(BF16)
    return hi, lo


def _dot3_nt(a, b):
    ah, al = _split(a)
    bh, bl = _split(b)
    d = functools.partial(lax.dot_general, dimension_numbers=(((1,), (1,)), ((), ())),
                          preferred_element_type=F32)
    return d(ah, bh) + (d(ah, bl) + d(al, bh))


def _pack_halves(x):
    n = x.shape[1] // 2
    hi = lax.bitcast_convert_type(x[:, :n].astype(BF16).astype(F32), jnp.uint32)
    lo = lax.bitcast_convert_type(x[:, n:].astype(BF16).astype(F32), jnp.uint32)
    return (hi & jnp.uint32(0xFFFF0000)) | (lo >> 16)


def _unpack_halves(w):
    hi = lax.bitcast_convert_type(w & jnp.uint32(0xFFFF0000), F32)
    lo = lax.bitcast_convert_type(w << 16, F32)
    return hi, lo


N_PIECE = 2
PIECE_W = D_MODEL // (2 * N_PIECE)


def _store_pieces(ref, x):
    for p in range(N_PIECE):
        ref[p] = _pack_halves(x[:, 2 * p * PIECE_W:2 * (p + 1) * PIECE_W])


def _load_pieces(pieces):
    cols = []
    for w in pieces:
        cols.extend(_unpack_halves(w))
    return cols


def _layer_norm(x, g, b):
    mu = jnp.mean(x, axis=-1, keepdims=True)
    xc = x - mu
    var = jnp.mean(xc * xc, axis=-1, keepdims=True)
    return xc * lax.rsqrt(var + LN_EPS) * g + b


def _in_proj_kernel(xa_ref, xb_ref, w_ref, wba_ref, o_ref, ba_ref, x_scr, *, tiles_a):
    @pl.when(pl.program_id(1) == 0)
    def _():
        x = jnp.where(pl.program_id(0) < tiles_a, xa_ref[...], xb_ref[...]).astype(BF16)
        x_scr[...] = x
        ba_ref[...] = jnp.dot(x, wba_ref[...], preferred_element_type=F32)

    o_ref[...] = jnp.dot(x_scr[...], w_ref[...], preferred_element_type=F32).astype(o_ref.dtype)


def _in_proj(x_a, x_b, w_main, w_ba, tm, tn):
    k = x_a.shape[1]
    t = x_a.shape[0] + x_b.shape[0]
    n = w_main.shape[1]
    tiles_a = x_a.shape[0] // tm
    return pl.pallas_call(
        functools.partial(_in_proj_kernel, tiles_a=tiles_a),
        grid=(t // tm, n // tn),
        in_specs=[pl.BlockSpec((tm, k), lambda i, j: (jnp.minimum(i, tiles_a - 1), 0)),
                  pl.BlockSpec((tm, k), lambda i, j: (jnp.maximum(i - tiles_a, 0), 0)),
                  pl.BlockSpec((k, tn), lambda i, j: (0, j)),
                  pl.BlockSpec(w_ba.shape, lambda i, j: (0, 0))],
        out_specs=[pl.BlockSpec((tm, tn), lambda i, j: (i, j)),
                   pl.BlockSpec((tm, w_ba.shape[1]), lambda i, j: (i, 0))],
        out_shape=[jax.ShapeDtypeStruct((t, n), BF16), jax.ShapeDtypeStruct((t, w_ba.shape[1]), F32)],
        scratch_shapes=[pltpu.VMEM((tm, k), BF16)],
        compiler_params=_cparams(("parallel", "arbitrary")),
        name="in_proj",
    )(x_a, x_b, w_main, w_ba)


def _mm_kernel(x_ref, w_ref, o_ref):
    o_ref[...] = jnp.dot(x_ref[...], w_ref[...], preferred_element_type=F32).astype(o_ref.dtype)


def _matmul(x, w, col0, n, tm, tn, out_dtype, name):
    t, k = x.shape
    cb0 = col0 // tn
    return pl.pallas_call(
        _mm_kernel,
        grid=(t // tm, n // tn),
        in_specs=[pl.BlockSpec((tm, k), lambda i, j: (i, 0)),
                  pl.BlockSpec((k, tn), lambda i, j: (0, cb0 + j))],
        out_specs=pl.BlockSpec((tm, tn), lambda i, j: (i, j)),
        out_shape=jax.ShapeDtypeStruct((t, n), out_dtype),
        compiler_params=_cparams(("parallel", "parallel")),
        name=name,
    )(x, w)


def _gdn_kernel(*refs, C, hg, nseq, nck, has_state):
    if has_state:
        (qkv_ref, z_ref, ba_ref, cw_ref, ad_ref, gam_ref, buf_ref, s0_ref,
         o_ref, sout_ref, s_scr, tail_scr, act_scr) = refs
    else:
        (qkv_ref, z_ref, ba_ref, cw_ref, ad_ref, gam_ref,
         o_ref, sout_ref, s_scr, tail_scr, act_scr) = refs
    R = hg * C
    ng = H_V // hg
    log2c = int(math.log2(C))
    c = pl.program_id(1)

    @pl.when(c == 0)
    def _init():
        if has_state:
            tail_scr[...] = jnp.zeros(tail_scr.shape, F32)
            for s in range(nseq):
                s_scr[s * H_V:(s + 1) * H_V] = s0_ref[s]
                tail_scr[(s + 1) * 8 - (CONV_W - 1):(s + 1) * 8] = buf_ref[s]
        else:
            s_scr[...] = jnp.zeros(s_scr.shape, F32)
            tail_scr[...] = jnp.zeros(tail_scr.shape, F32)

    cw = cw_ref[...]
    qkv_all = qkv_ref[...].astype(F32)
    n_prev = CONV_W - 1
    CB = nck * C
    for s in range(nseq):
        x = qkv_all[s * CB:(s + 1) * CB]
        xe = jnp.concatenate([tail_scr[s * 8:(s + 1) * 8], x], axis=0)
        y = x * cw[n_prev:CONV_W]
        for j in range(n_prev):
            y = y + xe[8 - n_prev + j:8 - n_prev + j + CB] * cw[j:j + 1]
        tail_scr[s * 8:(s + 1) * 8] = x[CB - 8:CB]
        act_scr[s * CB:(s + 1) * CB] = _silu(y)
        for off, scale in ((0, D_K ** -0.5), (Q_W, 1.0)):
            for kh in range(H_K):
                cols = slice(off + kh * D_K, off + (kh + 1) * D_K)
                v = act_scr[s * CB:(s + 1) * CB, cols]
                act_scr[s * CB:(s + 1) * CB, cols] = v * (
                    lax.rsqrt(jnp.sum(v * v, axis=-1, keepdims=True) + NORM_EPS) * scale)

    ri = lax.broadcasted_iota(jnp.int32, (R, R), 0)
    ci = lax.broadcasted_iota(jnp.int32, (R, R), 1)
    same = (ri >> log2c) == (ci >> log2c)
    incl = same & (ri >= ci)
    strict = same & (ri > ci)
    eye = ri == ci
    is_last = ci == (((ri >> log2c) << log2c) + (C - 1))
    eye_f = jnp.where(eye, 1.0, 0.0).astype(F32)
    cum_u = jnp.where(same & (ri <= ci), 1.0, 0.0).astype(BF16)

    nrow = ba_ref.shape[2] // 2
    adv = ad_ref[...]
    dd = functools.partial(jnp.dot, preferred_element_type=F32)
    beta_of, G_of = {}, {}
    for s in range(nseq):
        for k in range(nck):
            bav = ba_ref[s, k]
            beta_of[s, k] = _sigmoid(bav[0:nrow])
            g_rows = -jnp.exp(adv[0:nrow]) * _softplus(bav[nrow:2 * nrow] + adv[nrow:2 * nrow])
            g1 = g_rows.astype(BF16)
            r1 = g_rows - g1.astype(F32)
            g2 = r1.astype(BF16)
            g3 = (r1 - g2.astype(F32)).astype(BF16)
            G_of[s, k] = dd(g1, cum_u) + (dd(g2, cum_u) + dd(g3, cum_u))

    gam = gam_ref[...]
    rep = H_V // H_K

    def stacked(d, off, per_k_head):
        blocks = [(h // rep if per_k_head else h) for h in d["heads"]]
        return jnp.concatenate([act_scr[d["rows"], off + b * D_K:off + (b + 1) * D_K] for b in blocks], axis=0)

    log2b = min(log2c, int(math.log2(SOLVE_BASE)))
    same_base = ((ri ^ ci) >> log2b) == 0
    row_head = lax.broadcasted_iota(jnp.int32, (R, D_V), 0) >> log2c
    z_all = z_ref[...].astype(F32)
    grp = [dict(s=s, k=k, g=g, rows=slice(s * CB + k * C, s * CB + (k + 1) * C), s0=s * H_V,
                heads=[g * hg + hh for hh in range(hg)])
           for k in range(nck) for s in range(nseq) for g in range(ng)]
    for d in grp:
        kst = stacked(d, Q_W, True)
        d["kk"] = _dot_nt(kst, kst)
        d["qk"] = _dot_nt(stacked(d, 0, True), kst)
    for d in grp:
        grow = G_of[d["s"], d["k"]][d["g"]:d["g"] + 1]
        brow = beta_of[d["s"], d["k"]][d["g"]:d["g"] + 1]
        gcol = jnp.sum(eye_f * grow, axis=1, keepdims=True)
        bcol = jnp.sum(eye_f * brow, axis=1, keepdims=True)
        glast = jnp.sum(jnp.where(is_last, grow, 0.0), axis=1, keepdims=True)
        dm = jnp.where(incl, jnp.exp(jnp.minimum(gcol - grow, 0.0)), 0.0)
        d.update(gcol=gcol, bcol=bcol, glast=glast, eg=jnp.exp(gcol))
        d["m"] = jnp.where(strict, -(d["kk"] * bcol * dm), 0.0)
        d["attn"] = jnp.where(incl, d["qk"] * dm, 0.0)
        d["pw"] = jnp.where(same_base, d["m"], 0.0)
        d["t"] = eye_f + d["pw"]

    for r in range(log2b):
        last = r == log2b - 1
        for d in grp:
            if r == 0:
                if not last:
                    d["pw"] = _dot(d["pw"], d["pw"])
            elif last:
                d["t"] = d["t"] + _dot(d["pw"], d["t"])
            else:
                out = _dot(d["pw"], jnp.concatenate([d["pw"], d["t"]], axis=1))
                d["pw"] = out[:, :R]
                d["t"] = d["t"] + out[:, R:]
    for lev in range(log2b + 1, log2c + 1):
        coupling = ((ri ^ ci) >> (lev - 1)) == 1
        for d in grp:
            d["y"] = _dot(d["t"], jnp.where(coupling, d["m"], 0.0))
        for d in grp:
            d["t"] = d["t"] + _dot(d["y"], d["t"])
    for d in grp:
        rhs = jnp.concatenate([stacked(d, 2 * Q_W, False) * d["bcol"],
                               stacked(d, Q_W, True) * (d["bcol"] * d["eg"])], axis=1)
        d["x"] = _dot(d["t"], rhs)

    for k in range(nck):
        now = [d for d in grp if d["k"] == k]
        for d in now:
            qe = stacked(d, 0, True) * d["eg"]
            d["xs"] = [_dot(jnp.concatenate([d["x"][hh * C:(hh + 1) * C, D_V:], qe[hh * C:(hh + 1) * C]], axis=0),
                            s_scr[d["s0"] + h]) for hh, h in enumerate(d["heads"])]
        for d in now:
            d["vnew"] = jnp.concatenate(
                [d["x"][hh * C:(hh + 1) * C, :D_V] - xs[:C] for hh, xs in enumerate(d["xs"])], axis=0)
            d["o"] = jnp.concatenate([xs[C:] for xs in d["xs"]], axis=0) + _dot(d["attn"], d["vnew"])
        for d in now:
            kt = stacked(d, Q_W, True) * jnp.exp(d["glast"] - d["gcol"])
            egl = jnp.exp(d["glast"])
            for hh, h in enumerate(d["heads"]):
                sl = slice(hh * C, (hh + 1) * C)
                if C >= 16:
                    upd = _dot_tn(kt[sl], d["vnew"][sl])
                else:
                    upd = _dot_tn(kt, jnp.where(row_head == hh, d["vnew"], 0.0))
                s_scr[d["s0"] + h] = s_scr[d["s0"] + h] * egl[hh * C:hh * C + 1] + upd

    for d in grp:
        o = d["o"]
        zst = jnp.concatenate([z_all[d["rows"], h * D_V:(h + 1) * D_V] for h in d["heads"]], axis=0)
        on = o * lax.rsqrt(jnp.mean(o * o, axis=-1, keepdims=True) + NORM_EPS) * gam * _silu(zst)
        for hh, h in enumerate(d["heads"]):
            o_ref[d["rows"], h * D_V:(h + 1) * D_V] = on[hh * C:(hh + 1) * C].astype(o_ref.dtype)

    @pl.when(c == pl.num_programs(1) - 1)
    def _fin():
        for s in range(nseq):
            sout_ref[s] = s_scr[s * H_V:(s + 1) * H_V]


def _gdn(proj, ba, conv_w, a_log, dt_bias, gamma_a, row0, B, L, C, nseq, nck, conv_buf=None, s0=None):
    has_state = s0 is not None
    hg = STACK_ROWS // C
    ng = H_V // hg
    nrow = 16
    nc = L // C
    R = STACK_ROWS
    assert nseq == 1 or nc == nck
    rows_blk = nseq * nck * C
    steps = nc // nck

    def arrange(v):
        v = v.reshape(B, nc, C, ng, hg).transpose(0, 1, 3, 4, 2).reshape(B, nc, ng, R)
        return jnp.pad(v, ((0, 0), (0, 0), (0, nrow - ng), (0, 0)))

    rows = ba[row0:row0 + B * L]
    ba_arr = jnp.concatenate([arrange(rows[:, :H_V]), arrange(rows[:, H_V:2 * H_V])], axis=2)

    def arrange_param(p):
        v = jnp.broadcast_to(p.reshape(ng, hg, 1), (ng, hg, C)).reshape(ng, R)
        return jnp.pad(v, ((0, nrow - ng), (0, 0)))

    ad = jnp.concatenate([arrange_param(a_log), arrange_param(dt_bias)], axis=0)
    rb0 = row0 // rows_blk
    in_specs = [
        pl.BlockSpec((rows_blk, QKV_W), lambda b, c: (rb0 + b * steps + c, 0)),
        pl.BlockSpec((rows_blk, V_W), lambda b, c: (rb0 + b * steps + c, Z_BLK)),
        pl.BlockSpec((nseq, nck, 2 * nrow, R), lambda b, c: (b, c, 0, 0)),
        pl.BlockSpec((CONV_W, QKV_W), lambda b, c: (0, 0)),
        pl.BlockSpec((2 * nrow, R), lambda b, c: (0, 0)),
        pl.BlockSpec((1, D_V), lambda b, c: (0, 0)),
    ]
    args = [proj, proj, ba_arr, conv_w, ad, gamma_a.reshape(1, D_V)]
    if has_state:
        in_specs += [pl.BlockSpec((nseq, CONV_W - 1, QKV_W), lambda b, c: (b, 0, 0)),
                     pl.BlockSpec((nseq, H_V, D_K, D_V), lambda b, c: (b, 0, 0, 0))]
        args += [conv_buf, s0]
    return pl.pallas_call(
        functools.partial(_gdn_kernel, C=C, hg=hg, nseq=nseq, nck=nck, has_state=has_state),
        grid=(B // nseq, steps),
        in_specs=in_specs,
        out_specs=[pl.BlockSpec((rows_blk, V_W), lambda b, c: (b * steps + c, 0)),
                   pl.BlockSpec((nseq, H_V, D_K, D_V), lambda b, c: (b, 0, 0, 0))],
        out_shape=[jax.ShapeDtypeStruct((B * L, V_W), BF16),
                   jax.ShapeDtypeStruct((B, H_V, D_K, D_V), F32)],
        scratch_shapes=[pltpu.VMEM((nseq * H_V, D_K, D_V), F32),
                        pltpu.VMEM((nseq * 8, QKV_W), F32),
                        pltpu.VMEM((rows_blk, QKV_W), F32)],
        compiler_params=_cparams(("parallel", "arbitrary")),
        name="gdn_state" if has_state else "gdn_fresh",
    )(*args)


def _pool_kernel(*refs, tc, nseq, start_pos, has_state):
    if has_state:
        u_ref, buf_ref, o_ref, tail_scr = refs
    else:
        u_ref, o_ref, tail_scr = refs
    hist = POOL_BUF + 1
    c = pl.program_id(1)

    @pl.when(c == 0)
    def _init():
        tail_scr[...] = jnp.zeros(tail_scr.shape, F32)
        if has_state:
            for s in range(nseq):
                tail_scr[s * hist + 1:(s + 1) * hist] = buf_ref[s]

    u_all = u_ref[...].astype(F32)
    pos1 = (start_pos + 1 + c * tc + lax.broadcasted_iota(jnp.int32, (tc, 1), 0)).astype(F32)
    seq_out = []
    for s in range(nseq):
        u = u_all[s * tc:(s + 1) * tc]
        xe = jnp.concatenate([tail_scr[s * hist:(s + 1) * hist], u], axis=0)
        tail_scr[s * hist:(s + 1) * hist] = xe[tc:tc + hist]
        parts = []
        for gi, win in enumerate(POOL_WINDOWS):
            sl = slice(gi * POOL_GROUP_W, (gi + 1) * POOL_GROUP_W)
            acc = xe[:, sl]
            shift = 1
            while shift < win:
                acc = acc + pltpu.roll(acc, shift, 0)
                shift *= 2
            cnt = jnp.minimum(float(win), pos1)
            parts.append(acc[hist:hist + tc] / cnt - u[:, sl])
        seq_out.append(jnp.concatenate(parts, axis=1))
    o_ref[...] = jnp.concatenate(seq_out, axis=0).astype(o_ref.dtype)


def _pool(proj, row0, B, L, tc, nseq, start_pos, pool_buf=None):
    has_state = pool_buf is not None
    nc = L // tc
    assert nseq == 1 or nc == 1
    rows_blk = nseq * tc
    rb0 = row0 // rows_blk
    in_specs = [pl.BlockSpec((rows_blk, D_MODEL), lambda b, c: (rb0 + b * nc + c, U_BLK))]
    args = [proj]
    if has_state:
        in_specs.append(pl.BlockSpec((nseq, POOL_BUF, D_MODEL), lambda b, c: (b, 0, 0)))
        args.append(pool_buf)
    return pl.pallas_call(
        functools.partial(_pool_kernel, tc=tc, nseq=nseq, start_pos=start_pos, has_state=has_state),
        grid=(B // nseq, nc),
        in_specs=in_specs,
        out_specs=pl.BlockSpec((rows_blk, D_MODEL), lambda b, c: (b * nc + c, 0)),
        out_shape=jax.ShapeDtypeStruct((B * L, D_MODEL), BF16),
        scratch_shapes=[pltpu.VMEM((nseq * (POOL_BUF + 1), D_MODEL), F32)],
        compiler_params=_cparams(("parallel", "arbitrary")),
        name="pool_state" if has_state else "pool_fresh",
    )(*args)


def _post_kernel(pa_ref, pb_ref, oa_ref, ob_ref, ga_ref, gb_ref, xa_ref, xb_ref, wp_ref, ps_ref, wbb_ref, wba_ref,
                 wo_ref, g_ref, b_ref, h_ref, hp_ref, *, alpha, tiles_a):
    first = pl.program_id(0) < tiles_a
    x = jnp.where(first, xa_ref[...], xb_ref[...])
    pooled = jnp.where(first, pa_ref[...], pb_ref[...])
    o_gated = jnp.where(first, oa_ref[...], ob_ref[...])
    mixed = jnp.concatenate(
        [_dot(pooled[:, gi * POOL_GROUP_W:(gi + 1) * POOL_GROUP_W], wp_ref[gi]) for gi in range(len(POOL_WINDOWS))],
        axis=1) * ps_ref[...]
    branch_b = _dot(mixed, wbb_ref[...])
    branch_a = _dot(o_gated, wba_ref[...])
    merged = _sigmoid(ga_ref[...].astype(F32)) * branch_a + _sigmoid(gb_ref[...].astype(F32)) * branch_b
    h = _layer_norm(alpha * x + _dot(merged, wo_ref[...]), g_ref[...], b_ref[...])
    h_ref[...] = h
    _store_pieces(hp_ref, h)


def _post(pooled_a, pooled_b, o_a, o_b, proj, x_a, x_b, w_pool, pool_scale, w_br_b, w_br_a, w_out, ln_g, ln_b,
          alpha, tm):
    t = x_a.shape[0] + x_b.shape[0]
    tiles_a = x_a.shape[0] // tm
    row = lambda i: (i, 0)
    row_a = lambda i: (jnp.minimum(i, tiles_a - 1), 0)
    row_b = lambda i: (jnp.maximum(i - tiles_a, 0), 0)
    const2 = lambda i: (0, 0)
    return pl.pallas_call(
        functools.partial(_post_kernel, alpha=alpha, tiles_a=tiles_a),
        grid=(t // tm,),
        in_specs=[pl.BlockSpec((tm, D_MODEL), row_a),
                  pl.BlockSpec((tm, D_MODEL), row_b),
                  pl.BlockSpec((tm, V_W), row_a),
                  pl.BlockSpec((tm, V_W), row_b),
                  pl.BlockSpec((tm, D_MODEL), lambda i: (i, U_BLK + 1)),
                  pl.BlockSpec((tm, D_MODEL), lambda i: (i, U_BLK + 2)),
                  pl.BlockSpec((tm, D_MODEL), row_a),
                  pl.BlockSpec((tm, D_MODEL), row_b),
                  pl.BlockSpec(w_pool.shape, lambda i: (0, 0, 0), pipeline_mode=pl.Buffered(1)),
                  pl.BlockSpec((1, D_MODEL), const2),
                  pl.BlockSpec(w_br_b.shape, const2, pipeline_mode=pl.Buffered(1)),
                  pl.BlockSpec(w_br_a.shape, const2, pipeline_mode=pl.Buffered(1)),
                  pl.BlockSpec(w_out.shape, const2, pipeline_mode=pl.Buffered(1)),
                  pl.BlockSpec((1, D_MODEL), const2),
                  pl.BlockSpec((1, D_MODEL), const2)],
        out_specs=[pl.BlockSpec((tm, D_MODEL), row), pl.BlockSpec((N_PIECE, tm, PIECE_W), lambda i: (0, i, 0))],
        out_shape=[jax.ShapeDtypeStruct((t, D_MODEL), F32),
                   jax.ShapeDtypeStruct((N_PIECE, t, PIECE_W), jnp.uint32)],
        compiler_params=_cparams(("parallel",)),
        name="post_mixers",
    )(pooled_a, pooled_b, o_a, o_b, proj, proj, x_a, x_b, w_pool, pool_scale, w_br_b, w_br_a, w_out, ln_g, ln_b)


def _router_kernel(h_ref, wt_ref, bias_ref, idx_ref, wts_ref, rank_ref, cnt_ref, carry_scr, *, tm):
    i = pl.program_id(0)

    @pl.when(i == 0)
    def _init():
        carry_scr[...] = jnp.zeros(carry_scr.shape, F32)

    logits = _dot3_nt(wt_ref[...], h_ref[...])
    sc = _sigmoid(logits)
    ch = sc + bias_ref[...]
    neg = -jnp.inf
    e_in = lax.broadcasted_iota(jnp.int32, (GROUP_SZ, tm), 0)
    gs_rows = []
    for g in range(N_GROUP):
        blk = ch[g * GROUP_SZ:(g + 1) * GROUP_SZ]
        m1 = jnp.max(blk, axis=0, keepdims=True)
        i1 = jnp.min(jnp.where(blk == m1, e_in, GROUP_SZ), axis=0, keepdims=True)
        m2 = jnp.max(jnp.where(e_in == i1, neg, blk), axis=0, keepdims=True)
        gs_rows.append(m1 + m2)
    gs = jnp.concatenate(gs_rows, axis=0)
    g_io = lax.broadcasted_iota(jnp.int32, (N_GROUP, tm), 0)
    e_io = lax.broadcasted_iota(jnp.int32, (N_EXPERTS, tm), 0)
    e_grp = e_io >> int(math.log2(GROUP_SZ))
    masked = jnp.full((N_EXPERTS, tm), neg, F32)
    for _ in range(TOPK_GROUP):
        mx = jnp.max(gs, axis=0, keepdims=True)
        gi = jnp.min(jnp.where(gs == mx, g_io, N_GROUP), axis=0, keepdims=True)
        gs = jnp.where(g_io == gi, neg, gs)
        masked = jnp.where(e_grp == gi, ch, masked)
    idx_rows, w_rows = [], []
    onehot = jnp.zeros((N_EXPERTS, tm), F32)
    for _ in range(TOP_K):
        mx = jnp.max(masked, axis=0, keepdims=True)
        ei = jnp.min(jnp.where(masked == mx, e_io, N_EXPERTS), axis=0, keepdims=True)
        hit = e_io == ei
        idx_rows.append(ei)
        w_rows.append(jnp.sum(jnp.where(hit, sc, 0.0), axis=0, keepdims=True))
        onehot = jnp.where(hit, 1.0, onehot)
        masked = jnp.where(hit, neg, masked)
    wsel = jnp.concatenate(w_rows, axis=0)
    wts_ref[...] = wsel / jnp.sum(wsel, axis=0, keepdims=True) * ROUTED_SCALE
    idx_ref[...] = jnp.concatenate(idx_rows, axis=0)

    tr = lax.broadcasted_iota(jnp.int32, (tm, tm), 0)
    tc_ = lax.broadcasted_iota(jnp.int32, (tm, tm), 1)
    before = jnp.where(tr < tc_, 1.0, 0.0).astype(BF16)
    cum = jnp.dot(onehot.astype(BF16), before, preferred_element_type=F32) + carry_scr[...]
    rank_ref[...] = jnp.concatenate(
        [jnp.sum(jnp.where(e_io == ei, cum, 0.0), axis=0, keepdims=True) for ei in idx_rows],
        axis=0).astype(jnp.int32)
    carry_scr[...] = carry_scr[...] + jnp.sum(onehot, axis=1, keepdims=True)
    cnt_ref[...] = carry_scr[...]


def _router(h, w_router_t, bias_col, tm):
    t = h.shape[0]
    return pl.pallas_call(
        functools.partial(_router_kernel, tm=tm),
        grid=(t // tm,),
        in_specs=[pl.BlockSpec((tm, D_MODEL), lambda i: (i, 0)),
                  pl.BlockSpec((N_EXPERTS, D_MODEL), lambda i: (0, 0)),
                  pl.BlockSpec((N_EXPERTS, 1), lambda i: (0, 0))],
        out_specs=[pl.BlockSpec((TOP_K, tm), lambda i: (0, i)),
                   pl.BlockSpec((TOP_K, tm), lambda i: (0, i)),
                   pl.BlockSpec((TOP_K, tm), lambda i: (0, i)),
                   pl.BlockSpec((N_EXPERTS, 1), lambda i: (0, 0))],
        out_shape=[jax.ShapeDtypeStruct((TOP_K, t), jnp.int32),
                   jax.ShapeDtypeStruct((TOP_K, t), F32),
                   jax.ShapeDtypeStruct((TOP_K, t), jnp.int32),
                   jax.ShapeDtypeStruct((N_EXPERTS, 1), F32)],
        scratch_shapes=[pltpu.VMEM((N_EXPERTS, 1), F32)],
        compiler_params=_cparams(("arbitrary",)),
        name="router",
    )(h, w_router_t, bias_col)


def _dest_kernel(idx_ref, rank_ref, start_ref, dest_ref, *, tm, n_rows):
    e_io = lax.broadcasted_iota(jnp.int32, (N_EXPERTS, tm), 0)
    starts = start_ref[...]
    rows = []
    for k in range(TOP_K):
        seg = jnp.sum(jnp.where(e_io == idx_ref[k:k + 1, :], starts, 0.0), axis=0, keepdims=True)
        rows.append(seg.astype(jnp.int32) + rank_ref[k:k + 1, :])
    base = jnp.concatenate(rows, axis=0)
    for p in range(N_PIECE):
        dest_ref[p] = base + p * n_rows


def _dest(idx_t, rank_t, seg_start_col, n_rows, tm):
    t = idx_t.shape[1]
    blk = pl.BlockSpec((TOP_K, tm), lambda i: (0, i))
    return pl.pallas_call(
        functools.partial(_dest_kernel, tm=tm, n_rows=n_rows),
        grid=(t // tm,),
        in_specs=[blk, blk, pl.BlockSpec((N_EXPERTS, 1), lambda i: (0, 0))],
        out_specs=pl.BlockSpec((N_PIECE, TOP_K, tm), lambda i: (0, 0, i)),
        out_shape=jax.ShapeDtypeStruct((N_PIECE, TOP_K, t), jnp.int32),
        compiler_params=_cparams(("parallel",)),
        name="dispatch_rows",
    )(idx_t, rank_t, seg_start_col)


def _expert_kernel(be_ref, nu_ref, first_ref, nxt_ref, slot_ref, x_ref, wg_hbm, wu_hbm, wd_hbm, y_ref,
                   wg_buf, wu_buf, wd_buf, wgu_scr, wd_scr, sems):
    i = pl.program_id(0)
    live = i < nu_ref[0]

    def fetch(e, slot):
        return [pltpu.make_async_copy(wg_hbm.at[e], wg_buf.at[slot], sems.at[slot, 0]),
                pltpu.make_async_copy(wu_hbm.at[e], wu_buf.at[slot], sems.at[slot, 1]),
                pltpu.make_async_copy(wd_hbm.at[e], wd_buf.at[slot], sems.at[slot, 2])]

    @pl.when(live & (first_ref[i] == 1))
    def _new_expert():
        slot = slot_ref[i]

        @pl.when(i == 0)
        def _():
            for cp in fetch(be_ref[i], slot):
                cp.start()

        @pl.when(nxt_ref[i] >= 0)
        def _():
            for cp in fetch(nxt_ref[i], 1 - slot):
                cp.start()

        for cp in fetch(be_ref[i], slot):
            cp.wait()
        wgu_scr[:, :D_EXPERT] = wg_buf[slot].astype(BF16)
        wgu_scr[:, D_EXPERT:] = wu_buf[slot].astype(BF16)
        wd_scr[...] = wd_buf[slot].astype(BF16)

    @pl.when(live)
    def _():
        n_sub = 2
        sub = EXPERT_BM // n_sub
        xs = [jnp.concatenate([c.astype(BF16) for c in
                               _load_pieces([x_ref[p, s * sub:(s + 1) * sub] for p in range(N_PIECE)])], axis=1)
              for s in range(n_sub)]
        gus = [jnp.dot(x, wgu_scr[...], preferred_element_type=F32) for x in xs]
        acts = [(_silu(gu[:, :D_EXPERT]) * gu[:, D_EXPERT:]).astype(BF16) for gu in gus]
        ys = [jnp.dot(a, wd_scr[...], preferred_element_type=F32) for a in acts]
        for s, y in enumerate(ys):
            for p in range(N_PIECE):
                y_ref[p, s * sub:(s + 1) * sub] = _pack_halves(y[:, 2 * p * PIECE_W:2 * (p + 1) * PIECE_W])


def _experts(xs, blk_exp, n_used, blk_first, blk_next, blk_slot, w_gate, w_up, w_down):
    n_rows = xs.shape[1]
    n_blocks = n_rows // EXPERT_BM

    def row_block(i, be, nu, *_):
        return (0, jnp.minimum(i, nu[0] - 1), 0)

    grid_spec = pltpu.PrefetchScalarGridSpec(
        num_scalar_prefetch=5,
        grid=(n_blocks,),
        in_specs=[pl.BlockSpec((N_PIECE, EXPERT_BM, PIECE_W), row_block),
                  pl.BlockSpec(memory_space=pl.ANY),
                  pl.BlockSpec(memory_space=pl.ANY),
                  pl.BlockSpec(memory_space=pl.ANY)],
        out_specs=pl.BlockSpec((N_PIECE, EXPERT_BM, PIECE_W), row_block),
        scratch_shapes=[pltpu.VMEM((2, D_MODEL, D_EXPERT), F32),
                        pltpu.VMEM((2, D_MODEL, D_EXPERT), F32),
                        pltpu.VMEM((2, D_EXPERT, D_MODEL), F32),
                        pltpu.VMEM((D_MODEL, 2 * D_EXPERT), BF16),
                        pltpu.VMEM((D_EXPERT, D_MODEL), BF16),
                        pltpu.SemaphoreType.DMA((2, 3))],
    )
    return pl.pallas_call(
        _expert_kernel,
        grid_spec=grid_spec,
        out_shape=jax.ShapeDtypeStruct((N_PIECE, n_rows, PIECE_W), jnp.uint32),
        compiler_params=_cparams(("arbitrary",)),
        name="experts",
    )(blk_exp, n_used, blk_first, blk_next, blk_slot, xs, w_gate, w_up, w_down)


SC_WINDOW = 128
V7X_SC_CORES = 2
V7X_SC_SUBCORES = 16


def _sc_mesh():
    return plsc.VectorSubcoreMesh(core_axis_name="core", subcore_axis_name="subcore",
                                  num_cores=V7X_SC_CORES, num_subcores=V7X_SC_SUBCORES)


def _sc_scatter_rows(src, dest, n_rows, seg, repeat):
    d = src.shape[1]
    n_idx = dest.shape[0]
    seg_blocks = seg // SC_WINDOW
    dest2 = dest.reshape(1, n_idx)

    def src_block(i):
        return ((i // (repeat * seg_blocks)) * seg_blocks + i % seg_blocks, 0)

    @functools.partial(pl.kernel, out_type=jax.ShapeDtypeStruct((n_rows, d), src.dtype), mesh=_sc_mesh(),
                       scratch_types=[], name="sc_dispatch")
    def run(src_hbm, idx_hbm, out_hbm):
        def body(rows_vmem, idx_vmem):
            pltpu.sync_copy(rows_vmem, out_hbm.at[idx_vmem.at[0]])

        pltpu.emit_pipeline(
            body,
            grid=(n_idx // SC_WINDOW,),
            in_specs=[pl.BlockSpec((SC_WINDOW, d), src_block),
                      pl.BlockSpec((1, SC_WINDOW), lambda i: (0, i))],
            out_specs=[],
            core_axis_name=("core", "subcore"),
            dimension_semantics=(pltpu.PARALLEL,),
        )(src_hbm, idx_hbm)

    return run(src, dest2)


def _sc_gather_rows(table, idx):
    d = table.shape[1]
    n_idx = idx.shape[0]
    idx2 = idx.reshape(1, n_idx)

    @functools.partial(pl.kernel, out_type=jax.ShapeDtypeStruct((n_idx, d), table.dtype), mesh=_sc_mesh(),
                       scratch_types=[], name="sc_combine_gather")
    def run(table_hbm, idx_hbm, out_hbm):
        def body(idx_vmem, rows_vmem):
            pltpu.sync_copy(table_hbm.at[idx_vmem.at[0]], rows_vmem)

        pltpu.emit_pipeline(
            body,
            grid=(n_idx // SC_WINDOW,),
            in_specs=[pl.BlockSpec((1, SC_WINDOW), lambda i: (0, i))],
            out_specs=[pl.BlockSpec((SC_WINDOW, d), lambda i: (i, 0))],
            core_axis_name=("core", "subcore"),
            dimension_semantics=(pltpu.PARALLEL,),
        )(idx_hbm, out_hbm)

    return run(table, idx2)


def _combine_kernel(yg_ref, wts_ref, h_ref, wgu_ref, wd_ref, g_ref, b_ref, outa_ref, outb_ref, *, alpha, tiles_a):
    wts = wts_ref[...]
    acc = None
    for k in range(TOP_K):
        cols = _load_pieces([yg_ref[p, k] for p in range(N_PIECE)])
        wk = wts[:, k:k + 1]
        acc = [c * wk for c in cols] if acc is None else [a + c * wk for a, c in zip(acc, cols)]
    routed = jnp.concatenate(acc, axis=1)
    h = h_ref[...]
    gu = _dot(h, wgu_ref[...])
    shared = _dot(_silu(gu[:, :D_EXPERT]) * gu[:, D_EXPERT:], wd_ref[...])
    y = _layer_norm(alpha * h + (routed + shared), g_ref[...], b_ref[...])
    i = pl.program_id(0)

    @pl.when(i < tiles_a)
    def _():
        outa_ref[...] = y

    @pl.when(i >= tiles_a)
    def _():
        outb_ref[...] = y


def _combine(yg, wts, h, w_sh_gu, w_sh_down, ln_g, ln_b, alpha, tm, t_a):
    t = h.shape[0]
    tiles_a = t_a // tm
    row = lambda i: (i, 0)
    const2 = lambda i: (0, 0)
    return pl.pallas_call(
        functools.partial(_combine_kernel, alpha=alpha, tiles_a=tiles_a),
        grid=(t // tm,),
        in_specs=[pl.BlockSpec((N_PIECE, TOP_K, tm, PIECE_W), lambda i: (0, 0, i, 0)),
                  pl.BlockSpec((tm, TOP_K), row),
                  pl.BlockSpec((tm, D_MODEL), row),
                  pl.BlockSpec(w_sh_gu.shape, const2),
                  pl.BlockSpec(w_sh_down.shape, const2),
                  pl.BlockSpec((1, D_MODEL), const2),
                  pl.BlockSpec((1, D_MODEL), const2)],
        out_specs=[pl.BlockSpec((tm, D_MODEL), lambda i: (jnp.minimum(i, tiles_a - 1), 0)),
                   pl.BlockSpec((tm, D_MODEL), lambda i: (jnp.maximum(i - tiles_a, 0), 0))],
        out_shape=[jax.ShapeDtypeStruct((t_a, D_MODEL), F32), jax.ShapeDtypeStruct((t - t_a, D_MODEL), F32)],
        compiler_params=_cparams(("arbitrary",)),
        name="combine_ln2",
    )(yg, wts, h, w_sh_gu, w_sh_down, ln_g, ln_b)


def _layer(xp, xs, s_delta, s_conv, s_pool, w_in, conv_w, a_log, dt_bias, gamma_a, w_br_a, w_pool,
           pool_scale, w_br_b, w_out, ln1_g, ln1_b, w_router, router_bias, w_exp_gate, w_exp_up,
           w_exp_down, w_sh_gate, w_sh_up, w_sh_down, ln2_g, ln2_b, alpha):
    Bp, Lp, _ = xp.shape
    Bs, Ls, _ = xs.shape
    Tp, Ts = Bp * Lp, Bs * Ls
    T = Tp + Ts
    x_p = xp.reshape(Tp, D_MODEL)
    x_s = xs.reshape(Ts, D_MODEL)

    o_b, o_u = QKV_W + V_W, QKV_W + V_W + 2 * H_V
    w_main = jnp.concatenate([w_in[:, :o_b], w_in[:, o_u:]], axis=1).astype(BF16)
    w_ba = jnp.pad(w_in[:, o_b:o_u], ((0, 0), (0, 128 - 2 * H_V))).astype(BF16)
    proj, ba = _in_proj(x_p, x_s, w_main, w_ba, IN_PROJ_TM, IN_PROJ_TN)

    o_p, sd_p = _gdn(proj, ba, conv_w, a_log, dt_bias, gamma_a, 0, Bp, Lp, GDN_CHUNK, 1, GDN_FRESH_CHUNKS)
    o_s, sd_s = _gdn(proj, ba, conv_w, a_log, dt_bias, gamma_a, Tp, Bs, Ls, Ls, GDN_STATE_SEQS, 1,
                     conv_buf=s_conv, s0=s_delta)

    pooled_p = _pool(proj, 0, Bp, Lp, POOL_TILE, 1, 0)
    pooled_s = _pool(proj, Tp, Bs, Ls, Ls, POOL_STATE_SEQS, PAST_LEN, pool_buf=s_pool)

    h, hp = _post(pooled_p, pooled_s, o_p, o_s, proj, x_p, x_s, w_pool.astype(BF16), pool_scale.reshape(1, D_MODEL),
                  w_br_b.astype(BF16), w_br_a.astype(BF16), w_out.astype(BF16),
                  ln1_g.reshape(1, D_MODEL), ln1_b.reshape(1, D_MODEL), alpha, POST_TM)

    idx_t, wts_t, rank_t, cnt = _router(h, w_router.T, router_bias.reshape(N_EXPERTS, 1), ROUTER_TM)
    counts = cnt[:, 0].astype(jnp.int32)
    padded = ((counts + EXPERT_BM - 1) // EXPERT_BM) * EXPERT_BM
    pends = jnp.cumsum(padded)
    pstarts = pends - padded
    n_blocks = (T * TOP_K + N_EXPERTS * (EXPERT_BM - 1) + EXPERT_BM - 1) // EXPERT_BM
    n_rows = n_blocks * EXPERT_BM
    piece_rows = _dest(idx_t, rank_t, pstarts.astype(F32).reshape(N_EXPERTS, 1), n_rows, ROUTER_TM)
    blk_start = jnp.arange(n_blocks, dtype=jnp.int32) * EXPERT_BM
    blk_exp = jnp.minimum(jnp.sum((pends[None, :] <= blk_start[:, None]).astype(jnp.int32), axis=1),
                          N_EXPERTS - 1)
    n_used = (pends[-1:] // EXPERT_BM).astype(jnp.int32)
    has_rows = counts > 0
    e_ids = jnp.arange(N_EXPERTS, dtype=jnp.int32)
    slot_of = (jnp.cumsum(has_rows.astype(jnp.int32)) - 1) & 1
    later = jnp.where(has_rows, e_ids, N_EXPERTS)
    next_of = jnp.concatenate([lax.cummin(later, reverse=True)[1:], jnp.full((1,), N_EXPERTS, jnp.int32)])
    next_of = jnp.where(next_of < N_EXPERTS, next_of, -1)
    blk_first = (blk_start == pstarts[blk_exp]).astype(jnp.int32)
    blk_next = next_of[blk_exp]
    blk_slot = slot_of[blk_exp]

    piece_idx = piece_rows.reshape(N_PIECE * TOP_K * T)
    x_sorted = _sc_scatter_rows(hp.reshape(N_PIECE * T, PIECE_W), piece_idx, N_PIECE * n_rows, T, TOP_K)
    y_sorted = _experts(x_sorted.reshape(N_PIECE, n_rows, PIECE_W), blk_exp, n_used, blk_first, blk_next,
                        blk_slot, w_exp_gate, w_exp_up, w_exp_down)
    yg = _sc_gather_rows(y_sorted.reshape(N_PIECE * n_rows, PIECE_W), piece_idx)
    yg = yg.reshape(N_PIECE, TOP_K, T, PIECE_W)

    w_sh_gu = jnp.concatenate([w_sh_gate, w_sh_up], axis=1).astype(BF16)
    y_p, y_s = _combine(yg, wts_t.T, h, w_sh_gu, w_sh_down.astype(BF16),
                        ln2_g.reshape(1, D_MODEL), ln2_b.reshape(1, D_MODEL), alpha, COMBINE_TM, Tp)

    def carried_rows(keep, col0, width, name, old_p, old_s):
        parts, counts = [], []
        for x2d, B, L in ((x_p, Bp, Lp), (x_s, Bs, Ls)):
            n = min(L, keep)
            rows = jnp.arange(B, dtype=jnp.int32)[:, None] * L + (L - n) + jnp.arange(n, dtype=jnp.int32)
            parts.append(jnp.take(x2d, rows.reshape(-1), axis=0).astype(BF16))
            counts.append((B, n))
        n_tail = sum(B * n for B, n in counts)
        n_pad = -(-n_tail // 128) * 128
        xt = jnp.pad(jnp.concatenate(parts, axis=0), ((0, n_pad - n_tail), (0, 0)))
        tail = _matmul(xt, w_main, col0, width, n_pad, 1024, F32, name)
        outs, r0 = [], 0
        for (B, n), old in zip(counts, (old_p, old_s)):
            new = tail[r0:r0 + B * n].reshape(B, n, width)
            outs.append(new if n == keep else jnp.concatenate([old[:, n:], new], axis=1))
            r0 += B * n
        return outs

    new_conv_p, new_conv_s = carried_rows(CONV_W - 1, 0, QKV_W, "in_proj_conv_rows",
                                          jnp.zeros((Bp, CONV_W - 1, QKV_W), F32), s_conv)
    new_pool_p, new_pool_s = carried_rows(POOL_BUF, U_BLK * D_MODEL, D_MODEL, "in_proj_pool_rows",
                                          jnp.zeros((Bp, POOL_BUF, D_MODEL), F32), s_pool)
    return (y_p.reshape(Bp, Lp, D_MODEL), y_s.reshape(Bs, Ls, D_MODEL),
            sd_p, new_conv_p, new_pool_p, sd_s, new_conv_s, new_pool_s)


def kernel(x_prompt, x_sample, state_delta, state_conv, state_pool, w_in, conv_w, a_log, dt_bias, gamma_a,
           w_br_a, w_pool, pool_scale, w_br_b, w_out, ln1_g, ln1_b, w_router, router_bias,
           w_exp_gate, w_exp_up, w_exp_down, w_sh_gate, w_sh_up, w_sh_down, ln2_g, ln2_b):
    depth = w_in.shape[0]
    alpha = (2 * depth) ** 0.25
    yp, ys = x_prompt, x_sample
    outs = [[] for _ in range(6)]
    for l in range(depth):
        res = _layer(yp, ys, state_delta[l], state_conv[l], state_pool[l], w_in[l], conv_w[l], a_log[l],
                     dt_bias[l], gamma_a[l], w_br_a[l], w_pool[l], pool_scale[l], w_br_b[l], w_out[l],
                     ln1_g[l], ln1_b[l], w_router[l], router_bias[l], w_exp_gate[l], w_exp_up[l],
                     w_exp_down[l], w_sh_gate[l], w_sh_up[l], w_sh_down[l], ln2_g[l], ln2_b[l], alpha)
        yp, ys = res[0], res[1]
        for lst, v in zip(outs, res[2:]):
            lst.append(v)
    return (yp, ys) + tuple(jnp.stack(v) for v in outs)
```

```python
import functools
import math

import jax
import jax.numpy as jnp
from jax import lax
from jax.experimental import pallas as pl
from jax.experimental.pallas import tpu as pltpu
from jax.experimental.pallas import tpu_sc as plsc

F32 = jnp.float32
BF16 = jnp.bfloat16

D_MODEL = 1024
H_K = 8
D_K = 128
H_V = 16
D_V = 128
Q_W = H_K * D_K
V_W = H_V * D_V
QKV_W = 2 * Q_W + V_W
CONV_W = 4
POOL_WINDOWS = (2, 4, 8, 16)
POOL_GROUP_W = D_MODEL // len(POOL_WINDOWS)
POOL_BUF = max(POOL_WINDOWS) - 1
N_EXPERTS = 256
TOP_K = 8
N_GROUP = 8
TOPK_GROUP = 4
GROUP_SZ = N_EXPERTS // N_GROUP
D_EXPERT = D_MODEL // 4
ROUTED_SCALE = 2.5
LN_EPS = 1e-5
NORM_EPS = 1e-6
PAST_LEN = 16384

MAIN_W = QKV_W + V_W + 3 * D_MODEL
Z_BLK = QKV_W // V_W
U_BLK = (QKV_W + V_W) // D_MODEL

IN_PROJ_TM = 1024
IN_PROJ_TN = MAIN_W // 3
GDN_CHUNK = 64
POOL_TILE = 2048
POST_TM = 512
ROUTER_TM = 512
DEST_TM = 1024
COMBINE_TM = 512

STACK_ROWS = 128
SOLVE_BASE = 16
GDN_FRESH_CHUNKS = 4
EXPERT_BM = 512
GDN_STATE_SEQS = 8
POOL_STATE_SEQS = 16
VMEM_LIMIT = 56 * 1024 * 1024


def _cparams(sem):
    return pltpu.CompilerParams(dimension_semantics=sem, vmem_limit_bytes=VMEM_LIMIT)


def _sigmoid(x):
    return 0.5 * jnp.tanh(0.5 * x) + 0.5


def _silu(x):
    return x * _sigmoid(x)


def _softplus(x):
    return jnp.maximum(x, 0.0) + jnp.log(1.0 + jnp.exp(-jnp.abs(x)))


def _dot(a, b):
    return jnp.dot(a.astype(BF16), b.astype(BF16), preferred_element_type=F32)


def _dot_nt(a, b):
    return lax.dot_general(a.astype(BF16), b.astype(BF16), (((1,), (1,)), ((), ())),
                           preferred_element_type=F32)


def _dot_tn(a, b):
    return lax.dot_general(a.astype(BF16), b.astype(BF16), (((0,), (0,)), ((), ())),
                           preferred_element_type=F32)


def _split(a):
    hi = a.astype(BF16)
    lo = (a - hi.astype(F32)).astype(BF16)
    return hi, lo


def _dot3_nt(a, b):
    ah, al = _split(a)
    bh, bl = _split(b)
    d = functools.partial(lax.dot_general, dimension_numbers=(((1,), (1,)), ((), ())),
                          preferred_element_type=F32)
    return d(ah, bh) + (d(ah, bl) + d(al, bh))


def _pack_halves(x):
    n = x.shape[1] // 2
    hi = lax.bitcast_convert_type(x[:, :n].astype(BF16).astype(F32), jnp.uint32)
    lo = lax.bitcast_convert_type(x[:, n:].astype(BF16).astype(F32), jnp.uint32)
    return (hi & jnp.uint32(0xFFFF0000)) | (lo >> 16)


def _unpack_halves(w):
    hi = lax.bitcast_convert_type(w & jnp.uint32(0xFFFF0000), F32)
    lo = lax.bitcast_convert_type(w << 16, F32)
    return hi, lo


N_PIECE = 2
PIECE_W = D_MODEL // (2 * N_PIECE)


def _store_pieces(ref, x):
    for p in range(N_PIECE):
        ref[p] = _pack_halves(x[:, 2 * p * PIECE_W:2 * (p + 1) * PIECE_W])


def _load_pieces(pieces):
    cols = []
    for w in pieces:
        cols.extend(_unpack_halves(w))
    return cols


def _layer_norm(x, g, b):
    mu = jnp.mean(x, axis=-1, keepdims=True)
    xc = x - mu
    var = jnp.mean(xc * xc, axis=-1, keepdims=True)
    return xc * lax.rsqrt(var + LN_EPS) * g + b


def _in_proj_kernel(xa_ref, xb_ref, w_ref, wba_ref, o_ref, ba_ref, x_scr, *, tiles_a):
    @pl.when(pl.program_id(1) == 0)
    def _():
        x = jnp.where(pl.program_id(0) < tiles_a, xa_ref[...], xb_ref[...]).astype(BF16)
        x_scr[...] = x
        ba_ref[...] = jnp.dot(x, wba_ref[...], preferred_element_type=F32)

    o_ref[...] = jnp.dot(x_scr[...], w_ref[...], preferred_element_type=F32).astype(o_ref.dtype)


def _in_proj(x_a, x_b, w_main, w_ba, tm, tn):
    k = x_a.shape[1]
    t = x_a.shape[0] + x_b.shape[0]
    n = w_main.shape[1]
    tiles_a = x_a.shape[0] // tm
    return pl.pallas_call(
        functools.partial(_in_proj_kernel, tiles_a=tiles_a),
        grid=(t // tm, n // tn),
        in_specs=[pl.BlockSpec((tm, k), lambda i, j: (jnp.minimum(i, tiles_a - 1), 0)),
                  pl.BlockSpec((tm, k), lambda i, j: (jnp.maximum(i - tiles_a, 0), 0)),
                  pl.BlockSpec((k, tn), lambda i, j: (0, j)),
                  pl.BlockSpec(w_ba.shape, lambda i, j: (0, 0))],
        out_specs=[pl.BlockSpec((tm, tn), lambda i, j: (i, j)),
                   pl.BlockSpec((tm, w_ba.shape[1]), lambda i, j: (i, 0))],
        out_shape=[jax.ShapeDtypeStruct((t, n), BF16), jax.ShapeDtypeStruct((t, w_ba.shape[1]), F32)],
        scratch_shapes=[pltpu.VMEM((tm, k), BF16)],
        compiler_params=_cparams(("parallel", "arbitrary")),
        name="in_proj",
    )(x_a, x_b, w_main, w_ba)


def _mm_kernel(x_ref, w_ref, o_ref):
    o_ref[...] = jnp.dot(x_ref[...], w_ref[...], preferred_element_type=F32).astype(o_ref.dtype)


def _matmul(x, w, col0, n, tm, tn, out_dtype, name):
    t, k = x.shape
    cb0 = col0 // tn
    return pl.pallas_call(
        _mm_kernel,
        grid=(t // tm, n // tn),
        in_specs=[pl.BlockSpec((tm, k), lambda i, j: (i, 0)),
                  pl.BlockSpec((k, tn), lambda i, j: (0, cb0 + j))],
        out_specs=pl.BlockSpec((tm, tn), lambda i, j: (i, j)),
        out_shape=jax.ShapeDtypeStruct((t, n), out_dtype),
        compiler_params=_cparams(("parallel", "parallel")),
        name=name,
    )(x, w)


def _gdn_kernel(*refs, C, hg, nseq, nck, has_state):
    if has_state:
        (qkv_ref, z_ref, ba_ref, cw_ref, ad_ref, gam_ref, buf_ref, s0_ref,
         o_ref, sout_ref, s_scr, tail_scr, act_scr) = refs
    else:
        (qkv_ref, z_ref, ba_ref, cw_ref, ad_ref, gam_ref,
         o_ref, sout_ref, s_scr, tail_scr, act_scr) = refs
    R = hg * C
    ng = H_V // hg
    log2c = int(math.log2(C))
    c = pl.program_id(1)

    @pl.when(c == 0)
    def _init():
        if has_state:
            tail_scr[...] = jnp.zeros(tail_scr.shape, F32)
            for s in range(nseq):
                s_scr[s * H_V:(s + 1) * H_V] = s0_ref[s]
                tail_scr[(s + 1) * 8 - (CONV_W - 1):(s + 1) * 8] = buf_ref[s]
        else:
            s_scr[...] = jnp.zeros(s_scr.shape, F32)
            tail_scr[...] = jnp.zeros(tail_scr.shape, F32)

    cw = cw_ref[...]
    qkv_all = qkv_ref[...].astype(F32)
    n_prev = CONV_W - 1
    CB = nck * C
    for s in range(nseq):
        x = qkv_all[s * CB:(s + 1) * CB]
        xe = jnp.concatenate([tail_scr[s * 8:(s + 1) * 8], x], axis=0)
        y = x * cw[n_prev:CONV_W]
        for j in range(n_prev):
            y = y + xe[8 - n_prev + j:8 - n_prev + j + CB] * cw[j:j + 1]
        tail_scr[s * 8:(s + 1) * 8] = x[CB - 8:CB]
        act_scr[s * CB:(s + 1) * CB] = _silu(y)
        for off, scale in ((0, D_K ** -0.5), (Q_W, 1.0)):
            for kh in range(H_K):
                cols = slice(off + kh * D_K, off + (kh + 1) * D_K)
                v = act_scr[s * CB:(s + 1) * CB, cols]
                act_scr[s * CB:(s + 1) * CB, cols] = v * (
                    lax.rsqrt(jnp.sum(v * v, axis=-1, keepdims=True) + NORM_EPS) * scale)

    ri = lax.broadcasted_iota(jnp.int32, (R, R), 0)
    ci = lax.broadcasted_iota(jnp.int32, (R, R), 1)
    same = (ri >> log2c) == (ci >> log2c)
    incl = same & (ri >= ci)
    strict = same & (ri > ci)
    eye = ri == ci
    is_last = ci == (((ri >> log2c) << log2c) + (C - 1))
    eye_f = jnp.where(eye, 1.0, 0.0).astype(F32)
    cum_u = jnp.where(same & (ri <= ci), 1.0, 0.0).astype(BF16)

    nrow = ba_ref.shape[2] // 2
    adv = ad_ref[...]
    dd = functools.partial(jnp.dot, preferred_element_type=F32)
    beta_of, G_of = {}, {}
    for s in range(nseq):
        for k in range(nck):
            bav = ba_ref[s, k]
            beta_of[s, k] = _sigmoid(bav[0:nrow])
            g_rows = -jnp.exp(adv[0:nrow]) * _softplus(bav[nrow:2 * nrow] + adv[nrow:2 * nrow])
            g1 = g_rows.astype(BF16)
            r1 = g_rows - g1.astype(F32)
            g2 = r1.astype(BF16)
            g3 = (r1 - g2.astype(F32)).astype(BF16)
            G_of[s, k] = dd(g1, cum_u) + (dd(g2, cum_u) + dd(g3, cum_u))

    gam = gam_ref[...]
    rep = H_V // H_K

    def stacked(d, off, per_k_head):
        blocks = [(h // rep if per_k_head else h) for h in d["heads"]]
        return jnp.concatenate([act_scr[d["rows"], off + b * D_K:off + (b + 1) * D_K] for b in blocks], axis=0)

    log2b = min(log2c, int(math.log2(SOLVE_BASE)))
    same_base = ((ri ^ ci) >> log2b) == 0
    row_head = lax.broadcasted_iota(jnp.int32, (R, D_V), 0) >> log2c
    z_all = z_ref[...].astype(F32)
    grp = [dict(s=s, k=k, g=g, rows=slice(s * CB + k * C, s * CB + (k + 1) * C), s0=s * H_V,
                heads=[g * hg + hh for hh in range(hg)])
           for k in range(nck) for s in range(nseq) for g in range(ng)]
    for d in grp:
        kst = stacked(d, Q_W, True)
        d["kk"] = _dot_nt(kst, kst)
        d["qk"] = _dot_nt(stacked(d, 0, True), kst)
    for d in grp:
        grow = G_of[d["s"], d["k"]][d["g"]:d["g"] + 1]
        brow = beta_of[d["s"], d["k"]][d["g"]:d["g"] + 1]
        gcol = jnp.sum(eye_f * grow, axis=1, keepdims=True)
        bcol = jnp.sum(eye_f * brow, axis=1, keepdims=True)
        glast = jnp.sum(jnp.where(is_last, grow, 0.0), axis=1, keepdims=True)
        dm = jnp.where(incl, jnp.exp(jnp.minimum(gcol - grow, 0.0)), 0.0)
        d.update(gcol=gcol, bcol=bcol, glast=glast, eg=jnp.exp(gcol))
        d["m"] = jnp.where(strict, -(d["kk"] * bcol * dm), 0.0)
        d["attn"] = jnp.where(incl, d["qk"] * dm, 0.0)
        d["pw"] = jnp.where(same_base, d["m"], 0.0)
        d["t"] = eye_f + d["pw"]

    for r in range(log2b):
        last = r == log2b - 1
        for d in grp:
            if r == 0:
                if not last:
                    d["pw"] = _dot(d["pw"], d["pw"])
            elif last:
                d["t"] = d["t"] + _dot(d["pw"], d["t"])
            else:
                out = _dot(d["pw"], jnp.concatenate([d["pw"], d["t"]], axis=1))
                d["pw"] = out[:, :R]
                d["t"] = d["t"] + out[:, R:]
    for lev in range(log2b + 1, log2c + 1):
        coupling = ((ri ^ ci) >> (lev - 1)) == 1
        for d in grp:
            d["y"] = _dot(d["t"], jnp.where(coupling, d["m"], 0.0))
        for d in grp:
            d["t"] = d["t"] + _dot(d["y"], d["t"])
    for d in grp:
        rhs = jnp.concatenate([stacked(d, 2 * Q_W, False) * d["bcol"],
                               stacked(d, Q_W, True) * (d["bcol"] * d["eg"])], axis=1)
        d["x"] = _dot(d["t"], rhs)

    for k in range(nck):
        now = [d for d in grp if d["k"] == k]
        for d in now:
            qe = stacked(d, 0, True) * d["eg"]
            d["xs"] = [_dot(jnp.concatenate([d["x"][hh * C:(hh + 1) * C, D_V:], qe[hh * C:(hh + 1) * C]], axis=0),
                            s_scr[d["s0"] + h]) for hh, h in enumerate(d["heads"])]
        for d in now:
            d["vnew"] = jnp.concatenate(
                [d["x"][hh * C:(hh + 1) * C, :D_V] - xs[:C] for hh, xs in enumerate(d["xs"])], axis=0)
            d["o"] = jnp.concatenate([xs[C:] for xs in d["xs"]], axis=0) + _dot(d["attn"], d["vnew"])
        for d in now:
            kt = stacked(d, Q_W, True) * jnp.exp(d["glast"] - d["gcol"])
            egl = jnp.exp(d["glast"])
            for hh, h in enumerate(d["heads"]):
                sl = slice(hh * C, (hh + 1) * C)
                if C >= 16:
                    upd = _dot_tn(kt[sl], d["vnew"][sl])
                else:
                    upd = _dot_tn(kt, jnp.where(row_head == hh, d["vnew"], 0.0))
                s_scr[d["s0"] + h] = s_scr[d["s0"] + h] * egl[hh * C:hh * C + 1] + upd

    for d in grp:
        o = d["o"]
        zst = jnp.concatenate([z_all[d["rows"], h * D_V:(h + 1) * D_V] for h in d["heads"]], axis=0)
        on = o * lax.rsqrt(jnp.mean(o * o, axis=-1, keepdims=True) + NORM_EPS) * gam * _silu(zst)
        for hh, h in enumerate(d["heads"]):
            o_ref[d["rows"], h * D_V:(h + 1) * D_V] = on[hh * C:(hh + 1) * C].astype(o_ref.dtype)

    @pl.when(c == pl.num_programs(1) - 1)
    def _fin():
        for s in range(nseq):
            sout_ref[s] = s_scr[s * H_V:(s + 1) * H_V]


def _gdn(proj, ba, conv_w, a_log, dt_bias, gamma_a, row0, B, L, C, nseq, nck, conv_buf=None, s0=None):
    has_state = s0 is not None
    hg = STACK_ROWS // C
    ng = H_V // hg
    nrow = 16
    nc = L // C
    R = STACK_ROWS
    assert nseq == 1 or nc == nck
    rows_blk = nseq * nck * C
    steps = nc // nck

    def arrange(v):
        v = v.reshape(B, nc, C, ng, hg).transpose(0, 1, 3, 4, 2).reshape(B, nc, ng, R)
        return jnp.pad(v, ((0, 0), (0, 0), (0, nrow - ng), (0, 0)))

    rows = ba[row0:row0 + B * L]
    ba_arr = jnp.concatenate([arrange(rows[:, :H_V]), arrange(rows[:, H_V:2 * H_V])], axis=2)

    def arrange_param(p):
        v = jnp.broadcast_to(p.reshape(ng, hg, 1), (ng, hg, C)).reshape(ng, R)
        return jnp.pad(v, ((0, nrow - ng), (0, 0)))

    ad = jnp.concatenate([arrange_param(a_log), arrange_param(dt_bias)], axis=0)
    rb0 = row0 // rows_blk
    in_specs = [
        pl.BlockSpec((rows_blk, QKV_W), lambda b, c: (rb0 + b * steps + c, 0)),
        pl.BlockSpec((rows_blk, V_W), lambda b, c: (rb0 + b * steps + c, Z_BLK)),
        pl.BlockSpec((nseq, nck, 2 * nrow, R), lambda b, c: (b, c, 0, 0)),
        pl.BlockSpec((CONV_W, QKV_W), lambda b, c: (0, 0)),
        pl.BlockSpec((2 * nrow, R), lambda b, c: (0, 0)),
        pl.BlockSpec((1, D_V), lambda b, c: (0, 0)),
    ]
    args = [proj, proj, ba_arr, conv_w, ad, gamma_a.reshape(1, D_V)]
    if has_state:
        in_specs += [pl.BlockSpec((nseq, CONV_W - 1, QKV_W), lambda b, c: (b, 0, 0)),
                     pl.BlockSpec((nseq, H_V, D_K, D_V), lambda b, c: (b, 0, 0, 0))]
        args += [conv_buf, s0]
    return pl.pallas_call(
        functools.partial(_gdn_kernel, C=C, hg=hg, nseq=nseq, nck=nck, has_state=has_state),
        grid=(B // nseq, steps),
        in_specs=in_specs,
        out_specs=[pl.BlockSpec((rows_blk, V_W), lambda b, c: (b * steps + c, 0)),
                   pl.BlockSpec((nseq, H_V, D_K, D_V), lambda b, c: (b, 0, 0, 0))],
        out_shape=[jax.ShapeDtypeStruct((B * L, V_W), BF16),
                   jax.ShapeDtypeStruct((B, H_V, D_K, D_V), F32)],
        scratch_shapes=[pltpu.VMEM((nseq * H_V, D_K, D_V), F32),
                        pltpu.VMEM((nseq * 8, QKV_W), F32),
                        pltpu.VMEM((rows_blk, QKV_W), F32)],
        compiler_params=_cparams(("parallel", "arbitrary")),
        name="gdn_state" if has_state else "gdn_fresh",
    )(*args)


def _pool_kernel(*refs, tc, nseq, start_pos, has_state):
    if has_state:
        u_ref, buf_ref, o_ref, tail_scr = refs
    else:
        u_ref, o_ref, tail_scr = refs
    hist = POOL_BUF + 1
    c = pl.program_id(1)

    @pl.when(c == 0)
    def _init():
        tail_scr[...] = jnp.zeros(tail_scr.shape, F32)
        if has_state:
            for s in range(nseq):
                tail_scr[s * hist + 1:(s + 1) * hist] = buf_ref[s]

    u_all = u_ref[...].astype(F32)
    pos1 = (start_pos + 1 + c * tc + lax.broadcasted_iota(jnp.int32, (tc, 1), 0)).astype(F32)
    seq_out = []
    for s in range(nseq):
        u = u_all[s * tc:(s + 1) * tc]
        xe = jnp.concatenate([tail_scr[s * hist:(s + 1) * hist], u], axis=0)
        tail_scr[s * hist:(s + 1) * hist] = xe[tc:tc + hist]
        parts = []
        for gi, win in enumerate(POOL_WINDOWS):
            sl = slice(gi * POOL_GROUP_W, (gi + 1) * POOL_GROUP_W)
            acc = xe[:, sl]
            shift = 1
            while shift < win:
                acc = acc + pltpu.roll(acc, shift, 0)
                shift *= 2
            cnt = jnp.minimum(float(win), pos1)
            parts.append(acc[hist:hist + tc] / cnt - u[:, sl])
        seq_out.append(jnp.concatenate(parts, axis=1))
    o_ref[...] = jnp.concatenate(seq_out, axis=0).astype(o_ref.dtype)


def _pool(proj, row0, B, L, tc, nseq, start_pos, pool_buf=None):
    has_state = pool_buf is not None
    nc = L // tc
    assert nseq == 1 or nc == 1
    rows_blk = nseq * tc
    rb0 = row0 // rows_blk
    in_specs = [pl.BlockSpec((rows_blk, D_MODEL), lambda b, c: (rb0 + b * nc + c, U_BLK))]
    args = [proj]
    if has_state:
        in_specs.append(pl.BlockSpec((nseq, POOL_BUF, D_MODEL), lambda b, c: (b, 0, 0)))
        args.append(pool_buf)
    return pl.pallas_call(
        functools.partial(_pool_kernel, tc=tc, nseq=nseq, start_pos=start_pos, has_state=has_state),
        grid=(B // nseq, nc),
        in_specs=in_specs,
        out_specs=pl.BlockSpec((rows_blk, D_MODEL), lambda b, c: (b * nc + c, 0)),
        out_shape=jax.ShapeDtypeStruct((B * L, D_MODEL), BF16),
        scratch_shapes=[pltpu.VMEM((nseq * (POOL_BUF + 1), D_MODEL), F32)],
        compiler_params=_cparams(("parallel", "arbitrary")),
        name="pool_state" if has_state else "pool_fresh",
    )(*args)


def _post_kernel(pa_ref, pb_ref, oa_ref, ob_ref, ga_ref, gb_ref, xa_ref, xb_ref, wp_ref, ps_ref, wbb_ref, wba_ref,
                 wo_ref, g_ref, b_ref, h_ref, hp_ref, *, alpha, tiles_a):
    first = pl.program_id(0) < tiles_a
    x = jnp.where(first, xa_ref[...], xb_ref[...])
    pooled = jnp.where(first, pa_ref[...], pb_ref[...])
    o_gated = jnp.where(first, oa_ref[...], ob_ref[...])
    mixed = jnp.concatenate(
        [_dot(pooled[:, gi * POOL_GROUP_W:(gi + 1) * POOL_GROUP_W], wp_ref[gi]) for gi in range(len(POOL_WINDOWS))],
        axis=1) * ps_ref[...]
    branch_b = _dot(mixed, wbb_ref[...])
    branch_a = _dot(o_gated, wba_ref[...])
    merged = _sigmoid(ga_ref[...].astype(F32)) * branch_a + _sigmoid(gb_ref[...].astype(F32)) * branch_b
    h = _layer_norm(alpha * x + _dot(merged, wo_ref[...]), g_ref[...], b_ref[...])
    h_ref[...] = h
    _store_pieces(hp_ref, h)


def _post(pooled_a, pooled_b, o_a, o_b, proj, x_a, x_b, w_pool, pool_scale, w_br_b, w_br_a, w_out, ln_g, ln_b,
          alpha, tm):
    t = x_a.shape[0] + x_b.shape[0]
    tiles_a = x_a.shape[0] // tm
    row = lambda i: (i, 0)
    row_a = lambda i: (jnp.minimum(i, tiles_a - 1), 0)
    row_b = lambda i: (jnp.maximum(i - tiles_a, 0), 0)
    const2 = lambda i: (0, 0)
    return pl.pallas_call(
        functools.partial(_post_kernel, alpha=alpha, tiles_a=tiles_a),
        grid=(t // tm,),
        in_specs=[pl.BlockSpec((tm, D_MODEL), row_a),
                  pl.BlockSpec((tm, D_MODEL), row_b),
                  pl.BlockSpec((tm, V_W), row_a),
                  pl.BlockSpec((tm, V_W), row_b),
                  pl.BlockSpec((tm, D_MODEL), lambda i: (i, U_BLK + 1)),
                  pl.BlockSpec((tm, D_MODEL), lambda i: (i, U_BLK + 2)),
                  pl.BlockSpec((tm, D_MODEL), row_a),
                  pl.BlockSpec((tm, D_MODEL), row_b),
                  pl.BlockSpec(w_pool.shape, lambda i: (0, 0, 0), pipeline_mode=pl.Buffered(1)),
                  pl.BlockSpec((1, D_MODEL), const2),
                  pl.BlockSpec(w_br_b.shape, const2, pipeline_mode=pl.Buffered(1)),
                  pl.BlockSpec(w_br_a.shape, const2, pipeline_mode=pl.Buffered(1)),
                  pl.BlockSpec(w_out.shape, const2, pipeline_mode=pl.Buffered(1)),
                  pl.BlockSpec((1, D_MODEL), const2),
                  pl.BlockSpec((1, D_MODEL), const2)],
        out_specs=[pl.BlockSpec((tm, D_MODEL), row), pl.BlockSpec((N_PIECE, tm, PIECE_W), lambda i: (0, i, 0))],
        out_shape=[jax.ShapeDtypeStruct((t, D_MODEL), F32),
                   jax.ShapeDtypeStruct((N_PIECE, t, PIECE_W), jnp.uint32)],
        compiler_params=_cparams(("parallel",)),
        name="post_mixers",
    )(pooled_a, pooled_b, o_a, o_b, proj, proj, x_a, x_b, w_pool, pool_scale, w_br_b, w_br_a, w_out, ln_g, ln_b)


def _router_kernel(h_ref, wt_ref, bias_ref, idx_ref, wts_ref, rank_ref, cnt_ref, carry_scr, *, tm):
    i = pl.program_id(0)

    @pl.when(i == 0)
    def _init():
        carry_scr[...] = jnp.zeros(carry_scr.shape, F32)

    logits = _dot3_nt(wt_ref[...], h_ref[...])
    sc = _sigmoid(logits)
    ch = sc + bias_ref[...]
    neg = -jnp.inf
    e_in = lax.broadcasted_iota(jnp.int32, (GROUP_SZ, tm), 0)
    gs_rows = []
    for g in range(N_GROUP):
        blk = ch[g * GROUP_SZ:(g + 1) * GROUP_SZ]
        m1 = jnp.max(blk, axis=0, keepdims=True)
        i1 = jnp.min(jnp.where(blk == m1, e_in, GROUP_SZ), axis=0, keepdims=True)
        m2 = jnp.max(jnp.where(e_in == i1, neg, blk), axis=0, keepdims=True)
        gs_rows.append(m1 + m2)
    gs = jnp.concatenate(gs_rows, axis=0)
    g_io = lax.broadcasted_iota(jnp.int32, (N_GROUP, tm), 0)
    e_io = lax.broadcasted_iota(jnp.int32, (N_EXPERTS, tm), 0)
    e_grp = e_io >> int(math.log2(GROUP_SZ))
    masked = jnp.full((N_EXPERTS, tm), neg, F32)
    for _ in range(TOPK_GROUP):
        mx = jnp.max(gs, axis=0, keepdims=True)
        gi = jnp.min(jnp.where(gs == mx, g_io, N_GROUP), axis=0, keepdims=True)
        gs = jnp.where(g_io == gi, neg, gs)
        masked = jnp.where(e_grp == gi, ch, masked)
    idx_rows, w_rows = [], []
    onehot = jnp.zeros((N_EXPERTS, tm), F32)
    for _ in range(TOP_K):
        mx = jnp.max(masked, axis=0, keepdims=True)
        ei = jnp.min(jnp.where(masked == mx, e_io, N_EXPERTS), axis=0, keepdims=True)
        hit = e_io == ei
        idx_rows.append(ei)
        w_rows.append(jnp.sum(jnp.where(hit, sc, 0.0), axis=0, keepdims=True))
        onehot = jnp.where(hit, 1.0, onehot)
        masked = jnp.where(hit, neg, masked)
    wsel = jnp.concatenate(w_rows, axis=0)
    wts_ref[...] = wsel / jnp.sum(wsel, axis=0, keepdims=True) * ROUTED_SCALE
    idx_ref[...] = jnp.concatenate(idx_rows, axis=0)

    tr = lax.broadcasted_iota(jnp.int32, (tm, tm), 0)
    tc_ = lax.broadcasted_iota(jnp.int32, (tm, tm), 1)
    before = jnp.where(tr < tc_, 1.0, 0.0).astype(BF16)
    cum = jnp.dot(onehot.astype(BF16), before, preferred_element_type=F32) + carry_scr[...]
    rank_ref[...] = jnp.concatenate(
        [jnp.sum(jnp.where(e_io == ei, cum, 0.0), axis=0, keepdims=True) for ei in idx_rows],
        axis=0).astype(jnp.int32)
    carry_scr[...] = carry_scr[...] + jnp.sum(onehot, axis=1, keepdims=True)
    cnt_ref[...] = carry_scr[...]


def _router(h, w_router_t, bias_col, tm):
    t = h.shape[0]
    return pl.pallas_call(
        functools.partial(_router_kernel, tm=tm),
        grid=(t // tm,),
        in_specs=[pl.BlockSpec((tm, D_MODEL), lambda i: (i, 0)),
                  pl.BlockSpec((N_EXPERTS, D_MODEL), lambda i: (0, 0)),
                  pl.BlockSpec((N_EXPERTS, 1), lambda i: (0, 0))],
        out_specs=[pl.BlockSpec((TOP_K, tm), lambda i: (0, i)),
                   pl.BlockSpec((TOP_K, tm), lambda i: (0, i)),
                   pl.BlockSpec((TOP_K, tm), lambda i: (0, i)),
                   pl.BlockSpec((N_EXPERTS, 1), lambda i: (0, 0))],
        out_shape=[jax.ShapeDtypeStruct((TOP_K, t), jnp.int32),
                   jax.ShapeDtypeStruct((TOP_K, t), F32),
                   jax.ShapeDtypeStruct((TOP_K, t), jnp.int32),
                   jax.ShapeDtypeStruct((N_EXPERTS, 1), F32)],
        scratch_shapes=[pltpu.VMEM((N_EXPERTS, 1), F32)],
        compiler_params=_cparams(("arbitrary",)),
        name="router",
    )(h, w_router_t, bias_col)


def _dest_kernel(idx_ref, rank_ref, start_ref, dest_ref, *, tm, n_rows):
    e_io = lax.broadcasted_iota(jnp.int32, (N_EXPERTS, tm), 0)
    starts = start_ref[...]
    rows = []
    for k in range(TOP_K):
        seg = jnp.sum(jnp.where(e_io == idx_ref[k:k + 1, :], starts, 0.0), axis=0, keepdims=True)
        rows.append(seg.astype(jnp.int32) + rank_ref[k:k + 1, :])
    base = jnp.concatenate(rows, axis=0)
    for p in range(N_PIECE):
        dest_ref[p] = base + p * n_rows


def _dest(idx_t, rank_t, seg_start_col, n_rows, tm):
    t = idx_t.shape[1]
    blk = pl.BlockSpec((TOP_K, tm), lambda i: (0, i))
    return pl.pallas_call(
        functools.partial(_dest_kernel, tm=tm, n_rows=n_rows),
        grid=(t // tm,),
        in_specs=[blk, blk, pl.BlockSpec((N_EXPERTS, 1), lambda i: (0, 0))],
        out_specs=pl.BlockSpec((N_PIECE, TOP_K, tm), lambda i: (0, 0, i)),
        out_shape=jax.ShapeDtypeStruct((N_PIECE, TOP_K, t), jnp.int32),
        compiler_params=_cparams(("parallel",)),
        name="dispatch_rows",
    )(idx_t, rank_t, seg_start_col)


def _expert_kernel(be_ref, nu_ref, first_ref, nxt_ref, slot_ref, x_ref, wg_hbm, wu_hbm, wd_hbm, y_ref,
                   wg_buf, wu_buf, wd_buf, wgu_scr, wd_scr, sems):
    i = pl.program_id(0)
    live = i < nu_ref[0]

    def fetch(e, slot):
        return [pltpu.make_async_copy(wg_hbm.at[e], wg_buf.at[slot], sems.at[slot, 0]),
                pltpu.make_async_copy(wu_hbm.at[e], wu_buf.at[slot], sems.at[slot, 1]),
                pltpu.make_async_copy(wd_hbm.at[e], wd_buf.at[slot], sems.at[slot, 2])]

    @pl.when(live & (first_ref[i] == 1))
    def _new_expert():
        slot = slot_ref[i]

        @pl.when(i == 0)
        def _():
            for cp in fetch(be_ref[i], slot):
                cp.start()

        @pl.when(nxt_ref[i] >= 0)
        def _():
            for cp in fetch(nxt_ref[i], 1 - slot):
                cp.start()

        for cp in fetch(be_ref[i], slot):
            cp.wait()
        wgu_scr[:, :D_EXPERT] = wg_buf[slot].astype(BF16)
        wgu_scr[:, D_EXPERT:] = wu_buf[slot].astype(BF16)
        wd_scr[...] = wd_buf[slot].astype(BF16)

    @pl.when(live)
    def _():
        n_sub = 2
        sub = EXPERT_BM // n_sub
        xs = [jnp.concatenate([c.astype(BF16) for c in
                               _load_pieces([x_ref[p, s * sub:(s + 1) * sub] for p in range(N_PIECE)])], axis=1)
              for s in range(n_sub)]
        gus = [jnp.dot(x, wgu_scr[...], preferred_element_type=F32) for x in xs]
        acts = [(_silu(gu[:, :D_EXPERT]) * gu[:, D_EXPERT:]).astype(BF16) for gu in gus]
        ys = [jnp.dot(a, wd_scr[...], preferred_element_type=F32) for a in acts]
        for s, y in enumerate(ys):
            for p in range(N_PIECE):
                y_ref[p, s * sub:(s + 1) * sub] = _pack_halves(y[:, 2 * p * PIECE_W:2 * (p + 1) * PIECE_W])


def _experts(xs, blk_exp, n_used, blk_first, blk_next, blk_slot, w_gate, w_up, w_down):
    n_rows = xs.shape[1]
    n_blocks = n_rows // EXPERT_BM

    def row_block(i, be, nu, *_):
        return (0, jnp.minimum(i, nu[0] - 1), 0)

    grid_spec = pltpu.PrefetchScalarGridSpec(
        num_scalar_prefetch=5,
        grid=(n_blocks,),
        in_specs=[pl.BlockSpec((N_PIECE, EXPERT_BM, PIECE_W), row_block),
                  pl.BlockSpec(memory_space=pl.ANY),
                  pl.BlockSpec(memory_space=pl.ANY),
                  pl.BlockSpec(memory_space=pl.ANY)],
        out_specs=pl.BlockSpec((N_PIECE, EXPERT_BM, PIECE_W), row_block),
        scratch_shapes=[pltpu.VMEM((2, D_MODEL, D_EXPERT), F32),
                        pltpu.VMEM((2, D_MODEL, D_EXPERT), F32),
                        pltpu.VMEM((2, D_EXPERT, D_MODEL), F32),
                        pltpu.VMEM((D_MODEL, 2 * D_EXPERT), BF16),
                        pltpu.VMEM((D_EXPERT, D_MODEL), BF16),
                        pltpu.SemaphoreType.DMA((2, 3))],
    )
    return pl.pallas_call(
        _expert_kernel,
        grid_spec=grid_spec,
        out_shape=jax.ShapeDtypeStruct((N_PIECE, n_rows, PIECE_W), jnp.uint32),
        compiler_params=_cparams(("arbitrary",)),
        name="experts",
    )(blk_exp, n_used, blk_first, blk_next, blk_slot, xs, w_gate, w_up, w_down)


SC_WINDOW = 128
V7X_SC_CORES = 2
V7X_SC_SUBCORES = 16


def _sc_mesh():
    return plsc.VectorSubcoreMesh(core_axis_name="core", subcore_axis_name="subcore",
                                  num_cores=V7X_SC_CORES, num_subcores=V7X_SC_SUBCORES)


def _sc_scatter_rows(src, dest, n_rows, seg, repeat):
    d = src.shape[1]
    n_idx = dest.shape[0]
    seg_blocks = seg // SC_WINDOW
    dest2 = dest.reshape(1, n_idx)

    def src_block(i):
        return ((i // (repeat * seg_blocks)) * seg_blocks + i % seg_blocks, 0)

    @functools.partial(pl.kernel, out_type=jax.ShapeDtypeStruct((n_rows, d), src.dtype), mesh=_sc_mesh(),
                       scratch_types=[], name="sc_dispatch")
    def run(src_hbm, idx_hbm, out_hbm):
        def body(rows_vmem, idx_vmem):
            pltpu.sync_copy(rows_vmem, out_hbm.at[idx_vmem.at[0]])

        pltpu.emit_pipeline(
            body,
            grid=(n_idx // SC_WINDOW,),
            in_specs=[pl.BlockSpec((SC_WINDOW, d), src_block),
                      pl.BlockSpec((1, SC_WINDOW), lambda i: (0, i))],
            out_specs=[],
            core_axis_name=("core", "subcore"),
            dimension_semantics=(pltpu.PARALLEL,),
        )(src_hbm, idx_hbm)

    return run(src, dest2)


def _sc_gather_rows(table, idx):
    d = table.shape[1]
    n_idx = idx.shape[0]
    idx2 = idx.reshape(1, n_idx)

    @functools.partial(pl.kernel, out_type=jax.ShapeDtypeStruct((n_idx, d), table.dtype), mesh=_sc_mesh(),
                       scratch_types=[], name="sc_combine_gather")
    def run(table_hbm, idx_hbm, out_hbm):
        def body(idx_vmem, rows_vmem):
            pltpu.sync_copy(table_hbm.at[idx_vmem.at[0]], rows_vmem)

        pltpu.emit_pipeline(
            body,
            grid=(n_idx // SC_WINDOW,),
            in_specs=[pl.BlockSpec((1, SC_WINDOW), lambda i: (0, i))],
            out_specs=[pl.BlockSpec((SC_WINDOW, d), lambda i: (i, 0))],
            core_axis_name=("core", "subcore"),
            dimension_semantics=(pltpu.PARALLEL,),
        )(idx_hbm, out_hbm)

    return run(table, idx2)


def _combine_kernel(yg_ref, wts_ref, h_ref, wgu_ref, wd_ref, g_ref, b_ref, outa_ref, outb_ref, *, alpha, tiles_a):
    wts = wts_ref[...]
    acc = None
    for k in range(TOP_K):
        cols = _load_pieces([yg_ref[p, k] for p in range(N_PIECE)])
        wk = wts[:, k:k + 1]
        acc = [c * wk for c in cols] if acc is None else [a + c * wk for a, c in zip(acc, cols)]
    routed = jnp.concatenate(acc, axis=1)
    h = h_ref[...]
    gu = _dot(h, wgu_ref[...])
    shared = _dot(_silu(gu[:, :D_EXPERT]) * gu[:, D_EXPERT:], wd_ref[...])
    y = _layer_norm(alpha * h + (routed + shared), g_ref[...], b_ref[...])
    i = pl.program_id(0)

    @pl.when(i < tiles_a)
    def _():
        outa_ref[...] = y

    @pl.when(i >= tiles_a)
    def _():
        outb_ref[...] = y


def _combine(yg, wts, h, w_sh_gu, w_sh_down, ln_g, ln_b, alpha, tm, t_a):
    t = h.shape[0]
    tiles_a = t_a // tm
    row = lambda i: (i, 0)
    const2 = lambda i: (0, 0)
    return pl.pallas_call(
        functools.partial(_combine_kernel, alpha=alpha, tiles_a=tiles_a),
        grid=(t // tm,),
        in_specs=[pl.BlockSpec((N_PIECE, TOP_K, tm, PIECE_W), lambda i: (0, 0, i, 0)),
                  pl.BlockSpec((tm, TOP_K), row),
                  pl.BlockSpec((tm, D_MODEL), row),
                  pl.BlockSpec(w_sh_gu.shape, const2),
                  pl.BlockSpec(w_sh_down.shape, const2),
                  pl.BlockSpec((1, D_MODEL), const2),
                  pl.BlockSpec((1, D_MODEL), const2)],
        out_specs=[pl.BlockSpec((tm, D_MODEL), lambda i: (jnp.minimum(i, tiles_a - 1), 0)),
                   pl.BlockSpec((tm, D_MODEL), lambda i: (jnp.maximum(i - tiles_a, 0), 0))],
        out_shape=[jax.ShapeDtypeStruct((t_a, D_MODEL), F32), jax.ShapeDtypeStruct((t - t_a, D_MODEL), F32)],
        compiler_params=_cparams(("arbitrary",)),
        name="combine_ln2",
    )(yg, wts, h, w_sh_gu, w_sh_down, ln_g, ln_b)


def _layer(xp, xs, s_delta, s_conv, s_pool, w_in, conv_w, a_log, dt_bias, gamma_a, w_br_a, w_pool,
           pool_scale, w_br_b, w_out, ln1_g, ln1_b, w_router, router_bias, w_exp_gate, w_exp_up,
           w_exp_down, w_sh_gate, w_sh_up, w_sh_down, ln2_g, ln2_b, alpha):
    Bp, Lp, _ = xp.shape
    Bs, Ls, _ = xs.shape
    Tp, Ts = Bp * Lp, Bs * Ls
    T = Tp + Ts
    x_p = xp.reshape(Tp, D_MODEL)
    x_s = xs.reshape(Ts, D_MODEL)

    o_b, o_u = QKV_W + V_W, QKV_W + V_W + 2 * H_V
    w_main = jnp.concatenate([w_in[:, :o_b], w_in[:, o_u:]], axis=1).astype(BF16)
    w_ba = jnp.pad(w_in[:, o_b:o_u], ((0, 0), (0, 128 - 2 * H_V))).astype(BF16)
    proj, ba = _in_proj(x_p, x_s, w_main, w_ba, IN_PROJ_TM, IN_PROJ_TN)

    o_p, sd_p = _gdn(proj, ba, conv_w, a_log, dt_bias, gamma_a, 0, Bp, Lp, GDN_CHUNK, 1, GDN_FRESH_CHUNKS)
    o_s, sd_s = _gdn(proj, ba, conv_w, a_log, dt_bias, gamma_a, Tp, Bs, Ls, Ls, GDN_STATE_SEQS, 1,
                     conv_buf=s_conv, s0=s_delta)

    pooled_p = _pool(proj, 0, Bp, Lp, POOL_TILE, 1, 0)
    pooled_s = _pool(proj, Tp, Bs, Ls, Ls, POOL_STATE_SEQS, PAST_LEN, pool_buf=s_pool)

    h, hp = _post(pooled_p, pooled_s, o_p, o_s, proj, x_p, x_s, w_pool.astype(BF16), pool_scale.reshape(1, D_MODEL),
                  w_br_b.astype(BF16), w_br_a.astype(BF16), w_out.astype(BF16),
                  ln1_g.reshape(1, D_MODEL), ln1_b.reshape(1, D_MODEL), alpha, POST_TM)

    idx_t, wts_t, rank_t, cnt = _router(h, w_router.T, router_bias.reshape(N_EXPERTS, 1), ROUTER_TM)
    counts = cnt[:, 0].astype(jnp.int32)
    padded = ((counts + EXPERT_BM - 1) // EXPERT_BM) * EXPERT_BM
    pends = jnp.cumsum(padded)
    pstarts = pends - padded
    n_blocks = (T * TOP_K + N_EXPERTS * (EXPERT_BM - 1) + EXPERT_BM - 1) // EXPERT_BM
    n_rows = n_blocks * EXPERT_BM
    piece_rows = _dest(idx_t, rank_t, pstarts.astype(F32).reshape(N_EXPERTS, 1), n_rows, DEST_TM)
    blk_start = jnp.arange(n_blocks, dtype=jnp.int32) * EXPERT_BM
    blk_exp = jnp.minimum(jnp.sum((pends[None, :] <= blk_start[:, None]).astype(jnp.int32), axis=1),
                          N_EXPERTS - 1)
    n_used = (pends[-1:] // EXPERT_BM).astype(jnp.int32)
    has_rows = counts > 0
    e_ids = jnp.arange(N_EXPERTS, dtype=jnp.int32)
    slot_of = (jnp.cumsum(has_rows.astype(jnp.int32)) - 1) & 1
    later = jnp.where(has_rows, e_ids, N_EXPERTS)
    next_of = jnp.concatenate([lax.cummin(later, reverse=True)[1:], jnp.full((1,), N_EXPERTS, jnp.int32)])
    next_of = jnp.where(next_of < N_EXPERTS, next_of, -1)
    blk_first = (blk_start == pstarts[blk_exp]).astype(jnp.int32)
    blk_next = next_of[blk_exp]
    blk_slot = slot_of[blk_exp]

    piece_idx = piece_rows.reshape(N_PIECE * TOP_K * T)
    x_sorted = _sc_scatter_rows(hp.reshape(N_PIECE * T, PIECE_W), piece_idx, N_PIECE * n_rows, T, TOP_K)
    y_sorted = _experts(x_sorted.reshape(N_PIECE, n_rows, PIECE_W), blk_exp, n_used, blk_first, blk_next,
                        blk_slot, w_exp_gate, w_exp_up, w_exp_down)
    yg = _sc_gather_rows(y_sorted.reshape(N_PIECE * n_rows, PIECE_W), piece_idx)
    yg = yg.reshape(N_PIECE, TOP_K, T, PIECE_W)

    w_sh_gu = jnp.concatenate([w_sh_gate, w_sh_up], axis=1).astype(BF16)
    y_p, y_s = _combine(yg, wts_t.T, h, w_sh_gu, w_sh_down.astype(BF16),
                        ln2_g.reshape(1, D_MODEL), ln2_b.reshape(1, D_MODEL), alpha, COMBINE_TM, Tp)

    def carried_rows(keep, col0, width, name, old_p, old_s):
        parts, counts = [], []
        for x2d, B, L in ((x_p, Bp, Lp), (x_s, Bs, Ls)):
            n = min(L, keep)
            rows = jnp.arange(B, dtype=jnp.int32)[:, None] * L + (L - n) + jnp.arange(n, dtype=jnp.int32)
            parts.append(jnp.take(x2d, rows.reshape(-1), axis=0).astype(BF16))
            counts.append((B, n))
        n_tail = sum(B * n for B, n in counts)
        n_pad = -(-n_tail // 128) * 128
        xt = jnp.pad(jnp.concatenate(parts, axis=0), ((0, n_pad - n_tail), (0, 0)))
        tail = _matmul(xt, w_main, col0, width, n_pad, 1024, F32, name)
        outs, r0 = [], 0
        for (B, n), old in zip(counts, (old_p, old_s)):
            new = tail[r0:r0 + B * n].reshape(B, n, width)
            outs.append(new if n == keep else jnp.concatenate([old[:, n:], new], axis=1))
            r0 += B * n
        return outs

    new_conv_p, new_conv_s = carried_rows(CONV_W - 1, 0, QKV_W, "in_proj_conv_rows",
                                          jnp.zeros((Bp, CONV_W - 1, QKV_W), F32), s_conv)
    new_pool_p, new_pool_s = carried_rows(POOL_BUF, U_BLK * D_MODEL, D_MODEL, "in_proj_pool_rows",
                                          jnp.zeros((Bp, POOL_BUF, D_MODEL), F32), s_pool)
    return (y_p.reshape(Bp, Lp, D_MODEL), y_s.reshape(Bs, Ls, D_MODEL),
            sd_p, new_conv_p, new_pool_p, sd_s, new_conv_s, new_pool_s)


def kernel(x_prompt, x_sample, state_delta, state_conv, state_pool, w_in, conv_w, a_log, dt_bias, gamma_a,
           w_br_a, w_pool, pool_scale, w_br_b, w_out, ln1_g, ln1_b, w_router, router_bias,
           w_exp_gate, w_exp_up, w_exp_down, w_sh_gate, w_sh_up, w_sh_down, ln2_g, ln2_b):
    depth = w_in.shape[0]
    alpha = (2 * depth) ** 0.25
    yp, ys = x_prompt, x_sample
    outs = [[] for _ in range(6)]
    for l in range(depth):
        res = _layer(yp, ys, state_delta[l], state_conv[l], state_pool[l], w_in[l], conv_w[l], a_log[l],
                     dt_bias[l], gamma_a[l], w_br_a[l], w_pool[l], pool_scale[l], w_br_b[l], w_out[l],
                     ln1_g[l], ln1_b[l], w_router[l], router_bias[l], w_exp_gate[l], w_exp_up[l],
                     w_exp_down[l], w_sh_gate[l], w_sh_up[l], w_sh_down[l], ln2_g[l], ln2_b[l], alpha)
        yp, ys = res[0], res[1]
        for lst, v in zip(outs, res[2:]):
            lst.append(v)
    return (yp, ys) + tuple(jnp.stack(v) for v in outs)
```

```python
import functools
import math

import jax
import jax.numpy as jnp
from jax import lax
from jax.experimental import pallas as pl
from jax.experimental.pallas import tpu as pltpu
from jax.experimental.pallas import tpu_sc as plsc

F32 = jnp.float32
BF16 = jnp.bfloat16

D_MODEL = 1024
H_K = 8
D_K = 128
H_V = 16
D_V = 128
Q_W = H_K * D_K
V_W = H_V * D_V
QKV_W = 2 * Q_W + V_W
CONV_W = 4
POOL_WINDOWS = (2, 4, 8, 16)
POOL_GROUP_W = D_MODEL // len(POOL_WINDOWS)
POOL_BUF = max(POOL_WINDOWS) - 1
N_EXPERTS = 256
TOP_K = 8
N_GROUP = 8
TOPK_GROUP = 4
GROUP_SZ = N_EXPERTS // N_GROUP
D_EXPERT = D_MODEL // 4
ROUTED_SCALE = 2.5
LN_EPS = 1e-5
NORM_EPS = 1e-6
PAST_LEN = 16384

MAIN_W = QKV_W + V_W + 3 * D_MODEL
Z_BLK = QKV_W // V_W
U_BLK = (QKV_W + V_W) // D_MODEL

IN_PROJ_TM = 1024
IN_PROJ_TN = MAIN_W // 3
GDN_CHUNK = 64
POOL_TILE = 2048
POST_TM = 512
ROUTER_TM = 1024
DEST_TM = 1024
COMBINE_TM = 512

STACK_ROWS = 128
SOLVE_BASE = 16
GDN_FRESH_CHUNKS = 4
EXPERT_BM = 512
GDN_STATE_SEQS = 8
POOL_STATE_SEQS = 16
VMEM_LIMIT = 56 * 1024 * 1024


def _cparams(sem):
    return pltpu.CompilerParams(dimension_semantics=sem, vmem_limit_bytes=VMEM_LIMIT)


def _sigmoid(x):
    return 0.5 * jnp.tanh(0.5 * x) + 0.5


def _silu(x):
    return x * _sigmoid(x)


def _softplus(x):
    return jnp.maximum(x, 0.0) + jnp.log(1.0 + jnp.exp(-jnp.abs(x)))


def _dot(a, b):
    return jnp.dot(a.astype(BF16), b.astype(BF16), preferred_element_type=F32)


def _dot_nt(a, b):
    return lax.dot_general(a.astype(BF16), b.astype(BF16), (((1,), (1,)), ((), ())),
                           preferred_element_type=F32)


def _dot_tn(a, b):
    return lax.dot_general(a.astype(BF16), b.astype(BF16), (((0,), (0,)), ((), ())),
                           preferred_element_type=F32)


def _split(a):
    hi = a.astype(BF16)
    lo = (a - hi.astype(F32)).astype(BF16)
    return hi, lo


def _dot3_nt(a, b):
    ah, al = _split(a)
    bh, bl = _split(b)
    d = functools.partial(lax.dot_general, dimension_numbers=(((1,), (1,)), ((), ())),
                          preferred_element_type=F32)
    return d(ah, bh) + (d(ah, bl) + d(al, bh))


def _pack_halves(x):
    n = x.shape[1] // 2
    hi = lax.bitcast_convert_type(x[:, :n].astype(BF16).astype(F32), jnp.uint32)
    lo = lax.bitcast_convert_type(x[:, n:].astype(BF16).astype(F32), jnp.uint32)
    return (hi & jnp.uint32(0xFFFF0000)) | (lo >> 16)


def _unpack_halves(w):
    hi = lax.bitcast_convert_type(w & jnp.uint32(0xFFFF0000), F32)
    lo = lax.bitcast_convert_type(w << 16, F32)
    return hi, lo


N_PIECE = 2
PIECE_W = D_MODEL // (2 * N_PIECE)


def _store_pieces(ref, x):
    for p in range(N_PIECE):
        ref[p] = _pack_halves(x[:, 2 * p * PIECE_W:2 * (p + 1) * PIECE_W])


def _load_pieces(pieces):
    cols = []
    for w in pieces:
        cols.extend(_unpack_halves(w))
    return cols


def _layer_norm(x, g, b):
    mu = jnp.mean(x, axis=-1, keepdims=True)
    xc = x - mu
    var = jnp.mean(xc * xc, axis=-1, keepdims=True)
    return xc * lax.rsqrt(var + LN_EPS) * g + b


def _in_proj_kernel(xa_ref, xb_ref, w_ref, wba_ref, o_ref, ba_ref, x_scr, *, tiles_a):
    @pl.when(pl.program_id(1) == 0)
    def _():
        x = jnp.where(pl.program_id(0) < tiles_a, xa_ref[...], xb_ref[...]).astype(BF16)
        x_scr[...] = x
        ba_ref[...] = jnp.dot(x, wba_ref[...], preferred_element_type=F32)

    o_ref[...] = jnp.dot(x_scr[...], w_ref[...], preferred_element_type=F32).astype(o_ref.dtype)


def _in_proj(x_a, x_b, w_main, w_ba, tm, tn):
    k = x_a.shape[1]
    t = x_a.shape[0] + x_b.shape[0]
    n = w_main.shape[1]
    tiles_a = x_a.shape[0] // tm
    return pl.pallas_call(
        functools.partial(_in_proj_kernel, tiles_a=tiles_a),
        grid=(t // tm, n // tn),
        in_specs=[pl.BlockSpec((tm, k), lambda i, j: (jnp.minimum(i, tiles_a - 1), 0)),
                  pl.BlockSpec((tm, k), lambda i, j: (jnp.maximum(i - tiles_a, 0), 0)),
                  pl.BlockSpec((k, tn), lambda i, j: (0, j)),
                  pl.BlockSpec(w_ba.shape, lambda i, j: (0, 0))],
        out_specs=[pl.BlockSpec((tm, tn), lambda i, j: (i, j)),
                   pl.BlockSpec((tm, w_ba.shape[1]), lambda i, j: (i, 0))],
        out_shape=[jax.ShapeDtypeStruct((t, n), BF16), jax.ShapeDtypeStruct((t, w_ba.shape[1]), F32)],
        scratch_shapes=[pltpu.VMEM((tm, k), BF16)],
        compiler_params=_cparams(("parallel", "arbitrary")),
        name="in_proj",
    )(x_a, x_b, w_main, w_ba)


def _mm_kernel(x_ref, w_ref, o_ref):
    o_ref[...] = jnp.dot(x_ref[...], w_ref[...], preferred_element_type=F32).astype(o_ref.dtype)


def _matmul(x, w, col0, n, tm, tn, out_dtype, name):
    t, k = x.shape
    cb0 = col0 // tn
    return pl.pallas_call(
        _mm_kernel,
        grid=(t // tm, n // tn),
        in_specs=[pl.BlockSpec((tm, k), lambda i, j: (i, 0)),
                  pl.BlockSpec((k, tn), lambda i, j: (0, cb0 + j))],
        out_specs=pl.BlockSpec((tm, tn), lambda i, j: (i, j)),
        out_shape=jax.ShapeDtypeStruct((t, n), out_dtype),
        compiler_params=_cparams(("parallel", "parallel")),
        name=name,
    )(x, w)


def _gdn_kernel(*refs, C, hg, nseq, nck, has_state):
    if has_state:
        (qkv_ref, z_ref, ba_ref, cw_ref, ad_ref, gam_ref, buf_ref, s0_ref,
         o_ref, sout_ref, s_scr, tail_scr, act_scr) = refs
    else:
        (qkv_ref, z_ref, ba_ref, cw_ref, ad_ref, gam_ref,
         o_ref, sout_ref, s_scr, tail_scr, act_scr) = refs
    R = hg * C
    ng = H_V // hg
    log2c = int(math.log2(C))
    c = pl.program_id(1)

    @pl.when(c == 0)
    def _init():
        if has_state:
            tail_scr[...] = jnp.zeros(tail_scr.shape, F32)
            for s in range(nseq):
                s_scr[s * H_V:(s + 1) * H_V] = s0_ref[s]
                tail_scr[(s + 1) * 8 - (CONV_W - 1):(s + 1) * 8] = buf_ref[s]
        else:
            s_scr[...] = jnp.zeros(s_scr.shape, F32)
            tail_scr[...] = jnp.zeros(tail_scr.shape, F32)

    cw = cw_ref[...]
    qkv_all = qkv_ref[...].astype(F32)
    n_prev = CONV_W - 1
    CB = nck * C
    for s in range(nseq):
        x = qkv_all[s * CB:(s + 1) * CB]
        xe = jnp.concatenate([tail_scr[s * 8:(s + 1) * 8], x], axis=0)
        y = x * cw[n_prev:CONV_W]
        for j in range(n_prev):
            y = y + xe[8 - n_prev + j:8 - n_prev + j + CB] * cw[j:j + 1]
        tail_scr[s * 8:(s + 1) * 8] = x[CB - 8:CB]
        act_scr[s * CB:(s + 1) * CB] = _silu(y)
        for off, scale in ((0, D_K ** -0.5), (Q_W, 1.0)):
            for kh in range(H_K):
                cols = slice(off + kh * D_K, off + (kh + 1) * D_K)
                v = act_scr[s * CB:(s + 1) * CB, cols]
                act_scr[s * CB:(s + 1) * CB, cols] = v * (
                    lax.rsqrt(jnp.sum(v * v, axis=-1, keepdims=True) + NORM_EPS) * scale)

    ri = lax.broadcasted_iota(jnp.int32, (R, R), 0)
    ci = lax.broadcasted_iota(jnp.int32, (R, R), 1)
    same = (ri >> log2c) == (ci >> log2c)
    incl = same & (ri >= ci)
    strict = same & (ri > ci)
    eye = ri == ci
    is_last = ci == (((ri >> log2c) << log2c) + (C - 1))
    eye_f = jnp.where(eye, 1.0, 0.0).astype(F32)
    cum_u = jnp.where(same & (ri <= ci), 1.0, 0.0).astype(BF16)

    nrow = ba_ref.shape[2] // 2
    adv = ad_ref[...]
    dd = functools.partial(jnp.dot, preferred_element_type=F32)
    beta_of, G_of = {}, {}
    for s in range(nseq):
        for k in range(nck):
            bav = ba_ref[s, k]
            beta_of[s, k] = _sigmoid(bav[0:nrow])
            g_rows = -jnp.exp(adv[0:nrow]) * _softplus(bav[nrow:2 * nrow] + adv[nrow:2 * nrow])
            g1 = g_rows.astype(BF16)
            r1 = g_rows - g1.astype(F32)
            g2 = r1.astype(BF16)
            g3 = (r1 - g2.astype(F32)).astype(BF16)
            G_of[s, k] = dd(g1, cum_u) + (dd(g2, cum_u) + dd(g3, cum_u))

    gam = gam_ref[...]
    rep = H_V // H_K

    def stacked(d, off, per_k_head):
        blocks = [(h // rep if per_k_head else h) for h in d["heads"]]
        return jnp.concatenate([act_scr[d["rows"], off + b * D_K:off + (b + 1) * D_K] for b in blocks], axis=0)

    log2b = min(log2c, int(math.log2(SOLVE_BASE)))
    same_base = ((ri ^ ci) >> log2b) == 0
    row_head = lax.broadcasted_iota(jnp.int32, (R, D_V), 0) >> log2c
    z_all = z_ref[...].astype(F32)
    grp = [dict(s=s, k=k, g=g, rows=slice(s * CB + k * C, s * CB + (k + 1) * C), s0=s * H_V,
                heads=[g * hg + hh for hh in range(hg)])
           for k in range(nck) for s in range(nseq) for g in range(ng)]
    for d in grp:
        kst = stacked(d, Q_W, True)
        d["kk"] = _dot_nt(kst, kst)
        d["qk"] = _dot_nt(stacked(d, 0, True), kst)
    for d in grp:
        grow = G_of[d["s"], d["k"]][d["g"]:d["g"] + 1]
        brow = beta_of[d["s"], d["k"]][d["g"]:d["g"] + 1]
        gcol = jnp.sum(eye_f * grow, axis=1, keepdims=True)
        bcol = jnp.sum(eye_f * brow, axis=1, keepdims=True)
        glast = jnp.sum(jnp.where(is_last, grow, 0.0), axis=1, keepdims=True)
        dm = jnp.where(incl, jnp.exp(jnp.minimum(gcol - grow, 0.0)), 0.0)
        d.update(gcol=gcol, bcol=bcol, glast=glast, eg=jnp.exp(gcol))
        d["m"] = jnp.where(strict, -(d["kk"] * bcol * dm), 0.0)
        d["attn"] = jnp.where(incl, d["qk"] * dm, 0.0)
        d["pw"] = jnp.where(same_base, d["m"], 0.0)
        d["t"] = eye_f + d["pw"]

    for r in range(log2b):
        last = r == log2b - 1
        for d in grp:
            if r == 0:
                if not last:
                    d["pw"] = _dot(d["pw"], d["pw"])
            elif last:
                d["t"] = d["t"] + _dot(d["pw"], d["t"])
            else:
                out = _dot(d["pw"], jnp.concatenate([d["pw"], d["t"]], axis=1))
                d["pw"] = out[:, :R]
                d["t"] = d["t"] + out[:, R:]
    for lev in range(log2b + 1, log2c + 1):
        coupling = ((ri ^ ci) >> (lev - 1)) == 1
        for d in grp:
            d["y"] = _dot(d["t"], jnp.where(coupling, d["m"], 0.0))
        for d in grp:
            d["t"] = d["t"] + _dot(d["y"], d["t"])
    for d in grp:
        rhs = jnp.concatenate([stacked(d, 2 * Q_W, False) * d["bcol"],
                               stacked(d, Q_W, True) * (d["bcol"] * d["eg"])], axis=1)
        d["x"] = _dot(d["t"], rhs)

    for k in range(nck):
        now = [d for d in grp if d["k"] == k]
        for d in now:
            qe = stacked(d, 0, True) * d["eg"]
            d["xs"] = [_dot(jnp.concatenate([d["x"][hh * C:(hh + 1) * C, D_V:], qe[hh * C:(hh + 1) * C]], axis=0),
                            s_scr[d["s0"] + h]) for hh, h in enumerate(d["heads"])]
        for d in now:
            d["vnew"] = jnp.concatenate(
                [d["x"][hh * C:(hh + 1) * C, :D_V] - xs[:C] for hh, xs in enumerate(d["xs"])], axis=0)
            d["o"] = jnp.concatenate([xs[C:] for xs in d["xs"]], axis=0) + _dot(d["attn"], d["vnew"])
        for d in now:
            kt = stacked(d, Q_W, True) * jnp.exp(d["glast"] - d["gcol"])
            egl = jnp.exp(d["glast"])
            for hh, h in enumerate(d["heads"]):
                sl = slice(hh * C, (hh + 1) * C)
                if C >= 16:
                    upd = _dot_tn(kt[sl], d["vnew"][sl])
                else:
                    upd = _dot_tn(kt, jnp.where(row_head == hh, d["vnew"], 0.0))
                s_scr[d["s0"] + h] = s_scr[d["s0"] + h] * egl[hh * C:hh * C + 1] + upd

    for d in grp:
        o = d["o"]
        zst = jnp.concatenate([z_all[d["rows"], h * D_V:(h + 1) * D_V] for h in d["heads"]], axis=0)
        on = o * lax.rsqrt(jnp.mean(o * o, axis=-1, keepdims=True) + NORM_EPS) * gam * _silu(zst)
        for hh, h in enumerate(d["heads"]):
            o_ref[d["rows"], h * D_V:(h + 1) * D_V] = on[hh * C:(hh + 1) * C].astype(o_ref.dtype)

    @pl.when(c == pl.num_programs(1) - 1)
    def _fin():
        for s in range(nseq):
            sout_ref[s] = s_scr[s * H_V:(s + 1) * H_V]


def _gdn(proj, ba, conv_w, a_log, dt_bias, gamma_a, row0, B, L, C, nseq, nck, conv_buf=None, s0=None):
    has_state = s0 is not None
    hg = STACK_ROWS // C
    ng = H_V // hg
    nrow = 16
    nc = L // C
    R = STACK_ROWS
    assert nseq == 1 or nc == nck
    rows_blk = nseq * nck * C
    steps = nc // nck

    def arrange(v):
        v = v.reshape(B, nc, C, ng, hg).transpose(0, 1, 3, 4, 2).reshape(B, nc, ng, R)
        return jnp.pad(v, ((0, 0), (0, 0), (0, nrow - ng), (0, 0)))

    rows = ba[row0:row0 + B * L]
    ba_arr = jnp.concatenate([arrange(rows[:, :H_V]), arrange(rows[:, H_V:2 * H_V])], axis=2)

    def arrange_param(p):
        v = jnp.broadcast_to(p.reshape(ng, hg, 1), (ng, hg, C)).reshape(ng, R)
        return jnp.pad(v, ((0, nrow - ng), (0, 0)))

    ad = jnp.concatenate([arrange_param(a_log), arrange_param(dt_bias)], axis=0)
    rb0 = row0 // rows_blk
    in_specs = [
        pl.BlockSpec((rows_blk, QKV_W), lambda b, c: (rb0 + b * steps + c, 0)),
        pl.BlockSpec((rows_blk, V_W), lambda b, c: (rb0 + b * steps + c, Z_BLK)),
        pl.BlockSpec((nseq, nck, 2 * nrow, R), lambda b, c: (b, c, 0, 0)),
        pl.BlockSpec((CONV_W, QKV_W), lambda b, c: (0, 0)),
        pl.BlockSpec((2 * nrow, R), lambda b, c: (0, 0)),
        pl.BlockSpec((1, D_V), lambda b, c: (0, 0)),
    ]
    args = [proj, proj, ba_arr, conv_w, ad, gamma_a.reshape(1, D_V)]
    if has_state:
        in_specs += [pl.BlockSpec((nseq, CONV_W - 1, QKV_W), lambda b, c: (b, 0, 0)),
                     pl.BlockSpec((nseq, H_V, D_K, D_V), lambda b, c: (b, 0, 0, 0))]
        args += [conv_buf, s0]
    return pl.pallas_call(
        functools.partial(_gdn_kernel, C=C, hg=hg, nseq=nseq, nck=nck, has_state=has_state),
        grid=(B // nseq, steps),
        in_specs=in_specs,
        out_specs=[pl.BlockSpec((rows_blk, V_W), lambda b, c: (b * steps + c, 0)),
                   pl.BlockSpec((nseq, H_V, D_K, D_V), lambda b, c: (b, 0, 0, 0))],
        out_shape=[jax.ShapeDtypeStruct((B * L, V_W), BF16),
                   jax.ShapeDtypeStruct((B, H_V, D_K, D_V), F32)],
        scratch_shapes=[pltpu.VMEM((nseq * H_V, D_K, D_V), F32),
                        pltpu.VMEM((nseq * 8, QKV_W), F32),
                        pltpu.VMEM((rows_blk, QKV_W), F32)],
        compiler_params=_cparams(("parallel", "arbitrary")),
        name="gdn_state" if has_state else "gdn_fresh",
    )(*args)


def _pool_kernel(*refs, tc, nseq, start_pos, has_state):
    if has_state:
        u_ref, buf_ref, o_ref, tail_scr = refs
    else:
        u_ref, o_ref, tail_scr = refs
    hist = POOL_BUF + 1
    c = pl.program_id(1)

    @pl.when(c == 0)
    def _init():
        tail_scr[...] = jnp.zeros(tail_scr.shape, F32)
        if has_state:
            for s in range(nseq):
                tail_scr[s * hist + 1:(s + 1) * hist] = buf_ref[s]

    u_all = u_ref[...].astype(F32)
    pos1 = (start_pos + 1 + c * tc + lax.broadcasted_iota(jnp.int32, (tc, 1), 0)).astype(F32)
    seq_out = []
    for s in range(nseq):
        u = u_all[s * tc:(s + 1) * tc]
        xe = jnp.concatenate([tail_scr[s * hist:(s + 1) * hist], u], axis=0)
        tail_scr[s * hist:(s + 1) * hist] = xe[tc:tc + hist]
        parts = []
        for gi, win in enumerate(POOL_WINDOWS):
            sl = slice(gi * POOL_GROUP_W, (gi + 1) * POOL_GROUP_W)
            acc = xe[:, sl]
            shift = 1
            while shift < win:
                acc = acc + pltpu.roll(acc, shift, 0)
                shift *= 2
            cnt = jnp.minimum(float(win), pos1)
            parts.append(acc[hist:hist + tc] / cnt - u[:, sl])
        seq_out.append(jnp.concatenate(parts, axis=1))
    o_ref[...] = jnp.concatenate(seq_out, axis=0).astype(o_ref.dtype)


def _pool(proj, row0, B, L, tc, nseq, start_pos, pool_buf=None):
    has_state = pool_buf is not None
    nc = L // tc
    assert nseq == 1 or nc == 1
    rows_blk = nseq * tc
    rb0 = row0 // rows_blk
    in_specs = [pl.BlockSpec((rows_blk, D_MODEL), lambda b, c: (rb0 + b * nc + c, U_BLK))]
    args = [proj]
    if has_state:
        in_specs.append(pl.BlockSpec((nseq, POOL_BUF, D_MODEL), lambda b, c: (b, 0, 0)))
        args.append(pool_buf)
    return pl.pallas_call(
        functools.partial(_pool_kernel, tc=tc, nseq=nseq, start_pos=start_pos, has_state=has_state),
        grid=(B // nseq, nc),
        in_specs=in_specs,
        out_specs=pl.BlockSpec((rows_blk, D_MODEL), lambda b, c: (b * nc + c, 0)),
        out_shape=jax.ShapeDtypeStruct((B * L, D_MODEL), BF16),
        scratch_shapes=[pltpu.VMEM((nseq * (POOL_BUF + 1), D_MODEL), F32)],
        compiler_params=_cparams(("parallel", "arbitrary")),
        name="pool_state" if has_state else "pool_fresh",
    )(*args)


def _post_kernel(pa_ref, pb_ref, oa_ref, ob_ref, ga_ref, gb_ref, xa_ref, xb_ref, wp_ref, ps_ref, wbb_ref, wba_ref,
                 wo_ref, g_ref, b_ref, h_ref, hp_ref, *, alpha, tiles_a):
    first = pl.program_id(0) < tiles_a
    x = jnp.where(first, xa_ref[...], xb_ref[...])
    pooled = jnp.where(first, pa_ref[...], pb_ref[...])
    o_gated = jnp.where(first, oa_ref[...], ob_ref[...])
    mixed = jnp.concatenate(
        [_dot(pooled[:, gi * POOL_GROUP_W:(gi + 1) * POOL_GROUP_W], wp_ref[gi]) for gi in range(len(POOL_WINDOWS))],
        axis=1) * ps_ref[...]
    branch_b = _dot(mixed, wbb_ref[...])
    branch_a = _dot(o_gated, wba_ref[...])
    merged = _sigmoid(ga_ref[...].astype(F32)) * branch_a + _sigmoid(gb_ref[...].astype(F32)) * branch_b
    h = _layer_norm(alpha * x + _dot(merged, wo_ref[...]), g_ref[...], b_ref[...])
    h_ref[...] = h
    _store_pieces(hp_ref, h)


def _post(pooled_a, pooled_b, o_a, o_b, proj, x_a, x_b, w_pool, pool_scale, w_br_b, w_br_a, w_out, ln_g, ln_b,
          alpha, tm):
    t = x_a.shape[0] + x_b.shape[0]
    tiles_a = x_a.shape[0] // tm
    row = lambda i: (i, 0)
    row_a = lambda i: (jnp.minimum(i, tiles_a - 1), 0)
    row_b = lambda i: (jnp.maximum(i - tiles_a, 0), 0)
    const2 = lambda i: (0, 0)
    return pl.pallas_call(
        functools.partial(_post_kernel, alpha=alpha, tiles_a=tiles_a),
        grid=(t // tm,),
        in_specs=[pl.BlockSpec((tm, D_MODEL), row_a),
                  pl.BlockSpec((tm, D_MODEL), row_b),
                  pl.BlockSpec((tm, V_W), row_a),
                  pl.BlockSpec((tm, V_W), row_b),
                  pl.BlockSpec((tm, D_MODEL), lambda i: (i, U_BLK + 1)),
                  pl.BlockSpec((tm, D_MODEL), lambda i: (i, U_BLK + 2)),
                  pl.BlockSpec((tm, D_MODEL), row_a),
                  pl.BlockSpec((tm, D_MODEL), row_b),
                  pl.BlockSpec(w_pool.shape, lambda i: (0, 0, 0), pipeline_mode=pl.Buffered(1)),
                  pl.BlockSpec((1, D_MODEL), const2),
                  pl.BlockSpec(w_br_b.shape, const2, pipeline_mode=pl.Buffered(1)),
                  pl.BlockSpec(w_br_a.shape, const2, pipeline_mode=pl.Buffered(1)),
                  pl.BlockSpec(w_out.shape, const2, pipeline_mode=pl.Buffered(1)),
                  pl.BlockSpec((1, D_MODEL), const2),
                  pl.BlockSpec((1, D_MODEL), const2)],
        out_specs=[pl.BlockSpec((tm, D_MODEL), row), pl.BlockSpec((N_PIECE, tm, PIECE_W), lambda i: (0, i, 0))],
        out_shape=[jax.ShapeDtypeStruct((t, D_MODEL), F32),
                   jax.ShapeDtypeStruct((N_PIECE, t, PIECE_W), jnp.uint32)],
        compiler_params=_cparams(("parallel",)),
        name="post_mixers",
    )(pooled_a, pooled_b, o_a, o_b, proj, proj, x_a, x_b, w_pool, pool_scale, w_br_b, w_br_a, w_out, ln_g, ln_b)


def _router_kernel(h_ref, wt_ref, bias_ref, idx_ref, wts_ref, rank_ref, cnt_ref, carry_scr, *, tm):
    i = pl.program_id(0)

    @pl.when(i == 0)
    def _init():
        carry_scr[...] = jnp.zeros(carry_scr.shape, F32)

    logits = _dot3_nt(wt_ref[...], h_ref[...])
    sc = _sigmoid(logits)
    ch = sc + bias_ref[...]
    neg = -jnp.inf
    e_in = lax.broadcasted_iota(jnp.int32, (GROUP_SZ, tm), 0)
    gs_rows = []
    for g in range(N_GROUP):
        blk = ch[g * GROUP_SZ:(g + 1) * GROUP_SZ]
        m1 = jnp.max(blk, axis=0, keepdims=True)
        i1 = jnp.min(jnp.where(blk == m1, e_in, GROUP_SZ), axis=0, keepdims=True)
        m2 = jnp.max(jnp.where(e_in == i1, neg, blk), axis=0, keepdims=True)
        gs_rows.append(m1 + m2)
    gs = jnp.concatenate(gs_rows, axis=0)
    g_io = lax.broadcasted_iota(jnp.int32, (N_GROUP, tm), 0)
    e_io = lax.broadcasted_iota(jnp.int32, (N_EXPERTS, tm), 0)
    e_grp = e_io >> int(math.log2(GROUP_SZ))
    masked = jnp.full((N_EXPERTS, tm), neg, F32)
    for _ in range(TOPK_GROUP):
        mx = jnp.max(gs, axis=0, keepdims=True)
        gi = jnp.min(jnp.where(gs == mx, g_io, N_GROUP), axis=0, keepdims=True)
        gs = jnp.where(g_io == gi, neg, gs)
        masked = jnp.where(e_grp == gi, ch, masked)
    idx_rows, w_rows = [], []
    onehot = jnp.zeros((N_EXPERTS, tm), F32)
    for _ in range(TOP_K):
        mx = jnp.max(masked, axis=0, keepdims=True)
        ei = jnp.min(jnp.where(masked == mx, e_io, N_EXPERTS), axis=0, keepdims=True)
        hit = e_io == ei
        idx_rows.append(ei)
        w_rows.append(jnp.sum(jnp.where(hit, sc, 0.0), axis=0, keepdims=True))
        onehot = jnp.where(hit, 1.0, onehot)
        masked = jnp.where(hit, neg, masked)
    wsel = jnp.concatenate(w_rows, axis=0)
    wts_ref[...] = wsel / jnp.sum(wsel, axis=0, keepdims=True) * ROUTED_SCALE
    idx_ref[...] = jnp.concatenate(idx_rows, axis=0)

    tr = lax.broadcasted_iota(jnp.int32, (tm, tm), 0)
    tc_ = lax.broadcasted_iota(jnp.int32, (tm, tm), 1)
    before = jnp.where(tr < tc_, 1.0, 0.0).astype(BF16)
    cum = jnp.dot(onehot.astype(BF16), before, preferred_element_type=F32) + carry_scr[...]
    rank_ref[...] = jnp.concatenate(
        [jnp.sum(jnp.where(e_io == ei, cum, 0.0), axis=0, keepdims=True) for ei in idx_rows],
        axis=0).astype(jnp.int32)
    carry_scr[...] = carry_scr[...] + jnp.sum(onehot, axis=1, keepdims=True)
    cnt_ref[...] = carry_scr[...]


def _router(h, w_router_t, bias_col, tm):
    t = h.shape[0]
    return pl.pallas_call(
        functools.partial(_router_kernel, tm=tm),
        grid=(t // tm,),
        in_specs=[pl.BlockSpec((tm, D_MODEL), lambda i: (i, 0)),
                  pl.BlockSpec((N_EXPERTS, D_MODEL), lambda i: (0, 0)),
                  pl.BlockSpec((N_EXPERTS, 1), lambda i: (0, 0))],
        out_specs=[pl.BlockSpec((TOP_K, tm), lambda i: (0, i)),
                   pl.BlockSpec((TOP_K, tm), lambda i: (0, i)),
                   pl.BlockSpec((TOP_K, tm), lambda i: (0, i)),
                   pl.BlockSpec((N_EXPERTS, 1), lambda i: (0, 0))],
        out_shape=[jax.ShapeDtypeStruct((TOP_K, t), jnp.int32),
                   jax.ShapeDtypeStruct((TOP_K, t), F32),
                   jax.ShapeDtypeStruct((TOP_K, t), jnp.int32),
                   jax.ShapeDtypeStruct((N_EXPERTS, 1), F32)],
        scratch_shapes=[pltpu.VMEM((N_EXPERTS, 1), F32)],
        compiler_params=_cparams(("arbitrary",)),
        name="router",
    )(h, w_router_t, bias_col)


def _dest_kernel(idx_ref, rank_ref, start_ref, dest_ref, *, tm, n_rows):
    e_io = lax.broadcasted_iota(jnp.int32, (N_EXPERTS, tm), 0)
    starts = start_ref[...]
    rows = []
    for k in range(TOP_K):
        seg = jnp.sum(jnp.where(e_io == idx_ref[k:k + 1, :], starts, 0.0), axis=0, keepdims=True)
        rows.append(seg.astype(jnp.int32) + rank_ref[k:k + 1, :])
    base = jnp.concatenate(rows, axis=0)
    for p in range(N_PIECE):
        dest_ref[p] = base + p * n_rows


def _dest(idx_t, rank_t, seg_start_col, n_rows, tm):
    t = idx_t.shape[1]
    blk = pl.BlockSpec((TOP_K, tm), lambda i: (0, i))
    return pl.pallas_call(
        functools.partial(_dest_kernel, tm=tm, n_rows=n_rows),
        grid=(t // tm,),
        in_specs=[blk, blk, pl.BlockSpec((N_EXPERTS, 1), lambda i: (0, 0))],
        out_specs=pl.BlockSpec((N_PIECE, TOP_K, tm), lambda i: (0, 0, i)),
        out_shape=jax.ShapeDtypeStruct((N_PIECE, TOP_K, t), jnp.int32),
        compiler_params=_cparams(("parallel",)),
        name="dispatch_rows",
    )(idx_t, rank_t, seg_start_col)


def _expert_kernel(be_ref, nu_ref, first_ref, nxt_ref, slot_ref, x_ref, wg_hbm, wu_hbm, wd_hbm, y_ref,
                   wg_buf, wu_buf, wd_buf, wgu_scr, wd_scr, sems):
    i = pl.program_id(0)
    live = i < nu_ref[0]

    def fetch(e, slot):
        return [pltpu.make_async_copy(wg_hbm.at[e], wg_buf.at[slot], sems.at[slot, 0]),
                pltpu.make_async_copy(wu_hbm.at[e], wu_buf.at[slot], sems.at[slot, 1]),
                pltpu.make_async_copy(wd_hbm.at[e], wd_buf.at[slot], sems.at[slot, 2])]

    @pl.when(live & (first_ref[i] == 1))
    def _new_expert():
        slot = slot_ref[i]

        @pl.when(i == 0)
        def _():
            for cp in fetch(be_ref[i], slot):
                cp.start()

        @pl.when(nxt_ref[i] >= 0)
        def _():
            for cp in fetch(nxt_ref[i], 1 - slot):
                cp.start()

        for cp in fetch(be_ref[i], slot):
            cp.wait()
        wgu_scr[:, :D_EXPERT] = wg_buf[slot].astype(BF16)
        wgu_scr[:, D_EXPERT:] = wu_buf[slot].astype(BF16)
        wd_scr[...] = wd_buf[slot].astype(BF16)

    @pl.when(live)
    def _():
        n_sub = 2
        sub = EXPERT_BM // n_sub
        xs = [jnp.concatenate([c.astype(BF16) for c in
                               _load_pieces([x_ref[p, s * sub:(s + 1) * sub] for p in range(N_PIECE)])], axis=1)
              for s in range(n_sub)]
        gus = [jnp.dot(x, wgu_scr[...], preferred_element_type=F32) for x in xs]
        acts = [(_silu(gu[:, :D_EXPERT]) * gu[:, D_EXPERT:]).astype(BF16) for gu in gus]
        ys = [jnp.dot(a, wd_scr[...], preferred_element_type=F32) for a in acts]
        for s, y in enumerate(ys):
            for p in range(N_PIECE):
                y_ref[p, s * sub:(s + 1) * sub] = _pack_halves(y[:, 2 * p * PIECE_W:2 * (p + 1) * PIECE_W])


def _experts(xs, blk_exp, n_used, blk_first, blk_next, blk_slot, w_gate, w_up, w_down):
    n_rows = xs.shape[1]
    n_blocks = n_rows // EXPERT_BM

    def row_block(i, be, nu, *_):
        return (0, jnp.minimum(i, nu[0] - 1), 0)

    grid_spec = pltpu.PrefetchScalarGridSpec(
        num_scalar_prefetch=5,
        grid=(n_blocks,),
        in_specs=[pl.BlockSpec((N_PIECE, EXPERT_BM, PIECE_W), row_block),
                  pl.BlockSpec(memory_space=pl.ANY),
                  pl.BlockSpec(memory_space=pl.ANY),
                  pl.BlockSpec(memory_space=pl.ANY)],
        out_specs=pl.BlockSpec((N_PIECE, EXPERT_BM, PIECE_W), row_block),
        scratch_shapes=[pltpu.VMEM((2, D_MODEL, D_EXPERT), F32),
                        pltpu.VMEM((2, D_MODEL, D_EXPERT), F32),
                        pltpu.VMEM((2, D_EXPERT, D_MODEL), F32),
                        pltpu.VMEM((D_MODEL, 2 * D_EXPERT), BF16),
                        pltpu.VMEM((D_EXPERT, D_MODEL), BF16),
                        pltpu.SemaphoreType.DMA((2, 3))],
    )
    return pl.pallas_call(
        _expert_kernel,
        grid_spec=grid_spec,
        out_shape=jax.ShapeDtypeStruct((N_PIECE, n_rows, PIECE_W), jnp.uint32),
        compiler_params=_cparams(("arbitrary",)),
        name="experts",
    )(blk_exp, n_used, blk_first, blk_next, blk_slot, xs, w_gate, w_up, w_down)


SC_WINDOW = 128
V7X_SC_CORES = 2
V7X_SC_SUBCORES = 16


def _sc_mesh():
    return plsc.VectorSubcoreMesh(core_axis_name="core", subcore_axis_name="subcore",
                                  num_cores=V7X_SC_CORES, num_subcores=V7X_SC_SUBCORES)


def _sc_scatter_rows(src, dest, n_rows, seg, repeat):
    d = src.shape[1]
    n_idx = dest.shape[0]
    seg_blocks = seg // SC_WINDOW
    dest2 = dest.reshape(1, n_idx)

    def src_block(i):
        return ((i // (repeat * seg_blocks)) * seg_blocks + i % seg_blocks, 0)

    @functools.partial(pl.kernel, out_type=jax.ShapeDtypeStruct((n_rows, d), src.dtype), mesh=_sc_mesh(),
                       scratch_types=[], name="sc_dispatch")
    def run(src_hbm, idx_hbm, out_hbm):
        def body(rows_vmem, idx_vmem):
            pltpu.sync_copy(rows_vmem, out_hbm.at[idx_vmem.at[0]])

        pltpu.emit_pipeline(
            body,
            grid=(n_idx // SC_WINDOW,),
            in_specs=[pl.BlockSpec((SC_WINDOW, d), src_block),
                      pl.BlockSpec((1, SC_WINDOW), lambda i: (0, i))],
            out_specs=[],
            core_axis_name=("core", "subcore"),
            dimension_semantics=(pltpu.PARALLEL,),
        )(src_hbm, idx_hbm)

    return run(src, dest2)


def _sc_gather_rows(table, idx):
    d = table.shape[1]
    n_idx = idx.shape[0]
    idx2 = idx.reshape(1, n_idx)

    @functools.partial(pl.kernel, out_type=jax.ShapeDtypeStruct((n_idx, d), table.dtype), mesh=_sc_mesh(),
                       scratch_types=[], name="sc_combine_gather")
    def run(table_hbm, idx_hbm, out_hbm):
        def body(idx_vmem, rows_vmem):
            pltpu.sync_copy(table_hbm.at[idx_vmem.at[0]], rows_vmem)

        pltpu.emit_pipeline(
            body,
            grid=(n_idx // SC_WINDOW,),
            in_specs=[pl.BlockSpec((1, SC_WINDOW), lambda i: (0, i))],
            out_specs=[pl.BlockSpec((SC_WINDOW, d), lambda i: (i, 0))],
            core_axis_name=("core", "subcore"),
            dimension_semantics=(pltpu.PARALLEL,),
        )(idx_hbm, out_hbm)

    return run(table, idx2)


def _combine_kernel(yg_ref, wts_ref, h_ref, wgu_ref, wd_ref, g_ref, b_ref, outa_ref, outb_ref, *, alpha, tiles_a):
    wts = wts_ref[...]
    acc = None
    for k in range(TOP_K):
        cols = _load_pieces([yg_ref[p, k] for p in range(N_PIECE)])
        wk = wts[:, k:k + 1]
        acc = [c * wk for c in cols] if acc is None else [a + c * wk for a, c in zip(acc, cols)]
    routed = jnp.concatenate(acc, axis=1)
    h = h_ref[...]
    gu = _dot(h, wgu_ref[...])
    shared = _dot(_silu(gu[:, :D_EXPERT]) * gu[:, D_EXPERT:], wd_ref[...])
    y = _layer_norm(alpha * h + (routed + shared), g_ref[...], b_ref[...])
    i = pl.program_id(0)

    @pl.when(i < tiles_a)
    def _():
        outa_ref[...] = y

    @pl.when(i >= tiles_a)
    def _():
        outb_ref[...] = y


def _combine(yg, wts, h, w_sh_gu, w_sh_down, ln_g, ln_b, alpha, tm, t_a):
    t = h.shape[0]
    tiles_a = t_a // tm
    row = lambda i: (i, 0)
    const2 = lambda i: (0, 0)
    return pl.pallas_call(
        functools.partial(_combine_kernel, alpha=alpha, tiles_a=tiles_a),
        grid=(t // tm,),
        in_specs=[pl.BlockSpec((N_PIECE, TOP_K, tm, PIECE_W), lambda i: (0, 0, i, 0)),
                  pl.BlockSpec((tm, TOP_K), row),
                  pl.BlockSpec((tm, D_MODEL), row),
                  pl.BlockSpec(w_sh_gu.shape, const2),
                  pl.BlockSpec(w_sh_down.shape, const2),
                  pl.BlockSpec((1, D_MODEL), const2),
                  pl.BlockSpec((1, D_MODEL), const2)],
        out_specs=[pl.BlockSpec((tm, D_MODEL), lambda i: (jnp.minimum(i, tiles_a - 1), 0)),
                   pl.BlockSpec((tm, D_MODEL), lambda i: (jnp.maximum(i - tiles_a, 0), 0))],
        out_shape=[jax.ShapeDtypeStruct((t_a, D_MODEL), F32), jax.ShapeDtypeStruct((t - t_a, D_MODEL), F32)],
        compiler_params=_cparams(("arbitrary",)),
        name="combine_ln2",
    )(yg, wts, h, w_sh_gu, w_sh_down, ln_g, ln_b)


def _layer(xp, xs, s_delta, s_conv, s_pool, w_in, conv_w, a_log, dt_bias, gamma_a, w_br_a, w_pool,
           pool_scale, w_br_b, w_out, ln1_g, ln1_b, w_router, router_bias, w_exp_gate, w_exp_up,
           w_exp_down, w_sh_gate, w_sh_up, w_sh_down, ln2_g, ln2_b, alpha):
    Bp, Lp, _ = xp.shape
    Bs, Ls, _ = xs.shape
    Tp, Ts = Bp * Lp, Bs * Ls
    T = Tp + Ts
    x_p = xp.reshape(Tp, D_MODEL)
    x_s = xs.reshape(Ts, D_MODEL)

    o_b, o_u = QKV_W + V_W, QKV_W + V_W + 2 * H_V
    w_main = jnp.concatenate([w_in[:, :o_b], w_in[:, o_u:]], axis=1).astype(BF16)
    w_ba = jnp.pad(w_in[:, o_b:o_u], ((0, 0), (0, 128 - 2 * H_V))).astype(BF16)
    proj, ba = _in_proj(x_p, x_s, w_main, w_ba, IN_PROJ_TM, IN_PROJ_TN)

    o_p, sd_p = _gdn(proj, ba, conv_w, a_log, dt_bias, gamma_a, 0, Bp, Lp, GDN_CHUNK, 1, GDN_FRESH_CHUNKS)
    o_s, sd_s = _gdn(proj, ba, conv_w, a_log, dt_bias, gamma_a, Tp, Bs, Ls, Ls, GDN_STATE_SEQS, 1,
                     conv_buf=s_conv, s0=s_delta)

    pooled_p = _pool(proj, 0, Bp, Lp, POOL_TILE, 1, 0)
    pooled_s = _pool(proj, Tp, Bs, Ls, Ls, POOL_STATE_SEQS, PAST_LEN, pool_buf=s_pool)

    h, hp = _post(pooled_p, pooled_s, o_p, o_s, proj, x_p, x_s, w_pool.astype(BF16), pool_scale.reshape(1, D_MODEL),
                  w_br_b.astype(BF16), w_br_a.astype(BF16), w_out.astype(BF16),
                  ln1_g.reshape(1, D_MODEL), ln1_b.reshape(1, D_MODEL), alpha, POST_TM)

    idx_t, wts_t, rank_t, cnt = _router(h, w_router.T, router_bias.reshape(N_EXPERTS, 1), ROUTER_TM)
    counts = cnt[:, 0].astype(jnp.int32)
    padded = ((counts + EXPERT_BM - 1) // EXPERT_BM) * EXPERT_BM
    pends = jnp.cumsum(padded)
    pstarts = pends - padded
    n_blocks = (T * TOP_K + N_EXPERTS * (EXPERT_BM - 1) + EXPERT_BM - 1) // EXPERT_BM
    n_rows = n_blocks * EXPERT_BM
    piece_rows = _dest(idx_t, rank_t, pstarts.astype(F32).reshape(N_EXPERTS, 1), n_rows, DEST_TM)
    blk_start = jnp.arange(n_blocks, dtype=jnp.int32) * EXPERT_BM
    blk_exp = jnp.minimum(jnp.sum((pends[None, :] <= blk_start[:, None]).astype(jnp.int32), axis=1),
                          N_EXPERTS - 1)
    n_used = (pends[-1:] // EXPERT_BM).astype(jnp.int32)
    has_rows = counts > 0
    e_ids = jnp.arange(N_EXPERTS, dtype=jnp.int32)
    slot_of = (jnp.cumsum(has_rows.astype(jnp.int32)) - 1) & 1
    later = jnp.where(has_rows, e_ids, N_EXPERTS)
    next_of = jnp.concatenate([lax.cummin(later, reverse=True)[1:], jnp.full((1,), N_EXPERTS, jnp.int32)])
    next_of = jnp.where(next_of < N_EXPERTS, next_of, -1)
    blk_first = (blk_start == pstarts[blk_exp]).astype(jnp.int32)
    blk_next = next_of[blk_exp]
    blk_slot = slot_of[blk_exp]

    piece_idx = piece_rows.reshape(N_PIECE * TOP_K * T)
    x_sorted = _sc_scatter_rows(hp.reshape(N_PIECE * T, PIECE_W), piece_idx, N_PIECE * n_rows, T, TOP_K)
    y_sorted = _experts(x_sorted.reshape(N_PIECE, n_rows, PIECE_W), blk_exp, n_used, blk_first, blk_next,
                        blk_slot, w_exp_gate, w_exp_up, w_exp_down)
    yg = _sc_gather_rows(y_sorted.reshape(N_PIECE * n_rows, PIECE_W), piece_idx)
    yg = yg.reshape(N_PIECE, TOP_K, T, PIECE_W)

    w_sh_gu = jnp.concatenate([w_sh_gate, w_sh_up], axis=1).astype(BF16)
    y_p, y_s = _combine(yg, wts_t.T, h, w_sh_gu, w_sh_down.astype(BF16),
                        ln2_g.reshape(1, D_MODEL), ln2_b.reshape(1, D_MODEL), alpha, COMBINE_TM, Tp)

    def carried_rows(keep, col0, width, name, old_p, old_s):
        parts, counts = [], []
        for x2d, B, L in ((x_p, Bp, Lp), (x_s, Bs, Ls)):
            n = min(L, keep)
            rows = jnp.arange(B, dtype=jnp.int32)[:, None] * L + (L - n) + jnp.arange(n, dtype=jnp.int32)
            parts.append(jnp.take(x2d, rows.reshape(-1), axis=0).astype(BF16))
            counts.append((B, n))
        n_tail = sum(B * n for B, n in counts)
        n_pad = -(-n_tail // 128) * 128
        xt = jnp.pad(jnp.concatenate(parts, axis=0), ((0, n_pad - n_tail), (0, 0)))
        tail = _matmul(xt, w_main, col0, width, n_pad, 1024, F32, name)
        outs, r0 = [], 0
        for (B, n), old in zip(counts, (old_p, old_s)):
            new = tail[r0:r0 + B * n].reshape(B, n, width)
            outs.append(new if n == keep else jnp.concatenate([old[:, n:], new], axis=1))
            r0 += B * n
        return outs

    new_conv_p, new_conv_s = carried_rows(CONV_W - 1, 0, QKV_W, "in_proj_conv_rows",
                                          jnp.zeros((Bp, CONV_W - 1, QKV_W), F32), s_conv)
    new_pool_p, new_pool_s = carried_rows(POOL_BUF, U_BLK * D_MODEL, D_MODEL, "in_proj_pool_rows",
                                          jnp.zeros((Bp, POOL_BUF, D_MODEL), F32), s_pool)
    return (y_p.reshape(Bp, Lp, D_MODEL), y_s.reshape(Bs, Ls, D_MODEL),
            sd_p, new_conv_p, new_pool_p, sd_s, new_conv_s, new_pool_s)


def kernel(x_prompt, x_sample, state_delta, state_conv, state_pool, w_in, conv_w, a_log, dt_bias, gamma_a,
           w_br_a, w_pool, pool_scale, w_br_b, w_out, ln1_g, ln1_b, w_router, router_bias,
           w_exp_gate, w_exp_up, w_exp_down, w_sh_gate, w_sh_up, w_sh_down, ln2_g, ln2_b):
    depth = w_in.shape[0]
    alpha = (2 * depth) ** 0.25
    yp, ys = x_prompt, x_sample
    outs = [[] for _ in range(6)]
    for l in range(depth):
        res = _layer(yp, ys, state_delta[l], state_conv[l], state_pool[l], w_in[l], conv_w[l], a_log[l],
                     dt_bias[l], gamma_a[l], w_br_a[l], w_pool[l], pool_scale[l], w_br_b[l], w_out[l],
                     ln1_g[l], ln1_b[l], w_router[l], router_bias[l], w_exp_gate[l], w_exp_up[l],
                     w_exp_down[l], w_sh_gate[l], w_sh_up[l], w_sh_down[l], ln2_g[l], ln2_b[l], alpha)
        yp, ys = res[0], res[1]
        for lst, v in zip(outs, res[2:]):
            lst.append(v)
    return (yp, ys) + tuple(jnp.stack(v) for v in outs)
```
